```python
import math
import jax, jax.numpy as jnp
from jax import lax
import numpy as np

D_MODEL = 1024
BATCH = 8
SEQ = 2048
DEPTH = 1

D_MIX = D_MODEL
MLA_HEADS = 8
QK_NOPE = 64
QK_ROPE = 32
V_HEAD = 64
Q_RANK = 256
KV_RANK = 128
ROPE_THETA = 10000.0
Q_BLOCK = 128
MLA_OUT = MLA_HEADS * V_HEAD
GM_GROUPS = 8
GM_CH = (D_MIX - MLA_OUT) // GM_GROUPS
GM_OUT = GM_GROUPS * GM_CH
CHUNK = 128
IN_COLS = Q_RANK + KV_RANK + QK_ROPE + 2 * GM_OUT
N_GROUPS = 4
EXP_PER_GROUP = 8
N_EXPERTS = N_GROUPS * EXP_PER_GROUP
TOP_K = 2
EXPERT_FF = 512
MOE_BLOCK = 128
PLE_DIM = 256
EPS = 1e-6
ALPHA = (2.0 * DEPTH) ** 0.25
BETA = (8.0 * DEPTH) ** -0.25

kernel_name = "hymba_mla_gmlp_hiermoe_deepnorm_ple"


def layer_norm(x, g, b):
    xf = x.astype(jnp.float32)
    mu = jnp.mean(xf, -1, keepdims=True)
    var = jnp.mean(jnp.square(xf - mu), -1, keepdims=True)
    return ((xf - mu) * lax.rsqrt(var + EPS)).astype(x.dtype) * g + b


def rms_norm(x, g):
    xf = x.astype(jnp.float32)
    return (xf * lax.rsqrt(jnp.mean(xf * xf, -1, keepdims=True) + EPS)).astype(x.dtype) * g


def rope_angles(pos):
    inv = ROPE_THETA ** (-jnp.arange(0, QK_ROPE, 2, dtype=jnp.float32) / QK_ROPE)
    ang = pos.astype(jnp.float32)[..., None] * inv
    return jnp.cos(ang), jnp.sin(ang)


def apply_rope(x, cos, sin):
    x1, x2 = jnp.split(x.astype(jnp.float32), 2, axis=-1)
    return jnp.concatenate([x1 * cos - x2 * sin, x1 * sin + x2 * cos], -1).astype(x.dtype)


def mla_mixer(c_q, c_kv, k_rope, pos, q_norm_g, w_q_up, kv_norm_g, w_kv_up):
    B, S, _ = c_q.shape
    q = (rms_norm(c_q, q_norm_g) @ w_q_up).reshape(B, S, MLA_HEADS, QK_NOPE + QK_ROPE)
    kv = (rms_norm(c_kv, kv_norm_g) @ w_kv_up).reshape(B, S, MLA_HEADS, QK_NOPE + V_HEAD)
    cos, sin = rope_angles(pos)
    q_nope = q[..., :QK_NOPE]
    q_rope = apply_rope(q[..., QK_NOPE:], cos[:, :, None], sin[:, :, None])
    k_nope, v = kv[..., :QK_NOPE], kv[..., QK_NOPE:]
    k_rope = apply_rope(k_rope, cos, sin)
    scale = (QK_NOPE + QK_ROPE) ** -0.5
    outs = []
    for i in range(S // Q_BLOCK):
        q0, q1 = i * Q_BLOCK, (i + 1) * Q_BLOCK
        s = (jnp.einsum('bqhd,bkhd->bhqk', q_nope[:, q0:q1], k_nope[:, :q1])
             + jnp.einsum('bqhr,bkr->bhqk', q_rope[:, q0:q1], k_rope[:, :q1])).astype(jnp.float32) * scale
        causal = jnp.arange(q1)[None, :] <= jnp.arange(q0, q1)[:, None]
        s = jnp.where(causal, s, -1e30)
        pr = jax.nn.softmax(s, axis=-1).astype(v.dtype)
        outs.append(jnp.einsum('bhqk,bkhd->bqhd', pr, v[:, :q1]))
    return jnp.concatenate(outs, axis=1).reshape(B, S, MLA_OUT)


def chunked_gmlp(z, ln_g, ln_b, w_s, b_s):
    B, S, _ = z.shape
    z = jax.nn.gelu(z)
    u, v = z[..., :GM_OUT], z[..., GM_OUT:]
    v = layer_norm(v.reshape(B, S, GM_GROUPS, GM_CH),
                   ln_g.reshape(GM_GROUPS, GM_CH), ln_b.reshape(GM_GROUPS, GM_CH))
    v = v.reshape(B, S // CHUNK, CHUNK, GM_GROUPS, GM_CH)
    w = w_s * jnp.tril(jnp.ones((CHUNK, CHUNK), w_s.dtype))
    v = jnp.einsum('gts,bnsgc->bntgc', w, v) + b_s.T[None, None, :, :, None]
    return u * v.reshape(B, S, GM_OUT)


def hier_moe(x, w_rg, b_rg, w_re, b_re, w_gate, w_up, w_down):
    B, S, D = x.shape
    xt = x.reshape(-1, D)
    T = xt.shape[0]
    g_prob = jax.nn.softmax((xt @ w_rg).astype(jnp.float32) + b_rg, axis=-1)
    g_p, g_idx = lax.top_k(g_prob, 1)
    e_logits = ((xt @ w_re).astype(jnp.float32) + b_re).reshape(T, N_GROUPS, EXP_PER_GROUP)
    e_logits = e_logits[jnp.arange(T), g_idx[:, 0]]
    e_p, e_loc = lax.top_k(jax.nn.softmax(e_logits, axis=-1), TOP_K)
    gate = g_p * e_p / jnp.sum(e_p, -1, keepdims=True)
    expert = g_idx * EXP_PER_GROUP + e_loc
    A = T * TOP_K
    e_flat = expert.reshape(-1)
    tok_flat = jnp.repeat(jnp.arange(T), TOP_K)
    order = jnp.argsort(e_flat)
    e_sorted, tok_sorted = e_flat[order], tok_flat[order]
    gate_sorted = gate.reshape(-1)[order]
    counts = jnp.bincount(e_flat, length=N_EXPERTS)
    starts = jnp.cumsum(counts) - counts
    padded = (counts + MOE_BLOCK - 1) // MOE_BLOCK * MOE_BLOCK
    pad_ends = jnp.cumsum(padded)
    pad_starts = pad_ends - padded
    dest = pad_starts[e_sorted] + jnp.arange(A) - starts[e_sorted]
    NB = (A + MOE_BLOCK - 1) // MOE_BLOCK + N_EXPERTS
    P = NB * MOE_BLOCK
    buf = jnp.zeros((P, D), x.dtype).at[dest].set(xt[tok_sorted])
    block_expert = jnp.minimum(
        jnp.searchsorted(pad_ends, jnp.arange(NB) * MOE_BLOCK, side='right'), N_EXPERTS - 1)

    def expert_block(args):
        xb, e = args
        h = jax.nn.silu(xb @ w_gate[e]) * (xb @ w_up[e])
        return h @ w_down[e]

    y = lax.map(expert_block, (buf.reshape(NB, MOE_BLOCK, D), block_expert)).reshape(P, D)
    y = y[dest] * gate_sorted[:, None].astype(x.dtype)
    out = jnp.zeros((T, D), x.dtype).at[tok_sorted].add(y)
    return out.reshape(B, S, D)


def setup_inputs(seed: int = 0) -> dict:
    key = jax.random.key(seed)
    ks = iter(jax.random.split(key, 40))
    f32 = jnp.float32

    def nrm(shape, scale):
        return jax.random.normal(next(ks), shape, f32) * scale

    def gain(shape):
        return 1.0 + 0.05 * jax.random.normal(next(ks), shape, f32)

    L = DEPTH
    x = jax.random.normal(next(ks), (BATCH, SEQ, D_MODEL), f32)
    p = jax.random.normal(next(ks), (DEPTH, BATCH, SEQ, PLE_DIM), f32)
    positions = (jnp.arange(SEQ, dtype=jnp.int32)[None, :]
                 + jax.random.randint(next(ks), (BATCH, 1), 0, 1024, dtype=jnp.int32))
    return {
        "x": x,
        "p": p,
        "positions": positions,
        "w_in": nrm((L, D_MODEL, IN_COLS), D_MODEL ** -0.5),
        "q_norm_g": gain((L, Q_RANK)),
        "w_q_up": nrm((L, Q_RANK, MLA_HEADS * (QK_NOPE + QK_ROPE)), Q_RANK ** -0.5),
        "kv_norm_g": gain((L, KV_RANK)),
        "w_kv_up": nrm((L, KV_RANK, MLA_HEADS * (QK_NOPE + V_HEAD)), KV_RANK ** -0.5),
        "gm_ln_g": gain((L, GM_OUT)),
        "gm_ln_b": nrm((L, GM_OUT), 0.01),
        "gm_w_s": nrm((L, GM_GROUPS, CHUNK, CHUNK), CHUNK ** -0.5),
        "gm_b_s": gain((L, GM_GROUPS, CHUNK)),
        "mla_out_g": gain((L, MLA_OUT)),
        "gm_out_g": gain((L, GM_OUT)),
        "w_o": nrm((L, D_MIX, D_MODEL), BETA * D_MIX ** -0.5),
        "ln1_g": gain((L, D_MODEL)),
        "ln1_b": nrm((L, D_MODEL), 0.01),
        "w_rg": nrm((L, D_MODEL, N_GROUPS), D_MODEL ** -0.5),
        "b_rg": nrm((L, N_GROUPS), 0.01),
        "w_re": nrm((L, D_MODEL, N_EXPERTS), D_MODEL ** -0.5),
        "b_re": nrm((L, N_EXPERTS), 0.01),
        "w_gate": nrm((L, N_EXPERTS, D_MODEL, EXPERT_FF), D_MODEL ** -0.5),
        "w_up": nrm((L, N_EXPERTS, D_MODEL, EXPERT_FF), D_MODEL ** -0.5),
        "w_down": nrm((L, N_EXPERTS, EXPERT_FF, D_MODEL), BETA * EXPERT_FF ** -0.5),
        "ln2_g": gain((L, D_MODEL)),
        "ln2_b": nrm((L, D_MODEL), 0.01),
        "w_pg": nrm((L, D_MODEL, D_MODEL), D_MODEL ** -0.5),
        "b_pg": nrm((L, D_MODEL), 0.01),
        "w_pp": nrm((L, PLE_DIM, D_MODEL), BETA * PLE_DIM ** -0.5),
        "ln3_g": gain((L, D_MODEL)),
        "ln3_b": nrm((L, D_MODEL), 0.01),
    }


def reference(x, p, positions, w_in, q_norm_g, w_q_up, kv_norm_g, w_kv_up, gm_ln_g, gm_ln_b,
              gm_w_s, gm_b_s, mla_out_g, gm_out_g, w_o, ln1_g, ln1_b, w_rg, b_rg, w_re, b_re,
              w_gate, w_up, w_down, ln2_g, ln2_b, w_pg, b_pg, w_pp, ln3_g, ln3_b):
    c1 = Q_RANK
    c2 = c1 + KV_RANK
    c3 = c2 + QK_ROPE
    for i in range(DEPTH):
        h = x @ w_in[i]
        a = mla_mixer(h[..., :c1], h[..., c1:c2], h[..., c2:c3], positions,
                      q_norm_g[i], w_q_up[i], kv_norm_g[i], w_kv_up[i])
        g = chunked_gmlp(h[..., c3:], gm_ln_g[i], gm_ln_b[i], gm_w_s[i], gm_b_s[i])
        mix = jnp.concatenate([rms_norm(a, mla_out_g[i]), rms_norm(g, gm_out_g[i])], axis=-1) @ w_o[i]
        x = layer_norm(ALPHA * x + mix, ln1_g[i], ln1_b[i])
        moe = hier_moe(x, w_rg[i], b_rg[i], w_re[i], b_re[i], w_gate[i], w_up[i], w_down[i])
        x = layer_norm(ALPHA * x + moe, ln2_g[i], ln2_b[i])
        ple = jax.nn.sigmoid(x @ w_pg[i] + b_pg[i]) * (p[i] @ w_pp[i])
        x = layer_norm(ALPHA * x + ple, ln3_g[i], ln3_b[i])
    return x
```

```python
import functools

import jax
import jax.numpy as jnp
from jax import lax
from jax.experimental import pallas as pl
from jax.experimental.pallas import tpu as pltpu

F32 = jnp.float32
BF16 = jnp.bfloat16

MLA_HEADS = 8
QK_NOPE = 64
QK_ROPE = 32
V_HEAD = 64
Q_RANK = 256
KV_RANK = 128
ROPE_THETA = 10000.0
MLA_OUT = MLA_HEADS * V_HEAD
GM_GROUPS = 8
GM_CH = 64
GM_OUT = GM_GROUPS * GM_CH
CHUNK = 128
N_GROUPS = 4
EXP_PER_GROUP = 8
N_EXPERTS = N_GROUPS * EXP_PER_GROUP
TOP_K = 2
EPS = 1e-6
DEPTH = 1
ALPHA = (2.0 * DEPTH) ** 0.25
SM_SCALE = (QK_NOPE + QK_ROPE) ** -0.5

LANES = 128
VMEM_LIMIT = 56 * 1024 * 1024

PREP_ROWS = 256
ATTN_ROWS = 256
ROUTE_ROWS = 512
MOVE_ROWS = 256
EXPERT_ROWS = 256

C_Q = 0
C_KV = C_Q + Q_RANK
C_KRA = C_KV + KV_RANK
C_KRB = C_KRA + LANES
C_U = C_KRB + LANES
C_V = C_U + GM_OUT
C_END = C_V + GM_OUT
HP = MLA_HEADS * LANES

I_E0, I_E1, I_R0, I_R1, I_G0, I_G1 = range(6)
R_OFF = N_GROUPS


def _rms(v, g):
    return v * lax.rsqrt(jnp.mean(v * v, axis=-1, keepdims=True) + EPS) * g


def _ln(v, g, b):
    mu = jnp.mean(v, axis=-1, keepdims=True)
    d = v - mu
    var = jnp.mean(d * d, axis=-1, keepdims=True)
    return d * lax.rsqrt(var + EPS) * g + b


def _dot(a, b):
    return jnp.dot(a, b, preferred_element_type=F32)


def _prep_kernel(x_ref, pos_ref, win_ref, qg_ref, wq_ref, kvg_ref, wk_ref, wv_ref, inv_ref, sgn_ref,
                 lng_ref, lnb_ref, gavg_ref, ws_ref, bias_ref, gog_ref,
                 q_ref, k_ref, v_ref, g_ref):
    rows = x_ref.shape[1]
    h = _dot(x_ref[0].astype(BF16), win_ref[...])

    ang = pos_ref[0].astype(F32) * inv_ref[...]
    cos_t = jnp.cos(ang)
    sin_t = jnp.sin(ang) * sgn_ref[...]

    cq = _rms(h[:, C_Q:C_Q + Q_RANK], qg_ref[...]).astype(BF16)
    q2 = _dot(cq, wq_ref[...])
    for hd in range(MLA_HEADS):
        lo = hd * LANES
        qh = q2[:, lo:lo + LANES] * cos_t + q2[:, HP + lo:HP + lo + LANES] * sin_t
        q_ref[0, :, lo:lo + LANES] = (qh * SM_SCALE).astype(BF16)

    ckv = _rms(h[:, C_KV:C_KV + KV_RANK], kvg_ref[...]).astype(BF16)
    kp = _dot(ckv, wk_ref[...])
    kr = h[:, C_KRA:C_KRA + LANES] * cos_t + h[:, C_KRB:C_KRB + LANES] * sin_t
    for hd in range(MLA_HEADS):
        lo = hd * LANES
        k_ref[0, :, lo:lo + LANES] = (kp[:, lo:lo + LANES] + kr).astype(BF16)
    v_ref[0] = _dot(ckv, wv_ref[...]).astype(BF16)

    u = jax.nn.gelu(h[:, C_U:C_U + GM_OUT])
    vv = jax.nn.gelu(h[:, C_V:C_V + GM_OUT])
    mu = _dot(vv.astype(BF16), gavg_ref[...])
    d = vv - mu
    var = _dot((d * d).astype(BF16), gavg_ref[...])
    vn = (d * lax.rsqrt(var + EPS) * lng_ref[...] + lnb_ref[...]).astype(BF16)

    tri = lax.broadcasted_iota(jnp.int32, (CHUNK, CHUNK), 0) >= lax.broadcasted_iota(jnp.int32, (CHUNK, CHUNK), 1)
    wm = [jnp.where(tri, ws_ref[g], 0.0).astype(BF16) for g in range(GM_GROUPS)]
    low_half = lax.broadcasted_iota(jnp.int32, (CHUNK, LANES), 1) < GM_CH
    for c in range(rows // CHUNK):
        r0 = c * CHUNK
        parts = []
        for pr in range(GM_GROUPS // 2):
            tile = vn[r0:r0 + CHUNK, pr * LANES:(pr + 1) * LANES]
            parts.append(jnp.where(low_half, _dot(wm[2 * pr], tile), _dot(wm[2 * pr + 1], tile)))
        sg = jnp.concatenate(parts, axis=1) + bias_ref[...]
        gm = u[r0:r0 + CHUNK] * sg
        g_ref[0, r0:r0 + CHUNK, :] = _rms(gm, gog_ref[...]).astype(BF16)


def _prep(x, pos3, w):
    B, S, D = x.shape
    ts = PREP_ROWS
    full = lambda a: pl.BlockSpec(a.shape, lambda b, i: (0,) * a.ndim)
    consts = [w["win"], w["qg"], w["wq"], w["kvg"], w["wk"], w["wv"], w["inv"], w["sgn"],
              w["lng"], w["lnb"], w["gavg"], w["ws"], w["bias"], w["gog"]]
    return pl.pallas_call(
        _prep_kernel,
        grid=(B, S // ts),
        in_specs=[pl.BlockSpec((1, ts, D), lambda b, i: (b, i, 0)),
                  pl.BlockSpec((1, ts, 1), lambda b, i: (b, i, 0))] + [full(a) for a in consts],
        out_specs=[pl.BlockSpec((1, ts, HP), lambda b, i: (b, i, 0)),
                   pl.BlockSpec((1, ts, HP), lambda b, i: (b, i, 0)),
                   pl.BlockSpec((1, ts, HP), lambda b, i: (b, i, 0)),
                   pl.BlockSpec((1, ts, GM_OUT), lambda b, i: (b, i, 0))],
        out_shape=[jax.ShapeDtypeStruct((B, S, HP), BF16)] * 3 + [jax.ShapeDtypeStruct((B, S, GM_OUT), BF16)],
        compiler_params=pltpu.CompilerParams(dimension_semantics=("parallel", "parallel"),
                                             vmem_limit_bytes=VMEM_LIMIT),
        name="prep",
    )(x, pos3, *consts)


def _attn_kernel(q_ref, k_ref, v_ref, g_ref, x_ref, woa_ref, wog_ref, mog_ref, l1g_ref, l1b_ref,
                 o_ref, a_scr):
    i = pl.program_id(1)
    tq = q_ref.shape[1]
    tk = tq
    row = lax.broadcasted_iota(jnp.int32, (tq, tk), 0)
    col = lax.broadcasted_iota(jnp.int32, (tq, tk), 1)
    diag_mask = col <= row

    for hd in range(MLA_HEADS):
        lo = hd * LANES
        qh = q_ref[0, :, lo:lo + LANES]

        def step(j, carry, masked):
            m, l, acc = carry
            k0 = pl.multiple_of(j * tk, tk)
            kj = k_ref[0, pl.ds(k0, tk), lo:lo + LANES]
            vj = v_ref[0, pl.ds(k0, tk), lo:lo + LANES]
            s = lax.dot_general(qh, kj, (((1,), (1,)), ((), ())), preferred_element_type=F32)
            if masked:
                s = jnp.where(diag_mask, s, -1e30)
            m_new = jnp.maximum(m, jnp.max(s, axis=-1, keepdims=True))
            p = jnp.exp(s - m_new)
            scale = jnp.exp(m - m_new)
            l_new = scale * l + jnp.sum(p, axis=-1, keepdims=True)
            acc_new = scale * acc + _dot(p.astype(BF16), vj)
            return m_new, l_new, acc_new

        init = (jnp.full((tq, 1), -1e30, F32), jnp.zeros((tq, 1), F32), jnp.zeros((tq, LANES), F32))
        carry = lax.fori_loop(0, i, functools.partial(step, masked=False), init)
        m, l, acc = step(i, carry, True)
        a_scr[:, lo:lo + LANES] = acc / l

    a = _rms_padded(a_scr[...], mog_ref[...])
    mix = _dot(a.astype(BF16), woa_ref[...]) + _dot(g_ref[0], wog_ref[...])
    o_ref[0] = _ln(ALPHA * x_ref[0] + mix, l1g_ref[...], l1b_ref[...])


def _rms_padded(v, g):
    ms = jnp.sum(v * v, axis=-1, keepdims=True) * (1.0 / MLA_OUT)
    return v * lax.rsqrt(ms + EPS) * g


def _attn(q, k, v, g, x, w):
    B, S, D = x.shape
    tq = ATTN_ROWS
    full = lambda a: pl.BlockSpec(a.shape, lambda b, i: (0,) * a.ndim)
    consts = [w["woa"], w["wog"], w["mog"], w["l1g"], w["l1b"]]
    return pl.pallas_call(
        _attn_kernel,
        grid=(B, S // tq),
        in_specs=[pl.BlockSpec((1, tq, HP), lambda b, i: (b, i, 0)),
                  pl.BlockSpec((1, S, HP), lambda b, i: (b, 0, 0)),
                  pl.BlockSpec((1, S, HP), lambda b, i: (b, 0, 0)),
                  pl.BlockSpec((1, tq, GM_OUT), lambda b, i: (b, i, 0)),
                  pl.BlockSpec((1, tq, D), lambda b, i: (b, i, 0))] + [full(a) for a in consts],
        out_specs=pl.BlockSpec((1, tq, D), lambda b, i: (b, i, 0)),
        out_shape=jax.ShapeDtypeStruct((B, S, D), F32),
        scratch_shapes=[pltpu.VMEM((tq, HP), F32)],
        compiler_params=pltpu.CompilerParams(dimension_semantics=("parallel", "parallel"),
                                             vmem_limit_bytes=VMEM_LIMIT),
        name="attn",
    )(q, k, v, g, x, *consts)


def _route_kernel(x_ref, wr_ref, br_ref, info_ref, cnt_ref, carry_scr, tri_scr):
    step = pl.program_id(0)
    tt = x_ref.shape[0]

    @pl.when(step == 0)
    def _():
        carry_scr[...] = jnp.zeros_like(carry_scr)
        r = lax.broadcasted_iota(jnp.int32, (tt, tt), 0)
        c = lax.broadcasted_iota(jnp.int32, (tt, tt), 1)
        tri_scr[...] = jnp.where(c < r, 1.0, 0.0).astype(BF16)

    x = x_ref[...]
    xh = x.astype(BF16)
    xl = (x - xh.astype(F32)).astype(BF16)
    wr = wr_ref[...]
    wh = wr.astype(BF16)
    wl = (wr - wh.astype(F32)).astype(BF16)
    logits = _dot(xh, wh) + _dot(xl, wh) + _dot(xh, wl) + br_ref[...]

    lane = lax.broadcasted_iota(jnp.int32, (tt, LANES), 1)
    neg = jnp.float32(-jnp.inf)

    is_g = lane < N_GROUPS
    lg = jnp.where(is_g, logits, neg)
    gmax = jnp.max(lg, axis=-1, keepdims=True)
    g_idx = jnp.min(jnp.where(lg == gmax, lane, LANES), axis=-1, keepdims=True)
    g_den = jnp.sum(jnp.where(is_g, jnp.exp(lg - gmax), 0.0), axis=-1, keepdims=True)
    g_p = 1.0 / g_den

    in_grp = (lane >= R_OFF) & (lane < R_OFF + N_EXPERTS) & (((lane - R_OFF) >> 3) == g_idx)
    le = jnp.where(in_grp, logits, neg)
    m1 = jnp.max(le, axis=-1, keepdims=True)
    i1 = jnp.min(jnp.where(le == m1, lane, LANES), axis=-1, keepdims=True)
    le2 = jnp.where(lane == i1, neg, le)
    m2 = jnp.max(le2, axis=-1, keepdims=True)
    i2 = jnp.min(jnp.where(le2 == m2, lane, LANES), axis=-1, keepdims=True)
    e2 = jnp.exp(m2 - m1)
    gate0 = g_p / (1.0 + e2)
    gate1 = g_p * e2 / (1.0 + e2)

    hit1 = lane == i1
    hit2 = lane == i2
    onehot = jnp.where(hit1 | hit2, 1.0, 0.0)
    before = _dot(tri_scr[...], onehot.astype(BF16)) + carry_scr[...]
    r0 = jnp.sum(jnp.where(hit1, before, 0.0), axis=-1, keepdims=True)
    r1 = jnp.sum(jnp.where(hit2, before, 0.0), axis=-1, keepdims=True)
    carry_scr[...] = carry_scr[...] + jnp.sum(onehot, axis=0, keepdims=True)
    cnt_ref[...] = carry_scr[...]

    info = jnp.where(lane == I_E0, (i1 - R_OFF).astype(F32), 0.0)
    info = jnp.where(lane == I_E1, (i2 - R_OFF).astype(F32), info)
    info = jnp.where(lane == I_R0, r0, info)
    info = jnp.where(lane == I_R1, r1, info)
    info = jnp.where(lane == I_G0, gate0, info)
    info = jnp.where(lane == I_G1, gate1, info)
    info_ref[...] = info


def _route(x1, wr, br):
    T, D = x1.shape
    tt = ROUTE_ROWS
    return pl.pallas_call(
        _route_kernel,
        grid=(T // tt,),
        in_specs=[pl.BlockSpec((tt, D), lambda i: (i, 0)),
                  pl.BlockSpec(wr.shape, lambda i: (0, 0)),
                  pl.BlockSpec(br.shape, lambda i: (0, 0))],
        out_specs=[pl.BlockSpec((tt, LANES), lambda i: (i, 0)),
                   pl.BlockSpec((1, LANES), lambda i: (0, 0))],
        out_shape=[jax.ShapeDtypeStruct((T, LANES), F32), jax.ShapeDtypeStruct((1, LANES), F32)],
        scratch_shapes=[pltpu.VMEM((1, LANES), F32), pltpu.VMEM((tt, tt), BF16)],
        compiler_params=pltpu.CompilerParams(dimension_semantics=("arbitrary",), vmem_limit_bytes=VMEM_LIMIT),
        name="route",
    )(x1, wr, br)


def _row_copy(src_ref, src_row, dst_ref, dst_row, sem):
    return pltpu.make_async_copy(src_ref.at[pl.ds(src_row, 1)], dst_ref.at[pl.ds(dst_row, 1)], sem)


def _dispatch_kernel(dest_ref, x_ref, zero_ref, buf_ref, sem):
    del zero_ref
    rows = x_ref.shape[0]

    def start(r, c):
        for kk in range(TOP_K):
            _row_copy(x_ref, r, buf_ref, dest_ref[0, 0, TOP_K * r + kk], sem).start()
        return c

    def wait(r, c):
        for kk in range(TOP_K):
            _row_copy(x_ref, r, buf_ref, dest_ref[0, 0, TOP_K * r + kk], sem).wait()
        return c

    lax.fori_loop(0, rows, start, 0)
    lax.fori_loop(0, rows, wait, 0)


def _dispatch(dest3, x1, n_rows):
    T, D = x1.shape
    td = MOVE_ROWS
    zeros = jnp.zeros((n_rows, D), F32)
    return pl.pallas_call(
        _dispatch_kernel,
        grid=(T // td,),
        in_specs=[pl.BlockSpec((1, 1, TOP_K * td), lambda i: (i, 0, 0), memory_space=pltpu.SMEM),
                  pl.BlockSpec((td, D), lambda i: (i, 0)),
                  pl.BlockSpec(memory_space=pl.ANY)],
        out_specs=pl.BlockSpec(memory_space=pl.ANY),
        out_shape=jax.ShapeDtypeStruct((n_rows, D), F32),
        scratch_shapes=[pltpu.SemaphoreType.DMA(())],
        input_output_aliases={2: 0},
        compiler_params=pltpu.CompilerParams(dimension_semantics=("arbitrary",), vmem_limit_bytes=VMEM_LIMIT),
        name="dispatch",
    )(dest3, x1, zeros)


def _expert_kernel(be_ref, buf_ref, wg_ref, wu_ref, wd_ref, y_ref, wg_scr, wu_scr, wd_scr):
    b = pl.program_id(0)
    e = be_ref[b]
    prev = be_ref[jnp.maximum(b - 1, 0)]

    @pl.when((b == 0) | (e != prev))
    def _():
        wg_scr[...] = wg_ref[0].astype(BF16)
        wu_scr[...] = wu_ref[0].astype(BF16)
        wd_scr[...] = wd_ref[0].astype(BF16)

    xb = buf_ref[...].astype(BF16)
    hidden = jax.nn.silu(_dot(xb, wg_scr[...])) * _dot(xb, wu_scr[...])
    y_ref[...] = _dot(hidden.astype(BF16), wd_scr[...])


def _experts(block_expert, buf, w_gate, w_up, w_down):
    n_rows, D = buf.shape
    bm = EXPERT_ROWS
    ff = w_gate.shape[-1]
    grid_spec = pltpu.PrefetchScalarGridSpec(
        num_scalar_prefetch=1,
        grid=(n_rows // bm,),
        in_specs=[pl.BlockSpec((bm, D), lambda b, be: (b, 0)),
                  pl.BlockSpec((1, D, ff), lambda b, be: (be[b], 0, 0)),
                  pl.BlockSpec((1, D, ff), lambda b, be: (be[b], 0, 0)),
                  pl.BlockSpec((1, ff, D), lambda b, be: (be[b], 0, 0))],
        out_specs=pl.BlockSpec((bm, D), lambda b, be: (b, 0)),
        scratch_shapes=[pltpu.VMEM((D, ff), BF16), pltpu.VMEM((D, ff), BF16), pltpu.VMEM((ff, D), BF16)],
    )
    return pl.pallas_call(
        _expert_kernel,
        grid_spec=grid_spec,
        out_shape=jax.ShapeDtypeStruct((n_rows, D), F32),
        compiler_params=pltpu.CompilerParams(dimension_semantics=("arbitrary",), vmem_limit_bytes=VMEM_LIMIT),
        name="experts",
    )(block_expert, buf, w_gate, w_up, w_down)


def _final_kernel(dest_ref, x_ref, info_ref, y_ref, p_ref, wpg_ref, bpg_ref, wpp_ref,
                  l2g_ref, l2b_ref, l3g_ref, l3b_ref, o_ref, rows_scr, sem):
    rows = x_ref.shape[0]

    def copy(r, kk):
        return _row_copy(y_ref, dest_ref[0, 0, TOP_K * r + kk], rows_scr.at[kk], r, sem)

    def start(r, c):
        for kk in range(TOP_K):
            copy(r, kk).start()
        return c

    def wait(r, c):
        for kk in range(TOP_K):
            copy(r, kk).wait()
        return c

    lax.fori_loop(0, rows, start, 0)
    pp = _dot(p_ref[...].astype(BF16), wpp_ref[...])
    lax.fori_loop(0, rows, wait, 0)

    info = info_ref[...]
    gate0 = info[:, I_G0:I_G0 + 1]
    gate1 = info[:, I_G1:I_G1 + 1]
    moe = rows_scr[0] * gate0 + rows_scr[1] * gate1
    x2 = _ln(ALPHA * x_ref[...] + moe, l2g_ref[...], l2b_ref[...])
    gate = jax.nn.sigmoid(_dot(x2.astype(BF16), wpg_ref[...]) + bpg_ref[...])
    o_ref[...] = _ln(ALPHA * x2 + gate * pp, l3g_ref[...], l3b_ref[...])


def _final(dest3, x1, info, y, p2, w):
    T, D = x1.shape
    tc = MOVE_ROWS
    pd = p2.shape[1]
    full = lambda a: pl.BlockSpec(a.shape, lambda i: (0,) * a.ndim)
    consts = [w["wpg"], w["bpg"], w["wpp"], w["l2g"], w["l2b"], w["l3g"], w["l3b"]]
    return pl.pallas_call(
        _final_kernel,
        grid=(T // tc,),
        in_specs=[pl.BlockSpec((1, 1, TOP_K * tc), lambda i: (i, 0, 0), memory_space=pltpu.SMEM),
                  pl.BlockSpec((tc, D), lambda i: (i, 0)),
                  pl.BlockSpec((tc, LANES), lambda i: (i, 0)),
                  pl.BlockSpec(memory_space=pl.ANY),
                  pl.BlockSpec((tc, pd), lambda i: (i, 0))] + [full(a) for a in consts],
        out_specs=pl.BlockSpec((tc, D), lambda i: (i, 0)),
        out_shape=jax.ShapeDtypeStruct((T, D), F32),
        scratch_shapes=[pltpu.VMEM((TOP_K, tc, D), F32), pltpu.SemaphoreType.DMA(())],
        compiler_params=pltpu.CompilerParams(dimension_semantics=("arbitrary",), vmem_limit_bytes=VMEM_LIMIT),
        name="final",
    )(dest3, x1, info, y, p2, *consts)


def _pad_heads(a, width):
    lead = a.shape[:-1]
    a = a.reshape(lead + (MLA_HEADS, width))
    a = jnp.pad(a, [(0, 0)] * len(lead) + [(0, 0), (0, LANES - width)])
    return a.reshape(lead + (HP,))


def _layer_weights(w_in, q_norm_g, w_q_up, kv_norm_g, w_kv_up, gm_ln_g, gm_ln_b, gm_w_s, gm_b_s,
                   mla_out_g, gm_out_g, w_o, ln1_g, ln1_b):
    D = w_in.shape[0]
    half = QK_ROPE // 2
    c1, c2, c3 = Q_RANK, Q_RANK + KV_RANK, Q_RANK + KV_RANK + QK_ROPE
    wkr = w_in[:, c2:c3]
    zeros = lambda *s: jnp.zeros(s, F32)
    kra = jnp.concatenate([zeros(D, QK_NOPE), wkr, zeros(D, LANES - QK_NOPE - QK_ROPE)], axis=1)
    krb = jnp.concatenate([zeros(D, QK_NOPE), wkr[:, half:], wkr[:, :half], zeros(D, LANES - QK_NOPE - QK_ROPE)], axis=1)
    win = jnp.concatenate([w_in[:, :c2], kra, krb, w_in[:, c3:]], axis=1).astype(BF16)

    wq3 = w_q_up.reshape(Q_RANK, MLA_HEADS, QK_NOPE + QK_ROPE)
    rope = wq3[..., QK_NOPE:]
    rope_sw = jnp.concatenate([rope[..., half:], rope[..., :half]], axis=-1)
    sw3 = jnp.concatenate([jnp.zeros_like(wq3[..., :QK_NOPE]), rope_sw], axis=-1)
    wq = jnp.concatenate([_pad_heads(w_q_up, QK_NOPE + QK_ROPE),
                          _pad_heads(sw3.reshape(Q_RANK, -1), QK_NOPE + QK_ROPE)], axis=1).astype(BF16)

    wkv3 = w_kv_up.reshape(KV_RANK, MLA_HEADS, QK_NOPE + V_HEAD)
    wk = _pad_heads(wkv3[..., :QK_NOPE].reshape(KV_RANK, -1), QK_NOPE).astype(BF16)
    wv = _pad_heads(wkv3[..., QK_NOPE:].reshape(KV_RANK, -1), V_HEAD).astype(BF16)

    inv = ROPE_THETA ** (-jnp.arange(0, QK_ROPE, 2, dtype=F32) / QK_ROPE)
    pad_l, pad_r = jnp.zeros((QK_NOPE,), F32), jnp.zeros((LANES - QK_NOPE - QK_ROPE,), F32)
    inv_t = jnp.concatenate([pad_l, inv, inv, pad_r])[None, :]
    sgn_t = jnp.concatenate([pad_l, -jnp.ones((half,), F32), jnp.ones((half,), F32), pad_r])[None, :]

    grp = jnp.arange(GM_OUT) // GM_CH
    gavg = jnp.where(grp[:, None] == grp[None, :], 1.0 / GM_CH, 0.0).astype(BF16)
    bias = jnp.repeat(gm_b_s.T, GM_CH, axis=1)

    woa = _pad_heads(w_o[:MLA_OUT].T, V_HEAD).T.astype(BF16)
    wog = w_o[MLA_OUT:].astype(BF16)
    return dict(win=win, qg=q_norm_g[None, :], wq=wq, kvg=kv_norm_g[None, :], wk=wk, wv=wv, inv=inv_t, sgn=sgn_t,
                lng=gm_ln_g[None, :], lnb=gm_ln_b[None, :], gavg=gavg, ws=gm_w_s, bias=bias, gog=gm_out_g[None, :],
                woa=woa, wog=wog, mog=_pad_heads(mla_out_g, V_HEAD)[None, :], l1g=ln1_g[None, :], l1b=ln1_b[None, :])


def _moe(x1, w_rg, b_rg, w_re, b_re, w_gate, w_up, w_down):
    T, D = x1.shape
    pad = jnp.zeros((D, LANES - N_GROUPS - N_EXPERTS), F32)
    wr = jnp.concatenate([w_rg, w_re, pad], axis=1)
    br = jnp.concatenate([b_rg, b_re, pad[0]])[None, :]
    info, cnt = _route(x1, wr, br)

    bm = EXPERT_ROWS
    n_blocks = (T * TOP_K) // bm + N_EXPERTS
    counts = cnt[0, R_OFF:R_OFF + N_EXPERTS].astype(jnp.int32)
    padded = (counts + bm - 1) // bm * bm
    pad_ends = jnp.cumsum(padded)
    pad_starts = pad_ends - padded
    e_idx = info[:, I_E0:I_E1 + 1].astype(jnp.int32)
    rank = info[:, I_R0:I_R1 + 1].astype(jnp.int32)
    dest = (pad_starts[e_idx] + rank).reshape(T // MOVE_ROWS, 1, TOP_K * MOVE_ROWS)
    block_expert = jnp.minimum(jnp.searchsorted(pad_ends, jnp.arange(n_blocks) * bm, side="right"),
                               N_EXPERTS - 1).astype(jnp.int32)

    buf = _dispatch(dest, x1, n_blocks * bm)
    y = _experts(block_expert, buf, w_gate, w_up, w_down)
    return info, dest, y


def kernel(x, p, positions, w_in, q_norm_g, w_q_up, kv_norm_g, w_kv_up, gm_ln_g, gm_ln_b, gm_w_s, gm_b_s, mla_out_g, gm_out_g, w_o, ln1_g, ln1_b, w_rg, b_rg, w_re, b_re, w_gate, w_up, w_down, ln2_g, ln2_b, w_pg, b_pg, w_pp, ln3_g, ln3_b):
    B, S, D = x.shape
    T = B * S
    assert S % ATTN_ROWS == 0 and S % PREP_ROWS == 0 and PREP_ROWS % CHUNK == 0
    assert T % ROUTE_ROWS == 0 and T % MOVE_ROWS == 0 and (T * TOP_K) % EXPERT_ROWS == 0
    pos3 = positions.reshape(B, S, 1)
    for i in range(DEPTH):
        w = _layer_weights(w_in[i], q_norm_g[i], w_q_up[i], kv_norm_g[i], w_kv_up[i], gm_ln_g[i], gm_ln_b[i],
                           gm_w_s[i], gm_b_s[i], mla_out_g[i], gm_out_g[i], w_o[i], ln1_g[i], ln1_b[i])
        q, k, v, g = _prep(x, pos3, w)
        x1 = _attn(q, k, v, g, x, w).reshape(T, D)
        info, dest, y = _moe(x1, w_rg[i], b_rg[i], w_re[i], b_re[i], w_gate[i], w_up[i], w_down[i])
        wf = dict(wpg=w_pg[i].astype(BF16), bpg=b_pg[i][None, :], wpp=w_pp[i].astype(BF16),
                  l2g=ln2_g[i][None, :], l2b=ln2_b[i][None, :], l3g=ln3_g[i][None, :], l3b=ln3_b[i][None, :])
        x = _final(dest, x1, info, y, p[i].reshape(T, -1), wf).reshape(B, S, D)
    return x
```

```python
import functools

import jax
import jax.numpy as jnp
from jax import lax
from jax.experimental import pallas as pl
from jax.experimental.pallas import tpu as pltpu

F32 = jnp.float32
BF16 = jnp.bfloat16

MLA_HEADS = 8
QK_NOPE = 64
QK_ROPE = 32
V_HEAD = 64
Q_RANK = 256
KV_RANK = 128
ROPE_THETA = 10000.0
MLA_OUT = MLA_HEADS * V_HEAD
GM_GROUPS = 8
GM_CH = 64
GM_OUT = GM_GROUPS * GM_CH
CHUNK = 128
N_GROUPS = 4
EXP_PER_GROUP = 8
N_EXPERTS = N_GROUPS * EXP_PER_GROUP
TOP_K = 2
EPS = 1e-6
DEPTH = 1
ALPHA = (2.0 * DEPTH) ** 0.25
SM_SCALE = (QK_NOPE + QK_ROPE) ** -0.5

LANES = 128
VMEM_LIMIT = 56 * 1024 * 1024

PREP_ROWS = 256
ATTN_ROWS = 256
ROUTE_ROWS = 512
MOVE_ROWS = 256
EXPERT_ROWS = 256

C_Q = 0
C_KV = C_Q + Q_RANK
C_KRA = C_KV + KV_RANK
C_KRB = C_KRA + LANES
C_U = C_KRB + LANES
C_V = C_U + GM_OUT
C_END = C_V + GM_OUT
HP = MLA_HEADS * LANES

I_E0, I_E1, I_R0, I_R1, I_G0, I_G1 = range(6)
R_OFF = N_GROUPS


def _rms(v, g):
    return v * lax.rsqrt(jnp.mean(v * v, axis=-1, keepdims=True) + EPS) * g


def _ln(v, g, b):
    mu = jnp.mean(v, axis=-1, keepdims=True)
    d = v - mu
    var = jnp.mean(d * d, axis=-1, keepdims=True)
    return d * lax.rsqrt(var + EPS) * g + b


def _dot(a, b):
    return jnp.dot(a, b, preferred_element_type=F32)


def _prep_kernel(x_ref, pos_ref, win_ref, qg_ref, wq_ref, kvg_ref, wk_ref, wv_ref, inv_ref, sgn_ref,
                 lng_ref, lnb_ref, gavg_ref, ws_ref, bias_ref, gog_ref,
                 q_ref, k_ref, v_ref, g_ref):
    rows = x_ref.shape[1]
    h = _dot(x_ref[0].astype(BF16), win_ref[...])

    ang = pos_ref[0].astype(F32) * inv_ref[...]
    cos_t = jnp.cos(ang)
    sin_t = jnp.sin(ang) * sgn_ref[...]

    cq = _rms(h[:, C_Q:C_Q + Q_RANK], qg_ref[...]).astype(BF16)
    q2 = _dot(cq, wq_ref[...])
    for hd in range(MLA_HEADS):
        lo = hd * LANES
        qh = q2[:, lo:lo + LANES] * cos_t + q2[:, HP + lo:HP + lo + LANES] * sin_t
        q_ref[0, :, lo:lo + LANES] = (qh * SM_SCALE).astype(BF16)

    ckv = _rms(h[:, C_KV:C_KV + KV_RANK], kvg_ref[...]).astype(BF16)
    kp = _dot(ckv, wk_ref[...])
    kr = h[:, C_KRA:C_KRA + LANES] * cos_t + h[:, C_KRB:C_KRB + LANES] * sin_t
    for hd in range(MLA_HEADS):
        lo = hd * LANES
        k_ref[0, :, lo:lo + LANES] = (kp[:, lo:lo + LANES] + kr).astype(BF16)
    v_ref[0] = _dot(ckv, wv_ref[...]).astype(BF16)

    u = jax.nn.gelu(h[:, C_U:C_U + GM_OUT])
    vv = jax.nn.gelu(h[:, C_V:C_V + GM_OUT])
    mu = _dot(vv.astype(BF16), gavg_ref[...])
    d = vv - mu
    var = _dot((d * d).astype(BF16), gavg_ref[...])
    vn = (d * lax.rsqrt(var + EPS) * lng_ref[...] + lnb_ref[...]).astype(BF16)

    tri = lax.broadcasted_iota(jnp.int32, (CHUNK, CHUNK), 0) >= lax.broadcasted_iota(jnp.int32, (CHUNK, CHUNK), 1)
    wm = [jnp.where(tri, ws_ref[g], 0.0).astype(BF16) for g in range(GM_GROUPS)]
    low_half = lax.broadcasted_iota(jnp.int32, (CHUNK, LANES), 1) < GM_CH
    for c in range(rows // CHUNK):
        r0 = c * CHUNK
        parts = []
        for pr in range(GM_GROUPS // 2):
            tile = vn[r0:r0 + CHUNK, pr * LANES:(pr + 1) * LANES]
            parts.append(jnp.where(low_half, _dot(wm[2 * pr], tile), _dot(wm[2 * pr + 1], tile)))
        sg = jnp.concatenate(parts, axis=1) + bias_ref[...]
        gm = u[r0:r0 + CHUNK] * sg
        g_ref[0, r0:r0 + CHUNK, :] = _rms(gm, gog_ref[...]).astype(BF16)


def _prep(x, pos3, w):
    B, S, D = x.shape
    ts = PREP_ROWS
    full = lambda a: pl.BlockSpec(a.shape, lambda b, i: (0,) * a.ndim)
    consts = [w["win"], w["qg"], w["wq"], w["kvg"], w["wk"], w["wv"], w["inv"], w["sgn"],
              w["lng"], w["lnb"], w["gavg"], w["ws"], w["bias"], w["gog"]]
    return pl.pallas_call(
        _prep_kernel,
        grid=(B, S // ts),
        in_specs=[pl.BlockSpec((1, ts, D), lambda b, i: (b, i, 0)),
                  pl.BlockSpec((1, ts, 1), lambda b, i: (b, i, 0))] + [full(a) for a in consts],
        out_specs=[pl.BlockSpec((1, ts, HP), lambda b, i: (b, i, 0)),
                   pl.BlockSpec((1, ts, HP), lambda b, i: (b, i, 0)),
                   pl.BlockSpec((1, ts, HP), lambda b, i: (b, i, 0)),
                   pl.BlockSpec((1, ts, GM_OUT), lambda b, i: (b, i, 0))],
        out_shape=[jax.ShapeDtypeStruct((B, S, HP), BF16)] * 3 + [jax.ShapeDtypeStruct((B, S, GM_OUT), BF16)],
        compiler_params=pltpu.CompilerParams(dimension_semantics=("parallel", "parallel"),
                                             vmem_limit_bytes=VMEM_LIMIT),
        name="prep",
    )(x, pos3, *consts)


def _attn_kernel(q_ref, k_ref, v_ref, g_ref, x_ref, woa_ref, wog_ref, mog_ref, l1g_ref, l1b_ref,
                 o_ref, m_scr, l_scr, acc_scr):
    i = pl.program_id(1)
    tq = q_ref.shape[1]
    tk = tq
    row = lax.broadcasted_iota(jnp.int32, (tq, tk), 0)
    col = lax.broadcasted_iota(jnp.int32, (tq, tk), 1)
    diag_mask = col <= row

    m_scr[...] = jnp.full(m_scr.shape, -1e30, F32)
    l_scr[...] = jnp.zeros(l_scr.shape, F32)
    acc_scr[...] = jnp.zeros(acc_scr.shape, F32)

    def kv_step(j, masked):
        k0 = pl.multiple_of(j * tk, tk)
        for hd in range(MLA_HEADS):
            lo = hd * LANES
            qh = q_ref[0, :, lo:lo + LANES]
            kj = k_ref[0, pl.ds(k0, tk), lo:lo + LANES]
            vj = v_ref[0, pl.ds(k0, tk), lo:lo + LANES]
            s = lax.dot_general(qh, kj, (((1,), (1,)), ((), ())), preferred_element_type=F32)
            if masked:
                s = jnp.where(diag_mask, s, -1e30)
            m_prev = m_scr[hd]
            m_new = jnp.maximum(m_prev, jnp.max(s, axis=-1, keepdims=True))
            p = jnp.exp(s - jnp.concatenate([m_new] * (tk // LANES), axis=1))
            scale = jnp.exp(m_prev - m_new)
            l_scr[hd] = scale * l_scr[hd] + jnp.sum(p, axis=-1, keepdims=True)
            acc_scr[hd] = scale * acc_scr[hd] + _dot(p.astype(BF16), vj)
            m_scr[hd] = m_new

    def full_step(j, c):
        kv_step(j, False)
        return c

    lax.fori_loop(0, i, full_step, 0)
    kv_step(i, True)

    a = jnp.concatenate([acc_scr[hd] / l_scr[hd] for hd in range(MLA_HEADS)], axis=1)
    a = _rms_padded(a, mog_ref[...])
    mix = _dot(a.astype(BF16), woa_ref[...]) + _dot(g_ref[0], wog_ref[...])
    o_ref[0] = _ln(ALPHA * x_ref[0] + mix, l1g_ref[...], l1b_ref[...])


def _rms_padded(v, g):
    ms = jnp.sum(v * v, axis=-1, keepdims=True) * (1.0 / MLA_OUT)
    return v * lax.rsqrt(ms + EPS) * g


def _attn(q, k, v, g, x, w):
    B, S, D = x.shape
    tq = ATTN_ROWS
    full = lambda a: pl.BlockSpec(a.shape, lambda b, i: (0,) * a.ndim)
    consts = [w["woa"], w["wog"], w["mog"], w["l1g"], w["l1b"]]
    return pl.pallas_call(
        _attn_kernel,
        grid=(B, S // tq),
        in_specs=[pl.BlockSpec((1, tq, HP), lambda b, i: (b, i, 0)),
                  pl.BlockSpec((1, S, HP), lambda b, i: (b, 0, 0)),
                  pl.BlockSpec((1, S, HP), lambda b, i: (b, 0, 0)),
                  pl.BlockSpec((1, tq, GM_OUT), lambda b, i: (b, i, 0)),
                  pl.BlockSpec((1, tq, D), lambda b, i: (b, i, 0))] + [full(a) for a in consts],
        out_specs=pl.BlockSpec((1, tq, D), lambda b, i: (b, i, 0)),
        out_shape=jax.ShapeDtypeStruct((B, S, D), F32),
        scratch_shapes=[pltpu.VMEM((MLA_HEADS, tq, LANES), F32)] * 3,
        compiler_params=pltpu.CompilerParams(dimension_semantics=("parallel", "parallel"),
                                             vmem_limit_bytes=VMEM_LIMIT),
        name="attn",
    )(q, k, v, g, x, *consts)


def _route_kernel(x_ref, wr_ref, br_ref, info_ref, cnt_ref, carry_scr, tri_scr):
    step = pl.program_id(0)
    tt = x_ref.shape[0]

    @pl.when(step == 0)
    def _():
        carry_scr[...] = jnp.zeros_like(carry_scr)
        r = lax.broadcasted_iota(jnp.int32, (tt, tt), 0)
        c = lax.broadcasted_iota(jnp.int32, (tt, tt), 1)
        tri_scr[...] = jnp.where(c < r, 1.0, 0.0).astype(BF16)

    x = x_ref[...]
    xh = x.astype(BF16)
    xl = (x - xh.astype(F32)).astype(BF16)
    wr = wr_ref[...]
    wh = wr.astype(BF16)
    wl = (wr - wh.astype(F32)).astype(BF16)
    logits = _dot(xh, wh) + _dot(xl, wh) + _dot(xh, wl) + br_ref[...]

    lane = lax.broadcasted_iota(jnp.int32, (tt, LANES), 1)
    neg = jnp.float32(-jnp.inf)

    is_g = lane < N_GROUPS
    lg = jnp.where(is_g, logits, neg)
    gmax = jnp.max(lg, axis=-1, keepdims=True)
    g_idx = jnp.min(jnp.where(lg == gmax, lane, LANES), axis=-1, keepdims=True)
    g_den = jnp.sum(jnp.where(is_g, jnp.exp(lg - gmax), 0.0), axis=-1, keepdims=True)
    g_p = 1.0 / g_den

    in_grp = (lane >= R_OFF) & (lane < R_OFF + N_EXPERTS) & (((lane - R_OFF) >> 3) == g_idx)
    le = jnp.where(in_grp, logits, neg)
    m1 = jnp.max(le, axis=-1, keepdims=True)
    i1 = jnp.min(jnp.where(le == m1, lane, LANES), axis=-1, keepdims=True)
    le2 = jnp.where(lane == i1, neg, le)
    m2 = jnp.max(le2, axis=-1, keepdims=True)
    i2 = jnp.min(jnp.where(le2 == m2, lane, LANES), axis=-1, keepdims=True)
    e2 = jnp.exp(m2 - m1)
    gate0 = g_p / (1.0 + e2)
    gate1 = g_p * e2 / (1.0 + e2)

    hit1 = lane == i1
    hit2 = lane == i2
    onehot = jnp.where(hit1 | hit2, 1.0, 0.0)
    before = _dot(tri_scr[...], onehot.astype(BF16)) + carry_scr[...]
    r0 = jnp.sum(jnp.where(hit1, before, 0.0), axis=-1, keepdims=True)
    r1 = jnp.sum(jnp.where(hit2, before, 0.0), axis=-1, keepdims=True)
    carry_scr[...] = carry_scr[...] + jnp.sum(onehot, axis=0, keepdims=True)
    cnt_ref[...] = carry_scr[...]

    info = jnp.where(lane == I_E0, (i1 - R_OFF).astype(F32), 0.0)
    info = jnp.where(lane == I_E1, (i2 - R_OFF).astype(F32), info)
    info = jnp.where(lane == I_R0, r0, info)
    info = jnp.where(lane == I_R1, r1, info)
    info = jnp.where(lane == I_G0, gate0, info)
    info = jnp.where(lane == I_G1, gate1, info)
    info_ref[...] = info


def _route(x1, wr, br):
    T, D = x1.shape
    tt = ROUTE_ROWS
    return pl.pallas_call(
        _route_kernel,
        grid=(T // tt,),
        in_specs=[pl.BlockSpec((tt, D), lambda i: (i, 0)),
                  pl.BlockSpec(wr.shape, lambda i: (0, 0)),
                  pl.BlockSpec(br.shape, lambda i: (0, 0))],
        out_specs=[pl.BlockSpec((tt, LANES), lambda i: (i, 0)),
                   pl.BlockSpec((1, LANES), lambda i: (0, 0))],
        out_shape=[jax.ShapeDtypeStruct((T, LANES), F32), jax.ShapeDtypeStruct((1, LANES), F32)],
        scratch_shapes=[pltpu.VMEM((1, LANES), F32), pltpu.VMEM((tt, tt), BF16)],
        compiler_params=pltpu.CompilerParams(dimension_semantics=("arbitrary",), vmem_limit_bytes=VMEM_LIMIT),
        name="route",
    )(x1, wr, br)


def _row_copy(src_ref, src_row, dst_ref, dst_row, sem):
    return pltpu.make_async_copy(src_ref.at[pl.ds(src_row, 1)], dst_ref.at[pl.ds(dst_row, 1)], sem)


def _dispatch_kernel(dest_ref, x_ref, zero_ref, buf_ref, sem):
    del zero_ref
    rows = x_ref.shape[0]

    def start(r, c):
        for kk in range(TOP_K):
            _row_copy(x_ref, r, buf_ref, dest_ref[0, 0, TOP_K * r + kk], sem).start()
        return c

    def wait(r, c):
        for kk in range(TOP_K):
            _row_copy(x_ref, r, buf_ref, dest_ref[0, 0, TOP_K * r + kk], sem).wait()
        return c

    lax.fori_loop(0, rows, start, 0)
    lax.fori_loop(0, rows, wait, 0)


def _dispatch(dest3, x1, n_rows):
    T, D = x1.shape
    td = MOVE_ROWS
    zeros = jnp.zeros((n_rows, D), F32)
    return pl.pallas_call(
        _dispatch_kernel,
        grid=(T // td,),
        in_specs=[pl.BlockSpec((1, 1, TOP_K * td), lambda i: (i, 0, 0), memory_space=pltpu.SMEM),
                  pl.BlockSpec((td, D), lambda i: (i, 0)),
                  pl.BlockSpec(memory_space=pl.ANY)],
        out_specs=pl.BlockSpec(memory_space=pl.ANY),
        out_shape=jax.ShapeDtypeStruct((n_rows, D), F32),
        scratch_shapes=[pltpu.SemaphoreType.DMA(())],
        input_output_aliases={2: 0},
        compiler_params=pltpu.CompilerParams(dimension_semantics=("arbitrary",), vmem_limit_bytes=VMEM_LIMIT),
        name="dispatch",
    )(dest3, x1, zeros)


def _expert_kernel(be_ref, buf_ref, wg_ref, wu_ref, wd_ref, y_ref, wg_scr, wu_scr, wd_scr):
    b = pl.program_id(0)
    e = be_ref[b]
    prev = be_ref[jnp.maximum(b - 1, 0)]

    @pl.when((b == 0) | (e != prev))
    def _():
        wg_scr[...] = wg_ref[0].astype(BF16)
        wu_scr[...] = wu_ref[0].astype(BF16)
        wd_scr[...] = wd_ref[0].astype(BF16)

    xb = buf_ref[...].astype(BF16)
    hidden = jax.nn.silu(_dot(xb, wg_scr[...])) * _dot(xb, wu_scr[...])
    y_ref[...] = _dot(hidden.astype(BF16), wd_scr[...])


def _experts(block_expert, buf, w_gate, w_up, w_down):
    n_rows, D = buf.shape
    bm = EXPERT_ROWS
    ff = w_gate.shape[-1]
    grid_spec = pltpu.PrefetchScalarGridSpec(
        num_scalar_prefetch=1,
        grid=(n_rows // bm,),
        in_specs=[pl.BlockSpec((bm, D), lambda b, be: (b, 0)),
                  pl.BlockSpec((1, D, ff), lambda b, be: (be[b], 0, 0)),
                  pl.BlockSpec((1, D, ff), lambda b, be: (be[b], 0, 0)),
                  pl.BlockSpec((1, ff, D), lambda b, be: (be[b], 0, 0))],
        out_specs=pl.BlockSpec((bm, D), lambda b, be: (b, 0)),
        scratch_shapes=[pltpu.VMEM((D, ff), BF16), pltpu.VMEM((D, ff), BF16), pltpu.VMEM((ff, D), BF16)],
    )
    return pl.pallas_call(
        _expert_kernel,
        grid_spec=grid_spec,
        out_shape=jax.ShapeDtypeStruct((n_rows, D), F32),
        compiler_params=pltpu.CompilerParams(dimension_semantics=("arbitrary",), vmem_limit_bytes=VMEM_LIMIT),
        name="experts",
    )(block_expert, buf, w_gate, w_up, w_down)


def _final_kernel(dest_ref, x_ref, info_ref, y_ref, p_ref, wpg_ref, bpg_ref, wpp_ref,
                  l2g_ref, l2b_ref, l3g_ref, l3b_ref, o_ref, rows_scr, sem):
    rows = x_ref.shape[0]

    def copy(r, kk):
        return _row_copy(y_ref, dest_ref[0, 0, TOP_K * r + kk], rows_scr.at[kk], r, sem)

    def start(r, c):
        for kk in range(TOP_K):
            copy(r, kk).start()
        return c

    def wait(r, c):
        for kk in range(TOP_K):
            copy(r, kk).wait()
        return c

    lax.fori_loop(0, rows, start, 0)
    pp = _dot(p_ref[...].astype(BF16), wpp_ref[...])
    lax.fori_loop(0, rows, wait, 0)

    info = info_ref[...]
    gate0 = info[:, I_G0:I_G0 + 1]
    gate1 = info[:, I_G1:I_G1 + 1]
    moe = rows_scr[0] * gate0 + rows_scr[1] * gate1
    x2 = _ln(ALPHA * x_ref[...] + moe, l2g_ref[...], l2b_ref[...])
    gate = jax.nn.sigmoid(_dot(x2.astype(BF16), wpg_ref[...]) + bpg_ref[...])
    o_ref[...] = _ln(ALPHA * x2 + gate * pp, l3g_ref[...], l3b_ref[...])


def _final(dest3, x1, info, y, p2, w):
    T, D = x1.shape
    tc = MOVE_ROWS
    pd = p2.shape[1]
    full = lambda a: pl.BlockSpec(a.shape, lambda i: (0,) * a.ndim)
    consts = [w["wpg"], w["bpg"], w["wpp"], w["l2g"], w["l2b"], w["l3g"], w["l3b"]]
    return pl.pallas_call(
        _final_kernel,
        grid=(T // tc,),
        in_specs=[pl.BlockSpec((1, 1, TOP_K * tc), lambda i: (i, 0, 0), memory_space=pltpu.SMEM),
                  pl.BlockSpec((tc, D), lambda i: (i, 0)),
                  pl.BlockSpec((tc, LANES), lambda i: (i, 0)),
                  pl.BlockSpec(memory_space=pl.ANY),
                  pl.BlockSpec((tc, pd), lambda i: (i, 0))] + [full(a) for a in consts],
        out_specs=pl.BlockSpec((tc, D), lambda i: (i, 0)),
        out_shape=jax.ShapeDtypeStruct((T, D), F32),
        scratch_shapes=[pltpu.VMEM((TOP_K, tc, D), F32), pltpu.SemaphoreType.DMA(())],
        compiler_params=pltpu.CompilerParams(dimension_semantics=("arbitrary",), vmem_limit_bytes=VMEM_LIMIT),
        name="final",
    )(dest3, x1, info, y, p2, *consts)


def _pad_heads(a, width):
    lead = a.shape[:-1]
    a = a.reshape(lead + (MLA_HEADS, width))
    a = jnp.pad(a, [(0, 0)] * len(lead) + [(0, 0), (0, LANES - width)])
    return a.reshape(lead + (HP,))


def _layer_weights(w_in, q_norm_g, w_q_up, kv_norm_g, w_kv_up, gm_ln_g, gm_ln_b, gm_w_s, gm_b_s,
                   mla_out_g, gm_out_g, w_o, ln1_g, ln1_b):
    D = w_in.shape[0]
    half = QK_ROPE // 2
    c1, c2, c3 = Q_RANK, Q_RANK + KV_RANK, Q_RANK + KV_RANK + QK_ROPE
    wkr = w_in[:, c2:c3]
    zeros = lambda *s: jnp.zeros(s, F32)
    kra = jnp.concatenate([zeros(D, QK_NOPE), wkr, zeros(D, LANES - QK_NOPE - QK_ROPE)], axis=1)
    krb = jnp.concatenate([zeros(D, QK_NOPE), wkr[:, half:], wkr[:, :half], zeros(D, LANES - QK_NOPE - QK_ROPE)], axis=1)
    win = jnp.concatenate([w_in[:, :c2], kra, krb, w_in[:, c3:]], axis=1).astype(BF16)

    wq3 = w_q_up.reshape(Q_RANK, MLA_HEADS, QK_NOPE + QK_ROPE)
    rope = wq3[..., QK_NOPE:]
    rope_sw = jnp.concatenate([rope[..., half:], rope[..., :half]], axis=-1)
    sw3 = jnp.concatenate([jnp.zeros_like(wq3[..., :QK_NOPE]), rope_sw], axis=-1)
    wq = jnp.concatenate([_pad_heads(w_q_up, QK_NOPE + QK_ROPE),
                          _pad_heads(sw3.reshape(Q_RANK, -1), QK_NOPE + QK_ROPE)], axis=1).astype(BF16)

    wkv3 = w_kv_up.reshape(KV_RANK, MLA_HEADS, QK_NOPE + V_HEAD)
    wk = _pad_heads(wkv3[..., :QK_NOPE].reshape(KV_RANK, -1), QK_NOPE).astype(BF16)
    wv = _pad_heads(wkv3[..., QK_NOPE:].reshape(KV_RANK, -1), V_HEAD).astype(BF16)

    inv = ROPE_THETA ** (-jnp.arange(0, QK_ROPE, 2, dtype=F32) / QK_ROPE)
    pad_l, pad_r = jnp.zeros((QK_NOPE,), F32), jnp.zeros((LANES - QK_NOPE - QK_ROPE,), F32)
    inv_t = jnp.concatenate([pad_l, inv, inv, pad_r])[None, :]
    sgn_t = jnp.concatenate([pad_l, -jnp.ones((half,), F32), jnp.ones((half,), F32), pad_r])[None, :]

    grp = jnp.arange(GM_OUT) // GM_CH
    gavg = jnp.where(grp[:, None] == grp[None, :], 1.0 / GM_CH, 0.0).astype(BF16)
    bias = jnp.repeat(gm_b_s.T, GM_CH, axis=1)

    woa = _pad_heads(w_o[:MLA_OUT].T, V_HEAD).T.astype(BF16)
    wog = w_o[MLA_OUT:].astype(BF16)
    return dict(win=win, qg=q_norm_g[None, :], wq=wq, kvg=kv_norm_g[None, :], wk=wk, wv=wv, inv=inv_t, sgn=sgn_t,
                lng=gm_ln_g[None, :], lnb=gm_ln_b[None, :], gavg=gavg, ws=gm_w_s, bias=bias, gog=gm_out_g[None, :],
                woa=woa, wog=wog, mog=_pad_heads(mla_out_g, V_HEAD)[None, :], l1g=ln1_g[None, :], l1b=ln1_b[None, :])


def _moe(x1, w_rg, b_rg, w_re, b_re, w_gate, w_up, w_down):
    T, D = x1.shape
    pad = jnp.zeros((D, LANES - N_GROUPS - N_EXPERTS), F32)
    wr = jnp.concatenate([w_rg, w_re, pad], axis=1)
    br = jnp.concatenate([b_rg, b_re, pad[0]])[None, :]
    info, cnt = _route(x1, wr, br)

    bm = EXPERT_ROWS
    n_blocks = (T * TOP_K) // bm + N_EXPERTS
    counts = cnt[0, R_OFF:R_OFF + N_EXPERTS].astype(jnp.int32)
    padded = (counts + bm - 1) // bm * bm
    pad_ends = jnp.cumsum(padded)
    pad_starts = pad_ends - padded
    e_idx = info[:, I_E0:I_E1 + 1].astype(jnp.int32)
    rank = info[:, I_R0:I_R1 + 1].astype(jnp.int32)
    dest = (pad_starts[e_idx] + rank).reshape(T // MOVE_ROWS, 1, TOP_K * MOVE_ROWS)
    block_start = jnp.arange(n_blocks, dtype=jnp.int32) * bm
    block_expert = jnp.minimum(jnp.sum(pad_ends[None, :] <= block_start[:, None], axis=1),
                               N_EXPERTS - 1).astype(jnp.int32)

    buf = _dispatch(dest, x1, n_blocks * bm)
    y = _experts(block_expert, buf, w_gate, w_up, w_down)
    return info, dest, y


def kernel(x, p, positions, w_in, q_norm_g, w_q_up, kv_norm_g, w_kv_up, gm_ln_g, gm_ln_b, gm_w_s, gm_b_s, mla_out_g, gm_out_g, w_o, ln1_g, ln1_b, w_rg, b_rg, w_re, b_re, w_gate, w_up, w_down, ln2_g, ln2_b, w_pg, b_pg, w_pp, ln3_g, ln3_b):
    B, S, D = x.shape
    T = B * S
    assert S % ATTN_ROWS == 0 and S % PREP_ROWS == 0 and PREP_ROWS % CHUNK == 0
    assert T % ROUTE_ROWS == 0 and T % MOVE_ROWS == 0 and (T * TOP_K) % EXPERT_ROWS == 0
    pos3 = positions.reshape(B, S, 1)
    for i in range(DEPTH):
        w = _layer_weights(w_in[i], q_norm_g[i], w_q_up[i], kv_norm_g[i], w_kv_up[i], gm_ln_g[i], gm_ln_b[i],
                           gm_w_s[i], gm_b_s[i], mla_out_g[i], gm_out_g[i], w_o[i], ln1_g[i], ln1_b[i])
        q, k, v, g = _prep(x, pos3, w)
        x1 = _attn(q, k, v, g, x, w).reshape(T, D)
        info, dest, y = _moe(x1, w_rg[i], b_rg[i], w_re[i], b_re[i], w_gate[i], w_up[i], w_down[i])
        wf = dict(wpg=w_pg[i].astype(BF16), bpg=b_pg[i][None, :], wpp=w_pp[i].astype(BF16),
                  l2g=ln2_g[i][None, :], l2b=ln2_b[i][None, :], l3g=ln3_g[i][None, :], l3b=ln3_b[i][None, :])
        x = _final(dest, x1, info, y, p[i].reshape(T, -1), wf).reshape(B, S, D)
    return x
```

```python
import functools

import jax
import jax.numpy as jnp
from jax import lax
from jax.experimental import pallas as pl
from jax.experimental.pallas import tpu as pltpu

F32 = jnp.float32
BF16 = jnp.bfloat16

MLA_HEADS = 8
QK_NOPE = 64
QK_ROPE = 32
V_HEAD = 64
Q_RANK = 256
KV_RANK = 128
ROPE_THETA = 10000.0
MLA_OUT = MLA_HEADS * V_HEAD
GM_GROUPS = 8
GM_CH = 64
GM_OUT = GM_GROUPS * GM_CH
CHUNK = 128
N_GROUPS = 4
EXP_PER_GROUP = 8
N_EXPERTS = N_GROUPS * EXP_PER_GROUP
TOP_K = 2
EPS = 1e-6
DEPTH = 1
ALPHA = (2.0 * DEPTH) ** 0.25
SM_SCALE = (QK_NOPE + QK_ROPE) ** -0.5

LANES = 128
VMEM_LIMIT = 56 * 1024 * 1024

PREP_ROWS = 256
ATTN_ROWS = 256
ROUTE_ROWS = 512
MOVE_ROWS = 256
MOVE_UNROLL = 8
EXPERT_ROWS = 256

C_Q = 0
C_KV = C_Q + Q_RANK
C_KRA = C_KV + KV_RANK
C_KRB = C_KRA + LANES
C_U = C_KRB + LANES
C_V = C_U + GM_OUT
C_END = C_V + GM_OUT
HP = MLA_HEADS * LANES

I_E0, I_E1, I_R0, I_R1, I_G0, I_G1 = range(6)
R_OFF = N_GROUPS


def _rms(v, g):
    return v * lax.rsqrt(jnp.mean(v * v, axis=-1, keepdims=True) + EPS) * g


def _ln(v, g, b):
    mu = jnp.mean(v, axis=-1, keepdims=True)
    d = v - mu
    var = jnp.mean(d * d, axis=-1, keepdims=True)
    return d * lax.rsqrt(var + EPS) * g + b


def _dot(a, b):
    return jnp.dot(a, b, preferred_element_type=F32)


def _prep_kernel(x_ref, pos_ref, win_ref, qg_ref, wq_ref, kvg_ref, wk_ref, wv_ref, inv_ref, sgn_ref,
                 lng_ref, lnb_ref, gavg_ref, ws_ref, bias_ref, gog_ref,
                 q_ref, k_ref, v_ref, g_ref):
    rows = x_ref.shape[1]
    h = _dot(x_ref[0].astype(BF16), win_ref[...])

    ang = pos_ref[0].astype(F32) * inv_ref[...]
    cos_t = jnp.cos(ang)
    sin_t = jnp.sin(ang) * sgn_ref[...]

    cq = _rms(h[:, C_Q:C_Q + Q_RANK], qg_ref[...]).astype(BF16)
    q2 = _dot(cq, wq_ref[...])
    for hd in range(MLA_HEADS):
        lo = hd * LANES
        qh = q2[:, lo:lo + LANES] * cos_t + q2[:, HP + lo:HP + lo + LANES] * sin_t
        q_ref[0, :, lo:lo + LANES] = (qh * SM_SCALE).astype(BF16)

    ckv = _rms(h[:, C_KV:C_KV + KV_RANK], kvg_ref[...]).astype(BF16)
    kp = _dot(ckv, wk_ref[...])
    kr = h[:, C_KRA:C_KRA + LANES] * cos_t + h[:, C_KRB:C_KRB + LANES] * sin_t
    for hd in range(MLA_HEADS):
        lo = hd * LANES
        k_ref[0, :, lo:lo + LANES] = (kp[:, lo:lo + LANES] + kr).astype(BF16)
    v_ref[0] = _dot(ckv, wv_ref[...]).astype(BF16)

    u = jax.nn.gelu(h[:, C_U:C_U + GM_OUT])
    vv = jax.nn.gelu(h[:, C_V:C_V + GM_OUT])
    mu = _dot(vv.astype(BF16), gavg_ref[...])
    d = vv - mu
    var = _dot((d * d).astype(BF16), gavg_ref[...])
    vn = (d * lax.rsqrt(var + EPS) * lng_ref[...] + lnb_ref[...]).astype(BF16)

    tri = lax.broadcasted_iota(jnp.int32, (CHUNK, CHUNK), 0) >= lax.broadcasted_iota(jnp.int32, (CHUNK, CHUNK), 1)
    wm = [jnp.where(tri, ws_ref[g], 0.0).astype(BF16) for g in range(GM_GROUPS)]
    low_half = lax.broadcasted_iota(jnp.int32, (CHUNK, LANES), 1) < GM_CH
    for c in range(rows // CHUNK):
        r0 = c * CHUNK
        parts = []
        for pr in range(GM_GROUPS // 2):
            tile = vn[r0:r0 + CHUNK, pr * LANES:(pr + 1) * LANES]
            parts.append(jnp.where(low_half, _dot(wm[2 * pr], tile), _dot(wm[2 * pr + 1], tile)))
        sg = jnp.concatenate(parts, axis=1) + bias_ref[...]
        gm = u[r0:r0 + CHUNK] * sg
        g_ref[0, r0:r0 + CHUNK, :] = _rms(gm, gog_ref[...]).astype(BF16)


def _prep(x, pos3, w):
    B, S, D = x.shape
    ts = PREP_ROWS
    full = lambda a: pl.BlockSpec(a.shape, lambda b, i: (0,) * a.ndim)
    consts = [w["win"], w["qg"], w["wq"], w["kvg"], w["wk"], w["wv"], w["inv"], w["sgn"],
              w["lng"], w["lnb"], w["gavg"], w["ws"], w["bias"], w["gog"]]
    return pl.pallas_call(
        _prep_kernel,
        grid=(B, S // ts),
        in_specs=[pl.BlockSpec((1, ts, D), lambda b, i: (b, i, 0)),
                  pl.BlockSpec((1, ts, 1), lambda b, i: (b, i, 0))] + [full(a) for a in consts],
        out_specs=[pl.BlockSpec((1, ts, HP), lambda b, i: (b, i, 0)),
                   pl.BlockSpec((1, ts, HP), lambda b, i: (b, i, 0)),
                   pl.BlockSpec((1, ts, HP), lambda b, i: (b, i, 0)),
                   pl.BlockSpec((1, ts, GM_OUT), lambda b, i: (b, i, 0))],
        out_shape=[jax.ShapeDtypeStruct((B, S, HP), BF16)] * 3 + [jax.ShapeDtypeStruct((B, S, GM_OUT), BF16)],
        compiler_params=pltpu.CompilerParams(dimension_semantics=("parallel", "parallel"),
                                             vmem_limit_bytes=VMEM_LIMIT),
        name="prep",
    )(x, pos3, *consts)


def _attn_kernel(q_ref, k_ref, v_ref, g_ref, x_ref, woa_ref, wog_ref, mog_ref, l1g_ref, l1b_ref,
                 o_ref, m_scr, l_scr, acc_scr):
    i = pl.program_id(1)
    tq = q_ref.shape[1]
    tk = tq
    row = lax.broadcasted_iota(jnp.int32, (tq, tk), 0)
    col = lax.broadcasted_iota(jnp.int32, (tq, tk), 1)
    diag_mask = col <= row

    m_scr[...] = jnp.full(m_scr.shape, -1e30, F32)
    l_scr[...] = jnp.zeros(l_scr.shape, F32)
    acc_scr[...] = jnp.zeros(acc_scr.shape, F32)

    def kv_step(j, masked):
        k0 = pl.multiple_of(j * tk, tk)
        for hd in range(MLA_HEADS):
            lo = hd * LANES
            qh = q_ref[0, :, lo:lo + LANES]
            kj = k_ref[0, pl.ds(k0, tk), lo:lo + LANES]
            vj = v_ref[0, pl.ds(k0, tk), lo:lo + LANES]
            s = lax.dot_general(qh, kj, (((1,), (1,)), ((), ())), preferred_element_type=F32)
            if masked:
                s = jnp.where(diag_mask, s, -1e30)
            m_prev = m_scr[hd]
            m_new = jnp.maximum(m_prev, jnp.max(s, axis=-1, keepdims=True))
            p = jnp.exp(s - jnp.concatenate([m_new] * (tk // LANES), axis=1))
            scale = jnp.exp(m_prev - m_new)
            l_scr[hd] = scale * l_scr[hd] + jnp.sum(p, axis=-1, keepdims=True)
            acc_scr[hd] = scale * acc_scr[hd] + _dot(p.astype(BF16), vj)
            m_scr[hd] = m_new

    def full_step(j, c):
        kv_step(j, False)
        return c

    lax.fori_loop(0, i, full_step, 0)
    kv_step(i, True)

    a = jnp.concatenate([acc_scr[hd] / l_scr[hd] for hd in range(MLA_HEADS)], axis=1)
    a = _rms_padded(a, mog_ref[...])
    mix = _dot(a.astype(BF16), woa_ref[...]) + _dot(g_ref[0], wog_ref[...])
    o_ref[0] = _ln(ALPHA * x_ref[0] + mix, l1g_ref[...], l1b_ref[...])


def _rms_padded(v, g):
    ms = jnp.sum(v * v, axis=-1, keepdims=True) * (1.0 / MLA_OUT)
    return v * lax.rsqrt(ms + EPS) * g


def _attn(q, k, v, g, x, w):
    B, S, D = x.shape
    tq = ATTN_ROWS
    full = lambda a: pl.BlockSpec(a.shape, lambda b, i: (0,) * a.ndim)
    consts = [w["woa"], w["wog"], w["mog"], w["l1g"], w["l1b"]]
    return pl.pallas_call(
        _attn_kernel,
        grid=(B, S // tq),
        in_specs=[pl.BlockSpec((1, tq, HP), lambda b, i: (b, i, 0)),
                  pl.BlockSpec((1, S, HP), lambda b, i: (b, 0, 0)),
                  pl.BlockSpec((1, S, HP), lambda b, i: (b, 0, 0)),
                  pl.BlockSpec((1, tq, GM_OUT), lambda b, i: (b, i, 0)),
                  pl.BlockSpec((1, tq, D), lambda b, i: (b, i, 0))] + [full(a) for a in consts],
        out_specs=pl.BlockSpec((1, tq, D), lambda b, i: (b, i, 0)),
        out_shape=jax.ShapeDtypeStruct((B, S, D), F32),
        scratch_shapes=[pltpu.VMEM((MLA_HEADS, tq, LANES), F32)] * 3,
        compiler_params=pltpu.CompilerParams(dimension_semantics=("parallel", "parallel"),
                                             vmem_limit_bytes=VMEM_LIMIT),
        name="attn",
    )(q, k, v, g, x, *consts)


def _route_kernel(x_ref, wr_ref, br_ref, info_ref, cnt_ref, carry_scr, tri_scr):
    step = pl.program_id(0)
    tt = x_ref.shape[0]

    @pl.when(step == 0)
    def _():
        carry_scr[...] = jnp.zeros_like(carry_scr)
        r = lax.broadcasted_iota(jnp.int32, (tt, tt), 0)
        c = lax.broadcasted_iota(jnp.int32, (tt, tt), 1)
        tri_scr[...] = jnp.where(c < r, 1.0, 0.0).astype(BF16)

    x = x_ref[...]
    xh = x.astype(BF16)
    xl = (x - xh.astype(F32)).astype(BF16)
    wr = wr_ref[...]
    wh = wr.astype(BF16)
    wl = (wr - wh.astype(F32)).astype(BF16)
    logits = _dot(xh, wh) + _dot(xl, wh) + _dot(xh, wl) + br_ref[...]

    lane = lax.broadcasted_iota(jnp.int32, (tt, LANES), 1)
    neg = jnp.float32(-jnp.inf)

    is_g = lane < N_GROUPS
    lg = jnp.where(is_g, logits, neg)
    gmax = jnp.max(lg, axis=-1, keepdims=True)
    g_idx = jnp.min(jnp.where(lg == gmax, lane, LANES), axis=-1, keepdims=True)
    g_den = jnp.sum(jnp.where(is_g, jnp.exp(lg - gmax), 0.0), axis=-1, keepdims=True)
    g_p = 1.0 / g_den

    in_grp = (lane >= R_OFF) & (lane < R_OFF + N_EXPERTS) & (((lane - R_OFF) >> 3) == g_idx)
    le = jnp.where(in_grp, logits, neg)
    m1 = jnp.max(le, axis=-1, keepdims=True)
    i1 = jnp.min(jnp.where(le == m1, lane, LANES), axis=-1, keepdims=True)
    le2 = jnp.where(lane == i1, neg, le)
    m2 = jnp.max(le2, axis=-1, keepdims=True)
    i2 = jnp.min(jnp.where(le2 == m2, lane, LANES), axis=-1, keepdims=True)
    e2 = jnp.exp(m2 - m1)
    gate0 = g_p / (1.0 + e2)
    gate1 = g_p * e2 / (1.0 + e2)

    hit1 = lane == i1
    hit2 = lane == i2
    onehot = jnp.where(hit1 | hit2, 1.0, 0.0)
    before = _dot(tri_scr[...], onehot.astype(BF16)) + carry_scr[...]
    r0 = jnp.sum(jnp.where(hit1, before, 0.0), axis=-1, keepdims=True)
    r1 = jnp.sum(jnp.where(hit2, before, 0.0), axis=-1, keepdims=True)
    carry_scr[...] = carry_scr[...] + jnp.sum(onehot, axis=0, keepdims=True)
    cnt_ref[...] = carry_scr[...]

    info = jnp.where(lane == I_E0, (i1 - R_OFF).astype(F32), 0.0)
    info = jnp.where(lane == I_E1, (i2 - R_OFF).astype(F32), info)
    info = jnp.where(lane == I_R0, r0, info)
    info = jnp.where(lane == I_R1, r1, info)
    info = jnp.where(lane == I_G0, gate0, info)
    info = jnp.where(lane == I_G1, gate1, info)
    info_ref[...] = info


def _route(x1, wr, br):
    T, D = x1.shape
    tt = ROUTE_ROWS
    return pl.pallas_call(
        _route_kernel,
        grid=(T // tt,),
        in_specs=[pl.BlockSpec((tt, D), lambda i: (i, 0)),
                  pl.BlockSpec(wr.shape, lambda i: (0, 0)),
                  pl.BlockSpec(br.shape, lambda i: (0, 0))],
        out_specs=[pl.BlockSpec((tt, LANES), lambda i: (i, 0)),
                   pl.BlockSpec((1, LANES), lambda i: (0, 0))],
        out_shape=[jax.ShapeDtypeStruct((T, LANES), F32), jax.ShapeDtypeStruct((1, LANES), F32)],
        scratch_shapes=[pltpu.VMEM((1, LANES), F32), pltpu.VMEM((tt, tt), BF16)],
        compiler_params=pltpu.CompilerParams(dimension_semantics=("arbitrary",), vmem_limit_bytes=VMEM_LIMIT),
        name="route",
    )(x1, wr, br)


def _row_copy(src_ref, src_row, dst_ref, dst_row, sem):
    return pltpu.make_async_copy(src_ref.at[pl.ds(src_row, 1)], dst_ref.at[pl.ds(dst_row, 1)], sem)


def _dispatch_kernel(dest_ref, x_ref, zero_ref, buf_ref, sem):
    del zero_ref
    rows = x_ref.shape[0]

    def start(c, carry):
        for u in range(MOVE_UNROLL):
            r = c * MOVE_UNROLL + u
            for kk in range(TOP_K):
                _row_copy(x_ref, r, buf_ref, dest_ref[0, 0, TOP_K * r + kk], sem).start(priority=kk)
        return carry

    lax.fori_loop(0, rows // MOVE_UNROLL, start, 0)
    done = buf_ref.at[pl.ds(0, TOP_K * rows)]
    pltpu.make_async_copy(done, done, sem).wait()


def _dispatch(dest3, x1, n_rows):
    T, D = x1.shape
    td = MOVE_ROWS
    zeros = jnp.zeros((n_rows, D), F32)
    return pl.pallas_call(
        _dispatch_kernel,
        grid=(T // td,),
        in_specs=[pl.BlockSpec((1, 1, TOP_K * td), lambda i: (i, 0, 0), memory_space=pltpu.SMEM),
                  pl.BlockSpec((td, D), lambda i: (i, 0)),
                  pl.BlockSpec(memory_space=pl.ANY)],
        out_specs=pl.BlockSpec(memory_space=pl.ANY),
        out_shape=jax.ShapeDtypeStruct((n_rows, D), F32),
        scratch_shapes=[pltpu.SemaphoreType.DMA(())],
        input_output_aliases={2: 0},
        compiler_params=pltpu.CompilerParams(dimension_semantics=("arbitrary",), vmem_limit_bytes=VMEM_LIMIT),
        name="dispatch",
    )(dest3, x1, zeros)


def _expert_kernel(be_ref, buf_ref, wg_ref, wu_ref, wd_ref, y_ref, wg_scr, wu_scr, wd_scr):
    b = pl.program_id(0)
    e = be_ref[b]
    prev = be_ref[jnp.maximum(b - 1, 0)]

    @pl.when((b == 0) | (e != prev))
    def _():
        wg_scr[...] = wg_ref[0].astype(BF16)
        wu_scr[...] = wu_ref[0].astype(BF16)
        wd_scr[...] = wd_ref[0].astype(BF16)

    xb = buf_ref[...].astype(BF16)
    hidden = jax.nn.silu(_dot(xb, wg_scr[...])) * _dot(xb, wu_scr[...])
    y_ref[...] = _dot(hidden.astype(BF16), wd_scr[...])


def _experts(block_expert, buf, w_gate, w_up, w_down):
    n_rows, D = buf.shape
    bm = EXPERT_ROWS
    ff = w_gate.shape[-1]
    grid_spec = pltpu.PrefetchScalarGridSpec(
        num_scalar_prefetch=1,
        grid=(n_rows // bm,),
        in_specs=[pl.BlockSpec((bm, D), lambda b, be: (b, 0)),
                  pl.BlockSpec((1, D, ff), lambda b, be: (be[b], 0, 0)),
                  pl.BlockSpec((1, D, ff), lambda b, be: (be[b], 0, 0)),
                  pl.BlockSpec((1, ff, D), lambda b, be: (be[b], 0, 0))],
        out_specs=pl.BlockSpec((bm, D), lambda b, be: (b, 0)),
        scratch_shapes=[pltpu.VMEM((D, ff), BF16), pltpu.VMEM((D, ff), BF16), pltpu.VMEM((ff, D), BF16)],
    )
    return pl.pallas_call(
        _expert_kernel,
        grid_spec=grid_spec,
        out_shape=jax.ShapeDtypeStruct((n_rows, D), F32),
        compiler_params=pltpu.CompilerParams(dimension_semantics=("arbitrary",), vmem_limit_bytes=VMEM_LIMIT),
        name="experts",
    )(block_expert, buf, w_gate, w_up, w_down)


def _final_kernel(dcur_ref, dnxt_ref, x_ref, info_ref, y_ref, p_ref, wpg_ref, bpg_ref, wpp_ref,
                  l2g_ref, l2b_ref, l3g_ref, l3b_ref, o_ref, rows_scr, sem):
    i = pl.program_id(0)
    rows = x_ref.shape[0]
    slot = i % 2

    def gather(dref, s):
        def start(c, carry):
            for u in range(MOVE_UNROLL):
                r = c * MOVE_UNROLL + u
                for kk in range(TOP_K):
                    _row_copy(y_ref, dref[0, 0, TOP_K * r + kk], rows_scr.at[s, kk], r, sem.at[s]).start(priority=kk)
            return carry

        lax.fori_loop(0, rows // MOVE_UNROLL, start, 0)

    @pl.when(i == 0)
    def _():
        gather(dcur_ref, 0)

    @pl.when(i + 1 < pl.num_programs(0))
    def _():
        gather(dnxt_ref, 1 - slot)

    pp = _dot(p_ref[...].astype(BF16), wpp_ref[...])
    pltpu.make_async_copy(rows_scr.at[slot], rows_scr.at[slot], sem.at[slot]).wait()

    info = info_ref[...]
    gate0 = info[:, I_G0:I_G0 + 1]
    gate1 = info[:, I_G1:I_G1 + 1]
    moe = rows_scr[slot, 0] * gate0 + rows_scr[slot, 1] * gate1
    x2 = _ln(ALPHA * x_ref[...] + moe, l2g_ref[...], l2b_ref[...])
    gate = jax.nn.sigmoid(_dot(x2.astype(BF16), wpg_ref[...]) + bpg_ref[...])
    o_ref[...] = _ln(ALPHA * x2 + gate * pp, l3g_ref[...], l3b_ref[...])


def _final(dest3, x1, info, y, p2, w):
    T, D = x1.shape
    tc = MOVE_ROWS
    pd = p2.shape[1]
    full = lambda a: pl.BlockSpec(a.shape, lambda i: (0,) * a.ndim)
    consts = [w["wpg"], w["bpg"], w["wpp"], w["l2g"], w["l2b"], w["l3g"], w["l3b"]]
    last = T // tc - 1
    return pl.pallas_call(
        _final_kernel,
        grid=(T // tc,),
        in_specs=[pl.BlockSpec((1, 1, TOP_K * tc), lambda i: (i, 0, 0), memory_space=pltpu.SMEM),
                  pl.BlockSpec((1, 1, TOP_K * tc), lambda i: (jnp.minimum(i + 1, last), 0, 0), memory_space=pltpu.SMEM),
                  pl.BlockSpec((tc, D), lambda i: (i, 0)),
                  pl.BlockSpec((tc, LANES), lambda i: (i, 0)),
                  pl.BlockSpec(memory_space=pl.ANY),
                  pl.BlockSpec((tc, pd), lambda i: (i, 0))] + [full(a) for a in consts],
        out_specs=pl.BlockSpec((tc, D), lambda i: (i, 0)),
        out_shape=jax.ShapeDtypeStruct((T, D), F32),
        scratch_shapes=[pltpu.VMEM((2, TOP_K, tc, D), F32), pltpu.SemaphoreType.DMA((2,))],
        compiler_params=pltpu.CompilerParams(dimension_semantics=("arbitrary",), vmem_limit_bytes=VMEM_LIMIT),
        name="final",
    )(dest3, dest3, x1, info, y, p2, *consts)


def _pad_heads(a, width):
    lead = a.shape[:-1]
    a = a.reshape(lead + (MLA_HEADS, width))
    a = jnp.pad(a, [(0, 0)] * len(lead) + [(0, 0), (0, LANES - width)])
    return a.reshape(lead + (HP,))


def _layer_weights(w_in, q_norm_g, w_q_up, kv_norm_g, w_kv_up, gm_ln_g, gm_ln_b, gm_w_s, gm_b_s,
                   mla_out_g, gm_out_g, w_o, ln1_g, ln1_b):
    D = w_in.shape[0]
    half = QK_ROPE // 2
    c1, c2, c3 = Q_RANK, Q_RANK + KV_RANK, Q_RANK + KV_RANK + QK_ROPE
    wkr = w_in[:, c2:c3]
    zeros = lambda *s: jnp.zeros(s, F32)
    kra = jnp.concatenate([zeros(D, QK_NOPE), wkr, zeros(D, LANES - QK_NOPE - QK_ROPE)], axis=1)
    krb = jnp.concatenate([zeros(D, QK_NOPE), wkr[:, half:], wkr[:, :half], zeros(D, LANES - QK_NOPE - QK_ROPE)], axis=1)
    win = jnp.concatenate([w_in[:, :c2], kra, krb, w_in[:, c3:]], axis=1).astype(BF16)

    wq3 = w_q_up.reshape(Q_RANK, MLA_HEADS, QK_NOPE + QK_ROPE)
    rope = wq3[..., QK_NOPE:]
    rope_sw = jnp.concatenate([rope[..., half:], rope[..., :half]], axis=-1)
    sw3 = jnp.concatenate([jnp.zeros_like(wq3[..., :QK_NOPE]), rope_sw], axis=-1)
    wq = jnp.concatenate([_pad_heads(w_q_up, QK_NOPE + QK_ROPE),
                          _pad_heads(sw3.reshape(Q_RANK, -1), QK_NOPE + QK_ROPE)], axis=1).astype(BF16)

    wkv3 = w_kv_up.reshape(KV_RANK, MLA_HEADS, QK_NOPE + V_HEAD)
    wk = _pad_heads(wkv3[..., :QK_NOPE].reshape(KV_RANK, -1), QK_NOPE).astype(BF16)
    wv = _pad_heads(wkv3[..., QK_NOPE:].reshape(KV_RANK, -1), V_HEAD).astype(BF16)

    inv = ROPE_THETA ** (-jnp.arange(0, QK_ROPE, 2, dtype=F32) / QK_ROPE)
    pad_l, pad_r = jnp.zeros((QK_NOPE,), F32), jnp.zeros((LANES - QK_NOPE - QK_ROPE,), F32)
    inv_t = jnp.concatenate([pad_l, inv, inv, pad_r])[None, :]
    sgn_t = jnp.concatenate([pad_l, -jnp.ones((half,), F32), jnp.ones((half,), F32), pad_r])[None, :]

    grp = jnp.arange(GM_OUT) // GM_CH
    gavg = jnp.where(grp[:, None] == grp[None, :], 1.0 / GM_CH, 0.0).astype(BF16)
    bias = jnp.repeat(gm_b_s.T, GM_CH, axis=1)

    woa = _pad_heads(w_o[:MLA_OUT].T, V_HEAD).T.astype(BF16)
    wog = w_o[MLA_OUT:].astype(BF16)
    return dict(win=win, qg=q_norm_g[None, :], wq=wq, kvg=kv_norm_g[None, :], wk=wk, wv=wv, inv=inv_t, sgn=sgn_t,
                lng=gm_ln_g[None, :], lnb=gm_ln_b[None, :], gavg=gavg, ws=gm_w_s, bias=bias, gog=gm_out_g[None, :],
                woa=woa, wog=wog, mog=_pad_heads(mla_out_g, V_HEAD)[None, :], l1g=ln1_g[None, :], l1b=ln1_b[None, :])


def _moe(x1, w_rg, b_rg, w_re, b_re, w_gate, w_up, w_down):
    T, D = x1.shape
    pad = jnp.zeros((D, LANES - N_GROUPS - N_EXPERTS), F32)
    wr = jnp.concatenate([w_rg, w_re, pad], axis=1)
    br = jnp.concatenate([b_rg, b_re, pad[0]])[None, :]
    info, cnt = _route(x1, wr, br)

    bm = EXPERT_ROWS
    n_blocks = (T * TOP_K) // bm + N_EXPERTS
    counts = cnt[0, R_OFF:R_OFF + N_EXPERTS].astype(jnp.int32)
    padded = (counts + bm - 1) // bm * bm
    pad_ends = jnp.cumsum(padded)
    pad_starts = pad_ends - padded
    e_idx = info[:, I_E0:I_E1 + 1].astype(jnp.int32)
    rank = info[:, I_R0:I_R1 + 1].astype(jnp.int32)
    seg_start = jnp.sum(jnp.where(e_idx[..., None] == jnp.arange(N_EXPERTS), pad_starts, 0), axis=-1)
    dest = (seg_start + rank).reshape(T // MOVE_ROWS, 1, TOP_K * MOVE_ROWS)
    block_start = jnp.arange(n_blocks, dtype=jnp.int32) * bm
    block_expert = jnp.minimum(jnp.sum(pad_ends[None, :] <= block_start[:, None], axis=1),
                               N_EXPERTS - 1).astype(jnp.int32)

    buf = _dispatch(dest, x1, n_blocks * bm)
    y = _experts(block_expert, buf, w_gate, w_up, w_down)
    return info, dest, y


def kernel(x, p, positions, w_in, q_norm_g, w_q_up, kv_norm_g, w_kv_up, gm_ln_g, gm_ln_b, gm_w_s, gm_b_s, mla_out_g, gm_out_g, w_o, ln1_g, ln1_b, w_rg, b_rg, w_re, b_re, w_gate, w_up, w_down, ln2_g, ln2_b, w_pg, b_pg, w_pp, ln3_g, ln3_b):
    B, S, D = x.shape
    T = B * S
    assert S % ATTN_ROWS == 0 and S % PREP_ROWS == 0 and PREP_ROWS % CHUNK == 0
    assert T % ROUTE_ROWS == 0 and T % MOVE_ROWS == 0 and (T * TOP_K) % EXPERT_ROWS == 0
    pos3 = positions.reshape(B, S, 1)
    for i in range(DEPTH):
        w = _layer_weights(w_in[i], q_norm_g[i], w_q_up[i], kv_norm_g[i], w_kv_up[i], gm_ln_g[i], gm_ln_b[i],
                           gm_w_s[i], gm_b_s[i], mla_out_g[i], gm_out_g[i], w_o[i], ln1_g[i], ln1_b[i])
        q, k, v, g = _prep(x, pos3, w)
        x1 = _attn(q, k, v, g, x, w).reshape(T, D)
        info, dest, y = _moe(x1, w_rg[i], b_rg[i], w_re[i], b_re[i], w_gate[i], w_up[i], w_down[i])
        wf = dict(wpg=w_pg[i].astype(BF16), bpg=b_pg[i][None, :], wpp=w_pp[i].astype(BF16),
                  l2g=ln2_g[i][None, :], l2b=ln2_b[i][None, :], l3g=ln3_g[i][None, :], l3b=ln3_b[i][None, :])
        x = _final(dest, x1, info, y, p[i].reshape(T, -1), wf).reshape(B, S, D)
    return x
```

```python
import functools

import jax
import jax.numpy as jnp
from jax import lax
from jax.experimental import pallas as pl
from jax.experimental.pallas import tpu as pltpu

F32 = jnp.float32
BF16 = jnp.bfloat16

MLA_HEADS = 8
QK_NOPE = 64
QK_ROPE = 32
V_HEAD = 64
Q_RANK = 256
KV_RANK = 128
ROPE_THETA = 10000.0
MLA_OUT = MLA_HEADS * V_HEAD
GM_GROUPS = 8
GM_CH = 64
GM_OUT = GM_GROUPS * GM_CH
CHUNK = 128
N_GROUPS = 4
EXP_PER_GROUP = 8
N_EXPERTS = N_GROUPS * EXP_PER_GROUP
TOP_K = 2
EPS = 1e-6
DEPTH = 1
ALPHA = (2.0 * DEPTH) ** 0.25
SM_SCALE = (QK_NOPE + QK_ROPE) ** -0.5

LANES = 128
SUBLANES = 8
VMEM_LIMIT = 56 * 1024 * 1024

PREP_ROWS = 256
ATTN_ROWS = 256
ROUTE_ROWS = 512
MOVE_ROWS = 256
MOVE_UNROLL = 8
EXPERT_ROWS = 256

C_Q = 0
C_KV = C_Q + Q_RANK
C_KRA = C_KV + KV_RANK
C_KRB = C_KRA + LANES
C_U = C_KRB + LANES
C_V = C_U + GM_OUT
C_END = C_V + GM_OUT
HP = MLA_HEADS * LANES

I_E0, I_E1, I_R0, I_R1, I_G0, I_G1 = range(6)
R_OFF = N_GROUPS


def _rms(v, g):
    return v * lax.rsqrt(jnp.mean(v * v, axis=-1, keepdims=True) + EPS) * g


def _ln(v, g, b):
    mu = jnp.mean(v, axis=-1, keepdims=True)
    d = v - mu
    var = jnp.mean(d * d, axis=-1, keepdims=True)
    return d * lax.rsqrt(var + EPS) * g + b


def _dot(a, b):
    return jnp.dot(a, b, preferred_element_type=F32)


def _prep_kernel(x_ref, pos_ref, win_ref, qg_ref, wq_ref, kvg_ref, wk_ref, wv_ref, inv_ref, sgn_ref,
                 lng_ref, lnb_ref, gavg_ref, ws_ref, bias_ref, gog_ref,
                 q_ref, k_ref, v_ref, g_ref):
    rows = x_ref.shape[1]
    h = _dot(x_ref[0].astype(BF16), win_ref[...])

    ang = pos_ref[0].astype(F32) * inv_ref[...]
    cos_t = jnp.cos(ang)
    sin_t = jnp.sin(ang) * sgn_ref[...]

    cq = _rms(h[:, C_Q:C_Q + Q_RANK], qg_ref[...]).astype(BF16)
    q2 = _dot(cq, wq_ref[...])
    for hd in range(MLA_HEADS):
        lo = hd * LANES
        qh = q2[:, lo:lo + LANES] * cos_t + q2[:, HP + lo:HP + lo + LANES] * sin_t
        q_ref[0, :, lo:lo + LANES] = (qh * SM_SCALE).astype(BF16)

    ckv = _rms(h[:, C_KV:C_KV + KV_RANK], kvg_ref[...]).astype(BF16)
    kp = _dot(ckv, wk_ref[...])
    kr = h[:, C_KRA:C_KRA + LANES] * cos_t + h[:, C_KRB:C_KRB + LANES] * sin_t
    for hd in range(MLA_HEADS):
        lo = hd * LANES
        k_ref[0, :, lo:lo + LANES] = (kp[:, lo:lo + LANES] + kr).astype(BF16)
    v_ref[0] = _dot(ckv, wv_ref[...]).astype(BF16)

    u = jax.nn.gelu(h[:, C_U:C_U + GM_OUT])
    vv = jax.nn.gelu(h[:, C_V:C_V + GM_OUT])
    mu = _dot(vv.astype(BF16), gavg_ref[...])
    d = vv - mu
    var = _dot((d * d).astype(BF16), gavg_ref[...])
    vn = (d * lax.rsqrt(var + EPS) * lng_ref[...] + lnb_ref[...]).astype(BF16)

    tri = lax.broadcasted_iota(jnp.int32, (CHUNK, CHUNK), 0) >= lax.broadcasted_iota(jnp.int32, (CHUNK, CHUNK), 1)
    wm = [jnp.where(tri, ws_ref[g], 0.0).astype(BF16) for g in range(GM_GROUPS)]
    low_half = lax.broadcasted_iota(jnp.int32, (CHUNK, LANES), 1) < GM_CH
    for c in range(rows // CHUNK):
        r0 = c * CHUNK
        parts = []
        for pr in range(GM_GROUPS // 2):
            tile = vn[r0:r0 + CHUNK, pr * LANES:(pr + 1) * LANES]
            parts.append(jnp.where(low_half, _dot(wm[2 * pr], tile), _dot(wm[2 * pr + 1], tile)))
        sg = jnp.concatenate(parts, axis=1) + bias_ref[...]
        gm = u[r0:r0 + CHUNK] * sg
        g_ref[0, r0:r0 + CHUNK, :] = _rms(gm, gog_ref[...]).astype(BF16)


def _prep(x, pos3, w):
    B, S, D = x.shape
    ts = PREP_ROWS
    full = lambda a: pl.BlockSpec(a.shape, lambda b, i: (0,) * a.ndim)
    consts = [w["win"], w["qg"], w["wq"], w["kvg"], w["wk"], w["wv"], w["inv"], w["sgn"],
              w["lng"], w["lnb"], w["gavg"], w["ws"], w["bias"], w["gog"]]
    return pl.pallas_call(
        _prep_kernel,
        grid=(B, S // ts),
        in_specs=[pl.BlockSpec((1, ts, D), lambda b, i: (b, i, 0)),
                  pl.BlockSpec((1, ts, 1), lambda b, i: (b, i, 0))] + [full(a) for a in consts],
        out_specs=[pl.BlockSpec((1, ts, HP), lambda b, i: (b, i, 0)),
                   pl.BlockSpec((1, ts, HP), lambda b, i: (b, i, 0)),
                   pl.BlockSpec((1, ts, HP), lambda b, i: (b, i, 0)),
                   pl.BlockSpec((1, ts, GM_OUT), lambda b, i: (b, i, 0))],
        out_shape=[jax.ShapeDtypeStruct((B, S, HP), BF16)] * 3 + [jax.ShapeDtypeStruct((B, S, GM_OUT), BF16)],
        compiler_params=pltpu.CompilerParams(dimension_semantics=("parallel", "parallel"),
                                             vmem_limit_bytes=VMEM_LIMIT),
        name="prep",
    )(x, pos3, *consts)


def _attn_kernel(q_ref, k_ref, v_ref, g_ref, x_ref, woa_ref, wog_ref, mog_ref, l1g_ref, l1b_ref,
                 o_ref, m_scr, l_scr, acc_scr):
    i = pl.program_id(1)
    tq = q_ref.shape[1]
    tk = tq
    row = lax.broadcasted_iota(jnp.int32, (tq, tk), 0)
    col = lax.broadcasted_iota(jnp.int32, (tq, tk), 1)
    diag_mask = col <= row

    m_scr[...] = jnp.full(m_scr.shape, -1e30, F32)
    l_scr[...] = jnp.zeros(l_scr.shape, F32)
    acc_scr[...] = jnp.zeros(acc_scr.shape, F32)

    def kv_step(j, masked):
        k0 = pl.multiple_of(j * tk, tk)
        for hd in range(MLA_HEADS):
            lo = hd * LANES
            qh = q_ref[0, :, lo:lo + LANES]
            kj = k_ref[0, pl.ds(k0, tk), lo:lo + LANES]
            vj = v_ref[0, pl.ds(k0, tk), lo:lo + LANES]
            s = lax.dot_general(qh, kj, (((1,), (1,)), ((), ())), preferred_element_type=F32)
            if masked:
                s = jnp.where(diag_mask, s, -1e30)
            m_prev = m_scr[hd]
            m_new = jnp.maximum(m_prev, jnp.max(s, axis=-1, keepdims=True))
            p = jnp.exp(s - jnp.concatenate([m_new] * (tk // LANES), axis=1))
            scale = jnp.exp(m_prev - m_new)
            l_scr[hd] = scale * l_scr[hd] + jnp.sum(p, axis=-1, keepdims=True)
            acc_scr[hd] = scale * acc_scr[hd] + _dot(p.astype(BF16), vj)
            m_scr[hd] = m_new

    def full_step(j, c):
        kv_step(j, False)
        return c

    lax.fori_loop(0, i, full_step, 0)
    kv_step(i, True)

    a = jnp.concatenate([acc_scr[hd] / l_scr[hd] for hd in range(MLA_HEADS)], axis=1)
    a = _rms_padded(a, mog_ref[...])
    mix = _dot(a.astype(BF16), woa_ref[...]) + _dot(g_ref[0], wog_ref[...])
    o_ref[0] = _ln(ALPHA * x_ref[0] + mix, l1g_ref[...], l1b_ref[...])


def _rms_padded(v, g):
    ms = jnp.sum(v * v, axis=-1, keepdims=True) * (1.0 / MLA_OUT)
    return v * lax.rsqrt(ms + EPS) * g


def _attn(q, k, v, g, x, w):
    B, S, D = x.shape
    tq = ATTN_ROWS
    full = lambda a: pl.BlockSpec(a.shape, lambda b, i: (0,) * a.ndim)
    consts = [w["woa"], w["wog"], w["mog"], w["l1g"], w["l1b"]]
    return pl.pallas_call(
        _attn_kernel,
        grid=(B, S // tq),
        in_specs=[pl.BlockSpec((1, tq, HP), lambda b, i: (b, i, 0)),
                  pl.BlockSpec((1, S, HP), lambda b, i: (b, 0, 0)),
                  pl.BlockSpec((1, S, HP), lambda b, i: (b, 0, 0)),
                  pl.BlockSpec((1, tq, GM_OUT), lambda b, i: (b, i, 0)),
                  pl.BlockSpec((1, tq, D), lambda b, i: (b, i, 0))] + [full(a) for a in consts],
        out_specs=pl.BlockSpec((1, tq, D), lambda b, i: (b, i, 0)),
        out_shape=jax.ShapeDtypeStruct((B, S, D), F32),
        scratch_shapes=[pltpu.VMEM((MLA_HEADS, tq, LANES), F32)] * 3,
        compiler_params=pltpu.CompilerParams(dimension_semantics=("parallel", "parallel"),
                                             vmem_limit_bytes=VMEM_LIMIT),
        name="attn",
    )(q, k, v, g, x, *consts)


def _route_kernel(x_ref, wr_ref, br_ref, info_ref, cnt_ref, carry_scr, tri_scr):
    step = pl.program_id(0)
    tt = x_ref.shape[0]

    @pl.when(step == 0)
    def _():
        carry_scr[...] = jnp.zeros_like(carry_scr)
        r = lax.broadcasted_iota(jnp.int32, (tt, tt), 0)
        c = lax.broadcasted_iota(jnp.int32, (tt, tt), 1)
        tri_scr[...] = jnp.where(c < r, 1.0, 0.0).astype(BF16)

    x = x_ref[...]
    xh = x.astype(BF16)
    xl = (x - xh.astype(F32)).astype(BF16)
    wr = wr_ref[...]
    wh = wr.astype(BF16)
    wl = (wr - wh.astype(F32)).astype(BF16)
    logits = _dot(xh, wh) + _dot(xl, wh) + _dot(xh, wl) + br_ref[...]

    lane = lax.broadcasted_iota(jnp.int32, (tt, LANES), 1)
    neg = jnp.float32(-jnp.inf)

    is_g = lane < N_GROUPS
    lg = jnp.where(is_g, logits, neg)
    gmax = jnp.max(lg, axis=-1, keepdims=True)
    g_idx = jnp.min(jnp.where(lg == gmax, lane, LANES), axis=-1, keepdims=True)
    g_den = jnp.sum(jnp.where(is_g, jnp.exp(lg - gmax), 0.0), axis=-1, keepdims=True)
    g_p = 1.0 / g_den

    in_grp = (lane >= R_OFF) & (lane < R_OFF + N_EXPERTS) & (((lane - R_OFF) >> 3) == g_idx)
    le = jnp.where(in_grp, logits, neg)
    m1 = jnp.max(le, axis=-1, keepdims=True)
    i1 = jnp.min(jnp.where(le == m1, lane, LANES), axis=-1, keepdims=True)
    le2 = jnp.where(lane == i1, neg, le)
    m2 = jnp.max(le2, axis=-1, keepdims=True)
    i2 = jnp.min(jnp.where(le2 == m2, lane, LANES), axis=-1, keepdims=True)
    e2 = jnp.exp(m2 - m1)
    gate0 = g_p / (1.0 + e2)
    gate1 = g_p * e2 / (1.0 + e2)

    hit1 = lane == i1
    hit2 = lane == i2
    onehot = jnp.where(hit1 | hit2, 1.0, 0.0)
    before = _dot(tri_scr[...], onehot.astype(BF16)) + carry_scr[...]
    r0 = jnp.sum(jnp.where(hit1, before, 0.0), axis=-1, keepdims=True)
    r1 = jnp.sum(jnp.where(hit2, before, 0.0), axis=-1, keepdims=True)
    carry_scr[...] = carry_scr[...] + jnp.sum(onehot, axis=0, keepdims=True)
    cnt_ref[...] = carry_scr[...]

    info = jnp.where(lane == I_E0, (i1 - R_OFF).astype(F32), 0.0)
    info = jnp.where(lane == I_E1, (i2 - R_OFF).astype(F32), info)
    info = jnp.where(lane == I_R0, r0, info)
    info = jnp.where(lane == I_R1, r1, info)
    info = jnp.where(lane == I_G0, gate0, info)
    info = jnp.where(lane == I_G1, gate1, info)
    info_ref[...] = info


def _route(x1, wr, br):
    T, D = x1.shape
    tt = ROUTE_ROWS
    return pl.pallas_call(
        _route_kernel,
        grid=(T // tt,),
        in_specs=[pl.BlockSpec((tt, D), lambda i: (i, 0)),
                  pl.BlockSpec(wr.shape, lambda i: (0, 0)),
                  pl.BlockSpec(br.shape, lambda i: (0, 0))],
        out_specs=[pl.BlockSpec((tt, LANES), lambda i: (i, 0)),
                   pl.BlockSpec((1, LANES), lambda i: (0, 0))],
        out_shape=[jax.ShapeDtypeStruct((T, LANES), F32), jax.ShapeDtypeStruct((1, LANES), F32)],
        scratch_shapes=[pltpu.VMEM((1, LANES), F32), pltpu.VMEM((tt, tt), BF16)],
        compiler_params=pltpu.CompilerParams(dimension_semantics=("arbitrary",), vmem_limit_bytes=VMEM_LIMIT),
        name="route",
    )(x1, wr, br)


def _to_token_tiles(dst_ref, val):
    rows = val.shape[0]
    for c in range(SUBLANES):
        dst_ref[pl.ds(c, rows, stride=SUBLANES), :] = val[:, c * LANES:(c + 1) * LANES]


def _from_token_tiles(src_ref, rows):
    return jnp.concatenate([src_ref[pl.ds(c, rows, stride=SUBLANES), :] for c in range(SUBLANES)], axis=1)


def _tile_copy(src_ref, src_row, dst_ref, dst_row, sem):
    return pltpu.make_async_copy(src_ref.at[pl.ds(pl.multiple_of(src_row, SUBLANES), SUBLANES)],
                                 dst_ref.at[pl.ds(pl.multiple_of(dst_row, SUBLANES), SUBLANES)], sem)


def _dispatch_kernel(dest_ref, x_ref, zero_ref, buf_ref, stage_scr, sem, *, n_steps):
    del zero_ref
    i = pl.program_id(0)
    rows = x_ref.shape[0]
    slot = i % 2

    def drain(s):
        for _ in range(TOP_K):
            pltpu.make_async_copy(stage_scr.at[s], stage_scr.at[s], sem.at[s]).wait()

    @pl.when(i >= 2)
    def _():
        drain(slot)

    _to_token_tiles(stage_scr.at[slot], x_ref[...])

    def start(c, carry):
        for u in range(MOVE_UNROLL):
            r = c * MOVE_UNROLL + u
            for kk in range(TOP_K):
                _tile_copy(stage_scr.at[slot], r * SUBLANES, buf_ref, dest_ref[0, 0, TOP_K * r + kk],
                           sem.at[slot]).start(priority=kk)
        return carry

    lax.fori_loop(0, rows // MOVE_UNROLL, start, 0)

    @pl.when(i == n_steps - 1)
    def _():
        drain(slot)
        if n_steps >= 2:
            drain(1 - slot)


def _dispatch(dest3, x1, n_rows):
    T, D = x1.shape
    td = MOVE_ROWS
    n_steps = T // td
    zeros = jnp.zeros((n_rows * SUBLANES, LANES), F32)
    return pl.pallas_call(
        functools.partial(_dispatch_kernel, n_steps=n_steps),
        grid=(n_steps,),
        in_specs=[pl.BlockSpec((1, 1, TOP_K * td), lambda i: (i, 0, 0), memory_space=pltpu.SMEM),
                  pl.BlockSpec((td, D), lambda i: (i, 0)),
                  pl.BlockSpec(memory_space=pl.ANY)],
        out_specs=pl.BlockSpec(memory_space=pl.ANY),
        out_shape=jax.ShapeDtypeStruct((n_rows * SUBLANES, LANES), F32),
        scratch_shapes=[pltpu.VMEM((2, td * SUBLANES, LANES), F32), pltpu.SemaphoreType.DMA((2,))],
        input_output_aliases={2: 0},
        compiler_params=pltpu.CompilerParams(dimension_semantics=("arbitrary",), vmem_limit_bytes=VMEM_LIMIT),
        name="dispatch",
    )(dest3, x1, zeros)


def _expert_kernel(be_ref, buf_ref, wg_ref, wu_ref, wd_ref, y_ref, wg_scr, wu_scr, wd_scr):
    b = pl.program_id(0)
    e = be_ref[b]
    prev = be_ref[jnp.maximum(b - 1, 0)]

    @pl.when((b == 0) | (e != prev))
    def _():
        wg_scr[...] = wg_ref[0].astype(BF16)
        wu_scr[...] = wu_ref[0].astype(BF16)
        wd_scr[...] = wd_ref[0].astype(BF16)

    xb = _from_token_tiles(buf_ref, EXPERT_ROWS).astype(BF16)
    hidden = jax.nn.silu(_dot(xb, wg_scr[...])) * _dot(xb, wu_scr[...])
    _to_token_tiles(y_ref, _dot(hidden.astype(BF16), wd_scr[...]))


def _experts(block_expert, buf, w_gate, w_up, w_down):
    bm = EXPERT_ROWS
    D, ff = w_gate.shape[1:]
    n_rows = buf.shape[0] // SUBLANES
    grid_spec = pltpu.PrefetchScalarGridSpec(
        num_scalar_prefetch=1,
        grid=(n_rows // bm,),
        in_specs=[pl.BlockSpec((bm * SUBLANES, LANES), lambda b, be: (b, 0)),
                  pl.BlockSpec((1, D, ff), lambda b, be: (be[b], 0, 0)),
                  pl.BlockSpec((1, D, ff), lambda b, be: (be[b], 0, 0)),
                  pl.BlockSpec((1, ff, D), lambda b, be: (be[b], 0, 0))],
        out_specs=pl.BlockSpec((bm * SUBLANES, LANES), lambda b, be: (b, 0)),
        scratch_shapes=[pltpu.VMEM((D, ff), BF16), pltpu.VMEM((D, ff), BF16), pltpu.VMEM((ff, D), BF16)],
    )
    return pl.pallas_call(
        _expert_kernel,
        grid_spec=grid_spec,
        out_shape=jax.ShapeDtypeStruct(buf.shape, F32),
        compiler_params=pltpu.CompilerParams(dimension_semantics=("arbitrary",), vmem_limit_bytes=VMEM_LIMIT),
        name="experts",
    )(block_expert, buf, w_gate, w_up, w_down)


def _final_kernel(dcur_ref, dnxt_ref, x_ref, info_ref, y_ref, p_ref, wpg_ref, bpg_ref, wpp_ref,
                  l2g_ref, l2b_ref, l3g_ref, l3b_ref, o_ref, rows_scr, sem):
    i = pl.program_id(0)
    rows = x_ref.shape[0]
    slot = i % 2

    def gather(dref, s):
        def start(c, carry):
            for u in range(MOVE_UNROLL):
                r = c * MOVE_UNROLL + u
                for kk in range(TOP_K):
                    _tile_copy(y_ref, dref[0, 0, TOP_K * r + kk], rows_scr.at[s, kk], r * SUBLANES,
                               sem.at[s]).start(priority=kk)
            return carry

        lax.fori_loop(0, rows // MOVE_UNROLL, start, 0)

    @pl.when(i == 0)
    def _():
        gather(dcur_ref, 0)

    @pl.when(i + 1 < pl.num_programs(0))
    def _():
        gather(dnxt_ref, 1 - slot)

    pp = _dot(p_ref[...].astype(BF16), wpp_ref[...])
    pltpu.make_async_copy(rows_scr.at[slot], rows_scr.at[slot], sem.at[slot]).wait()

    info = info_ref[...]
    gate0 = info[:, I_G0:I_G0 + 1]
    gate1 = info[:, I_G1:I_G1 + 1]
    moe = (_from_token_tiles(rows_scr.at[slot, 0], rows) * gate0
           + _from_token_tiles(rows_scr.at[slot, 1], rows) * gate1)
    x2 = _ln(ALPHA * x_ref[...] + moe, l2g_ref[...], l2b_ref[...])
    gate = jax.nn.sigmoid(_dot(x2.astype(BF16), wpg_ref[...]) + bpg_ref[...])
    o_ref[...] = _ln(ALPHA * x2 + gate * pp, l3g_ref[...], l3b_ref[...])


def _final(dest3, x1, info, y, p2, w):
    T, D = x1.shape
    tc = MOVE_ROWS
    pd = p2.shape[1]
    full = lambda a: pl.BlockSpec(a.shape, lambda i: (0,) * a.ndim)
    consts = [w["wpg"], w["bpg"], w["wpp"], w["l2g"], w["l2b"], w["l3g"], w["l3b"]]
    last = T // tc - 1
    return pl.pallas_call(
        _final_kernel,
        grid=(T // tc,),
        in_specs=[pl.BlockSpec((1, 1, TOP_K * tc), lambda i: (i, 0, 0), memory_space=pltpu.SMEM),
                  pl.BlockSpec((1, 1, TOP_K * tc), lambda i: (jnp.minimum(i + 1, last), 0, 0), memory_space=pltpu.SMEM),
                  pl.BlockSpec((tc, D), lambda i: (i, 0)),
                  pl.BlockSpec((tc, LANES), lambda i: (i, 0)),
                  pl.BlockSpec(memory_space=pl.ANY),
                  pl.BlockSpec((tc, pd), lambda i: (i, 0))] + [full(a) for a in consts],
        out_specs=pl.BlockSpec((tc, D), lambda i: (i, 0)),
        out_shape=jax.ShapeDtypeStruct((T, D), F32),
        scratch_shapes=[pltpu.VMEM((2, TOP_K, tc * SUBLANES, LANES), F32), pltpu.SemaphoreType.DMA((2,))],
        compiler_params=pltpu.CompilerParams(dimension_semantics=("arbitrary",), vmem_limit_bytes=VMEM_LIMIT),
        name="final",
    )(dest3, dest3, x1, info, y, p2, *consts)


def _pad_heads(a, width):
    lead = a.shape[:-1]
    a = a.reshape(lead + (MLA_HEADS, width))
    a = jnp.pad(a, [(0, 0)] * len(lead) + [(0, 0), (0, LANES - width)])
    return a.reshape(lead + (HP,))


def _layer_weights(w_in, q_norm_g, w_q_up, kv_norm_g, w_kv_up, gm_ln_g, gm_ln_b, gm_w_s, gm_b_s,
                   mla_out_g, gm_out_g, w_o, ln1_g, ln1_b):
    D = w_in.shape[0]
    half = QK_ROPE // 2
    c1, c2, c3 = Q_RANK, Q_RANK + KV_RANK, Q_RANK + KV_RANK + QK_ROPE
    wkr = w_in[:, c2:c3]
    zeros = lambda *s: jnp.zeros(s, F32)
    kra = jnp.concatenate([zeros(D, QK_NOPE), wkr, zeros(D, LANES - QK_NOPE - QK_ROPE)], axis=1)
    krb = jnp.concatenate([zeros(D, QK_NOPE), wkr[:, half:], wkr[:, :half], zeros(D, LANES - QK_NOPE - QK_ROPE)], axis=1)
    win = jnp.concatenate([w_in[:, :c2], kra, krb, w_in[:, c3:]], axis=1).astype(BF16)

    wq3 = w_q_up.reshape(Q_RANK, MLA_HEADS, QK_NOPE + QK_ROPE)
    rope = wq3[..., QK_NOPE:]
    rope_sw = jnp.concatenate([rope[..., half:], rope[..., :half]], axis=-1)
    sw3 = jnp.concatenate([jnp.zeros_like(wq3[..., :QK_NOPE]), rope_sw], axis=-1)
    wq = jnp.concatenate([_pad_heads(w_q_up, QK_NOPE + QK_ROPE),
                          _pad_heads(sw3.reshape(Q_RANK, -1), QK_NOPE + QK_ROPE)], axis=1).astype(BF16)

    wkv3 = w_kv_up.reshape(KV_RANK, MLA_HEADS, QK_NOPE + V_HEAD)
    wk = _pad_heads(wkv3[..., :QK_NOPE].reshape(KV_RANK, -1), QK_NOPE).astype(BF16)
    wv = _pad_heads(wkv3[..., QK_NOPE:].reshape(KV_RANK, -1), V_HEAD).astype(BF16)

    inv = ROPE_THETA ** (-jnp.arange(0, QK_ROPE, 2, dtype=F32) / QK_ROPE)
    pad_l, pad_r = jnp.zeros((QK_NOPE,), F32), jnp.zeros((LANES - QK_NOPE - QK_ROPE,), F32)
    inv_t = jnp.concatenate([pad_l, inv, inv, pad_r])[None, :]
    sgn_t = jnp.concatenate([pad_l, -jnp.ones((half,), F32), jnp.ones((half,), F32), pad_r])[None, :]

    grp = jnp.arange(GM_OUT) // GM_CH
    gavg = jnp.where(grp[:, None] == grp[None, :], 1.0 / GM_CH, 0.0).astype(BF16)
    bias = jnp.repeat(gm_b_s.T, GM_CH, axis=1)

    woa = _pad_heads(w_o[:MLA_OUT].T, V_HEAD).T.astype(BF16)
    wog = w_o[MLA_OUT:].astype(BF16)
    return dict(win=win, qg=q_norm_g[None, :], wq=wq, kvg=kv_norm_g[None, :], wk=wk, wv=wv, inv=inv_t, sgn=sgn_t,
                lng=gm_ln_g[None, :], lnb=gm_ln_b[None, :], gavg=gavg, ws=gm_w_s, bias=bias, gog=gm_out_g[None, :],
                woa=woa, wog=wog, mog=_pad_heads(mla_out_g, V_HEAD)[None, :], l1g=ln1_g[None, :], l1b=ln1_b[None, :])


def _moe(x1, w_rg, b_rg, w_re, b_re, w_gate, w_up, w_down):
    T, D = x1.shape
    pad = jnp.zeros((D, LANES - N_GROUPS - N_EXPERTS), F32)
    wr = jnp.concatenate([w_rg, w_re, pad], axis=1)
    br = jnp.concatenate([b_rg, b_re, pad[0]])[None, :]
    info, cnt = _route(x1, wr, br)

    bm = EXPERT_ROWS
    n_blocks = (T * TOP_K) // bm + N_EXPERTS
    counts = cnt[0, R_OFF:R_OFF + N_EXPERTS].astype(jnp.int32)
    padded = (counts + bm - 1) // bm * bm
    pad_ends = jnp.cumsum(padded)
    pad_starts = pad_ends - padded
    e_idx = info[:, I_E0:I_E1 + 1].astype(jnp.int32)
    rank = info[:, I_R0:I_R1 + 1].astype(jnp.int32)
    seg_start = jnp.sum(jnp.where(e_idx[..., None] == jnp.arange(N_EXPERTS), pad_starts, 0), axis=-1)
    dest = ((seg_start + rank) * SUBLANES).reshape(T // MOVE_ROWS, 1, TOP_K * MOVE_ROWS)
    block_start = jnp.arange(n_blocks, dtype=jnp.int32) * bm
    block_expert = jnp.minimum(jnp.sum(pad_ends[None, :] <= block_start[:, None], axis=1),
                               N_EXPERTS - 1).astype(jnp.int32)

    buf = _dispatch(dest, x1, n_blocks * bm)
    y = _experts(block_expert, buf, w_gate, w_up, w_down)
    return info, dest, y


def kernel(x, p, positions, w_in, q_norm_g, w_q_up, kv_norm_g, w_kv_up, gm_ln_g, gm_ln_b, gm_w_s, gm_b_s, mla_out_g, gm_out_g, w_o, ln1_g, ln1_b, w_rg, b_rg, w_re, b_re, w_gate, w_up, w_down, ln2_g, ln2_b, w_pg, b_pg, w_pp, ln3_g, ln3_b):
    B, S, D = x.shape
    T = B * S
    assert S % ATTN_ROWS == 0 and S % PREP_ROWS == 0 and PREP_ROWS % CHUNK == 0
    assert T % ROUTE_ROWS == 0 and T % MOVE_ROWS == 0 and (T * TOP_K) % EXPERT_ROWS == 0
    assert D == SUBLANES * LANES and MOVE_ROWS % MOVE_UNROLL == 0
    pos3 = positions.reshape(B, S, 1)
    for i in range(DEPTH):
        w = _layer_weights(w_in[i], q_norm_g[i], w_q_up[i], kv_norm_g[i], w_kv_up[i], gm_ln_g[i], gm_ln_b[i],
                           gm_w_s[i], gm_b_s[i], mla_out_g[i], gm_out_g[i], w_o[i], ln1_g[i], ln1_b[i])
        q, k, v, g = _prep(x, pos3, w)
        x1 = _attn(q, k, v, g, x, w).reshape(T, D)
        info, dest, y = _moe(x1, w_rg[i], b_rg[i], w_re[i], b_re[i], w_gate[i], w_up[i], w_down[i])
        wf = dict(wpg=w_pg[i].astype(BF16), bpg=b_pg[i][None, :], wpp=w_pp[i].astype(BF16),
                  l2g=ln2_g[i][None, :], l2b=ln2_b[i][None, :], l3g=ln3_g[i][None, :], l3b=ln3_b[i][None, :])
        x = _final(dest, x1, info, y, p[i].reshape(T, -1), wf).reshape(B, S, D)
    return x
```

```python
import functools

import jax
import jax.numpy as jnp
from jax import lax
from jax.experimental import pallas as pl
from jax.experimental.pallas import tpu as pltpu

F32 = jnp.float32
BF16 = jnp.bfloat16

MLA_HEADS = 8
QK_NOPE = 64
QK_ROPE = 32
V_HEAD = 64
Q_RANK = 256
KV_RANK = 128
ROPE_THETA = 10000.0
MLA_OUT = MLA_HEADS * V_HEAD
GM_GROUPS = 8
GM_CH = 64
GM_OUT = GM_GROUPS * GM_CH
CHUNK = 128
N_GROUPS = 4
EXP_PER_GROUP = 8
N_EXPERTS = N_GROUPS * EXP_PER_GROUP
TOP_K = 2
EPS = 1e-6
DEPTH = 1
ALPHA = (2.0 * DEPTH) ** 0.25
SM_SCALE = (QK_NOPE + QK_ROPE) ** -0.5
LOG2E = 1.4426950408889634

LANES = 128
SUBLANES = 8
VMEM_LIMIT = 56 * 1024 * 1024

PREP_ROWS = 256
ATTN_ROWS = 256
ROUTE_ROWS = 512
MOVE_ROWS = 256
MOVE_UNROLL = 8
EXPERT_ROWS = 256

C_Q = 0
C_KV = C_Q + Q_RANK
C_KRA = C_KV + KV_RANK
C_KRB = C_KRA + LANES
C_U = C_KRB + LANES
C_V = C_U + GM_OUT
C_END = C_V + GM_OUT
HP = MLA_HEADS * LANES

I_E0, I_E1, I_R0, I_R1, I_G0, I_G1 = range(6)
R_OFF = N_GROUPS


def _rms(v, g):
    return v * lax.rsqrt(jnp.mean(v * v, axis=-1, keepdims=True) + EPS) * g


def _ln(v, g, b):
    mu = jnp.mean(v, axis=-1, keepdims=True)
    d = v - mu
    var = jnp.mean(d * d, axis=-1, keepdims=True)
    return d * lax.rsqrt(var + EPS) * g + b


def _dot(a, b):
    return jnp.dot(a, b, preferred_element_type=F32)


def _prep_kernel(x_ref, pos_ref, win_ref, qg_ref, wq_ref, kvg_ref, wk_ref, wv_ref, inv_ref, sgn_ref,
                 lng_ref, lnb_ref, gavg_ref, ws_ref, bias_ref, gog_ref,
                 q_ref, k_ref, vt_ref, g_ref):
    rows = x_ref.shape[1]
    h = _dot(x_ref[0].astype(BF16), win_ref[...])

    ang = pos_ref[0].astype(F32) * inv_ref[...]
    cos_t = jnp.cos(ang)
    sin_t = jnp.sin(ang) * sgn_ref[...]

    cq = _rms(h[:, C_Q:C_Q + Q_RANK], qg_ref[...]).astype(BF16)
    q2 = _dot(cq, wq_ref[...])
    for hd in range(MLA_HEADS):
        lo = hd * LANES
        qh = q2[:, lo:lo + LANES] * cos_t + q2[:, HP + lo:HP + lo + LANES] * sin_t
        q_ref[0, :, lo:lo + LANES] = (qh * (SM_SCALE * LOG2E)).astype(BF16)

    ckv = _rms(h[:, C_KV:C_KV + KV_RANK], kvg_ref[...]).astype(BF16)
    kp = _dot(ckv, wk_ref[...])
    kr = h[:, C_KRA:C_KRA + LANES] * cos_t + h[:, C_KRB:C_KRB + LANES] * sin_t
    for hd in range(MLA_HEADS):
        lo = hd * LANES
        k_ref[0, :, lo:lo + LANES] = (kp[:, lo:lo + LANES] + kr).astype(BF16)
    vt_ref[0, 0] = _dot(ckv, wv_ref[...]).T.astype(BF16)

    u = jax.nn.gelu(h[:, C_U:C_U + GM_OUT])
    vv = jax.nn.gelu(h[:, C_V:C_V + GM_OUT])
    mu = _dot(vv.astype(BF16), gavg_ref[...])
    d = vv - mu
    var = _dot((d * d).astype(BF16), gavg_ref[...])
    vn = (d * lax.rsqrt(var + EPS) * lng_ref[...] + lnb_ref[...]).astype(BF16)

    tri = lax.broadcasted_iota(jnp.int32, (CHUNK, CHUNK), 0) >= lax.broadcasted_iota(jnp.int32, (CHUNK, CHUNK), 1)
    wm = [jnp.where(tri, ws_ref[g], 0.0).astype(BF16) for g in range(GM_GROUPS)]
    low_half = lax.broadcasted_iota(jnp.int32, (CHUNK, LANES), 1) < GM_CH
    for c in range(rows // CHUNK):
        r0 = c * CHUNK
        parts = []
        for pr in range(GM_GROUPS // 2):
            tile = vn[r0:r0 + CHUNK, pr * LANES:(pr + 1) * LANES]
            parts.append(jnp.where(low_half, _dot(wm[2 * pr], tile), _dot(wm[2 * pr + 1], tile)))
        sg = jnp.concatenate(parts, axis=1) + bias_ref[...]
        gm = u[r0:r0 + CHUNK] * sg
        g_ref[0, r0:r0 + CHUNK, :] = _rms(gm, gog_ref[...]).astype(BF16)


def _prep(x, pos3, w):
    B, S, D = x.shape
    ts = PREP_ROWS
    full = lambda a: pl.BlockSpec(a.shape, lambda b, i: (0,) * a.ndim)
    consts = [w["win"], w["qg"], w["wq"], w["kvg"], w["wk"], w["wv"], w["inv"], w["sgn"],
              w["lng"], w["lnb"], w["gavg"], w["ws"], w["bias"], w["gog"]]
    return pl.pallas_call(
        _prep_kernel,
        grid=(B, S // ts),
        in_specs=[pl.BlockSpec((1, ts, D), lambda b, i: (b, i, 0)),
                  pl.BlockSpec((1, ts, 1), lambda b, i: (b, i, 0))] + [full(a) for a in consts],
        out_specs=[pl.BlockSpec((1, ts, HP), lambda b, i: (b, i, 0)),
                   pl.BlockSpec((1, ts, HP), lambda b, i: (b, i, 0)),
                   pl.BlockSpec((1, 1, MLA_OUT, ts), lambda b, i: (b, i, 0, 0)),
                   pl.BlockSpec((1, ts, GM_OUT), lambda b, i: (b, i, 0))],
        out_shape=[jax.ShapeDtypeStruct((B, S, HP), BF16)] * 2
        + [jax.ShapeDtypeStruct((B, S // ts, MLA_OUT, ts), BF16), jax.ShapeDtypeStruct((B, S, GM_OUT), BF16)],
        compiler_params=pltpu.CompilerParams(dimension_semantics=("parallel", "parallel"),
                                             vmem_limit_bytes=VMEM_LIMIT),
        name="prep",
    )(x, pos3, *consts)


def _attn_kernel(q_ref, k_ref, vt_ref, g_ref, x_ref, woa_ref, wog_ref, mog_ref, l1g_ref, l1b_ref,
                 o_ref, m_scr, l_scr, acc_scr):
    i = pl.program_id(1)
    tq = q_ref.shape[1]
    tk = tq
    key = lax.broadcasted_iota(jnp.int32, (tk, tq), 0)
    qry = lax.broadcasted_iota(jnp.int32, (tk, tq), 1)
    diag_mask = key <= qry

    m_scr[...] = jnp.full(m_scr.shape, -1e30, F32)
    l_scr[...] = jnp.zeros(l_scr.shape, F32)
    acc_scr[...] = jnp.zeros(acc_scr.shape, F32)

    def kv_step(j, masked):
        k0 = pl.multiple_of(j * tk, tk)
        scores = []
        for hd in range(MLA_HEADS):
            lo = hd * LANES
            qh = q_ref[0, :, lo:lo + LANES]
            kj = k_ref[0, pl.ds(k0, tk), lo:lo + LANES]
            scores.append(lax.dot_general(kj, qh, (((1,), (1,)), ((), ())), preferred_element_type=F32))
        for hd in range(MLA_HEADS):
            s = scores[hd]
            vt = vt_ref[0, j, hd * V_HEAD:(hd + 1) * V_HEAD, :]
            if masked:
                s = jnp.where(diag_mask, s, -1e30)
            m_prev = m_scr[hd]
            m_new = jnp.maximum(m_prev, jnp.max(s, axis=0, keepdims=True))
            p = jnp.exp2(s - m_new)
            scale = jnp.exp2(m_prev - m_new)
            l_scr[hd] = scale * l_scr[hd] + jnp.sum(p, axis=0, keepdims=True)
            acc_scr[hd] = scale * acc_scr[hd] + _dot(vt, p.astype(BF16))
            m_scr[hd] = m_new

    def full_step(j, c):
        kv_step(j, False)
        return c

    lax.fori_loop(0, i, full_step, 0)
    kv_step(i, True)

    at = jnp.concatenate([acc_scr[hd] / l_scr[hd] for hd in range(MLA_HEADS)], axis=0)
    at = at * lax.rsqrt(jnp.mean(at * at, axis=0, keepdims=True) + EPS) * mog_ref[...]
    mix = _dot(at.T.astype(BF16), woa_ref[...]) + _dot(g_ref[0], wog_ref[...])
    o_ref[0] = _ln(ALPHA * x_ref[0] + mix, l1g_ref[...], l1b_ref[...])


def _attn(q, k, vt, g, x, w):
    B, S, D = x.shape
    tq = ATTN_ROWS
    full = lambda a: pl.BlockSpec(a.shape, lambda b, i: (0,) * a.ndim)
    consts = [w["woa"], w["wog"], w["mog"], w["l1g"], w["l1b"]]
    return pl.pallas_call(
        _attn_kernel,
        grid=(B, S // tq),
        in_specs=[pl.BlockSpec((1, tq, HP), lambda b, i: (b, i, 0)),
                  pl.BlockSpec((1, S, HP), lambda b, i: (b, 0, 0)),
                  pl.BlockSpec((1,) + vt.shape[1:], lambda b, i: (b, 0, 0, 0)),
                  pl.BlockSpec((1, tq, GM_OUT), lambda b, i: (b, i, 0)),
                  pl.BlockSpec((1, tq, D), lambda b, i: (b, i, 0))] + [full(a) for a in consts],
        out_specs=pl.BlockSpec((1, tq, D), lambda b, i: (b, i, 0)),
        out_shape=jax.ShapeDtypeStruct((B, S, D), F32),
        scratch_shapes=[pltpu.VMEM((MLA_HEADS, 1, tq), F32), pltpu.VMEM((MLA_HEADS, 1, tq), F32),
                        pltpu.VMEM((MLA_HEADS, V_HEAD, tq), F32)],
        compiler_params=pltpu.CompilerParams(dimension_semantics=("parallel", "parallel"),
                                             vmem_limit_bytes=VMEM_LIMIT),
        name="attn",
    )(q, k, vt, g, x, *consts)


def _route_kernel(x_ref, wr_ref, br_ref, info_ref, cnt_ref, carry_scr, tri_scr):
    step = pl.program_id(0)
    tt = x_ref.shape[0]

    @pl.when(step == 0)
    def _():
        carry_scr[...] = jnp.zeros_like(carry_scr)
        r = lax.broadcasted_iota(jnp.int32, (tt, tt), 0)
        c = lax.broadcasted_iota(jnp.int32, (tt, tt), 1)
        tri_scr[...] = jnp.where(c < r, 1.0, 0.0).astype(BF16)

    x = x_ref[...]
    xh = x.astype(BF16)
    xl = (x - xh.astype(F32)).astype(BF16)
    wr = wr_ref[...]
    wh = wr.astype(BF16)
    wl = (wr - wh.astype(F32)).astype(BF16)
    logits = _dot(xh, wh) + _dot(xl, wh) + _dot(xh, wl) + br_ref[...]

    lane = lax.broadcasted_iota(jnp.int32, (tt, LANES), 1)
    neg = jnp.float32(-jnp.inf)

    is_g = lane < N_GROUPS
    lg = jnp.where(is_g, logits, neg)
    gmax = jnp.max(lg, axis=-1, keepdims=True)
    g_idx = jnp.min(jnp.where(lg == gmax, lane, LANES), axis=-1, keepdims=True)
    g_den = jnp.sum(jnp.where(is_g, jnp.exp(lg - gmax), 0.0), axis=-1, keepdims=True)
    g_p = 1.0 / g_den

    in_grp = (lane >= R_OFF) & (lane < R_OFF + N_EXPERTS) & (((lane - R_OFF) >> 3) == g_idx)
    le = jnp.where(in_grp, logits, neg)
    m1 = jnp.max(le, axis=-1, keepdims=True)
    i1 = jnp.min(jnp.where(le == m1, lane, LANES), axis=-1, keepdims=True)
    le2 = jnp.where(lane == i1, neg, le)
    m2 = jnp.max(le2, axis=-1, keepdims=True)
    i2 = jnp.min(jnp.where(le2 == m2, lane, LANES), axis=-1, keepdims=True)
    e2 = jnp.exp(m2 - m1)
    gate0 = g_p / (1.0 + e2)
    gate1 = g_p * e2 / (1.0 + e2)

    hit1 = lane == i1
    hit2 = lane == i2
    onehot = jnp.where(hit1 | hit2, 1.0, 0.0)
    before = _dot(tri_scr[...], onehot.astype(BF16)) + carry_scr[...]
    r0 = jnp.sum(jnp.where(hit1, before, 0.0), axis=-1, keepdims=True)
    r1 = jnp.sum(jnp.where(hit2, before, 0.0), axis=-1, keepdims=True)
    carry_scr[...] = carry_scr[...] + jnp.sum(onehot, axis=0, keepdims=True)
    cnt_ref[...] = carry_scr[...]

    info = jnp.where(lane == I_E0, (i1 - R_OFF).astype(F32), 0.0)
    info = jnp.where(lane == I_E1, (i2 - R_OFF).astype(F32), info)
    info = jnp.where(lane == I_R0, r0, info)
    info = jnp.where(lane == I_R1, r1, info)
    info = jnp.where(lane == I_G0, gate0, info)
    info = jnp.where(lane == I_G1, gate1, info)
    info_ref[...] = info


def _route(x1, wr, br):
    T, D = x1.shape
    tt = ROUTE_ROWS
    return pl.pallas_call(
        _route_kernel,
        grid=(T // tt,),
        in_specs=[pl.BlockSpec((tt, D), lambda i: (i, 0)),
                  pl.BlockSpec(wr.shape, lambda i: (0, 0)),
                  pl.BlockSpec(br.shape, lambda i: (0, 0))],
        out_specs=[pl.BlockSpec((tt, LANES), lambda i: (i, 0)),
                   pl.BlockSpec((1, LANES), lambda i: (0, 0))],
        out_shape=[jax.ShapeDtypeStruct((T, LANES), F32), jax.ShapeDtypeStruct((1, LANES), F32)],
        scratch_shapes=[pltpu.VMEM((1, LANES), F32), pltpu.VMEM((tt, tt), BF16)],
        compiler_params=pltpu.CompilerParams(dimension_semantics=("arbitrary",), vmem_limit_bytes=VMEM_LIMIT),
        name="route",
    )(x1, wr, br)


def _to_token_tiles(dst_ref, val):
    rows = val.shape[0]
    for c in range(SUBLANES):
        dst_ref[pl.ds(c, rows, stride=SUBLANES), :] = val[:, c * LANES:(c + 1) * LANES]


def _from_token_tiles(src_ref, rows):
    return jnp.concatenate([src_ref[pl.ds(c, rows, stride=SUBLANES), :] for c in range(SUBLANES)], axis=1)


def _tile_copy(src_ref, src_row, dst_ref, dst_row, sem):
    return pltpu.make_async_copy(src_ref.at[pl.ds(pl.multiple_of(src_row, SUBLANES), SUBLANES)],
                                 dst_ref.at[pl.ds(pl.multiple_of(dst_row, SUBLANES), SUBLANES)], sem)


def _dispatch_kernel(dest_ref, x_ref, zero_ref, buf_ref, stage_scr, sem, *, n_steps):
    del zero_ref
    i = pl.program_id(0)
    rows = x_ref.shape[0]
    slot = i % 2

    def drain(s):
        for _ in range(TOP_K):
            pltpu.make_async_copy(stage_scr.at[s], stage_scr.at[s], sem.at[s]).wait()

    @pl.when(i >= 2)
    def _():
        drain(slot)

    _to_token_tiles(stage_scr.at[slot], x_ref[...])

    def start(c, carry):
        for u in range(MOVE_UNROLL):
            r = c * MOVE_UNROLL + u
            for kk in range(TOP_K):
                _tile_copy(stage_scr.at[slot], r * SUBLANES, buf_ref, dest_ref[0, 0, TOP_K * r + kk],
                           sem.at[slot]).start(priority=kk)
        return carry

    lax.fori_loop(0, rows // MOVE_UNROLL, start, 0)

    @pl.when(i == n_steps - 1)
    def _():
        drain(slot)
        if n_steps >= 2:
            drain(1 - slot)


def _dispatch(dest3, x1, n_rows):
    T, D = x1.shape
    td = MOVE_ROWS
    n_steps = T // td
    zeros = jnp.zeros((n_rows * SUBLANES, LANES), F32)
    return pl.pallas_call(
        functools.partial(_dispatch_kernel, n_steps=n_steps),
        grid=(n_steps,),
        in_specs=[pl.BlockSpec((1, 1, TOP_K * td), lambda i: (i, 0, 0), memory_space=pltpu.SMEM),
                  pl.BlockSpec((td, D), lambda i: (i, 0)),
                  pl.BlockSpec(memory_space=pl.ANY)],
        out_specs=pl.BlockSpec(memory_space=pl.ANY),
        out_shape=jax.ShapeDtypeStruct((n_rows * SUBLANES, LANES), F32),
        scratch_shapes=[pltpu.VMEM((2, td * SUBLANES, LANES), F32), pltpu.SemaphoreType.DMA((2,))],
        input_output_aliases={2: 0},
        compiler_params=pltpu.CompilerParams(dimension_semantics=("arbitrary",), vmem_limit_bytes=VMEM_LIMIT),
        name="dispatch",
    )(dest3, x1, zeros)


def _expert_kernel(be_ref, buf_ref, wg_ref, wu_ref, wd_ref, y_ref, wg_scr, wu_scr, wd_scr):
    b = pl.program_id(0)
    e = be_ref[b]
    prev = be_ref[jnp.maximum(b - 1, 0)]

    @pl.when((b == 0) | (e != prev))
    def _():
        wg_scr[...] = wg_ref[0].astype(BF16)
        wu_scr[...] = wu_ref[0].astype(BF16)
        wd_scr[...] = wd_ref[0].astype(BF16)

    xb = _from_token_tiles(buf_ref, EXPERT_ROWS).astype(BF16)
    hidden = jax.nn.silu(_dot(xb, wg_scr[...])) * _dot(xb, wu_scr[...])
    _to_token_tiles(y_ref, _dot(hidden.astype(BF16), wd_scr[...]))


def _experts(block_expert, buf, w_gate, w_up, w_down):
    bm = EXPERT_ROWS
    D, ff = w_gate.shape[1:]
    n_rows = buf.shape[0] // SUBLANES
    grid_spec = pltpu.PrefetchScalarGridSpec(
        num_scalar_prefetch=1,
        grid=(n_rows // bm,),
        in_specs=[pl.BlockSpec((bm * SUBLANES, LANES), lambda b, be: (b, 0)),
                  pl.BlockSpec((1, D, ff), lambda b, be: (be[b], 0, 0)),
                  pl.BlockSpec((1, D, ff), lambda b, be: (be[b], 0, 0)),
                  pl.BlockSpec((1, ff, D), lambda b, be: (be[b], 0, 0))],
        out_specs=pl.BlockSpec((bm * SUBLANES, LANES), lambda b, be: (b, 0)),
        scratch_shapes=[pltpu.VMEM((D, ff), BF16), pltpu.VMEM((D, ff), BF16), pltpu.VMEM((ff, D), BF16)],
    )
    return pl.pallas_call(
        _expert_kernel,
        grid_spec=grid_spec,
        out_shape=jax.ShapeDtypeStruct(buf.shape, F32),
        compiler_params=pltpu.CompilerParams(dimension_semantics=("arbitrary",), vmem_limit_bytes=VMEM_LIMIT),
        name="experts",
    )(block_expert, buf, w_gate, w_up, w_down)


def _final_kernel(dcur_ref, dnxt_ref, x_ref, info_ref, y_ref, p_ref, wpg_ref, bpg_ref, wpp_ref,
                  l2g_ref, l2b_ref, l3g_ref, l3b_ref, o_ref, rows_scr, sem):
    i = pl.program_id(0)
    rows = x_ref.shape[0]
    slot = i % 2

    def gather(dref, s):
        def start(c, carry):
            for u in range(MOVE_UNROLL):
                r = c * MOVE_UNROLL + u
                for kk in range(TOP_K):
                    _tile_copy(y_ref, dref[0, 0, TOP_K * r + kk], rows_scr.at[s, kk], r * SUBLANES,
                               sem.at[s]).start(priority=kk)
            return carry

        lax.fori_loop(0, rows // MOVE_UNROLL, start, 0)

    @pl.when(i == 0)
    def _():
        gather(dcur_ref, 0)

    @pl.when(i + 1 < pl.num_programs(0))
    def _():
        gather(dnxt_ref, 1 - slot)

    pp = _dot(p_ref[...].astype(BF16), wpp_ref[...])
    pltpu.make_async_copy(rows_scr.at[slot], rows_scr.at[slot], sem.at[slot]).wait()

    info = info_ref[...]
    gate0 = info[:, I_G0:I_G0 + 1]
    gate1 = info[:, I_G1:I_G1 + 1]
    moe = (_from_token_tiles(rows_scr.at[slot, 0], rows) * gate0
           + _from_token_tiles(rows_scr.at[slot, 1], rows) * gate1)
    x2 = _ln(ALPHA * x_ref[...] + moe, l2g_ref[...], l2b_ref[...])
    gate = jax.nn.sigmoid(_dot(x2.astype(BF16), wpg_ref[...]) + bpg_ref[...])
    o_ref[...] = _ln(ALPHA * x2 + gate * pp, l3g_ref[...], l3b_ref[...])


def _final(dest3, x1, info, y, p2, w):
    T, D = x1.shape
    tc = MOVE_ROWS
    pd = p2.shape[1]
    full = lambda a: pl.BlockSpec(a.shape, lambda i: (0,) * a.ndim)
    consts = [w["wpg"], w["bpg"], w["wpp"], w["l2g"], w["l2b"], w["l3g"], w["l3b"]]
    last = T // tc - 1
    return pl.pallas_call(
        _final_kernel,
        grid=(T // tc,),
        in_specs=[pl.BlockSpec((1, 1, TOP_K * tc), lambda i: (i, 0, 0), memory_space=pltpu.SMEM),
                  pl.BlockSpec((1, 1, TOP_K * tc), lambda i: (jnp.minimum(i + 1, last), 0, 0), memory_space=pltpu.SMEM),
                  pl.BlockSpec((tc, D), lambda i: (i, 0)),
                  pl.BlockSpec((tc, LANES), lambda i: (i, 0)),
                  pl.BlockSpec(memory_space=pl.ANY),
                  pl.BlockSpec((tc, pd), lambda i: (i, 0))] + [full(a) for a in consts],
        out_specs=pl.BlockSpec((tc, D), lambda i: (i, 0)),
        out_shape=jax.ShapeDtypeStruct((T, D), F32),
        scratch_shapes=[pltpu.VMEM((2, TOP_K, tc * SUBLANES, LANES), F32), pltpu.SemaphoreType.DMA((2,))],
        compiler_params=pltpu.CompilerParams(dimension_semantics=("arbitrary",), vmem_limit_bytes=VMEM_LIMIT),
        name="final",
    )(dest3, dest3, x1, info, y, p2, *consts)


def _pad_heads(a, width):
    lead = a.shape[:-1]
    a = a.reshape(lead + (MLA_HEADS, width))
    a = jnp.pad(a, [(0, 0)] * len(lead) + [(0, 0), (0, LANES - width)])
    return a.reshape(lead + (HP,))


def _layer_weights(w_in, q_norm_g, w_q_up, kv_norm_g, w_kv_up, gm_ln_g, gm_ln_b, gm_w_s, gm_b_s,
                   mla_out_g, gm_out_g, w_o, ln1_g, ln1_b):
    D = w_in.shape[0]
    half = QK_ROPE // 2
    c1, c2, c3 = Q_RANK, Q_RANK + KV_RANK, Q_RANK + KV_RANK + QK_ROPE
    wkr = w_in[:, c2:c3]
    zeros = lambda *s: jnp.zeros(s, F32)
    kra = jnp.concatenate([zeros(D, QK_NOPE), wkr, zeros(D, LANES - QK_NOPE - QK_ROPE)], axis=1)
    krb = jnp.concatenate([zeros(D, QK_NOPE), wkr[:, half:], wkr[:, :half], zeros(D, LANES - QK_NOPE - QK_ROPE)], axis=1)
    win = jnp.concatenate([w_in[:, :c2], kra, krb, w_in[:, c3:]], axis=1).astype(BF16)

    wq3 = w_q_up.reshape(Q_RANK, MLA_HEADS, QK_NOPE + QK_ROPE)
    rope = wq3[..., QK_NOPE:]
    rope_sw = jnp.concatenate([rope[..., half:], rope[..., :half]], axis=-1)
    sw3 = jnp.concatenate([jnp.zeros_like(wq3[..., :QK_NOPE]), rope_sw], axis=-1)
    wq = jnp.concatenate([_pad_heads(w_q_up, QK_NOPE + QK_ROPE),
                          _pad_heads(sw3.reshape(Q_RANK, -1), QK_NOPE + QK_ROPE)], axis=1).astype(BF16)

    wkv3 = w_kv_up.reshape(KV_RANK, MLA_HEADS, QK_NOPE + V_HEAD)
    wk = _pad_heads(wkv3[..., :QK_NOPE].reshape(KV_RANK, -1), QK_NOPE).astype(BF16)
    wv = wkv3[..., QK_NOPE:].reshape(KV_RANK, -1).astype(BF16)

    inv = ROPE_THETA ** (-jnp.arange(0, QK_ROPE, 2, dtype=F32) / QK_ROPE)
    pad_l, pad_r = jnp.zeros((QK_NOPE,), F32), jnp.zeros((LANES - QK_NOPE - QK_ROPE,), F32)
    inv_t = jnp.concatenate([pad_l, inv, inv, pad_r])[None, :]
    sgn_t = jnp.concatenate([pad_l, -jnp.ones((half,), F32), jnp.ones((half,), F32), pad_r])[None, :]

    grp = jnp.arange(GM_OUT) // GM_CH
    gavg = jnp.where(grp[:, None] == grp[None, :], 1.0 / GM_CH, 0.0).astype(BF16)
    bias = jnp.repeat(gm_b_s.T, GM_CH, axis=1)

    woa = w_o[:MLA_OUT].astype(BF16)
    wog = w_o[MLA_OUT:].astype(BF16)
    return dict(win=win, qg=q_norm_g[None, :], wq=wq, kvg=kv_norm_g[None, :], wk=wk, wv=wv, inv=inv_t, sgn=sgn_t,
                lng=gm_ln_g[None, :], lnb=gm_ln_b[None, :], gavg=gavg, ws=gm_w_s, bias=bias, gog=gm_out_g[None, :],
                woa=woa, wog=wog, mog=mla_out_g[:, None], l1g=ln1_g[None, :], l1b=ln1_b[None, :])


def _moe(x1, w_rg, b_rg, w_re, b_re, w_gate, w_up, w_down):
    T, D = x1.shape
    pad = jnp.zeros((D, LANES - N_GROUPS - N_EXPERTS), F32)
    wr = jnp.concatenate([w_rg, w_re, pad], axis=1)
    br = jnp.concatenate([b_rg, b_re, pad[0]])[None, :]
    info, cnt = _route(x1, wr, br)

    bm = EXPERT_ROWS
    n_blocks = (T * TOP_K) // bm + N_EXPERTS
    counts = cnt[0, R_OFF:R_OFF + N_EXPERTS].astype(jnp.int32)
    padded = (counts + bm - 1) // bm * bm
    pad_ends = jnp.cumsum(padded)
    pad_starts = pad_ends - padded
    e_idx = info[:, I_E0:I_E1 + 1].astype(jnp.int32)
    rank = info[:, I_R0:I_R1 + 1].astype(jnp.int32)
    seg_start = jnp.sum(jnp.where(e_idx[..., None] == jnp.arange(N_EXPERTS), pad_starts, 0), axis=-1)
    dest = ((seg_start + rank) * SUBLANES).reshape(T // MOVE_ROWS, 1, TOP_K * MOVE_ROWS)
    block_start = jnp.arange(n_blocks, dtype=jnp.int32) * bm
    block_expert = jnp.minimum(jnp.sum(pad_ends[None, :] <= block_start[:, None], axis=1),
                               N_EXPERTS - 1).astype(jnp.int32)

    buf = _dispatch(dest, x1, n_blocks * bm)
    y = _experts(block_expert, buf, w_gate, w_up, w_down)
    return info, dest, y


def kernel(x, p, positions, w_in, q_norm_g, w_q_up, kv_norm_g, w_kv_up, gm_ln_g, gm_ln_b, gm_w_s, gm_b_s, mla_out_g, gm_out_g, w_o, ln1_g, ln1_b, w_rg, b_rg, w_re, b_re, w_gate, w_up, w_down, ln2_g, ln2_b, w_pg, b_pg, w_pp, ln3_g, ln3_b):
    B, S, D = x.shape
    T = B * S
    assert S % ATTN_ROWS == 0 and PREP_ROWS == ATTN_ROWS and PREP_ROWS % CHUNK == 0
    assert T % ROUTE_ROWS == 0 and T % MOVE_ROWS == 0 and (T * TOP_K) % EXPERT_ROWS == 0
    assert D == SUBLANES * LANES and MOVE_ROWS % MOVE_UNROLL == 0
    pos3 = positions.reshape(B, S, 1)
    for i in range(DEPTH):
        w = _layer_weights(w_in[i], q_norm_g[i], w_q_up[i], kv_norm_g[i], w_kv_up[i], gm_ln_g[i], gm_ln_b[i],
                           gm_w_s[i], gm_b_s[i], mla_out_g[i], gm_out_g[i], w_o[i], ln1_g[i], ln1_b[i])
        q, k, vt, g = _prep(x, pos3, w)
        x1 = _attn(q, k, vt, g, x, w).reshape(T, D)
        info, dest, y = _moe(x1, w_rg[i], b_rg[i], w_re[i], b_re[i], w_gate[i], w_up[i], w_down[i])
        wf = dict(wpg=w_pg[i].astype(BF16), bpg=b_pg[i][None, :], wpp=w_pp[i].astype(BF16),
                  l2g=ln2_g[i][None, :], l2b=ln2_b[i][None, :], l3g=ln3_g[i][None, :], l3b=ln3_b[i][None, :])
        x = _final(dest, x1, info, y, p[i].reshape(T, -1), wf).reshape(B, S, D)
    return x
```

```python
import functools

import jax
import jax.numpy as jnp
from jax import lax
from jax.experimental import pallas as pl
from jax.experimental.pallas import tpu as pltpu

F32 = jnp.float32
BF16 = jnp.bfloat16

MLA_HEADS = 8
QK_NOPE = 64
QK_ROPE = 32
V_HEAD = 64
Q_RANK = 256
KV_RANK = 128
ROPE_THETA = 10000.0
MLA_OUT = MLA_HEADS * V_HEAD
GM_GROUPS = 8
GM_CH = 64
GM_OUT = GM_GROUPS * GM_CH
CHUNK = 128
N_GROUPS = 4
EXP_PER_GROUP = 8
N_EXPERTS = N_GROUPS * EXP_PER_GROUP
TOP_K = 2
EPS = 1e-6
DEPTH = 1
ALPHA = (2.0 * DEPTH) ** 0.25
SM_SCALE = (QK_NOPE + QK_ROPE) ** -0.5
LOG2E = 1.4426950408889634

LANES = 128
SUBLANES = 8
VMEM_LIMIT = 56 * 1024 * 1024

PREP_ROWS = 256
ATTN_ROWS = 256
ROUTE_ROWS = 512
MOVE_ROWS = 256
MOVE_UNROLL = 8
EXPERT_ROWS = 256

C_Q = 0
C_KV = C_Q + Q_RANK
C_KRA = C_KV + KV_RANK
C_KRB = C_KRA + LANES
C_U = C_KRB + LANES
C_V = C_U + GM_OUT
C_END = C_V + GM_OUT
HP = MLA_HEADS * LANES

I_E0, I_E1, I_R0, I_R1, I_G0, I_G1 = range(6)
R_OFF = N_GROUPS


def _rms(v, g):
    return v * lax.rsqrt(jnp.mean(v * v, axis=-1, keepdims=True) + EPS) * g


def _ln(v, g, b):
    mu = jnp.mean(v, axis=-1, keepdims=True)
    d = v - mu
    var = jnp.mean(d * d, axis=-1, keepdims=True)
    return d * lax.rsqrt(var + EPS) * g + b


def _dot(a, b):
    return jnp.dot(a, b, preferred_element_type=F32)


def _prep_kernel(x_ref, pos_ref, win_ref, qg_ref, wq_ref, kvg_ref, wk_ref, wv_ref, inv_ref, sgn_ref,
                 lng_ref, lnb_ref, gavg_ref, ws_ref, bias_ref, gog_ref,
                 q_ref, k_ref, vt_ref, g_ref):
    rows = x_ref.shape[1]
    h = _dot(x_ref[0].astype(BF16), win_ref[...])

    ang = pos_ref[0].astype(F32) * inv_ref[...]
    cos_t = jnp.cos(ang)
    sin_t = jnp.sin(ang) * sgn_ref[...]

    cq = _rms(h[:, C_Q:C_Q + Q_RANK], qg_ref[...]).astype(BF16)
    q2 = _dot(cq, wq_ref[...])
    for hd in range(MLA_HEADS):
        lo = hd * LANES
        qh = q2[:, lo:lo + LANES] * cos_t + q2[:, HP + lo:HP + lo + LANES] * sin_t
        q_ref[0, :, lo:lo + LANES] = (qh * (SM_SCALE * LOG2E)).astype(BF16)

    ckv = _rms(h[:, C_KV:C_KV + KV_RANK], kvg_ref[...]).astype(BF16)
    kp = _dot(ckv, wk_ref[...])
    kr = h[:, C_KRA:C_KRA + LANES] * cos_t + h[:, C_KRB:C_KRB + LANES] * sin_t
    for hd in range(MLA_HEADS):
        lo = hd * LANES
        k_ref[0, :, lo:lo + LANES] = (kp[:, lo:lo + LANES] + kr).astype(BF16)
    vt_ref[0, 0] = _dot(ckv, wv_ref[...]).T.astype(BF16)

    u = jax.nn.gelu(h[:, C_U:C_U + GM_OUT])
    vv = jax.nn.gelu(h[:, C_V:C_V + GM_OUT])
    mu = _dot(vv.astype(BF16), gavg_ref[...])
    d = vv - mu
    var = _dot((d * d).astype(BF16), gavg_ref[...])
    vn = (d * lax.rsqrt(var + EPS) * lng_ref[...] + lnb_ref[...]).astype(BF16)

    tri = lax.broadcasted_iota(jnp.int32, (CHUNK, CHUNK), 0) >= lax.broadcasted_iota(jnp.int32, (CHUNK, CHUNK), 1)
    wm = [jnp.where(tri, ws_ref[g], 0.0).astype(BF16) for g in range(GM_GROUPS)]
    low_half = lax.broadcasted_iota(jnp.int32, (CHUNK, LANES), 1) < GM_CH
    for c in range(rows // CHUNK):
        r0 = c * CHUNK
        parts = []
        for pr in range(GM_GROUPS // 2):
            tile = vn[r0:r0 + CHUNK, pr * LANES:(pr + 1) * LANES]
            parts.append(jnp.where(low_half, _dot(wm[2 * pr], tile), _dot(wm[2 * pr + 1], tile)))
        sg = jnp.concatenate(parts, axis=1) + bias_ref[...]
        gm = u[r0:r0 + CHUNK] * sg
        g_ref[0, r0:r0 + CHUNK, :] = _rms(gm, gog_ref[...]).astype(BF16)


def _prep(x, pos3, w):
    B, S, D = x.shape
    ts = PREP_ROWS
    full = lambda a: pl.BlockSpec(a.shape, lambda b, i: (0,) * a.ndim)
    consts = [w["win"], w["qg"], w["wq"], w["kvg"], w["wk"], w["wv"], w["inv"], w["sgn"],
              w["lng"], w["lnb"], w["gavg"], w["ws"], w["bias"], w["gog"]]
    return pl.pallas_call(
        _prep_kernel,
        grid=(B, S // ts),
        in_specs=[pl.BlockSpec((1, ts, D), lambda b, i: (b, i, 0)),
                  pl.BlockSpec((1, ts, 1), lambda b, i: (b, i, 0))] + [full(a) for a in consts],
        out_specs=[pl.BlockSpec((1, ts, HP), lambda b, i: (b, i, 0)),
                   pl.BlockSpec((1, ts, HP), lambda b, i: (b, i, 0)),
                   pl.BlockSpec((1, 1, MLA_OUT, ts), lambda b, i: (b, i, 0, 0)),
                   pl.BlockSpec((1, ts, GM_OUT), lambda b, i: (b, i, 0))],
        out_shape=[jax.ShapeDtypeStruct((B, S, HP), BF16)] * 2
        + [jax.ShapeDtypeStruct((B, S // ts, MLA_OUT, ts), BF16), jax.ShapeDtypeStruct((B, S, GM_OUT), BF16)],
        compiler_params=pltpu.CompilerParams(dimension_semantics=("parallel", "parallel"),
                                             vmem_limit_bytes=VMEM_LIMIT),
        name="prep",
    )(x, pos3, *consts)


def _attn_kernel(q_ref, k_ref, vt_ref, g_ref, x_ref, woa_ref, wog_ref, mog_ref, l1g_ref, l1b_ref,
                 o_ref, m_scr, l_scr, acc_scr):
    i = pl.program_id(1)
    tq = q_ref.shape[1]
    tk = tq
    key = lax.broadcasted_iota(jnp.int32, (tk, tq), 0)
    qry = lax.broadcasted_iota(jnp.int32, (tk, tq), 1)
    diag_mask = key <= qry

    m_scr[...] = jnp.full(m_scr.shape, -1e30, F32)
    l_scr[...] = jnp.zeros(l_scr.shape, F32)
    acc_scr[...] = jnp.zeros(acc_scr.shape, F32)

    def kv_step(j, masked):
        k0 = pl.multiple_of(j * tk, tk)
        scores = []
        for hd in range(MLA_HEADS):
            lo = hd * LANES
            qh = q_ref[0, :, lo:lo + LANES]
            kj = k_ref[0, pl.ds(k0, tk), lo:lo + LANES]
            scores.append(lax.dot_general(kj, qh, (((1,), (1,)), ((), ())), preferred_element_type=F32))
        for hd in range(MLA_HEADS):
            s = scores[hd]
            vt = vt_ref[0, j, hd * V_HEAD:(hd + 1) * V_HEAD, :]
            if masked:
                s = jnp.where(diag_mask, s, -1e30)
            m_prev = m_scr[hd]
            m_new = jnp.maximum(m_prev, jnp.max(s, axis=0, keepdims=True))
            p = jnp.exp2(s - m_new)
            scale = jnp.exp2(m_prev - m_new)
            l_scr[hd] = scale * l_scr[hd] + jnp.sum(p, axis=0, keepdims=True)
            acc_scr[hd] = scale * acc_scr[hd] + _dot(vt, p.astype(BF16))
            m_scr[hd] = m_new

    def full_step(j, c):
        kv_step(j, False)
        return c

    lax.fori_loop(0, i, full_step, 0)
    kv_step(i, True)

    at = jnp.concatenate([acc_scr[hd] / l_scr[hd] for hd in range(MLA_HEADS)], axis=0)
    at = at * lax.rsqrt(jnp.mean(at * at, axis=0, keepdims=True) + EPS) * mog_ref[...]
    mix = _dot(at.T.astype(BF16), woa_ref[...]) + _dot(g_ref[0], wog_ref[...])
    o_ref[0] = _ln(ALPHA * x_ref[0] + mix, l1g_ref[...], l1b_ref[...])


def _attn(q, k, vt, g, x, w):
    B, S, D = x.shape
    tq = ATTN_ROWS
    full = lambda a: pl.BlockSpec(a.shape, lambda b, i: (0,) * a.ndim)
    consts = [w["woa"], w["wog"], w["mog"], w["l1g"], w["l1b"]]
    return pl.pallas_call(
        _attn_kernel,
        grid=(B, S // tq),
        in_specs=[pl.BlockSpec((1, tq, HP), lambda b, i: (b, i, 0)),
                  pl.BlockSpec((1, S, HP), lambda b, i: (b, 0, 0)),
                  pl.BlockSpec((1,) + vt.shape[1:], lambda b, i: (b, 0, 0, 0)),
                  pl.BlockSpec((1, tq, GM_OUT), lambda b, i: (b, i, 0)),
                  pl.BlockSpec((1, tq, D), lambda b, i: (b, i, 0))] + [full(a) for a in consts],
        out_specs=pl.BlockSpec((1, tq, D), lambda b, i: (b, i, 0)),
        out_shape=jax.ShapeDtypeStruct((B, S, D), F32),
        scratch_shapes=[pltpu.VMEM((MLA_HEADS, 1, tq), F32), pltpu.VMEM((MLA_HEADS, 1, tq), F32),
                        pltpu.VMEM((MLA_HEADS, V_HEAD, tq), F32)],
        compiler_params=pltpu.CompilerParams(dimension_semantics=("parallel", "parallel"),
                                             vmem_limit_bytes=VMEM_LIMIT),
        name="attn",
    )(q, k, vt, g, x, *consts)


def _route_kernel(x_ref, wr_ref, br_ref, info_ref, cnt_ref, carry_scr, tri_scr):
    step = pl.program_id(0)
    tt = x_ref.shape[0]

    @pl.when(step == 0)
    def _():
        carry_scr[...] = jnp.zeros_like(carry_scr)
        r = lax.broadcasted_iota(jnp.int32, (tt, tt), 0)
        c = lax.broadcasted_iota(jnp.int32, (tt, tt), 1)
        tri_scr[...] = jnp.where(c < r, 1.0, 0.0).astype(BF16)

    x = x_ref[...]
    xh = x.astype(BF16)
    xl = (x - xh.astype(F32)).astype(BF16)
    wr = wr_ref[...]
    wh = wr.astype(BF16)
    wl = (wr - wh.astype(F32)).astype(BF16)
    logits = _dot(xh, wh) + _dot(xl, wh) + _dot(xh, wl) + br_ref[...]

    lane = lax.broadcasted_iota(jnp.int32, (tt, LANES), 1)
    neg = jnp.float32(-jnp.inf)

    is_g = lane < N_GROUPS
    lg = jnp.where(is_g, logits, neg)
    gmax = jnp.max(lg, axis=-1, keepdims=True)
    g_idx = jnp.min(jnp.where(lg == gmax, lane, LANES), axis=-1, keepdims=True)
    g_den = jnp.sum(jnp.where(is_g, jnp.exp(lg - gmax), 0.0), axis=-1, keepdims=True)
    g_p = 1.0 / g_den

    in_grp = (lane >= R_OFF) & (lane < R_OFF + N_EXPERTS) & (((lane - R_OFF) >> 3) == g_idx)
    le = jnp.where(in_grp, logits, neg)
    m1 = jnp.max(le, axis=-1, keepdims=True)
    i1 = jnp.min(jnp.where(le == m1, lane, LANES), axis=-1, keepdims=True)
    le2 = jnp.where(lane == i1, neg, le)
    m2 = jnp.max(le2, axis=-1, keepdims=True)
    i2 = jnp.min(jnp.where(le2 == m2, lane, LANES), axis=-1, keepdims=True)
    e2 = jnp.exp(m2 - m1)
    gate0 = g_p / (1.0 + e2)
    gate1 = g_p * e2 / (1.0 + e2)

    hit1 = lane == i1
    hit2 = lane == i2
    onehot = jnp.where(hit1 | hit2, 1.0, 0.0)
    before = _dot(tri_scr[...], onehot.astype(BF16)) + carry_scr[...]
    r0 = jnp.sum(jnp.where(hit1, before, 0.0), axis=-1, keepdims=True)
    r1 = jnp.sum(jnp.where(hit2, before, 0.0), axis=-1, keepdims=True)
    carry_scr[...] = carry_scr[...] + jnp.sum(onehot, axis=0, keepdims=True)
    cnt_ref[...] = carry_scr[...]

    info = jnp.where(lane == I_E0, (i1 - R_OFF).astype(F32), 0.0)
    info = jnp.where(lane == I_E1, (i2 - R_OFF).astype(F32), info)
    info = jnp.where(lane == I_R0, r0, info)
    info = jnp.where(lane == I_R1, r1, info)
    info = jnp.where(lane == I_G0, gate0, info)
    info = jnp.where(lane == I_G1, gate1, info)
    info_ref[...] = info


def _route(x1, wr, br):
    T, D = x1.shape
    tt = ROUTE_ROWS
    return pl.pallas_call(
        _route_kernel,
        grid=(T // tt,),
        in_specs=[pl.BlockSpec((tt, D), lambda i: (i, 0)),
                  pl.BlockSpec(wr.shape, lambda i: (0, 0)),
                  pl.BlockSpec(br.shape, lambda i: (0, 0))],
        out_specs=[pl.BlockSpec((tt, LANES), lambda i: (i, 0)),
                   pl.BlockSpec((1, LANES), lambda i: (0, 0))],
        out_shape=[jax.ShapeDtypeStruct((T, LANES), F32), jax.ShapeDtypeStruct((1, LANES), F32)],
        scratch_shapes=[pltpu.VMEM((1, LANES), F32), pltpu.VMEM((tt, tt), BF16)],
        compiler_params=pltpu.CompilerParams(dimension_semantics=("arbitrary",), vmem_limit_bytes=VMEM_LIMIT),
        name="route",
    )(x1, wr, br)


def _to_token_tiles(dst_ref, val):
    rows = val.shape[0]
    for c in range(SUBLANES):
        dst_ref[pl.ds(c, rows, stride=SUBLANES), :] = val[:, c * LANES:(c + 1) * LANES]


def _from_token_tiles(src_ref, rows):
    return jnp.concatenate([src_ref[pl.ds(c, rows, stride=SUBLANES), :] for c in range(SUBLANES)], axis=1)


def _tile_copy(src_ref, src_row, dst_ref, dst_row, sem):
    return pltpu.make_async_copy(src_ref.at[pl.ds(pl.multiple_of(src_row, SUBLANES), SUBLANES)],
                                 dst_ref.at[pl.ds(pl.multiple_of(dst_row, SUBLANES), SUBLANES)], sem)


def _dispatch_kernel(dest_ref, x_ref, zero_ref, buf_ref, stage_scr, sem, *, n_steps):
    del zero_ref
    i = pl.program_id(0)
    rows = x_ref.shape[0]
    slot = i % 2

    def drain(s):
        for _ in range(TOP_K):
            pltpu.make_async_copy(stage_scr.at[s], stage_scr.at[s], sem.at[s]).wait()

    @pl.when(i >= 2)
    def _():
        drain(slot)

    _to_token_tiles(stage_scr.at[slot], x_ref[...])

    def start(c, carry):
        for u in range(MOVE_UNROLL):
            r = c * MOVE_UNROLL + u
            for kk in range(TOP_K):
                _tile_copy(stage_scr.at[slot], r * SUBLANES, buf_ref, dest_ref[0, 0, TOP_K * r + kk],
                           sem.at[slot]).start(priority=kk)
        return carry

    lax.fori_loop(0, rows // MOVE_UNROLL, start, 0)

    @pl.when(i == n_steps - 1)
    def _():
        drain(slot)
        if n_steps >= 2:
            drain(1 - slot)


def _dispatch(dest3, x1, n_rows):
    T, D = x1.shape
    td = MOVE_ROWS
    n_steps = T // td
    zeros = jnp.zeros((n_rows * SUBLANES, LANES), F32)
    return pl.pallas_call(
        functools.partial(_dispatch_kernel, n_steps=n_steps),
        grid=(n_steps,),
        in_specs=[pl.BlockSpec((1, 1, TOP_K * td), lambda i: (i, 0, 0), memory_space=pltpu.SMEM),
                  pl.BlockSpec((td, D), lambda i: (i, 0)),
                  pl.BlockSpec(memory_space=pl.ANY)],
        out_specs=pl.BlockSpec(memory_space=pl.ANY),
        out_shape=jax.ShapeDtypeStruct((n_rows * SUBLANES, LANES), F32),
        scratch_shapes=[pltpu.VMEM((2, td * SUBLANES, LANES), F32), pltpu.SemaphoreType.DMA((2,))],
        input_output_aliases={2: 0},
        compiler_params=pltpu.CompilerParams(dimension_semantics=("arbitrary",), vmem_limit_bytes=VMEM_LIMIT),
        name="dispatch",
    )(dest3, x1, zeros)


def _expert_kernel(be_ref, ne_ref, nu_ref, buf_ref, wg_hbm, wu_hbm, wd_hbm, y_ref,
                   sg_scr, su_scr, sd_scr, wg_scr, wu_scr, wd_scr, cur_ref, sem):
    b = pl.program_id(0)
    e = be_ref[b]

    def fetch(expert, s):
        return (pltpu.make_async_copy(wg_hbm.at[expert], sg_scr.at[s], sem.at[s, 0]),
                pltpu.make_async_copy(wu_hbm.at[expert], su_scr.at[s], sem.at[s, 1]),
                pltpu.make_async_copy(wd_hbm.at[expert], sd_scr.at[s], sem.at[s, 2]))

    @pl.when(b == 0)
    def _():
        cur_ref[0] = 0
        for c in fetch(e, 0):
            c.start()

    @pl.when((b == 0) | (be_ref[jnp.maximum(b - 1, 0)] != e))
    def _():
        s = cur_ref[0]
        for c in fetch(e, s):
            c.wait()
        wg_scr[...] = sg_scr[s].astype(BF16)
        wu_scr[...] = su_scr[s].astype(BF16)
        wd_scr[...] = sd_scr[s].astype(BF16)
        nxt = ne_ref[b]

        @pl.when(nxt >= 0)
        def _():
            for c in fetch(nxt, 1 - s):
                c.start()

        cur_ref[0] = 1 - s

    @pl.when(b < nu_ref[0])
    def _():
        xb = _from_token_tiles(buf_ref, EXPERT_ROWS).astype(BF16)
        hidden = jax.nn.silu(_dot(xb, wg_scr[...])) * _dot(xb, wu_scr[...])
        _to_token_tiles(y_ref, _dot(hidden.astype(BF16), wd_scr[...]))

    @pl.when(b >= nu_ref[0])
    def _():
        y_ref[...] = jnp.zeros(y_ref.shape, F32)


def _experts(block_expert, next_expert, n_used, buf, w_gate, w_up, w_down):
    bm = EXPERT_ROWS
    D, ff = w_gate.shape[1:]
    n_rows = buf.shape[0] // SUBLANES
    grid_spec = pltpu.PrefetchScalarGridSpec(
        num_scalar_prefetch=3,
        grid=(n_rows // bm,),
        in_specs=[pl.BlockSpec((bm * SUBLANES, LANES), lambda b, *_: (b, 0)),
                  pl.BlockSpec(memory_space=pl.ANY),
                  pl.BlockSpec(memory_space=pl.ANY),
                  pl.BlockSpec(memory_space=pl.ANY)],
        out_specs=pl.BlockSpec((bm * SUBLANES, LANES), lambda b, *_: (b, 0)),
        scratch_shapes=[pltpu.VMEM((2, D, ff), F32), pltpu.VMEM((2, D, ff), F32), pltpu.VMEM((2, ff, D), F32),
                        pltpu.VMEM((D, ff), BF16), pltpu.VMEM((D, ff), BF16), pltpu.VMEM((ff, D), BF16),
                        pltpu.SMEM((1,), jnp.int32), pltpu.SemaphoreType.DMA((2, 3))],
    )
    return pl.pallas_call(
        _expert_kernel,
        grid_spec=grid_spec,
        out_shape=jax.ShapeDtypeStruct(buf.shape, F32),
        compiler_params=pltpu.CompilerParams(dimension_semantics=("arbitrary",), vmem_limit_bytes=VMEM_LIMIT),
        name="experts",
    )(block_expert, next_expert, n_used, buf, w_gate, w_up, w_down)


def _final_kernel(dcur_ref, dnxt_ref, x_ref, info_ref, y_ref, p_ref, wpg_ref, bpg_ref, wpp_ref,
                  l2g_ref, l2b_ref, l3g_ref, l3b_ref, o_ref, rows_scr, sem):
    i = pl.program_id(0)
    rows = x_ref.shape[0]
    slot = i % 2

    def gather(dref, s):
        def start(c, carry):
            for u in range(MOVE_UNROLL):
                r = c * MOVE_UNROLL + u
                for kk in range(TOP_K):
                    _tile_copy(y_ref, dref[0, 0, TOP_K * r + kk], rows_scr.at[s, kk], r * SUBLANES,
                               sem.at[s]).start(priority=kk)
            return carry

        lax.fori_loop(0, rows // MOVE_UNROLL, start, 0)

    @pl.when(i == 0)
    def _():
        gather(dcur_ref, 0)

    @pl.when(i + 1 < pl.num_programs(0))
    def _():
        gather(dnxt_ref, 1 - slot)

    pp = _dot(p_ref[...].astype(BF16), wpp_ref[...])
    pltpu.make_async_copy(rows_scr.at[slot], rows_scr.at[slot], sem.at[slot]).wait()

    info = info_ref[...]
    gate0 = info[:, I_G0:I_G0 + 1]
    gate1 = info[:, I_G1:I_G1 + 1]
    moe = (_from_token_tiles(rows_scr.at[slot, 0], rows) * gate0
           + _from_token_tiles(rows_scr.at[slot, 1], rows) * gate1)
    x2 = _ln(ALPHA * x_ref[...] + moe, l2g_ref[...], l2b_ref[...])
    gate = jax.nn.sigmoid(_dot(x2.astype(BF16), wpg_ref[...]) + bpg_ref[...])
    o_ref[...] = _ln(ALPHA * x2 + gate * pp, l3g_ref[...], l3b_ref[...])


def _final(dest3, x1, info, y, p2, w):
    T, D = x1.shape
    tc = MOVE_ROWS
    pd = p2.shape[1]
    full = lambda a: pl.BlockSpec(a.shape, lambda i: (0,) * a.ndim)
    consts = [w["wpg"], w["bpg"], w["wpp"], w["l2g"], w["l2b"], w["l3g"], w["l3b"]]
    last = T // tc - 1
    return pl.pallas_call(
        _final_kernel,
        grid=(T // tc,),
        in_specs=[pl.BlockSpec((1, 1, TOP_K * tc), lambda i: (i, 0, 0), memory_space=pltpu.SMEM),
                  pl.BlockSpec((1, 1, TOP_K * tc), lambda i: (jnp.minimum(i + 1, last), 0, 0), memory_space=pltpu.SMEM),
                  pl.BlockSpec((tc, D), lambda i: (i, 0)),
                  pl.BlockSpec((tc, LANES), lambda i: (i, 0)),
                  pl.BlockSpec(memory_space=pl.ANY),
                  pl.BlockSpec((tc, pd), lambda i: (i, 0))] + [full(a) for a in consts],
        out_specs=pl.BlockSpec((tc, D), lambda i: (i, 0)),
        out_shape=jax.ShapeDtypeStruct((T, D), F32),
        scratch_shapes=[pltpu.VMEM((2, TOP_K, tc * SUBLANES, LANES), F32), pltpu.SemaphoreType.DMA((2,))],
        compiler_params=pltpu.CompilerParams(dimension_semantics=("arbitrary",), vmem_limit_bytes=VMEM_LIMIT),
        name="final",
    )(dest3, dest3, x1, info, y, p2, *consts)


def _pad_heads(a, width):
    lead = a.shape[:-1]
    a = a.reshape(lead + (MLA_HEADS, width))
    a = jnp.pad(a, [(0, 0)] * len(lead) + [(0, 0), (0, LANES - width)])
    return a.reshape(lead + (HP,))


def _layer_weights(w_in, q_norm_g, w_q_up, kv_norm_g, w_kv_up, gm_ln_g, gm_ln_b, gm_w_s, gm_b_s,
                   mla_out_g, gm_out_g, w_o, ln1_g, ln1_b):
    D = w_in.shape[0]
    half = QK_ROPE // 2
    c1, c2, c3 = Q_RANK, Q_RANK + KV_RANK, Q_RANK + KV_RANK + QK_ROPE
    wkr = w_in[:, c2:c3]
    zeros = lambda *s: jnp.zeros(s, F32)
    kra = jnp.concatenate([zeros(D, QK_NOPE), wkr, zeros(D, LANES - QK_NOPE - QK_ROPE)], axis=1)
    krb = jnp.concatenate([zeros(D, QK_NOPE), wkr[:, half:], wkr[:, :half], zeros(D, LANES - QK_NOPE - QK_ROPE)], axis=1)
    win = jnp.concatenate([w_in[:, :c2], kra, krb, w_in[:, c3:]], axis=1).astype(BF16)

    wq3 = w_q_up.reshape(Q_RANK, MLA_HEADS, QK_NOPE + QK_ROPE)
    rope = wq3[..., QK_NOPE:]
    rope_sw = jnp.concatenate([rope[..., half:], rope[..., :half]], axis=-1)
    sw3 = jnp.concatenate([jnp.zeros_like(wq3[..., :QK_NOPE]), rope_sw], axis=-1)
    wq = jnp.concatenate([_pad_heads(w_q_up, QK_NOPE + QK_ROPE),
                          _pad_heads(sw3.reshape(Q_RANK, -1), QK_NOPE + QK_ROPE)], axis=1).astype(BF16)

    wkv3 = w_kv_up.reshape(KV_RANK, MLA_HEADS, QK_NOPE + V_HEAD)
    wk = _pad_heads(wkv3[..., :QK_NOPE].reshape(KV_RANK, -1), QK_NOPE).astype(BF16)
    wv = wkv3[..., QK_NOPE:].reshape(KV_RANK, -1).astype(BF16)

    inv = ROPE_THETA ** (-jnp.arange(0, QK_ROPE, 2, dtype=F32) / QK_ROPE)
    pad_l, pad_r = jnp.zeros((QK_NOPE,), F32), jnp.zeros((LANES - QK_NOPE - QK_ROPE,), F32)
    inv_t = jnp.concatenate([pad_l, inv, inv, pad_r])[None, :]
    sgn_t = jnp.concatenate([pad_l, -jnp.ones((half,), F32), jnp.ones((half,), F32), pad_r])[None, :]

    grp = jnp.arange(GM_OUT) // GM_CH
    gavg = jnp.where(grp[:, None] == grp[None, :], 1.0 / GM_CH, 0.0).astype(BF16)
    bias = jnp.repeat(gm_b_s.T, GM_CH, axis=1)

    woa = w_o[:MLA_OUT].astype(BF16)
    wog = w_o[MLA_OUT:].astype(BF16)
    return dict(win=win, qg=q_norm_g[None, :], wq=wq, kvg=kv_norm_g[None, :], wk=wk, wv=wv, inv=inv_t, sgn=sgn_t,
                lng=gm_ln_g[None, :], lnb=gm_ln_b[None, :], gavg=gavg, ws=gm_w_s, bias=bias, gog=gm_out_g[None, :],
                woa=woa, wog=wog, mog=mla_out_g[:, None], l1g=ln1_g[None, :], l1b=ln1_b[None, :])


def _moe(x1, w_rg, b_rg, w_re, b_re, w_gate, w_up, w_down):
    T, D = x1.shape
    pad = jnp.zeros((D, LANES - N_GROUPS - N_EXPERTS), F32)
    wr = jnp.concatenate([w_rg, w_re, pad], axis=1)
    br = jnp.concatenate([b_rg, b_re, pad[0]])[None, :]
    info, cnt = _route(x1, wr, br)

    bm = EXPERT_ROWS
    n_blocks = (T * TOP_K) // bm + N_EXPERTS
    counts = cnt[0, R_OFF:R_OFF + N_EXPERTS].astype(jnp.int32)
    padded = (counts + bm - 1) // bm * bm
    pad_ends = jnp.cumsum(padded)
    pad_starts = pad_ends - padded
    e_idx = info[:, I_E0:I_E1 + 1].astype(jnp.int32)
    rank = info[:, I_R0:I_R1 + 1].astype(jnp.int32)
    seg_start = jnp.sum(jnp.where(e_idx[..., None] == jnp.arange(N_EXPERTS), pad_starts, 0), axis=-1)
    dest = ((seg_start + rank) * SUBLANES).reshape(T // MOVE_ROWS, 1, TOP_K * MOVE_ROWS)
    block_start = jnp.arange(n_blocks, dtype=jnp.int32) * bm
    block_expert = jnp.minimum(jnp.sum(pad_ends[None, :] <= block_start[:, None], axis=1),
                               N_EXPERTS - 1).astype(jnp.int32)

    blk = jnp.arange(n_blocks)
    later = (blk[None, :] > blk[:, None]) & (block_expert[None, :] != block_expert[:, None])
    next_expert = jnp.min(jnp.where(later, block_expert[None, :], N_EXPERTS), axis=1)
    next_expert = jnp.where(next_expert == N_EXPERTS, -1, next_expert).astype(jnp.int32)
    n_used = (pad_ends[-1:] // bm).astype(jnp.int32)

    buf = _dispatch(dest, x1, n_blocks * bm)
    y = _experts(block_expert, next_expert, n_used, buf, w_gate, w_up, w_down)
    return info, dest, y


def kernel(x, p, positions, w_in, q_norm_g, w_q_up, kv_norm_g, w_kv_up, gm_ln_g, gm_ln_b, gm_w_s, gm_b_s, mla_out_g, gm_out_g, w_o, ln1_g, ln1_b, w_rg, b_rg, w_re, b_re, w_gate, w_up, w_down, ln2_g, ln2_b, w_pg, b_pg, w_pp, ln3_g, ln3_b):
    B, S, D = x.shape
    T = B * S
    assert S % ATTN_ROWS == 0 and PREP_ROWS == ATTN_ROWS and PREP_ROWS % CHUNK == 0
    assert T % ROUTE_ROWS == 0 and T % MOVE_ROWS == 0 and (T * TOP_K) % EXPERT_ROWS == 0
    assert D == SUBLANES * LANES and MOVE_ROWS % MOVE_UNROLL == 0
    pos3 = positions.reshape(B, S, 1)
    for i in range(DEPTH):
        w = _layer_weights(w_in[i], q_norm_g[i], w_q_up[i], kv_norm_g[i], w_kv_up[i], gm_ln_g[i], gm_ln_b[i],
                           gm_w_s[i], gm_b_s[i], mla_out_g[i], gm_out_g[i], w_o[i], ln1_g[i], ln1_b[i])
        q, k, vt, g = _prep(x, pos3, w)
        x1 = _attn(q, k, vt, g, x, w).reshape(T, D)
        info, dest, y = _moe(x1, w_rg[i], b_rg[i], w_re[i], b_re[i], w_gate[i], w_up[i], w_down[i])
        wf = dict(wpg=w_pg[i].astype(BF16), bpg=b_pg[i][None, :], wpp=w_pp[i].astype(BF16),
                  l2g=ln2_g[i][None, :], l2b=ln2_b[i][None, :], l3g=ln3_g[i][None, :], l3b=ln3_b[i][None, :])
        x = _final(dest, x1, info, y, p[i].reshape(T, -1), wf).reshape(B, S, D)
    return x
```

```python
import functools

import jax
import jax.numpy as jnp
from jax import lax
from jax.experimental import pallas as pl
from jax.experimental.pallas import tpu as pltpu

F32 = jnp.float32
BF16 = jnp.bfloat16

MLA_HEADS = 8
QK_NOPE = 64
QK_ROPE = 32
V_HEAD = 64
Q_RANK = 256
KV_RANK = 128
ROPE_THETA = 10000.0
MLA_OUT = MLA_HEADS * V_HEAD
GM_GROUPS = 8
GM_CH = 64
GM_OUT = GM_GROUPS * GM_CH
CHUNK = 128
N_GROUPS = 4
EXP_PER_GROUP = 8
N_EXPERTS = N_GROUPS * EXP_PER_GROUP
TOP_K = 2
EPS = 1e-6
DEPTH = 1
ALPHA = (2.0 * DEPTH) ** 0.25
SM_SCALE = (QK_NOPE + QK_ROPE) ** -0.5
LOG2E = 1.4426950408889634

LANES = 128
SUBLANES = 8
VMEM_LIMIT = 56 * 1024 * 1024

PREP_ROWS = 256
ATTN_ROWS = 256
ROUTE_ROWS = 512
MOVE_ROWS = 256
MOVE_UNROLL = 8
EXPERT_ROWS = 256

C_Q = 0
C_KV = C_Q + Q_RANK
C_KRA = C_KV + KV_RANK
C_KRB = C_KRA + LANES
C_U = C_KRB + LANES
C_V = C_U + GM_OUT
C_END = C_V + GM_OUT
HP = MLA_HEADS * LANES

I_E0, I_E1, I_R0, I_R1, I_G0, I_G1 = range(6)
R_OFF = N_GROUPS


def _rms(v, g):
    return v * lax.rsqrt(jnp.mean(v * v, axis=-1, keepdims=True) + EPS) * g


def _ln(v, g, b):
    mu = jnp.mean(v, axis=-1, keepdims=True)
    d = v - mu
    var = jnp.mean(d * d, axis=-1, keepdims=True)
    return d * lax.rsqrt(var + EPS) * g + b


def _dot(a, b):
    return jnp.dot(a, b, preferred_element_type=F32)


def _prep_kernel(x_ref, pos_ref, win_ref, qg_ref, wq_ref, kvg_ref, wk_ref, wv_ref, inv_ref, sgn_ref,
                 lng_ref, lnb_ref, gavg_ref, ws_ref, bias_ref, gog_ref,
                 q_ref, k_ref, vt_ref, g_ref):
    rows = x_ref.shape[1]
    h = _dot(x_ref[0].astype(BF16), win_ref[...])

    ang = pos_ref[0].astype(F32) * inv_ref[...]
    cos_t = jnp.cos(ang)
    sin_t = jnp.sin(ang) * sgn_ref[...]

    cq = _rms(h[:, C_Q:C_Q + Q_RANK], qg_ref[...]).astype(BF16)
    q2 = _dot(cq, wq_ref[...])
    for hd in range(MLA_HEADS):
        lo = hd * LANES
        qh = q2[:, lo:lo + LANES] * cos_t + q2[:, HP + lo:HP + lo + LANES] * sin_t
        q_ref[0, :, lo:lo + LANES] = (qh * (SM_SCALE * LOG2E)).astype(BF16)

    ckv = _rms(h[:, C_KV:C_KV + KV_RANK], kvg_ref[...]).astype(BF16)
    kp = _dot(ckv, wk_ref[...])
    kr = h[:, C_KRA:C_KRA + LANES] * cos_t + h[:, C_KRB:C_KRB + LANES] * sin_t
    for hd in range(MLA_HEADS):
        lo = hd * LANES
        k_ref[0, :, lo:lo + LANES] = (kp[:, lo:lo + LANES] + kr).astype(BF16)
    vt_ref[0, 0] = _dot(ckv, wv_ref[...]).T.astype(BF16)

    u = jax.nn.gelu(h[:, C_U:C_U + GM_OUT])
    vv = jax.nn.gelu(h[:, C_V:C_V + GM_OUT])
    mu = _dot(vv.astype(BF16), gavg_ref[...])
    d = vv - mu
    var = _dot((d * d).astype(BF16), gavg_ref[...])
    vn = (d * lax.rsqrt(var + EPS) * lng_ref[...] + lnb_ref[...]).astype(BF16)

    tri = lax.broadcasted_iota(jnp.int32, (CHUNK, CHUNK), 0) >= lax.broadcasted_iota(jnp.int32, (CHUNK, CHUNK), 1)
    wm = [jnp.where(tri, ws_ref[g], 0.0).astype(BF16) for g in range(GM_GROUPS)]
    low_half = lax.broadcasted_iota(jnp.int32, (CHUNK, LANES), 1) < GM_CH
    for c in range(rows // CHUNK):
        r0 = c * CHUNK
        parts = []
        for pr in range(GM_GROUPS // 2):
            tile = vn[r0:r0 + CHUNK, pr * LANES:(pr + 1) * LANES]
            parts.append(jnp.where(low_half, _dot(wm[2 * pr], tile), _dot(wm[2 * pr + 1], tile)))
        sg = jnp.concatenate(parts, axis=1) + bias_ref[...]
        gm = u[r0:r0 + CHUNK] * sg
        g_ref[0, r0:r0 + CHUNK, :] = _rms(gm, gog_ref[...]).astype(BF16)


def _prep(x, pos3, w):
    B, S, D = x.shape
    ts = PREP_ROWS
    full = lambda a: pl.BlockSpec(a.shape, lambda b, i: (0,) * a.ndim)
    consts = [w["win"], w["qg"], w["wq"], w["kvg"], w["wk"], w["wv"], w["inv"], w["sgn"],
              w["lng"], w["lnb"], w["gavg"], w["ws"], w["bias"], w["gog"]]
    return pl.pallas_call(
        _prep_kernel,
        grid=(B, S // ts),
        in_specs=[pl.BlockSpec((1, ts, D), lambda b, i: (b, i, 0)),
                  pl.BlockSpec((1, ts, 1), lambda b, i: (b, i, 0))] + [full(a) for a in consts],
        out_specs=[pl.BlockSpec((1, ts, HP), lambda b, i: (b, i, 0)),
                   pl.BlockSpec((1, ts, HP), lambda b, i: (b, i, 0)),
                   pl.BlockSpec((1, 1, MLA_OUT, ts), lambda b, i: (b, i, 0, 0)),
                   pl.BlockSpec((1, ts, GM_OUT), lambda b, i: (b, i, 0))],
        out_shape=[jax.ShapeDtypeStruct((B, S, HP), BF16)] * 2
        + [jax.ShapeDtypeStruct((B, S // ts, MLA_OUT, ts), BF16), jax.ShapeDtypeStruct((B, S, GM_OUT), BF16)],
        compiler_params=pltpu.CompilerParams(dimension_semantics=("parallel", "parallel"),
                                             vmem_limit_bytes=VMEM_LIMIT),
        name="prep",
    )(x, pos3, *consts)


def _attn_kernel(q_ref, k_ref, vt_ref, g_ref, x_ref, woa_ref, wog_ref, mog_ref, l1g_ref, l1b_ref,
                 o_ref, m_scr, l_scr, acc_scr, sa_scr, sb_scr):
    i = pl.program_id(1)
    tq = q_ref.shape[1]
    tk = tq
    key = lax.broadcasted_iota(jnp.int32, (tk, tq), 0)
    qry = lax.broadcasted_iota(jnp.int32, (tk, tq), 1)
    diag_mask = key <= qry

    m_scr[...] = jnp.full(m_scr.shape, -1e30, F32)
    l_scr[...] = jnp.zeros(l_scr.shape, F32)
    acc_scr[...] = jnp.zeros(acc_scr.shape, F32)

    def scores(j, s_scr):
        k0 = pl.multiple_of(j * tk, tk)
        for hd in range(MLA_HEADS):
            lo = hd * LANES
            qh = q_ref[0, :, lo:lo + LANES]
            kj = k_ref[0, pl.ds(k0, tk), lo:lo + LANES]
            s_scr[hd] = lax.dot_general(kj, qh, (((1,), (1,)), ((), ())), preferred_element_type=F32)

    def update(j, s_scr, masked):
        for hd in range(MLA_HEADS):
            s = s_scr[hd]
            vt = vt_ref[0, j, hd * V_HEAD:(hd + 1) * V_HEAD, :]
            if masked:
                s = jnp.where(diag_mask, s, -1e30)
            m_prev = m_scr[hd]
            m_new = jnp.maximum(m_prev, jnp.max(s, axis=0, keepdims=True))
            p = jnp.exp2(s - m_new)
            scale = jnp.exp2(m_prev - m_new)
            l_scr[hd] = scale * l_scr[hd] + jnp.sum(p, axis=0, keepdims=True)
            acc_scr[hd] = scale * acc_scr[hd] + _dot(vt, p.astype(BF16))
            m_scr[hd] = m_new

    def pair(jj, c):
        j = 2 * jj
        scores(j + 1, sb_scr)
        update(j, sa_scr, False)
        scores(j + 2, sa_scr)
        update(j + 1, sb_scr, False)
        return c

    scores(0, sa_scr)
    lax.fori_loop(0, lax.shift_right_logical(i, 1), pair, 0)

    @pl.when((i & 1) == 0)
    def _():
        update(i, sa_scr, True)

    @pl.when((i & 1) == 1)
    def _():
        scores(i, sb_scr)
        update(i - 1, sa_scr, False)
        update(i, sb_scr, True)

    at = jnp.concatenate([acc_scr[hd] / l_scr[hd] for hd in range(MLA_HEADS)], axis=0)
    at = at * lax.rsqrt(jnp.mean(at * at, axis=0, keepdims=True) + EPS) * mog_ref[...]
    mix = _dot(at.T.astype(BF16), woa_ref[...]) + _dot(g_ref[0], wog_ref[...])
    o_ref[0] = _ln(ALPHA * x_ref[0] + mix, l1g_ref[...], l1b_ref[...])


def _attn(q, k, vt, g, x, w):
    B, S, D = x.shape
    tq = ATTN_ROWS
    full = lambda a: pl.BlockSpec(a.shape, lambda b, i: (0,) * a.ndim)
    consts = [w["woa"], w["wog"], w["mog"], w["l1g"], w["l1b"]]
    return pl.pallas_call(
        _attn_kernel,
        grid=(B, S // tq),
        in_specs=[pl.BlockSpec((1, tq, HP), lambda b, i: (b, i, 0)),
                  pl.BlockSpec((1, S, HP), lambda b, i: (b, 0, 0)),
                  pl.BlockSpec((1,) + vt.shape[1:], lambda b, i: (b, 0, 0, 0)),
                  pl.BlockSpec((1, tq, GM_OUT), lambda b, i: (b, i, 0)),
                  pl.BlockSpec((1, tq, D), lambda b, i: (b, i, 0))] + [full(a) for a in consts],
        out_specs=pl.BlockSpec((1, tq, D), lambda b, i: (b, i, 0)),
        out_shape=jax.ShapeDtypeStruct((B, S, D), F32),
        scratch_shapes=[pltpu.VMEM((MLA_HEADS, 1, tq), F32), pltpu.VMEM((MLA_HEADS, 1, tq), F32),
                        pltpu.VMEM((MLA_HEADS, V_HEAD, tq), F32),
                        pltpu.VMEM((MLA_HEADS, tq, tq), F32), pltpu.VMEM((MLA_HEADS, tq, tq), F32)],
        compiler_params=pltpu.CompilerParams(dimension_semantics=("parallel", "parallel"),
                                             vmem_limit_bytes=VMEM_LIMIT),
        name="attn",
    )(q, k, vt, g, x, *consts)


def _route_kernel(x_ref, wr_ref, br_ref, info_ref, cnt_ref, carry_scr, tri_scr):
    step = pl.program_id(0)
    tt = x_ref.shape[0]

    @pl.when(step == 0)
    def _():
        carry_scr[...] = jnp.zeros_like(carry_scr)
        r = lax.broadcasted_iota(jnp.int32, (tt, tt), 0)
        c = lax.broadcasted_iota(jnp.int32, (tt, tt), 1)
        tri_scr[...] = jnp.where(c < r, 1.0, 0.0).astype(BF16)

    x = x_ref[...]
    xh = x.astype(BF16)
    xl = (x - xh.astype(F32)).astype(BF16)
    wr = wr_ref[...]
    wh = wr.astype(BF16)
    wl = (wr - wh.astype(F32)).astype(BF16)
    logits = _dot(xh, wh) + _dot(xl, wh) + _dot(xh, wl) + br_ref[...]

    lane = lax.broadcasted_iota(jnp.int32, (tt, LANES), 1)
    neg = jnp.float32(-jnp.inf)

    is_g = lane < N_GROUPS
    lg = jnp.where(is_g, logits, neg)
    gmax = jnp.max(lg, axis=-1, keepdims=True)
    g_idx = jnp.min(jnp.where(lg == gmax, lane, LANES), axis=-1, keepdims=True)
    g_den = jnp.sum(jnp.where(is_g, jnp.exp(lg - gmax), 0.0), axis=-1, keepdims=True)
    g_p = 1.0 / g_den

    in_grp = (lane >= R_OFF) & (lane < R_OFF + N_EXPERTS) & (((lane - R_OFF) >> 3) == g_idx)
    le = jnp.where(in_grp, logits, neg)
    m1 = jnp.max(le, axis=-1, keepdims=True)
    i1 = jnp.min(jnp.where(le == m1, lane, LANES), axis=-1, keepdims=True)
    le2 = jnp.where(lane == i1, neg, le)
    m2 = jnp.max(le2, axis=-1, keepdims=True)
    i2 = jnp.min(jnp.where(le2 == m2, lane, LANES), axis=-1, keepdims=True)
    e2 = jnp.exp(m2 - m1)
    gate0 = g_p / (1.0 + e2)
    gate1 = g_p * e2 / (1.0 + e2)

    hit1 = lane == i1
    hit2 = lane == i2
    onehot = jnp.where(hit1 | hit2, 1.0, 0.0)
    before = _dot(tri_scr[...], onehot.astype(BF16)) + carry_scr[...]
    r0 = jnp.sum(jnp.where(hit1, before, 0.0), axis=-1, keepdims=True)
    r1 = jnp.sum(jnp.where(hit2, before, 0.0), axis=-1, keepdims=True)
    carry_scr[...] = carry_scr[...] + jnp.sum(onehot, axis=0, keepdims=True)
    cnt_ref[...] = carry_scr[...]

    info = jnp.where(lane == I_E0, (i1 - R_OFF).astype(F32), 0.0)
    info = jnp.where(lane == I_E1, (i2 - R_OFF).astype(F32), info)
    info = jnp.where(lane == I_R0, r0, info)
    info = jnp.where(lane == I_R1, r1, info)
    info = jnp.where(lane == I_G0, gate0, info)
    info = jnp.where(lane == I_G1, gate1, info)
    info_ref[...] = info


def _route(x1, wr, br):
    T, D = x1.shape
    tt = ROUTE_ROWS
    return pl.pallas_call(
        _route_kernel,
        grid=(T // tt,),
        in_specs=[pl.BlockSpec((tt, D), lambda i: (i, 0)),
                  pl.BlockSpec(wr.shape, lambda i: (0, 0)),
                  pl.BlockSpec(br.shape, lambda i: (0, 0))],
        out_specs=[pl.BlockSpec((tt, LANES), lambda i: (i, 0)),
                   pl.BlockSpec((1, LANES), lambda i: (0, 0))],
        out_shape=[jax.ShapeDtypeStruct((T, LANES), F32), jax.ShapeDtypeStruct((1, LANES), F32)],
        scratch_shapes=[pltpu.VMEM((1, LANES), F32), pltpu.VMEM((tt, tt), BF16)],
        compiler_params=pltpu.CompilerParams(dimension_semantics=("arbitrary",), vmem_limit_bytes=VMEM_LIMIT),
        name="route",
    )(x1, wr, br)


def _to_token_tiles(dst_ref, val):
    rows = val.shape[0]
    for c in range(SUBLANES):
        dst_ref[pl.ds(c, rows, stride=SUBLANES), :] = val[:, c * LANES:(c + 1) * LANES]


def _from_token_tiles(src_ref, rows):
    return jnp.concatenate([src_ref[pl.ds(c, rows, stride=SUBLANES), :] for c in range(SUBLANES)], axis=1)


def _tile_copy(src_ref, src_row, dst_ref, dst_row, sem):
    return pltpu.make_async_copy(src_ref.at[pl.ds(pl.multiple_of(src_row, SUBLANES), SUBLANES)],
                                 dst_ref.at[pl.ds(pl.multiple_of(dst_row, SUBLANES), SUBLANES)], sem)


def _dispatch_kernel(seg_ref, dest_ref, x_ref, buf_ref, stage_scr, zero_scr, sem, zero_sem, *, n_steps):
    i = pl.program_id(0)
    rows = x_ref.shape[0]
    slot = i % 2

    @pl.when(i == 0)
    def _():
        zero_scr[...] = jnp.zeros(zero_scr.shape, F32)

        block = EXPERT_ROWS * SUBLANES
        n_blocks = buf_ref.shape[0] // block

        def clear_rows(first):
            return pltpu.make_async_copy(zero_scr, buf_ref.at[pl.ds(pl.multiple_of(first, SUBLANES), block)], zero_sem)

        def clear(e):
            return clear_rows((seg_ref[0, e] - EXPERT_ROWS) * SUBLANES)

        def start_tail(b, c):
            clear_rows(b * block).start()
            return c

        def wait_tail(b, c):
            clear_rows(b * block).wait()
            return c

        for e in range(N_EXPERTS):
            pl.when(seg_ref[1, e] > 0)(lambda e=e: clear(e).start())
        lax.fori_loop(seg_ref[2, 0], n_blocks, start_tail, 0)
        for e in range(N_EXPERTS):
            pl.when(seg_ref[1, e] > 0)(lambda e=e: clear(e).wait())
        lax.fori_loop(seg_ref[2, 0], n_blocks, wait_tail, 0)

    def drain(s):
        for _ in range(TOP_K):
            pltpu.make_async_copy(stage_scr.at[s], stage_scr.at[s], sem.at[s]).wait()

    @pl.when(i >= 2)
    def _():
        drain(slot)

    _to_token_tiles(stage_scr.at[slot], x_ref[...])

    def start(c, carry):
        for u in range(MOVE_UNROLL):
            r = c * MOVE_UNROLL + u
            for kk in range(TOP_K):
                _tile_copy(stage_scr.at[slot], r * SUBLANES, buf_ref, dest_ref[0, 0, TOP_K * r + kk],
                           sem.at[slot]).start(priority=kk)
        return carry

    lax.fori_loop(0, rows // MOVE_UNROLL, start, 0)

    @pl.when(i == n_steps - 1)
    def _():
        drain(slot)
        if n_steps >= 2:
            drain(1 - slot)


def _dispatch(seg, dest3, x1, n_rows):
    T, D = x1.shape
    td = MOVE_ROWS
    n_steps = T // td
    grid_spec = pltpu.PrefetchScalarGridSpec(
        num_scalar_prefetch=1,
        grid=(n_steps,),
        in_specs=[pl.BlockSpec((1, 1, TOP_K * td), lambda i, seg: (i, 0, 0), memory_space=pltpu.SMEM),
                  pl.BlockSpec((td, D), lambda i, seg: (i, 0))],
        out_specs=pl.BlockSpec(memory_space=pl.ANY),
        scratch_shapes=[pltpu.VMEM((2, td * SUBLANES, LANES), F32), pltpu.VMEM((EXPERT_ROWS * SUBLANES, LANES), F32),
                        pltpu.SemaphoreType.DMA((2,)), pltpu.SemaphoreType.DMA(())],
    )
    return pl.pallas_call(
        functools.partial(_dispatch_kernel, n_steps=n_steps),
        grid_spec=grid_spec,
        out_shape=jax.ShapeDtypeStruct((n_rows * SUBLANES, LANES), F32),
        compiler_params=pltpu.CompilerParams(dimension_semantics=("arbitrary",), vmem_limit_bytes=VMEM_LIMIT),
        name="dispatch",
    )(seg, dest3, x1)


def _expert_kernel(be_ref, ne_ref, nu_ref, buf_ref, wg_hbm, wu_hbm, wd_hbm, y_ref,
                   sg_scr, su_scr, sd_scr, wg_scr, wu_scr, wd_scr, cur_ref, sem):
    b = pl.program_id(0)
    e = be_ref[b]

    def fetch(expert, s):
        return (pltpu.make_async_copy(wg_hbm.at[expert], sg_scr.at[s], sem.at[s, 0]),
                pltpu.make_async_copy(wu_hbm.at[expert], su_scr.at[s], sem.at[s, 1]),
                pltpu.make_async_copy(wd_hbm.at[expert], sd_scr.at[s], sem.at[s, 2]))

    @pl.when(b == 0)
    def _():
        cur_ref[0] = 0
        for c in fetch(e, 0):
            c.start()

    @pl.when((b == 0) | (be_ref[jnp.maximum(b - 1, 0)] != e))
    def _():
        s = cur_ref[0]
        for c in fetch(e, s):
            c.wait()
        wg_scr[...] = sg_scr[s].astype(BF16)
        wu_scr[...] = su_scr[s].astype(BF16)
        wd_scr[...] = sd_scr[s].astype(BF16)
        nxt = ne_ref[b]

        @pl.when(nxt >= 0)
        def _():
            for c in fetch(nxt, 1 - s):
                c.start()

        cur_ref[0] = 1 - s

    @pl.when(b < nu_ref[0])
    def _():
        xb = _from_token_tiles(buf_ref, EXPERT_ROWS).astype(BF16)
        hidden = jax.nn.silu(_dot(xb, wg_scr[...])) * _dot(xb, wu_scr[...])
        _to_token_tiles(y_ref, _dot(hidden.astype(BF16), wd_scr[...]))

    @pl.when(b >= nu_ref[0])
    def _():
        y_ref[...] = jnp.zeros(y_ref.shape, F32)


def _experts(block_expert, next_expert, n_used, buf, w_gate, w_up, w_down):
    bm = EXPERT_ROWS
    D, ff = w_gate.shape[1:]
    n_rows = buf.shape[0] // SUBLANES
    grid_spec = pltpu.PrefetchScalarGridSpec(
        num_scalar_prefetch=3,
        grid=(n_rows // bm,),
        in_specs=[pl.BlockSpec((bm * SUBLANES, LANES), lambda b, *_: (b, 0)),
                  pl.BlockSpec(memory_space=pl.ANY),
                  pl.BlockSpec(memory_space=pl.ANY),
                  pl.BlockSpec(memory_space=pl.ANY)],
        out_specs=pl.BlockSpec((bm * SUBLANES, LANES), lambda b, *_: (b, 0)),
        scratch_shapes=[pltpu.VMEM((2, D, ff), F32), pltpu.VMEM((2, D, ff), F32), pltpu.VMEM((2, ff, D), F32),
                        pltpu.VMEM((D, ff), BF16), pltpu.VMEM((D, ff), BF16), pltpu.VMEM((ff, D), BF16),
                        pltpu.SMEM((1,), jnp.int32), pltpu.SemaphoreType.DMA((2, 3))],
    )
    return pl.pallas_call(
        _expert_kernel,
        grid_spec=grid_spec,
        out_shape=jax.ShapeDtypeStruct(buf.shape, F32),
        compiler_params=pltpu.CompilerParams(dimension_semantics=("arbitrary",), vmem_limit_bytes=VMEM_LIMIT),
        name="experts",
    )(block_expert, next_expert, n_used, buf, w_gate, w_up, w_down)


def _final_kernel(dcur_ref, dnxt_ref, x_ref, info_ref, y_ref, p_ref, wpg_ref, bpg_ref, wpp_ref,
                  l2g_ref, l2b_ref, l3g_ref, l3b_ref, o_ref, rows_scr, sem):
    i = pl.program_id(0)
    rows = x_ref.shape[0]
    slot = i % 2

    def gather(dref, s):
        def start(c, carry):
            for u in range(MOVE_UNROLL):
                r = c * MOVE_UNROLL + u
                for kk in range(TOP_K):
                    _tile_copy(y_ref, dref[0, 0, TOP_K * r + kk], rows_scr.at[s, kk], r * SUBLANES,
                               sem.at[s]).start(priority=kk)
            return carry

        lax.fori_loop(0, rows // MOVE_UNROLL, start, 0)

    @pl.when(i == 0)
    def _():
        gather(dcur_ref, 0)

    @pl.when(i + 1 < pl.num_programs(0))
    def _():
        gather(dnxt_ref, 1 - slot)

    pp = _dot(p_ref[...].astype(BF16), wpp_ref[...])
    pltpu.make_async_copy(rows_scr.at[slot], rows_scr.at[slot], sem.at[slot]).wait()

    info = info_ref[...]
    gate0 = info[:, I_G0:I_G0 + 1]
    gate1 = info[:, I_G1:I_G1 + 1]
    moe = (_from_token_tiles(rows_scr.at[slot, 0], rows) * gate0
           + _from_token_tiles(rows_scr.at[slot, 1], rows) * gate1)
    x2 = _ln(ALPHA * x_ref[...] + moe, l2g_ref[...], l2b_ref[...])
    gate = jax.nn.sigmoid(_dot(x2.astype(BF16), wpg_ref[...]) + bpg_ref[...])
    o_ref[...] = _ln(ALPHA * x2 + gate * pp, l3g_ref[...], l3b_ref[...])


def _final(dest3, x1, info, y, p2, w):
    T, D = x1.shape
    tc = MOVE_ROWS
    pd = p2.shape[1]
    full = lambda a: pl.BlockSpec(a.shape, lambda i: (0,) * a.ndim)
    consts = [w["wpg"], w["bpg"], w["wpp"], w["l2g"], w["l2b"], w["l3g"], w["l3b"]]
    last = T // tc - 1
    return pl.pallas_call(
        _final_kernel,
        grid=(T // tc,),
        in_specs=[pl.BlockSpec((1, 1, TOP_K * tc), lambda i: (i, 0, 0), memory_space=pltpu.SMEM),
                  pl.BlockSpec((1, 1, TOP_K * tc), lambda i: (jnp.minimum(i + 1, last), 0, 0), memory_space=pltpu.SMEM),
                  pl.BlockSpec((tc, D), lambda i: (i, 0)),
                  pl.BlockSpec((tc, LANES), lambda i: (i, 0)),
                  pl.BlockSpec(memory_space=pl.ANY),
                  pl.BlockSpec((tc, pd), lambda i: (i, 0))] + [full(a) for a in consts],
        out_specs=pl.BlockSpec((tc, D), lambda i: (i, 0)),
        out_shape=jax.ShapeDtypeStruct((T, D), F32),
        scratch_shapes=[pltpu.VMEM((2, TOP_K, tc * SUBLANES, LANES), F32), pltpu.SemaphoreType.DMA((2,))],
        compiler_params=pltpu.CompilerParams(dimension_semantics=("arbitrary",), vmem_limit_bytes=VMEM_LIMIT),
        name="final",
    )(dest3, dest3, x1, info, y, p2, *consts)


def _pad_heads(a, width):
    lead = a.shape[:-1]
    a = a.reshape(lead + (MLA_HEADS, width))
    a = jnp.pad(a, [(0, 0)] * len(lead) + [(0, 0), (0, LANES - width)])
    return a.reshape(lead + (HP,))


def _layer_weights(w_in, q_norm_g, w_q_up, kv_norm_g, w_kv_up, gm_ln_g, gm_ln_b, gm_w_s, gm_b_s,
                   mla_out_g, gm_out_g, w_o, ln1_g, ln1_b):
    D = w_in.shape[0]
    half = QK_ROPE // 2
    c1, c2, c3 = Q_RANK, Q_RANK + KV_RANK, Q_RANK + KV_RANK + QK_ROPE
    wkr = w_in[:, c2:c3]
    zeros = lambda *s: jnp.zeros(s, F32)
    kra = jnp.concatenate([zeros(D, QK_NOPE), wkr, zeros(D, LANES - QK_NOPE - QK_ROPE)], axis=1)
    krb = jnp.concatenate([zeros(D, QK_NOPE), wkr[:, half:], wkr[:, :half], zeros(D, LANES - QK_NOPE - QK_ROPE)], axis=1)
    win = jnp.concatenate([w_in[:, :c2], kra, krb, w_in[:, c3:]], axis=1).astype(BF16)

    wq3 = w_q_up.reshape(Q_RANK, MLA_HEADS, QK_NOPE + QK_ROPE)
    rope = wq3[..., QK_NOPE:]
    rope_sw = jnp.concatenate([rope[..., half:], rope[..., :half]], axis=-1)
    sw3 = jnp.concatenate([jnp.zeros_like(wq3[..., :QK_NOPE]), rope_sw], axis=-1)
    wq = jnp.concatenate([_pad_heads(w_q_up, QK_NOPE + QK_ROPE),
                          _pad_heads(sw3.reshape(Q_RANK, -1), QK_NOPE + QK_ROPE)], axis=1).astype(BF16)

    wkv3 = w_kv_up.reshape(KV_RANK, MLA_HEADS, QK_NOPE + V_HEAD)
    wk = _pad_heads(wkv3[..., :QK_NOPE].reshape(KV_RANK, -1), QK_NOPE).astype(BF16)
    wv = wkv3[..., QK_NOPE:].reshape(KV_RANK, -1).astype(BF16)

    inv = ROPE_THETA ** (-jnp.arange(0, QK_ROPE, 2, dtype=F32) / QK_ROPE)
    pad_l, pad_r = jnp.zeros((QK_NOPE,), F32), jnp.zeros((LANES - QK_NOPE - QK_ROPE,), F32)
    inv_t = jnp.concatenate([pad_l, inv, inv, pad_r])[None, :]
    sgn_t = jnp.concatenate([pad_l, -jnp.ones((half,), F32), jnp.ones((half,), F32), pad_r])[None, :]

    grp = jnp.arange(GM_OUT) // GM_CH
    gavg = jnp.where(grp[:, None] == grp[None, :], 1.0 / GM_CH, 0.0).astype(BF16)
    bias = jnp.repeat(gm_b_s.T, GM_CH, axis=1)

    woa = w_o[:MLA_OUT].astype(BF16)
    wog = w_o[MLA_OUT:].astype(BF16)
    return dict(win=win, qg=q_norm_g[None, :], wq=wq, kvg=kv_norm_g[None, :], wk=wk, wv=wv, inv=inv_t, sgn=sgn_t,
                lng=gm_ln_g[None, :], lnb=gm_ln_b[None, :], gavg=gavg, ws=gm_w_s, bias=bias, gog=gm_out_g[None, :],
                woa=woa, wog=wog, mog=mla_out_g[:, None], l1g=ln1_g[None, :], l1b=ln1_b[None, :])


def _moe(x1, w_rg, b_rg, w_re, b_re, w_gate, w_up, w_down):
    T, D = x1.shape
    pad = jnp.zeros((D, LANES - N_GROUPS - N_EXPERTS), F32)
    wr = jnp.concatenate([w_rg, w_re, pad], axis=1)
    br = jnp.concatenate([b_rg, b_re, pad[0]])[None, :]
    info, cnt = _route(x1, wr, br)

    bm = EXPERT_ROWS
    n_blocks = (T * TOP_K) // bm + N_EXPERTS
    counts = cnt[0, R_OFF:R_OFF + N_EXPERTS].astype(jnp.int32)
    padded = (counts + bm - 1) // bm * bm
    pad_ends = jnp.cumsum(padded)
    pad_starts = pad_ends - padded
    e_idx = info[:, I_E0:I_E1 + 1].astype(jnp.int32)
    rank = info[:, I_R0:I_R1 + 1].astype(jnp.int32)
    seg_start = jnp.sum(jnp.where(e_idx[..., None] == jnp.arange(N_EXPERTS), pad_starts, 0), axis=-1)
    dest = ((seg_start + rank) * SUBLANES).reshape(T // MOVE_ROWS, 1, TOP_K * MOVE_ROWS)
    block_start = jnp.arange(n_blocks, dtype=jnp.int32) * bm
    block_expert = jnp.minimum(jnp.sum(pad_ends[None, :] <= block_start[:, None], axis=1),
                               N_EXPERTS - 1).astype(jnp.int32)

    blk = jnp.arange(n_blocks)
    later = (blk[None, :] > blk[:, None]) & (block_expert[None, :] != block_expert[:, None])
    next_expert = jnp.min(jnp.where(later, block_expert[None, :], N_EXPERTS), axis=1)
    next_expert = jnp.where(next_expert == N_EXPERTS, -1, next_expert).astype(jnp.int32)
    n_used = (pad_ends[-1:] // bm).astype(jnp.int32)

    seg = jnp.stack([pad_ends, padded, jnp.broadcast_to(n_used, (N_EXPERTS,))]).astype(jnp.int32)
    buf = _dispatch(seg, dest, x1, n_blocks * bm)
    y = _experts(block_expert, next_expert, n_used, buf, w_gate, w_up, w_down)
    return info, dest, y


def kernel(x, p, positions, w_in, q_norm_g, w_q_up, kv_norm_g, w_kv_up, gm_ln_g, gm_ln_b, gm_w_s, gm_b_s, mla_out_g, gm_out_g, w_o, ln1_g, ln1_b, w_rg, b_rg, w_re, b_re, w_gate, w_up, w_down, ln2_g, ln2_b, w_pg, b_pg, w_pp, ln3_g, ln3_b):
    B, S, D = x.shape
    T = B * S
    assert S % ATTN_ROWS == 0 and PREP_ROWS == ATTN_ROWS and PREP_ROWS % CHUNK == 0
    assert T % ROUTE_ROWS == 0 and T % MOVE_ROWS == 0 and (T * TOP_K) % EXPERT_ROWS == 0
    assert D == SUBLANES * LANES and MOVE_ROWS % MOVE_UNROLL == 0
    pos3 = positions.reshape(B, S, 1)
    for i in range(DEPTH):
        w = _layer_weights(w_in[i], q_norm_g[i], w_q_up[i], kv_norm_g[i], w_kv_up[i], gm_ln_g[i], gm_ln_b[i],
                           gm_w_s[i], gm_b_s[i], mla_out_g[i], gm_out_g[i], w_o[i], ln1_g[i], ln1_b[i])
        q, k, vt, g = _prep(x, pos3, w)
        x1 = _attn(q, k, vt, g, x, w).reshape(T, D)
        info, dest, y = _moe(x1, w_rg[i], b_rg[i], w_re[i], b_re[i], w_gate[i], w_up[i], w_down[i])
        wf = dict(wpg=w_pg[i].astype(BF16), bpg=b_pg[i][None, :], wpp=w_pp[i].astype(BF16),
                  l2g=ln2_g[i][None, :], l2b=ln2_b[i][None, :], l3g=ln3_g[i][None, :], l3b=ln3_b[i][None, :])
        x = _final(dest, x1, info, y, p[i].reshape(T, -1), wf).reshape(B, S, D)
    return x
```

```python
import functools

import jax
import jax.numpy as jnp
from jax import lax
from jax.experimental import pallas as pl
from jax.experimental.pallas import tpu as pltpu

F32 = jnp.float32
BF16 = jnp.bfloat16

MLA_HEADS = 8
QK_NOPE = 64
QK_ROPE = 32
V_HEAD = 64
Q_RANK = 256
KV_RANK = 128
ROPE_THETA = 10000.0
MLA_OUT = MLA_HEADS * V_HEAD
GM_GROUPS = 8
GM_CH = 64
GM_OUT = GM_GROUPS * GM_CH
CHUNK = 128
N_GROUPS = 4
EXP_PER_GROUP = 8
N_EXPERTS = N_GROUPS * EXP_PER_GROUP
TOP_K = 2
EPS = 1e-6
DEPTH = 1
ALPHA = (2.0 * DEPTH) ** 0.25
SM_SCALE = (QK_NOPE + QK_ROPE) ** -0.5
LOG2E = 1.4426950408889634

LANES = 128
SUBLANES = 8
VMEM_LIMIT = 56 * 1024 * 1024

PREP_ROWS = 256
ATTN_ROWS = 256
ROUTE_ROWS = 512
MOVE_ROWS = 256
MOVE_UNROLL = 8
EXPERT_ROWS = 256

C_Q = 0
C_KV = C_Q + Q_RANK
C_KRA = C_KV + KV_RANK
C_KRB = C_KRA + LANES
C_U = C_KRB + LANES
C_V = C_U + GM_OUT
C_END = C_V + GM_OUT
HP = MLA_HEADS * LANES

I_E0, I_E1, I_R0, I_R1, I_G0, I_G1 = range(6)
R_OFF = N_GROUPS


def _rms(v, g):
    return v * lax.rsqrt(jnp.mean(v * v, axis=-1, keepdims=True) + EPS) * g


def _ln(v, g, b):
    mu = jnp.mean(v, axis=-1, keepdims=True)
    d = v - mu
    var = jnp.mean(d * d, axis=-1, keepdims=True)
    return d * lax.rsqrt(var + EPS) * g + b


def _dot(a, b):
    return jnp.dot(a, b, preferred_element_type=F32)


def _prep_kernel(x_ref, pos_ref, win_ref, qg_ref, wq_ref, kvg_ref, wk_ref, wv_ref, inv_ref, sgn_ref,
                 lng_ref, lnb_ref, gavg_ref, ws_ref, bias_ref, gog_ref,
                 q_ref, k_ref, vt_ref, g_ref):
    rows = x_ref.shape[1]
    h = _dot(x_ref[0].astype(BF16), win_ref[...])

    ang = pos_ref[0].astype(F32) * inv_ref[...]
    cos_t = jnp.cos(ang)
    sin_t = jnp.sin(ang) * sgn_ref[...]

    cq = _rms(h[:, C_Q:C_Q + Q_RANK], qg_ref[...]).astype(BF16)
    q2 = _dot(cq, wq_ref[...])
    for hd in range(MLA_HEADS):
        lo = hd * LANES
        qh = q2[:, lo:lo + LANES] * cos_t + q2[:, HP + lo:HP + lo + LANES] * sin_t
        q_ref[0, :, lo:lo + LANES] = (qh * (SM_SCALE * LOG2E)).astype(BF16)

    ckv = _rms(h[:, C_KV:C_KV + KV_RANK], kvg_ref[...]).astype(BF16)
    kp = _dot(ckv, wk_ref[...])
    kr = h[:, C_KRA:C_KRA + LANES] * cos_t + h[:, C_KRB:C_KRB + LANES] * sin_t
    for hd in range(MLA_HEADS):
        lo = hd * LANES
        k_ref[0, :, lo:lo + LANES] = (kp[:, lo:lo + LANES] + kr).astype(BF16)
    vt_ref[0, 0] = _dot(ckv, wv_ref[...]).T.astype(BF16)

    u = jax.nn.gelu(h[:, C_U:C_U + GM_OUT])
    vv = jax.nn.gelu(h[:, C_V:C_V + GM_OUT])
    mu = _dot(vv.astype(BF16), gavg_ref[...])
    d = vv - mu
    var = _dot((d * d).astype(BF16), gavg_ref[...])
    vn = (d * lax.rsqrt(var + EPS) * lng_ref[...] + lnb_ref[...]).astype(BF16)

    tri = lax.broadcasted_iota(jnp.int32, (CHUNK, CHUNK), 0) >= lax.broadcasted_iota(jnp.int32, (CHUNK, CHUNK), 1)
    wm = [jnp.where(tri, ws_ref[g], 0.0).astype(BF16) for g in range(GM_GROUPS)]
    low_half = lax.broadcasted_iota(jnp.int32, (CHUNK, LANES), 1) < GM_CH
    for c in range(rows // CHUNK):
        r0 = c * CHUNK
        parts = []
        for pr in range(GM_GROUPS // 2):
            tile = vn[r0:r0 + CHUNK, pr * LANES:(pr + 1) * LANES]
            parts.append(jnp.where(low_half, _dot(wm[2 * pr], tile), _dot(wm[2 * pr + 1], tile)))
        sg = jnp.concatenate(parts, axis=1) + bias_ref[...]
        gm = u[r0:r0 + CHUNK] * sg
        g_ref[0, r0:r0 + CHUNK, :] = _rms(gm, gog_ref[...]).astype(BF16)


def _prep(x, pos3, w):
    B, S, D = x.shape
    ts = PREP_ROWS
    full = lambda a: pl.BlockSpec(a.shape, lambda b, i: (0,) * a.ndim)
    consts = [w["win"], w["qg"], w["wq"], w["kvg"], w["wk"], w["wv"], w["inv"], w["sgn"],
              w["lng"], w["lnb"], w["gavg"], w["ws"], w["bias"], w["gog"]]
    return pl.pallas_call(
        _prep_kernel,
        grid=(B, S // ts),
        in_specs=[pl.BlockSpec((1, ts, D), lambda b, i: (b, i, 0)),
                  pl.BlockSpec((1, ts, 1), lambda b, i: (b, i, 0))] + [full(a) for a in consts],
        out_specs=[pl.BlockSpec((1, ts, HP), lambda b, i: (b, i, 0)),
                   pl.BlockSpec((1, ts, HP), lambda b, i: (b, i, 0)),
                   pl.BlockSpec((1, 1, MLA_OUT, ts), lambda b, i: (b, i, 0, 0)),
                   pl.BlockSpec((1, ts, GM_OUT), lambda b, i: (b, i, 0))],
        out_shape=[jax.ShapeDtypeStruct((B, S, HP), BF16)] * 2
        + [jax.ShapeDtypeStruct((B, S // ts, MLA_OUT, ts), BF16), jax.ShapeDtypeStruct((B, S, GM_OUT), BF16)],
        compiler_params=pltpu.CompilerParams(dimension_semantics=("parallel", "parallel"),
                                             vmem_limit_bytes=VMEM_LIMIT),
        name="prep",
    )(x, pos3, *consts)


def _attn_kernel(q_ref, k_ref, vt_ref, g_ref, x_ref, woa_ref, wog_ref, mog_ref, l1g_ref, l1b_ref,
                 o_ref, m_scr, l_scr, acc_scr, sa_scr, sb_scr):
    i = pl.program_id(1)
    tq = q_ref.shape[1]
    tk = tq
    key = lax.broadcasted_iota(jnp.int32, (tk, tq), 0)
    qry = lax.broadcasted_iota(jnp.int32, (tk, tq), 1)
    diag_mask = key <= qry

    m_scr[...] = jnp.full(m_scr.shape, -1e30, F32)
    l_scr[...] = jnp.zeros(l_scr.shape, F32)
    acc_scr[...] = jnp.zeros(acc_scr.shape, F32)

    def scores(j, s_scr):
        k0 = pl.multiple_of(j * tk, tk)
        for hd in range(MLA_HEADS):
            lo = hd * LANES
            qh = q_ref[0, :, lo:lo + LANES]
            kj = k_ref[0, pl.ds(k0, tk), lo:lo + LANES]
            s_scr[hd] = lax.dot_general(kj, qh, (((1,), (1,)), ((), ())), preferred_element_type=F32)

    def update(j, s_scr, masked):
        for hd in range(MLA_HEADS):
            s = s_scr[hd]
            vt = vt_ref[0, j, hd * V_HEAD:(hd + 1) * V_HEAD, :]
            if masked:
                s = jnp.where(diag_mask, s, -1e30)
            m_prev = m_scr[hd]
            m_new = jnp.maximum(m_prev, jnp.max(s, axis=0, keepdims=True))
            p = jnp.exp2(s - m_new)
            scale = jnp.exp2(m_prev - m_new)
            l_scr[hd] = scale * l_scr[hd] + jnp.sum(p, axis=0, keepdims=True)
            acc_scr[hd] = scale * acc_scr[hd] + _dot(vt, p.astype(BF16))
            m_scr[hd] = m_new

    def pair(jj, c):
        j = 2 * jj
        scores(j + 1, sb_scr)
        update(j, sa_scr, False)
        scores(j + 2, sa_scr)
        update(j + 1, sb_scr, False)
        return c

    scores(0, sa_scr)
    lax.fori_loop(0, lax.shift_right_logical(i, 1), pair, 0)

    @pl.when((i & 1) == 0)
    def _():
        update(i, sa_scr, True)

    @pl.when((i & 1) == 1)
    def _():
        scores(i, sb_scr)
        update(i - 1, sa_scr, False)
        update(i, sb_scr, True)

    at = jnp.concatenate([acc_scr[hd] / l_scr[hd] for hd in range(MLA_HEADS)], axis=0)
    at = at * lax.rsqrt(jnp.mean(at * at, axis=0, keepdims=True) + EPS) * mog_ref[...]
    mix = _dot(at.T.astype(BF16), woa_ref[...]) + _dot(g_ref[0], wog_ref[...])
    o_ref[0] = _ln(ALPHA * x_ref[0] + mix, l1g_ref[...], l1b_ref[...])


def _attn(q, k, vt, g, x, w):
    B, S, D = x.shape
    tq = ATTN_ROWS
    full = lambda a: pl.BlockSpec(a.shape, lambda b, i: (0,) * a.ndim)
    consts = [w["woa"], w["wog"], w["mog"], w["l1g"], w["l1b"]]
    return pl.pallas_call(
        _attn_kernel,
        grid=(B, S // tq),
        in_specs=[pl.BlockSpec((1, tq, HP), lambda b, i: (b, i, 0)),
                  pl.BlockSpec((1, S, HP), lambda b, i: (b, 0, 0)),
                  pl.BlockSpec((1,) + vt.shape[1:], lambda b, i: (b, 0, 0, 0)),
                  pl.BlockSpec((1, tq, GM_OUT), lambda b, i: (b, i, 0)),
                  pl.BlockSpec((1, tq, D), lambda b, i: (b, i, 0))] + [full(a) for a in consts],
        out_specs=pl.BlockSpec((1, tq, D), lambda b, i: (b, i, 0)),
        out_shape=jax.ShapeDtypeStruct((B, S, D), F32),
        scratch_shapes=[pltpu.VMEM((MLA_HEADS, 1, tq), F32), pltpu.VMEM((MLA_HEADS, 1, tq), F32),
                        pltpu.VMEM((MLA_HEADS, V_HEAD, tq), F32),
                        pltpu.VMEM((MLA_HEADS, tq, tq), F32), pltpu.VMEM((MLA_HEADS, tq, tq), F32)],
        compiler_params=pltpu.CompilerParams(dimension_semantics=("parallel", "parallel"),
                                             vmem_limit_bytes=VMEM_LIMIT),
        name="attn",
    )(q, k, vt, g, x, *consts)


def _route_kernel(x_ref, wr_ref, br_ref, info_ref, cnt_ref, carry_scr, tri_scr):
    step = pl.program_id(0)
    tt = x_ref.shape[0]

    @pl.when(step == 0)
    def _():
        carry_scr[...] = jnp.zeros_like(carry_scr)
        r = lax.broadcasted_iota(jnp.int32, (tt, tt), 0)
        c = lax.broadcasted_iota(jnp.int32, (tt, tt), 1)
        tri_scr[...] = jnp.where(c < r, 1.0, 0.0).astype(BF16)

    x = x_ref[...]
    xh = x.astype(BF16)
    xl = (x - xh.astype(F32)).astype(BF16)
    wr = wr_ref[...]
    wh = wr.astype(BF16)
    wl = (wr - wh.astype(F32)).astype(BF16)
    logits = _dot(xh, wh) + _dot(xl, wh) + _dot(xh, wl) + br_ref[...]

    lane = lax.broadcasted_iota(jnp.int32, (tt, LANES), 1)
    neg = jnp.float32(-jnp.inf)

    is_g = lane < N_GROUPS
    lg = jnp.where(is_g, logits, neg)
    gmax = jnp.max(lg, axis=-1, keepdims=True)
    g_idx = jnp.min(jnp.where(lg == gmax, lane, LANES), axis=-1, keepdims=True)
    g_den = jnp.sum(jnp.where(is_g, jnp.exp(lg - gmax), 0.0), axis=-1, keepdims=True)
    g_p = 1.0 / g_den

    in_grp = (lane >= R_OFF) & (lane < R_OFF + N_EXPERTS) & (((lane - R_OFF) >> 3) == g_idx)
    le = jnp.where(in_grp, logits, neg)
    m1 = jnp.max(le, axis=-1, keepdims=True)
    i1 = jnp.min(jnp.where(le == m1, lane, LANES), axis=-1, keepdims=True)
    le2 = jnp.where(lane == i1, neg, le)
    m2 = jnp.max(le2, axis=-1, keepdims=True)
    i2 = jnp.min(jnp.where(le2 == m2, lane, LANES), axis=-1, keepdims=True)
    e2 = jnp.exp(m2 - m1)
    gate0 = g_p / (1.0 + e2)
    gate1 = g_p * e2 / (1.0 + e2)

    hit1 = lane == i1
    hit2 = lane == i2
    onehot = jnp.where(hit1 | hit2, 1.0, 0.0)
    before = _dot(tri_scr[...], onehot.astype(BF16)) + carry_scr[...]
    r0 = jnp.sum(jnp.where(hit1, before, 0.0), axis=-1, keepdims=True)
    r1 = jnp.sum(jnp.where(hit2, before, 0.0), axis=-1, keepdims=True)
    carry_scr[...] = carry_scr[...] + jnp.sum(onehot, axis=0, keepdims=True)
    cnt_ref[...] = carry_scr[...]

    info = jnp.where(lane == I_E0, (i1 - R_OFF).astype(F32), 0.0)
    info = jnp.where(lane == I_E1, (i2 - R_OFF).astype(F32), info)
    info = jnp.where(lane == I_R0, r0, info)
    info = jnp.where(lane == I_R1, r1, info)
    info = jnp.where(lane == I_G0, gate0, info)
    info = jnp.where(lane == I_G1, gate1, info)
    info_ref[...] = info


def _route(x1, wr, br):
    T, D = x1.shape
    tt = ROUTE_ROWS
    return pl.pallas_call(
        _route_kernel,
        grid=(T // tt,),
        in_specs=[pl.BlockSpec((tt, D), lambda i: (i, 0)),
                  pl.BlockSpec(wr.shape, lambda i: (0, 0)),
                  pl.BlockSpec(br.shape, lambda i: (0, 0))],
        out_specs=[pl.BlockSpec((tt, LANES), lambda i: (i, 0)),
                   pl.BlockSpec((1, LANES), lambda i: (0, 0))],
        out_shape=[jax.ShapeDtypeStruct((T, LANES), F32), jax.ShapeDtypeStruct((1, LANES), F32)],
        scratch_shapes=[pltpu.VMEM((1, LANES), F32), pltpu.VMEM((tt, tt), BF16)],
        compiler_params=pltpu.CompilerParams(dimension_semantics=("arbitrary",), vmem_limit_bytes=VMEM_LIMIT),
        name="route",
    )(x1, wr, br)


def _to_token_tiles(dst_ref, val):
    rows = val.shape[0]
    for c in range(SUBLANES):
        dst_ref[pl.ds(c, rows, stride=SUBLANES), :] = val[:, c * LANES:(c + 1) * LANES]


def _from_token_tiles(src_ref, rows):
    return jnp.concatenate([src_ref[pl.ds(c, rows, stride=SUBLANES), :] for c in range(SUBLANES)], axis=1)


def _tile_copy(src_ref, src_row, dst_ref, dst_row, sem):
    return pltpu.make_async_copy(src_ref.at[pl.ds(pl.multiple_of(src_row, SUBLANES), SUBLANES)],
                                 dst_ref.at[pl.ds(pl.multiple_of(dst_row, SUBLANES), SUBLANES)], sem)


def _dispatch_kernel(seg_ref, dest_ref, x_ref, buf_ref, stage_scr, zero_scr, sem, zero_sem, *, n_steps):
    i = pl.program_id(0)
    rows = x_ref.shape[0]
    slot = i % 2

    @pl.when(i == 0)
    def _():
        zero_scr[...] = jnp.zeros(zero_scr.shape, F32)

        block = EXPERT_ROWS * SUBLANES
        n_blocks = buf_ref.shape[0] // block

        def clear_rows(first):
            return pltpu.make_async_copy(zero_scr, buf_ref.at[pl.ds(pl.multiple_of(first, SUBLANES), block)], zero_sem)

        def clear(e):
            return clear_rows((seg_ref[0, e] - EXPERT_ROWS) * SUBLANES)

        def start_tail(b, c):
            clear_rows(b * block).start()
            return c

        def wait_tail(b, c):
            clear_rows(b * block).wait()
            return c

        for e in range(N_EXPERTS):
            pl.when(seg_ref[1, e] > 0)(lambda e=e: clear(e).start())
        lax.fori_loop(seg_ref[2, 0], n_blocks, start_tail, 0)
        for e in range(N_EXPERTS):
            pl.when(seg_ref[1, e] > 0)(lambda e=e: clear(e).wait())
        lax.fori_loop(seg_ref[2, 0], n_blocks, wait_tail, 0)

    def drain(s):
        for _ in range(TOP_K):
            pltpu.make_async_copy(stage_scr.at[s], stage_scr.at[s], sem.at[s]).wait()

    @pl.when(i >= 2)
    def _():
        drain(slot)

    _to_token_tiles(stage_scr.at[slot], x_ref[...])

    def start(c, carry):
        for u in range(MOVE_UNROLL):
            r = c * MOVE_UNROLL + u
            for kk in range(TOP_K):
                _tile_copy(stage_scr.at[slot], r * SUBLANES, buf_ref, dest_ref[0, 0, TOP_K * r + kk],
                           sem.at[slot]).start(priority=kk)
        return carry

    lax.fori_loop(0, rows // MOVE_UNROLL, start, 0)

    @pl.when(i == n_steps - 1)
    def _():
        drain(slot)
        if n_steps >= 2:
            drain(1 - slot)


def _dispatch(seg, dest3, x1, n_rows):
    T, D = x1.shape
    td = MOVE_ROWS
    n_steps = T // td
    grid_spec = pltpu.PrefetchScalarGridSpec(
        num_scalar_prefetch=1,
        grid=(n_steps,),
        in_specs=[pl.BlockSpec((1, 1, TOP_K * td), lambda i, seg: (i, 0, 0), memory_space=pltpu.SMEM),
                  pl.BlockSpec((td, D), lambda i, seg: (i, 0))],
        out_specs=pl.BlockSpec(memory_space=pl.ANY),
        scratch_shapes=[pltpu.VMEM((2, td * SUBLANES, LANES), F32), pltpu.VMEM((EXPERT_ROWS * SUBLANES, LANES), F32),
                        pltpu.SemaphoreType.DMA((2,)), pltpu.SemaphoreType.DMA(())],
    )
    return pl.pallas_call(
        functools.partial(_dispatch_kernel, n_steps=n_steps),
        grid_spec=grid_spec,
        out_shape=jax.ShapeDtypeStruct((n_rows * SUBLANES, LANES), F32),
        compiler_params=pltpu.CompilerParams(dimension_semantics=("arbitrary",), vmem_limit_bytes=VMEM_LIMIT),
        name="dispatch",
    )(seg, dest3, x1)


def _expert_kernel(be_ref, ne_ref, nu_ref, buf_ref, wg_hbm, wu_hbm, wd_hbm, y_ref,
                   sg_scr, su_scr, sd_scr, wg_scr, wu_scr, wd_scr, cur_ref, sem):
    b = pl.program_id(0)
    e = be_ref[b]

    def fetch(expert, s):
        return (pltpu.make_async_copy(wg_hbm.at[expert], sg_scr.at[s], sem.at[s, 0]),
                pltpu.make_async_copy(wu_hbm.at[expert], su_scr.at[s], sem.at[s, 1]),
                pltpu.make_async_copy(wd_hbm.at[expert], sd_scr.at[s], sem.at[s, 2]))

    @pl.when(b == 0)
    def _():
        cur_ref[0] = 0
        for c in fetch(e, 0):
            c.start()

    @pl.when((b == 0) | (be_ref[jnp.maximum(b - 1, 0)] != e))
    def _():
        s = cur_ref[0]
        for c in fetch(e, s):
            c.wait()
        wg_scr[...] = sg_scr[s].astype(BF16)
        wu_scr[...] = su_scr[s].astype(BF16)
        wd_scr[...] = sd_scr[s].astype(BF16)
        nxt = ne_ref[b]

        @pl.when(nxt >= 0)
        def _():
            for c in fetch(nxt, 1 - s):
                c.start()

        cur_ref[0] = 1 - s

    @pl.when(b < nu_ref[0])
    def _():
        xb = _from_token_tiles(buf_ref, EXPERT_ROWS).astype(BF16)
        hidden = jax.nn.silu(_dot(xb, wg_scr[...])) * _dot(xb, wu_scr[...])
        _to_token_tiles(y_ref, _dot(hidden.astype(BF16), wd_scr[...]))

    @pl.when(b >= nu_ref[0])
    def _():
        y_ref[...] = jnp.zeros(y_ref.shape, F32)


def _experts(block_expert, next_expert, n_used, buf, w_gate, w_up, w_down):
    bm = EXPERT_ROWS
    D, ff = w_gate.shape[1:]
    n_rows = buf.shape[0] // SUBLANES
    grid_spec = pltpu.PrefetchScalarGridSpec(
        num_scalar_prefetch=3,
        grid=(n_rows // bm,),
        in_specs=[pl.BlockSpec((bm * SUBLANES, LANES), lambda b, *_: (b, 0)),
                  pl.BlockSpec(memory_space=pl.ANY),
                  pl.BlockSpec(memory_space=pl.ANY),
                  pl.BlockSpec(memory_space=pl.ANY)],
        out_specs=pl.BlockSpec((bm * SUBLANES, LANES), lambda b, *_: (b, 0)),
        scratch_shapes=[pltpu.VMEM((2, D, ff), F32), pltpu.VMEM((2, D, ff), F32), pltpu.VMEM((2, ff, D), F32),
                        pltpu.VMEM((D, ff), BF16), pltpu.VMEM((D, ff), BF16), pltpu.VMEM((ff, D), BF16),
                        pltpu.SMEM((1,), jnp.int32), pltpu.SemaphoreType.DMA((2, 3))],
    )
    return pl.pallas_call(
        _expert_kernel,
        grid_spec=grid_spec,
        out_shape=jax.ShapeDtypeStruct(buf.shape, F32),
        compiler_params=pltpu.CompilerParams(dimension_semantics=("arbitrary",), vmem_limit_bytes=VMEM_LIMIT),
        name="experts",
    )(block_expert, next_expert, n_used, buf, w_gate, w_up, w_down)


def _final_kernel(dcur_ref, dnxt_ref, x_ref, info_ref, y_ref, p_ref, wpg_ref, bpg_ref, wpp_ref,
                  l2g_ref, l2b_ref, l3g_ref, l3b_ref, o_ref, rows_scr, sem):
    i = pl.program_id(0)
    last = pl.num_programs(0) - 1
    rows = x_ref.shape[0]
    slot = i % 2

    def row_copy(dref, s, r, kk):
        return _tile_copy(y_ref, dref[0, 0, TOP_K * r + kk], rows_scr.at[s, kk], r * SUBLANES, sem.at[s])

    def landed(s):
        pltpu.make_async_copy(rows_scr.at[s], rows_scr.at[s], sem.at[s]).wait()

    @pl.when(i == 0)
    def _():
        def start(c, carry):
            for u in range(MOVE_UNROLL):
                for kk in range(TOP_K):
                    row_copy(dcur_ref, 0, c * MOVE_UNROLL + u, kk).start(priority=kk)
            return carry

        lax.fori_loop(0, rows // MOVE_UNROLL, start, 0)

    landed(slot)
    info = info_ref[...]
    gate0 = info[:, I_G0:I_G0 + 1]
    gate1 = info[:, I_G1:I_G1 + 1]
    moe = (_from_token_tiles(rows_scr.at[slot, 0], rows) * gate0
           + _from_token_tiles(rows_scr.at[slot, 1], rows) * gate1)

    for r in range(rows):
        for kk in range(TOP_K):
            row_copy(dnxt_ref, 1 - slot, r, kk).start(priority=kk)

    pp = _dot(p_ref[...].astype(BF16), wpp_ref[...])
    x2 = _ln(ALPHA * x_ref[...] + moe, l2g_ref[...], l2b_ref[...])
    gate = jax.nn.sigmoid(_dot(x2.astype(BF16), wpg_ref[...]) + bpg_ref[...])
    o_ref[...] = _ln(ALPHA * x2 + gate * pp, l3g_ref[...], l3b_ref[...])

    @pl.when(i == last)
    def _():
        landed(1 - slot)


def _final(dest3, x1, info, y, p2, w):
    T, D = x1.shape
    tc = MOVE_ROWS
    pd = p2.shape[1]
    full = lambda a: pl.BlockSpec(a.shape, lambda i: (0,) * a.ndim)
    consts = [w["wpg"], w["bpg"], w["wpp"], w["l2g"], w["l2b"], w["l3g"], w["l3b"]]
    last = T // tc - 1
    return pl.pallas_call(
        _final_kernel,
        grid=(T // tc,),
        in_specs=[pl.BlockSpec((1, 1, TOP_K * tc), lambda i: (i, 0, 0), memory_space=pltpu.SMEM),
                  pl.BlockSpec((1, 1, TOP_K * tc), lambda i: (jnp.minimum(i + 1, last), 0, 0), memory_space=pltpu.SMEM),
                  pl.BlockSpec((tc, D), lambda i: (i, 0)),
                  pl.BlockSpec((tc, LANES), lambda i: (i, 0)),
                  pl.BlockSpec(memory_space=pl.ANY),
                  pl.BlockSpec((tc, pd), lambda i: (i, 0))] + [full(a) for a in consts],
        out_specs=pl.BlockSpec((tc, D), lambda i: (i, 0)),
        out_shape=jax.ShapeDtypeStruct((T, D), F32),
        scratch_shapes=[pltpu.VMEM((2, TOP_K, tc * SUBLANES, LANES), F32), pltpu.SemaphoreType.DMA((2,))],
        compiler_params=pltpu.CompilerParams(dimension_semantics=("arbitrary",), vmem_limit_bytes=VMEM_LIMIT),
        name="final",
    )(dest3, dest3, x1, info, y, p2, *consts)


def _pad_heads(a, width):
    lead = a.shape[:-1]
    a = a.reshape(lead + (MLA_HEADS, width))
    a = jnp.pad(a, [(0, 0)] * len(lead) + [(0, 0), (0, LANES - width)])
    return a.reshape(lead + (HP,))


def _layer_weights(w_in, q_norm_g, w_q_up, kv_norm_g, w_kv_up, gm_ln_g, gm_ln_b, gm_w_s, gm_b_s,
                   mla_out_g, gm_out_g, w_o, ln1_g, ln1_b):
    D = w_in.shape[0]
    half = QK_ROPE // 2
    c1, c2, c3 = Q_RANK, Q_RANK + KV_RANK, Q_RANK + KV_RANK + QK_ROPE
    wkr = w_in[:, c2:c3]
    zeros = lambda *s: jnp.zeros(s, F32)
    kra = jnp.concatenate([zeros(D, QK_NOPE), wkr, zeros(D, LANES - QK_NOPE - QK_ROPE)], axis=1)
    krb = jnp.concatenate([zeros(D, QK_NOPE), wkr[:, half:], wkr[:, :half], zeros(D, LANES - QK_NOPE - QK_ROPE)], axis=1)
    win = jnp.concatenate([w_in[:, :c2], kra, krb, w_in[:, c3:]], axis=1).astype(BF16)

    wq3 = w_q_up.reshape(Q_RANK, MLA_HEADS, QK_NOPE + QK_ROPE)
    rope = wq3[..., QK_NOPE:]
    rope_sw = jnp.concatenate([rope[..., half:], rope[..., :half]], axis=-1)
    sw3 = jnp.concatenate([jnp.zeros_like(wq3[..., :QK_NOPE]), rope_sw], axis=-1)
    wq = jnp.concatenate([_pad_heads(w_q_up, QK_NOPE + QK_ROPE),
                          _pad_heads(sw3.reshape(Q_RANK, -1), QK_NOPE + QK_ROPE)], axis=1).astype(BF16)

    wkv3 = w_kv_up.reshape(KV_RANK, MLA_HEADS, QK_NOPE + V_HEAD)
    wk = _pad_heads(wkv3[..., :QK_NOPE].reshape(KV_RANK, -1), QK_NOPE).astype(BF16)
    wv = wkv3[..., QK_NOPE:].reshape(KV_RANK, -1).astype(BF16)

    inv = ROPE_THETA ** (-jnp.arange(0, QK_ROPE, 2, dtype=F32) / QK_ROPE)
    pad_l, pad_r = jnp.zeros((QK_NOPE,), F32), jnp.zeros((LANES - QK_NOPE - QK_ROPE,), F32)
    inv_t = jnp.concatenate([pad_l, inv, inv, pad_r])[None, :]
    sgn_t = jnp.concatenate([pad_l, -jnp.ones((half,), F32), jnp.ones((half,), F32), pad_r])[None, :]

    grp = jnp.arange(GM_OUT) // GM_CH
    gavg = jnp.where(grp[:, None] == grp[None, :], 1.0 / GM_CH, 0.0).astype(BF16)
    bias = jnp.repeat(gm_b_s.T, GM_CH, axis=1)

    woa = w_o[:MLA_OUT].astype(BF16)
    wog = w_o[MLA_OUT:].astype(BF16)
    return dict(win=win, qg=q_norm_g[None, :], wq=wq, kvg=kv_norm_g[None, :], wk=wk, wv=wv, inv=inv_t, sgn=sgn_t,
                lng=gm_ln_g[None, :], lnb=gm_ln_b[None, :], gavg=gavg, ws=gm_w_s, bias=bias, gog=gm_out_g[None, :],
                woa=woa, wog=wog, mog=mla_out_g[:, None], l1g=ln1_g[None, :], l1b=ln1_b[None, :])


def _moe(x1, w_rg, b_rg, w_re, b_re, w_gate, w_up, w_down):
    T, D = x1.shape
    pad = jnp.zeros((D, LANES - N_GROUPS - N_EXPERTS), F32)
    wr = jnp.concatenate([w_rg, w_re, pad], axis=1)
    br = jnp.concatenate([b_rg, b_re, pad[0]])[None, :]
    info, cnt = _route(x1, wr, br)

    bm = EXPERT_ROWS
    n_blocks = (T * TOP_K) // bm + N_EXPERTS
    counts = cnt[0, R_OFF:R_OFF + N_EXPERTS].astype(jnp.int32)
    padded = (counts + bm - 1) // bm * bm
    pad_ends = jnp.cumsum(padded)
    pad_starts = pad_ends - padded
    e_idx = info[:, I_E0:I_E1 + 1].astype(jnp.int32)
    rank = info[:, I_R0:I_R1 + 1].astype(jnp.int32)
    seg_start = jnp.sum(jnp.where(e_idx[..., None] == jnp.arange(N_EXPERTS), pad_starts, 0), axis=-1)
    dest = ((seg_start + rank) * SUBLANES).reshape(T // MOVE_ROWS, 1, TOP_K * MOVE_ROWS)
    block_start = jnp.arange(n_blocks, dtype=jnp.int32) * bm
    block_expert = jnp.minimum(jnp.sum(pad_ends[None, :] <= block_start[:, None], axis=1),
                               N_EXPERTS - 1).astype(jnp.int32)

    blk = jnp.arange(n_blocks)
    later = (blk[None, :] > blk[:, None]) & (block_expert[None, :] != block_expert[:, None])
    next_expert = jnp.min(jnp.where(later, block_expert[None, :], N_EXPERTS), axis=1)
    next_expert = jnp.where(next_expert == N_EXPERTS, -1, next_expert).astype(jnp.int32)
    n_used = (pad_ends[-1:] // bm).astype(jnp.int32)

    seg = jnp.stack([pad_ends, padded, jnp.broadcast_to(n_used, (N_EXPERTS,))]).astype(jnp.int32)
    buf = _dispatch(seg, dest, x1, n_blocks * bm)
    y = _experts(block_expert, next_expert, n_used, buf, w_gate, w_up, w_down)
    return info, dest, y


def kernel(x, p, positions, w_in, q_norm_g, w_q_up, kv_norm_g, w_kv_up, gm_ln_g, gm_ln_b, gm_w_s, gm_b_s, mla_out_g, gm_out_g, w_o, ln1_g, ln1_b, w_rg, b_rg, w_re, b_re, w_gate, w_up, w_down, ln2_g, ln2_b, w_pg, b_pg, w_pp, ln3_g, ln3_b):
    B, S, D = x.shape
    T = B * S
    assert S % ATTN_ROWS == 0 and PREP_ROWS == ATTN_ROWS and PREP_ROWS % CHUNK == 0
    assert T % ROUTE_ROWS == 0 and T % MOVE_ROWS == 0 and (T * TOP_K) % EXPERT_ROWS == 0
    assert D == SUBLANES * LANES and MOVE_ROWS % MOVE_UNROLL == 0
    pos3 = positions.reshape(B, S, 1)
    for i in range(DEPTH):
        w = _layer_weights(w_in[i], q_norm_g[i], w_q_up[i], kv_norm_g[i], w_kv_up[i], gm_ln_g[i], gm_ln_b[i],
                           gm_w_s[i], gm_b_s[i], mla_out_g[i], gm_out_g[i], w_o[i], ln1_g[i], ln1_b[i])
        q, k, vt, g = _prep(x, pos3, w)
        x1 = _attn(q, k, vt, g, x, w).reshape(T, D)
        info, dest, y = _moe(x1, w_rg[i], b_rg[i], w_re[i], b_re[i], w_gate[i], w_up[i], w_down[i])
        wf = dict(wpg=w_pg[i].astype(BF16), bpg=b_pg[i][None, :], wpp=w_pp[i].astype(BF16),
                  l2g=ln2_g[i][None, :], l2b=ln2_b[i][None, :], l3g=ln3_g[i][None, :], l3b=ln3_b[i][None, :])
        x = _final(dest, x1, info, y, p[i].reshape(T, -1), wf).reshape(B, S, D)
    return x
```

```python
import functools

import jax
import jax.numpy as jnp
from jax import lax
from jax.experimental import pallas as pl
from jax.experimental.pallas import tpu as pltpu

F32 = jnp.float32
BF16 = jnp.bfloat16

MLA_HEADS = 8
QK_NOPE = 64
QK_ROPE = 32
V_HEAD = 64
Q_RANK = 256
KV_RANK = 128
ROPE_THETA = 10000.0
MLA_OUT = MLA_HEADS * V_HEAD
GM_GROUPS = 8
GM_CH = 64
GM_OUT = GM_GROUPS * GM_CH
CHUNK = 128
N_GROUPS = 4
EXP_PER_GROUP = 8
N_EXPERTS = N_GROUPS * EXP_PER_GROUP
TOP_K = 2
EPS = 1e-6
DEPTH = 1
ALPHA = (2.0 * DEPTH) ** 0.25
SM_SCALE = (QK_NOPE + QK_ROPE) ** -0.5
LOG2E = 1.4426950408889634

LANES = 128
SUBLANES = 8
VMEM_LIMIT = 56 * 1024 * 1024

PREP_ROWS = 256
ATTN_ROWS = 256
ROUTE_ROWS = 512
MOVE_ROWS = 256
MOVE_UNROLL = 8
EXPERT_ROWS = 256

C_Q = 0
C_KV = C_Q + Q_RANK
C_KR = C_KV + KV_RANK
C_U = C_KR + LANES
C_V = C_U + GM_OUT
C_END = C_V + GM_OUT
HP = MLA_HEADS * LANES

I_E0, I_E1, I_R0, I_R1, I_G0, I_G1 = range(6)
R_OFF = N_GROUPS


def _rms(v, g):
    return v * lax.rsqrt(jnp.mean(v * v, axis=-1, keepdims=True) + EPS) * g


def _ln(v, g, b):
    mu = jnp.mean(v, axis=-1, keepdims=True)
    d = v - mu
    var = jnp.mean(d * d, axis=-1, keepdims=True)
    return d * lax.rsqrt(var + EPS) * g + b


def _dot(a, b):
    return jnp.dot(a, b, preferred_element_type=F32)


def _prep_kernel(x_ref, pos_ref, win_ref, qg_ref, wq_ref, kvg_ref, wk_ref, wv_ref, inv_ref, rope_ref, one_ref,
                 lng_ref, lnb_ref, gavg_ref, ws_ref, bias_ref, gog_ref,
                 q_ref, k_ref, vt_ref, g_ref):
    rows = x_ref.shape[1]
    h = _dot(x_ref[0].astype(BF16), win_ref[...])

    ang = inv_ref[...] * pos_ref[0, 0].astype(F32)
    parts = []
    for t in (jnp.cos(ang), jnp.sin(ang)):
        hi = t.astype(BF16).astype(F32)
        parts += [hi, t - hi]
    tabs = _dot(jnp.concatenate(parts, axis=0).T.astype(BF16), rope_ref[...])
    cos_t = tabs[:, :LANES] + one_ref[...]
    sin_a = tabs[:, LANES:2 * LANES]
    sin_b = tabs[:, 2 * LANES:]
    half = QK_ROPE // 2

    def rotate(v):
        return v * cos_t + pltpu.roll(v, LANES - half, 1) * sin_a + pltpu.roll(v, half, 1) * sin_b

    cq = _rms(h[:, C_Q:C_Q + Q_RANK], qg_ref[...]).astype(BF16)
    q2 = _dot(cq, wq_ref[...])
    for hd in range(MLA_HEADS):
        lo = hd * LANES
        q_ref[0, :, lo:lo + LANES] = (rotate(q2[:, lo:lo + LANES]) * (SM_SCALE * LOG2E)).astype(BF16)

    ckv = _rms(h[:, C_KV:C_KV + KV_RANK], kvg_ref[...]).astype(BF16)
    kp = _dot(ckv, wk_ref[...])
    kr = rotate(h[:, C_KR:C_KR + LANES])
    for hd in range(MLA_HEADS):
        lo = hd * LANES
        k_ref[0, :, lo:lo + LANES] = (kp[:, lo:lo + LANES] + kr).astype(BF16)
    vt_ref[0, 0] = _dot(ckv, wv_ref[...]).T.astype(BF16)

    u = jax.nn.gelu(h[:, C_U:C_U + GM_OUT])
    vv = jax.nn.gelu(h[:, C_V:C_V + GM_OUT])
    mu = _dot(vv.astype(BF16), gavg_ref[...])
    d = vv - mu
    var = _dot((d * d).astype(BF16), gavg_ref[...])
    vn = (d * lax.rsqrt(var + EPS) * lng_ref[...] + lnb_ref[...]).astype(BF16)

    tri = lax.broadcasted_iota(jnp.int32, (CHUNK, CHUNK), 0) >= lax.broadcasted_iota(jnp.int32, (CHUNK, CHUNK), 1)
    wm = [jnp.where(tri, ws_ref[g], 0.0).astype(BF16) for g in range(GM_GROUPS)]
    low_half = lax.broadcasted_iota(jnp.int32, (CHUNK, LANES), 1) < GM_CH
    for c in range(rows // CHUNK):
        r0 = c * CHUNK
        parts = []
        for pr in range(GM_GROUPS // 2):
            tile = vn[r0:r0 + CHUNK, pr * LANES:(pr + 1) * LANES]
            parts.append(jnp.where(low_half, _dot(wm[2 * pr], tile), _dot(wm[2 * pr + 1], tile)))
        sg = jnp.concatenate(parts, axis=1) + bias_ref[...]
        gm = u[r0:r0 + CHUNK] * sg
        g_ref[0, r0:r0 + CHUNK, :] = _rms(gm, gog_ref[...]).astype(BF16)


def _prep(x, pos4, w):
    B, S, D = x.shape
    ts = PREP_ROWS
    full = lambda a: pl.BlockSpec(a.shape, lambda b, i: (0,) * a.ndim)
    consts = [w["win"], w["qg"], w["wq"], w["kvg"], w["wk"], w["wv"], w["inv"], w["rope"], w["one"],
              w["lng"], w["lnb"], w["gavg"], w["ws"], w["bias"], w["gog"]]
    return pl.pallas_call(
        _prep_kernel,
        grid=(B, S // ts),
        in_specs=[pl.BlockSpec((1, ts, D), lambda b, i: (b, i, 0)),
                  pl.BlockSpec((1, 1, 1, ts), lambda b, i: (b, i, 0, 0))] + [full(a) for a in consts],
        out_specs=[pl.BlockSpec((1, ts, HP), lambda b, i: (b, i, 0)),
                   pl.BlockSpec((1, ts, HP), lambda b, i: (b, i, 0)),
                   pl.BlockSpec((1, 1, MLA_OUT, ts), lambda b, i: (b, i, 0, 0)),
                   pl.BlockSpec((1, ts, GM_OUT), lambda b, i: (b, i, 0))],
        out_shape=[jax.ShapeDtypeStruct((B, S, HP), BF16)] * 2
        + [jax.ShapeDtypeStruct((B, S // ts, MLA_OUT, ts), BF16), jax.ShapeDtypeStruct((B, S, GM_OUT), BF16)],
        compiler_params=pltpu.CompilerParams(dimension_semantics=("parallel", "parallel"),
                                             vmem_limit_bytes=VMEM_LIMIT),
        name="prep",
    )(x, pos4, *consts)


def _attn_kernel(q_ref, k_ref, vt_ref, g_ref, x_ref, woa_ref, wog_ref, mog_ref, l1g_ref, l1b_ref,
                 o_ref, m_scr, l_scr, acc_scr, sa_scr, sb_scr):
    i = pl.program_id(1)
    tq = q_ref.shape[1]
    tk = tq
    key = lax.broadcasted_iota(jnp.int32, (tk, tq), 0)
    qry = lax.broadcasted_iota(jnp.int32, (tk, tq), 1)
    diag_mask = key <= qry

    m_scr[...] = jnp.full(m_scr.shape, -1e30, F32)
    l_scr[...] = jnp.zeros(l_scr.shape, F32)
    acc_scr[...] = jnp.zeros(acc_scr.shape, F32)

    def scores(j, s_scr):
        k0 = pl.multiple_of(j * tk, tk)
        for hd in range(MLA_HEADS):
            lo = hd * LANES
            qh = q_ref[0, :, lo:lo + LANES]
            kj = k_ref[0, pl.ds(k0, tk), lo:lo + LANES]
            s_scr[hd] = lax.dot_general(kj, qh, (((1,), (1,)), ((), ())), preferred_element_type=F32)

    def update(j, s_scr, masked):
        for hd in range(MLA_HEADS):
            s = s_scr[hd]
            vt = vt_ref[0, j, hd * V_HEAD:(hd + 1) * V_HEAD, :]
            if masked:
                s = jnp.where(diag_mask, s, -1e30)
            m_prev = m_scr[hd]
            m_new = jnp.maximum(m_prev, jnp.max(s, axis=0, keepdims=True))
            p = jnp.exp2(s - m_new)
            scale = jnp.exp2(m_prev - m_new)
            l_scr[hd] = scale * l_scr[hd] + jnp.sum(p, axis=0, keepdims=True)
            acc_scr[hd] = scale * acc_scr[hd] + _dot(vt, p.astype(BF16))
            m_scr[hd] = m_new

    def pair(jj, c):
        j = 2 * jj
        scores(j + 1, sb_scr)
        update(j, sa_scr, False)
        scores(j + 2, sa_scr)
        update(j + 1, sb_scr, False)
        return c

    scores(0, sa_scr)
    lax.fori_loop(0, lax.shift_right_logical(i, 1), pair, 0)

    @pl.when((i & 1) == 0)
    def _():
        update(i, sa_scr, True)

    @pl.when((i & 1) == 1)
    def _():
        scores(i, sb_scr)
        update(i - 1, sa_scr, False)
        update(i, sb_scr, True)

    at = jnp.concatenate([acc_scr[hd] / l_scr[hd] for hd in range(MLA_HEADS)], axis=0)
    at = at * lax.rsqrt(jnp.mean(at * at, axis=0, keepdims=True) + EPS) * mog_ref[...]
    mix = _dot(at.T.astype(BF16), woa_ref[...]) + _dot(g_ref[0], wog_ref[...])
    o_ref[0] = _ln(ALPHA * x_ref[0] + mix, l1g_ref[...], l1b_ref[...])


def _attn(q, k, vt, g, x, w):
    B, S, D = x.shape
    tq = ATTN_ROWS
    full = lambda a: pl.BlockSpec(a.shape, lambda b, i: (0,) * a.ndim)
    consts = [w["woa"], w["wog"], w["mog"], w["l1g"], w["l1b"]]
    return pl.pallas_call(
        _attn_kernel,
        grid=(B, S // tq),
        in_specs=[pl.BlockSpec((1, tq, HP), lambda b, i: (b, i, 0)),
                  pl.BlockSpec((1, S, HP), lambda b, i: (b, 0, 0)),
                  pl.BlockSpec((1,) + vt.shape[1:], lambda b, i: (b, 0, 0, 0)),
                  pl.BlockSpec((1, tq, GM_OUT), lambda b, i: (b, i, 0)),
                  pl.BlockSpec((1, tq, D), lambda b, i: (b, i, 0))] + [full(a) for a in consts],
        out_specs=pl.BlockSpec((1, tq, D), lambda b, i: (b, i, 0)),
        out_shape=jax.ShapeDtypeStruct((B, S, D), F32),
        scratch_shapes=[pltpu.VMEM((MLA_HEADS, 1, tq), F32), pltpu.VMEM((MLA_HEADS, 1, tq), F32),
                        pltpu.VMEM((MLA_HEADS, V_HEAD, tq), F32),
                        pltpu.VMEM((MLA_HEADS, tq, tq), F32), pltpu.VMEM((MLA_HEADS, tq, tq), F32)],
        compiler_params=pltpu.CompilerParams(dimension_semantics=("parallel", "parallel"),
                                             vmem_limit_bytes=VMEM_LIMIT),
        name="attn",
    )(q, k, vt, g, x, *consts)


def _route_kernel(x_ref, wr_ref, br_ref, info_ref, cnt_ref, carry_scr, tri_scr):
    step = pl.program_id(0)
    tt = x_ref.shape[0]

    @pl.when(step == 0)
    def _():
        carry_scr[...] = jnp.zeros_like(carry_scr)
        r = lax.broadcasted_iota(jnp.int32, (tt, tt), 0)
        c = lax.broadcasted_iota(jnp.int32, (tt, tt), 1)
        tri_scr[...] = jnp.where(c < r, 1.0, 0.0).astype(BF16)

    x = x_ref[...]
    xh = x.astype(BF16)
    xl = (x - xh.astype(F32)).astype(BF16)
    wr = wr_ref[...]
    wh = wr.astype(BF16)
    wl = (wr - wh.astype(F32)).astype(BF16)
    logits = _dot(xh, wh) + _dot(xl, wh) + _dot(xh, wl) + br_ref[...]

    lane = lax.broadcasted_iota(jnp.int32, (tt, LANES), 1)
    neg = jnp.float32(-jnp.inf)

    is_g = lane < N_GROUPS
    lg = jnp.where(is_g, logits, neg)
    gmax = jnp.max(lg, axis=-1, keepdims=True)
    g_idx = jnp.min(jnp.where(lg == gmax, lane, LANES), axis=-1, keepdims=True)
    g_den = jnp.sum(jnp.where(is_g, jnp.exp(lg - gmax), 0.0), axis=-1, keepdims=True)
    g_p = 1.0 / g_den

    in_grp = (lane >= R_OFF) & (lane < R_OFF + N_EXPERTS) & (((lane - R_OFF) >> 3) == g_idx)
    le = jnp.where(in_grp, logits, neg)
    m1 = jnp.max(le, axis=-1, keepdims=True)
    i1 = jnp.min(jnp.where(le == m1, lane, LANES), axis=-1, keepdims=True)
    le2 = jnp.where(lane == i1, neg, le)
    m2 = jnp.max(le2, axis=-1, keepdims=True)
    i2 = jnp.min(jnp.where(le2 == m2, lane, LANES), axis=-1, keepdims=True)
    e2 = jnp.exp(m2 - m1)
    gate0 = g_p / (1.0 + e2)
    gate1 = g_p * e2 / (1.0 + e2)

    hit1 = lane == i1
    hit2 = lane == i2
    onehot = jnp.where(hit1 | hit2, 1.0, 0.0)
    before = _dot(tri_scr[...], onehot.astype(BF16)) + carry_scr[...]
    r0 = jnp.sum(jnp.where(hit1, before, 0.0), axis=-1, keepdims=True)
    r1 = jnp.sum(jnp.where(hit2, before, 0.0), axis=-1, keepdims=True)
    carry_scr[...] = carry_scr[...] + jnp.sum(onehot, axis=0, keepdims=True)
    cnt_ref[...] = carry_scr[...]

    info = jnp.where(lane == I_E0, (i1 - R_OFF).astype(F32), 0.0)
    info = jnp.where(lane == I_E1, (i2 - R_OFF).astype(F32), info)
    info = jnp.where(lane == I_R0, r0, info)
    info = jnp.where(lane == I_R1, r1, info)
    info = jnp.where(lane == I_G0, gate0, info)
    info = jnp.where(lane == I_G1, gate1, info)
    info_ref[...] = info


def _route(x1, wr, br):
    T, D = x1.shape
    tt = ROUTE_ROWS
    return pl.pallas_call(
        _route_kernel,
        grid=(T // tt,),
        in_specs=[pl.BlockSpec((tt, D), lambda i: (i, 0)),
                  pl.BlockSpec(wr.shape, lambda i: (0, 0)),
                  pl.BlockSpec(br.shape, lambda i: (0, 0))],
        out_specs=[pl.BlockSpec((tt, LANES), lambda i: (i, 0)),
                   pl.BlockSpec((1, LANES), lambda i: (0, 0))],
        out_shape=[jax.ShapeDtypeStruct((T, LANES), F32), jax.ShapeDtypeStruct((1, LANES), F32)],
        scratch_shapes=[pltpu.VMEM((1, LANES), F32), pltpu.VMEM((tt, tt), BF16)],
        compiler_params=pltpu.CompilerParams(dimension_semantics=("arbitrary",), vmem_limit_bytes=VMEM_LIMIT),
        name="route",
    )(x1, wr, br)


def _to_token_tiles(dst_ref, val):
    rows = val.shape[0]
    for c in range(SUBLANES):
        dst_ref[pl.ds(c, rows, stride=SUBLANES), :] = val[:, c * LANES:(c + 1) * LANES]


def _from_token_tiles(src_ref, rows):
    return jnp.concatenate([src_ref[pl.ds(c, rows, stride=SUBLANES), :] for c in range(SUBLANES)], axis=1)


def _tile_copy(src_ref, src_row, dst_ref, dst_row, sem):
    return pltpu.make_async_copy(src_ref.at[pl.ds(pl.multiple_of(src_row, SUBLANES), SUBLANES)],
                                 dst_ref.at[pl.ds(pl.multiple_of(dst_row, SUBLANES), SUBLANES)], sem)


def _dispatch_kernel(seg_ref, dest_ref, x_ref, buf_ref, stage_scr, zero_scr, sem, zero_sem, *, n_steps):
    i = pl.program_id(0)
    rows = x_ref.shape[0]
    slot = i % 2

    @pl.when(i == 0)
    def _():
        zero_scr[...] = jnp.zeros(zero_scr.shape, F32)

        block = EXPERT_ROWS * SUBLANES
        n_blocks = buf_ref.shape[0] // block

        def clear_rows(first):
            return pltpu.make_async_copy(zero_scr, buf_ref.at[pl.ds(pl.multiple_of(first, SUBLANES), block)], zero_sem)

        def clear(e):
            return clear_rows((seg_ref[0, e] - EXPERT_ROWS) * SUBLANES)

        def start_tail(b, c):
            clear_rows(b * block).start()
            return c

        def wait_tail(b, c):
            clear_rows(b * block).wait()
            return c

        for e in range(N_EXPERTS):
            pl.when(seg_ref[1, e] > 0)(lambda e=e: clear(e).start())
        lax.fori_loop(seg_ref[2, 0], n_blocks, start_tail, 0)
        for e in range(N_EXPERTS):
            pl.when(seg_ref[1, e] > 0)(lambda e=e: clear(e).wait())
        lax.fori_loop(seg_ref[2, 0], n_blocks, wait_tail, 0)

    def drain(s):
        for _ in range(TOP_K):
            pltpu.make_async_copy(stage_scr.at[s], stage_scr.at[s], sem.at[s]).wait()

    @pl.when(i >= 2)
    def _():
        drain(slot)

    _to_token_tiles(stage_scr.at[slot], x_ref[...])

    def start(c, carry):
        for u in range(MOVE_UNROLL):
            r = c * MOVE_UNROLL + u
            for kk in range(TOP_K):
                _tile_copy(stage_scr.at[slot], r * SUBLANES, buf_ref, dest_ref[0, 0, TOP_K * r + kk],
                           sem.at[slot]).start(priority=kk)
        return carry

    lax.fori_loop(0, rows // MOVE_UNROLL, start, 0)

    @pl.when(i == n_steps - 1)
    def _():
        drain(slot)
        if n_steps >= 2:
            drain(1 - slot)


def _dispatch(seg, dest3, x1, n_rows):
    T, D = x1.shape
    td = MOVE_ROWS
    n_steps = T // td
    grid_spec = pltpu.PrefetchScalarGridSpec(
        num_scalar_prefetch=1,
        grid=(n_steps,),
        in_specs=[pl.BlockSpec((1, 1, TOP_K * td), lambda i, seg: (i, 0, 0), memory_space=pltpu.SMEM),
                  pl.BlockSpec((td, D), lambda i, seg: (i, 0))],
        out_specs=pl.BlockSpec(memory_space=pl.ANY),
        scratch_shapes=[pltpu.VMEM((2, td * SUBLANES, LANES), F32), pltpu.VMEM((EXPERT_ROWS * SUBLANES, LANES), F32),
                        pltpu.SemaphoreType.DMA((2,)), pltpu.SemaphoreType.DMA(())],
    )
    return pl.pallas_call(
        functools.partial(_dispatch_kernel, n_steps=n_steps),
        grid_spec=grid_spec,
        out_shape=jax.ShapeDtypeStruct((n_rows * SUBLANES, LANES), F32),
        compiler_params=pltpu.CompilerParams(dimension_semantics=("arbitrary",), vmem_limit_bytes=VMEM_LIMIT),
        name="dispatch",
    )(seg, dest3, x1)


def _expert_kernel(be_ref, ne_ref, nu_ref, buf_ref, wg_hbm, wu_hbm, wd_hbm, y_ref,
                   sg_scr, su_scr, sd_scr, wg_scr, wu_scr, wd_scr, cur_ref, sem):
    b = pl.program_id(0)
    e = be_ref[b]

    def fetch(expert, s):
        return (pltpu.make_async_copy(wg_hbm.at[expert], sg_scr.at[s], sem.at[s, 0]),
                pltpu.make_async_copy(wu_hbm.at[expert], su_scr.at[s], sem.at[s, 1]),
                pltpu.make_async_copy(wd_hbm.at[expert], sd_scr.at[s], sem.at[s, 2]))

    @pl.when(b == 0)
    def _():
        cur_ref[0] = 0
        for c in fetch(e, 0):
            c.start()

    @pl.when((b == 0) | (be_ref[jnp.maximum(b - 1, 0)] != e))
    def _():
        s = cur_ref[0]
        for c in fetch(e, s):
            c.wait()
        wg_scr[...] = sg_scr[s].astype(BF16)
        wu_scr[...] = su_scr[s].astype(BF16)
        wd_scr[...] = sd_scr[s].astype(BF16)
        nxt = ne_ref[b]

        @pl.when(nxt >= 0)
        def _():
            for c in fetch(nxt, 1 - s):
                c.start()

        cur_ref[0] = 1 - s

    @pl.when(b < nu_ref[0])
    def _():
        xb = _from_token_tiles(buf_ref, EXPERT_ROWS).astype(BF16)
        hidden = jax.nn.silu(_dot(xb, wg_scr[...])) * _dot(xb, wu_scr[...])
        _to_token_tiles(y_ref, _dot(hidden.astype(BF16), wd_scr[...]))

    @pl.when(b >= nu_ref[0])
    def _():
        y_ref[...] = jnp.zeros(y_ref.shape, F32)


def _experts(block_expert, next_expert, n_used, buf, w_gate, w_up, w_down):
    bm = EXPERT_ROWS
    D, ff = w_gate.shape[1:]
    n_rows = buf.shape[0] // SUBLANES
    grid_spec = pltpu.PrefetchScalarGridSpec(
        num_scalar_prefetch=3,
        grid=(n_rows // bm,),
        in_specs=[pl.BlockSpec((bm * SUBLANES, LANES), lambda b, *_: (b, 0)),
                  pl.BlockSpec(memory_space=pl.ANY),
                  pl.BlockSpec(memory_space=pl.ANY),
                  pl.BlockSpec(memory_space=pl.ANY)],
        out_specs=pl.BlockSpec((bm * SUBLANES, LANES), lambda b, *_: (b, 0)),
        scratch_shapes=[pltpu.VMEM((2, D, ff), F32), pltpu.VMEM((2, D, ff), F32), pltpu.VMEM((2, ff, D), F32),
                        pltpu.VMEM((D, ff), BF16), pltpu.VMEM((D, ff), BF16), pltpu.VMEM((ff, D), BF16),
                        pltpu.SMEM((1,), jnp.int32), pltpu.SemaphoreType.DMA((2, 3))],
    )
    return pl.pallas_call(
        _expert_kernel,
        grid_spec=grid_spec,
        out_shape=jax.ShapeDtypeStruct(buf.shape, F32),
        compiler_params=pltpu.CompilerParams(dimension_semantics=("arbitrary",), vmem_limit_bytes=VMEM_LIMIT),
        name="experts",
    )(block_expert, next_expert, n_used, buf, w_gate, w_up, w_down)


def _final_kernel(dcur_ref, dnxt_ref, x_ref, info_ref, y_ref, p_ref, wpg_ref, bpg_ref, wpp_ref,
                  l2g_ref, l2b_ref, l3g_ref, l3b_ref, o_ref, rows_scr, sem):
    i = pl.program_id(0)
    last = pl.num_programs(0) - 1
    rows = x_ref.shape[0]
    slot = i % 2

    def row_copy(dref, s, r, kk):
        return _tile_copy(y_ref, dref[0, 0, TOP_K * r + kk], rows_scr.at[s, kk], r * SUBLANES, sem.at[s])

    def landed(s):
        pltpu.make_async_copy(rows_scr.at[s], rows_scr.at[s], sem.at[s]).wait()

    @pl.when(i == 0)
    def _():
        def start(c, carry):
            for u in range(MOVE_UNROLL):
                for kk in range(TOP_K):
                    row_copy(dcur_ref, 0, c * MOVE_UNROLL + u, kk).start(priority=kk)
            return carry

        lax.fori_loop(0, rows // MOVE_UNROLL, start, 0)

    landed(slot)
    info = info_ref[...]
    gate0 = info[:, I_G0:I_G0 + 1]
    gate1 = info[:, I_G1:I_G1 + 1]
    moe = (_from_token_tiles(rows_scr.at[slot, 0], rows) * gate0
           + _from_token_tiles(rows_scr.at[slot, 1], rows) * gate1)

    for r in range(rows):
        for kk in range(TOP_K):
            row_copy(dnxt_ref, 1 - slot, r, kk).start(priority=kk)

    pp = _dot(p_ref[...].astype(BF16), wpp_ref[...])
    x2 = _ln(ALPHA * x_ref[...] + moe, l2g_ref[...], l2b_ref[...])
    gate = jax.nn.sigmoid(_dot(x2.astype(BF16), wpg_ref[...]) + bpg_ref[...])
    o_ref[...] = _ln(ALPHA * x2 + gate * pp, l3g_ref[...], l3b_ref[...])

    @pl.when(i == last)
    def _():
        landed(1 - slot)


def _final(dest3, x1, info, y, p2, w):
    T, D = x1.shape
    tc = MOVE_ROWS
    pd = p2.shape[1]
    full = lambda a: pl.BlockSpec(a.shape, lambda i: (0,) * a.ndim)
    consts = [w["wpg"], w["bpg"], w["wpp"], w["l2g"], w["l2b"], w["l3g"], w["l3b"]]
    last = T // tc - 1
    return pl.pallas_call(
        _final_kernel,
        grid=(T // tc,),
        in_specs=[pl.BlockSpec((1, 1, TOP_K * tc), lambda i: (i, 0, 0), memory_space=pltpu.SMEM),
                  pl.BlockSpec((1, 1, TOP_K * tc), lambda i: (jnp.minimum(i + 1, last), 0, 0), memory_space=pltpu.SMEM),
                  pl.BlockSpec((tc, D), lambda i: (i, 0)),
                  pl.BlockSpec((tc, LANES), lambda i: (i, 0)),
                  pl.BlockSpec(memory_space=pl.ANY),
                  pl.BlockSpec((tc, pd), lambda i: (i, 0))] + [full(a) for a in consts],
        out_specs=pl.BlockSpec((tc, D), lambda i: (i, 0)),
        out_shape=jax.ShapeDtypeStruct((T, D), F32),
        scratch_shapes=[pltpu.VMEM((2, TOP_K, tc * SUBLANES, LANES), F32), pltpu.SemaphoreType.DMA((2,))],
        compiler_params=pltpu.CompilerParams(dimension_semantics=("arbitrary",), vmem_limit_bytes=VMEM_LIMIT),
        name="final",
    )(dest3, dest3, x1, info, y, p2, *consts)


def _pad_heads(a, width):
    lead = a.shape[:-1]
    a = a.reshape(lead + (MLA_HEADS, width))
    a = jnp.pad(a, [(0, 0)] * len(lead) + [(0, 0), (0, LANES - width)])
    return a.reshape(lead + (HP,))


def _layer_weights(w_in, q_norm_g, w_q_up, kv_norm_g, w_kv_up, gm_ln_g, gm_ln_b, gm_w_s, gm_b_s,
                   mla_out_g, gm_out_g, w_o, ln1_g, ln1_b):
    D = w_in.shape[0]
    half = QK_ROPE // 2
    c1, c2, c3 = Q_RANK, Q_RANK + KV_RANK, Q_RANK + KV_RANK + QK_ROPE
    zeros = lambda *s: jnp.zeros(s, F32)
    kr = jnp.concatenate([zeros(D, QK_NOPE), w_in[:, c2:c3], zeros(D, LANES - QK_NOPE - QK_ROPE)], axis=1)
    win = jnp.concatenate([w_in[:, :c2], kr, w_in[:, c3:]], axis=1).astype(BF16)
    wq = _pad_heads(w_q_up, QK_NOPE + QK_ROPE).astype(BF16)

    wkv3 = w_kv_up.reshape(KV_RANK, MLA_HEADS, QK_NOPE + V_HEAD)
    wk = _pad_heads(wkv3[..., :QK_NOPE].reshape(KV_RANK, -1), QK_NOPE).astype(BF16)
    wv = wkv3[..., QK_NOPE:].reshape(KV_RANK, -1).astype(BF16)

    inv = (ROPE_THETA ** (-jnp.arange(0, QK_ROPE, 2, dtype=F32) / QK_ROPE))[:, None]
    eye = jnp.eye(half, dtype=F32)
    first = jnp.pad(eye, ((0, 0), (QK_NOPE, LANES - QK_NOPE - half)))
    second = jnp.pad(eye, ((0, 0), (QK_NOPE + half, LANES - QK_NOPE - QK_ROPE)))
    zero = jnp.zeros_like(first)
    cos_rows = jnp.concatenate([first + second, zero, zero], axis=1)
    sin_rows = jnp.concatenate([zero, -first, second], axis=1)
    rope = jnp.concatenate([cos_rows, cos_rows, sin_rows, sin_rows], axis=0).astype(BF16)
    lane = jnp.arange(LANES)
    one = jnp.where((lane >= QK_NOPE) & (lane < QK_NOPE + QK_ROPE), 0.0, 1.0)[None, :]

    grp = jnp.arange(GM_OUT) // GM_CH
    gavg = jnp.where(grp[:, None] == grp[None, :], 1.0 / GM_CH, 0.0).astype(BF16)
    bias = jnp.repeat(gm_b_s.T, GM_CH, axis=1)

    woa = w_o[:MLA_OUT].astype(BF16)
    wog = w_o[MLA_OUT:].astype(BF16)
    return dict(win=win, qg=q_norm_g[None, :], wq=wq, kvg=kv_norm_g[None, :], wk=wk, wv=wv, inv=inv, rope=rope, one=one,
                lng=gm_ln_g[None, :], lnb=gm_ln_b[None, :], gavg=gavg, ws=gm_w_s, bias=bias, gog=gm_out_g[None, :],
                woa=woa, wog=wog, mog=mla_out_g[:, None], l1g=ln1_g[None, :], l1b=ln1_b[None, :])


def _moe(x1, w_rg, b_rg, w_re, b_re, w_gate, w_up, w_down):
    T, D = x1.shape
    pad = jnp.zeros((D, LANES - N_GROUPS - N_EXPERTS), F32)
    wr = jnp.concatenate([w_rg, w_re, pad], axis=1)
    br = jnp.concatenate([b_rg, b_re, pad[0]])[None, :]
    info, cnt = _route(x1, wr, br)

    bm = EXPERT_ROWS
    n_blocks = (T * TOP_K) // bm + N_EXPERTS
    counts = cnt[0, R_OFF:R_OFF + N_EXPERTS].astype(jnp.int32)
    padded = (counts + bm - 1) // bm * bm
    pad_ends = jnp.cumsum(padded)
    pad_starts = pad_ends - padded
    e_idx = info[:, I_E0:I_E1 + 1].astype(jnp.int32)
    rank = info[:, I_R0:I_R1 + 1].astype(jnp.int32)
    seg_start = jnp.sum(jnp.where(e_idx[..., None] == jnp.arange(N_EXPERTS), pad_starts, 0), axis=-1)
    dest = ((seg_start + rank) * SUBLANES).reshape(T // MOVE_ROWS, 1, TOP_K * MOVE_ROWS)
    block_start = jnp.arange(n_blocks, dtype=jnp.int32) * bm
    block_expert = jnp.minimum(jnp.sum(pad_ends[None, :] <= block_start[:, None], axis=1),
                               N_EXPERTS - 1).astype(jnp.int32)

    blk = jnp.arange(n_blocks)
    later = (blk[None, :] > blk[:, None]) & (block_expert[None, :] != block_expert[:, None])
    next_expert = jnp.min(jnp.where(later, block_expert[None, :], N_EXPERTS), axis=1)
    next_expert = jnp.where(next_expert == N_EXPERTS, -1, next_expert).astype(jnp.int32)
    n_used = (pad_ends[-1:] // bm).astype(jnp.int32)

    seg = jnp.stack([pad_ends, padded, jnp.broadcast_to(n_used, (N_EXPERTS,))]).astype(jnp.int32)
    buf = _dispatch(seg, dest, x1, n_blocks * bm)
    y = _experts(block_expert, next_expert, n_used, buf, w_gate, w_up, w_down)
    return info, dest, y


def kernel(x, p, positions, w_in, q_norm_g, w_q_up, kv_norm_g, w_kv_up, gm_ln_g, gm_ln_b, gm_w_s, gm_b_s, mla_out_g, gm_out_g, w_o, ln1_g, ln1_b, w_rg, b_rg, w_re, b_re, w_gate, w_up, w_down, ln2_g, ln2_b, w_pg, b_pg, w_pp, ln3_g, ln3_b):
    B, S, D = x.shape
    T = B * S
    assert S % ATTN_ROWS == 0 and PREP_ROWS == ATTN_ROWS and PREP_ROWS % CHUNK == 0
    assert T % ROUTE_ROWS == 0 and T % MOVE_ROWS == 0 and (T * TOP_K) % EXPERT_ROWS == 0
    assert D == SUBLANES * LANES and MOVE_ROWS % MOVE_UNROLL == 0
    pos4 = positions.reshape(B, S // PREP_ROWS, 1, PREP_ROWS)
    for i in range(DEPTH):
        w = _layer_weights(w_in[i], q_norm_g[i], w_q_up[i], kv_norm_g[i], w_kv_up[i], gm_ln_g[i], gm_ln_b[i],
                           gm_w_s[i], gm_b_s[i], mla_out_g[i], gm_out_g[i], w_o[i], ln1_g[i], ln1_b[i])
        q, k, vt, g = _prep(x, pos4, w)
        x1 = _attn(q, k, vt, g, x, w).reshape(T, D)
        info, dest, y = _moe(x1, w_rg[i], b_rg[i], w_re[i], b_re[i], w_gate[i], w_up[i], w_down[i])
        wf = dict(wpg=w_pg[i].astype(BF16), bpg=b_pg[i][None, :], wpp=w_pp[i].astype(BF16),
                  l2g=ln2_g[i][None, :], l2b=ln2_b[i][None, :], l3g=ln3_g[i][None, :], l3b=ln3_b[i][None, :])
        x = _final(dest, x1, info, y, p[i].reshape(T, -1), wf).reshape(B, S, D)
    return x
```

```python
import functools

import jax
import jax.numpy as jnp
from jax import lax
from jax.experimental import pallas as pl
from jax.experimental.pallas import tpu as pltpu

F32 = jnp.float32
BF16 = jnp.bfloat16

MLA_HEADS = 8
QK_NOPE = 64
QK_ROPE = 32
V_HEAD = 64
Q_RANK = 256
KV_RANK = 128
ROPE_THETA = 10000.0
MLA_OUT = MLA_HEADS * V_HEAD
GM_GROUPS = 8
GM_CH = 64
GM_OUT = GM_GROUPS * GM_CH
CHUNK = 128
N_GROUPS = 4
EXP_PER_GROUP = 8
N_EXPERTS = N_GROUPS * EXP_PER_GROUP
TOP_K = 2
EPS = 1e-6
DEPTH = 1
ALPHA = (2.0 * DEPTH) ** 0.25
SM_SCALE = (QK_NOPE + QK_ROPE) ** -0.5
LOG2E = 1.4426950408889634

LANES = 128
SUBLANES = 8
TOKEN_ROWS = 8
VMEM_LIMIT = 56 * 1024 * 1024

PREP_ROWS = 256
ATTN_ROWS = 256
ROUTE_ROWS = 512
MOVE_ROWS = 256
MOVE_UNROLL = 8
EXPERT_ROWS = 256

C_Q = 0
C_KV = C_Q + Q_RANK
C_KR = C_KV + KV_RANK
C_U = C_KR + LANES
C_V = C_U + GM_OUT
C_END = C_V + GM_OUT
HP = MLA_HEADS * LANES

I_E0, I_E1, I_R0, I_R1, I_G0, I_G1 = range(6)
R_OFF = N_GROUPS


def _rms(v, g):
    return v * lax.rsqrt(jnp.mean(v * v, axis=-1, keepdims=True) + EPS) * g


def _ln(v, g, b):
    mu = jnp.mean(v, axis=-1, keepdims=True)
    d = v - mu
    var = jnp.mean(d * d, axis=-1, keepdims=True)
    return d * lax.rsqrt(var + EPS) * g + b


def _dot(a, b):
    return jnp.dot(a, b, preferred_element_type=F32)


def _prep_kernel(x_ref, pos_ref, win_ref, qg_ref, wq_ref, kvg_ref, wk_ref, wv_ref, inv_ref, rope_ref, one_ref,
                 lng_ref, lnb_ref, gavg_ref, ws_ref, bias_ref, gog_ref,
                 q_ref, k_ref, vt_ref, g_ref):
    rows = x_ref.shape[1]
    h = _dot(x_ref[0].astype(BF16), win_ref[...])

    ang = inv_ref[...] * pos_ref[0, 0].astype(F32)
    parts = []
    for t in (jnp.cos(ang), jnp.sin(ang)):
        hi = t.astype(BF16).astype(F32)
        parts += [hi, t - hi]
    tabs = _dot(jnp.concatenate(parts, axis=0).T.astype(BF16), rope_ref[...])
    cos_t = tabs[:, :LANES] + one_ref[...]
    sin_a = tabs[:, LANES:2 * LANES]
    sin_b = tabs[:, 2 * LANES:]
    half = QK_ROPE // 2

    def rotate(v):
        return v * cos_t + pltpu.roll(v, LANES - half, 1) * sin_a + pltpu.roll(v, half, 1) * sin_b

    cq = _rms(h[:, C_Q:C_Q + Q_RANK], qg_ref[...]).astype(BF16)
    q2 = _dot(cq, wq_ref[...])
    for hd in range(MLA_HEADS):
        lo = hd * LANES
        q_ref[0, :, lo:lo + LANES] = (rotate(q2[:, lo:lo + LANES]) * (SM_SCALE * LOG2E)).astype(BF16)

    ckv = _rms(h[:, C_KV:C_KV + KV_RANK], kvg_ref[...]).astype(BF16)
    kp = _dot(ckv, wk_ref[...])
    kr = rotate(h[:, C_KR:C_KR + LANES])
    for hd in range(MLA_HEADS):
        lo = hd * LANES
        k_ref[0, :, lo:lo + LANES] = (kp[:, lo:lo + LANES] + kr).astype(BF16)
    vt_ref[0, 0] = _dot(ckv, wv_ref[...]).T.astype(BF16)

    u = jax.nn.gelu(h[:, C_U:C_U + GM_OUT])
    vv = jax.nn.gelu(h[:, C_V:C_V + GM_OUT])
    mu = _dot(vv.astype(BF16), gavg_ref[...])
    d = vv - mu
    var = _dot((d * d).astype(BF16), gavg_ref[...])
    vn = (d * lax.rsqrt(var + EPS) * lng_ref[...] + lnb_ref[...]).astype(BF16)

    tri = lax.broadcasted_iota(jnp.int32, (CHUNK, CHUNK), 0) >= lax.broadcasted_iota(jnp.int32, (CHUNK, CHUNK), 1)
    wm = [jnp.where(tri, ws_ref[g], 0.0).astype(BF16) for g in range(GM_GROUPS)]
    low_half = lax.broadcasted_iota(jnp.int32, (CHUNK, LANES), 1) < GM_CH
    for c in range(rows // CHUNK):
        r0 = c * CHUNK
        parts = []
        for pr in range(GM_GROUPS // 2):
            tile = vn[r0:r0 + CHUNK, pr * LANES:(pr + 1) * LANES]
            parts.append(jnp.where(low_half, _dot(wm[2 * pr], tile), _dot(wm[2 * pr + 1], tile)))
        sg = jnp.concatenate(parts, axis=1) + bias_ref[...]
        gm = u[r0:r0 + CHUNK] * sg
        g_ref[0, r0:r0 + CHUNK, :] = _rms(gm, gog_ref[...]).astype(BF16)


def _prep(x, pos4, w):
    B, S, D = x.shape
    ts = PREP_ROWS
    full = lambda a: pl.BlockSpec(a.shape, lambda b, i: (0,) * a.ndim)
    consts = [w["win"], w["qg"], w["wq"], w["kvg"], w["wk"], w["wv"], w["inv"], w["rope"], w["one"],
              w["lng"], w["lnb"], w["gavg"], w["ws"], w["bias"], w["gog"]]
    return pl.pallas_call(
        _prep_kernel,
        grid=(B, S // ts),
        in_specs=[pl.BlockSpec((1, ts, D), lambda b, i: (b, i, 0)),
                  pl.BlockSpec((1, 1, 1, ts), lambda b, i: (b, i, 0, 0))] + [full(a) for a in consts],
        out_specs=[pl.BlockSpec((1, ts, HP), lambda b, i: (b, i, 0)),
                   pl.BlockSpec((1, ts, HP), lambda b, i: (b, i, 0)),
                   pl.BlockSpec((1, 1, MLA_OUT, ts), lambda b, i: (b, i, 0, 0)),
                   pl.BlockSpec((1, ts, GM_OUT), lambda b, i: (b, i, 0))],
        out_shape=[jax.ShapeDtypeStruct((B, S, HP), BF16)] * 2
        + [jax.ShapeDtypeStruct((B, S // ts, MLA_OUT, ts), BF16), jax.ShapeDtypeStruct((B, S, GM_OUT), BF16)],
        compiler_params=pltpu.CompilerParams(dimension_semantics=("parallel", "parallel"),
                                             vmem_limit_bytes=VMEM_LIMIT),
        name="prep",
    )(x, pos4, *consts)


def _attn_kernel(q_ref, k_ref, vt_ref, g_ref, x_ref, woa_ref, wog_ref, mog_ref, l1g_ref, l1b_ref,
                 o_ref, m_scr, l_scr, acc_scr, sa_scr, sb_scr):
    i = pl.program_id(1)
    tq = q_ref.shape[1]
    tk = tq
    key = lax.broadcasted_iota(jnp.int32, (tk, tq), 0)
    qry = lax.broadcasted_iota(jnp.int32, (tk, tq), 1)
    diag_mask = key <= qry

    m_scr[...] = jnp.full(m_scr.shape, -1e30, F32)
    l_scr[...] = jnp.zeros(l_scr.shape, F32)
    acc_scr[...] = jnp.zeros(acc_scr.shape, F32)

    def scores(j, s_scr):
        k0 = pl.multiple_of(j * tk, tk)
        for hd in range(MLA_HEADS):
            lo = hd * LANES
            qh = q_ref[0, :, lo:lo + LANES]
            kj = k_ref[0, pl.ds(k0, tk), lo:lo + LANES]
            s_scr[hd] = lax.dot_general(kj, qh, (((1,), (1,)), ((), ())), preferred_element_type=F32)

    def update(j, s_scr, masked):
        for hd in range(MLA_HEADS):
            s = s_scr[hd]
            vt = vt_ref[0, j, hd * V_HEAD:(hd + 1) * V_HEAD, :]
            if masked:
                s = jnp.where(diag_mask, s, -1e30)
            m_prev = m_scr[hd]
            m_new = jnp.maximum(m_prev, jnp.max(s, axis=0, keepdims=True))
            p = jnp.exp2(s - m_new)
            scale = jnp.exp2(m_prev - m_new)
            l_scr[hd] = scale * l_scr[hd] + jnp.sum(p, axis=0, keepdims=True)
            acc_scr[hd] = scale * acc_scr[hd] + _dot(vt, p.astype(BF16))
            m_scr[hd] = m_new

    def pair(jj, c):
        j = 2 * jj
        scores(j + 1, sb_scr)
        update(j, sa_scr, False)
        scores(j + 2, sa_scr)
        update(j + 1, sb_scr, False)
        return c

    scores(0, sa_scr)
    lax.fori_loop(0, lax.shift_right_logical(i, 1), pair, 0)

    @pl.when((i & 1) == 0)
    def _():
        update(i, sa_scr, True)

    @pl.when((i & 1) == 1)
    def _():
        scores(i, sb_scr)
        update(i - 1, sa_scr, False)
        update(i, sb_scr, True)

    at = jnp.concatenate([acc_scr[hd] / l_scr[hd] for hd in range(MLA_HEADS)], axis=0)
    at = at * lax.rsqrt(jnp.mean(at * at, axis=0, keepdims=True) + EPS) * mog_ref[...]
    mix = _dot(at.T.astype(BF16), woa_ref[...]) + _dot(g_ref[0], wog_ref[...])
    o_ref[0] = _ln(ALPHA * x_ref[0] + mix, l1g_ref[...], l1b_ref[...])


def _attn(q, k, vt, g, x, w):
    B, S, D = x.shape
    tq = ATTN_ROWS
    full = lambda a: pl.BlockSpec(a.shape, lambda b, i: (0,) * a.ndim)
    consts = [w["woa"], w["wog"], w["mog"], w["l1g"], w["l1b"]]
    return pl.pallas_call(
        _attn_kernel,
        grid=(B, S // tq),
        in_specs=[pl.BlockSpec((1, tq, HP), lambda b, i: (b, i, 0)),
                  pl.BlockSpec((1, S, HP), lambda b, i: (b, 0, 0)),
                  pl.BlockSpec((1,) + vt.shape[1:], lambda b, i: (b, 0, 0, 0)),
                  pl.BlockSpec((1, tq, GM_OUT), lambda b, i: (b, i, 0)),
                  pl.BlockSpec((1, tq, D), lambda b, i: (b, i, 0))] + [full(a) for a in consts],
        out_specs=pl.BlockSpec((1, tq, D), lambda b, i: (b, i, 0)),
        out_shape=jax.ShapeDtypeStruct((B, S, D), F32),
        scratch_shapes=[pltpu.VMEM((MLA_HEADS, 1, tq), F32), pltpu.VMEM((MLA_HEADS, 1, tq), F32),
                        pltpu.VMEM((MLA_HEADS, V_HEAD, tq), F32),
                        pltpu.VMEM((MLA_HEADS, tq, tq), F32), pltpu.VMEM((MLA_HEADS, tq, tq), F32)],
        compiler_params=pltpu.CompilerParams(dimension_semantics=("parallel", "parallel"),
                                             vmem_limit_bytes=VMEM_LIMIT),
        name="attn",
    )(q, k, vt, g, x, *consts)


def _route_kernel(x_ref, wr_ref, br_ref, info_ref, cnt_ref, carry_scr, tri_scr):
    step = pl.program_id(0)
    tt = x_ref.shape[0]

    @pl.when(step == 0)
    def _():
        carry_scr[...] = jnp.zeros_like(carry_scr)
        r = lax.broadcasted_iota(jnp.int32, (tt, tt), 0)
        c = lax.broadcasted_iota(jnp.int32, (tt, tt), 1)
        tri_scr[...] = jnp.where(c < r, 1.0, 0.0).astype(BF16)

    x = x_ref[...]
    xh = x.astype(BF16)
    xl = (x - xh.astype(F32)).astype(BF16)
    wr = wr_ref[...]
    wh = wr.astype(BF16)
    wl = (wr - wh.astype(F32)).astype(BF16)
    logits = _dot(xh, wh) + _dot(xl, wh) + _dot(xh, wl) + br_ref[...]

    lane = lax.broadcasted_iota(jnp.int32, (tt, LANES), 1)
    neg = jnp.float32(-jnp.inf)

    is_g = lane < N_GROUPS
    lg = jnp.where(is_g, logits, neg)
    gmax = jnp.max(lg, axis=-1, keepdims=True)
    g_idx = jnp.min(jnp.where(lg == gmax, lane, LANES), axis=-1, keepdims=True)
    g_den = jnp.sum(jnp.where(is_g, jnp.exp(lg - gmax), 0.0), axis=-1, keepdims=True)
    g_p = 1.0 / g_den

    in_grp = (lane >= R_OFF) & (lane < R_OFF + N_EXPERTS) & (((lane - R_OFF) >> 3) == g_idx)
    le = jnp.where(in_grp, logits, neg)
    m1 = jnp.max(le, axis=-1, keepdims=True)
    i1 = jnp.min(jnp.where(le == m1, lane, LANES), axis=-1, keepdims=True)
    le2 = jnp.where(lane == i1, neg, le)
    m2 = jnp.max(le2, axis=-1, keepdims=True)
    i2 = jnp.min(jnp.where(le2 == m2, lane, LANES), axis=-1, keepdims=True)
    e2 = jnp.exp(m2 - m1)
    gate0 = g_p / (1.0 + e2)
    gate1 = g_p * e2 / (1.0 + e2)

    hit1 = lane == i1
    hit2 = lane == i2
    onehot = jnp.where(hit1 | hit2, 1.0, 0.0)
    before = _dot(tri_scr[...], onehot.astype(BF16)) + carry_scr[...]
    r0 = jnp.sum(jnp.where(hit1, before, 0.0), axis=-1, keepdims=True)
    r1 = jnp.sum(jnp.where(hit2, before, 0.0), axis=-1, keepdims=True)
    carry_scr[...] = carry_scr[...] + jnp.sum(onehot, axis=0, keepdims=True)
    cnt_ref[...] = carry_scr[...]

    info = jnp.where(lane == I_E0, (i1 - R_OFF).astype(F32), 0.0)
    info = jnp.where(lane == I_E1, (i2 - R_OFF).astype(F32), info)
    info = jnp.where(lane == I_R0, r0, info)
    info = jnp.where(lane == I_R1, r1, info)
    info = jnp.where(lane == I_G0, gate0, info)
    info = jnp.where(lane == I_G1, gate1, info)
    info_ref[...] = info


def _route(x1, wr, br):
    T, D = x1.shape
    tt = ROUTE_ROWS
    return pl.pallas_call(
        _route_kernel,
        grid=(T // tt,),
        in_specs=[pl.BlockSpec((tt, D), lambda i: (i, 0)),
                  pl.BlockSpec(wr.shape, lambda i: (0, 0)),
                  pl.BlockSpec(br.shape, lambda i: (0, 0))],
        out_specs=[pl.BlockSpec((tt, LANES), lambda i: (i, 0)),
                   pl.BlockSpec((1, LANES), lambda i: (0, 0))],
        out_shape=[jax.ShapeDtypeStruct((T, LANES), F32), jax.ShapeDtypeStruct((1, LANES), F32)],
        scratch_shapes=[pltpu.VMEM((1, LANES), F32), pltpu.VMEM((tt, tt), BF16)],
        compiler_params=pltpu.CompilerParams(dimension_semantics=("arbitrary",), vmem_limit_bytes=VMEM_LIMIT),
        name="route",
    )(x1, wr, br)


def _to_token_tiles(dst_ref, val):
    dst_ref[...] = val.astype(BF16).reshape(dst_ref.shape)


def _from_token_tiles(src_ref, rows):
    return src_ref[...].reshape(rows, TOKEN_ROWS * LANES)


def _tile_copy(src_ref, src_row, dst_ref, dst_row, sem):
    return pltpu.make_async_copy(src_ref.at[pl.ds(pl.multiple_of(src_row, TOKEN_ROWS), TOKEN_ROWS)],
                                 dst_ref.at[pl.ds(pl.multiple_of(dst_row, TOKEN_ROWS), TOKEN_ROWS)], sem)


def _dispatch_kernel(seg_ref, dest_ref, x_ref, buf_ref, stage_scr, zero_scr, sem, zero_sem, *, n_steps):
    i = pl.program_id(0)
    rows = x_ref.shape[0]
    slot = i % 2

    @pl.when(i == 0)
    def _():
        zero_scr[...] = jnp.zeros(zero_scr.shape, BF16)

        block = EXPERT_ROWS * TOKEN_ROWS
        n_blocks = buf_ref.shape[0] // block

        def clear_rows(first):
            return pltpu.make_async_copy(zero_scr, buf_ref.at[pl.ds(pl.multiple_of(first, SUBLANES), block)], zero_sem)

        def clear(e):
            return clear_rows((seg_ref[0, e] - EXPERT_ROWS) * TOKEN_ROWS)

        def start_tail(b, c):
            clear_rows(b * block).start()
            return c

        def wait_tail(b, c):
            clear_rows(b * block).wait()
            return c

        for e in range(N_EXPERTS):
            pl.when(seg_ref[1, e] > 0)(lambda e=e: clear(e).start())
        lax.fori_loop(seg_ref[2, 0], n_blocks, start_tail, 0)
        for e in range(N_EXPERTS):
            pl.when(seg_ref[1, e] > 0)(lambda e=e: clear(e).wait())
        lax.fori_loop(seg_ref[2, 0], n_blocks, wait_tail, 0)

    def drain(s):
        for _ in range(TOP_K):
            pltpu.make_async_copy(stage_scr.at[s], stage_scr.at[s], sem.at[s]).wait()

    @pl.when(i >= 2)
    def _():
        drain(slot)

    _to_token_tiles(stage_scr.at[slot], x_ref[...])

    def start(c, carry):
        for u in range(MOVE_UNROLL):
            r = c * MOVE_UNROLL + u
            for kk in range(TOP_K):
                _tile_copy(stage_scr.at[slot], r * TOKEN_ROWS, buf_ref, dest_ref[0, 0, TOP_K * r + kk],
                           sem.at[slot]).start(priority=kk)
        return carry

    lax.fori_loop(0, rows // MOVE_UNROLL, start, 0)

    @pl.when(i == n_steps - 1)
    def _():
        drain(slot)
        if n_steps >= 2:
            drain(1 - slot)


def _dispatch(seg, dest3, x1, n_rows):
    T, D = x1.shape
    td = MOVE_ROWS
    n_steps = T // td
    grid_spec = pltpu.PrefetchScalarGridSpec(
        num_scalar_prefetch=1,
        grid=(n_steps,),
        in_specs=[pl.BlockSpec((1, 1, TOP_K * td), lambda i, seg: (i, 0, 0), memory_space=pltpu.SMEM),
                  pl.BlockSpec((td, D), lambda i, seg: (i, 0))],
        out_specs=pl.BlockSpec(memory_space=pl.ANY),
        scratch_shapes=[pltpu.VMEM((2, td * TOKEN_ROWS, LANES), BF16),
                        pltpu.VMEM((EXPERT_ROWS * TOKEN_ROWS, LANES), BF16),
                        pltpu.SemaphoreType.DMA((2,)), pltpu.SemaphoreType.DMA(())],
    )
    return pl.pallas_call(
        functools.partial(_dispatch_kernel, n_steps=n_steps),
        grid_spec=grid_spec,
        out_shape=jax.ShapeDtypeStruct((n_rows * TOKEN_ROWS, LANES), BF16),
        compiler_params=pltpu.CompilerParams(dimension_semantics=("arbitrary",), vmem_limit_bytes=VMEM_LIMIT),
        name="dispatch",
    )(seg, dest3, x1)


def _expert_kernel(be_ref, ne_ref, nu_ref, buf_ref, wg_hbm, wu_hbm, wd_hbm, y_ref,
                   sg_scr, su_scr, sd_scr, wg_scr, wu_scr, wd_scr, cur_ref, sem):
    b = pl.program_id(0)
    e = be_ref[b]

    def fetch(expert, s):
        return (pltpu.make_async_copy(wg_hbm.at[expert], sg_scr.at[s], sem.at[s, 0]),
                pltpu.make_async_copy(wu_hbm.at[expert], su_scr.at[s], sem.at[s, 1]),
                pltpu.make_async_copy(wd_hbm.at[expert], sd_scr.at[s], sem.at[s, 2]))

    @pl.when(b == 0)
    def _():
        cur_ref[0] = 0
        for c in fetch(e, 0):
            c.start()

    @pl.when((b == 0) | (be_ref[jnp.maximum(b - 1, 0)] != e))
    def _():
        s = cur_ref[0]
        for c in fetch(e, s):
            c.wait()
        wg_scr[...] = sg_scr[s].astype(BF16)
        wu_scr[...] = su_scr[s].astype(BF16)
        wd_scr[...] = sd_scr[s].astype(BF16)
        nxt = ne_ref[b]

        @pl.when(nxt >= 0)
        def _():
            for c in fetch(nxt, 1 - s):
                c.start()

        cur_ref[0] = 1 - s

    @pl.when(b < nu_ref[0])
    def _():
        xb = _from_token_tiles(buf_ref, EXPERT_ROWS)
        hidden = jax.nn.silu(_dot(xb, wg_scr[...])) * _dot(xb, wu_scr[...])
        _to_token_tiles(y_ref, _dot(hidden.astype(BF16), wd_scr[...]))

    @pl.when(b >= nu_ref[0])
    def _():
        y_ref[...] = jnp.zeros(y_ref.shape, BF16)


def _experts(block_expert, next_expert, n_used, buf, w_gate, w_up, w_down):
    bm = EXPERT_ROWS
    D, ff = w_gate.shape[1:]
    n_rows = buf.shape[0] // TOKEN_ROWS
    grid_spec = pltpu.PrefetchScalarGridSpec(
        num_scalar_prefetch=3,
        grid=(n_rows // bm,),
        in_specs=[pl.BlockSpec((bm * TOKEN_ROWS, LANES), lambda b, *_: (b, 0)),
                  pl.BlockSpec(memory_space=pl.ANY),
                  pl.BlockSpec(memory_space=pl.ANY),
                  pl.BlockSpec(memory_space=pl.ANY)],
        out_specs=pl.BlockSpec((bm * TOKEN_ROWS, LANES), lambda b, *_: (b, 0)),
        scratch_shapes=[pltpu.VMEM((2, D, ff), F32), pltpu.VMEM((2, D, ff), F32), pltpu.VMEM((2, ff, D), F32),
                        pltpu.VMEM((D, ff), BF16), pltpu.VMEM((D, ff), BF16), pltpu.VMEM((ff, D), BF16),
                        pltpu.SMEM((1,), jnp.int32), pltpu.SemaphoreType.DMA((2, 3))],
    )
    return pl.pallas_call(
        _expert_kernel,
        grid_spec=grid_spec,
        out_shape=jax.ShapeDtypeStruct(buf.shape, BF16),
        compiler_params=pltpu.CompilerParams(dimension_semantics=("arbitrary",), vmem_limit_bytes=VMEM_LIMIT),
        name="experts",
    )(block_expert, next_expert, n_used, buf, w_gate, w_up, w_down)


def _final_kernel(dcur_ref, dnxt_ref, x_ref, info_ref, y_ref, p_ref, wpg_ref, bpg_ref, wpp_ref,
                  l2g_ref, l2b_ref, l3g_ref, l3b_ref, o_ref, rows_scr, sem):
    i = pl.program_id(0)
    last = pl.num_programs(0) - 1
    rows = x_ref.shape[0]
    slot = i % 2

    def row_copy(dref, s, r, kk):
        return _tile_copy(y_ref, dref[0, 0, TOP_K * r + kk], rows_scr.at[s, kk], r * TOKEN_ROWS, sem.at[s])

    def landed(s):
        pltpu.make_async_copy(rows_scr.at[s], rows_scr.at[s], sem.at[s]).wait()

    @pl.when(i == 0)
    def _():
        def start(c, carry):
            for u in range(MOVE_UNROLL):
                for kk in range(TOP_K):
                    row_copy(dcur_ref, 0, c * MOVE_UNROLL + u, kk).start(priority=kk)
            return carry

        lax.fori_loop(0, rows // MOVE_UNROLL, start, 0)

    landed(slot)
    info = info_ref[...]
    gate0 = info[:, I_G0:I_G0 + 1]
    gate1 = info[:, I_G1:I_G1 + 1]
    moe = (_from_token_tiles(rows_scr.at[slot, 0], rows).astype(F32) * gate0
           + _from_token_tiles(rows_scr.at[slot, 1], rows).astype(F32) * gate1)

    for r in range(rows):
        for kk in range(TOP_K):
            row_copy(dnxt_ref, 1 - slot, r, kk).start(priority=kk)

    pp = _dot(p_ref[...].astype(BF16), wpp_ref[...])
    x2 = _ln(ALPHA * x_ref[...] + moe, l2g_ref[...], l2b_ref[...])
    gate = jax.nn.sigmoid(_dot(x2.astype(BF16), wpg_ref[...]) + bpg_ref[...])
    o_ref[...] = _ln(ALPHA * x2 + gate * pp, l3g_ref[...], l3b_ref[...])

    @pl.when(i == last)
    def _():
        landed(1 - slot)


def _final(dest3, x1, info, y, p2, w):
    T, D = x1.shape
    tc = MOVE_ROWS
    pd = p2.shape[1]
    full = lambda a: pl.BlockSpec(a.shape, lambda i: (0,) * a.ndim)
    consts = [w["wpg"], w["bpg"], w["wpp"], w["l2g"], w["l2b"], w["l3g"], w["l3b"]]
    last = T // tc - 1
    return pl.pallas_call(
        _final_kernel,
        grid=(T // tc,),
        in_specs=[pl.BlockSpec((1, 1, TOP_K * tc), lambda i: (i, 0, 0), memory_space=pltpu.SMEM),
                  pl.BlockSpec((1, 1, TOP_K * tc), lambda i: (jnp.minimum(i + 1, last), 0, 0), memory_space=pltpu.SMEM),
                  pl.BlockSpec((tc, D), lambda i: (i, 0)),
                  pl.BlockSpec((tc, LANES), lambda i: (i, 0)),
                  pl.BlockSpec(memory_space=pl.ANY),
                  pl.BlockSpec((tc, pd), lambda i: (i, 0))] + [full(a) for a in consts],
        out_specs=pl.BlockSpec((tc, D), lambda i: (i, 0)),
        out_shape=jax.ShapeDtypeStruct((T, D), F32),
        scratch_shapes=[pltpu.VMEM((2, TOP_K, tc * TOKEN_ROWS, LANES), BF16), pltpu.SemaphoreType.DMA((2,))],
        compiler_params=pltpu.CompilerParams(dimension_semantics=("arbitrary",), vmem_limit_bytes=VMEM_LIMIT),
        name="final",
    )(dest3, dest3, x1, info, y, p2, *consts)


def _pad_heads(a, width):
    lead = a.shape[:-1]
    a = a.reshape(lead + (MLA_HEADS, width))
    a = jnp.pad(a, [(0, 0)] * len(lead) + [(0, 0), (0, LANES - width)])
    return a.reshape(lead + (HP,))


def _layer_weights(w_in, q_norm_g, w_q_up, kv_norm_g, w_kv_up, gm_ln_g, gm_ln_b, gm_w_s, gm_b_s,
                   mla_out_g, gm_out_g, w_o, ln1_g, ln1_b):
    D = w_in.shape[0]
    half = QK_ROPE // 2
    c1, c2, c3 = Q_RANK, Q_RANK + KV_RANK, Q_RANK + KV_RANK + QK_ROPE
    zeros = lambda *s: jnp.zeros(s, F32)
    kr = jnp.concatenate([zeros(D, QK_NOPE), w_in[:, c2:c3], zeros(D, LANES - QK_NOPE - QK_ROPE)], axis=1)
    win = jnp.concatenate([w_in[:, :c2], kr, w_in[:, c3:]], axis=1).astype(BF16)
    wq = _pad_heads(w_q_up, QK_NOPE + QK_ROPE).astype(BF16)

    wkv3 = w_kv_up.reshape(KV_RANK, MLA_HEADS, QK_NOPE + V_HEAD)
    wk = _pad_heads(wkv3[..., :QK_NOPE].reshape(KV_RANK, -1), QK_NOPE).astype(BF16)
    wv = wkv3[..., QK_NOPE:].reshape(KV_RANK, -1).astype(BF16)

    inv = (ROPE_THETA ** (-jnp.arange(0, QK_ROPE, 2, dtype=F32) / QK_ROPE))[:, None]
    eye = jnp.eye(half, dtype=F32)
    first = jnp.pad(eye, ((0, 0), (QK_NOPE, LANES - QK_NOPE - half)))
    second = jnp.pad(eye, ((0, 0), (QK_NOPE + half, LANES - QK_NOPE - QK_ROPE)))
    zero = jnp.zeros_like(first)
    cos_rows = jnp.concatenate([first + second, zero, zero], axis=1)
    sin_rows = jnp.concatenate([zero, -first, second], axis=1)
    rope = jnp.concatenate([cos_rows, cos_rows, sin_rows, sin_rows], axis=0).astype(BF16)
    lane = jnp.arange(LANES)
    one = jnp.where((lane >= QK_NOPE) & (lane < QK_NOPE + QK_ROPE), 0.0, 1.0)[None, :]

    grp = jnp.arange(GM_OUT) // GM_CH
    gavg = jnp.where(grp[:, None] == grp[None, :], 1.0 / GM_CH, 0.0).astype(BF16)
    bias = jnp.repeat(gm_b_s.T, GM_CH, axis=1)

    woa = w_o[:MLA_OUT].astype(BF16)
    wog = w_o[MLA_OUT:].astype(BF16)
    return dict(win=win, qg=q_norm_g[None, :], wq=wq, kvg=kv_norm_g[None, :], wk=wk, wv=wv, inv=inv, rope=rope, one=one,
                lng=gm_ln_g[None, :], lnb=gm_ln_b[None, :], gavg=gavg, ws=gm_w_s, bias=bias, gog=gm_out_g[None, :],
                woa=woa, wog=wog, mog=mla_out_g[:, None], l1g=ln1_g[None, :], l1b=ln1_b[None, :])


def _moe(x1, w_rg, b_rg, w_re, b_re, w_gate, w_up, w_down):
    T, D = x1.shape
    pad = jnp.zeros((D, LANES - N_GROUPS - N_EXPERTS), F32)
    wr = jnp.concatenate([w_rg, w_re, pad], axis=1)
    br = jnp.concatenate([b_rg, b_re, pad[0]])[None, :]
    info, cnt = _route(x1, wr, br)

    bm = EXPERT_ROWS
    n_blocks = (T * TOP_K) // bm + N_EXPERTS
    counts = cnt[0, R_OFF:R_OFF + N_EXPERTS].astype(jnp.int32)
    padded = (counts + bm - 1) // bm * bm
    pad_ends = jnp.cumsum(padded)
    pad_starts = pad_ends - padded
    e_idx = info[:, I_E0:I_E1 + 1].astype(jnp.int32)
    rank = info[:, I_R0:I_R1 + 1].astype(jnp.int32)
    seg_start = jnp.sum(jnp.where(e_idx[..., None] == jnp.arange(N_EXPERTS), pad_starts, 0), axis=-1)
    dest = ((seg_start + rank) * TOKEN_ROWS).reshape(T // MOVE_ROWS, 1, TOP_K * MOVE_ROWS)
    block_start = jnp.arange(n_blocks, dtype=jnp.int32) * bm
    block_expert = jnp.minimum(jnp.sum(pad_ends[None, :] <= block_start[:, None], axis=1),
                               N_EXPERTS - 1).astype(jnp.int32)

    blk = jnp.arange(n_blocks)
    later = (blk[None, :] > blk[:, None]) & (block_expert[None, :] != block_expert[:, None])
    next_expert = jnp.min(jnp.where(later, block_expert[None, :], N_EXPERTS), axis=1)
    next_expert = jnp.where(next_expert == N_EXPERTS, -1, next_expert).astype(jnp.int32)
    n_used = (pad_ends[-1:] // bm).astype(jnp.int32)

    seg = jnp.stack([pad_ends, padded, jnp.broadcast_to(n_used, (N_EXPERTS,))]).astype(jnp.int32)
    buf = _dispatch(seg, dest, x1, n_blocks * bm)
    y = _experts(block_expert, next_expert, n_used, buf, w_gate, w_up, w_down)
    return info, dest, y


def kernel(x, p, positions, w_in, q_norm_g, w_q_up, kv_norm_g, w_kv_up, gm_ln_g, gm_ln_b, gm_w_s, gm_b_s, mla_out_g, gm_out_g, w_o, ln1_g, ln1_b, w_rg, b_rg, w_re, b_re, w_gate, w_up, w_down, ln2_g, ln2_b, w_pg, b_pg, w_pp, ln3_g, ln3_b):
    B, S, D = x.shape
    T = B * S
    assert S % ATTN_ROWS == 0 and PREP_ROWS == ATTN_ROWS and PREP_ROWS % CHUNK == 0
    assert T % ROUTE_ROWS == 0 and T % MOVE_ROWS == 0 and (T * TOP_K) % EXPERT_ROWS == 0
    assert D == TOKEN_ROWS * LANES and MOVE_ROWS % MOVE_UNROLL == 0
    pos4 = positions.reshape(B, S // PREP_ROWS, 1, PREP_ROWS)
    for i in range(DEPTH):
        w = _layer_weights(w_in[i], q_norm_g[i], w_q_up[i], kv_norm_g[i], w_kv_up[i], gm_ln_g[i], gm_ln_b[i],
                           gm_w_s[i], gm_b_s[i], mla_out_g[i], gm_out_g[i], w_o[i], ln1_g[i], ln1_b[i])
        q, k, vt, g = _prep(x, pos4, w)
        x1 = _attn(q, k, vt, g, x, w).reshape(T, D)
        info, dest, y = _moe(x1, w_rg[i], b_rg[i], w_re[i], b_re[i], w_gate[i], w_up[i], w_down[i])
        wf = dict(wpg=w_pg[i].astype(BF16), bpg=b_pg[i][None, :], wpp=w_pp[i].astype(BF16),
                  l2g=ln2_g[i][None, :], l2b=ln2_b[i][None, :], l3g=ln3_g[i][None, :], l3b=ln3_b[i][None, :])
        x = _final(dest, x1, info, y, p[i].reshape(T, -1), wf).reshape(B, S, D)
    return x
```

```python
import functools

import jax
import jax.numpy as jnp
from jax import lax
from jax.experimental import pallas as pl
from jax.experimental.pallas import tpu as pltpu

F32 = jnp.float32
BF16 = jnp.bfloat16

MLA_HEADS = 8
QK_NOPE = 64
QK_ROPE = 32
V_HEAD = 64
Q_RANK = 256
KV_RANK = 128
ROPE_THETA = 10000.0
MLA_OUT = MLA_HEADS * V_HEAD
GM_GROUPS = 8
GM_CH = 64
GM_OUT = GM_GROUPS * GM_CH
CHUNK = 128
N_GROUPS = 4
EXP_PER_GROUP = 8
N_EXPERTS = N_GROUPS * EXP_PER_GROUP
TOP_K = 2
EPS = 1e-6
DEPTH = 1
ALPHA = (2.0 * DEPTH) ** 0.25
SM_SCALE = (QK_NOPE + QK_ROPE) ** -0.5
LOG2E = 1.4426950408889634

LANES = 128
SUBLANES = 8
TOKEN_ROWS = 8
VMEM_LIMIT = 56 * 1024 * 1024

PREP_ROWS = 256
ATTN_ROWS = 256
ROUTE_ROWS = 512
MOVE_ROWS = 256
MOVE_UNROLL = 8
EXPERT_ROWS = 256

C_Q = 0
C_KV = C_Q + Q_RANK
C_KR = C_KV + KV_RANK
C_U = C_KR + LANES
C_V = C_U + GM_OUT
C_END = C_V + GM_OUT
HP = MLA_HEADS * LANES

I_E0, I_E1, I_R0, I_R1, I_G0, I_G1 = range(6)
R_OFF = N_GROUPS


def _rms(v, g):
    return v * lax.rsqrt(jnp.mean(v * v, axis=-1, keepdims=True) + EPS) * g


def _ln(v, g, b):
    mu = jnp.mean(v, axis=-1, keepdims=True)
    d = v - mu
    var = jnp.mean(d * d, axis=-1, keepdims=True)
    return d * lax.rsqrt(var + EPS) * g + b


def _dot(a, b):
    return jnp.dot(a, b, preferred_element_type=F32)


def _prep_kernel(x_ref, pos_ref, win_ref, qg_ref, wq_ref, kvg_ref, wk_ref, wv_ref, inv_ref, rope_ref, one_ref,
                 lng_ref, lnb_ref, gavg_ref, ws_ref, bias_ref, gog_ref,
                 q_ref, k_ref, vt_ref, g_ref):
    rows = x_ref.shape[1]
    h = _dot(x_ref[0].astype(BF16), win_ref[...])

    ang = inv_ref[...] * pos_ref[0, 0].astype(F32)
    parts = []
    for t in (jnp.cos(ang), jnp.sin(ang)):
        hi = t.astype(BF16).astype(F32)
        parts += [hi, t - hi]
    tabs = _dot(jnp.concatenate(parts, axis=0).T.astype(BF16), rope_ref[...])
    cos_t = tabs[:, :LANES] + one_ref[...]
    sin_a = tabs[:, LANES:2 * LANES]
    sin_b = tabs[:, 2 * LANES:]
    half = QK_ROPE // 2

    def rotate(v):
        return v * cos_t + pltpu.roll(v, LANES - half, 1) * sin_a + pltpu.roll(v, half, 1) * sin_b

    cq = _rms(h[:, C_Q:C_Q + Q_RANK], qg_ref[...]).astype(BF16)
    q2 = _dot(cq, wq_ref[...])
    for hd in range(MLA_HEADS):
        lo = hd * LANES
        q_ref[0, :, lo:lo + LANES] = (rotate(q2[:, lo:lo + LANES]) * (SM_SCALE * LOG2E)).astype(BF16)

    ckv = _rms(h[:, C_KV:C_KV + KV_RANK], kvg_ref[...]).astype(BF16)
    kp = _dot(ckv, wk_ref[...])
    kr = rotate(h[:, C_KR:C_KR + LANES])
    for hd in range(MLA_HEADS):
        lo = hd * LANES
        k_ref[0, :, lo:lo + LANES] = (kp[:, lo:lo + LANES] + kr).astype(BF16)
    vt_ref[0, 0] = _dot(ckv, wv_ref[...]).T.astype(BF16)

    u = jax.nn.gelu(h[:, C_U:C_U + GM_OUT])
    vv = jax.nn.gelu(h[:, C_V:C_V + GM_OUT])
    mu = _dot(vv.astype(BF16), gavg_ref[...])
    d = vv - mu
    var = _dot((d * d).astype(BF16), gavg_ref[...])
    vn = (d * lax.rsqrt(var + EPS) * lng_ref[...] + lnb_ref[...]).astype(BF16)

    tri = lax.broadcasted_iota(jnp.int32, (CHUNK, CHUNK), 0) >= lax.broadcasted_iota(jnp.int32, (CHUNK, CHUNK), 1)
    wm = [jnp.where(tri, ws_ref[g], 0.0).astype(BF16) for g in range(GM_GROUPS)]
    low_half = lax.broadcasted_iota(jnp.int32, (CHUNK, LANES), 1) < GM_CH
    for c in range(rows // CHUNK):
        r0 = c * CHUNK
        parts = []
        for pr in range(GM_GROUPS // 2):
            tile = vn[r0:r0 + CHUNK, pr * LANES:(pr + 1) * LANES]
            parts.append(jnp.where(low_half, _dot(wm[2 * pr], tile), _dot(wm[2 * pr + 1], tile)))
        sg = jnp.concatenate(parts, axis=1) + bias_ref[...]
        gm = u[r0:r0 + CHUNK] * sg
        g_ref[0, r0:r0 + CHUNK, :] = _rms(gm, gog_ref[...]).astype(BF16)


def _prep(x, pos4, w):
    B, S, D = x.shape
    ts = PREP_ROWS
    full = lambda a: pl.BlockSpec(a.shape, lambda b, i: (0,) * a.ndim)
    consts = [w["win"], w["qg"], w["wq"], w["kvg"], w["wk"], w["wv"], w["inv"], w["rope"], w["one"],
              w["lng"], w["lnb"], w["gavg"], w["ws"], w["bias"], w["gog"]]
    return pl.pallas_call(
        _prep_kernel,
        grid=(B, S // ts),
        in_specs=[pl.BlockSpec((1, ts, D), lambda b, i: (b, i, 0)),
                  pl.BlockSpec((1, 1, 1, ts), lambda b, i: (b, i, 0, 0))] + [full(a) for a in consts],
        out_specs=[pl.BlockSpec((1, ts, HP), lambda b, i: (b, i, 0)),
                   pl.BlockSpec((1, ts, HP), lambda b, i: (b, i, 0)),
                   pl.BlockSpec((1, 1, MLA_OUT, ts), lambda b, i: (b, i, 0, 0)),
                   pl.BlockSpec((1, ts, GM_OUT), lambda b, i: (b, i, 0))],
        out_shape=[jax.ShapeDtypeStruct((B, S, HP), BF16)] * 2
        + [jax.ShapeDtypeStruct((B, S // ts, MLA_OUT, ts), BF16), jax.ShapeDtypeStruct((B, S, GM_OUT), BF16)],
        compiler_params=pltpu.CompilerParams(dimension_semantics=("parallel", "parallel"),
                                             vmem_limit_bytes=VMEM_LIMIT),
        name="prep",
    )(x, pos4, *consts)


def _attn_kernel(q_ref, k_ref, vt_ref, g_ref, x_ref, woa_ref, wog_ref, mog_ref, l1g_ref, l1b_ref,
                 o_ref, m_scr, l_scr, acc_scr, sa_scr, sb_scr):
    i = pl.program_id(1)
    tq = q_ref.shape[1]
    tk = tq
    key = lax.broadcasted_iota(jnp.int32, (tk, tq), 0)
    qry = lax.broadcasted_iota(jnp.int32, (tk, tq), 1)
    diag_mask = key <= qry

    m_scr[...] = jnp.full(m_scr.shape, -1e30, F32)
    l_scr[...] = jnp.zeros(l_scr.shape, F32)
    acc_scr[...] = jnp.zeros(acc_scr.shape, F32)

    def scores(j, s_scr):
        k0 = pl.multiple_of(j * tk, tk)
        for hd in range(MLA_HEADS):
            lo = hd * LANES
            qh = q_ref[0, :, lo:lo + LANES]
            kj = k_ref[0, pl.ds(k0, tk), lo:lo + LANES]
            s_scr[hd] = lax.dot_general(kj, qh, (((1,), (1,)), ((), ())), preferred_element_type=F32)

    def update(j, s_scr, masked):
        for hd in range(MLA_HEADS):
            s = s_scr[hd]
            vt = vt_ref[0, j, hd * V_HEAD:(hd + 1) * V_HEAD, :]
            if masked:
                s = jnp.where(diag_mask, s, -1e30)
            m_prev = m_scr[hd]
            m_new = jnp.maximum(m_prev, jnp.max(s, axis=0, keepdims=True))
            p = jnp.exp2(s - m_new)
            scale = jnp.exp2(m_prev - m_new)
            l_scr[hd] = scale * l_scr[hd] + jnp.sum(p, axis=0, keepdims=True)
            acc_scr[hd] = scale * acc_scr[hd] + _dot(vt, p.astype(BF16))
            m_scr[hd] = m_new

    def pair(jj, c):
        j = 2 * jj
        scores(j + 1, sb_scr)
        update(j, sa_scr, False)
        scores(j + 2, sa_scr)
        update(j + 1, sb_scr, False)
        return c

    scores(0, sa_scr)
    lax.fori_loop(0, lax.shift_right_logical(i, 1), pair, 0)

    @pl.when((i & 1) == 0)
    def _():
        update(i, sa_scr, True)

    @pl.when((i & 1) == 1)
    def _():
        scores(i, sb_scr)
        update(i - 1, sa_scr, False)
        update(i, sb_scr, True)

    at = jnp.concatenate([acc_scr[hd] / l_scr[hd] for hd in range(MLA_HEADS)], axis=0)
    at = at * lax.rsqrt(jnp.mean(at * at, axis=0, keepdims=True) + EPS) * mog_ref[...]
    mix = _dot(at.T.astype(BF16), woa_ref[...]) + _dot(g_ref[0], wog_ref[...])
    o_ref[0] = _ln(ALPHA * x_ref[0] + mix, l1g_ref[...], l1b_ref[...])


def _attn(q, k, vt, g, x, w):
    B, S, D = x.shape
    tq = ATTN_ROWS
    full = lambda a: pl.BlockSpec(a.shape, lambda b, i: (0,) * a.ndim)
    consts = [w["woa"], w["wog"], w["mog"], w["l1g"], w["l1b"]]
    return pl.pallas_call(
        _attn_kernel,
        grid=(B, S // tq),
        in_specs=[pl.BlockSpec((1, tq, HP), lambda b, i: (b, i, 0)),
                  pl.BlockSpec((1, S, HP), lambda b, i: (b, 0, 0)),
                  pl.BlockSpec((1,) + vt.shape[1:], lambda b, i: (b, 0, 0, 0)),
                  pl.BlockSpec((1, tq, GM_OUT), lambda b, i: (b, i, 0)),
                  pl.BlockSpec((1, tq, D), lambda b, i: (b, i, 0))] + [full(a) for a in consts],
        out_specs=pl.BlockSpec((1, tq, D), lambda b, i: (b, i, 0)),
        out_shape=jax.ShapeDtypeStruct((B, S, D), F32),
        scratch_shapes=[pltpu.VMEM((MLA_HEADS, 1, tq), F32), pltpu.VMEM((MLA_HEADS, 1, tq), F32),
                        pltpu.VMEM((MLA_HEADS, V_HEAD, tq), F32),
                        pltpu.VMEM((MLA_HEADS, tq, tq), F32), pltpu.VMEM((MLA_HEADS, tq, tq), F32)],
        compiler_params=pltpu.CompilerParams(dimension_semantics=("parallel", "parallel"),
                                             vmem_limit_bytes=VMEM_LIMIT),
        name="attn",
    )(q, k, vt, g, x, *consts)


def _route_kernel(x_ref, wr_ref, br_ref, info_ref, cnt_ref, carry_scr, tri_scr):
    step = pl.program_id(0)
    tt = x_ref.shape[0]

    @pl.when(step == 0)
    def _():
        carry_scr[...] = jnp.zeros_like(carry_scr)
        r = lax.broadcasted_iota(jnp.int32, (tt, tt), 0)
        c = lax.broadcasted_iota(jnp.int32, (tt, tt), 1)
        tri_scr[...] = jnp.where(c < r, 1.0, 0.0).astype(BF16)

    x = x_ref[...]
    xh = x.astype(BF16)
    xl = (x - xh.astype(F32)).astype(BF16)
    wr = wr_ref[...]
    wh = wr.astype(BF16)
    wl = (wr - wh.astype(F32)).astype(BF16)
    logits = _dot(xh, wh) + _dot(xl, wh) + _dot(xh, wl) + br_ref[...]

    lane = lax.broadcasted_iota(jnp.int32, (tt, LANES), 1)
    neg = jnp.float32(-jnp.inf)

    is_g = lane < N_GROUPS
    lg = jnp.where(is_g, logits, neg)
    gmax = jnp.max(lg, axis=-1, keepdims=True)
    g_idx = jnp.min(jnp.where(lg == gmax, lane, LANES), axis=-1, keepdims=True)
    g_den = jnp.sum(jnp.where(is_g, jnp.exp(lg - gmax), 0.0), axis=-1, keepdims=True)
    g_p = 1.0 / g_den

    in_grp = (lane >= R_OFF) & (lane < R_OFF + N_EXPERTS) & (((lane - R_OFF) >> 3) == g_idx)
    le = jnp.where(in_grp, logits, neg)
    m1 = jnp.max(le, axis=-1, keepdims=True)
    i1 = jnp.min(jnp.where(le == m1, lane, LANES), axis=-1, keepdims=True)
    le2 = jnp.where(lane == i1, neg, le)
    m2 = jnp.max(le2, axis=-1, keepdims=True)
    i2 = jnp.min(jnp.where(le2 == m2, lane, LANES), axis=-1, keepdims=True)
    e2 = jnp.exp(m2 - m1)
    gate0 = g_p / (1.0 + e2)
    gate1 = g_p * e2 / (1.0 + e2)

    hit1 = lane == i1
    hit2 = lane == i2
    onehot = jnp.where(hit1 | hit2, 1.0, 0.0)
    before = _dot(tri_scr[...], onehot.astype(BF16)) + carry_scr[...]
    r0 = jnp.sum(jnp.where(hit1, before, 0.0), axis=-1, keepdims=True)
    r1 = jnp.sum(jnp.where(hit2, before, 0.0), axis=-1, keepdims=True)
    carry_scr[...] = carry_scr[...] + jnp.sum(onehot, axis=0, keepdims=True)
    cnt_ref[...] = carry_scr[...]

    info = jnp.where(lane == I_E0, (i1 - R_OFF).astype(F32), 0.0)
    info = jnp.where(lane == I_E1, (i2 - R_OFF).astype(F32), info)
    info = jnp.where(lane == I_R0, r0, info)
    info = jnp.where(lane == I_R1, r1, info)
    info = jnp.where(lane == I_G0, gate0, info)
    info = jnp.where(lane == I_G1, gate1, info)
    info_ref[...] = info


def _route(x1, wr, br):
    T, D = x1.shape
    tt = ROUTE_ROWS
    return pl.pallas_call(
        _route_kernel,
        grid=(T // tt,),
        in_specs=[pl.BlockSpec((tt, D), lambda i: (i, 0)),
                  pl.BlockSpec(wr.shape, lambda i: (0, 0)),
                  pl.BlockSpec(br.shape, lambda i: (0, 0))],
        out_specs=[pl.BlockSpec((tt, LANES), lambda i: (i, 0)),
                   pl.BlockSpec((1, LANES), lambda i: (0, 0))],
        out_shape=[jax.ShapeDtypeStruct((T, LANES), F32), jax.ShapeDtypeStruct((1, LANES), F32)],
        scratch_shapes=[pltpu.VMEM((1, LANES), F32), pltpu.VMEM((tt, tt), BF16)],
        compiler_params=pltpu.CompilerParams(dimension_semantics=("arbitrary",), vmem_limit_bytes=VMEM_LIMIT),
        name="route",
    )(x1, wr, br)


def _to_token_tiles(dst_ref, val):
    dst_ref[...] = val.astype(BF16).reshape(dst_ref.shape)


def _from_token_tiles(src_ref, rows):
    return src_ref[...].reshape(rows, TOKEN_ROWS * LANES)


def _tile_copy(src_ref, src_row, dst_ref, dst_row, sem):
    return pltpu.make_async_copy(src_ref.at[pl.ds(pl.multiple_of(src_row, TOKEN_ROWS), TOKEN_ROWS)],
                                 dst_ref.at[pl.ds(pl.multiple_of(dst_row, TOKEN_ROWS), TOKEN_ROWS)], sem)


def _dispatch_kernel(seg_ref, dest_ref, x_ref, buf_ref, stage_scr, zero_scr, sem, zero_sem, *, n_steps):
    i = pl.program_id(0)
    rows = x_ref.shape[0]
    slot = i % 2

    @pl.when(i == 0)
    def _():
        zero_scr[...] = jnp.zeros(zero_scr.shape, BF16)

        block = EXPERT_ROWS * TOKEN_ROWS
        n_blocks = buf_ref.shape[0] // block

        def clear_rows(first):
            return pltpu.make_async_copy(zero_scr, buf_ref.at[pl.ds(pl.multiple_of(first, SUBLANES), block)], zero_sem)

        def clear(e):
            return clear_rows((seg_ref[0, e] - EXPERT_ROWS) * TOKEN_ROWS)

        def start_tail(b, c):
            clear_rows(b * block).start()
            return c

        def wait_tail(b, c):
            clear_rows(b * block).wait()
            return c

        for e in range(N_EXPERTS):
            pl.when(seg_ref[1, e] > 0)(lambda e=e: clear(e).start())
        lax.fori_loop(seg_ref[2, 0], n_blocks, start_tail, 0)
        for e in range(N_EXPERTS):
            pl.when(seg_ref[1, e] > 0)(lambda e=e: clear(e).wait())
        lax.fori_loop(seg_ref[2, 0], n_blocks, wait_tail, 0)

    def drain(s):
        for _ in range(TOP_K):
            pltpu.make_async_copy(stage_scr.at[s], stage_scr.at[s], sem.at[s]).wait()

    @pl.when(i >= 2)
    def _():
        drain(slot)

    _to_token_tiles(stage_scr.at[slot], x_ref[...])

    def start(c, carry):
        for u in range(MOVE_UNROLL):
            r = c * MOVE_UNROLL + u
            for kk in range(TOP_K):
                _tile_copy(stage_scr.at[slot], r * TOKEN_ROWS, buf_ref, dest_ref[0, 0, TOP_K * r + kk],
                           sem.at[slot]).start(priority=kk)
        return carry

    lax.fori_loop(0, rows // MOVE_UNROLL, start, 0)

    @pl.when(i == n_steps - 1)
    def _():
        drain(slot)
        if n_steps >= 2:
            drain(1 - slot)


def _dispatch(seg, dest3, x1, n_rows):
    T, D = x1.shape
    td = MOVE_ROWS
    n_steps = T // td
    grid_spec = pltpu.PrefetchScalarGridSpec(
        num_scalar_prefetch=1,
        grid=(n_steps,),
        in_specs=[pl.BlockSpec((1, 1, TOP_K * td), lambda i, seg: (i, 0, 0), memory_space=pltpu.SMEM),
                  pl.BlockSpec((td, D), lambda i, seg: (i, 0))],
        out_specs=pl.BlockSpec(memory_space=pl.ANY),
        scratch_shapes=[pltpu.VMEM((2, td * TOKEN_ROWS, LANES), BF16),
                        pltpu.VMEM((EXPERT_ROWS * TOKEN_ROWS, LANES), BF16),
                        pltpu.SemaphoreType.DMA((2,)), pltpu.SemaphoreType.DMA(())],
    )
    return pl.pallas_call(
        functools.partial(_dispatch_kernel, n_steps=n_steps),
        grid_spec=grid_spec,
        out_shape=jax.ShapeDtypeStruct((n_rows * TOKEN_ROWS, LANES), BF16),
        compiler_params=pltpu.CompilerParams(dimension_semantics=("arbitrary",), vmem_limit_bytes=VMEM_LIMIT),
        name="dispatch",
    )(seg, dest3, x1)


def _expert_kernel(be_ref, ne_ref, nu_ref, buf0_ref, bufa_ref, bufb_ref, wg_hbm, wu_hbm, wd_hbm, y_ref,
                   sg_scr, su_scr, sd_scr, wg_scr, wu_scr, wd_scr, xa_scr, xb_scr, cur_ref, sem):
    step = pl.program_id(0)
    bm = EXPERT_ROWS
    half = bm * TOKEN_ROWS

    def fetch(expert, s):
        return (pltpu.make_async_copy(wg_hbm.at[expert], sg_scr.at[s], sem.at[s, 0]),
                pltpu.make_async_copy(wu_hbm.at[expert], su_scr.at[s], sem.at[s, 1]),
                pltpu.make_async_copy(wd_hbm.at[expert], sd_scr.at[s], sem.at[s, 2]))

    @pl.when(step == 0)
    def _():
        cur_ref[0] = 0
        for c in fetch(be_ref[0], 0):
            c.start()
        xa_scr[...] = _from_token_tiles(buf0_ref, bm)

    def load_weights(blk):
        e = be_ref[blk]

        @pl.when((blk == 0) | (be_ref[jnp.maximum(blk - 1, 0)] != e))
        def _():
            s = cur_ref[0]
            for c in fetch(e, s):
                c.wait()
            wg_scr[...] = sg_scr[s].astype(BF16)
            wu_scr[...] = su_scr[s].astype(BF16)
            wd_scr[...] = sd_scr[s].astype(BF16)
            nxt = ne_ref[blk]

            @pl.when(nxt >= 0)
            def _():
                for c in fetch(nxt, 1 - s):
                    c.start()

            cur_ref[0] = 1 - s

    def run(blk, x_scr, nxt_ref, nxt_scr, out_rows):
        load_weights(blk)

        @pl.when(blk < nu_ref[0])
        def _():
            nxt_scr[...] = _from_token_tiles(nxt_ref, bm)
            xb = x_scr[...]
            hidden = jax.nn.silu(_dot(xb, wg_scr[...])) * _dot(xb, wu_scr[...])
            _to_token_tiles(y_ref.at[out_rows], _dot(hidden.astype(BF16), wd_scr[...]))

        @pl.when(blk >= nu_ref[0])
        def _():
            y_ref[out_rows, :] = jnp.zeros((half, LANES), BF16)

    run(2 * step, xa_scr, bufa_ref, xb_scr, pl.ds(0, half))
    run(2 * step + 1, xb_scr, bufb_ref, xa_scr, pl.ds(half, half))


def _experts(block_expert, next_expert, n_used, buf, w_gate, w_up, w_down):
    bm = EXPERT_ROWS
    D, ff = w_gate.shape[1:]
    n_blocks = buf.shape[0] // (bm * TOKEN_ROWS)
    assert n_blocks % 2 == 0
    last = n_blocks - 1
    grid_spec = pltpu.PrefetchScalarGridSpec(
        num_scalar_prefetch=3,
        grid=(n_blocks // 2,),
        in_specs=[pl.BlockSpec((bm * TOKEN_ROWS, LANES), lambda s, *_: (0, 0)),
                  pl.BlockSpec((bm * TOKEN_ROWS, LANES), lambda s, *_: (2 * s + 1, 0)),
                  pl.BlockSpec((bm * TOKEN_ROWS, LANES), lambda s, *_: (jnp.minimum(2 * s + 2, last), 0)),
                  pl.BlockSpec(memory_space=pl.ANY),
                  pl.BlockSpec(memory_space=pl.ANY),
                  pl.BlockSpec(memory_space=pl.ANY)],
        out_specs=pl.BlockSpec((2 * bm * TOKEN_ROWS, LANES), lambda s, *_: (s, 0)),
        scratch_shapes=[pltpu.VMEM((2, D, ff), F32), pltpu.VMEM((2, D, ff), F32), pltpu.VMEM((2, ff, D), F32),
                        pltpu.VMEM((D, ff), BF16), pltpu.VMEM((D, ff), BF16), pltpu.VMEM((ff, D), BF16),
                        pltpu.VMEM((bm, D), BF16), pltpu.VMEM((bm, D), BF16),
                        pltpu.SMEM((1,), jnp.int32), pltpu.SemaphoreType.DMA((2, 3))],
    )
    return pl.pallas_call(
        _expert_kernel,
        grid_spec=grid_spec,
        out_shape=jax.ShapeDtypeStruct(buf.shape, BF16),
        compiler_params=pltpu.CompilerParams(dimension_semantics=("arbitrary",), vmem_limit_bytes=VMEM_LIMIT),
        name="experts",
    )(block_expert, next_expert, n_used, buf, buf, buf, w_gate, w_up, w_down)


def _final_kernel(dcur_ref, dnxt_ref, x_ref, info_ref, y_ref, p_ref, wpg_ref, bpg_ref, wpp_ref,
                  l2g_ref, l2b_ref, l3g_ref, l3b_ref, o_ref, rows_scr, sem):
    i = pl.program_id(0)
    last = pl.num_programs(0) - 1
    rows = x_ref.shape[0]
    slot = i % 2

    def row_copy(dref, s, r, kk):
        return _tile_copy(y_ref, dref[0, 0, TOP_K * r + kk], rows_scr.at[s, kk], r * TOKEN_ROWS, sem.at[s])

    def landed(s):
        pltpu.make_async_copy(rows_scr.at[s], rows_scr.at[s], sem.at[s]).wait()

    @pl.when(i == 0)
    def _():
        def start(c, carry):
            for u in range(MOVE_UNROLL):
                for kk in range(TOP_K):
                    row_copy(dcur_ref, 0, c * MOVE_UNROLL + u, kk).start(priority=kk)
            return carry

        lax.fori_loop(0, rows // MOVE_UNROLL, start, 0)

    landed(slot)
    info = info_ref[...]
    gate0 = info[:, I_G0:I_G0 + 1]
    gate1 = info[:, I_G1:I_G1 + 1]
    moe = (_from_token_tiles(rows_scr.at[slot, 0], rows).astype(F32) * gate0
           + _from_token_tiles(rows_scr.at[slot, 1], rows).astype(F32) * gate1)

    for r in range(rows):
        for kk in range(TOP_K):
            row_copy(dnxt_ref, 1 - slot, r, kk).start(priority=kk)

    pp = _dot(p_ref[...].astype(BF16), wpp_ref[...])
    x2 = _ln(ALPHA * x_ref[...] + moe, l2g_ref[...], l2b_ref[...])
    gate = jax.nn.sigmoid(_dot(x2.astype(BF16), wpg_ref[...]) + bpg_ref[...])
    o_ref[...] = _ln(ALPHA * x2 + gate * pp, l3g_ref[...], l3b_ref[...])

    @pl.when(i == last)
    def _():
        landed(1 - slot)


def _final(dest3, x1, info, y, p2, w):
    T, D = x1.shape
    tc = MOVE_ROWS
    pd = p2.shape[1]
    full = lambda a: pl.BlockSpec(a.shape, lambda i: (0,) * a.ndim)
    consts = [w["wpg"], w["bpg"], w["wpp"], w["l2g"], w["l2b"], w["l3g"], w["l3b"]]
    last = T // tc - 1
    return pl.pallas_call(
        _final_kernel,
        grid=(T // tc,),
        in_specs=[pl.BlockSpec((1, 1, TOP_K * tc), lambda i: (i, 0, 0), memory_space=pltpu.SMEM),
                  pl.BlockSpec((1, 1, TOP_K * tc), lambda i: (jnp.minimum(i + 1, last), 0, 0), memory_space=pltpu.SMEM),
                  pl.BlockSpec((tc, D), lambda i: (i, 0)),
                  pl.BlockSpec((tc, LANES), lambda i: (i, 0)),
                  pl.BlockSpec(memory_space=pl.ANY),
                  pl.BlockSpec((tc, pd), lambda i: (i, 0))] + [full(a) for a in consts],
        out_specs=pl.BlockSpec((tc, D), lambda i: (i, 0)),
        out_shape=jax.ShapeDtypeStruct((T, D), F32),
        scratch_shapes=[pltpu.VMEM((2, TOP_K, tc * TOKEN_ROWS, LANES), BF16), pltpu.SemaphoreType.DMA((2,))],
        compiler_params=pltpu.CompilerParams(dimension_semantics=("arbitrary",), vmem_limit_bytes=VMEM_LIMIT),
        name="final",
    )(dest3, dest3, x1, info, y, p2, *consts)


def _pad_heads(a, width):
    lead = a.shape[:-1]
    a = a.reshape(lead + (MLA_HEADS, width))
    a = jnp.pad(a, [(0, 0)] * len(lead) + [(0, 0), (0, LANES - width)])
    return a.reshape(lead + (HP,))


def _layer_weights(w_in, q_norm_g, w_q_up, kv_norm_g, w_kv_up, gm_ln_g, gm_ln_b, gm_w_s, gm_b_s,
                   mla_out_g, gm_out_g, w_o, ln1_g, ln1_b):
    D = w_in.shape[0]
    half = QK_ROPE // 2
    c1, c2, c3 = Q_RANK, Q_RANK + KV_RANK, Q_RANK + KV_RANK + QK_ROPE
    zeros = lambda *s: jnp.zeros(s, F32)
    kr = jnp.concatenate([zeros(D, QK_NOPE), w_in[:, c2:c3], zeros(D, LANES - QK_NOPE - QK_ROPE)], axis=1)
    win = jnp.concatenate([w_in[:, :c2], kr, w_in[:, c3:]], axis=1).astype(BF16)
    wq = _pad_heads(w_q_up, QK_NOPE + QK_ROPE).astype(BF16)

    wkv3 = w_kv_up.reshape(KV_RANK, MLA_HEADS, QK_NOPE + V_HEAD)
    wk = _pad_heads(wkv3[..., :QK_NOPE].reshape(KV_RANK, -1), QK_NOPE).astype(BF16)
    wv = wkv3[..., QK_NOPE:].reshape(KV_RANK, -1).astype(BF16)

    inv = (ROPE_THETA ** (-jnp.arange(0, QK_ROPE, 2, dtype=F32) / QK_ROPE))[:, None]
    eye = jnp.eye(half, dtype=F32)
    first = jnp.pad(eye, ((0, 0), (QK_NOPE, LANES - QK_NOPE - half)))
    second = jnp.pad(eye, ((0, 0), (QK_NOPE + half, LANES - QK_NOPE - QK_ROPE)))
    zero = jnp.zeros_like(first)
    cos_rows = jnp.concatenate([first + second, zero, zero], axis=1)
    sin_rows = jnp.concatenate([zero, -first, second], axis=1)
    rope = jnp.concatenate([cos_rows, cos_rows, sin_rows, sin_rows], axis=0).astype(BF16)
    lane = jnp.arange(LANES)
    one = jnp.where((lane >= QK_NOPE) & (lane < QK_NOPE + QK_ROPE), 0.0, 1.0)[None, :]

    grp = jnp.arange(GM_OUT) // GM_CH
    gavg = jnp.where(grp[:, None] == grp[None, :], 1.0 / GM_CH, 0.0).astype(BF16)
    bias = jnp.repeat(gm_b_s.T, GM_CH, axis=1)

    woa = w_o[:MLA_OUT].astype(BF16)
    wog = w_o[MLA_OUT:].astype(BF16)
    return dict(win=win, qg=q_norm_g[None, :], wq=wq, kvg=kv_norm_g[None, :], wk=wk, wv=wv, inv=inv, rope=rope, one=one,
                lng=gm_ln_g[None, :], lnb=gm_ln_b[None, :], gavg=gavg, ws=gm_w_s, bias=bias, gog=gm_out_g[None, :],
                woa=woa, wog=wog, mog=mla_out_g[:, None], l1g=ln1_g[None, :], l1b=ln1_b[None, :])


def _moe(x1, w_rg, b_rg, w_re, b_re, w_gate, w_up, w_down):
    T, D = x1.shape
    pad = jnp.zeros((D, LANES - N_GROUPS - N_EXPERTS), F32)
    wr = jnp.concatenate([w_rg, w_re, pad], axis=1)
    br = jnp.concatenate([b_rg, b_re, pad[0]])[None, :]
    info, cnt = _route(x1, wr, br)

    bm = EXPERT_ROWS
    n_blocks = (T * TOP_K) // bm + N_EXPERTS
    counts = cnt[0, R_OFF:R_OFF + N_EXPERTS].astype(jnp.int32)
    padded = (counts + bm - 1) // bm * bm
    pad_ends = jnp.cumsum(padded)
    pad_starts = pad_ends - padded
    e_idx = info[:, I_E0:I_E1 + 1].astype(jnp.int32)
    rank = info[:, I_R0:I_R1 + 1].astype(jnp.int32)
    seg_start = jnp.sum(jnp.where(e_idx[..., None] == jnp.arange(N_EXPERTS), pad_starts, 0), axis=-1)
    dest = ((seg_start + rank) * TOKEN_ROWS).reshape(T // MOVE_ROWS, 1, TOP_K * MOVE_ROWS)
    block_start = jnp.arange(n_blocks, dtype=jnp.int32) * bm
    block_expert = jnp.minimum(jnp.sum(pad_ends[None, :] <= block_start[:, None], axis=1),
                               N_EXPERTS - 1).astype(jnp.int32)

    blk = jnp.arange(n_blocks)
    later = (blk[None, :] > blk[:, None]) & (block_expert[None, :] != block_expert[:, None])
    next_expert = jnp.min(jnp.where(later, block_expert[None, :], N_EXPERTS), axis=1)
    next_expert = jnp.where(next_expert == N_EXPERTS, -1, next_expert).astype(jnp.int32)
    n_used = (pad_ends[-1:] // bm).astype(jnp.int32)

    seg = jnp.stack([pad_ends, padded, jnp.broadcast_to(n_used, (N_EXPERTS,))]).astype(jnp.int32)
    buf = _dispatch(seg, dest, x1, n_blocks * bm)
    y = _experts(block_expert, next_expert, n_used, buf, w_gate, w_up, w_down)
    return info, dest, y


def kernel(x, p, positions, w_in, q_norm_g, w_q_up, kv_norm_g, w_kv_up, gm_ln_g, gm_ln_b, gm_w_s, gm_b_s, mla_out_g, gm_out_g, w_o, ln1_g, ln1_b, w_rg, b_rg, w_re, b_re, w_gate, w_up, w_down, ln2_g, ln2_b, w_pg, b_pg, w_pp, ln3_g, ln3_b):
    B, S, D = x.shape
    T = B * S
    assert S % ATTN_ROWS == 0 and PREP_ROWS == ATTN_ROWS and PREP_ROWS % CHUNK == 0
    assert T % ROUTE_ROWS == 0 and T % MOVE_ROWS == 0 and (T * TOP_K) % EXPERT_ROWS == 0
    assert D == TOKEN_ROWS * LANES and MOVE_ROWS % MOVE_UNROLL == 0
    pos4 = positions.reshape(B, S // PREP_ROWS, 1, PREP_ROWS)
    for i in range(DEPTH):
        w = _layer_weights(w_in[i], q_norm_g[i], w_q_up[i], kv_norm_g[i], w_kv_up[i], gm_ln_g[i], gm_ln_b[i],
                           gm_w_s[i], gm_b_s[i], mla_out_g[i], gm_out_g[i], w_o[i], ln1_g[i], ln1_b[i])
        q, k, vt, g = _prep(x, pos4, w)
        x1 = _attn(q, k, vt, g, x, w).reshape(T, D)
        info, dest, y = _moe(x1, w_rg[i], b_rg[i], w_re[i], b_re[i], w_gate[i], w_up[i], w_down[i])
        wf = dict(wpg=w_pg[i].astype(BF16), bpg=b_pg[i][None, :], wpp=w_pp[i].astype(BF16),
                  l2g=ln2_g[i][None, :], l2b=ln2_b[i][None, :], l3g=ln3_g[i][None, :], l3b=ln3_b[i][None, :])
        x = _final(dest, x1, info, y, p[i].reshape(T, -1), wf).reshape(B, S, D)
    return x
```

```python
import functools

import jax
import jax.numpy as jnp
from jax import lax
from jax.experimental import pallas as pl
from jax.experimental.pallas import tpu as pltpu

F32 = jnp.float32
BF16 = jnp.bfloat16

MLA_HEADS = 8
QK_NOPE = 64
QK_ROPE = 32
V_HEAD = 64
Q_RANK = 256
KV_RANK = 128
ROPE_THETA = 10000.0
MLA_OUT = MLA_HEADS * V_HEAD
GM_GROUPS = 8
GM_CH = 64
GM_OUT = GM_GROUPS * GM_CH
CHUNK = 128
N_GROUPS = 4
EXP_PER_GROUP = 8
N_EXPERTS = N_GROUPS * EXP_PER_GROUP
TOP_K = 2
EPS = 1e-6
DEPTH = 1
ALPHA = (2.0 * DEPTH) ** 0.25
SM_SCALE = (QK_NOPE + QK_ROPE) ** -0.5
LOG2E = 1.4426950408889634

LANES = 128
SUBLANES = 8
TOKEN_ROWS = 8
ONES_ROWS = 16
VMEM_LIMIT = 56 * 1024 * 1024

PREP_ROWS = 256
ATTN_ROWS = 256
ROUTE_ROWS = 512
MOVE_ROWS = 256
MOVE_UNROLL = 8
EXPERT_ROWS = 256

C_Q = 0
C_KV = C_Q + Q_RANK
C_KR = C_KV + KV_RANK
C_U = C_KR + LANES
C_V = C_U + GM_OUT
C_END = C_V + GM_OUT
HP = MLA_HEADS * LANES

I_E0, I_E1, I_R0, I_R1, I_G0, I_G1 = range(6)
R_OFF = N_GROUPS


def _rms(v, g):
    return v * lax.rsqrt(jnp.mean(v * v, axis=-1, keepdims=True) + EPS) * g


def _ln(v, g, b):
    mu = jnp.mean(v, axis=-1, keepdims=True)
    d = v - mu
    var = jnp.mean(d * d, axis=-1, keepdims=True)
    return d * lax.rsqrt(var + EPS) * g + b


def _dot(a, b):
    return jnp.dot(a, b, preferred_element_type=F32)


def _prep_kernel(x_ref, pos_ref, win_ref, qg_ref, wq_ref, kvg_ref, wk_ref, wv_ref, inv_ref, rope_ref, one_ref,
                 lng_ref, lnb_ref, gavg_ref, ws_ref, bias_ref, gog_ref,
                 q_ref, k_ref, vt_ref, g_ref):
    rows = x_ref.shape[1]
    h = _dot(x_ref[0].astype(BF16), win_ref[...])

    ang = inv_ref[...] * pos_ref[0, 0].astype(F32)
    parts = []
    for t in (jnp.cos(ang), jnp.sin(ang)):
        hi = t.astype(BF16).astype(F32)
        parts += [hi, t - hi]
    tabs = _dot(jnp.concatenate(parts, axis=0).T.astype(BF16), rope_ref[...])
    cos_t = tabs[:, :LANES] + one_ref[...]
    sin_a = tabs[:, LANES:2 * LANES]
    sin_b = tabs[:, 2 * LANES:]
    half = QK_ROPE // 2

    def rotate(v):
        return v * cos_t + pltpu.roll(v, LANES - half, 1) * sin_a + pltpu.roll(v, half, 1) * sin_b

    cq = _rms(h[:, C_Q:C_Q + Q_RANK], qg_ref[...]).astype(BF16)
    q2 = _dot(cq, wq_ref[...])
    for hd in range(MLA_HEADS):
        lo = hd * LANES
        q_ref[0, :, lo:lo + LANES] = (rotate(q2[:, lo:lo + LANES]) * (SM_SCALE * LOG2E)).astype(BF16)

    ckv = _rms(h[:, C_KV:C_KV + KV_RANK], kvg_ref[...]).astype(BF16)
    kp = _dot(ckv, wk_ref[...])
    kr = rotate(h[:, C_KR:C_KR + LANES])
    for hd in range(MLA_HEADS):
        lo = hd * LANES
        k_ref[0, :, lo:lo + LANES] = (kp[:, lo:lo + LANES] + kr).astype(BF16)
    vt_ref[0, 0] = _dot(ckv, wv_ref[...]).T.astype(BF16)

    u = jax.nn.gelu(h[:, C_U:C_U + GM_OUT])
    vv = jax.nn.gelu(h[:, C_V:C_V + GM_OUT])
    mu = _dot(vv.astype(BF16), gavg_ref[...])
    d = vv - mu
    var = _dot((d * d).astype(BF16), gavg_ref[...])
    vn = (d * lax.rsqrt(var + EPS) * lng_ref[...] + lnb_ref[...]).astype(BF16)

    tri = lax.broadcasted_iota(jnp.int32, (CHUNK, CHUNK), 0) >= lax.broadcasted_iota(jnp.int32, (CHUNK, CHUNK), 1)
    wm = [jnp.where(tri, ws_ref[g], 0.0).astype(BF16) for g in range(GM_GROUPS)]
    low_half = lax.broadcasted_iota(jnp.int32, (CHUNK, LANES), 1) < GM_CH
    for c in range(rows // CHUNK):
        r0 = c * CHUNK
        parts = []
        for pr in range(GM_GROUPS // 2):
            tile = vn[r0:r0 + CHUNK, pr * LANES:(pr + 1) * LANES]
            parts.append(jnp.where(low_half, _dot(wm[2 * pr], tile), _dot(wm[2 * pr + 1], tile)))
        sg = jnp.concatenate(parts, axis=1) + bias_ref[...]
        gm = u[r0:r0 + CHUNK] * sg
        g_ref[0, r0:r0 + CHUNK, :] = _rms(gm, gog_ref[...]).astype(BF16)


def _prep(x, pos4, w):
    B, S, D = x.shape
    ts = PREP_ROWS
    full = lambda a: pl.BlockSpec(a.shape, lambda b, i: (0,) * a.ndim)
    consts = [w["win"], w["qg"], w["wq"], w["kvg"], w["wk"], w["wv"], w["inv"], w["rope"], w["one"],
              w["lng"], w["lnb"], w["gavg"], w["ws"], w["bias"], w["gog"]]
    return pl.pallas_call(
        _prep_kernel,
        grid=(B, S // ts),
        in_specs=[pl.BlockSpec((1, ts, D), lambda b, i: (b, i, 0)),
                  pl.BlockSpec((1, 1, 1, ts), lambda b, i: (b, i, 0, 0))] + [full(a) for a in consts],
        out_specs=[pl.BlockSpec((1, ts, HP), lambda b, i: (b, i, 0)),
                   pl.BlockSpec((1, ts, HP), lambda b, i: (b, i, 0)),
                   pl.BlockSpec((1, 1, MLA_OUT, ts), lambda b, i: (b, i, 0, 0)),
                   pl.BlockSpec((1, ts, GM_OUT), lambda b, i: (b, i, 0))],
        out_shape=[jax.ShapeDtypeStruct((B, S, HP), BF16)] * 2
        + [jax.ShapeDtypeStruct((B, S // ts, MLA_OUT, ts), BF16), jax.ShapeDtypeStruct((B, S, GM_OUT), BF16)],
        compiler_params=pltpu.CompilerParams(dimension_semantics=("parallel", "parallel"),
                                             vmem_limit_bytes=VMEM_LIMIT),
        name="prep",
    )(x, pos4, *consts)


def _attn_kernel(q_ref, k_ref, vt_ref, g_ref, x_ref, woa_ref, wog_ref, mog_ref, l1g_ref, l1b_ref,
                 o_ref, m_scr, acc_scr, sa_scr, sb_scr, res_scr):
    i = pl.program_id(1)
    tq = q_ref.shape[1]
    tk = tq
    key = lax.broadcasted_iota(jnp.int32, (tk, tq), 0)
    qry = lax.broadcasted_iota(jnp.int32, (tk, tq), 1)
    diag_mask = key <= qry

    m_scr[...] = jnp.full(m_scr.shape, -1e30, F32)
    acc_scr[...] = jnp.zeros(acc_scr.shape, F32)
    ones = jnp.ones((ONES_ROWS, tk), BF16)

    def scores(j, s_scr):
        k0 = pl.multiple_of(j * tk, tk)
        for hd in range(MLA_HEADS):
            lo = hd * LANES
            qh = q_ref[0, :, lo:lo + LANES]
            kj = k_ref[0, pl.ds(k0, tk), lo:lo + LANES]
            s_scr[hd] = lax.dot_general(kj, qh, (((1,), (1,)), ((), ())), preferred_element_type=F32)

    def update(j, s_scr, masked):
        for hd in range(MLA_HEADS):
            s = s_scr[hd]
            vt = vt_ref[0, j, hd * V_HEAD:(hd + 1) * V_HEAD, :]
            if masked:
                s = jnp.where(diag_mask, s, -1e30)
            m_prev = m_scr[hd]
            m_new = jnp.maximum(m_prev, jnp.max(s, axis=0, keepdims=True))
            p = jnp.exp2(s - m_new).astype(BF16)
            scale = jnp.exp2(m_prev - m_new)
            acc_scr[hd] = scale * acc_scr[hd] + _dot(jnp.concatenate([vt, ones], axis=0), p)
            m_scr[hd] = m_new

    def residual():
        res_scr[...] = ALPHA * x_ref[0] + _dot(g_ref[0], wog_ref[...])

    def pair(jj, c):
        j = 2 * jj
        scores(j + 1, sb_scr)
        update(j, sa_scr, False)
        scores(j + 2, sa_scr)
        update(j + 1, sb_scr, False)
        return c

    scores(0, sa_scr)
    lax.fori_loop(0, lax.shift_right_logical(i, 1), pair, 0)

    @pl.when((i & 1) == 0)
    def _():
        residual()
        update(i, sa_scr, True)

    @pl.when((i & 1) == 1)
    def _():
        scores(i, sb_scr)
        update(i - 1, sa_scr, False)
        residual()
        update(i, sb_scr, True)

    at = jnp.concatenate([acc_scr[hd, :V_HEAD] / acc_scr[hd, V_HEAD:V_HEAD + 1] for hd in range(MLA_HEADS)],
                         axis=0)
    at = at * lax.rsqrt(jnp.mean(at * at, axis=0, keepdims=True) + EPS) * mog_ref[...]
    o_ref[0] = _ln(res_scr[...] + _dot(at.T.astype(BF16), woa_ref[...]), l1g_ref[...], l1b_ref[...])


def _attn(q, k, vt, g, x, w):
    B, S, D = x.shape
    tq = ATTN_ROWS
    full = lambda a: pl.BlockSpec(a.shape, lambda b, i: (0,) * a.ndim)
    consts = [w["woa"], w["wog"], w["mog"], w["l1g"], w["l1b"]]
    return pl.pallas_call(
        _attn_kernel,
        grid=(B, S // tq),
        in_specs=[pl.BlockSpec((1, tq, HP), lambda b, i: (b, i, 0)),
                  pl.BlockSpec((1, S, HP), lambda b, i: (b, 0, 0)),
                  pl.BlockSpec((1,) + vt.shape[1:], lambda b, i: (b, 0, 0, 0)),
                  pl.BlockSpec((1, tq, GM_OUT), lambda b, i: (b, i, 0)),
                  pl.BlockSpec((1, tq, D), lambda b, i: (b, i, 0))] + [full(a) for a in consts],
        out_specs=pl.BlockSpec((1, tq, D), lambda b, i: (b, i, 0)),
        out_shape=jax.ShapeDtypeStruct((B, S, D), F32),
        scratch_shapes=[pltpu.VMEM((MLA_HEADS, 1, tq), F32),
                        pltpu.VMEM((MLA_HEADS, V_HEAD + ONES_ROWS, tq), F32),
                        pltpu.VMEM((MLA_HEADS, tq, tq), F32), pltpu.VMEM((MLA_HEADS, tq, tq), F32),
                        pltpu.VMEM((tq, D), F32)],
        compiler_params=pltpu.CompilerParams(dimension_semantics=("parallel", "parallel"),
                                             vmem_limit_bytes=VMEM_LIMIT),
        name="attn",
    )(q, k, vt, g, x, *consts)


def _route_kernel(x_ref, wr_ref, br_ref, info_ref, cnt_ref, carry_scr, tri_scr):
    step = pl.program_id(0)
    tt = x_ref.shape[0]

    @pl.when(step == 0)
    def _():
        carry_scr[...] = jnp.zeros_like(carry_scr)
        r = lax.broadcasted_iota(jnp.int32, (tt, tt), 0)
        c = lax.broadcasted_iota(jnp.int32, (tt, tt), 1)
        tri_scr[...] = jnp.where(c < r, 1.0, 0.0).astype(BF16)

    x = x_ref[...]
    xh = x.astype(BF16)
    xl = (x - xh.astype(F32)).astype(BF16)
    wr = wr_ref[...]
    wh = wr.astype(BF16)
    wl = (wr - wh.astype(F32)).astype(BF16)
    logits = _dot(xh, wh) + _dot(xl, wh) + _dot(xh, wl) + br_ref[...]

    lane = lax.broadcasted_iota(jnp.int32, (tt, LANES), 1)
    neg = jnp.float32(-jnp.inf)

    is_g = lane < N_GROUPS
    lg = jnp.where(is_g, logits, neg)
    gmax = jnp.max(lg, axis=-1, keepdims=True)
    g_idx = jnp.min(jnp.where(lg == gmax, lane, LANES), axis=-1, keepdims=True)
    g_den = jnp.sum(jnp.where(is_g, jnp.exp(lg - gmax), 0.0), axis=-1, keepdims=True)
    g_p = 1.0 / g_den

    in_grp = (lane >= R_OFF) & (lane < R_OFF + N_EXPERTS) & (((lane - R_OFF) >> 3) == g_idx)
    le = jnp.where(in_grp, logits, neg)
    m1 = jnp.max(le, axis=-1, keepdims=True)
    i1 = jnp.min(jnp.where(le == m1, lane, LANES), axis=-1, keepdims=True)
    le2 = jnp.where(lane == i1, neg, le)
    m2 = jnp.max(le2, axis=-1, keepdims=True)
    i2 = jnp.min(jnp.where(le2 == m2, lane, LANES), axis=-1, keepdims=True)
    e2 = jnp.exp(m2 - m1)
    gate0 = g_p / (1.0 + e2)
    gate1 = g_p * e2 / (1.0 + e2)

    hit1 = lane == i1
    hit2 = lane == i2
    onehot = jnp.where(hit1 | hit2, 1.0, 0.0)
    before = _dot(tri_scr[...], onehot.astype(BF16)) + carry_scr[...]
    r0 = jnp.sum(jnp.where(hit1, before, 0.0), axis=-1, keepdims=True)
    r1 = jnp.sum(jnp.where(hit2, before, 0.0), axis=-1, keepdims=True)
    carry_scr[...] = carry_scr[...] + jnp.sum(onehot, axis=0, keepdims=True)
    cnt_ref[...] = carry_scr[...]

    info = jnp.where(lane == I_E0, (i1 - R_OFF).astype(F32), 0.0)
    info = jnp.where(lane == I_E1, (i2 - R_OFF).astype(F32), info)
    info = jnp.where(lane == I_R0, r0, info)
    info = jnp.where(lane == I_R1, r1, info)
    info = jnp.where(lane == I_G0, gate0, info)
    info = jnp.where(lane == I_G1, gate1, info)
    info_ref[...] = info


def _route(x1, wr, br):
    T, D = x1.shape
    tt = ROUTE_ROWS
    return pl.pallas_call(
        _route_kernel,
        grid=(T // tt,),
        in_specs=[pl.BlockSpec((tt, D), lambda i: (i, 0)),
                  pl.BlockSpec(wr.shape, lambda i: (0, 0)),
                  pl.BlockSpec(br.shape, lambda i: (0, 0))],
        out_specs=[pl.BlockSpec((tt, LANES), lambda i: (i, 0)),
                   pl.BlockSpec((1, LANES), lambda i: (0, 0))],
        out_shape=[jax.ShapeDtypeStruct((T, LANES), F32), jax.ShapeDtypeStruct((1, LANES), F32)],
        scratch_shapes=[pltpu.VMEM((1, LANES), F32), pltpu.VMEM((tt, tt), BF16)],
        compiler_params=pltpu.CompilerParams(dimension_semantics=("arbitrary",), vmem_limit_bytes=VMEM_LIMIT),
        name="route",
    )(x1, wr, br)


def _to_token_tiles(dst_ref, val):
    dst_ref[...] = val.astype(BF16).reshape(dst_ref.shape)


def _from_token_tiles(src_ref, rows):
    return src_ref[...].reshape(rows, TOKEN_ROWS * LANES)


def _tile_copy(src_ref, src_row, dst_ref, dst_row, sem):
    return pltpu.make_async_copy(src_ref.at[pl.ds(pl.multiple_of(src_row, TOKEN_ROWS), TOKEN_ROWS)],
                                 dst_ref.at[pl.ds(pl.multiple_of(dst_row, TOKEN_ROWS), TOKEN_ROWS)], sem)


def _dispatch_kernel(seg_ref, dest_ref, x_ref, buf_ref, stage_scr, zero_scr, sem, zero_sem, *, n_steps):
    i = pl.program_id(0)
    rows = x_ref.shape[0]
    slot = i % 2

    @pl.when(i == 0)
    def _():
        zero_scr[...] = jnp.zeros(zero_scr.shape, BF16)

        block = EXPERT_ROWS * TOKEN_ROWS
        n_blocks = buf_ref.shape[0] // block

        def clear_rows(first):
            return pltpu.make_async_copy(zero_scr, buf_ref.at[pl.ds(pl.multiple_of(first, SUBLANES), block)], zero_sem)

        def clear(e):
            return clear_rows((seg_ref[0, e] - EXPERT_ROWS) * TOKEN_ROWS)

        def start_tail(b, c):
            clear_rows(b * block).start()
            return c

        def wait_tail(b, c):
            clear_rows(b * block).wait()
            return c

        for e in range(N_EXPERTS):
            pl.when(seg_ref[1, e] > 0)(lambda e=e: clear(e).start())
        lax.fori_loop(seg_ref[2, 0], n_blocks, start_tail, 0)
        for e in range(N_EXPERTS):
            pl.when(seg_ref[1, e] > 0)(lambda e=e: clear(e).wait())
        lax.fori_loop(seg_ref[2, 0], n_blocks, wait_tail, 0)

    def drain(s):
        for _ in range(TOP_K):
            pltpu.make_async_copy(stage_scr.at[s], stage_scr.at[s], sem.at[s]).wait()

    @pl.when(i >= 2)
    def _():
        drain(slot)

    _to_token_tiles(stage_scr.at[slot], x_ref[...])

    def start(c, carry):
        for u in range(MOVE_UNROLL):
            r = c * MOVE_UNROLL + u
            for kk in range(TOP_K):
                _tile_copy(stage_scr.at[slot], r * TOKEN_ROWS, buf_ref, dest_ref[0, 0, TOP_K * r + kk],
                           sem.at[slot]).start(priority=kk)
        return carry

    lax.fori_loop(0, rows // MOVE_UNROLL, start, 0)

    @pl.when(i == n_steps - 1)
    def _():
        drain(slot)
        if n_steps >= 2:
            drain(1 - slot)


def _dispatch(seg, dest3, x1, n_rows):
    T, D = x1.shape
    td = MOVE_ROWS
    n_steps = T // td
    grid_spec = pltpu.PrefetchScalarGridSpec(
        num_scalar_prefetch=1,
        grid=(n_steps,),
        in_specs=[pl.BlockSpec((1, 1, TOP_K * td), lambda i, seg: (i, 0, 0), memory_space=pltpu.SMEM),
                  pl.BlockSpec((td, D), lambda i, seg: (i, 0))],
        out_specs=pl.BlockSpec(memory_space=pl.ANY),
        scratch_shapes=[pltpu.VMEM((2, td * TOKEN_ROWS, LANES), BF16),
                        pltpu.VMEM((EXPERT_ROWS * TOKEN_ROWS, LANES), BF16),
                        pltpu.SemaphoreType.DMA((2,)), pltpu.SemaphoreType.DMA(())],
    )
    return pl.pallas_call(
        functools.partial(_dispatch_kernel, n_steps=n_steps),
        grid_spec=grid_spec,
        out_shape=jax.ShapeDtypeStruct((n_rows * TOKEN_ROWS, LANES), BF16),
        compiler_params=pltpu.CompilerParams(dimension_semantics=("arbitrary",), vmem_limit_bytes=VMEM_LIMIT),
        name="dispatch",
    )(seg, dest3, x1)


def _expert_kernel(be_ref, ne_ref, nu_ref, buf0_ref, bufa_ref, bufb_ref, wg_hbm, wu_hbm, wd_hbm, y_ref,
                   sg_scr, su_scr, sd_scr, wg_scr, wu_scr, wd_scr, xa_scr, xb_scr, cur_ref, sem):
    step = pl.program_id(0)
    bm = EXPERT_ROWS
    half = bm * TOKEN_ROWS

    def fetch(expert, s):
        return (pltpu.make_async_copy(wg_hbm.at[expert], sg_scr.at[s], sem.at[s, 0]),
                pltpu.make_async_copy(wu_hbm.at[expert], su_scr.at[s], sem.at[s, 1]),
                pltpu.make_async_copy(wd_hbm.at[expert], sd_scr.at[s], sem.at[s, 2]))

    @pl.when(step == 0)
    def _():
        cur_ref[0] = 0
        for c in fetch(be_ref[0], 0):
            c.start()
        xa_scr[...] = _from_token_tiles(buf0_ref, bm)

    def load_weights(blk):
        e = be_ref[blk]

        @pl.when((blk == 0) | (be_ref[jnp.maximum(blk - 1, 0)] != e))
        def _():
            s = cur_ref[0]
            for c in fetch(e, s):
                c.wait()
            wg_scr[...] = sg_scr[s].astype(BF16)
            wu_scr[...] = su_scr[s].astype(BF16)
            wd_scr[...] = sd_scr[s].astype(BF16)
            nxt = ne_ref[blk]

            @pl.when(nxt >= 0)
            def _():
                for c in fetch(nxt, 1 - s):
                    c.start()

            cur_ref[0] = 1 - s

    def run(blk, x_scr, nxt_ref, nxt_scr, out_rows):
        load_weights(blk)

        @pl.when(blk < nu_ref[0])
        def _():
            nxt_scr[...] = _from_token_tiles(nxt_ref, bm)
            xb = x_scr[...]
            hidden = jax.nn.silu(_dot(xb, wg_scr[...])) * _dot(xb, wu_scr[...])
            _to_token_tiles(y_ref.at[out_rows], _dot(hidden.astype(BF16), wd_scr[...]))

        @pl.when(blk >= nu_ref[0])
        def _():
            y_ref[out_rows, :] = jnp.zeros((half, LANES), BF16)

    run(2 * step, xa_scr, bufa_ref, xb_scr, pl.ds(0, half))
    run(2 * step + 1, xb_scr, bufb_ref, xa_scr, pl.ds(half, half))


def _experts(block_expert, next_expert, n_used, buf, w_gate, w_up, w_down):
    bm = EXPERT_ROWS
    D, ff = w_gate.shape[1:]
    n_blocks = buf.shape[0] // (bm * TOKEN_ROWS)
    assert n_blocks % 2 == 0
    last = n_blocks - 1
    grid_spec = pltpu.PrefetchScalarGridSpec(
        num_scalar_prefetch=3,
        grid=(n_blocks // 2,),
        in_specs=[pl.BlockSpec((bm * TOKEN_ROWS, LANES), lambda s, *_: (0, 0)),
                  pl.BlockSpec((bm * TOKEN_ROWS, LANES), lambda s, *_: (2 * s + 1, 0)),
                  pl.BlockSpec((bm * TOKEN_ROWS, LANES), lambda s, *_: (jnp.minimum(2 * s + 2, last), 0)),
                  pl.BlockSpec(memory_space=pl.ANY),
                  pl.BlockSpec(memory_space=pl.ANY),
                  pl.BlockSpec(memory_space=pl.ANY)],
        out_specs=pl.BlockSpec((2 * bm * TOKEN_ROWS, LANES), lambda s, *_: (s, 0)),
        scratch_shapes=[pltpu.VMEM((2, D, ff), F32), pltpu.VMEM((2, D, ff), F32), pltpu.VMEM((2, ff, D), F32),
                        pltpu.VMEM((D, ff), BF16), pltpu.VMEM((D, ff), BF16), pltpu.VMEM((ff, D), BF16),
                        pltpu.VMEM((bm, D), BF16), pltpu.VMEM((bm, D), BF16),
                        pltpu.SMEM((1,), jnp.int32), pltpu.SemaphoreType.DMA((2, 3))],
    )
    return pl.pallas_call(
        _expert_kernel,
        grid_spec=grid_spec,
        out_shape=jax.ShapeDtypeStruct(buf.shape, BF16),
        compiler_params=pltpu.CompilerParams(dimension_semantics=("arbitrary",), vmem_limit_bytes=VMEM_LIMIT),
        name="experts",
    )(block_expert, next_expert, n_used, buf, buf, buf, w_gate, w_up, w_down)


def _final_kernel(dcur_ref, dnxt_ref, x_ref, info_ref, y_ref, p_ref, wpg_ref, bpg_ref, wpp_ref,
                  l2g_ref, l2b_ref, l3g_ref, l3b_ref, o_ref, rows_scr, sem):
    i = pl.program_id(0)
    last = pl.num_programs(0) - 1
    rows = x_ref.shape[0]
    slot = i % 2

    def row_copy(dref, s, r, kk):
        return _tile_copy(y_ref, dref[0, 0, TOP_K * r + kk], rows_scr.at[s, kk], r * TOKEN_ROWS, sem.at[s])

    def landed(s):
        pltpu.make_async_copy(rows_scr.at[s], rows_scr.at[s], sem.at[s]).wait()

    @pl.when(i == 0)
    def _():
        def start(c, carry):
            for u in range(MOVE_UNROLL):
                for kk in range(TOP_K):
                    row_copy(dcur_ref, 0, c * MOVE_UNROLL + u, kk).start(priority=kk)
            return carry

        lax.fori_loop(0, rows // MOVE_UNROLL, start, 0)

    landed(slot)
    info = info_ref[...]
    gate0 = info[:, I_G0:I_G0 + 1]
    gate1 = info[:, I_G1:I_G1 + 1]
    moe = (_from_token_tiles(rows_scr.at[slot, 0], rows).astype(F32) * gate0
           + _from_token_tiles(rows_scr.at[slot, 1], rows).astype(F32) * gate1)

    for r in range(rows):
        for kk in range(TOP_K):
            row_copy(dnxt_ref, 1 - slot, r, kk).start(priority=kk)

    pp = _dot(p_ref[...].astype(BF16), wpp_ref[...])
    x2 = _ln(ALPHA * x_ref[...] + moe, l2g_ref[...], l2b_ref[...])
    gate = jax.nn.sigmoid(_dot(x2.astype(BF16), wpg_ref[...]) + bpg_ref[...])
    o_ref[...] = _ln(ALPHA * x2 + gate * pp, l3g_ref[...], l3b_ref[...])

    @pl.when(i == last)
    def _():
        landed(1 - slot)


def _final(dest3, x1, info, y, p2, w):
    T, D = x1.shape
    tc = MOVE_ROWS
    pd = p2.shape[1]
    full = lambda a: pl.BlockSpec(a.shape, lambda i: (0,) * a.ndim)
    consts = [w["wpg"], w["bpg"], w["wpp"], w["l2g"], w["l2b"], w["l3g"], w["l3b"]]
    last = T // tc - 1
    return pl.pallas_call(
        _final_kernel,
        grid=(T // tc,),
        in_specs=[pl.BlockSpec((1, 1, TOP_K * tc), lambda i: (i, 0, 0), memory_space=pltpu.SMEM),
                  pl.BlockSpec((1, 1, TOP_K * tc), lambda i: (jnp.minimum(i + 1, last), 0, 0), memory_space=pltpu.SMEM),
                  pl.BlockSpec((tc, D), lambda i: (i, 0)),
                  pl.BlockSpec((tc, LANES), lambda i: (i, 0)),
                  pl.BlockSpec(memory_space=pl.ANY),
                  pl.BlockSpec((tc, pd), lambda i: (i, 0))] + [full(a) for a in consts],
        out_specs=pl.BlockSpec((tc, D), lambda i: (i, 0)),
        out_shape=jax.ShapeDtypeStruct((T, D), F32),
        scratch_shapes=[pltpu.VMEM((2, TOP_K, tc * TOKEN_ROWS, LANES), BF16), pltpu.SemaphoreType.DMA((2,))],
        compiler_params=pltpu.CompilerParams(dimension_semantics=("arbitrary",), vmem_limit_bytes=VMEM_LIMIT),
        name="final",
    )(dest3, dest3, x1, info, y, p2, *consts)


def _pad_heads(a, width):
    lead = a.shape[:-1]
    a = a.reshape(lead + (MLA_HEADS, width))
    a = jnp.pad(a, [(0, 0)] * len(lead) + [(0, 0), (0, LANES - width)])
    return a.reshape(lead + (HP,))


def _layer_weights(w_in, q_norm_g, w_q_up, kv_norm_g, w_kv_up, gm_ln_g, gm_ln_b, gm_w_s, gm_b_s,
                   mla_out_g, gm_out_g, w_o, ln1_g, ln1_b):
    D = w_in.shape[0]
    half = QK_ROPE // 2
    c1, c2, c3 = Q_RANK, Q_RANK + KV_RANK, Q_RANK + KV_RANK + QK_ROPE
    zeros = lambda *s: jnp.zeros(s, F32)
    kr = jnp.concatenate([zeros(D, QK_NOPE), w_in[:, c2:c3], zeros(D, LANES - QK_NOPE - QK_ROPE)], axis=1)
    win = jnp.concatenate([w_in[:, :c2], kr, w_in[:, c3:]], axis=1).astype(BF16)
    wq = _pad_heads(w_q_up, QK_NOPE + QK_ROPE).astype(BF16)

    wkv3 = w_kv_up.reshape(KV_RANK, MLA_HEADS, QK_NOPE + V_HEAD)
    wk = _pad_heads(wkv3[..., :QK_NOPE].reshape(KV_RANK, -1), QK_NOPE).astype(BF16)
    wv = wkv3[..., QK_NOPE:].reshape(KV_RANK, -1).astype(BF16)

    inv = (ROPE_THETA ** (-jnp.arange(0, QK_ROPE, 2, dtype=F32) / QK_ROPE))[:, None]
    eye = jnp.eye(half, dtype=F32)
    first = jnp.pad(eye, ((0, 0), (QK_NOPE, LANES - QK_NOPE - half)))
    second = jnp.pad(eye, ((0, 0), (QK_NOPE + half, LANES - QK_NOPE - QK_ROPE)))
    zero = jnp.zeros_like(first)
    cos_rows = jnp.concatenate([first + second, zero, zero], axis=1)
    sin_rows = jnp.concatenate([zero, -first, second], axis=1)
    rope = jnp.concatenate([cos_rows, cos_rows, sin_rows, sin_rows], axis=0).astype(BF16)
    lane = jnp.arange(LANES)
    one = jnp.where((lane >= QK_NOPE) & (lane < QK_NOPE + QK_ROPE), 0.0, 1.0)[None, :]

    grp = jnp.arange(GM_OUT) // GM_CH
    gavg = jnp.where(grp[:, None] == grp[None, :], 1.0 / GM_CH, 0.0).astype(BF16)
    bias = jnp.repeat(gm_b_s.T, GM_CH, axis=1)

    woa = w_o[:MLA_OUT].astype(BF16)
    wog = w_o[MLA_OUT:].astype(BF16)
    return dict(win=win, qg=q_norm_g[None, :], wq=wq, kvg=kv_norm_g[None, :], wk=wk, wv=wv, inv=inv, rope=rope, one=one,
                lng=gm_ln_g[None, :], lnb=gm_ln_b[None, :], gavg=gavg, ws=gm_w_s, bias=bias, gog=gm_out_g[None, :],
                woa=woa, wog=wog, mog=mla_out_g[:, None], l1g=ln1_g[None, :], l1b=ln1_b[None, :])


def _moe(x1, w_rg, b_rg, w_re, b_re, w_gate, w_up, w_down):
    T, D = x1.shape
    pad = jnp.zeros((D, LANES - N_GROUPS - N_EXPERTS), F32)
    wr = jnp.concatenate([w_rg, w_re, pad], axis=1)
    br = jnp.concatenate([b_rg, b_re, pad[0]])[None, :]
    info, cnt = _route(x1, wr, br)

    bm = EXPERT_ROWS
    n_blocks = (T * TOP_K) // bm + N_EXPERTS
    counts = cnt[0, R_OFF:R_OFF + N_EXPERTS].astype(jnp.int32)
    padded = (counts + bm - 1) // bm * bm
    pad_ends = jnp.cumsum(padded)
    pad_starts = pad_ends - padded
    e_idx = info[:, I_E0:I_E1 + 1].astype(jnp.int32)
    rank = info[:, I_R0:I_R1 + 1].astype(jnp.int32)
    seg_start = jnp.sum(jnp.where(e_idx[..., None] == jnp.arange(N_EXPERTS), pad_starts, 0), axis=-1)
    dest = ((seg_start + rank) * TOKEN_ROWS).reshape(T // MOVE_ROWS, 1, TOP_K * MOVE_ROWS)
    block_start = jnp.arange(n_blocks, dtype=jnp.int32) * bm
    block_expert = jnp.minimum(jnp.sum(pad_ends[None, :] <= block_start[:, None], axis=1),
                               N_EXPERTS - 1).astype(jnp.int32)

    blk = jnp.arange(n_blocks)
    later = (blk[None, :] > blk[:, None]) & (block_expert[None, :] != block_expert[:, None])
    next_expert = jnp.min(jnp.where(later, block_expert[None, :], N_EXPERTS), axis=1)
    next_expert = jnp.where(next_expert == N_EXPERTS, -1, next_expert).astype(jnp.int32)
    n_used = (pad_ends[-1:] // bm).astype(jnp.int32)

    seg = jnp.stack([pad_ends, padded, jnp.broadcast_to(n_used, (N_EXPERTS,))]).astype(jnp.int32)
    buf = _dispatch(seg, dest, x1, n_blocks * bm)
    y = _experts(block_expert, next_expert, n_used, buf, w_gate, w_up, w_down)
    return info, dest, y


def kernel(x, p, positions, w_in, q_norm_g, w_q_up, kv_norm_g, w_kv_up, gm_ln_g, gm_ln_b, gm_w_s, gm_b_s, mla_out_g, gm_out_g, w_o, ln1_g, ln1_b, w_rg, b_rg, w_re, b_re, w_gate, w_up, w_down, ln2_g, ln2_b, w_pg, b_pg, w_pp, ln3_g, ln3_b):
    B, S, D = x.shape
    T = B * S
    assert S % ATTN_ROWS == 0 and PREP_ROWS == ATTN_ROWS and PREP_ROWS % CHUNK == 0
    assert T % ROUTE_ROWS == 0 and T % MOVE_ROWS == 0 and (T * TOP_K) % EXPERT_ROWS == 0
    assert D == TOKEN_ROWS * LANES and MOVE_ROWS % MOVE_UNROLL == 0
    pos4 = positions.reshape(B, S // PREP_ROWS, 1, PREP_ROWS)
    for i in range(DEPTH):
        w = _layer_weights(w_in[i], q_norm_g[i], w_q_up[i], kv_norm_g[i], w_kv_up[i], gm_ln_g[i], gm_ln_b[i],
                           gm_w_s[i], gm_b_s[i], mla_out_g[i], gm_out_g[i], w_o[i], ln1_g[i], ln1_b[i])
        q, k, vt, g = _prep(x, pos4, w)
        x1 = _attn(q, k, vt, g, x, w).reshape(T, D)
        info, dest, y = _moe(x1, w_rg[i], b_rg[i], w_re[i], b_re[i], w_gate[i], w_up[i], w_down[i])
        wf = dict(wpg=w_pg[i].astype(BF16), bpg=b_pg[i][None, :], wpp=w_pp[i].astype(BF16),
                  l2g=ln2_g[i][None, :], l2b=ln2_b[i][None, :], l3g=ln3_g[i][None, :], l3b=ln3_b[i][None, :])
        x = _final(dest, x1, info, y, p[i].reshape(T, -1), wf).reshape(B, S, D)
    return x
```

```python
import functools

import jax
import jax.numpy as jnp
from jax import lax
from jax.experimental import pallas as pl
from jax.experimental.pallas import tpu as pltpu

F32 = jnp.float32
BF16 = jnp.bfloat16

MLA_HEADS = 8
QK_NOPE = 64
QK_ROPE = 32
V_HEAD = 64
Q_RANK = 256
KV_RANK = 128
ROPE_THETA = 10000.0
MLA_OUT = MLA_HEADS * V_HEAD
GM_GROUPS = 8
GM_CH = 64
GM_OUT = GM_GROUPS * GM_CH
CHUNK = 128
N_GROUPS = 4
EXP_PER_GROUP = 8
N_EXPERTS = N_GROUPS * EXP_PER_GROUP
TOP_K = 2
EPS = 1e-6
DEPTH = 1
ALPHA = (2.0 * DEPTH) ** 0.25
SM_SCALE = (QK_NOPE + QK_ROPE) ** -0.5
LOG2E = 1.4426950408889634

LANES = 128
SUBLANES = 8
TOKEN_ROWS = 8
ONES_ROWS = 16
VMEM_LIMIT = 56 * 1024 * 1024

PREP_ROWS = 256
ATTN_ROWS = 256
ROUTE_ROWS = 512
MOVE_ROWS = 256
MOVE_UNROLL = 8
EXPERT_ROWS = 256

C_Q = 0
C_KV = C_Q + Q_RANK
C_KR = C_KV + KV_RANK
C_U = C_KR + LANES
C_V = C_U + GM_OUT
C_END = C_V + GM_OUT
HP = MLA_HEADS * LANES

I_E0, I_E1, I_R0, I_R1, I_G0, I_G1 = range(6)
R_OFF = N_GROUPS


def _rms(v, g):
    return v * lax.rsqrt(jnp.mean(v * v, axis=-1, keepdims=True) + EPS) * g


def _ln(v, g, b):
    mu = jnp.mean(v, axis=-1, keepdims=True)
    d = v - mu
    var = jnp.mean(d * d, axis=-1, keepdims=True)
    return d * lax.rsqrt(var + EPS) * g + b


def _dot(a, b):
    return jnp.dot(a, b, preferred_element_type=F32)


def _prep_kernel(x_ref, pos_ref, win_ref, qg_ref, wq_ref, kvg_ref, wk_ref, wv_ref, inv_ref, rope_ref, one_ref,
                 lng_ref, lnb_ref, gavg_ref, ws_ref, bias_ref, gog_ref,
                 q_ref, k_ref, vt_ref, g_ref):
    rows = x_ref.shape[1]
    h = _dot(x_ref[0].astype(BF16), win_ref[...])

    ang = inv_ref[...] * pos_ref[0, 0].astype(F32)
    parts = []
    for t in (jnp.cos(ang), jnp.sin(ang)):
        hi = t.astype(BF16).astype(F32)
        parts += [hi, t - hi]
    tabs = _dot(jnp.concatenate(parts, axis=0).T.astype(BF16), rope_ref[...])
    cos_t = tabs[:, :LANES] + one_ref[...]
    sin_a = tabs[:, LANES:2 * LANES]
    sin_b = tabs[:, 2 * LANES:]
    half = QK_ROPE // 2

    def rotate(v):
        return v * cos_t + pltpu.roll(v, LANES - half, 1) * sin_a + pltpu.roll(v, half, 1) * sin_b

    cq = _rms(h[:, C_Q:C_Q + Q_RANK], qg_ref[...]).astype(BF16)
    q2 = _dot(cq, wq_ref[...])
    for hd in range(MLA_HEADS):
        lo = hd * LANES
        q_ref[0, :, lo:lo + LANES] = (rotate(q2[:, lo:lo + LANES]) * (SM_SCALE * LOG2E)).astype(BF16)

    ckv = _rms(h[:, C_KV:C_KV + KV_RANK], kvg_ref[...]).astype(BF16)
    kp = _dot(ckv, wk_ref[...])
    kr = rotate(h[:, C_KR:C_KR + LANES])
    for hd in range(MLA_HEADS):
        lo = hd * LANES
        k_ref[0, :, lo:lo + LANES] = (kp[:, lo:lo + LANES] + kr).astype(BF16)
    vt_ref[0, 0] = _dot(ckv, wv_ref[...]).T.astype(BF16)

    u = jax.nn.gelu(h[:, C_U:C_U + GM_OUT])
    vv = jax.nn.gelu(h[:, C_V:C_V + GM_OUT])
    mu = _dot(vv.astype(BF16), gavg_ref[...])
    d = vv - mu
    var = _dot((d * d).astype(BF16), gavg_ref[...])
    vn = (d * lax.rsqrt(var + EPS) * lng_ref[...] + lnb_ref[...]).astype(BF16)

    tri = lax.broadcasted_iota(jnp.int32, (CHUNK, CHUNK), 0) >= lax.broadcasted_iota(jnp.int32, (CHUNK, CHUNK), 1)
    wm = [jnp.where(tri, ws_ref[g], 0.0).astype(BF16) for g in range(GM_GROUPS)]
    low_half = lax.broadcasted_iota(jnp.int32, (CHUNK, LANES), 1) < GM_CH
    for c in range(rows // CHUNK):
        r0 = c * CHUNK
        parts = []
        for pr in range(GM_GROUPS // 2):
            tile = vn[r0:r0 + CHUNK, pr * LANES:(pr + 1) * LANES]
            parts.append(jnp.where(low_half, _dot(wm[2 * pr], tile), _dot(wm[2 * pr + 1], tile)))
        sg = jnp.concatenate(parts, axis=1) + bias_ref[...]
        gm = u[r0:r0 + CHUNK] * sg
        g_ref[0, r0:r0 + CHUNK, :] = _rms(gm, gog_ref[...]).astype(BF16)


def _prep(x, pos4, w):
    B, S, D = x.shape
    ts = PREP_ROWS
    full = lambda a: pl.BlockSpec(a.shape, lambda b, i: (0,) * a.ndim)
    consts = [w["win"], w["qg"], w["wq"], w["kvg"], w["wk"], w["wv"], w["inv"], w["rope"], w["one"],
              w["lng"], w["lnb"], w["gavg"], w["ws"], w["bias"], w["gog"]]
    return pl.pallas_call(
        _prep_kernel,
        grid=(B, S // ts),
        in_specs=[pl.BlockSpec((1, ts, D), lambda b, i: (b, i, 0)),
                  pl.BlockSpec((1, 1, 1, ts), lambda b, i: (b, i, 0, 0))] + [full(a) for a in consts],
        out_specs=[pl.BlockSpec((1, ts, HP), lambda b, i: (b, i, 0)),
                   pl.BlockSpec((1, ts, HP), lambda b, i: (b, i, 0)),
                   pl.BlockSpec((1, 1, MLA_OUT, ts), lambda b, i: (b, i, 0, 0)),
                   pl.BlockSpec((1, ts, GM_OUT), lambda b, i: (b, i, 0))],
        out_shape=[jax.ShapeDtypeStruct((B, S, HP), BF16)] * 2
        + [jax.ShapeDtypeStruct((B, S // ts, MLA_OUT, ts), BF16), jax.ShapeDtypeStruct((B, S, GM_OUT), BF16)],
        compiler_params=pltpu.CompilerParams(dimension_semantics=("parallel", "parallel"),
                                             vmem_limit_bytes=VMEM_LIMIT),
        name="prep",
    )(x, pos4, *consts)


def _attn_kernel(q_ref, k_ref, vt_ref, g_ref, x_ref, woa_ref, wog_ref, mog_ref, l1g_ref, l1b_ref,
                 o_ref, m_scr, acc_scr, sa_scr, sb_scr):
    i = pl.program_id(1)
    tq = q_ref.shape[1]
    tk = tq
    key = lax.broadcasted_iota(jnp.int32, (tk, tq), 0)
    qry = lax.broadcasted_iota(jnp.int32, (tk, tq), 1)
    diag_mask = key <= qry

    m_scr[...] = jnp.full(m_scr.shape, -1e30, F32)
    acc_scr[...] = jnp.zeros(acc_scr.shape, F32)
    ones = jnp.ones((ONES_ROWS, tk), BF16)

    def scores(j, s_scr):
        k0 = pl.multiple_of(j * tk, tk)
        for hd in range(MLA_HEADS):
            lo = hd * LANES
            qh = q_ref[0, :, lo:lo + LANES]
            kj = k_ref[0, pl.ds(k0, tk), lo:lo + LANES]
            s_scr[hd] = lax.dot_general(kj, qh, (((1,), (1,)), ((), ())), preferred_element_type=F32)

    def update(j, s_scr, masked):
        for hd in range(MLA_HEADS):
            s = s_scr[hd]
            vt = vt_ref[0, j, hd * V_HEAD:(hd + 1) * V_HEAD, :]
            if masked:
                s = jnp.where(diag_mask, s, -1e30)
            m_prev = m_scr[hd]
            m_new = jnp.maximum(m_prev, jnp.max(s, axis=0, keepdims=True))
            p = jnp.exp2(s - m_new).astype(BF16)
            scale = jnp.exp2(m_prev - m_new)
            acc_scr[hd] = scale * acc_scr[hd] + _dot(jnp.concatenate([vt, ones], axis=0), p)
            m_scr[hd] = m_new

    def pair(jj, c):
        j = 2 * jj
        scores(j + 1, sb_scr)
        update(j, sa_scr, False)
        scores(j + 2, sa_scr)
        update(j + 1, sb_scr, False)
        return c

    scores(0, sa_scr)
    lax.fori_loop(0, lax.shift_right_logical(i, 1), pair, 0)

    @pl.when((i & 1) == 0)
    def _():
        update(i, sa_scr, True)

    @pl.when((i & 1) == 1)
    def _():
        scores(i, sb_scr)
        update(i - 1, sa_scr, False)
        update(i, sb_scr, True)

    at = jnp.concatenate([acc_scr[hd, :V_HEAD] / acc_scr[hd, V_HEAD:V_HEAD + 1] for hd in range(MLA_HEADS)],
                         axis=0)
    at = at * lax.rsqrt(jnp.mean(at * at, axis=0, keepdims=True) + EPS) * mog_ref[...]
    mix = _dot(at.T.astype(BF16), woa_ref[...]) + _dot(g_ref[0], wog_ref[...])
    o_ref[0] = _ln(ALPHA * x_ref[0] + mix, l1g_ref[...], l1b_ref[...])


def _attn(q, k, vt, g, x, w):
    B, S, D = x.shape
    tq = ATTN_ROWS
    full = lambda a: pl.BlockSpec(a.shape, lambda b, i: (0,) * a.ndim)
    consts = [w["woa"], w["wog"], w["mog"], w["l1g"], w["l1b"]]
    return pl.pallas_call(
        _attn_kernel,
        grid=(B, S // tq),
        in_specs=[pl.BlockSpec((1, tq, HP), lambda b, i: (b, i, 0)),
                  pl.BlockSpec((1, S, HP), lambda b, i: (b, 0, 0)),
                  pl.BlockSpec((1,) + vt.shape[1:], lambda b, i: (b, 0, 0, 0)),
                  pl.BlockSpec((1, tq, GM_OUT), lambda b, i: (b, i, 0)),
                  pl.BlockSpec((1, tq, D), lambda b, i: (b, i, 0))] + [full(a) for a in consts],
        out_specs=pl.BlockSpec((1, tq, D), lambda b, i: (b, i, 0)),
        out_shape=jax.ShapeDtypeStruct((B, S, D), F32),
        scratch_shapes=[pltpu.VMEM((MLA_HEADS, 1, tq), F32),
                        pltpu.VMEM((MLA_HEADS, V_HEAD + ONES_ROWS, tq), F32),
                        pltpu.VMEM((MLA_HEADS, tq, tq), F32), pltpu.VMEM((MLA_HEADS, tq, tq), F32)],
        compiler_params=pltpu.CompilerParams(dimension_semantics=("parallel", "parallel"),
                                             vmem_limit_bytes=VMEM_LIMIT),
        name="attn",
    )(q, k, vt, g, x, *consts)


def _route_kernel(x_ref, wr_ref, br_ref, info_ref, cnt_ref, carry_scr, tri_scr):
    step = pl.program_id(0)
    tt = x_ref.shape[0]

    @pl.when(step == 0)
    def _():
        carry_scr[...] = jnp.zeros_like(carry_scr)
        r = lax.broadcasted_iota(jnp.int32, (tt, tt), 0)
        c = lax.broadcasted_iota(jnp.int32, (tt, tt), 1)
        tri_scr[...] = jnp.where(c < r, 1.0, 0.0).astype(BF16)

    x = x_ref[...]
    xh = x.astype(BF16)
    xl = (x - xh.astype(F32)).astype(BF16)
    wr = wr_ref[...]
    wh = wr.astype(BF16)
    wl = (wr - wh.astype(F32)).astype(BF16)
    logits = _dot(xh, wh) + _dot(xl, wh) + _dot(xh, wl) + br_ref[...]

    lane = lax.broadcasted_iota(jnp.int32, (tt, LANES), 1)
    neg = jnp.float32(-jnp.inf)

    is_g = lane < N_GROUPS
    lg = jnp.where(is_g, logits, neg)
    gmax = jnp.max(lg, axis=-1, keepdims=True)
    g_idx = jnp.min(jnp.where(lg == gmax, lane, LANES), axis=-1, keepdims=True)
    g_den = jnp.sum(jnp.where(is_g, jnp.exp(lg - gmax), 0.0), axis=-1, keepdims=True)
    g_p = 1.0 / g_den

    in_grp = (lane >= R_OFF) & (lane < R_OFF + N_EXPERTS) & (((lane - R_OFF) >> 3) == g_idx)
    le = jnp.where(in_grp, logits, neg)
    m1 = jnp.max(le, axis=-1, keepdims=True)
    i1 = jnp.min(jnp.where(le == m1, lane, LANES), axis=-1, keepdims=True)
    le2 = jnp.where(lane == i1, neg, le)
    m2 = jnp.max(le2, axis=-1, keepdims=True)
    i2 = jnp.min(jnp.where(le2 == m2, lane, LANES), axis=-1, keepdims=True)
    e2 = jnp.exp(m2 - m1)
    gate0 = g_p / (1.0 + e2)
    gate1 = g_p * e2 / (1.0 + e2)

    hit1 = lane == i1
    hit2 = lane == i2
    onehot = jnp.where(hit1 | hit2, 1.0, 0.0)
    before = _dot(tri_scr[...], onehot.astype(BF16)) + carry_scr[...]
    r0 = jnp.sum(jnp.where(hit1, before, 0.0), axis=-1, keepdims=True)
    r1 = jnp.sum(jnp.where(hit2, before, 0.0), axis=-1, keepdims=True)
    carry_scr[...] = carry_scr[...] + jnp.sum(onehot, axis=0, keepdims=True)
    cnt_ref[...] = carry_scr[...]

    info = jnp.where(lane == I_E0, (i1 - R_OFF).astype(F32), 0.0)
    info = jnp.where(lane == I_E1, (i2 - R_OFF).astype(F32), info)
    info = jnp.where(lane == I_R0, r0, info)
    info = jnp.where(lane == I_R1, r1, info)
    info = jnp.where(lane == I_G0, gate0, info)
    info = jnp.where(lane == I_G1, gate1, info)
    info_ref[...] = info


def _route(x1, wr, br):
    T, D = x1.shape
    tt = ROUTE_ROWS
    return pl.pallas_call(
        _route_kernel,
        grid=(T // tt,),
        in_specs=[pl.BlockSpec((tt, D), lambda i: (i, 0)),
                  pl.BlockSpec(wr.shape, lambda i: (0, 0)),
                  pl.BlockSpec(br.shape, lambda i: (0, 0))],
        out_specs=[pl.BlockSpec((tt, LANES), lambda i: (i, 0)),
                   pl.BlockSpec((1, LANES), lambda i: (0, 0))],
        out_shape=[jax.ShapeDtypeStruct((T, LANES), F32), jax.ShapeDtypeStruct((1, LANES), F32)],
        scratch_shapes=[pltpu.VMEM((1, LANES), F32), pltpu.VMEM((tt, tt), BF16)],
        compiler_params=pltpu.CompilerParams(dimension_semantics=("arbitrary",), vmem_limit_bytes=VMEM_LIMIT),
        name="route",
    )(x1, wr, br)


def _to_token_tiles(dst_ref, val):
    dst_ref[...] = val.astype(BF16).reshape(dst_ref.shape)


def _from_token_tiles(src_ref, rows):
    return src_ref[...].reshape(rows, TOKEN_ROWS * LANES)


def _tile_copy(src_ref, src_row, dst_ref, dst_row, sem):
    return pltpu.make_async_copy(src_ref.at[pl.ds(pl.multiple_of(src_row, TOKEN_ROWS), TOKEN_ROWS)],
                                 dst_ref.at[pl.ds(pl.multiple_of(dst_row, TOKEN_ROWS), TOKEN_ROWS)], sem)


def _dispatch_kernel(seg_ref, dest_ref, x_ref, buf_ref, stage_scr, zero_scr, sem, zero_sem, *, n_steps):
    i = pl.program_id(0)
    rows = x_ref.shape[0]
    slot = i % 2

    @pl.when(i == 0)
    def _():
        zero_scr[...] = jnp.zeros(zero_scr.shape, BF16)

        block = EXPERT_ROWS * TOKEN_ROWS
        n_blocks = buf_ref.shape[0] // block

        def clear_rows(first):
            return pltpu.make_async_copy(zero_scr, buf_ref.at[pl.ds(pl.multiple_of(first, SUBLANES), block)], zero_sem)

        def clear(e):
            return clear_rows((seg_ref[0, e] - EXPERT_ROWS) * TOKEN_ROWS)

        def start_tail(b, c):
            clear_rows(b * block).start()
            return c

        def wait_tail(b, c):
            clear_rows(b * block).wait()
            return c

        for e in range(N_EXPERTS):
            pl.when(seg_ref[1, e] > 0)(lambda e=e: clear(e).start())
        lax.fori_loop(seg_ref[2, 0], n_blocks, start_tail, 0)
        for e in range(N_EXPERTS):
            pl.when(seg_ref[1, e] > 0)(lambda e=e: clear(e).wait())
        lax.fori_loop(seg_ref[2, 0], n_blocks, wait_tail, 0)

    def drain(s):
        for _ in range(TOP_K):
            pltpu.make_async_copy(stage_scr.at[s], stage_scr.at[s], sem.at[s]).wait()

    @pl.when(i >= 2)
    def _():
        drain(slot)

    _to_token_tiles(stage_scr.at[slot], x_ref[...])

    def start(c, carry):
        for u in range(MOVE_UNROLL):
            r = c * MOVE_UNROLL + u
            for kk in range(TOP_K):
                _tile_copy(stage_scr.at[slot], r * TOKEN_ROWS, buf_ref, dest_ref[0, 0, TOP_K * r + kk],
                           sem.at[slot]).start(priority=kk)
        return carry

    lax.fori_loop(0, rows // MOVE_UNROLL, start, 0)

    @pl.when(i == n_steps - 1)
    def _():
        drain(slot)
        if n_steps >= 2:
            drain(1 - slot)


def _dispatch(seg, dest3, x1, n_rows):
    T, D = x1.shape
    td = MOVE_ROWS
    n_steps = T // td
    grid_spec = pltpu.PrefetchScalarGridSpec(
        num_scalar_prefetch=1,
        grid=(n_steps,),
        in_specs=[pl.BlockSpec((1, 1, TOP_K * td), lambda i, seg: (i, 0, 0), memory_space=pltpu.SMEM),
                  pl.BlockSpec((td, D), lambda i, seg: (i, 0))],
        out_specs=pl.BlockSpec(memory_space=pl.ANY),
        scratch_shapes=[pltpu.VMEM((2, td * TOKEN_ROWS, LANES), BF16),
                        pltpu.VMEM((EXPERT_ROWS * TOKEN_ROWS, LANES), BF16),
                        pltpu.SemaphoreType.DMA((2,)), pltpu.SemaphoreType.DMA(())],
    )
    return pl.pallas_call(
        functools.partial(_dispatch_kernel, n_steps=n_steps),
        grid_spec=grid_spec,
        out_shape=jax.ShapeDtypeStruct((n_rows * TOKEN_ROWS, LANES), BF16),
        compiler_params=pltpu.CompilerParams(dimension_semantics=("arbitrary",), vmem_limit_bytes=VMEM_LIMIT),
        name="dispatch",
    )(seg, dest3, x1)


def _expert_kernel(be_ref, ne_ref, nu_ref, buf0_ref, bufa_ref, bufb_ref, wg_hbm, wu_hbm, wd_hbm, y_ref,
                   sg_scr, su_scr, sd_scr, wg_scr, wu_scr, wd_scr, xa_scr, xb_scr, cur_ref, sem):
    step = pl.program_id(0)
    bm = EXPERT_ROWS
    half = bm * TOKEN_ROWS

    def fetch(expert, s):
        return (pltpu.make_async_copy(wg_hbm.at[expert], sg_scr.at[s], sem.at[s, 0]),
                pltpu.make_async_copy(wu_hbm.at[expert], su_scr.at[s], sem.at[s, 1]),
                pltpu.make_async_copy(wd_hbm.at[expert], sd_scr.at[s], sem.at[s, 2]))

    @pl.when(step == 0)
    def _():
        cur_ref[0] = 0
        for c in fetch(be_ref[0], 0):
            c.start()
        xa_scr[...] = _from_token_tiles(buf0_ref, bm)

    def load_weights(blk):
        e = be_ref[blk]

        @pl.when((blk == 0) | (be_ref[jnp.maximum(blk - 1, 0)] != e))
        def _():
            s = cur_ref[0]
            for c in fetch(e, s):
                c.wait()
            wg_scr[...] = sg_scr[s].astype(BF16)
            wu_scr[...] = su_scr[s].astype(BF16)
            wd_scr[...] = sd_scr[s].astype(BF16)
            nxt = ne_ref[blk]

            @pl.when(nxt >= 0)
            def _():
                for c in fetch(nxt, 1 - s):
                    c.start()

            cur_ref[0] = 1 - s

    def run(blk, x_scr, nxt_ref, nxt_scr, out_rows):
        load_weights(blk)

        @pl.when(blk < nu_ref[0])
        def _():
            nxt_scr[...] = _from_token_tiles(nxt_ref, bm)
            xb = x_scr[...]
            hidden = jax.nn.silu(_dot(xb, wg_scr[...])) * _dot(xb, wu_scr[...])
            _to_token_tiles(y_ref.at[out_rows], _dot(hidden.astype(BF16), wd_scr[...]))

        @pl.when(blk >= nu_ref[0])
        def _():
            y_ref[out_rows, :] = jnp.zeros((half, LANES), BF16)

    run(2 * step, xa_scr, bufa_ref, xb_scr, pl.ds(0, half))
    run(2 * step + 1, xb_scr, bufb_ref, xa_scr, pl.ds(half, half))


def _experts(block_expert, next_expert, n_used, buf, w_gate, w_up, w_down):
    bm = EXPERT_ROWS
    D, ff = w_gate.shape[1:]
    n_blocks = buf.shape[0] // (bm * TOKEN_ROWS)
    assert n_blocks % 2 == 0
    last = n_blocks - 1
    grid_spec = pltpu.PrefetchScalarGridSpec(
        num_scalar_prefetch=3,
        grid=(n_blocks // 2,),
        in_specs=[pl.BlockSpec((bm * TOKEN_ROWS, LANES), lambda s, *_: (0, 0)),
                  pl.BlockSpec((bm * TOKEN_ROWS, LANES), lambda s, *_: (2 * s + 1, 0)),
                  pl.BlockSpec((bm * TOKEN_ROWS, LANES), lambda s, *_: (jnp.minimum(2 * s + 2, last), 0)),
                  pl.BlockSpec(memory_space=pl.ANY),
                  pl.BlockSpec(memory_space=pl.ANY),
                  pl.BlockSpec(memory_space=pl.ANY)],
        out_specs=pl.BlockSpec((2 * bm * TOKEN_ROWS, LANES), lambda s, *_: (s, 0)),
        scratch_shapes=[pltpu.VMEM((2, D, ff), F32), pltpu.VMEM((2, D, ff), F32), pltpu.VMEM((2, ff, D), F32),
                        pltpu.VMEM((D, ff), BF16), pltpu.VMEM((D, ff), BF16), pltpu.VMEM((ff, D), BF16),
                        pltpu.VMEM((bm, D), BF16), pltpu.VMEM((bm, D), BF16),
                        pltpu.SMEM((1,), jnp.int32), pltpu.SemaphoreType.DMA((2, 3))],
    )
    return pl.pallas_call(
        _expert_kernel,
        grid_spec=grid_spec,
        out_shape=jax.ShapeDtypeStruct(buf.shape, BF16),
        compiler_params=pltpu.CompilerParams(dimension_semantics=("arbitrary",), vmem_limit_bytes=VMEM_LIMIT),
        name="experts",
    )(block_expert, next_expert, n_used, buf, buf, buf, w_gate, w_up, w_down)


def _final_kernel(dcur_ref, dnxt_ref, x_ref, info_ref, y_ref, p_ref, wpg_ref, bpg_ref, wpp_ref,
                  l2g_ref, l2b_ref, l3g_ref, l3b_ref, o_ref, rows_scr, sem):
    i = pl.program_id(0)
    last = pl.num_programs(0) - 1
    rows = x_ref.shape[0]
    slot = i % 2

    def row_copy(dref, s, r, kk):
        return _tile_copy(y_ref, dref[0, 0, TOP_K * r + kk], rows_scr.at[s, kk], r * TOKEN_ROWS, sem.at[s])

    def landed(s):
        pltpu.make_async_copy(rows_scr.at[s], rows_scr.at[s], sem.at[s]).wait()

    @pl.when(i == 0)
    def _():
        def start(c, carry):
            for u in range(MOVE_UNROLL):
                for kk in range(TOP_K):
                    row_copy(dcur_ref, 0, c * MOVE_UNROLL + u, kk).start(priority=kk)
            return carry

        lax.fori_loop(0, rows // MOVE_UNROLL, start, 0)

    landed(slot)
    info = info_ref[...]
    gate0 = info[:, I_G0:I_G0 + 1]
    gate1 = info[:, I_G1:I_G1 + 1]
    moe = (_from_token_tiles(rows_scr.at[slot, 0], rows).astype(F32) * gate0
           + _from_token_tiles(rows_scr.at[slot, 1], rows).astype(F32) * gate1)

    for r in range(rows):
        for kk in range(TOP_K):
            row_copy(dnxt_ref, 1 - slot, r, kk).start(priority=kk)

    pp = _dot(p_ref[...].astype(BF16), wpp_ref[...])
    x2 = _ln(ALPHA * x_ref[...] + moe, l2g_ref[...], l2b_ref[...])
    gate = jax.nn.sigmoid(_dot(x2.astype(BF16), wpg_ref[...]) + bpg_ref[...])
    o_ref[...] = _ln(ALPHA * x2 + gate * pp, l3g_ref[...], l3b_ref[...])

    @pl.when(i == last)
    def _():
        landed(1 - slot)


def _final(dest3, x1, info, y, p2, w):
    T, D = x1.shape
    tc = MOVE_ROWS
    pd = p2.shape[1]
    full = lambda a: pl.BlockSpec(a.shape, lambda i: (0,) * a.ndim)
    consts = [w["wpg"], w["bpg"], w["wpp"], w["l2g"], w["l2b"], w["l3g"], w["l3b"]]
    last = T // tc - 1
    return pl.pallas_call(
        _final_kernel,
        grid=(T // tc,),
        in_specs=[pl.BlockSpec((1, 1, TOP_K * tc), lambda i: (i, 0, 0), memory_space=pltpu.SMEM),
                  pl.BlockSpec((1, 1, TOP_K * tc), lambda i: (jnp.minimum(i + 1, last), 0, 0), memory_space=pltpu.SMEM),
                  pl.BlockSpec((tc, D), lambda i: (i, 0)),
                  pl.BlockSpec((tc, LANES), lambda i: (i, 0)),
                  pl.BlockSpec(memory_space=pl.ANY),
                  pl.BlockSpec((tc, pd), lambda i: (i, 0))] + [full(a) for a in consts],
        out_specs=pl.BlockSpec((tc, D), lambda i: (i, 0)),
        out_shape=jax.ShapeDtypeStruct((T, D), F32),
        scratch_shapes=[pltpu.VMEM((2, TOP_K, tc * TOKEN_ROWS, LANES), BF16), pltpu.SemaphoreType.DMA((2,))],
        compiler_params=pltpu.CompilerParams(dimension_semantics=("arbitrary",), vmem_limit_bytes=VMEM_LIMIT),
        name="final",
    )(dest3, dest3, x1, info, y, p2, *consts)


def _pad_heads(a, width):
    lead = a.shape[:-1]
    a = a.reshape(lead + (MLA_HEADS, width))
    a = jnp.pad(a, [(0, 0)] * len(lead) + [(0, 0), (0, LANES - width)])
    return a.reshape(lead + (HP,))


def _layer_weights(w_in, q_norm_g, w_q_up, kv_norm_g, w_kv_up, gm_ln_g, gm_ln_b, gm_w_s, gm_b_s,
                   mla_out_g, gm_out_g, w_o, ln1_g, ln1_b):
    D = w_in.shape[0]
    half = QK_ROPE // 2
    c1, c2, c3 = Q_RANK, Q_RANK + KV_RANK, Q_RANK + KV_RANK + QK_ROPE
    zeros = lambda *s: jnp.zeros(s, F32)
    kr = jnp.concatenate([zeros(D, QK_NOPE), w_in[:, c2:c3], zeros(D, LANES - QK_NOPE - QK_ROPE)], axis=1)
    win = jnp.concatenate([w_in[:, :c2], kr, w_in[:, c3:]], axis=1).astype(BF16)
    wq = _pad_heads(w_q_up, QK_NOPE + QK_ROPE).astype(BF16)

    wkv3 = w_kv_up.reshape(KV_RANK, MLA_HEADS, QK_NOPE + V_HEAD)
    wk = _pad_heads(wkv3[..., :QK_NOPE].reshape(KV_RANK, -1), QK_NOPE).astype(BF16)
    wv = wkv3[..., QK_NOPE:].reshape(KV_RANK, -1).astype(BF16)

    inv = (ROPE_THETA ** (-jnp.arange(0, QK_ROPE, 2, dtype=F32) / QK_ROPE))[:, None]
    eye = jnp.eye(half, dtype=F32)
    first = jnp.pad(eye, ((0, 0), (QK_NOPE, LANES - QK_NOPE - half)))
    second = jnp.pad(eye, ((0, 0), (QK_NOPE + half, LANES - QK_NOPE - QK_ROPE)))
    zero = jnp.zeros_like(first)
    cos_rows = jnp.concatenate([first + second, zero, zero], axis=1)
    sin_rows = jnp.concatenate([zero, -first, second], axis=1)
    rope = jnp.concatenate([cos_rows, cos_rows, sin_rows, sin_rows], axis=0).astype(BF16)
    lane = jnp.arange(LANES)
    one = jnp.where((lane >= QK_NOPE) & (lane < QK_NOPE + QK_ROPE), 0.0, 1.0)[None, :]

    grp = jnp.arange(GM_OUT) // GM_CH
    gavg = jnp.where(grp[:, None] == grp[None, :], 1.0 / GM_CH, 0.0).astype(BF16)
    bias = jnp.repeat(gm_b_s.T, GM_CH, axis=1)

    woa = w_o[:MLA_OUT].astype(BF16)
    wog = w_o[MLA_OUT:].astype(BF16)
    return dict(win=win, qg=q_norm_g[None, :], wq=wq, kvg=kv_norm_g[None, :], wk=wk, wv=wv, inv=inv, rope=rope, one=one,
                lng=gm_ln_g[None, :], lnb=gm_ln_b[None, :], gavg=gavg, ws=gm_w_s, bias=bias, gog=gm_out_g[None, :],
                woa=woa, wog=wog, mog=mla_out_g[:, None], l1g=ln1_g[None, :], l1b=ln1_b[None, :])


def _moe(x1, w_rg, b_rg, w_re, b_re, w_gate, w_up, w_down):
    T, D = x1.shape
    pad = jnp.zeros((D, LANES - N_GROUPS - N_EXPERTS), F32)
    wr = jnp.concatenate([w_rg, w_re, pad], axis=1)
    br = jnp.concatenate([b_rg, b_re, pad[0]])[None, :]
    info, cnt = _route(x1, wr, br)

    bm = EXPERT_ROWS
    n_blocks = (T * TOP_K) // bm + N_EXPERTS
    counts = cnt[0, R_OFF:R_OFF + N_EXPERTS].astype(jnp.int32)
    padded = (counts + bm - 1) // bm * bm
    pad_ends = jnp.cumsum(padded)
    pad_starts = pad_ends - padded
    e_idx = info[:, I_E0:I_E1 + 1].astype(jnp.int32)
    rank = info[:, I_R0:I_R1 + 1].astype(jnp.int32)
    seg_start = jnp.sum(jnp.where(e_idx[..., None] == jnp.arange(N_EXPERTS), pad_starts, 0), axis=-1)
    dest = ((seg_start + rank) * TOKEN_ROWS).reshape(T // MOVE_ROWS, 1, TOP_K * MOVE_ROWS)
    block_start = jnp.arange(n_blocks, dtype=jnp.int32) * bm
    block_expert = jnp.minimum(jnp.sum(pad_ends[None, :] <= block_start[:, None], axis=1),
                               N_EXPERTS - 1).astype(jnp.int32)

    blk = jnp.arange(n_blocks)
    later = (blk[None, :] > blk[:, None]) & (block_expert[None, :] != block_expert[:, None])
    next_expert = jnp.min(jnp.where(later, block_expert[None, :], N_EXPERTS), axis=1)
    next_expert = jnp.where(next_expert == N_EXPERTS, -1, next_expert).astype(jnp.int32)
    n_used = (pad_ends[-1:] // bm).astype(jnp.int32)

    seg = jnp.stack([pad_ends, padded, jnp.broadcast_to(n_used, (N_EXPERTS,))]).astype(jnp.int32)
    buf = _dispatch(seg, dest, x1, n_blocks * bm)
    y = _experts(block_expert, next_expert, n_used, buf, w_gate, w_up, w_down)
    return info, dest, y


def kernel(x, p, positions, w_in, q_norm_g, w_q_up, kv_norm_g, w_kv_up, gm_ln_g, gm_ln_b, gm_w_s, gm_b_s, mla_out_g, gm_out_g, w_o, ln1_g, ln1_b, w_rg, b_rg, w_re, b_re, w_gate, w_up, w_down, ln2_g, ln2_b, w_pg, b_pg, w_pp, ln3_g, ln3_b):
    B, S, D = x.shape
    T = B * S
    assert S % ATTN_ROWS == 0 and PREP_ROWS == ATTN_ROWS and PREP_ROWS % CHUNK == 0
    assert T % ROUTE_ROWS == 0 and T % MOVE_ROWS == 0 and (T * TOP_K) % EXPERT_ROWS == 0
    assert D == TOKEN_ROWS * LANES and MOVE_ROWS % MOVE_UNROLL == 0
    pos4 = positions.reshape(B, S // PREP_ROWS, 1, PREP_ROWS)
    for i in range(DEPTH):
        w = _layer_weights(w_in[i], q_norm_g[i], w_q_up[i], kv_norm_g[i], w_kv_up[i], gm_ln_g[i], gm_ln_b[i],
                           gm_w_s[i], gm_b_s[i], mla_out_g[i], gm_out_g[i], w_o[i], ln1_g[i], ln1_b[i])
        q, k, vt, g = _prep(x, pos4, w)
        x1 = _attn(q, k, vt, g, x, w).reshape(T, D)
        info, dest, y = _moe(x1, w_rg[i], b_rg[i], w_re[i], b_re[i], w_gate[i], w_up[i], w_down[i])
        wf = dict(wpg=w_pg[i].astype(BF16), bpg=b_pg[i][None, :], wpp=w_pp[i].astype(BF16),
                  l2g=ln2_g[i][None, :], l2b=ln2_b[i][None, :], l3g=ln3_g[i][None, :], l3b=ln3_b[i][None, :])
        x = _final(dest, x1, info, y, p[i].reshape(T, -1), wf).reshape(B, S, D)
    return x
```

```python
import functools

import jax
import jax.numpy as jnp
from jax import lax
from jax.experimental import pallas as pl
from jax.experimental.pallas import tpu as pltpu

F32 = jnp.float32
BF16 = jnp.bfloat16

MLA_HEADS = 8
QK_NOPE = 64
QK_ROPE = 32
V_HEAD = 64
Q_RANK = 256
KV_RANK = 128
ROPE_THETA = 10000.0
MLA_OUT = MLA_HEADS * V_HEAD
GM_GROUPS = 8
GM_CH = 64
GM_OUT = GM_GROUPS * GM_CH
CHUNK = 128
N_GROUPS = 4
EXP_PER_GROUP = 8
N_EXPERTS = N_GROUPS * EXP_PER_GROUP
TOP_K = 2
EPS = 1e-6
DEPTH = 1
ALPHA = (2.0 * DEPTH) ** 0.25
SM_SCALE = (QK_NOPE + QK_ROPE) ** -0.5
LOG2E = 1.4426950408889634

LANES = 128
SUBLANES = 8
TOKEN_ROWS = 8
ONES_ROWS = 16
VMEM_LIMIT = 56 * 1024 * 1024

PREP_ROWS = 256
ATTN_ROWS = 256
ROUTE_ROWS = 512
MOVE_ROWS = 256
MOVE_UNROLL = 8
EXPERT_ROWS = 256

C_Q = 0
C_KV = C_Q + Q_RANK
C_KR = C_KV + KV_RANK
C_U = C_KR + LANES
C_V = C_U + GM_OUT
C_END = C_V + GM_OUT
HP = MLA_HEADS * LANES

I_E0, I_E1, I_R0, I_R1, I_G0, I_G1 = range(6)
R_OFF = N_GROUPS


def _rms(v, g):
    return v * lax.rsqrt(jnp.mean(v * v, axis=-1, keepdims=True) + EPS) * g


def _ln(v, g, b):
    mu = jnp.mean(v, axis=-1, keepdims=True)
    d = v - mu
    var = jnp.mean(d * d, axis=-1, keepdims=True)
    return d * lax.rsqrt(var + EPS) * g + b


def _dot(a, b):
    return jnp.dot(a, b, preferred_element_type=F32)


def _prep_kernel(x_ref, pos_ref, win_ref, qg_ref, wq_ref, kvg_ref, wk_ref, wv_ref, inv_ref, rope_ref, one_ref,
                 lng_ref, lnb_ref, gavg_ref, ws_ref, bias_ref, gog_ref,
                 q_ref, k_ref, vt_ref, g_ref):
    rows = x_ref.shape[1]
    h = _dot(x_ref[0].astype(BF16), win_ref[...])

    ang = inv_ref[...] * pos_ref[0, 0].astype(F32)
    parts = []
    for t in (jnp.cos(ang), jnp.sin(ang)):
        hi = t.astype(BF16).astype(F32)
        parts += [hi, t - hi]
    tabs = _dot(jnp.concatenate(parts, axis=0).T.astype(BF16), rope_ref[...])
    cos_t = tabs[:, :LANES] + one_ref[...]
    sin_a = tabs[:, LANES:2 * LANES]
    sin_b = tabs[:, 2 * LANES:]
    half = QK_ROPE // 2

    def rotate(v):
        return v * cos_t + pltpu.roll(v, LANES - half, 1) * sin_a + pltpu.roll(v, half, 1) * sin_b

    cq = _rms(h[:, C_Q:C_Q + Q_RANK], qg_ref[...]).astype(BF16)
    q2 = _dot(cq, wq_ref[...])
    for hd in range(MLA_HEADS):
        lo = hd * LANES
        q_ref[0, :, lo:lo + LANES] = (rotate(q2[:, lo:lo + LANES]) * (SM_SCALE * LOG2E)).astype(BF16)

    ckv = _rms(h[:, C_KV:C_KV + KV_RANK], kvg_ref[...]).astype(BF16)
    kp = _dot(ckv, wk_ref[...])
    kr = rotate(h[:, C_KR:C_KR + LANES])
    for hd in range(MLA_HEADS):
        lo = hd * LANES
        k_ref[0, :, lo:lo + LANES] = (kp[:, lo:lo + LANES] + kr).astype(BF16)
    vt_ref[0, 0] = _dot(ckv, wv_ref[...]).T.astype(BF16)

    u = jax.nn.gelu(h[:, C_U:C_U + GM_OUT])
    vv = jax.nn.gelu(h[:, C_V:C_V + GM_OUT])
    mu = _dot(vv.astype(BF16), gavg_ref[...])
    d = vv - mu
    var = _dot((d * d).astype(BF16), gavg_ref[...])
    vn = (d * lax.rsqrt(var + EPS) * lng_ref[...] + lnb_ref[...]).astype(BF16)

    tri = lax.broadcasted_iota(jnp.int32, (CHUNK, CHUNK), 0) >= lax.broadcasted_iota(jnp.int32, (CHUNK, CHUNK), 1)
    wm = [jnp.where(tri, ws_ref[g], 0.0).astype(BF16) for g in range(GM_GROUPS)]
    low_half = lax.broadcasted_iota(jnp.int32, (CHUNK, LANES), 1) < GM_CH
    for c in range(rows // CHUNK):
        r0 = c * CHUNK
        parts = []
        for pr in range(GM_GROUPS // 2):
            tile = vn[r0:r0 + CHUNK, pr * LANES:(pr + 1) * LANES]
            parts.append(jnp.where(low_half, _dot(wm[2 * pr], tile), _dot(wm[2 * pr + 1], tile)))
        sg = jnp.concatenate(parts, axis=1) + bias_ref[...]
        gm = u[r0:r0 + CHUNK] * sg
        g_ref[0, r0:r0 + CHUNK, :] = _rms(gm, gog_ref[...]).astype(BF16)


def _prep(x, pos4, w):
    B, S, D = x.shape
    ts = PREP_ROWS
    full = lambda a: pl.BlockSpec(a.shape, lambda b, i: (0,) * a.ndim)
    consts = [w["win"], w["qg"], w["wq"], w["kvg"], w["wk"], w["wv"], w["inv"], w["rope"], w["one"],
              w["lng"], w["lnb"], w["gavg"], w["ws"], w["bias"], w["gog"]]
    return pl.pallas_call(
        _prep_kernel,
        grid=(B, S // ts),
        in_specs=[pl.BlockSpec((1, ts, D), lambda b, i: (b, i, 0)),
                  pl.BlockSpec((1, 1, 1, ts), lambda b, i: (b, i, 0, 0))] + [full(a) for a in consts],
        out_specs=[pl.BlockSpec((1, ts, HP), lambda b, i: (b, i, 0)),
                   pl.BlockSpec((1, ts, HP), lambda b, i: (b, i, 0)),
                   pl.BlockSpec((1, 1, MLA_OUT, ts), lambda b, i: (b, i, 0, 0)),
                   pl.BlockSpec((1, ts, GM_OUT), lambda b, i: (b, i, 0))],
        out_shape=[jax.ShapeDtypeStruct((B, S, HP), BF16)] * 2
        + [jax.ShapeDtypeStruct((B, S // ts, MLA_OUT, ts), BF16), jax.ShapeDtypeStruct((B, S, GM_OUT), BF16)],
        compiler_params=pltpu.CompilerParams(dimension_semantics=("parallel", "parallel"),
                                             vmem_limit_bytes=VMEM_LIMIT),
        name="prep",
    )(x, pos4, *consts)


def _attn_kernel(q_ref, k_ref, vt_ref, g_ref, x_ref, woa_ref, wog_ref, mog_ref, l1g_ref, l1b_ref,
                 o_ref, m_scr, acc_scr, sa_scr, sb_scr):
    i = pl.program_id(1)
    tq = q_ref.shape[1]
    tk = tq
    key = lax.broadcasted_iota(jnp.int32, (tk, tq), 0)
    qry = lax.broadcasted_iota(jnp.int32, (tk, tq), 1)
    diag_mask = key <= qry

    m_scr[...] = jnp.full(m_scr.shape, -1e30, F32)
    acc_scr[...] = jnp.zeros(acc_scr.shape, F32)
    ones = jnp.ones((ONES_ROWS, tk), BF16)

    def scores(j, s_scr):
        k0 = pl.multiple_of(j * tk, tk)
        for hd in range(MLA_HEADS):
            lo = hd * LANES
            qh = q_ref[0, :, lo:lo + LANES]
            kj = k_ref[0, pl.ds(k0, tk), lo:lo + LANES]
            s_scr[hd] = lax.dot_general(kj, qh, (((1,), (1,)), ((), ())), preferred_element_type=F32)

    def update(j, s_scr, masked):
        for hd in range(MLA_HEADS):
            s = s_scr[hd]
            vt = vt_ref[0, j, hd * V_HEAD:(hd + 1) * V_HEAD, :]
            if masked:
                s = jnp.where(diag_mask, s, -1e30)
            m_prev = m_scr[hd]
            m_new = jnp.maximum(m_prev, jnp.max(s, axis=0, keepdims=True))
            p = jnp.exp2(s - m_new).astype(BF16)
            scale = jnp.exp2(m_prev - m_new)
            acc_scr[hd] = scale * acc_scr[hd] + _dot(jnp.concatenate([vt, ones], axis=0), p)
            m_scr[hd] = m_new

    def pair(jj, c):
        j = 2 * jj
        scores(j + 1, sb_scr)
        update(j, sa_scr, False)
        scores(j + 2, sa_scr)
        update(j + 1, sb_scr, False)
        return c

    scores(0, sa_scr)
    lax.fori_loop(0, lax.shift_right_logical(i, 1), pair, 0)

    @pl.when((i & 1) == 0)
    def _():
        update(i, sa_scr, True)

    @pl.when((i & 1) == 1)
    def _():
        scores(i, sb_scr)
        update(i - 1, sa_scr, False)
        update(i, sb_scr, True)

    at = jnp.concatenate([acc_scr[hd, :V_HEAD] / acc_scr[hd, V_HEAD:V_HEAD + 1] for hd in range(MLA_HEADS)],
                         axis=0)
    at = at * lax.rsqrt(jnp.mean(at * at, axis=0, keepdims=True) + EPS) * mog_ref[...]
    mix = _dot(at.T.astype(BF16), woa_ref[...]) + _dot(g_ref[0], wog_ref[...])
    o_ref[0] = _ln(ALPHA * x_ref[0] + mix, l1g_ref[...], l1b_ref[...])


def _attn(q, k, vt, g, x, w):
    B, S, D = x.shape
    tq = ATTN_ROWS
    full = lambda a: pl.BlockSpec(a.shape, lambda b, i: (0,) * a.ndim)
    consts = [w["woa"], w["wog"], w["mog"], w["l1g"], w["l1b"]]
    return pl.pallas_call(
        _attn_kernel,
        grid=(B, S // tq),
        in_specs=[pl.BlockSpec((1, tq, HP), lambda b, i: (b, i, 0)),
                  pl.BlockSpec((1, S, HP), lambda b, i: (b, 0, 0)),
                  pl.BlockSpec((1,) + vt.shape[1:], lambda b, i: (b, 0, 0, 0)),
                  pl.BlockSpec((1, tq, GM_OUT), lambda b, i: (b, i, 0)),
                  pl.BlockSpec((1, tq, D), lambda b, i: (b, i, 0))] + [full(a) for a in consts],
        out_specs=pl.BlockSpec((1, tq, D), lambda b, i: (b, i, 0)),
        out_shape=jax.ShapeDtypeStruct((B, S, D), F32),
        scratch_shapes=[pltpu.VMEM((MLA_HEADS, 1, tq), F32),
                        pltpu.VMEM((MLA_HEADS, V_HEAD + ONES_ROWS, tq), F32),
                        pltpu.VMEM((MLA_HEADS, tq, tq), F32), pltpu.VMEM((MLA_HEADS, tq, tq), F32)],
        compiler_params=pltpu.CompilerParams(dimension_semantics=("parallel", "parallel"),
                                             vmem_limit_bytes=VMEM_LIMIT),
        name="attn",
    )(q, k, vt, g, x, *consts)


def _route_kernel(x_ref, wr_ref, br_ref, info_ref, infot_ref, cnt_ref, carry_scr, tri_scr):
    step = pl.program_id(0)
    tt = x_ref.shape[0]

    @pl.when(step == 0)
    def _():
        carry_scr[...] = jnp.zeros_like(carry_scr)
        r = lax.broadcasted_iota(jnp.int32, (tt, tt), 0)
        c = lax.broadcasted_iota(jnp.int32, (tt, tt), 1)
        tri_scr[...] = jnp.where(c < r, 1.0, 0.0).astype(BF16)

    x = x_ref[...]
    xh = x.astype(BF16)
    xl = (x - xh.astype(F32)).astype(BF16)
    wr = wr_ref[...]
    wh = wr.astype(BF16)
    wl = (wr - wh.astype(F32)).astype(BF16)
    logits = _dot(xh, wh) + _dot(xl, wh) + _dot(xh, wl) + br_ref[...]

    lane = lax.broadcasted_iota(jnp.int32, (tt, LANES), 1)
    neg = jnp.float32(-jnp.inf)

    is_g = lane < N_GROUPS
    lg = jnp.where(is_g, logits, neg)
    gmax = jnp.max(lg, axis=-1, keepdims=True)
    g_idx = jnp.min(jnp.where(lg == gmax, lane, LANES), axis=-1, keepdims=True)
    g_den = jnp.sum(jnp.where(is_g, jnp.exp(lg - gmax), 0.0), axis=-1, keepdims=True)
    g_p = 1.0 / g_den

    in_grp = (lane >= R_OFF) & (lane < R_OFF + N_EXPERTS) & (((lane - R_OFF) >> 3) == g_idx)
    le = jnp.where(in_grp, logits, neg)
    m1 = jnp.max(le, axis=-1, keepdims=True)
    i1 = jnp.min(jnp.where(le == m1, lane, LANES), axis=-1, keepdims=True)
    le2 = jnp.where(lane == i1, neg, le)
    m2 = jnp.max(le2, axis=-1, keepdims=True)
    i2 = jnp.min(jnp.where(le2 == m2, lane, LANES), axis=-1, keepdims=True)
    e2 = jnp.exp(m2 - m1)
    gate0 = g_p / (1.0 + e2)
    gate1 = g_p * e2 / (1.0 + e2)

    hit1 = lane == i1
    hit2 = lane == i2
    onehot = jnp.where(hit1 | hit2, 1.0, 0.0)
    before = _dot(tri_scr[...], onehot.astype(BF16)) + carry_scr[...]
    r0 = jnp.sum(jnp.where(hit1, before, 0.0), axis=-1, keepdims=True)
    r1 = jnp.sum(jnp.where(hit2, before, 0.0), axis=-1, keepdims=True)
    carry_scr[...] = carry_scr[...] + jnp.sum(onehot, axis=0, keepdims=True)
    cnt_ref[...] = carry_scr[...]

    info = jnp.where(lane == I_E0, (i1 - R_OFF).astype(F32), 0.0)
    info = jnp.where(lane == I_E1, (i2 - R_OFF).astype(F32), info)
    info = jnp.where(lane == I_R0, r0, info)
    info = jnp.where(lane == I_R1, r1, info)
    info = jnp.where(lane == I_G0, gate0, info)
    info = jnp.where(lane == I_G1, gate1, info)
    info_ref[...] = info
    infot_ref[0] = info.T[:SUBLANES]


def _route(x1, wr, br):
    T, D = x1.shape
    tt = ROUTE_ROWS
    return pl.pallas_call(
        _route_kernel,
        grid=(T // tt,),
        in_specs=[pl.BlockSpec((tt, D), lambda i: (i, 0)),
                  pl.BlockSpec(wr.shape, lambda i: (0, 0)),
                  pl.BlockSpec(br.shape, lambda i: (0, 0))],
        out_specs=[pl.BlockSpec((tt, LANES), lambda i: (i, 0)),
                   pl.BlockSpec((1, SUBLANES, tt), lambda i: (i, 0, 0)),
                   pl.BlockSpec((1, LANES), lambda i: (0, 0))],
        out_shape=[jax.ShapeDtypeStruct((T, LANES), F32), jax.ShapeDtypeStruct((T // tt, SUBLANES, tt), F32),
                   jax.ShapeDtypeStruct((1, LANES), F32)],
        scratch_shapes=[pltpu.VMEM((1, LANES), F32), pltpu.VMEM((tt, tt), BF16)],
        compiler_params=pltpu.CompilerParams(dimension_semantics=("arbitrary",), vmem_limit_bytes=VMEM_LIMIT),
        name="route",
    )(x1, wr, br)


def _to_token_tiles(dst_ref, val):
    dst_ref[...] = val.astype(BF16).reshape(dst_ref.shape)


def _from_token_tiles(src_ref, rows):
    return src_ref[...].reshape(rows, TOKEN_ROWS * LANES)


def _tile_copy(src_ref, src_row, dst_ref, dst_row, sem):
    return pltpu.make_async_copy(src_ref.at[pl.ds(pl.multiple_of(src_row, TOKEN_ROWS), TOKEN_ROWS)],
                                 dst_ref.at[pl.ds(pl.multiple_of(dst_row, TOKEN_ROWS), TOKEN_ROWS)], sem)


def _dispatch_kernel(seg_ref, dest_ref, x_ref, buf_ref, stage_scr, zero_scr, sem, zero_sem, *, n_steps):
    i = pl.program_id(0)
    rows = x_ref.shape[0]
    slot = i % 2

    @pl.when(i == 0)
    def _():
        zero_scr[...] = jnp.zeros(zero_scr.shape, BF16)

        block = EXPERT_ROWS * TOKEN_ROWS
        n_blocks = buf_ref.shape[0] // block

        def clear_rows(first):
            return pltpu.make_async_copy(zero_scr, buf_ref.at[pl.ds(pl.multiple_of(first, SUBLANES), block)], zero_sem)

        def clear(e):
            return clear_rows((seg_ref[0, e] - EXPERT_ROWS) * TOKEN_ROWS)

        def start_tail(b, c):
            clear_rows(b * block).start()
            return c

        def wait_tail(b, c):
            clear_rows(b * block).wait()
            return c

        for e in range(N_EXPERTS):
            pl.when(seg_ref[1, e] > 0)(lambda e=e: clear(e).start())
        lax.fori_loop(seg_ref[2, 0], n_blocks, start_tail, 0)
        for e in range(N_EXPERTS):
            pl.when(seg_ref[1, e] > 0)(lambda e=e: clear(e).wait())
        lax.fori_loop(seg_ref[2, 0], n_blocks, wait_tail, 0)

    def drain(s):
        for _ in range(TOP_K):
            pltpu.make_async_copy(stage_scr.at[s], stage_scr.at[s], sem.at[s]).wait()

    @pl.when(i >= 2)
    def _():
        drain(slot)

    _to_token_tiles(stage_scr.at[slot], x_ref[...])

    def start(c, carry):
        for u in range(MOVE_UNROLL):
            r = c * MOVE_UNROLL + u
            for kk in range(TOP_K):
                _tile_copy(stage_scr.at[slot], r * TOKEN_ROWS, buf_ref, dest_ref[0, 0, kk * rows + r],
                           sem.at[slot]).start(priority=kk)
        return carry

    lax.fori_loop(0, rows // MOVE_UNROLL, start, 0)

    @pl.when(i == n_steps - 1)
    def _():
        drain(slot)
        if n_steps >= 2:
            drain(1 - slot)


def _dispatch(seg, dest3, x1, n_rows):
    T, D = x1.shape
    td = MOVE_ROWS
    n_steps = T // td
    grid_spec = pltpu.PrefetchScalarGridSpec(
        num_scalar_prefetch=1,
        grid=(n_steps,),
        in_specs=[pl.BlockSpec((1, 1, TOP_K * td), lambda i, seg: (i, 0, 0), memory_space=pltpu.SMEM),
                  pl.BlockSpec((td, D), lambda i, seg: (i, 0))],
        out_specs=pl.BlockSpec(memory_space=pl.ANY),
        scratch_shapes=[pltpu.VMEM((2, td * TOKEN_ROWS, LANES), BF16),
                        pltpu.VMEM((EXPERT_ROWS * TOKEN_ROWS, LANES), BF16),
                        pltpu.SemaphoreType.DMA((2,)), pltpu.SemaphoreType.DMA(())],
    )
    return pl.pallas_call(
        functools.partial(_dispatch_kernel, n_steps=n_steps),
        grid_spec=grid_spec,
        out_shape=jax.ShapeDtypeStruct((n_rows * TOKEN_ROWS, LANES), BF16),
        compiler_params=pltpu.CompilerParams(dimension_semantics=("arbitrary",), vmem_limit_bytes=VMEM_LIMIT),
        name="dispatch",
    )(seg, dest3, x1)


def _expert_kernel(be_ref, ne_ref, nu_ref, buf0_ref, bufa_ref, bufb_ref, wg_hbm, wu_hbm, wd_hbm, y_ref,
                   sg_scr, su_scr, sd_scr, wg_scr, wu_scr, wd_scr, xa_scr, xb_scr, cur_ref, sem):
    step = pl.program_id(0)
    bm = EXPERT_ROWS
    half = bm * TOKEN_ROWS

    def fetch(expert, s):
        return (pltpu.make_async_copy(wg_hbm.at[expert], sg_scr.at[s], sem.at[s, 0]),
                pltpu.make_async_copy(wu_hbm.at[expert], su_scr.at[s], sem.at[s, 1]),
                pltpu.make_async_copy(wd_hbm.at[expert], sd_scr.at[s], sem.at[s, 2]))

    @pl.when(step == 0)
    def _():
        cur_ref[0] = 0
        for c in fetch(be_ref[0], 0):
            c.start()
        xa_scr[...] = _from_token_tiles(buf0_ref, bm)

    def load_weights(blk):
        e = be_ref[blk]

        @pl.when((blk == 0) | (be_ref[jnp.maximum(blk - 1, 0)] != e))
        def _():
            s = cur_ref[0]
            for c in fetch(e, s):
                c.wait()
            wg_scr[...] = sg_scr[s].astype(BF16)
            wu_scr[...] = su_scr[s].astype(BF16)
            wd_scr[...] = sd_scr[s].astype(BF16)
            nxt = ne_ref[blk]

            @pl.when(nxt >= 0)
            def _():
                for c in fetch(nxt, 1 - s):
                    c.start()

            cur_ref[0] = 1 - s

    def run(blk, x_scr, nxt_ref, nxt_scr, out_rows):
        load_weights(blk)

        @pl.when(blk < nu_ref[0])
        def _():
            nxt_scr[...] = _from_token_tiles(nxt_ref, bm)
            xb = x_scr[...]
            hidden = jax.nn.silu(_dot(xb, wg_scr[...])) * _dot(xb, wu_scr[...])
            _to_token_tiles(y_ref.at[out_rows], _dot(hidden.astype(BF16), wd_scr[...]))

        @pl.when(blk >= nu_ref[0])
        def _():
            y_ref[out_rows, :] = jnp.zeros((half, LANES), BF16)

    run(2 * step, xa_scr, bufa_ref, xb_scr, pl.ds(0, half))
    run(2 * step + 1, xb_scr, bufb_ref, xa_scr, pl.ds(half, half))


def _experts(block_expert, next_expert, n_used, buf, w_gate, w_up, w_down):
    bm = EXPERT_ROWS
    D, ff = w_gate.shape[1:]
    n_blocks = buf.shape[0] // (bm * TOKEN_ROWS)
    assert n_blocks % 2 == 0
    last = n_blocks - 1
    grid_spec = pltpu.PrefetchScalarGridSpec(
        num_scalar_prefetch=3,
        grid=(n_blocks // 2,),
        in_specs=[pl.BlockSpec((bm * TOKEN_ROWS, LANES), lambda s, *_: (0, 0)),
                  pl.BlockSpec((bm * TOKEN_ROWS, LANES), lambda s, *_: (2 * s + 1, 0)),
                  pl.BlockSpec((bm * TOKEN_ROWS, LANES), lambda s, *_: (jnp.minimum(2 * s + 2, last), 0)),
                  pl.BlockSpec(memory_space=pl.ANY),
                  pl.BlockSpec(memory_space=pl.ANY),
                  pl.BlockSpec(memory_space=pl.ANY)],
        out_specs=pl.BlockSpec((2 * bm * TOKEN_ROWS, LANES), lambda s, *_: (s, 0)),
        scratch_shapes=[pltpu.VMEM((2, D, ff), F32), pltpu.VMEM((2, D, ff), F32), pltpu.VMEM((2, ff, D), F32),
                        pltpu.VMEM((D, ff), BF16), pltpu.VMEM((D, ff), BF16), pltpu.VMEM((ff, D), BF16),
                        pltpu.VMEM((bm, D), BF16), pltpu.VMEM((bm, D), BF16),
                        pltpu.SMEM((1,), jnp.int32), pltpu.SemaphoreType.DMA((2, 3))],
    )
    return pl.pallas_call(
        _expert_kernel,
        grid_spec=grid_spec,
        out_shape=jax.ShapeDtypeStruct(buf.shape, BF16),
        compiler_params=pltpu.CompilerParams(dimension_semantics=("arbitrary",), vmem_limit_bytes=VMEM_LIMIT),
        name="experts",
    )(block_expert, next_expert, n_used, buf, buf, buf, w_gate, w_up, w_down)


def _final_kernel(dcur_ref, dnxt_ref, x_ref, info_ref, y_ref, p_ref, wpg_ref, bpg_ref, wpp_ref,
                  l2g_ref, l2b_ref, l3g_ref, l3b_ref, o_ref, rows_scr, sem):
    i = pl.program_id(0)
    last = pl.num_programs(0) - 1
    rows = x_ref.shape[0]
    slot = i % 2

    def row_copy(dref, s, r, kk):
        return _tile_copy(y_ref, dref[0, 0, kk * rows + r], rows_scr.at[s, kk], r * TOKEN_ROWS, sem.at[s])

    def landed(s):
        pltpu.make_async_copy(rows_scr.at[s], rows_scr.at[s], sem.at[s]).wait()

    @pl.when(i == 0)
    def _():
        def start(c, carry):
            for u in range(MOVE_UNROLL):
                for kk in range(TOP_K):
                    row_copy(dcur_ref, 0, c * MOVE_UNROLL + u, kk).start(priority=kk)
            return carry

        lax.fori_loop(0, rows // MOVE_UNROLL, start, 0)

    landed(slot)
    info = info_ref[...]
    gate0 = info[:, I_G0:I_G0 + 1]
    gate1 = info[:, I_G1:I_G1 + 1]
    moe = (_from_token_tiles(rows_scr.at[slot, 0], rows).astype(F32) * gate0
           + _from_token_tiles(rows_scr.at[slot, 1], rows).astype(F32) * gate1)

    for r in range(rows):
        for kk in range(TOP_K):
            row_copy(dnxt_ref, 1 - slot, r, kk).start(priority=kk)

    pp = _dot(p_ref[...].astype(BF16), wpp_ref[...])
    x2 = _ln(ALPHA * x_ref[...] + moe, l2g_ref[...], l2b_ref[...])
    gate = jax.nn.sigmoid(_dot(x2.astype(BF16), wpg_ref[...]) + bpg_ref[...])
    o_ref[...] = _ln(ALPHA * x2 + gate * pp, l3g_ref[...], l3b_ref[...])

    @pl.when(i == last)
    def _():
        landed(1 - slot)


def _final(dest3, x1, info, y, p2, w):
    T, D = x1.shape
    tc = MOVE_ROWS
    pd = p2.shape[1]
    full = lambda a: pl.BlockSpec(a.shape, lambda i: (0,) * a.ndim)
    consts = [w["wpg"], w["bpg"], w["wpp"], w["l2g"], w["l2b"], w["l3g"], w["l3b"]]
    last = T // tc - 1
    return pl.pallas_call(
        _final_kernel,
        grid=(T // tc,),
        in_specs=[pl.BlockSpec((1, 1, TOP_K * tc), lambda i: (i, 0, 0), memory_space=pltpu.SMEM),
                  pl.BlockSpec((1, 1, TOP_K * tc), lambda i: (jnp.minimum(i + 1, last), 0, 0), memory_space=pltpu.SMEM),
                  pl.BlockSpec((tc, D), lambda i: (i, 0)),
                  pl.BlockSpec((tc, LANES), lambda i: (i, 0)),
                  pl.BlockSpec(memory_space=pl.ANY),
                  pl.BlockSpec((tc, pd), lambda i: (i, 0))] + [full(a) for a in consts],
        out_specs=pl.BlockSpec((tc, D), lambda i: (i, 0)),
        out_shape=jax.ShapeDtypeStruct((T, D), F32),
        scratch_shapes=[pltpu.VMEM((2, TOP_K, tc * TOKEN_ROWS, LANES), BF16), pltpu.SemaphoreType.DMA((2,))],
        compiler_params=pltpu.CompilerParams(dimension_semantics=("arbitrary",), vmem_limit_bytes=VMEM_LIMIT),
        name="final",
    )(dest3, dest3, x1, info, y, p2, *consts)


def _pad_heads(a, width):
    lead = a.shape[:-1]
    a = a.reshape(lead + (MLA_HEADS, width))
    a = jnp.pad(a, [(0, 0)] * len(lead) + [(0, 0), (0, LANES - width)])
    return a.reshape(lead + (HP,))


def _layer_weights(w_in, q_norm_g, w_q_up, kv_norm_g, w_kv_up, gm_ln_g, gm_ln_b, gm_w_s, gm_b_s,
                   mla_out_g, gm_out_g, w_o, ln1_g, ln1_b):
    D = w_in.shape[0]
    half = QK_ROPE // 2
    c1, c2, c3 = Q_RANK, Q_RANK + KV_RANK, Q_RANK + KV_RANK + QK_ROPE
    zeros = lambda *s: jnp.zeros(s, F32)
    kr = jnp.concatenate([zeros(D, QK_NOPE), w_in[:, c2:c3], zeros(D, LANES - QK_NOPE - QK_ROPE)], axis=1)
    win = jnp.concatenate([w_in[:, :c2], kr, w_in[:, c3:]], axis=1).astype(BF16)
    wq = _pad_heads(w_q_up, QK_NOPE + QK_ROPE).astype(BF16)

    wkv3 = w_kv_up.reshape(KV_RANK, MLA_HEADS, QK_NOPE + V_HEAD)
    wk = _pad_heads(wkv3[..., :QK_NOPE].reshape(KV_RANK, -1), QK_NOPE).astype(BF16)
    wv = wkv3[..., QK_NOPE:].reshape(KV_RANK, -1).astype(BF16)

    inv = (ROPE_THETA ** (-jnp.arange(0, QK_ROPE, 2, dtype=F32) / QK_ROPE))[:, None]
    eye = jnp.eye(half, dtype=F32)
    first = jnp.pad(eye, ((0, 0), (QK_NOPE, LANES - QK_NOPE - half)))
    second = jnp.pad(eye, ((0, 0), (QK_NOPE + half, LANES - QK_NOPE - QK_ROPE)))
    zero = jnp.zeros_like(first)
    cos_rows = jnp.concatenate([first + second, zero, zero], axis=1)
    sin_rows = jnp.concatenate([zero, -first, second], axis=1)
    rope = jnp.concatenate([cos_rows, cos_rows, sin_rows, sin_rows], axis=0).astype(BF16)
    lane = jnp.arange(LANES)
    one = jnp.where((lane >= QK_NOPE) & (lane < QK_NOPE + QK_ROPE), 0.0, 1.0)[None, :]

    grp = jnp.arange(GM_OUT) // GM_CH
    gavg = jnp.where(grp[:, None] == grp[None, :], 1.0 / GM_CH, 0.0).astype(BF16)
    bias = jnp.repeat(gm_b_s.T, GM_CH, axis=1)

    woa = w_o[:MLA_OUT].astype(BF16)
    wog = w_o[MLA_OUT:].astype(BF16)
    return dict(win=win, qg=q_norm_g[None, :], wq=wq, kvg=kv_norm_g[None, :], wk=wk, wv=wv, inv=inv, rope=rope, one=one,
                lng=gm_ln_g[None, :], lnb=gm_ln_b[None, :], gavg=gavg, ws=gm_w_s, bias=bias, gog=gm_out_g[None, :],
                woa=woa, wog=wog, mog=mla_out_g[:, None], l1g=ln1_g[None, :], l1b=ln1_b[None, :])


def _moe(x1, w_rg, b_rg, w_re, b_re, w_gate, w_up, w_down):
    T, D = x1.shape
    pad = jnp.zeros((D, LANES - N_GROUPS - N_EXPERTS), F32)
    wr = jnp.concatenate([w_rg, w_re, pad], axis=1)
    br = jnp.concatenate([b_rg, b_re, pad[0]])[None, :]
    info, info_t, cnt = _route(x1, wr, br)

    bm = EXPERT_ROWS
    n_blocks = (T * TOP_K) // bm + N_EXPERTS
    counts = cnt[0, R_OFF:R_OFF + N_EXPERTS].astype(jnp.int32)
    padded = (counts + bm - 1) // bm * bm
    pad_ends = jnp.cumsum(padded)
    pad_starts = pad_ends - padded
    def dest_rows(e_lane, r_lane):
        e = info_t[:, e_lane, :].astype(jnp.int32)
        ids = jnp.arange(N_EXPERTS)[:, None, None]
        seg_start = jnp.sum(jnp.where(e[None] == ids, pad_starts[:, None, None], 0), axis=0)
        return ((seg_start + info_t[:, r_lane, :].astype(jnp.int32)) * TOKEN_ROWS).reshape(T // MOVE_ROWS, MOVE_ROWS)

    dest = jnp.concatenate([dest_rows(I_E0, I_R0), dest_rows(I_E1, I_R1)], axis=1)[:, None, :]
    block_start = jnp.arange(n_blocks, dtype=jnp.int32) * bm
    block_expert = jnp.minimum(jnp.sum(pad_ends[None, :] <= block_start[:, None], axis=1),
                               N_EXPERTS - 1).astype(jnp.int32)

    blk = jnp.arange(n_blocks)
    later = (blk[None, :] > blk[:, None]) & (block_expert[None, :] != block_expert[:, None])
    next_expert = jnp.min(jnp.where(later, block_expert[None, :], N_EXPERTS), axis=1)
    next_expert = jnp.where(next_expert == N_EXPERTS, -1, next_expert).astype(jnp.int32)
    n_used = (pad_ends[-1:] // bm).astype(jnp.int32)

    seg = jnp.stack([pad_ends, padded, jnp.broadcast_to(n_used, (N_EXPERTS,))]).astype(jnp.int32)
    buf = _dispatch(seg, dest, x1, n_blocks * bm)
    y = _experts(block_expert, next_expert, n_used, buf, w_gate, w_up, w_down)
    return info, dest, y


def kernel(x, p, positions, w_in, q_norm_g, w_q_up, kv_norm_g, w_kv_up, gm_ln_g, gm_ln_b, gm_w_s, gm_b_s, mla_out_g, gm_out_g, w_o, ln1_g, ln1_b, w_rg, b_rg, w_re, b_re, w_gate, w_up, w_down, ln2_g, ln2_b, w_pg, b_pg, w_pp, ln3_g, ln3_b):
    B, S, D = x.shape
    T = B * S
    assert S % ATTN_ROWS == 0 and PREP_ROWS == ATTN_ROWS and PREP_ROWS % CHUNK == 0
    assert T % ROUTE_ROWS == 0 and T % MOVE_ROWS == 0 and (T * TOP_K) % EXPERT_ROWS == 0
    assert D == TOKEN_ROWS * LANES and MOVE_ROWS % MOVE_UNROLL == 0
    pos4 = positions.reshape(B, S // PREP_ROWS, 1, PREP_ROWS)
    for i in range(DEPTH):
        w = _layer_weights(w_in[i], q_norm_g[i], w_q_up[i], kv_norm_g[i], w_kv_up[i], gm_ln_g[i], gm_ln_b[i],
                           gm_w_s[i], gm_b_s[i], mla_out_g[i], gm_out_g[i], w_o[i], ln1_g[i], ln1_b[i])
        q, k, vt, g = _prep(x, pos4, w)
        x1 = _attn(q, k, vt, g, x, w).reshape(T, D)
        info, dest, y = _moe(x1, w_rg[i], b_rg[i], w_re[i], b_re[i], w_gate[i], w_up[i], w_down[i])
        wf = dict(wpg=w_pg[i].astype(BF16), bpg=b_pg[i][None, :], wpp=w_pp[i].astype(BF16),
                  l2g=ln2_g[i][None, :], l2b=ln2_b[i][None, :], l3g=ln3_g[i][None, :], l3b=ln3_b[i][None, :])
        x = _final(dest, x1, info, y, p[i].reshape(T, -1), wf).reshape(B, S, D)
    return x
```

```python
import functools

import jax
import jax.numpy as jnp
from jax import lax
from jax.experimental import pallas as pl
from jax.experimental.pallas import tpu as pltpu

F32 = jnp.float32
BF16 = jnp.bfloat16

MLA_HEADS = 8
QK_NOPE = 64
QK_ROPE = 32
V_HEAD = 64
Q_RANK = 256
KV_RANK = 128
ROPE_THETA = 10000.0
MLA_OUT = MLA_HEADS * V_HEAD
GM_GROUPS = 8
GM_CH = 64
GM_OUT = GM_GROUPS * GM_CH
CHUNK = 128
N_GROUPS = 4
EXP_PER_GROUP = 8
N_EXPERTS = N_GROUPS * EXP_PER_GROUP
TOP_K = 2
EPS = 1e-6
DEPTH = 1
ALPHA = (2.0 * DEPTH) ** 0.25
SM_SCALE = (QK_NOPE + QK_ROPE) ** -0.5
LOG2E = 1.4426950408889634

LANES = 128
SUBLANES = 8
TOKEN_ROWS = 8
ONES_ROWS = 16
VMEM_LIMIT = 56 * 1024 * 1024

PREP_ROWS = 512
ATTN_ROWS = 256
ROUTE_ROWS = 512
ROUTE_SUB = 512
MOVE_ROWS = 256
MOVE_UNROLL = 8
EXPERT_ROWS = 256

C_Q = 0
C_KV = C_Q + Q_RANK
C_KR = C_KV + KV_RANK
C_U = C_KR + LANES
C_V = C_U + GM_OUT
C_END = C_V + GM_OUT
HP = MLA_HEADS * LANES

I_E0, I_E1, I_R0, I_R1, I_G0, I_G1 = range(6)
R_OFF = N_GROUPS


def _rms(v, g):
    return v * lax.rsqrt(jnp.mean(v * v, axis=-1, keepdims=True) + EPS) * g


def _ln(v, g, b):
    mu = jnp.mean(v, axis=-1, keepdims=True)
    d = v - mu
    var = jnp.mean(d * d, axis=-1, keepdims=True)
    return d * lax.rsqrt(var + EPS) * g + b


def _dot(a, b):
    return jnp.dot(a, b, preferred_element_type=F32)


def _prep_kernel(x_ref, pos_ref, win_ref, qg_ref, wq_ref, kvg_ref, wk_ref, wv_ref, inv_ref, rope_ref, one_ref,
                 lng_ref, lnb_ref, gavg_ref, ws_ref, bias_ref, gog_ref,
                 q_ref, k_ref, vt_ref, g_ref):
    rows = x_ref.shape[1]
    h = _dot(x_ref[0].astype(BF16), win_ref[...])

    ang = inv_ref[...] * pos_ref[0, 0].astype(F32)
    parts = []
    for t in (jnp.cos(ang), jnp.sin(ang)):
        hi = t.astype(BF16).astype(F32)
        parts += [hi, t - hi]
    tabs = _dot(jnp.concatenate(parts, axis=0).T.astype(BF16), rope_ref[...])
    cos_t = tabs[:, :LANES] + one_ref[...]
    sin_a = tabs[:, LANES:2 * LANES]
    sin_b = tabs[:, 2 * LANES:]
    half = QK_ROPE // 2

    def rotate(v):
        return v * cos_t + pltpu.roll(v, LANES - half, 1) * sin_a + pltpu.roll(v, half, 1) * sin_b

    cq = _rms(h[:, C_Q:C_Q + Q_RANK], qg_ref[...]).astype(BF16)
    q2 = _dot(cq, wq_ref[...])
    for hd in range(MLA_HEADS):
        lo = hd * LANES
        q_ref[0, :, lo:lo + LANES] = (rotate(q2[:, lo:lo + LANES]) * (SM_SCALE * LOG2E)).astype(BF16)

    ckv = _rms(h[:, C_KV:C_KV + KV_RANK], kvg_ref[...]).astype(BF16)
    kp = _dot(ckv, wk_ref[...])
    kr = rotate(h[:, C_KR:C_KR + LANES])
    for hd in range(MLA_HEADS):
        lo = hd * LANES
        k_ref[0, :, lo:lo + LANES] = (kp[:, lo:lo + LANES] + kr).astype(BF16)
    vp = _dot(ckv, wv_ref[...])
    for kb in range(rows // ATTN_ROWS):
        vt_ref[0, kb] = vp[kb * ATTN_ROWS:(kb + 1) * ATTN_ROWS].T.astype(BF16)

    u = jax.nn.gelu(h[:, C_U:C_U + GM_OUT])
    vv = jax.nn.gelu(h[:, C_V:C_V + GM_OUT])
    mu = _dot(vv.astype(BF16), gavg_ref[...])
    d = vv - mu
    var = _dot((d * d).astype(BF16), gavg_ref[...])
    vn = (d * lax.rsqrt(var + EPS) * lng_ref[...] + lnb_ref[...]).astype(BF16)

    tri = lax.broadcasted_iota(jnp.int32, (CHUNK, CHUNK), 0) >= lax.broadcasted_iota(jnp.int32, (CHUNK, CHUNK), 1)
    wm = [jnp.where(tri, ws_ref[g], 0.0).astype(BF16) for g in range(GM_GROUPS)]
    low_half = lax.broadcasted_iota(jnp.int32, (CHUNK, LANES), 1) < GM_CH
    for c in range(rows // CHUNK):
        r0 = c * CHUNK
        parts = []
        for pr in range(GM_GROUPS // 2):
            tile = vn[r0:r0 + CHUNK, pr * LANES:(pr + 1) * LANES]
            parts.append(jnp.where(low_half, _dot(wm[2 * pr], tile), _dot(wm[2 * pr + 1], tile)))
        sg = jnp.concatenate(parts, axis=1) + bias_ref[...]
        gm = u[r0:r0 + CHUNK] * sg
        g_ref[0, r0:r0 + CHUNK, :] = _rms(gm, gog_ref[...]).astype(BF16)


def _prep(x, pos4, w):
    B, S, D = x.shape
    ts = PREP_ROWS
    full = lambda a: pl.BlockSpec(a.shape, lambda b, i: (0,) * a.ndim)
    consts = [w["win"], w["qg"], w["wq"], w["kvg"], w["wk"], w["wv"], w["inv"], w["rope"], w["one"],
              w["lng"], w["lnb"], w["gavg"], w["ws"], w["bias"], w["gog"]]
    return pl.pallas_call(
        _prep_kernel,
        grid=(B, S // ts),
        in_specs=[pl.BlockSpec((1, ts, D), lambda b, i: (b, i, 0)),
                  pl.BlockSpec((1, 1, 1, ts), lambda b, i: (b, i, 0, 0))] + [full(a) for a in consts],
        out_specs=[pl.BlockSpec((1, ts, HP), lambda b, i: (b, i, 0)),
                   pl.BlockSpec((1, ts, HP), lambda b, i: (b, i, 0)),
                   pl.BlockSpec((1, ts // ATTN_ROWS, MLA_OUT, ATTN_ROWS), lambda b, i: (b, i, 0, 0)),
                   pl.BlockSpec((1, ts, GM_OUT), lambda b, i: (b, i, 0))],
        out_shape=[jax.ShapeDtypeStruct((B, S, HP), BF16)] * 2
        + [jax.ShapeDtypeStruct((B, S // ATTN_ROWS, MLA_OUT, ATTN_ROWS), BF16),
           jax.ShapeDtypeStruct((B, S, GM_OUT), BF16)],
        compiler_params=pltpu.CompilerParams(dimension_semantics=("parallel", "parallel"),
                                             vmem_limit_bytes=VMEM_LIMIT),
        name="prep",
    )(x, pos4, *consts)


def _attn_kernel(q_ref, k_ref, vt_ref, g_ref, x_ref, woa_ref, wog_ref, mog_ref, l1g_ref, l1b_ref,
                 o_ref, m_scr, acc_scr, sa_scr, sb_scr):
    i = pl.program_id(1)
    tq = q_ref.shape[1]
    tk = tq
    key = lax.broadcasted_iota(jnp.int32, (tk, tq), 0)
    qry = lax.broadcasted_iota(jnp.int32, (tk, tq), 1)
    diag_mask = key <= qry

    m_scr[...] = jnp.full(m_scr.shape, -1e30, F32)
    acc_scr[...] = jnp.zeros(acc_scr.shape, F32)
    ones = jnp.ones((ONES_ROWS, tk), BF16)

    def scores(j, s_scr):
        k0 = pl.multiple_of(j * tk, tk)
        for hd in range(MLA_HEADS):
            lo = hd * LANES
            qh = q_ref[0, :, lo:lo + LANES]
            kj = k_ref[0, pl.ds(k0, tk), lo:lo + LANES]
            s_scr[hd] = lax.dot_general(kj, qh, (((1,), (1,)), ((), ())), preferred_element_type=F32)

    def update(j, s_scr, masked):
        for hd in range(MLA_HEADS):
            s = s_scr[hd]
            vt = vt_ref[0, j, hd * V_HEAD:(hd + 1) * V_HEAD, :]
            if masked:
                s = jnp.where(diag_mask, s, -1e30)
            m_prev = m_scr[hd]
            m_new = jnp.maximum(m_prev, jnp.max(s, axis=0, keepdims=True))
            p = jnp.exp2(s - m_new).astype(BF16)
            scale = jnp.exp2(m_prev - m_new)
            acc_scr[hd] = scale * acc_scr[hd] + _dot(jnp.concatenate([vt, ones], axis=0), p)
            m_scr[hd] = m_new

    def pair(jj, c):
        j = 2 * jj
        scores(j + 1, sb_scr)
        update(j, sa_scr, False)
        scores(j + 2, sa_scr)
        update(j + 1, sb_scr, False)
        return c

    scores(0, sa_scr)
    lax.fori_loop(0, lax.shift_right_logical(i, 1), pair, 0)

    @pl.when((i & 1) == 0)
    def _():
        update(i, sa_scr, True)

    @pl.when((i & 1) == 1)
    def _():
        scores(i, sb_scr)
        update(i - 1, sa_scr, False)
        update(i, sb_scr, True)

    at = jnp.concatenate([acc_scr[hd, :V_HEAD] / acc_scr[hd, V_HEAD:V_HEAD + 1] for hd in range(MLA_HEADS)],
                         axis=0)
    at = at * lax.rsqrt(jnp.mean(at * at, axis=0, keepdims=True) + EPS) * mog_ref[...]
    mix = _dot(at.T.astype(BF16), woa_ref[...]) + _dot(g_ref[0], wog_ref[...])
    o_ref[0] = _ln(ALPHA * x_ref[0] + mix, l1g_ref[...], l1b_ref[...])


def _attn(q, k, vt, g, x, w):
    B, S, D = x.shape
    tq = ATTN_ROWS
    full = lambda a: pl.BlockSpec(a.shape, lambda b, i: (0,) * a.ndim)
    consts = [w["woa"], w["wog"], w["mog"], w["l1g"], w["l1b"]]
    return pl.pallas_call(
        _attn_kernel,
        grid=(B, S // tq),
        in_specs=[pl.BlockSpec((1, tq, HP), lambda b, i: (b, i, 0)),
                  pl.BlockSpec((1, S, HP), lambda b, i: (b, 0, 0)),
                  pl.BlockSpec((1,) + vt.shape[1:], lambda b, i: (b, 0, 0, 0)),
                  pl.BlockSpec((1, tq, GM_OUT), lambda b, i: (b, i, 0)),
                  pl.BlockSpec((1, tq, D), lambda b, i: (b, i, 0))] + [full(a) for a in consts],
        out_specs=pl.BlockSpec((1, tq, D), lambda b, i: (b, i, 0)),
        out_shape=jax.ShapeDtypeStruct((B, S, D), F32),
        scratch_shapes=[pltpu.VMEM((MLA_HEADS, 1, tq), F32),
                        pltpu.VMEM((MLA_HEADS, V_HEAD + ONES_ROWS, tq), F32),
                        pltpu.VMEM((MLA_HEADS, tq, tq), F32), pltpu.VMEM((MLA_HEADS, tq, tq), F32)],
        compiler_params=pltpu.CompilerParams(dimension_semantics=("parallel", "parallel"),
                                             vmem_limit_bytes=VMEM_LIMIT),
        name="attn",
    )(q, k, vt, g, x, *consts)


def _route_kernel(x_ref, wr_ref, br_ref, info_ref, infot_ref, cnt_ref, carry_scr, tri_scr):
    step = pl.program_id(0)
    sub = tri_scr.shape[0]

    @pl.when(step == 0)
    def _():
        carry_scr[...] = jnp.zeros_like(carry_scr)
        r = lax.broadcasted_iota(jnp.int32, (sub, sub), 0)
        c = lax.broadcasted_iota(jnp.int32, (sub, sub), 1)
        tri_scr[...] = jnp.where(c < r, 1.0, 0.0).astype(BF16)

    wr = wr_ref[...]
    wh = wr.astype(BF16)
    wl = (wr - wh.astype(F32)).astype(BF16)
    lane = lax.broadcasted_iota(jnp.int32, (sub, LANES), 1)
    neg = jnp.float32(-jnp.inf)
    carry = carry_scr[...]

    for h in range(x_ref.shape[0] // sub):
        r0_, r1_ = h * sub, (h + 1) * sub
        x = x_ref[r0_:r1_, :]
        xh = x.astype(BF16)
        xl = (x - xh.astype(F32)).astype(BF16)
        logits = _dot(xh, wh) + _dot(xl, wh) + _dot(xh, wl) + br_ref[...]

        is_g = lane < N_GROUPS
        lg = jnp.where(is_g, logits, neg)
        gmax = jnp.max(lg, axis=-1, keepdims=True)
        g_idx = jnp.min(jnp.where(lg == gmax, lane, LANES), axis=-1, keepdims=True)
        g_den = jnp.sum(jnp.where(is_g, jnp.exp(lg - gmax), 0.0), axis=-1, keepdims=True)
        g_p = 1.0 / g_den

        in_grp = (lane >= R_OFF) & (lane < R_OFF + N_EXPERTS) & (((lane - R_OFF) >> 3) == g_idx)
        le = jnp.where(in_grp, logits, neg)
        m1 = jnp.max(le, axis=-1, keepdims=True)
        i1 = jnp.min(jnp.where(le == m1, lane, LANES), axis=-1, keepdims=True)
        le2 = jnp.where(lane == i1, neg, le)
        m2 = jnp.max(le2, axis=-1, keepdims=True)
        i2 = jnp.min(jnp.where(le2 == m2, lane, LANES), axis=-1, keepdims=True)
        e2 = jnp.exp(m2 - m1)
        gate0 = g_p / (1.0 + e2)
        gate1 = g_p * e2 / (1.0 + e2)

        hit1 = lane == i1
        hit2 = lane == i2
        onehot = jnp.where(hit1 | hit2, 1.0, 0.0)
        before = _dot(tri_scr[...], onehot.astype(BF16)) + carry
        rank0 = jnp.sum(jnp.where(hit1, before, 0.0), axis=-1, keepdims=True)
        rank1 = jnp.sum(jnp.where(hit2, before, 0.0), axis=-1, keepdims=True)
        carry = carry + jnp.sum(onehot, axis=0, keepdims=True)

        info = jnp.where(lane == I_E0, (i1 - R_OFF).astype(F32), 0.0)
        info = jnp.where(lane == I_E1, (i2 - R_OFF).astype(F32), info)
        info = jnp.where(lane == I_R0, rank0, info)
        info = jnp.where(lane == I_R1, rank1, info)
        info = jnp.where(lane == I_G0, gate0, info)
        info = jnp.where(lane == I_G1, gate1, info)
        info_ref[r0_:r1_, :] = info
        infot_ref[0, :, r0_:r1_] = info.T[:SUBLANES]

    carry_scr[...] = carry
    cnt_ref[...] = carry


def _route(x1, wr, br):
    T, D = x1.shape
    tt = ROUTE_ROWS
    return pl.pallas_call(
        _route_kernel,
        grid=(T // tt,),
        in_specs=[pl.BlockSpec((tt, D), lambda i: (i, 0)),
                  pl.BlockSpec(wr.shape, lambda i: (0, 0)),
                  pl.BlockSpec(br.shape, lambda i: (0, 0))],
        out_specs=[pl.BlockSpec((tt, LANES), lambda i: (i, 0)),
                   pl.BlockSpec((1, SUBLANES, tt), lambda i: (i, 0, 0)),
                   pl.BlockSpec((1, LANES), lambda i: (0, 0))],
        out_shape=[jax.ShapeDtypeStruct((T, LANES), F32), jax.ShapeDtypeStruct((T // tt, SUBLANES, tt), F32),
                   jax.ShapeDtypeStruct((1, LANES), F32)],
        scratch_shapes=[pltpu.VMEM((1, LANES), F32), pltpu.VMEM((ROUTE_SUB, ROUTE_SUB), BF16)],
        compiler_params=pltpu.CompilerParams(dimension_semantics=("arbitrary",), vmem_limit_bytes=VMEM_LIMIT),
        name="route",
    )(x1, wr, br)


def _to_token_tiles(dst_ref, val):
    dst_ref[...] = val.astype(BF16).reshape(dst_ref.shape)


def _from_token_tiles(src_ref, rows):
    return src_ref[...].reshape(rows, TOKEN_ROWS * LANES)


def _tile_copy(src_ref, src_row, dst_ref, dst_row, sem):
    return pltpu.make_async_copy(src_ref.at[pl.ds(pl.multiple_of(src_row, TOKEN_ROWS), TOKEN_ROWS)],
                                 dst_ref.at[pl.ds(pl.multiple_of(dst_row, TOKEN_ROWS), TOKEN_ROWS)], sem)


def _dispatch_kernel(seg_ref, dest_ref, x_ref, buf_ref, stage_scr, zero_scr, sem, zero_sem, *, n_steps):
    i = pl.program_id(0)
    rows = x_ref.shape[0]
    slot = i % 2

    @pl.when(i == 0)
    def _():
        zero_scr[...] = jnp.zeros(zero_scr.shape, BF16)

        block = EXPERT_ROWS * TOKEN_ROWS
        n_blocks = buf_ref.shape[0] // block

        def clear_rows(first):
            return pltpu.make_async_copy(zero_scr, buf_ref.at[pl.ds(pl.multiple_of(first, SUBLANES), block)], zero_sem)

        def clear(e):
            return clear_rows((seg_ref[0, e] - EXPERT_ROWS) * TOKEN_ROWS)

        def start_tail(b, c):
            clear_rows(b * block).start()
            return c

        def wait_tail(b, c):
            clear_rows(b * block).wait()
            return c

        for e in range(N_EXPERTS):
            pl.when(seg_ref[1, e] > 0)(lambda e=e: clear(e).start())
        lax.fori_loop(seg_ref[2, 0], n_blocks, start_tail, 0)
        for e in range(N_EXPERTS):
            pl.when(seg_ref[1, e] > 0)(lambda e=e: clear(e).wait())
        lax.fori_loop(seg_ref[2, 0], n_blocks, wait_tail, 0)

    def drain(s):
        for _ in range(TOP_K):
            pltpu.make_async_copy(stage_scr.at[s], stage_scr.at[s], sem.at[s]).wait()

    @pl.when(i >= 2)
    def _():
        drain(slot)

    _to_token_tiles(stage_scr.at[slot], x_ref[...])

    def start(c, carry):
        for u in range(MOVE_UNROLL):
            r = c * MOVE_UNROLL + u
            for kk in range(TOP_K):
                _tile_copy(stage_scr.at[slot], r * TOKEN_ROWS, buf_ref, dest_ref[0, 0, kk * rows + r],
                           sem.at[slot]).start(priority=kk)
        return carry

    lax.fori_loop(0, rows // MOVE_UNROLL, start, 0)

    @pl.when(i == n_steps - 1)
    def _():
        drain(slot)
        if n_steps >= 2:
            drain(1 - slot)


def _dispatch(seg, dest3, x1, n_rows):
    T, D = x1.shape
    td = MOVE_ROWS
    n_steps = T // td
    grid_spec = pltpu.PrefetchScalarGridSpec(
        num_scalar_prefetch=1,
        grid=(n_steps,),
        in_specs=[pl.BlockSpec((1, 1, TOP_K * td), lambda i, seg: (i, 0, 0), memory_space=pltpu.SMEM),
                  pl.BlockSpec((td, D), lambda i, seg: (i, 0))],
        out_specs=pl.BlockSpec(memory_space=pl.ANY),
        scratch_shapes=[pltpu.VMEM((2, td * TOKEN_ROWS, LANES), BF16),
                        pltpu.VMEM((EXPERT_ROWS * TOKEN_ROWS, LANES), BF16),
                        pltpu.SemaphoreType.DMA((2,)), pltpu.SemaphoreType.DMA(())],
    )
    return pl.pallas_call(
        functools.partial(_dispatch_kernel, n_steps=n_steps),
        grid_spec=grid_spec,
        out_shape=jax.ShapeDtypeStruct((n_rows * TOKEN_ROWS, LANES), BF16),
        compiler_params=pltpu.CompilerParams(dimension_semantics=("arbitrary",), vmem_limit_bytes=VMEM_LIMIT),
        name="dispatch",
    )(seg, dest3, x1)


def _expert_kernel(be_ref, ne_ref, nu_ref, buf0_ref, bufa_ref, bufb_ref, wg_hbm, wu_hbm, wd_hbm, y_ref,
                   sg_scr, su_scr, sd_scr, wg_scr, wu_scr, wd_scr, xa_scr, xb_scr, cur_ref, sem):
    step = pl.program_id(0)
    bm = EXPERT_ROWS
    half = bm * TOKEN_ROWS

    def fetch(expert, s):
        return (pltpu.make_async_copy(wg_hbm.at[expert], sg_scr.at[s], sem.at[s, 0]),
                pltpu.make_async_copy(wu_hbm.at[expert], su_scr.at[s], sem.at[s, 1]),
                pltpu.make_async_copy(wd_hbm.at[expert], sd_scr.at[s], sem.at[s, 2]))

    @pl.when(step == 0)
    def _():
        cur_ref[0] = 0
        for c in fetch(be_ref[0], 0):
            c.start()
        xa_scr[...] = _from_token_tiles(buf0_ref, bm)

    def load_weights(blk):
        e = be_ref[blk]

        @pl.when((blk == 0) | (be_ref[jnp.maximum(blk - 1, 0)] != e))
        def _():
            s = cur_ref[0]
            for c in fetch(e, s):
                c.wait()
            wg_scr[...] = sg_scr[s].astype(BF16)
            wu_scr[...] = su_scr[s].astype(BF16)
            wd_scr[...] = sd_scr[s].astype(BF16)
            nxt = ne_ref[blk]

            @pl.when(nxt >= 0)
            def _():
                for c in fetch(nxt, 1 - s):
                    c.start()

            cur_ref[0] = 1 - s

    def run(blk, x_scr, nxt_ref, nxt_scr, out_rows):
        load_weights(blk)

        @pl.when(blk < nu_ref[0])
        def _():
            nxt_scr[...] = _from_token_tiles(nxt_ref, bm)
            xb = x_scr[...]
            hidden = jax.nn.silu(_dot(xb, wg_scr[...])) * _dot(xb, wu_scr[...])
            _to_token_tiles(y_ref.at[out_rows], _dot(hidden.astype(BF16), wd_scr[...]))

        @pl.when(blk >= nu_ref[0])
        def _():
            y_ref[out_rows, :] = jnp.zeros((half, LANES), BF16)

    run(2 * step, xa_scr, bufa_ref, xb_scr, pl.ds(0, half))
    run(2 * step + 1, xb_scr, bufb_ref, xa_scr, pl.ds(half, half))


def _experts(block_expert, next_expert, n_used, buf, w_gate, w_up, w_down):
    bm = EXPERT_ROWS
    D, ff = w_gate.shape[1:]
    n_blocks = buf.shape[0] // (bm * TOKEN_ROWS)
    assert n_blocks % 2 == 0
    last = n_blocks - 1
    grid_spec = pltpu.PrefetchScalarGridSpec(
        num_scalar_prefetch=3,
        grid=(n_blocks // 2,),
        in_specs=[pl.BlockSpec((bm * TOKEN_ROWS, LANES), lambda s, *_: (0, 0)),
                  pl.BlockSpec((bm * TOKEN_ROWS, LANES), lambda s, *_: (2 * s + 1, 0)),
                  pl.BlockSpec((bm * TOKEN_ROWS, LANES), lambda s, *_: (jnp.minimum(2 * s + 2, last), 0)),
                  pl.BlockSpec(memory_space=pl.ANY),
                  pl.BlockSpec(memory_space=pl.ANY),
                  pl.BlockSpec(memory_space=pl.ANY)],
        out_specs=pl.BlockSpec((2 * bm * TOKEN_ROWS, LANES), lambda s, *_: (s, 0)),
        scratch_shapes=[pltpu.VMEM((2, D, ff), F32), pltpu.VMEM((2, D, ff), F32), pltpu.VMEM((2, ff, D), F32),
                        pltpu.VMEM((D, ff), BF16), pltpu.VMEM((D, ff), BF16), pltpu.VMEM((ff, D), BF16),
                        pltpu.VMEM((bm, D), BF16), pltpu.VMEM((bm, D), BF16),
                        pltpu.SMEM((1,), jnp.int32), pltpu.SemaphoreType.DMA((2, 3))],
    )
    return pl.pallas_call(
        _expert_kernel,
        grid_spec=grid_spec,
        out_shape=jax.ShapeDtypeStruct(buf.shape, BF16),
        compiler_params=pltpu.CompilerParams(dimension_semantics=("arbitrary",), vmem_limit_bytes=VMEM_LIMIT),
        name="experts",
    )(block_expert, next_expert, n_used, buf, buf, buf, w_gate, w_up, w_down)


def _final_kernel(dcur_ref, dnxt_ref, x_ref, info_ref, y_ref, p_ref, wpg_ref, bpg_ref, wpp_ref,
                  l2g_ref, l2b_ref, l3g_ref, l3b_ref, o_ref, rows_scr, sem):
    i = pl.program_id(0)
    last = pl.num_programs(0) - 1
    rows = x_ref.shape[0]
    slot = i % 2

    def row_copy(dref, s, r, kk):
        return _tile_copy(y_ref, dref[0, 0, kk * rows + r], rows_scr.at[s, kk], r * TOKEN_ROWS, sem.at[s])

    def landed(s):
        pltpu.make_async_copy(rows_scr.at[s], rows_scr.at[s], sem.at[s]).wait()

    @pl.when(i == 0)
    def _():
        def start(c, carry):
            for u in range(MOVE_UNROLL):
                for kk in range(TOP_K):
                    row_copy(dcur_ref, 0, c * MOVE_UNROLL + u, kk).start(priority=kk)
            return carry

        lax.fori_loop(0, rows // MOVE_UNROLL, start, 0)

    landed(slot)
    info = info_ref[...]
    gate0 = info[:, I_G0:I_G0 + 1]
    gate1 = info[:, I_G1:I_G1 + 1]
    moe = (_from_token_tiles(rows_scr.at[slot, 0], rows).astype(F32) * gate0
           + _from_token_tiles(rows_scr.at[slot, 1], rows).astype(F32) * gate1)

    for r in range(rows):
        for kk in range(TOP_K):
            row_copy(dnxt_ref, 1 - slot, r, kk).start(priority=kk)

    pp = _dot(p_ref[...].astype(BF16), wpp_ref[...])
    x2 = _ln(ALPHA * x_ref[...] + moe, l2g_ref[...], l2b_ref[...])
    gate = jax.nn.sigmoid(_dot(x2.astype(BF16), wpg_ref[...]) + bpg_ref[...])
    o_ref[...] = _ln(ALPHA * x2 + gate * pp, l3g_ref[...], l3b_ref[...])

    @pl.when(i == last)
    def _():
        landed(1 - slot)


def _final(dest3, x1, info, y, p2, w):
    T, D = x1.shape
    tc = MOVE_ROWS
    pd = p2.shape[1]
    full = lambda a: pl.BlockSpec(a.shape, lambda i: (0,) * a.ndim)
    consts = [w["wpg"], w["bpg"], w["wpp"], w["l2g"], w["l2b"], w["l3g"], w["l3b"]]
    last = T // tc - 1
    return pl.pallas_call(
        _final_kernel,
        grid=(T // tc,),
        in_specs=[pl.BlockSpec((1, 1, TOP_K * tc), lambda i: (i, 0, 0), memory_space=pltpu.SMEM),
                  pl.BlockSpec((1, 1, TOP_K * tc), lambda i: (jnp.minimum(i + 1, last), 0, 0), memory_space=pltpu.SMEM),
                  pl.BlockSpec((tc, D), lambda i: (i, 0)),
                  pl.BlockSpec((tc, LANES), lambda i: (i, 0)),
                  pl.BlockSpec(memory_space=pl.ANY),
                  pl.BlockSpec((tc, pd), lambda i: (i, 0))] + [full(a) for a in consts],
        out_specs=pl.BlockSpec((tc, D), lambda i: (i, 0)),
        out_shape=jax.ShapeDtypeStruct((T, D), F32),
        scratch_shapes=[pltpu.VMEM((2, TOP_K, tc * TOKEN_ROWS, LANES), BF16), pltpu.SemaphoreType.DMA((2,))],
        compiler_params=pltpu.CompilerParams(dimension_semantics=("arbitrary",), vmem_limit_bytes=VMEM_LIMIT),
        name="final",
    )(dest3, dest3, x1, info, y, p2, *consts)


def _pad_heads(a, width):
    lead = a.shape[:-1]
    a = a.reshape(lead + (MLA_HEADS, width))
    a = jnp.pad(a, [(0, 0)] * len(lead) + [(0, 0), (0, LANES - width)])
    return a.reshape(lead + (HP,))


def _layer_weights(w_in, q_norm_g, w_q_up, kv_norm_g, w_kv_up, gm_ln_g, gm_ln_b, gm_w_s, gm_b_s,
                   mla_out_g, gm_out_g, w_o, ln1_g, ln1_b):
    D = w_in.shape[0]
    half = QK_ROPE // 2
    c1, c2, c3 = Q_RANK, Q_RANK + KV_RANK, Q_RANK + KV_RANK + QK_ROPE
    zeros = lambda *s: jnp.zeros(s, F32)
    kr = jnp.concatenate([zeros(D, QK_NOPE), w_in[:, c2:c3], zeros(D, LANES - QK_NOPE - QK_ROPE)], axis=1)
    win = jnp.concatenate([w_in[:, :c2], kr, w_in[:, c3:]], axis=1).astype(BF16)
    wq = _pad_heads(w_q_up, QK_NOPE + QK_ROPE).astype(BF16)

    wkv3 = w_kv_up.reshape(KV_RANK, MLA_HEADS, QK_NOPE + V_HEAD)
    wk = _pad_heads(wkv3[..., :QK_NOPE].reshape(KV_RANK, -1), QK_NOPE).astype(BF16)
    wv = wkv3[..., QK_NOPE:].reshape(KV_RANK, -1).astype(BF16)

    inv = (ROPE_THETA ** (-jnp.arange(0, QK_ROPE, 2, dtype=F32) / QK_ROPE))[:, None]
    eye = jnp.eye(half, dtype=F32)
    first = jnp.pad(eye, ((0, 0), (QK_NOPE, LANES - QK_NOPE - half)))
    second = jnp.pad(eye, ((0, 0), (QK_NOPE + half, LANES - QK_NOPE - QK_ROPE)))
    zero = jnp.zeros_like(first)
    cos_rows = jnp.concatenate([first + second, zero, zero], axis=1)
    sin_rows = jnp.concatenate([zero, -first, second], axis=1)
    rope = jnp.concatenate([cos_rows, cos_rows, sin_rows, sin_rows], axis=0).astype(BF16)
    lane = jnp.arange(LANES)
    one = jnp.where((lane >= QK_NOPE) & (lane < QK_NOPE + QK_ROPE), 0.0, 1.0)[None, :]

    grp = jnp.arange(GM_OUT) // GM_CH
    gavg = jnp.where(grp[:, None] == grp[None, :], 1.0 / GM_CH, 0.0).astype(BF16)
    bias = jnp.repeat(gm_b_s.T, GM_CH, axis=1)

    woa = w_o[:MLA_OUT].astype(BF16)
    wog = w_o[MLA_OUT:].astype(BF16)
    return dict(win=win, qg=q_norm_g[None, :], wq=wq, kvg=kv_norm_g[None, :], wk=wk, wv=wv, inv=inv, rope=rope, one=one,
                lng=gm_ln_g[None, :], lnb=gm_ln_b[None, :], gavg=gavg, ws=gm_w_s, bias=bias, gog=gm_out_g[None, :],
                woa=woa, wog=wog, mog=mla_out_g[:, None], l1g=ln1_g[None, :], l1b=ln1_b[None, :])


def _moe(x1, w_rg, b_rg, w_re, b_re, w_gate, w_up, w_down):
    T, D = x1.shape
    pad = jnp.zeros((D, LANES - N_GROUPS - N_EXPERTS), F32)
    wr = jnp.concatenate([w_rg, w_re, pad], axis=1)
    br = jnp.concatenate([b_rg, b_re, pad[0]])[None, :]
    info, info_t, cnt = _route(x1, wr, br)

    bm = EXPERT_ROWS
    n_blocks = (T * TOP_K) // bm + N_EXPERTS
    counts = cnt[0, R_OFF:R_OFF + N_EXPERTS].astype(jnp.int32)
    padded = (counts + bm - 1) // bm * bm
    pad_ends = jnp.cumsum(padded)
    pad_starts = pad_ends - padded
    def dest_rows(e_lane, r_lane):
        e = info_t[:, e_lane, :].astype(jnp.int32)
        ids = jnp.arange(N_EXPERTS)[:, None, None]
        seg_start = jnp.sum(jnp.where(e[None] == ids, pad_starts[:, None, None], 0), axis=0)
        return ((seg_start + info_t[:, r_lane, :].astype(jnp.int32)) * TOKEN_ROWS).reshape(T // MOVE_ROWS, MOVE_ROWS)

    dest = jnp.concatenate([dest_rows(I_E0, I_R0), dest_rows(I_E1, I_R1)], axis=1)[:, None, :]
    block_start = jnp.arange(n_blocks, dtype=jnp.int32) * bm
    block_expert = jnp.minimum(jnp.sum(pad_ends[None, :] <= block_start[:, None], axis=1),
                               N_EXPERTS - 1).astype(jnp.int32)

    blk = jnp.arange(n_blocks)
    later = (blk[None, :] > blk[:, None]) & (block_expert[None, :] != block_expert[:, None])
    next_expert = jnp.min(jnp.where(later, block_expert[None, :], N_EXPERTS), axis=1)
    next_expert = jnp.where(next_expert == N_EXPERTS, -1, next_expert).astype(jnp.int32)
    n_used = (pad_ends[-1:] // bm).astype(jnp.int32)

    seg = jnp.stack([pad_ends, padded, jnp.broadcast_to(n_used, (N_EXPERTS,))]).astype(jnp.int32)
    buf = _dispatch(seg, dest, x1, n_blocks * bm)
    y = _experts(block_expert, next_expert, n_used, buf, w_gate, w_up, w_down)
    return info, dest, y


def kernel(x, p, positions, w_in, q_norm_g, w_q_up, kv_norm_g, w_kv_up, gm_ln_g, gm_ln_b, gm_w_s, gm_b_s, mla_out_g, gm_out_g, w_o, ln1_g, ln1_b, w_rg, b_rg, w_re, b_re, w_gate, w_up, w_down, ln2_g, ln2_b, w_pg, b_pg, w_pp, ln3_g, ln3_b):
    B, S, D = x.shape
    T = B * S
    assert S % PREP_ROWS == 0 and PREP_ROWS % ATTN_ROWS == 0 and PREP_ROWS % CHUNK == 0
    assert T % ROUTE_ROWS == 0 and T % MOVE_ROWS == 0 and (T * TOP_K) % EXPERT_ROWS == 0
    assert D == TOKEN_ROWS * LANES and MOVE_ROWS % MOVE_UNROLL == 0
    pos4 = positions.reshape(B, S // PREP_ROWS, 1, PREP_ROWS)
    for i in range(DEPTH):
        w = _layer_weights(w_in[i], q_norm_g[i], w_q_up[i], kv_norm_g[i], w_kv_up[i], gm_ln_g[i], gm_ln_b[i],
                           gm_w_s[i], gm_b_s[i], mla_out_g[i], gm_out_g[i], w_o[i], ln1_g[i], ln1_b[i])
        q, k, vt, g = _prep(x, pos4, w)
        x1 = _attn(q, k, vt, g, x, w).reshape(T, D)
        info, dest, y = _moe(x1, w_rg[i], b_rg[i], w_re[i], b_re[i], w_gate[i], w_up[i], w_down[i])
        wf = dict(wpg=w_pg[i].astype(BF16), bpg=b_pg[i][None, :], wpp=w_pp[i].astype(BF16),
                  l2g=ln2_g[i][None, :], l2b=ln2_b[i][None, :], l3g=ln3_g[i][None, :], l3b=ln3_b[i][None, :])
        x = _final(dest, x1, info, y, p[i].reshape(T, -1), wf).reshape(B, S, D)
    return x
```

```python
import functools

import jax
import jax.numpy as jnp
from jax import lax
from jax.experimental import pallas as pl
from jax.experimental.pallas import tpu as pltpu

F32 = jnp.float32
BF16 = jnp.bfloat16

MLA_HEADS = 8
QK_NOPE = 64
QK_ROPE = 32
V_HEAD = 64
Q_RANK = 256
KV_RANK = 128
ROPE_THETA = 10000.0
MLA_OUT = MLA_HEADS * V_HEAD
GM_GROUPS = 8
GM_CH = 64
GM_OUT = GM_GROUPS * GM_CH
CHUNK = 128
N_GROUPS = 4
EXP_PER_GROUP = 8
N_EXPERTS = N_GROUPS * EXP_PER_GROUP
TOP_K = 2
EPS = 1e-6
DEPTH = 1
ALPHA = (2.0 * DEPTH) ** 0.25
SM_SCALE = (QK_NOPE + QK_ROPE) ** -0.5
LOG2E = 1.4426950408889634

LANES = 128
SUBLANES = 8
TOKEN_ROWS = 8
ONES_ROWS = 16
VMEM_LIMIT = 56 * 1024 * 1024

PREP_ROWS = 512
ATTN_ROWS = 256
ROUTE_ROWS = 512
ROUTE_SUB = 512
MOVE_ROWS = 256
MOVE_UNROLL = 8
EXPERT_ROWS = 256

C_Q = 0
C_KV = C_Q + Q_RANK
C_KR = C_KV + KV_RANK
C_U = C_KR + LANES
C_V = C_U + GM_OUT
C_END = C_V + GM_OUT
HP = MLA_HEADS * LANES

I_E0, I_E1, I_R0, I_R1, I_G0, I_G1 = range(6)
R_OFF = N_GROUPS


def _rms(v, g):
    return v * lax.rsqrt(jnp.mean(v * v, axis=-1, keepdims=True) + EPS) * g


def _ln(v, g, b):
    mu = jnp.mean(v, axis=-1, keepdims=True)
    d = v - mu
    var = jnp.mean(d * d, axis=-1, keepdims=True)
    return d * lax.rsqrt(var + EPS) * g + b


def _dot(a, b):
    return jnp.dot(a, b, preferred_element_type=F32)


def _prep_kernel(x_ref, pos_ref, win_ref, qg_ref, wq_ref, kvg_ref, wk_ref, wv_ref, inv_ref, rope_ref, one_ref,
                 lng_ref, lnb_ref, gavg_ref, ws_ref, bias_ref, gog_ref,
                 q_ref, k_ref, vt_ref, g_ref):
    rows = x_ref.shape[1]
    h = _dot(x_ref[0].astype(BF16), win_ref[...])

    ang = inv_ref[...] * pos_ref[0, 0].astype(F32)
    parts = []
    for t in (jnp.cos(ang), jnp.sin(ang)):
        hi = t.astype(BF16).astype(F32)
        parts += [hi, t - hi]
    tabs = _dot(jnp.concatenate(parts, axis=0).T.astype(BF16), rope_ref[...])
    cos_t = tabs[:, :LANES] + one_ref[...]
    sin_a = tabs[:, LANES:2 * LANES]
    sin_b = tabs[:, 2 * LANES:]
    half = QK_ROPE // 2

    def rotate(v):
        return v * cos_t + pltpu.roll(v, LANES - half, 1) * sin_a + pltpu.roll(v, half, 1) * sin_b

    cq = _rms(h[:, C_Q:C_Q + Q_RANK], qg_ref[...]).astype(BF16)
    q2 = _dot(cq, wq_ref[...])
    for hd in range(MLA_HEADS):
        lo = hd * LANES
        q_ref[0, :, lo:lo + LANES] = (rotate(q2[:, lo:lo + LANES]) * (SM_SCALE * LOG2E)).astype(BF16)

    ckv = _rms(h[:, C_KV:C_KV + KV_RANK], kvg_ref[...]).astype(BF16)
    kp = _dot(ckv, wk_ref[...])
    kr = rotate(h[:, C_KR:C_KR + LANES])
    for hd in range(MLA_HEADS):
        lo = hd * LANES
        k_ref[0, :, lo:lo + LANES] = (kp[:, lo:lo + LANES] + kr).astype(BF16)
    vp = _dot(ckv, wv_ref[...])
    for kb in range(rows // ATTN_ROWS):
        vt_ref[0, kb] = vp[kb * ATTN_ROWS:(kb + 1) * ATTN_ROWS].T.astype(BF16)

    u = jax.nn.gelu(h[:, C_U:C_U + GM_OUT])
    vv = jax.nn.gelu(h[:, C_V:C_V + GM_OUT])
    mu = _dot(vv.astype(BF16), gavg_ref[...])
    d = vv - mu
    var = _dot((d * d).astype(BF16), gavg_ref[...])
    vn = (d * lax.rsqrt(var + EPS) * lng_ref[...] + lnb_ref[...]).astype(BF16)

    tri = lax.broadcasted_iota(jnp.int32, (CHUNK, CHUNK), 0) >= lax.broadcasted_iota(jnp.int32, (CHUNK, CHUNK), 1)
    wm = [jnp.where(tri, ws_ref[g], 0.0).astype(BF16) for g in range(GM_GROUPS)]
    low_half = lax.broadcasted_iota(jnp.int32, (CHUNK, LANES), 1) < GM_CH
    for c in range(rows // CHUNK):
        r0 = c * CHUNK
        parts = []
        for pr in range(GM_GROUPS // 2):
            tile = vn[r0:r0 + CHUNK, pr * LANES:(pr + 1) * LANES]
            parts.append(jnp.where(low_half, _dot(wm[2 * pr], tile), _dot(wm[2 * pr + 1], tile)))
        sg = jnp.concatenate(parts, axis=1) + bias_ref[...]
        gm = u[r0:r0 + CHUNK] * sg
        g_ref[0, r0:r0 + CHUNK, :] = _rms(gm, gog_ref[...]).astype(BF16)


def _prep(x, pos4, w):
    B, S, D = x.shape
    ts = PREP_ROWS
    full = lambda a: pl.BlockSpec(a.shape, lambda b, i: (0,) * a.ndim)
    consts = [w["win"], w["qg"], w["wq"], w["kvg"], w["wk"], w["wv"], w["inv"], w["rope"], w["one"],
              w["lng"], w["lnb"], w["gavg"], w["ws"], w["bias"], w["gog"]]
    return pl.pallas_call(
        _prep_kernel,
        grid=(B, S // ts),
        in_specs=[pl.BlockSpec((1, ts, D), lambda b, i: (b, i, 0)),
                  pl.BlockSpec((1, 1, 1, ts), lambda b, i: (b, i, 0, 0))] + [full(a) for a in consts],
        out_specs=[pl.BlockSpec((1, ts, HP), lambda b, i: (b, i, 0)),
                   pl.BlockSpec((1, ts, HP), lambda b, i: (b, i, 0)),
                   pl.BlockSpec((1, ts // ATTN_ROWS, MLA_OUT, ATTN_ROWS), lambda b, i: (b, i, 0, 0)),
                   pl.BlockSpec((1, ts, GM_OUT), lambda b, i: (b, i, 0))],
        out_shape=[jax.ShapeDtypeStruct((B, S, HP), BF16)] * 2
        + [jax.ShapeDtypeStruct((B, S // ATTN_ROWS, MLA_OUT, ATTN_ROWS), BF16),
           jax.ShapeDtypeStruct((B, S, GM_OUT), BF16)],
        compiler_params=pltpu.CompilerParams(dimension_semantics=("parallel", "parallel"),
                                             vmem_limit_bytes=VMEM_LIMIT),
        name="prep",
    )(x, pos4, *consts)


def _attn_kernel(q_ref, k_ref, vt_ref, g_ref, x_ref, woa_ref, wog_ref, mog_ref, l1g_ref, l1b_ref,
                 o_ref, m_scr, acc_scr, sa_scr, sb_scr):
    i = pl.program_id(1)
    tq = q_ref.shape[1]
    tk = tq
    key = lax.broadcasted_iota(jnp.int32, (tk, tq), 0)
    qry = lax.broadcasted_iota(jnp.int32, (tk, tq), 1)
    diag_mask = key <= qry

    m_scr[...] = jnp.full(m_scr.shape, -1e30, F32)
    acc_scr[...] = jnp.zeros(acc_scr.shape, F32)
    ones = jnp.ones((ONES_ROWS, tk), BF16)

    def scores(j, s_scr):
        k0 = pl.multiple_of(j * tk, tk)
        for hd in range(MLA_HEADS):
            lo = hd * LANES
            qh = q_ref[0, :, lo:lo + LANES]
            kj = k_ref[0, pl.ds(k0, tk), lo:lo + LANES]
            s_scr[hd] = lax.dot_general(kj, qh, (((1,), (1,)), ((), ())), preferred_element_type=F32)

    def update(j, s_scr, masked):
        for hd in range(MLA_HEADS):
            s = s_scr[hd]
            vt = vt_ref[0, j, hd * V_HEAD:(hd + 1) * V_HEAD, :]
            if masked:
                s = jnp.where(diag_mask, s, -1e30)
            m_prev = m_scr[hd]
            m_new = jnp.maximum(m_prev, jnp.max(s, axis=0, keepdims=True))
            p = jnp.exp2(s - m_new).astype(BF16)
            scale = jnp.exp2(m_prev - m_new)
            acc_scr[hd] = scale * acc_scr[hd] + _dot(jnp.concatenate([vt, ones], axis=0), p)
            m_scr[hd] = m_new

    def pair(jj, c):
        j = 2 * jj
        scores(j + 1, sb_scr)
        update(j, sa_scr, False)
        scores(j + 2, sa_scr)
        update(j + 1, sb_scr, False)
        return c

    scores(0, sa_scr)
    lax.fori_loop(0, lax.shift_right_logical(i, 1), pair, 0)

    @pl.when((i & 1) == 0)
    def _():
        update(i, sa_scr, True)

    @pl.when((i & 1) == 1)
    def _():
        scores(i, sb_scr)
        update(i - 1, sa_scr, False)
        update(i, sb_scr, True)

    at = jnp.concatenate([acc_scr[hd, :V_HEAD] / acc_scr[hd, V_HEAD:V_HEAD + 1] for hd in range(MLA_HEADS)],
                         axis=0)
    at = at * lax.rsqrt(jnp.mean(at * at, axis=0, keepdims=True) + EPS) * mog_ref[...]
    mix = _dot(at.T.astype(BF16), woa_ref[...]) + _dot(g_ref[0], wog_ref[...])
    o_ref[0] = _ln(ALPHA * x_ref[0] + mix, l1g_ref[...], l1b_ref[...])


def _attn(q, k, vt, g, x, w):
    B, S, D = x.shape
    tq = ATTN_ROWS
    full = lambda a: pl.BlockSpec(a.shape, lambda b, i: (0,) * a.ndim)
    consts = [w["woa"], w["wog"], w["mog"], w["l1g"], w["l1b"]]
    return pl.pallas_call(
        _attn_kernel,
        grid=(B, S // tq),
        in_specs=[pl.BlockSpec((1, tq, HP), lambda b, i: (b, i, 0)),
                  pl.BlockSpec((1, S, HP), lambda b, i: (b, 0, 0)),
                  pl.BlockSpec((1,) + vt.shape[1:], lambda b, i: (b, 0, 0, 0)),
                  pl.BlockSpec((1, tq, GM_OUT), lambda b, i: (b, i, 0)),
                  pl.BlockSpec((1, tq, D), lambda b, i: (b, i, 0))] + [full(a) for a in consts],
        out_specs=pl.BlockSpec((1, tq, D), lambda b, i: (b, i, 0)),
        out_shape=jax.ShapeDtypeStruct((B, S, D), F32),
        scratch_shapes=[pltpu.VMEM((MLA_HEADS, 1, tq), F32),
                        pltpu.VMEM((MLA_HEADS, V_HEAD + ONES_ROWS, tq), F32),
                        pltpu.VMEM((MLA_HEADS, tq, tq), F32), pltpu.VMEM((MLA_HEADS, tq, tq), F32)],
        compiler_params=pltpu.CompilerParams(dimension_semantics=("parallel", "parallel"),
                                             vmem_limit_bytes=VMEM_LIMIT),
        name="attn",
    )(q, k, vt, g, x, *consts)


def _route_kernel(x_ref, wr_ref, br_ref, info_ref, infot_ref, cnt_ref, carry_scr, tri_scr):
    step = pl.program_id(0)
    sub = tri_scr.shape[0]

    @pl.when(step == 0)
    def _():
        carry_scr[...] = jnp.zeros_like(carry_scr)
        r = lax.broadcasted_iota(jnp.int32, (sub, sub), 0)
        c = lax.broadcasted_iota(jnp.int32, (sub, sub), 1)
        tri_scr[...] = jnp.where(c < r, 1.0, 0.0).astype(BF16)

    wr = wr_ref[...]
    wh = wr.astype(BF16)
    wl = (wr - wh.astype(F32)).astype(BF16)
    lane = lax.broadcasted_iota(jnp.int32, (sub, LANES), 1)
    neg = jnp.float32(-jnp.inf)
    carry = carry_scr[...]

    for h in range(x_ref.shape[0] // sub):
        r0_, r1_ = h * sub, (h + 1) * sub
        x = x_ref[r0_:r1_, :]
        xh = x.astype(BF16)
        xl = (x - xh.astype(F32)).astype(BF16)
        logits = _dot(xh, wh) + _dot(xl, wh) + _dot(xh, wl) + br_ref[...]

        is_g = lane < N_GROUPS
        lg = jnp.where(is_g, logits, neg)
        gmax = jnp.max(lg, axis=-1, keepdims=True)
        g_idx = jnp.min(jnp.where(lg == gmax, lane, LANES), axis=-1, keepdims=True)
        g_den = jnp.sum(jnp.where(is_g, jnp.exp(lg - gmax), 0.0), axis=-1, keepdims=True)
        g_p = 1.0 / g_den

        in_grp = (lane >= R_OFF) & (lane < R_OFF + N_EXPERTS) & (((lane - R_OFF) >> 3) == g_idx)
        le = jnp.where(in_grp, logits, neg)
        m1 = jnp.max(le, axis=-1, keepdims=True)
        i1 = jnp.min(jnp.where(le == m1, lane, LANES), axis=-1, keepdims=True)
        le2 = jnp.where(lane == i1, neg, le)
        m2 = jnp.max(le2, axis=-1, keepdims=True)
        i2 = jnp.min(jnp.where(le2 == m2, lane, LANES), axis=-1, keepdims=True)
        e2 = jnp.exp(m2 - m1)
        gate0 = g_p / (1.0 + e2)
        gate1 = g_p * e2 / (1.0 + e2)

        hit1 = lane == i1
        hit2 = lane == i2
        onehot = jnp.where(hit1 | hit2, 1.0, 0.0)
        before = _dot(tri_scr[...], onehot.astype(BF16)) + carry
        rank0 = jnp.sum(jnp.where(hit1, before, 0.0), axis=-1, keepdims=True)
        rank1 = jnp.sum(jnp.where(hit2, before, 0.0), axis=-1, keepdims=True)
        carry = carry + jnp.sum(onehot, axis=0, keepdims=True)

        info = jnp.where(lane == I_E0, (i1 - R_OFF).astype(F32), 0.0)
        info = jnp.where(lane == I_E1, (i2 - R_OFF).astype(F32), info)
        info = jnp.where(lane == I_R0, rank0, info)
        info = jnp.where(lane == I_R1, rank1, info)
        info = jnp.where(lane == I_G0, gate0, info)
        info = jnp.where(lane == I_G1, gate1, info)
        info_ref[r0_:r1_, :] = info
        infot_ref[0, :, r0_:r1_] = info.T[:SUBLANES]

    carry_scr[...] = carry
    cnt_ref[...] = carry


def _route(x1, wr, br):
    T, D = x1.shape
    tt = ROUTE_ROWS
    return pl.pallas_call(
        _route_kernel,
        grid=(T // tt,),
        in_specs=[pl.BlockSpec((tt, D), lambda i: (i, 0)),
                  pl.BlockSpec(wr.shape, lambda i: (0, 0)),
                  pl.BlockSpec(br.shape, lambda i: (0, 0))],
        out_specs=[pl.BlockSpec((tt, LANES), lambda i: (i, 0)),
                   pl.BlockSpec((1, SUBLANES, tt), lambda i: (i, 0, 0)),
                   pl.BlockSpec((1, LANES), lambda i: (0, 0))],
        out_shape=[jax.ShapeDtypeStruct((T, LANES), F32), jax.ShapeDtypeStruct((T // tt, SUBLANES, tt), F32),
                   jax.ShapeDtypeStruct((1, LANES), F32)],
        scratch_shapes=[pltpu.VMEM((1, LANES), F32), pltpu.VMEM((ROUTE_SUB, ROUTE_SUB), BF16)],
        compiler_params=pltpu.CompilerParams(dimension_semantics=("arbitrary",), vmem_limit_bytes=VMEM_LIMIT),
        name="route",
    )(x1, wr, br)


def _to_token_tiles(dst_ref, val):
    dst_ref[...] = val.astype(BF16).reshape(dst_ref.shape)


def _from_token_tiles(src_ref, rows):
    return src_ref[...].reshape(rows, TOKEN_ROWS * LANES)


def _to_token_tiles_f32(dst_ref, val):
    rows = val.shape[0]
    for c in range(TOKEN_ROWS):
        dst_ref[pl.ds(c, rows, stride=TOKEN_ROWS), :] = val[:, c * LANES:(c + 1) * LANES]


def _from_token_tiles_f32(src_ref, rows):
    return jnp.concatenate([src_ref[pl.ds(c, rows, stride=TOKEN_ROWS), :] for c in range(TOKEN_ROWS)], axis=1)


def _tile_copy(src_ref, src_row, dst_ref, dst_row, sem):
    return pltpu.make_async_copy(src_ref.at[pl.ds(pl.multiple_of(src_row, TOKEN_ROWS), TOKEN_ROWS)],
                                 dst_ref.at[pl.ds(pl.multiple_of(dst_row, TOKEN_ROWS), TOKEN_ROWS)], sem)


def _dispatch_kernel(seg_ref, dest_ref, x_ref, buf_ref, stage_scr, zero_scr, sem, zero_sem, *, n_steps):
    i = pl.program_id(0)
    rows = x_ref.shape[0]
    slot = i % 2

    @pl.when(i == 0)
    def _():
        zero_scr[...] = jnp.zeros(zero_scr.shape, BF16)

        block = EXPERT_ROWS * TOKEN_ROWS
        n_blocks = buf_ref.shape[0] // block

        def clear_rows(first):
            return pltpu.make_async_copy(zero_scr, buf_ref.at[pl.ds(pl.multiple_of(first, SUBLANES), block)], zero_sem)

        def clear(e):
            return clear_rows((seg_ref[0, e] - EXPERT_ROWS) * TOKEN_ROWS)

        def start_tail(b, c):
            clear_rows(b * block).start()
            return c

        def wait_tail(b, c):
            clear_rows(b * block).wait()
            return c

        for e in range(N_EXPERTS):
            pl.when(seg_ref[1, e] > 0)(lambda e=e: clear(e).start())
        lax.fori_loop(seg_ref[2, 0], n_blocks, start_tail, 0)
        for e in range(N_EXPERTS):
            pl.when(seg_ref[1, e] > 0)(lambda e=e: clear(e).wait())
        lax.fori_loop(seg_ref[2, 0], n_blocks, wait_tail, 0)

    def drain(s):
        for _ in range(TOP_K):
            pltpu.make_async_copy(stage_scr.at[s], stage_scr.at[s], sem.at[s]).wait()

    @pl.when(i >= 2)
    def _():
        drain(slot)

    _to_token_tiles(stage_scr.at[slot], x_ref[...])

    def start(c, carry):
        for u in range(MOVE_UNROLL):
            r = c * MOVE_UNROLL + u
            for kk in range(TOP_K):
                _tile_copy(stage_scr.at[slot], r * TOKEN_ROWS, buf_ref, dest_ref[0, 0, kk * rows + r],
                           sem.at[slot]).start(priority=kk)
        return carry

    lax.fori_loop(0, rows // MOVE_UNROLL, start, 0)

    @pl.when(i == n_steps - 1)
    def _():
        drain(slot)
        if n_steps >= 2:
            drain(1 - slot)


def _dispatch(seg, dest3, x1, n_rows):
    T, D = x1.shape
    td = MOVE_ROWS
    n_steps = T // td
    grid_spec = pltpu.PrefetchScalarGridSpec(
        num_scalar_prefetch=1,
        grid=(n_steps,),
        in_specs=[pl.BlockSpec((1, 1, TOP_K * td), lambda i, seg: (i, 0, 0), memory_space=pltpu.SMEM),
                  pl.BlockSpec((td, D), lambda i, seg: (i, 0))],
        out_specs=pl.BlockSpec(memory_space=pl.ANY),
        scratch_shapes=[pltpu.VMEM((2, td * TOKEN_ROWS, LANES), BF16),
                        pltpu.VMEM((EXPERT_ROWS * TOKEN_ROWS, LANES), BF16),
                        pltpu.SemaphoreType.DMA((2,)), pltpu.SemaphoreType.DMA(())],
    )
    return pl.pallas_call(
        functools.partial(_dispatch_kernel, n_steps=n_steps),
        grid_spec=grid_spec,
        out_shape=jax.ShapeDtypeStruct((n_rows * TOKEN_ROWS, LANES), BF16),
        compiler_params=pltpu.CompilerParams(dimension_semantics=("arbitrary",), vmem_limit_bytes=VMEM_LIMIT),
        name="dispatch",
    )(seg, dest3, x1)


def _expert_kernel(be_ref, ne_ref, nu_ref, buf0_ref, bufa_ref, bufb_ref, wg_hbm, wu_hbm, wd_hbm, y_ref,
                   sg_scr, su_scr, sd_scr, wg_scr, wu_scr, wd_scr, xa_scr, xb_scr, cur_ref, sem):
    step = pl.program_id(0)
    bm = EXPERT_ROWS
    half = bm * TOKEN_ROWS

    def fetch(expert, s):
        return (pltpu.make_async_copy(wg_hbm.at[expert], sg_scr.at[s], sem.at[s, 0]),
                pltpu.make_async_copy(wu_hbm.at[expert], su_scr.at[s], sem.at[s, 1]),
                pltpu.make_async_copy(wd_hbm.at[expert], sd_scr.at[s], sem.at[s, 2]))

    @pl.when(step == 0)
    def _():
        cur_ref[0] = 0
        for c in fetch(be_ref[0], 0):
            c.start()
        xa_scr[...] = _from_token_tiles(buf0_ref, bm)

    def load_weights(blk):
        e = be_ref[blk]

        @pl.when((blk == 0) | (be_ref[jnp.maximum(blk - 1, 0)] != e))
        def _():
            s = cur_ref[0]
            for c in fetch(e, s):
                c.wait()
            wg_scr[...] = sg_scr[s].astype(BF16)
            wu_scr[...] = su_scr[s].astype(BF16)
            wd_scr[...] = sd_scr[s].astype(BF16)
            nxt = ne_ref[blk]

            @pl.when(nxt >= 0)
            def _():
                for c in fetch(nxt, 1 - s):
                    c.start()

            cur_ref[0] = 1 - s

    def run(blk, x_scr, nxt_ref, nxt_scr, out_rows):
        load_weights(blk)

        @pl.when(blk < nu_ref[0])
        def _():
            nxt_scr[...] = _from_token_tiles(nxt_ref, bm)
            xb = x_scr[...]
            hidden = jax.nn.silu(_dot(xb, wg_scr[...])) * _dot(xb, wu_scr[...])
            _to_token_tiles_f32(y_ref.at[out_rows], _dot(hidden.astype(BF16), wd_scr[...]))

        @pl.when(blk >= nu_ref[0])
        def _():
            y_ref[out_rows, :] = jnp.zeros((half, LANES), F32)

    run(2 * step, xa_scr, bufa_ref, xb_scr, pl.ds(0, half))
    run(2 * step + 1, xb_scr, bufb_ref, xa_scr, pl.ds(half, half))


def _experts(block_expert, next_expert, n_used, buf, w_gate, w_up, w_down):
    bm = EXPERT_ROWS
    D, ff = w_gate.shape[1:]
    n_blocks = buf.shape[0] // (bm * TOKEN_ROWS)
    assert n_blocks % 2 == 0
    last = n_blocks - 1
    grid_spec = pltpu.PrefetchScalarGridSpec(
        num_scalar_prefetch=3,
        grid=(n_blocks // 2,),
        in_specs=[pl.BlockSpec((bm * TOKEN_ROWS, LANES), lambda s, *_: (0, 0)),
                  pl.BlockSpec((bm * TOKEN_ROWS, LANES), lambda s, *_: (2 * s + 1, 0)),
                  pl.BlockSpec((bm * TOKEN_ROWS, LANES), lambda s, *_: (jnp.minimum(2 * s + 2, last), 0)),
                  pl.BlockSpec(memory_space=pl.ANY),
                  pl.BlockSpec(memory_space=pl.ANY),
                  pl.BlockSpec(memory_space=pl.ANY)],
        out_specs=pl.BlockSpec((2 * bm * TOKEN_ROWS, LANES), lambda s, *_: (s, 0)),
        scratch_shapes=[pltpu.VMEM((2, D, ff), F32), pltpu.VMEM((2, D, ff), F32), pltpu.VMEM((2, ff, D), F32),
                        pltpu.VMEM((D, ff), BF16), pltpu.VMEM((D, ff), BF16), pltpu.VMEM((ff, D), BF16),
                        pltpu.VMEM((bm, D), BF16), pltpu.VMEM((bm, D), BF16),
                        pltpu.SMEM((1,), jnp.int32), pltpu.SemaphoreType.DMA((2, 3))],
    )
    return pl.pallas_call(
        _expert_kernel,
        grid_spec=grid_spec,
        out_shape=jax.ShapeDtypeStruct(buf.shape, F32),
        compiler_params=pltpu.CompilerParams(dimension_semantics=("arbitrary",), vmem_limit_bytes=VMEM_LIMIT),
        name="experts",
    )(block_expert, next_expert, n_used, buf, buf, buf, w_gate, w_up, w_down)


def _final_kernel(dcur_ref, dnxt_ref, x_ref, info_ref, y_ref, p_ref, wpg_ref, bpg_ref, wpp_ref,
                  l2g_ref, l2b_ref, l3g_ref, l3b_ref, o_ref, rows_scr, sem):
    i = pl.program_id(0)
    last = pl.num_programs(0) - 1
    rows = x_ref.shape[0]
    slot = i % 2

    def row_copy(dref, s, r, kk):
        return _tile_copy(y_ref, dref[0, 0, kk * rows + r], rows_scr.at[s, kk], r * TOKEN_ROWS, sem.at[s])

    def landed(s):
        pltpu.make_async_copy(rows_scr.at[s], rows_scr.at[s], sem.at[s]).wait()

    @pl.when(i == 0)
    def _():
        def start(c, carry):
            for u in range(MOVE_UNROLL):
                for kk in range(TOP_K):
                    row_copy(dcur_ref, 0, c * MOVE_UNROLL + u, kk).start(priority=kk)
            return carry

        lax.fori_loop(0, rows // MOVE_UNROLL, start, 0)

    landed(slot)
    info = info_ref[...]
    gate0 = info[:, I_G0:I_G0 + 1]
    gate1 = info[:, I_G1:I_G1 + 1]
    moe = (_from_token_tiles_f32(rows_scr.at[slot, 0], rows) * gate0
           + _from_token_tiles_f32(rows_scr.at[slot, 1], rows) * gate1)

    for r in range(rows):
        for kk in range(TOP_K):
            row_copy(dnxt_ref, 1 - slot, r, kk).start(priority=kk)

    pp = _dot(p_ref[...].astype(BF16), wpp_ref[...])
    x2 = _ln(ALPHA * x_ref[...] + moe, l2g_ref[...], l2b_ref[...])
    gate = jax.nn.sigmoid(_dot(x2.astype(BF16), wpg_ref[...]) + bpg_ref[...])
    o_ref[...] = _ln(ALPHA * x2 + gate * pp, l3g_ref[...], l3b_ref[...])

    @pl.when(i == last)
    def _():
        landed(1 - slot)


def _final(dest3, x1, info, y, p2, w):
    T, D = x1.shape
    tc = MOVE_ROWS
    pd = p2.shape[1]
    full = lambda a: pl.BlockSpec(a.shape, lambda i: (0,) * a.ndim)
    consts = [w["wpg"], w["bpg"], w["wpp"], w["l2g"], w["l2b"], w["l3g"], w["l3b"]]
    last = T // tc - 1
    return pl.pallas_call(
        _final_kernel,
        grid=(T // tc,),
        in_specs=[pl.BlockSpec((1, 1, TOP_K * tc), lambda i: (i, 0, 0), memory_space=pltpu.SMEM),
                  pl.BlockSpec((1, 1, TOP_K * tc), lambda i: (jnp.minimum(i + 1, last), 0, 0), memory_space=pltpu.SMEM),
                  pl.BlockSpec((tc, D), lambda i: (i, 0)),
                  pl.BlockSpec((tc, LANES), lambda i: (i, 0)),
                  pl.BlockSpec(memory_space=pl.ANY),
                  pl.BlockSpec((tc, pd), lambda i: (i, 0))] + [full(a) for a in consts],
        out_specs=pl.BlockSpec((tc, D), lambda i: (i, 0)),
        out_shape=jax.ShapeDtypeStruct((T, D), F32),
        scratch_shapes=[pltpu.VMEM((2, TOP_K, tc * TOKEN_ROWS, LANES), F32), pltpu.SemaphoreType.DMA((2,))],
        compiler_params=pltpu.CompilerParams(dimension_semantics=("arbitrary",), vmem_limit_bytes=VMEM_LIMIT),
        name="final",
    )(dest3, dest3, x1, info, y, p2, *consts)


def _pad_heads(a, width):
    lead = a.shape[:-1]
    a = a.reshape(lead + (MLA_HEADS, width))
    a = jnp.pad(a, [(0, 0)] * len(lead) + [(0, 0), (0, LANES - width)])
    return a.reshape(lead + (HP,))


def _layer_weights(w_in, q_norm_g, w_q_up, kv_norm_g, w_kv_up, gm_ln_g, gm_ln_b, gm_w_s, gm_b_s,
                   mla_out_g, gm_out_g, w_o, ln1_g, ln1_b):
    D = w_in.shape[0]
    half = QK_ROPE // 2
    c1, c2, c3 = Q_RANK, Q_RANK + KV_RANK, Q_RANK + KV_RANK + QK_ROPE
    zeros = lambda *s: jnp.zeros(s, F32)
    kr = jnp.concatenate([zeros(D, QK_NOPE), w_in[:, c2:c3], zeros(D, LANES - QK_NOPE - QK_ROPE)], axis=1)
    win = jnp.concatenate([w_in[:, :c2], kr, w_in[:, c3:]], axis=1).astype(BF16)
    wq = _pad_heads(w_q_up, QK_NOPE + QK_ROPE).astype(BF16)

    wkv3 = w_kv_up.reshape(KV_RANK, MLA_HEADS, QK_NOPE + V_HEAD)
    wk = _pad_heads(wkv3[..., :QK_NOPE].reshape(KV_RANK, -1), QK_NOPE).astype(BF16)
    wv = wkv3[..., QK_NOPE:].reshape(KV_RANK, -1).astype(BF16)

    inv = (ROPE_THETA ** (-jnp.arange(0, QK_ROPE, 2, dtype=F32) / QK_ROPE))[:, None]
    eye = jnp.eye(half, dtype=F32)
    first = jnp.pad(eye, ((0, 0), (QK_NOPE, LANES - QK_NOPE - half)))
    second = jnp.pad(eye, ((0, 0), (QK_NOPE + half, LANES - QK_NOPE - QK_ROPE)))
    zero = jnp.zeros_like(first)
    cos_rows = jnp.concatenate([first + second, zero, zero], axis=1)
    sin_rows = jnp.concatenate([zero, -first, second], axis=1)
    rope = jnp.concatenate([cos_rows, cos_rows, sin_rows, sin_rows], axis=0).astype(BF16)
    lane = jnp.arange(LANES)
    one = jnp.where((lane >= QK_NOPE) & (lane < QK_NOPE + QK_ROPE), 0.0, 1.0)[None, :]

    grp = jnp.arange(GM_OUT) // GM_CH
    gavg = jnp.where(grp[:, None] == grp[None, :], 1.0 / GM_CH, 0.0).astype(BF16)
    bias = jnp.repeat(gm_b_s.T, GM_CH, axis=1)

    woa = w_o[:MLA_OUT].astype(BF16)
    wog = w_o[MLA_OUT:].astype(BF16)
    return dict(win=win, qg=q_norm_g[None, :], wq=wq, kvg=kv_norm_g[None, :], wk=wk, wv=wv, inv=inv, rope=rope, one=one,
                lng=gm_ln_g[None, :], lnb=gm_ln_b[None, :], gavg=gavg, ws=gm_w_s, bias=bias, gog=gm_out_g[None, :],
                woa=woa, wog=wog, mog=mla_out_g[:, None], l1g=ln1_g[None, :], l1b=ln1_b[None, :])


def _moe(x1, w_rg, b_rg, w_re, b_re, w_gate, w_up, w_down):
    T, D = x1.shape
    pad = jnp.zeros((D, LANES - N_GROUPS - N_EXPERTS), F32)
    wr = jnp.concatenate([w_rg, w_re, pad], axis=1)
    br = jnp.concatenate([b_rg, b_re, pad[0]])[None, :]
    info, info_t, cnt = _route(x1, wr, br)

    bm = EXPERT_ROWS
    n_blocks = (T * TOP_K) // bm + N_EXPERTS
    counts = cnt[0, R_OFF:R_OFF + N_EXPERTS].astype(jnp.int32)
    padded = (counts + bm - 1) // bm * bm
    pad_ends = jnp.cumsum(padded)
    pad_starts = pad_ends - padded
    def dest_rows(e_lane, r_lane):
        e = info_t[:, e_lane, :].astype(jnp.int32)
        ids = jnp.arange(N_EXPERTS)[:, None, None]
        seg_start = jnp.sum(jnp.where(e[None] == ids, pad_starts[:, None, None], 0), axis=0)
        return ((seg_start + info_t[:, r_lane, :].astype(jnp.int32)) * TOKEN_ROWS).reshape(T // MOVE_ROWS, MOVE_ROWS)

    dest = jnp.concatenate([dest_rows(I_E0, I_R0), dest_rows(I_E1, I_R1)], axis=1)[:, None, :]
    block_start = jnp.arange(n_blocks, dtype=jnp.int32) * bm
    block_expert = jnp.minimum(jnp.sum(pad_ends[None, :] <= block_start[:, None], axis=1),
                               N_EXPERTS - 1).astype(jnp.int32)

    blk = jnp.arange(n_blocks)
    later = (blk[None, :] > blk[:, None]) & (block_expert[None, :] != block_expert[:, None])
    next_expert = jnp.min(jnp.where(later, block_expert[None, :], N_EXPERTS), axis=1)
    next_expert = jnp.where(next_expert == N_EXPERTS, -1, next_expert).astype(jnp.int32)
    n_used = (pad_ends[-1:] // bm).astype(jnp.int32)

    seg = jnp.stack([pad_ends, padded, jnp.broadcast_to(n_used, (N_EXPERTS,))]).astype(jnp.int32)
    buf = _dispatch(seg, dest, x1, n_blocks * bm)
    y = _experts(block_expert, next_expert, n_used, buf, w_gate, w_up, w_down)
    return info, dest, y


def kernel(x, p, positions, w_in, q_norm_g, w_q_up, kv_norm_g, w_kv_up, gm_ln_g, gm_ln_b, gm_w_s, gm_b_s, mla_out_g, gm_out_g, w_o, ln1_g, ln1_b, w_rg, b_rg, w_re, b_re, w_gate, w_up, w_down, ln2_g, ln2_b, w_pg, b_pg, w_pp, ln3_g, ln3_b):
    B, S, D = x.shape
    T = B * S
    assert S % PREP_ROWS == 0 and PREP_ROWS % ATTN_ROWS == 0 and PREP_ROWS % CHUNK == 0
    assert T % ROUTE_ROWS == 0 and T % MOVE_ROWS == 0 and (T * TOP_K) % EXPERT_ROWS == 0
    assert D == TOKEN_ROWS * LANES and MOVE_ROWS % MOVE_UNROLL == 0
    pos4 = positions.reshape(B, S // PREP_ROWS, 1, PREP_ROWS)
    for i in range(DEPTH):
        w = _layer_weights(w_in[i], q_norm_g[i], w_q_up[i], kv_norm_g[i], w_kv_up[i], gm_ln_g[i], gm_ln_b[i],
                           gm_w_s[i], gm_b_s[i], mla_out_g[i], gm_out_g[i], w_o[i], ln1_g[i], ln1_b[i])
        q, k, vt, g = _prep(x, pos4, w)
        x1 = _attn(q, k, vt, g, x, w).reshape(T, D)
        info, dest, y = _moe(x1, w_rg[i], b_rg[i], w_re[i], b_re[i], w_gate[i], w_up[i], w_down[i])
        wf = dict(wpg=w_pg[i].astype(BF16), bpg=b_pg[i][None, :], wpp=w_pp[i].astype(BF16),
                  l2g=ln2_g[i][None, :], l2b=ln2_b[i][None, :], l3g=ln3_g[i][None, :], l3b=ln3_b[i][None, :])
        x = _final(dest, x1, info, y, p[i].reshape(T, -1), wf).reshape(B, S, D)
    return x
```

```python
import functools

import jax
import jax.numpy as jnp
from jax import lax
from jax.experimental import pallas as pl
from jax.experimental.pallas import tpu as pltpu

F32 = jnp.float32
BF16 = jnp.bfloat16

MLA_HEADS = 8
QK_NOPE = 64
QK_ROPE = 32
V_HEAD = 64
Q_RANK = 256
KV_RANK = 128
ROPE_THETA = 10000.0
MLA_OUT = MLA_HEADS * V_HEAD
GM_GROUPS = 8
GM_CH = 64
GM_OUT = GM_GROUPS * GM_CH
CHUNK = 128
N_GROUPS = 4
EXP_PER_GROUP = 8
N_EXPERTS = N_GROUPS * EXP_PER_GROUP
TOP_K = 2
EPS = 1e-6
DEPTH = 1
ALPHA = (2.0 * DEPTH) ** 0.25
SM_SCALE = (QK_NOPE + QK_ROPE) ** -0.5
LOG2E = 1.4426950408889634

LANES = 128
SUBLANES = 8
TOKEN_ROWS = 8
ONES_ROWS = 16
VMEM_LIMIT = 56 * 1024 * 1024

PREP_ROWS = 512
ATTN_ROWS = 256
ROUTE_ROWS = 512
ROUTE_SUB = 512
MOVE_ROWS = 256
MOVE_UNROLL = 8
EXPERT_ROWS = 256

C_Q = 0
C_KV = C_Q + Q_RANK
C_KR = C_KV + KV_RANK
C_U = C_KR + LANES
C_V = C_U + GM_OUT
C_END = C_V + GM_OUT
HP = MLA_HEADS * LANES

I_E0, I_E1, I_R0, I_R1, I_G0, I_G1 = range(6)
R_OFF = N_GROUPS


def _rms(v, g):
    return v * lax.rsqrt(jnp.mean(v * v, axis=-1, keepdims=True) + EPS) * g


def _ln(v, g, b):
    mu = jnp.mean(v, axis=-1, keepdims=True)
    d = v - mu
    var = jnp.mean(d * d, axis=-1, keepdims=True)
    return d * lax.rsqrt(var + EPS) * g + b


def _dot(a, b):
    return jnp.dot(a, b, preferred_element_type=F32)


def _prep_kernel(x_ref, pos_ref, win_ref, qg_ref, wq_ref, kvg_ref, wk_ref, wv_ref, inv_ref, rope_ref, one_ref,
                 lng_ref, lnb_ref, gavg_ref, ws_ref, bias_ref, gog_ref,
                 q_ref, k_ref, vt_ref, g_ref):
    rows = x_ref.shape[1]
    h = _dot(x_ref[0].astype(BF16), win_ref[...])

    ang = inv_ref[...] * pos_ref[0, 0].astype(F32)
    parts = []
    for t in (jnp.cos(ang), jnp.sin(ang)):
        hi = t.astype(BF16).astype(F32)
        parts += [hi, t - hi]
    tabs = _dot(jnp.concatenate(parts, axis=0).T.astype(BF16), rope_ref[...])
    cos_t = tabs[:, :LANES] + one_ref[...]
    sin_a = tabs[:, LANES:2 * LANES]
    sin_b = tabs[:, 2 * LANES:]
    half = QK_ROPE // 2

    def rotate(v):
        return v * cos_t + pltpu.roll(v, LANES - half, 1) * sin_a + pltpu.roll(v, half, 1) * sin_b

    cq = _rms(h[:, C_Q:C_Q + Q_RANK], qg_ref[...]).astype(BF16)
    q2 = _dot(cq, wq_ref[...])
    for hd in range(MLA_HEADS):
        lo = hd * LANES
        q_ref[0, :, lo:lo + LANES] = (rotate(q2[:, lo:lo + LANES]) * (SM_SCALE * LOG2E)).astype(BF16)

    ckv = _rms(h[:, C_KV:C_KV + KV_RANK], kvg_ref[...]).astype(BF16)
    kp = _dot(ckv, wk_ref[...])
    kr = rotate(h[:, C_KR:C_KR + LANES])
    for hd in range(MLA_HEADS):
        lo = hd * LANES
        k_ref[0, :, lo:lo + LANES] = (kp[:, lo:lo + LANES] + kr).astype(BF16)
    vp = _dot(ckv, wv_ref[...])
    for kb in range(rows // ATTN_ROWS):
        vt_ref[0, kb] = vp[kb * ATTN_ROWS:(kb + 1) * ATTN_ROWS].T.astype(BF16)

    u = jax.nn.gelu(h[:, C_U:C_U + GM_OUT])
    vv = jax.nn.gelu(h[:, C_V:C_V + GM_OUT])
    mu = _dot(vv.astype(BF16), gavg_ref[...])
    d = vv - mu
    var = _dot((d * d).astype(BF16), gavg_ref[...])
    vn = (d * lax.rsqrt(var + EPS) * lng_ref[...] + lnb_ref[...]).astype(BF16)

    tri = lax.broadcasted_iota(jnp.int32, (CHUNK, CHUNK), 0) >= lax.broadcasted_iota(jnp.int32, (CHUNK, CHUNK), 1)
    wm = [jnp.where(tri, ws_ref[g], 0.0).astype(BF16) for g in range(GM_GROUPS)]
    low_half = lax.broadcasted_iota(jnp.int32, (CHUNK, LANES), 1) < GM_CH
    for c in range(rows // CHUNK):
        r0 = c * CHUNK
        parts = []
        for pr in range(GM_GROUPS // 2):
            tile = vn[r0:r0 + CHUNK, pr * LANES:(pr + 1) * LANES]
            parts.append(jnp.where(low_half, _dot(wm[2 * pr], tile), _dot(wm[2 * pr + 1], tile)))
        sg = jnp.concatenate(parts, axis=1) + bias_ref[...]
        gm = u[r0:r0 + CHUNK] * sg
        g_ref[0, r0:r0 + CHUNK, :] = _rms(gm, gog_ref[...]).astype(BF16)


def _prep(x, pos4, w):
    B, S, D = x.shape
    ts = PREP_ROWS
    full = lambda a: pl.BlockSpec(a.shape, lambda b, i: (0,) * a.ndim)
    consts = [w["win"], w["qg"], w["wq"], w["kvg"], w["wk"], w["wv"], w["inv"], w["rope"], w["one"],
              w["lng"], w["lnb"], w["gavg"], w["ws"], w["bias"], w["gog"]]
    return pl.pallas_call(
        _prep_kernel,
        grid=(B, S // ts),
        in_specs=[pl.BlockSpec((1, ts, D), lambda b, i: (b, i, 0)),
                  pl.BlockSpec((1, 1, 1, ts), lambda b, i: (b, i, 0, 0))] + [full(a) for a in consts],
        out_specs=[pl.BlockSpec((1, ts, HP), lambda b, i: (b, i, 0)),
                   pl.BlockSpec((1, ts, HP), lambda b, i: (b, i, 0)),
                   pl.BlockSpec((1, ts // ATTN_ROWS, MLA_OUT, ATTN_ROWS), lambda b, i: (b, i, 0, 0)),
                   pl.BlockSpec((1, ts, GM_OUT), lambda b, i: (b, i, 0))],
        out_shape=[jax.ShapeDtypeStruct((B, S, HP), BF16)] * 2
        + [jax.ShapeDtypeStruct((B, S // ATTN_ROWS, MLA_OUT, ATTN_ROWS), BF16),
           jax.ShapeDtypeStruct((B, S, GM_OUT), BF16)],
        compiler_params=pltpu.CompilerParams(dimension_semantics=("parallel", "parallel"),
                                             vmem_limit_bytes=VMEM_LIMIT),
        name="prep",
    )(x, pos4, *consts)


def _attn_kernel(q_ref, k_ref, vt_ref, g_ref, x_ref, woa_ref, wog_ref, mog_ref, l1g_ref, l1b_ref,
                 o_ref, m_scr, acc_scr, sa_scr, sb_scr):
    pid = pl.program_id(1)
    tq = ATTN_ROWS
    tk = tq
    key = lax.broadcasted_iota(jnp.int32, (tk, tq), 0)
    qry = lax.broadcasted_iota(jnp.int32, (tk, tq), 1)
    diag_mask = key <= qry
    ones = jnp.ones((ONES_ROWS, tk), BF16)

    def tile(r0, i, odd):
        m_scr[...] = jnp.full(m_scr.shape, -1e30, F32)
        acc_scr[...] = jnp.zeros(acc_scr.shape, F32)

        def scores(j, s_scr):
            k0 = pl.multiple_of(j * tk, tk)
            for hd in range(MLA_HEADS):
                lo = hd * LANES
                qh = q_ref[0, r0:r0 + tq, lo:lo + LANES]
                kj = k_ref[0, pl.ds(k0, tk), lo:lo + LANES]
                s_scr[hd] = lax.dot_general(kj, qh, (((1,), (1,)), ((), ())), preferred_element_type=F32)

        def update(j, s_scr, masked):
            for hd in range(MLA_HEADS):
                s = s_scr[hd]
                vt = vt_ref[0, j, hd * V_HEAD:(hd + 1) * V_HEAD, :]
                if masked:
                    s = jnp.where(diag_mask, s, -1e30)
                m_prev = m_scr[hd]
                m_new = jnp.maximum(m_prev, jnp.max(s, axis=0, keepdims=True))
                p = jnp.exp2(s - m_new).astype(BF16)
                scale = jnp.exp2(m_prev - m_new)
                acc_scr[hd] = scale * acc_scr[hd] + _dot(jnp.concatenate([vt, ones], axis=0), p)
                m_scr[hd] = m_new

        def pair(jj, c):
            j = 2 * jj
            scores(j + 1, sb_scr)
            update(j, sa_scr, False)
            scores(j + 2, sa_scr)
            update(j + 1, sb_scr, False)
            return c

        scores(0, sa_scr)
        lax.fori_loop(0, pid, pair, 0)
        if odd:
            scores(i, sb_scr)
            update(i - 1, sa_scr, False)
            update(i, sb_scr, True)
        else:
            update(i, sa_scr, True)

        at = jnp.concatenate([acc_scr[hd, :V_HEAD] / acc_scr[hd, V_HEAD:V_HEAD + 1] for hd in range(MLA_HEADS)],
                             axis=0)
        at = at * lax.rsqrt(jnp.mean(at * at, axis=0, keepdims=True) + EPS) * mog_ref[...]
        mix = _dot(at.T.astype(BF16), woa_ref[...]) + _dot(g_ref[0, r0:r0 + tq, :], wog_ref[...])
        o_ref[0, r0:r0 + tq, :] = _ln(ALPHA * x_ref[0, r0:r0 + tq, :] + mix, l1g_ref[...], l1b_ref[...])

    tile(0, 2 * pid, False)
    tile(tq, 2 * pid + 1, True)


def _attn(q, k, vt, g, x, w):
    B, S, D = x.shape
    tq = ATTN_ROWS
    rows = 2 * tq
    full = lambda a: pl.BlockSpec(a.shape, lambda b, i: (0,) * a.ndim)
    consts = [w["woa"], w["wog"], w["mog"], w["l1g"], w["l1b"]]
    return pl.pallas_call(
        _attn_kernel,
        grid=(B, S // rows),
        in_specs=[pl.BlockSpec((1, rows, HP), lambda b, i: (b, i, 0)),
                  pl.BlockSpec((1, S, HP), lambda b, i: (b, 0, 0)),
                  pl.BlockSpec((1,) + vt.shape[1:], lambda b, i: (b, 0, 0, 0)),
                  pl.BlockSpec((1, rows, GM_OUT), lambda b, i: (b, i, 0)),
                  pl.BlockSpec((1, rows, D), lambda b, i: (b, i, 0))] + [full(a) for a in consts],
        out_specs=pl.BlockSpec((1, rows, D), lambda b, i: (b, i, 0)),
        out_shape=jax.ShapeDtypeStruct((B, S, D), F32),
        scratch_shapes=[pltpu.VMEM((MLA_HEADS, 1, tq), F32),
                        pltpu.VMEM((MLA_HEADS, V_HEAD + ONES_ROWS, tq), F32),
                        pltpu.VMEM((MLA_HEADS, tq, tq), F32), pltpu.VMEM((MLA_HEADS, tq, tq), F32)],
        compiler_params=pltpu.CompilerParams(dimension_semantics=("parallel", "parallel"),
                                             vmem_limit_bytes=VMEM_LIMIT),
        name="attn",
    )(q, k, vt, g, x, *consts)


def _route_kernel(x_ref, wr_ref, br_ref, info_ref, infot_ref, cnt_ref, carry_scr, tri_scr):
    step = pl.program_id(0)
    sub = tri_scr.shape[0]

    @pl.when(step == 0)
    def _():
        carry_scr[...] = jnp.zeros_like(carry_scr)
        r = lax.broadcasted_iota(jnp.int32, (sub, sub), 0)
        c = lax.broadcasted_iota(jnp.int32, (sub, sub), 1)
        tri_scr[...] = jnp.where(c < r, 1.0, 0.0).astype(BF16)

    wr = wr_ref[...]
    wh = wr.astype(BF16)
    wl = (wr - wh.astype(F32)).astype(BF16)
    lane = lax.broadcasted_iota(jnp.int32, (sub, LANES), 1)
    neg = jnp.float32(-jnp.inf)
    carry = carry_scr[...]

    for h in range(x_ref.shape[0] // sub):
        r0_, r1_ = h * sub, (h + 1) * sub
        x = x_ref[r0_:r1_, :]
        xh = x.astype(BF16)
        xl = (x - xh.astype(F32)).astype(BF16)
        logits = _dot(xh, wh) + _dot(xl, wh) + _dot(xh, wl) + br_ref[...]

        is_g = lane < N_GROUPS
        lg = jnp.where(is_g, logits, neg)
        gmax = jnp.max(lg, axis=-1, keepdims=True)
        g_idx = jnp.min(jnp.where(lg == gmax, lane, LANES), axis=-1, keepdims=True)
        g_den = jnp.sum(jnp.where(is_g, jnp.exp(lg - gmax), 0.0), axis=-1, keepdims=True)
        g_p = 1.0 / g_den

        in_grp = (lane >= R_OFF) & (lane < R_OFF + N_EXPERTS) & (((lane - R_OFF) >> 3) == g_idx)
        le = jnp.where(in_grp, logits, neg)
        m1 = jnp.max(le, axis=-1, keepdims=True)
        i1 = jnp.min(jnp.where(le == m1, lane, LANES), axis=-1, keepdims=True)
        le2 = jnp.where(lane == i1, neg, le)
        m2 = jnp.max(le2, axis=-1, keepdims=True)
        i2 = jnp.min(jnp.where(le2 == m2, lane, LANES), axis=-1, keepdims=True)
        e2 = jnp.exp(m2 - m1)
        gate0 = g_p / (1.0 + e2)
        gate1 = g_p * e2 / (1.0 + e2)

        hit1 = lane == i1
        hit2 = lane == i2
        onehot = jnp.where(hit1 | hit2, 1.0, 0.0)
        before = _dot(tri_scr[...], onehot.astype(BF16)) + carry
        rank0 = jnp.sum(jnp.where(hit1, before, 0.0), axis=-1, keepdims=True)
        rank1 = jnp.sum(jnp.where(hit2, before, 0.0), axis=-1, keepdims=True)
        carry = carry + jnp.sum(onehot, axis=0, keepdims=True)

        info = jnp.where(lane == I_E0, (i1 - R_OFF).astype(F32), 0.0)
        info = jnp.where(lane == I_E1, (i2 - R_OFF).astype(F32), info)
        info = jnp.where(lane == I_R0, rank0, info)
        info = jnp.where(lane == I_R1, rank1, info)
        info = jnp.where(lane == I_G0, gate0, info)
        info = jnp.where(lane == I_G1, gate1, info)
        info_ref[r0_:r1_, :] = info
        infot_ref[0, :, r0_:r1_] = info.T[:SUBLANES]

    carry_scr[...] = carry
    cnt_ref[...] = carry


def _route(x1, wr, br):
    T, D = x1.shape
    tt = ROUTE_ROWS
    return pl.pallas_call(
        _route_kernel,
        grid=(T // tt,),
        in_specs=[pl.BlockSpec((tt, D), lambda i: (i, 0)),
                  pl.BlockSpec(wr.shape, lambda i: (0, 0)),
                  pl.BlockSpec(br.shape, lambda i: (0, 0))],
        out_specs=[pl.BlockSpec((tt, LANES), lambda i: (i, 0)),
                   pl.BlockSpec((1, SUBLANES, tt), lambda i: (i, 0, 0)),
                   pl.BlockSpec((1, LANES), lambda i: (0, 0))],
        out_shape=[jax.ShapeDtypeStruct((T, LANES), F32), jax.ShapeDtypeStruct((T // tt, SUBLANES, tt), F32),
                   jax.ShapeDtypeStruct((1, LANES), F32)],
        scratch_shapes=[pltpu.VMEM((1, LANES), F32), pltpu.VMEM((ROUTE_SUB, ROUTE_SUB), BF16)],
        compiler_params=pltpu.CompilerParams(dimension_semantics=("arbitrary",), vmem_limit_bytes=VMEM_LIMIT),
        name="route",
    )(x1, wr, br)


def _to_token_tiles(dst_ref, val):
    dst_ref[...] = val.astype(BF16).reshape(dst_ref.shape)


def _from_token_tiles(src_ref, rows):
    return src_ref[...].reshape(rows, TOKEN_ROWS * LANES)


def _to_token_tiles_f32(dst_ref, val):
    rows = val.shape[0]
    for c in range(TOKEN_ROWS):
        dst_ref[pl.ds(c, rows, stride=TOKEN_ROWS), :] = val[:, c * LANES:(c + 1) * LANES]


def _from_token_tiles_f32(src_ref, rows):
    return jnp.concatenate([src_ref[pl.ds(c, rows, stride=TOKEN_ROWS), :] for c in range(TOKEN_ROWS)], axis=1)


def _tile_copy(src_ref, src_row, dst_ref, dst_row, sem):
    return pltpu.make_async_copy(src_ref.at[pl.ds(pl.multiple_of(src_row, TOKEN_ROWS), TOKEN_ROWS)],
                                 dst_ref.at[pl.ds(pl.multiple_of(dst_row, TOKEN_ROWS), TOKEN_ROWS)], sem)


def _dispatch_kernel(seg_ref, dest_ref, x_ref, buf_ref, stage_scr, zero_scr, sem, zero_sem, *, n_steps):
    i = pl.program_id(0)
    rows = x_ref.shape[0]
    slot = i % 2

    @pl.when(i == 0)
    def _():
        zero_scr[...] = jnp.zeros(zero_scr.shape, BF16)

        block = EXPERT_ROWS * TOKEN_ROWS
        n_blocks = buf_ref.shape[0] // block

        def clear_rows(first):
            return pltpu.make_async_copy(zero_scr, buf_ref.at[pl.ds(pl.multiple_of(first, SUBLANES), block)], zero_sem)

        def clear(e):
            return clear_rows((seg_ref[0, e] - EXPERT_ROWS) * TOKEN_ROWS)

        def start_tail(b, c):
            clear_rows(b * block).start()
            return c

        def wait_tail(b, c):
            clear_rows(b * block).wait()
            return c

        for e in range(N_EXPERTS):
            pl.when(seg_ref[1, e] > 0)(lambda e=e: clear(e).start())
        lax.fori_loop(seg_ref[2, 0], n_blocks, start_tail, 0)
        for e in range(N_EXPERTS):
            pl.when(seg_ref[1, e] > 0)(lambda e=e: clear(e).wait())
        lax.fori_loop(seg_ref[2, 0], n_blocks, wait_tail, 0)

    def drain(s):
        for _ in range(TOP_K):
            pltpu.make_async_copy(stage_scr.at[s], stage_scr.at[s], sem.at[s]).wait()

    @pl.when(i >= 2)
    def _():
        drain(slot)

    _to_token_tiles(stage_scr.at[slot], x_ref[...])

    def start(c, carry):
        for u in range(MOVE_UNROLL):
            r = c * MOVE_UNROLL + u
            for kk in range(TOP_K):
                _tile_copy(stage_scr.at[slot], r * TOKEN_ROWS, buf_ref, dest_ref[0, 0, kk * rows + r],
                           sem.at[slot]).start(priority=kk)
        return carry

    lax.fori_loop(0, rows // MOVE_UNROLL, start, 0)

    @pl.when(i == n_steps - 1)
    def _():
        drain(slot)
        if n_steps >= 2:
            drain(1 - slot)


def _dispatch(seg, dest3, x1, n_rows):
    T, D = x1.shape
    td = MOVE_ROWS
    n_steps = T // td
    grid_spec = pltpu.PrefetchScalarGridSpec(
        num_scalar_prefetch=1,
        grid=(n_steps,),
        in_specs=[pl.BlockSpec((1, 1, TOP_K * td), lambda i, seg: (i, 0, 0), memory_space=pltpu.SMEM),
                  pl.BlockSpec((td, D), lambda i, seg: (i, 0))],
        out_specs=pl.BlockSpec(memory_space=pl.ANY),
        scratch_shapes=[pltpu.VMEM((2, td * TOKEN_ROWS, LANES), BF16),
                        pltpu.VMEM((EXPERT_ROWS * TOKEN_ROWS, LANES), BF16),
                        pltpu.SemaphoreType.DMA((2,)), pltpu.SemaphoreType.DMA(())],
    )
    return pl.pallas_call(
        functools.partial(_dispatch_kernel, n_steps=n_steps),
        grid_spec=grid_spec,
        out_shape=jax.ShapeDtypeStruct((n_rows * TOKEN_ROWS, LANES), BF16),
        compiler_params=pltpu.CompilerParams(dimension_semantics=("arbitrary",), vmem_limit_bytes=VMEM_LIMIT),
        name="dispatch",
    )(seg, dest3, x1)


def _expert_kernel(be_ref, ne_ref, nu_ref, buf0_ref, bufa_ref, bufb_ref, wg_hbm, wu_hbm, wd_hbm, y_ref,
                   sg_scr, su_scr, sd_scr, wg_scr, wu_scr, wd_scr, xa_scr, xb_scr, cur_ref, sem):
    step = pl.program_id(0)
    bm = EXPERT_ROWS
    half = bm * TOKEN_ROWS

    def fetch(expert, s):
        return (pltpu.make_async_copy(wg_hbm.at[expert], sg_scr.at[s], sem.at[s, 0]),
                pltpu.make_async_copy(wu_hbm.at[expert], su_scr.at[s], sem.at[s, 1]),
                pltpu.make_async_copy(wd_hbm.at[expert], sd_scr.at[s], sem.at[s, 2]))

    @pl.when(step == 0)
    def _():
        cur_ref[0] = 0
        for c in fetch(be_ref[0], 0):
            c.start()
        xa_scr[...] = _from_token_tiles(buf0_ref, bm)

    def load_weights(blk):
        e = be_ref[blk]

        @pl.when((blk == 0) | (be_ref[jnp.maximum(blk - 1, 0)] != e))
        def _():
            s = cur_ref[0]
            for c in fetch(e, s):
                c.wait()
            wg_scr[...] = sg_scr[s].astype(BF16)
            wu_scr[...] = su_scr[s].astype(BF16)
            wd_scr[...] = sd_scr[s].astype(BF16)
            nxt = ne_ref[blk]

            @pl.when(nxt >= 0)
            def _():
                for c in fetch(nxt, 1 - s):
                    c.start()

            cur_ref[0] = 1 - s

    def run(blk, x_scr, nxt_ref, nxt_scr, out_rows):
        load_weights(blk)

        @pl.when(blk < nu_ref[0])
        def _():
            nxt_scr[...] = _from_token_tiles(nxt_ref, bm)
            xb = x_scr[...]
            hidden = jax.nn.silu(_dot(xb, wg_scr[...])) * _dot(xb, wu_scr[...])
            _to_token_tiles_f32(y_ref.at[out_rows], _dot(hidden.astype(BF16), wd_scr[...]))

        @pl.when(blk >= nu_ref[0])
        def _():
            y_ref[out_rows, :] = jnp.zeros((half, LANES), F32)

    run(2 * step, xa_scr, bufa_ref, xb_scr, pl.ds(0, half))
    run(2 * step + 1, xb_scr, bufb_ref, xa_scr, pl.ds(half, half))


def _experts(block_expert, next_expert, n_used, buf, w_gate, w_up, w_down):
    bm = EXPERT_ROWS
    D, ff = w_gate.shape[1:]
    n_blocks = buf.shape[0] // (bm * TOKEN_ROWS)
    assert n_blocks % 2 == 0
    last = n_blocks - 1
    grid_spec = pltpu.PrefetchScalarGridSpec(
        num_scalar_prefetch=3,
        grid=(n_blocks // 2,),
        in_specs=[pl.BlockSpec((bm * TOKEN_ROWS, LANES), lambda s, *_: (0, 0)),
                  pl.BlockSpec((bm * TOKEN_ROWS, LANES), lambda s, *_: (2 * s + 1, 0)),
                  pl.BlockSpec((bm * TOKEN_ROWS, LANES), lambda s, *_: (jnp.minimum(2 * s + 2, last), 0)),
                  pl.BlockSpec(memory_space=pl.ANY),
                  pl.BlockSpec(memory_space=pl.ANY),
                  pl.BlockSpec(memory_space=pl.ANY)],
        out_specs=pl.BlockSpec((2 * bm * TOKEN_ROWS, LANES), lambda s, *_: (s, 0)),
        scratch_shapes=[pltpu.VMEM((2, D, ff), F32), pltpu.VMEM((2, D, ff), F32), pltpu.VMEM((2, ff, D), F32),
                        pltpu.VMEM((D, ff), BF16), pltpu.VMEM((D, ff), BF16), pltpu.VMEM((ff, D), BF16),
                        pltpu.VMEM((bm, D), BF16), pltpu.VMEM((bm, D), BF16),
                        pltpu.SMEM((1,), jnp.int32), pltpu.SemaphoreType.DMA((2, 3))],
    )
    return pl.pallas_call(
        _expert_kernel,
        grid_spec=grid_spec,
        out_shape=jax.ShapeDtypeStruct(buf.shape, F32),
        compiler_params=pltpu.CompilerParams(dimension_semantics=("arbitrary",), vmem_limit_bytes=VMEM_LIMIT),
        name="experts",
    )(block_expert, next_expert, n_used, buf, buf, buf, w_gate, w_up, w_down)


def _final_kernel(dcur_ref, dnxt_ref, x_ref, info_ref, y_ref, p_ref, wpg_ref, bpg_ref, wpp_ref,
                  l2g_ref, l2b_ref, l3g_ref, l3b_ref, o_ref, rows_scr, sem):
    i = pl.program_id(0)
    last = pl.num_programs(0) - 1
    rows = x_ref.shape[0]
    slot = i % 2

    def row_copy(dref, s, r, kk):
        return _tile_copy(y_ref, dref[0, 0, kk * rows + r], rows_scr.at[s, kk], r * TOKEN_ROWS, sem.at[s])

    def landed(s):
        pltpu.make_async_copy(rows_scr.at[s], rows_scr.at[s], sem.at[s]).wait()

    @pl.when(i == 0)
    def _():
        def start(c, carry):
            for u in range(MOVE_UNROLL):
                for kk in range(TOP_K):
                    row_copy(dcur_ref, 0, c * MOVE_UNROLL + u, kk).start(priority=kk)
            return carry

        lax.fori_loop(0, rows // MOVE_UNROLL, start, 0)

    landed(slot)
    info = info_ref[...]
    gate0 = info[:, I_G0:I_G0 + 1]
    gate1 = info[:, I_G1:I_G1 + 1]
    moe = (_from_token_tiles_f32(rows_scr.at[slot, 0], rows) * gate0
           + _from_token_tiles_f32(rows_scr.at[slot, 1], rows) * gate1)

    for r in range(rows):
        for kk in range(TOP_K):
            row_copy(dnxt_ref, 1 - slot, r, kk).start(priority=kk)

    pp = _dot(p_ref[...].astype(BF16), wpp_ref[...])
    x2 = _ln(ALPHA * x_ref[...] + moe, l2g_ref[...], l2b_ref[...])
    gate = jax.nn.sigmoid(_dot(x2.astype(BF16), wpg_ref[...]) + bpg_ref[...])
    o_ref[...] = _ln(ALPHA * x2 + gate * pp, l3g_ref[...], l3b_ref[...])

    @pl.when(i == last)
    def _():
        landed(1 - slot)


def _final(dest3, x1, info, y, p2, w):
    T, D = x1.shape
    tc = MOVE_ROWS
    pd = p2.shape[1]
    full = lambda a: pl.BlockSpec(a.shape, lambda i: (0,) * a.ndim)
    consts = [w["wpg"], w["bpg"], w["wpp"], w["l2g"], w["l2b"], w["l3g"], w["l3b"]]
    last = T // tc - 1
    return pl.pallas_call(
        _final_kernel,
        grid=(T // tc,),
        in_specs=[pl.BlockSpec((1, 1, TOP_K * tc), lambda i: (i, 0, 0), memory_space=pltpu.SMEM),
                  pl.BlockSpec((1, 1, TOP_K * tc), lambda i: (jnp.minimum(i + 1, last), 0, 0), memory_space=pltpu.SMEM),
                  pl.BlockSpec((tc, D), lambda i: (i, 0)),
                  pl.BlockSpec((tc, LANES), lambda i: (i, 0)),
                  pl.BlockSpec(memory_space=pl.ANY),
                  pl.BlockSpec((tc, pd), lambda i: (i, 0))] + [full(a) for a in consts],
        out_specs=pl.BlockSpec((tc, D), lambda i: (i, 0)),
        out_shape=jax.ShapeDtypeStruct((T, D), F32),
        scratch_shapes=[pltpu.VMEM((2, TOP_K, tc * TOKEN_ROWS, LANES), F32), pltpu.SemaphoreType.DMA((2,))],
        compiler_params=pltpu.CompilerParams(dimension_semantics=("arbitrary",), vmem_limit_bytes=VMEM_LIMIT),
        name="final",
    )(dest3, dest3, x1, info, y, p2, *consts)


def _pad_heads(a, width):
    lead = a.shape[:-1]
    a = a.reshape(lead + (MLA_HEADS, width))
    a = jnp.pad(a, [(0, 0)] * len(lead) + [(0, 0), (0, LANES - width)])
    return a.reshape(lead + (HP,))


def _layer_weights(w_in, q_norm_g, w_q_up, kv_norm_g, w_kv_up, gm_ln_g, gm_ln_b, gm_w_s, gm_b_s,
                   mla_out_g, gm_out_g, w_o, ln1_g, ln1_b):
    D = w_in.shape[0]
    half = QK_ROPE // 2
    c1, c2, c3 = Q_RANK, Q_RANK + KV_RANK, Q_RANK + KV_RANK + QK_ROPE
    zeros = lambda *s: jnp.zeros(s, F32)
    kr = jnp.concatenate([zeros(D, QK_NOPE), w_in[:, c2:c3], zeros(D, LANES - QK_NOPE - QK_ROPE)], axis=1)
    win = jnp.concatenate([w_in[:, :c2], kr, w_in[:, c3:]], axis=1).astype(BF16)
    wq = _pad_heads(w_q_up, QK_NOPE + QK_ROPE).astype(BF16)

    wkv3 = w_kv_up.reshape(KV_RANK, MLA_HEADS, QK_NOPE + V_HEAD)
    wk = _pad_heads(wkv3[..., :QK_NOPE].reshape(KV_RANK, -1), QK_NOPE).astype(BF16)
    wv = wkv3[..., QK_NOPE:].reshape(KV_RANK, -1).astype(BF16)

    inv = (ROPE_THETA ** (-jnp.arange(0, QK_ROPE, 2, dtype=F32) / QK_ROPE))[:, None]
    eye = jnp.eye(half, dtype=F32)
    first = jnp.pad(eye, ((0, 0), (QK_NOPE, LANES - QK_NOPE - half)))
    second = jnp.pad(eye, ((0, 0), (QK_NOPE + half, LANES - QK_NOPE - QK_ROPE)))
    zero = jnp.zeros_like(first)
    cos_rows = jnp.concatenate([first + second, zero, zero], axis=1)
    sin_rows = jnp.concatenate([zero, -first, second], axis=1)
    rope = jnp.concatenate([cos_rows, cos_rows, sin_rows, sin_rows], axis=0).astype(BF16)
    lane = jnp.arange(LANES)
    one = jnp.where((lane >= QK_NOPE) & (lane < QK_NOPE + QK_ROPE), 0.0, 1.0)[None, :]

    grp = jnp.arange(GM_OUT) // GM_CH
    gavg = jnp.where(grp[:, None] == grp[None, :], 1.0 / GM_CH, 0.0).astype(BF16)
    bias = jnp.repeat(gm_b_s.T, GM_CH, axis=1)

    woa = w_o[:MLA_OUT].astype(BF16)
    wog = w_o[MLA_OUT:].astype(BF16)
    return dict(win=win, qg=q_norm_g[None, :], wq=wq, kvg=kv_norm_g[None, :], wk=wk, wv=wv, inv=inv, rope=rope, one=one,
                lng=gm_ln_g[None, :], lnb=gm_ln_b[None, :], gavg=gavg, ws=gm_w_s, bias=bias, gog=gm_out_g[None, :],
                woa=woa, wog=wog, mog=mla_out_g[:, None], l1g=ln1_g[None, :], l1b=ln1_b[None, :])


def _moe(x1, w_rg, b_rg, w_re, b_re, w_gate, w_up, w_down):
    T, D = x1.shape
    pad = jnp.zeros((D, LANES - N_GROUPS - N_EXPERTS), F32)
    wr = jnp.concatenate([w_rg, w_re, pad], axis=1)
    br = jnp.concatenate([b_rg, b_re, pad[0]])[None, :]
    info, info_t, cnt = _route(x1, wr, br)

    bm = EXPERT_ROWS
    n_blocks = (T * TOP_K) // bm + N_EXPERTS
    counts = cnt[0, R_OFF:R_OFF + N_EXPERTS].astype(jnp.int32)
    padded = (counts + bm - 1) // bm * bm
    pad_ends = jnp.cumsum(padded)
    pad_starts = pad_ends - padded
    def dest_rows(e_lane, r_lane):
        e = info_t[:, e_lane, :].astype(jnp.int32)
        ids = jnp.arange(N_EXPERTS)[:, None, None]
        seg_start = jnp.sum(jnp.where(e[None] == ids, pad_starts[:, None, None], 0), axis=0)
        return ((seg_start + info_t[:, r_lane, :].astype(jnp.int32)) * TOKEN_ROWS).reshape(T // MOVE_ROWS, MOVE_ROWS)

    dest = jnp.concatenate([dest_rows(I_E0, I_R0), dest_rows(I_E1, I_R1)], axis=1)[:, None, :]
    block_start = jnp.arange(n_blocks, dtype=jnp.int32) * bm
    block_expert = jnp.minimum(jnp.sum(pad_ends[None, :] <= block_start[:, None], axis=1),
                               N_EXPERTS - 1).astype(jnp.int32)

    blk = jnp.arange(n_blocks)
    later = (blk[None, :] > blk[:, None]) & (block_expert[None, :] != block_expert[:, None])
    next_expert = jnp.min(jnp.where(later, block_expert[None, :], N_EXPERTS), axis=1)
    next_expert = jnp.where(next_expert == N_EXPERTS, -1, next_expert).astype(jnp.int32)
    n_used = (pad_ends[-1:] // bm).astype(jnp.int32)

    seg = jnp.stack([pad_ends, padded, jnp.broadcast_to(n_used, (N_EXPERTS,))]).astype(jnp.int32)
    buf = _dispatch(seg, dest, x1, n_blocks * bm)
    y = _experts(block_expert, next_expert, n_used, buf, w_gate, w_up, w_down)
    return info, dest, y


def kernel(x, p, positions, w_in, q_norm_g, w_q_up, kv_norm_g, w_kv_up, gm_ln_g, gm_ln_b, gm_w_s, gm_b_s, mla_out_g, gm_out_g, w_o, ln1_g, ln1_b, w_rg, b_rg, w_re, b_re, w_gate, w_up, w_down, ln2_g, ln2_b, w_pg, b_pg, w_pp, ln3_g, ln3_b):
    B, S, D = x.shape
    T = B * S
    assert S % PREP_ROWS == 0 and PREP_ROWS % ATTN_ROWS == 0 and PREP_ROWS % CHUNK == 0 and S % (2 * ATTN_ROWS) == 0
    assert T % ROUTE_ROWS == 0 and T % MOVE_ROWS == 0 and (T * TOP_K) % EXPERT_ROWS == 0
    assert D == TOKEN_ROWS * LANES and MOVE_ROWS % MOVE_UNROLL == 0
    pos4 = positions.reshape(B, S // PREP_ROWS, 1, PREP_ROWS)
    for i in range(DEPTH):
        w = _layer_weights(w_in[i], q_norm_g[i], w_q_up[i], kv_norm_g[i], w_kv_up[i], gm_ln_g[i], gm_ln_b[i],
                           gm_w_s[i], gm_b_s[i], mla_out_g[i], gm_out_g[i], w_o[i], ln1_g[i], ln1_b[i])
        q, k, vt, g = _prep(x, pos4, w)
        x1 = _attn(q, k, vt, g, x, w).reshape(T, D)
        info, dest, y = _moe(x1, w_rg[i], b_rg[i], w_re[i], b_re[i], w_gate[i], w_up[i], w_down[i])
        wf = dict(wpg=w_pg[i].astype(BF16), bpg=b_pg[i][None, :], wpp=w_pp[i].astype(BF16),
                  l2g=ln2_g[i][None, :], l2b=ln2_b[i][None, :], l3g=ln3_g[i][None, :], l3b=ln3_b[i][None, :])
        x = _final(dest, x1, info, y, p[i].reshape(T, -1), wf).reshape(B, S, D)
    return x
```

```python
import functools

import jax
import jax.numpy as jnp
from jax import lax
from jax.experimental import pallas as pl
from jax.experimental.pallas import tpu as pltpu

F32 = jnp.float32
BF16 = jnp.bfloat16

MLA_HEADS = 8
QK_NOPE = 64
QK_ROPE = 32
V_HEAD = 64
Q_RANK = 256
KV_RANK = 128
ROPE_THETA = 10000.0
MLA_OUT = MLA_HEADS * V_HEAD
GM_GROUPS = 8
GM_CH = 64
GM_OUT = GM_GROUPS * GM_CH
CHUNK = 128
N_GROUPS = 4
EXP_PER_GROUP = 8
N_EXPERTS = N_GROUPS * EXP_PER_GROUP
TOP_K = 2
EPS = 1e-6
DEPTH = 1
ALPHA = (2.0 * DEPTH) ** 0.25
SM_SCALE = (QK_NOPE + QK_ROPE) ** -0.5
LOG2E = 1.4426950408889634

LANES = 128
SUBLANES = 8
TOKEN_ROWS = 8
ONES_ROWS = 16
VMEM_LIMIT = 56 * 1024 * 1024

PREP_ROWS = 512
ATTN_ROWS = 256
ATTN_TILES = 4
ROUTE_ROWS = 512
ROUTE_SUB = 512
MOVE_ROWS = 256
MOVE_UNROLL = 8
EXPERT_ROWS = 256

C_Q = 0
C_KV = C_Q + Q_RANK
C_KR = C_KV + KV_RANK
C_U = C_KR + LANES
C_V = C_U + GM_OUT
C_END = C_V + GM_OUT
HP = MLA_HEADS * LANES

I_E0, I_E1, I_R0, I_R1, I_G0, I_G1 = range(6)
R_OFF = N_GROUPS


def _rms(v, g):
    return v * lax.rsqrt(jnp.mean(v * v, axis=-1, keepdims=True) + EPS) * g


def _ln(v, g, b):
    mu = jnp.mean(v, axis=-1, keepdims=True)
    d = v - mu
    var = jnp.mean(d * d, axis=-1, keepdims=True)
    return d * lax.rsqrt(var + EPS) * g + b


def _dot(a, b):
    return jnp.dot(a, b, preferred_element_type=F32)


def _prep_kernel(x_ref, pos_ref, win_ref, qg_ref, wq_ref, kvg_ref, wk_ref, wv_ref, inv_ref, rope_ref, one_ref,
                 lng_ref, lnb_ref, gavg_ref, ws_ref, bias_ref, gog_ref,
                 q_ref, k_ref, vt_ref, g_ref):
    rows = x_ref.shape[1]
    h = _dot(x_ref[0].astype(BF16), win_ref[...])

    ang = inv_ref[...] * pos_ref[0, 0].astype(F32)
    parts = []
    for t in (jnp.cos(ang), jnp.sin(ang)):
        hi = t.astype(BF16).astype(F32)
        parts += [hi, t - hi]
    tabs = _dot(jnp.concatenate(parts, axis=0).T.astype(BF16), rope_ref[...])
    cos_t = tabs[:, :LANES] + one_ref[...]
    sin_a = tabs[:, LANES:2 * LANES]
    sin_b = tabs[:, 2 * LANES:]
    half = QK_ROPE // 2

    def rotate(v):
        return v * cos_t + pltpu.roll(v, LANES - half, 1) * sin_a + pltpu.roll(v, half, 1) * sin_b

    cq = _rms(h[:, C_Q:C_Q + Q_RANK], qg_ref[...]).astype(BF16)
    q2 = _dot(cq, wq_ref[...])
    for hd in range(MLA_HEADS):
        lo = hd * LANES
        q_ref[0, :, lo:lo + LANES] = (rotate(q2[:, lo:lo + LANES]) * (SM_SCALE * LOG2E)).astype(BF16)

    ckv = _rms(h[:, C_KV:C_KV + KV_RANK], kvg_ref[...]).astype(BF16)
    kp = _dot(ckv, wk_ref[...])
    kr = rotate(h[:, C_KR:C_KR + LANES])
    for hd in range(MLA_HEADS):
        lo = hd * LANES
        k_ref[0, :, lo:lo + LANES] = (kp[:, lo:lo + LANES] + kr).astype(BF16)
    vp = _dot(ckv, wv_ref[...])
    for kb in range(rows // ATTN_ROWS):
        vt_ref[0, kb] = vp[kb * ATTN_ROWS:(kb + 1) * ATTN_ROWS].T.astype(BF16)

    u = jax.nn.gelu(h[:, C_U:C_U + GM_OUT])
    vv = jax.nn.gelu(h[:, C_V:C_V + GM_OUT])
    mu = _dot(vv.astype(BF16), gavg_ref[...])
    d = vv - mu
    var = _dot((d * d).astype(BF16), gavg_ref[...])
    vn = (d * lax.rsqrt(var + EPS) * lng_ref[...] + lnb_ref[...]).astype(BF16)

    tri = lax.broadcasted_iota(jnp.int32, (CHUNK, CHUNK), 0) >= lax.broadcasted_iota(jnp.int32, (CHUNK, CHUNK), 1)
    wm = [jnp.where(tri, ws_ref[g], 0.0).astype(BF16) for g in range(GM_GROUPS)]
    low_half = lax.broadcasted_iota(jnp.int32, (CHUNK, LANES), 1) < GM_CH
    for c in range(rows // CHUNK):
        r0 = c * CHUNK
        parts = []
        for pr in range(GM_GROUPS // 2):
            tile = vn[r0:r0 + CHUNK, pr * LANES:(pr + 1) * LANES]
            parts.append(jnp.where(low_half, _dot(wm[2 * pr], tile), _dot(wm[2 * pr + 1], tile)))
        sg = jnp.concatenate(parts, axis=1) + bias_ref[...]
        gm = u[r0:r0 + CHUNK] * sg
        g_ref[0, r0:r0 + CHUNK, :] = _rms(gm, gog_ref[...]).astype(BF16)


def _prep(x, pos4, w):
    B, S, D = x.shape
    ts = PREP_ROWS
    full = lambda a: pl.BlockSpec(a.shape, lambda b, i: (0,) * a.ndim)
    consts = [w["win"], w["qg"], w["wq"], w["kvg"], w["wk"], w["wv"], w["inv"], w["rope"], w["one"],
              w["lng"], w["lnb"], w["gavg"], w["ws"], w["bias"], w["gog"]]
    return pl.pallas_call(
        _prep_kernel,
        grid=(B, S // ts),
        in_specs=[pl.BlockSpec((1, ts, D), lambda b, i: (b, i, 0)),
                  pl.BlockSpec((1, 1, 1, ts), lambda b, i: (b, i, 0, 0))] + [full(a) for a in consts],
        out_specs=[pl.BlockSpec((1, ts, HP), lambda b, i: (b, i, 0)),
                   pl.BlockSpec((1, ts, HP), lambda b, i: (b, i, 0)),
                   pl.BlockSpec((1, ts // ATTN_ROWS, MLA_OUT, ATTN_ROWS), lambda b, i: (b, i, 0, 0)),
                   pl.BlockSpec((1, ts, GM_OUT), lambda b, i: (b, i, 0))],
        out_shape=[jax.ShapeDtypeStruct((B, S, HP), BF16)] * 2
        + [jax.ShapeDtypeStruct((B, S // ATTN_ROWS, MLA_OUT, ATTN_ROWS), BF16),
           jax.ShapeDtypeStruct((B, S, GM_OUT), BF16)],
        compiler_params=pltpu.CompilerParams(dimension_semantics=("parallel", "parallel"),
                                             vmem_limit_bytes=VMEM_LIMIT),
        name="prep",
    )(x, pos4, *consts)


def _attn_kernel(q_ref, k_ref, vt_ref, g_ref, x_ref, woa_ref, wog_ref, mog_ref, l1g_ref, l1b_ref,
                 o_ref, m_scr, acc_scr, sa_scr, sb_scr):
    pid = pl.program_id(1)
    tq = ATTN_ROWS
    tk = tq
    key = lax.broadcasted_iota(jnp.int32, (tk, tq), 0)
    qry = lax.broadcasted_iota(jnp.int32, (tk, tq), 1)
    diag_mask = key <= qry
    ones = jnp.ones((ONES_ROWS, tk), BF16)

    def tile(t):
        r0 = t * tq
        i = ATTN_TILES * pid + t
        odd = t % 2 == 1
        m_scr[...] = jnp.full(m_scr.shape, -1e30, F32)
        acc_scr[...] = jnp.zeros(acc_scr.shape, F32)

        def scores(j, s_scr):
            k0 = pl.multiple_of(j * tk, tk)
            for hd in range(MLA_HEADS):
                lo = hd * LANES
                qh = q_ref[0, r0:r0 + tq, lo:lo + LANES]
                kj = k_ref[0, pl.ds(k0, tk), lo:lo + LANES]
                s_scr[hd] = lax.dot_general(kj, qh, (((1,), (1,)), ((), ())), preferred_element_type=F32)

        def update(j, s_scr, masked):
            for hd in range(MLA_HEADS):
                s = s_scr[hd]
                vt = vt_ref[0, j, hd * V_HEAD:(hd + 1) * V_HEAD, :]
                if masked:
                    s = jnp.where(diag_mask, s, -1e30)
                m_prev = m_scr[hd]
                m_new = jnp.maximum(m_prev, jnp.max(s, axis=0, keepdims=True))
                p = jnp.exp2(s - m_new).astype(BF16)
                scale = jnp.exp2(m_prev - m_new)
                acc_scr[hd] = scale * acc_scr[hd] + _dot(jnp.concatenate([vt, ones], axis=0), p)
                m_scr[hd] = m_new

        def pair(jj, c):
            j = 2 * jj
            scores(j + 1, sb_scr)
            update(j, sa_scr, False)
            scores(j + 2, sa_scr)
            update(j + 1, sb_scr, False)
            return c

        scores(0, sa_scr)
        lax.fori_loop(0, (ATTN_TILES // 2) * pid + t // 2, pair, 0)
        if odd:
            scores(i, sb_scr)
            update(i - 1, sa_scr, False)
            update(i, sb_scr, True)
        else:
            update(i, sa_scr, True)

        at = jnp.concatenate([acc_scr[hd, :V_HEAD] / acc_scr[hd, V_HEAD:V_HEAD + 1] for hd in range(MLA_HEADS)],
                             axis=0)
        at = at * lax.rsqrt(jnp.mean(at * at, axis=0, keepdims=True) + EPS) * mog_ref[...]
        mix = _dot(at.T.astype(BF16), woa_ref[...]) + _dot(g_ref[0, r0:r0 + tq, :], wog_ref[...])
        o_ref[0, r0:r0 + tq, :] = _ln(ALPHA * x_ref[0, r0:r0 + tq, :] + mix, l1g_ref[...], l1b_ref[...])

    for t in range(ATTN_TILES):
        tile(t)


def _attn(q, k, vt, g, x, w):
    B, S, D = x.shape
    tq = ATTN_ROWS
    rows = ATTN_TILES * tq
    full = lambda a: pl.BlockSpec(a.shape, lambda b, i: (0,) * a.ndim)
    consts = [w["woa"], w["wog"], w["mog"], w["l1g"], w["l1b"]]
    return pl.pallas_call(
        _attn_kernel,
        grid=(B, S // rows),
        in_specs=[pl.BlockSpec((1, rows, HP), lambda b, i: (b, i, 0)),
                  pl.BlockSpec((1, S, HP), lambda b, i: (b, 0, 0)),
                  pl.BlockSpec((1,) + vt.shape[1:], lambda b, i: (b, 0, 0, 0)),
                  pl.BlockSpec((1, rows, GM_OUT), lambda b, i: (b, i, 0)),
                  pl.BlockSpec((1, rows, D), lambda b, i: (b, i, 0))] + [full(a) for a in consts],
        out_specs=pl.BlockSpec((1, rows, D), lambda b, i: (b, i, 0)),
        out_shape=jax.ShapeDtypeStruct((B, S, D), F32),
        scratch_shapes=[pltpu.VMEM((MLA_HEADS, 1, tq), F32),
                        pltpu.VMEM((MLA_HEADS, V_HEAD + ONES_ROWS, tq), F32),
                        pltpu.VMEM((MLA_HEADS, tq, tq), F32), pltpu.VMEM((MLA_HEADS, tq, tq), F32)],
        compiler_params=pltpu.CompilerParams(dimension_semantics=("parallel", "parallel"),
                                             vmem_limit_bytes=VMEM_LIMIT),
        name="attn",
    )(q, k, vt, g, x, *consts)


def _route_kernel(x_ref, wr_ref, br_ref, info_ref, infot_ref, cnt_ref, carry_scr, tri_scr):
    step = pl.program_id(0)
    sub = tri_scr.shape[0]

    @pl.when(step == 0)
    def _():
        carry_scr[...] = jnp.zeros_like(carry_scr)
        r = lax.broadcasted_iota(jnp.int32, (sub, sub), 0)
        c = lax.broadcasted_iota(jnp.int32, (sub, sub), 1)
        tri_scr[...] = jnp.where(c < r, 1.0, 0.0).astype(BF16)

    wr = wr_ref[...]
    wh = wr.astype(BF16)
    wl = (wr - wh.astype(F32)).astype(BF16)
    lane = lax.broadcasted_iota(jnp.int32, (sub, LANES), 1)
    neg = jnp.float32(-jnp.inf)
    carry = carry_scr[...]

    for h in range(x_ref.shape[0] // sub):
        r0_, r1_ = h * sub, (h + 1) * sub
        x = x_ref[r0_:r1_, :]
        xh = x.astype(BF16)
        xl = (x - xh.astype(F32)).astype(BF16)
        logits = _dot(xh, wh) + _dot(xl, wh) + _dot(xh, wl) + br_ref[...]

        is_g = lane < N_GROUPS
        lg = jnp.where(is_g, logits, neg)
        gmax = jnp.max(lg, axis=-1, keepdims=True)
        g_idx = jnp.min(jnp.where(lg == gmax, lane, LANES), axis=-1, keepdims=True)
        g_den = jnp.sum(jnp.where(is_g, jnp.exp(lg - gmax), 0.0), axis=-1, keepdims=True)
        g_p = 1.0 / g_den

        in_grp = (lane >= R_OFF) & (lane < R_OFF + N_EXPERTS) & (((lane - R_OFF) >> 3) == g_idx)
        le = jnp.where(in_grp, logits, neg)
        m1 = jnp.max(le, axis=-1, keepdims=True)
        i1 = jnp.min(jnp.where(le == m1, lane, LANES), axis=-1, keepdims=True)
        le2 = jnp.where(lane == i1, neg, le)
        m2 = jnp.max(le2, axis=-1, keepdims=True)
        i2 = jnp.min(jnp.where(le2 == m2, lane, LANES), axis=-1, keepdims=True)
        e2 = jnp.exp(m2 - m1)
        gate0 = g_p / (1.0 + e2)
        gate1 = g_p * e2 / (1.0 + e2)

        hit1 = lane == i1
        hit2 = lane == i2
        onehot = jnp.where(hit1 | hit2, 1.0, 0.0)
        before = _dot(tri_scr[...], onehot.astype(BF16)) + carry
        rank0 = jnp.sum(jnp.where(hit1, before, 0.0), axis=-1, keepdims=True)
        rank1 = jnp.sum(jnp.where(hit2, before, 0.0), axis=-1, keepdims=True)
        carry = carry + jnp.sum(onehot, axis=0, keepdims=True)

        info = jnp.where(lane == I_E0, (i1 - R_OFF).astype(F32), 0.0)
        info = jnp.where(lane == I_E1, (i2 - R_OFF).astype(F32), info)
        info = jnp.where(lane == I_R0, rank0, info)
        info = jnp.where(lane == I_R1, rank1, info)
        info = jnp.where(lane == I_G0, gate0, info)
        info = jnp.where(lane == I_G1, gate1, info)
        info_ref[r0_:r1_, :] = info
        infot_ref[0, :, r0_:r1_] = info.T[:SUBLANES]

    carry_scr[...] = carry
    cnt_ref[...] = carry


def _route(x1, wr, br):
    T, D = x1.shape
    tt = ROUTE_ROWS
    return pl.pallas_call(
        _route_kernel,
        grid=(T // tt,),
        in_specs=[pl.BlockSpec((tt, D), lambda i: (i, 0)),
                  pl.BlockSpec(wr.shape, lambda i: (0, 0)),
                  pl.BlockSpec(br.shape, lambda i: (0, 0))],
        out_specs=[pl.BlockSpec((tt, LANES), lambda i: (i, 0)),
                   pl.BlockSpec((1, SUBLANES, tt), lambda i: (i, 0, 0)),
                   pl.BlockSpec((1, LANES), lambda i: (0, 0))],
        out_shape=[jax.ShapeDtypeStruct((T, LANES), F32), jax.ShapeDtypeStruct((T // tt, SUBLANES, tt), F32),
                   jax.ShapeDtypeStruct((1, LANES), F32)],
        scratch_shapes=[pltpu.VMEM((1, LANES), F32), pltpu.VMEM((ROUTE_SUB, ROUTE_SUB), BF16)],
        compiler_params=pltpu.CompilerParams(dimension_semantics=("arbitrary",), vmem_limit_bytes=VMEM_LIMIT),
        name="route",
    )(x1, wr, br)


def _to_token_tiles(dst_ref, val):
    dst_ref[...] = val.astype(BF16).reshape(dst_ref.shape)


def _from_token_tiles(src_ref, rows):
    return src_ref[...].reshape(rows, TOKEN_ROWS * LANES)


def _to_token_tiles_f32(dst_ref, val):
    rows = val.shape[0]
    for c in range(TOKEN_ROWS):
        dst_ref[pl.ds(c, rows, stride=TOKEN_ROWS), :] = val[:, c * LANES:(c + 1) * LANES]


def _from_token_tiles_f32(src_ref, rows):
    return jnp.concatenate([src_ref[pl.ds(c, rows, stride=TOKEN_ROWS), :] for c in range(TOKEN_ROWS)], axis=1)


def _tile_copy(src_ref, src_row, dst_ref, dst_row, sem):
    return pltpu.make_async_copy(src_ref.at[pl.ds(pl.multiple_of(src_row, TOKEN_ROWS), TOKEN_ROWS)],
                                 dst_ref.at[pl.ds(pl.multiple_of(dst_row, TOKEN_ROWS), TOKEN_ROWS)], sem)


def _dispatch_kernel(seg_ref, dest_ref, x_ref, buf_ref, stage_scr, zero_scr, sem, zero_sem, *, n_steps):
    i = pl.program_id(0)
    rows = x_ref.shape[0]
    slot = i % 2

    @pl.when(i == 0)
    def _():
        zero_scr[...] = jnp.zeros(zero_scr.shape, BF16)

        block = EXPERT_ROWS * TOKEN_ROWS
        n_blocks = buf_ref.shape[0] // block

        def clear_rows(first):
            return pltpu.make_async_copy(zero_scr, buf_ref.at[pl.ds(pl.multiple_of(first, SUBLANES), block)], zero_sem)

        def clear(e):
            return clear_rows((seg_ref[0, e] - EXPERT_ROWS) * TOKEN_ROWS)

        def start_tail(b, c):
            clear_rows(b * block).start()
            return c

        def wait_tail(b, c):
            clear_rows(b * block).wait()
            return c

        for e in range(N_EXPERTS):
            pl.when(seg_ref[1, e] > 0)(lambda e=e: clear(e).start())
        lax.fori_loop(seg_ref[2, 0], n_blocks, start_tail, 0)
        for e in range(N_EXPERTS):
            pl.when(seg_ref[1, e] > 0)(lambda e=e: clear(e).wait())
        lax.fori_loop(seg_ref[2, 0], n_blocks, wait_tail, 0)

    def drain(s):
        for _ in range(TOP_K):
            pltpu.make_async_copy(stage_scr.at[s], stage_scr.at[s], sem.at[s]).wait()

    @pl.when(i >= 2)
    def _():
        drain(slot)

    _to_token_tiles(stage_scr.at[slot], x_ref[...])

    def start(c, carry):
        for u in range(MOVE_UNROLL):
            r = c * MOVE_UNROLL + u
            for kk in range(TOP_K):
                _tile_copy(stage_scr.at[slot], r * TOKEN_ROWS, buf_ref, dest_ref[0, 0, kk * rows + r],
                           sem.at[slot]).start(priority=kk)
        return carry

    lax.fori_loop(0, rows // MOVE_UNROLL, start, 0)

    @pl.when(i == n_steps - 1)
    def _():
        drain(slot)
        if n_steps >= 2:
            drain(1 - slot)


def _dispatch(seg, dest3, x1, n_rows):
    T, D = x1.shape
    td = MOVE_ROWS
    n_steps = T // td
    grid_spec = pltpu.PrefetchScalarGridSpec(
        num_scalar_prefetch=1,
        grid=(n_steps,),
        in_specs=[pl.BlockSpec((1, 1, TOP_K * td), lambda i, seg: (i, 0, 0), memory_space=pltpu.SMEM),
                  pl.BlockSpec((td, D), lambda i, seg: (i, 0))],
        out_specs=pl.BlockSpec(memory_space=pl.ANY),
        scratch_shapes=[pltpu.VMEM((2, td * TOKEN_ROWS, LANES), BF16),
                        pltpu.VMEM((EXPERT_ROWS * TOKEN_ROWS, LANES), BF16),
                        pltpu.SemaphoreType.DMA((2,)), pltpu.SemaphoreType.DMA(())],
    )
    return pl.pallas_call(
        functools.partial(_dispatch_kernel, n_steps=n_steps),
        grid_spec=grid_spec,
        out_shape=jax.ShapeDtypeStruct((n_rows * TOKEN_ROWS, LANES), BF16),
        compiler_params=pltpu.CompilerParams(dimension_semantics=("arbitrary",), vmem_limit_bytes=VMEM_LIMIT),
        name="dispatch",
    )(seg, dest3, x1)


def _expert_kernel(be_ref, ne_ref, nu_ref, buf0_ref, bufa_ref, bufb_ref, wg_hbm, wu_hbm, wd_hbm, y_ref,
                   sg_scr, su_scr, sd_scr, wg_scr, wu_scr, wd_scr, xa_scr, xb_scr, cur_ref, sem):
    step = pl.program_id(0)
    bm = EXPERT_ROWS
    half = bm * TOKEN_ROWS

    def fetch(expert, s):
        return (pltpu.make_async_copy(wg_hbm.at[expert], sg_scr.at[s], sem.at[s, 0]),
                pltpu.make_async_copy(wu_hbm.at[expert], su_scr.at[s], sem.at[s, 1]),
                pltpu.make_async_copy(wd_hbm.at[expert], sd_scr.at[s], sem.at[s, 2]))

    @pl.when(step == 0)
    def _():
        cur_ref[0] = 0
        for c in fetch(be_ref[0], 0):
            c.start()
        xa_scr[...] = _from_token_tiles(buf0_ref, bm)

    def load_weights(blk):
        e = be_ref[blk]

        @pl.when((blk == 0) | (be_ref[jnp.maximum(blk - 1, 0)] != e))
        def _():
            s = cur_ref[0]
            for c in fetch(e, s):
                c.wait()
            wg_scr[...] = sg_scr[s].astype(BF16)
            wu_scr[...] = su_scr[s].astype(BF16)
            wd_scr[...] = sd_scr[s].astype(BF16)
            nxt = ne_ref[blk]

            @pl.when(nxt >= 0)
            def _():
                for c in fetch(nxt, 1 - s):
                    c.start()

            cur_ref[0] = 1 - s

    def run(blk, x_scr, nxt_ref, nxt_scr, out_rows):
        load_weights(blk)

        @pl.when(blk < nu_ref[0])
        def _():
            nxt_scr[...] = _from_token_tiles(nxt_ref, bm)
            xb = x_scr[...]
            hidden = jax.nn.silu(_dot(xb, wg_scr[...])) * _dot(xb, wu_scr[...])
            _to_token_tiles_f32(y_ref.at[out_rows], _dot(hidden.astype(BF16), wd_scr[...]))

        @pl.when(blk >= nu_ref[0])
        def _():
            y_ref[out_rows, :] = jnp.zeros((half, LANES), F32)

    run(2 * step, xa_scr, bufa_ref, xb_scr, pl.ds(0, half))
    run(2 * step + 1, xb_scr, bufb_ref, xa_scr, pl.ds(half, half))


def _experts(block_expert, next_expert, n_used, buf, w_gate, w_up, w_down):
    bm = EXPERT_ROWS
    D, ff = w_gate.shape[1:]
    n_blocks = buf.shape[0] // (bm * TOKEN_ROWS)
    assert n_blocks % 2 == 0
    last = n_blocks - 1
    grid_spec = pltpu.PrefetchScalarGridSpec(
        num_scalar_prefetch=3,
        grid=(n_blocks // 2,),
        in_specs=[pl.BlockSpec((bm * TOKEN_ROWS, LANES), lambda s, *_: (0, 0)),
                  pl.BlockSpec((bm * TOKEN_ROWS, LANES), lambda s, *_: (2 * s + 1, 0)),
                  pl.BlockSpec((bm * TOKEN_ROWS, LANES), lambda s, *_: (jnp.minimum(2 * s + 2, last), 0)),
                  pl.BlockSpec(memory_space=pl.ANY),
                  pl.BlockSpec(memory_space=pl.ANY),
                  pl.BlockSpec(memory_space=pl.ANY)],
        out_specs=pl.BlockSpec((2 * bm * TOKEN_ROWS, LANES), lambda s, *_: (s, 0)),
        scratch_shapes=[pltpu.VMEM((2, D, ff), F32), pltpu.VMEM((2, D, ff), F32), pltpu.VMEM((2, ff, D), F32),
                        pltpu.VMEM((D, ff), BF16), pltpu.VMEM((D, ff), BF16), pltpu.VMEM((ff, D), BF16),
                        pltpu.VMEM((bm, D), BF16), pltpu.VMEM((bm, D), BF16),
                        pltpu.SMEM((1,), jnp.int32), pltpu.SemaphoreType.DMA((2, 3))],
    )
    return pl.pallas_call(
        _expert_kernel,
        grid_spec=grid_spec,
        out_shape=jax.ShapeDtypeStruct(buf.shape, F32),
        compiler_params=pltpu.CompilerParams(dimension_semantics=("arbitrary",), vmem_limit_bytes=VMEM_LIMIT),
        name="experts",
    )(block_expert, next_expert, n_used, buf, buf, buf, w_gate, w_up, w_down)


def _final_kernel(dcur_ref, dnxt_ref, x_ref, info_ref, y_ref, p_ref, wpg_ref, bpg_ref, wpp_ref,
                  l2g_ref, l2b_ref, l3g_ref, l3b_ref, o_ref, rows_scr, sem):
    i = pl.program_id(0)
    last = pl.num_programs(0) - 1
    rows = x_ref.shape[0]
    slot = i % 2

    def row_copy(dref, s, r, kk):
        return _tile_copy(y_ref, dref[0, 0, kk * rows + r], rows_scr.at[s, kk], r * TOKEN_ROWS, sem.at[s])

    def landed(s):
        pltpu.make_async_copy(rows_scr.at[s], rows_scr.at[s], sem.at[s]).wait()

    @pl.when(i == 0)
    def _():
        def start(c, carry):
            for u in range(MOVE_UNROLL):
                for kk in range(TOP_K):
                    row_copy(dcur_ref, 0, c * MOVE_UNROLL + u, kk).start(priority=kk)
            return carry

        lax.fori_loop(0, rows // MOVE_UNROLL, start, 0)

    landed(slot)
    info = info_ref[...]
    gate0 = info[:, I_G0:I_G0 + 1]
    gate1 = info[:, I_G1:I_G1 + 1]
    moe = (_from_token_tiles_f32(rows_scr.at[slot, 0], rows) * gate0
           + _from_token_tiles_f32(rows_scr.at[slot, 1], rows) * gate1)

    for r in range(rows):
        for kk in range(TOP_K):
            row_copy(dnxt_ref, 1 - slot, r, kk).start(priority=kk)

    pp = _dot(p_ref[...].astype(BF16), wpp_ref[...])
    x2 = _ln(ALPHA * x_ref[...] + moe, l2g_ref[...], l2b_ref[...])
    gate = jax.nn.sigmoid(_dot(x2.astype(BF16), wpg_ref[...]) + bpg_ref[...])
    o_ref[...] = _ln(ALPHA * x2 + gate * pp, l3g_ref[...], l3b_ref[...])

    @pl.when(i == last)
    def _():
        landed(1 - slot)


def _final(dest3, x1, info, y, p2, w):
    T, D = x1.shape
    tc = MOVE_ROWS
    pd = p2.shape[1]
    full = lambda a: pl.BlockSpec(a.shape, lambda i: (0,) * a.ndim)
    consts = [w["wpg"], w["bpg"], w["wpp"], w["l2g"], w["l2b"], w["l3g"], w["l3b"]]
    last = T // tc - 1
    return pl.pallas_call(
        _final_kernel,
        grid=(T // tc,),
        in_specs=[pl.BlockSpec((1, 1, TOP_K * tc), lambda i: (i, 0, 0), memory_space=pltpu.SMEM),
                  pl.BlockSpec((1, 1, TOP_K * tc), lambda i: (jnp.minimum(i + 1, last), 0, 0), memory_space=pltpu.SMEM),
                  pl.BlockSpec((tc, D), lambda i: (i, 0)),
                  pl.BlockSpec((tc, LANES), lambda i: (i, 0)),
                  pl.BlockSpec(memory_space=pl.ANY),
                  pl.BlockSpec((tc, pd), lambda i: (i, 0))] + [full(a) for a in consts],
        out_specs=pl.BlockSpec((tc, D), lambda i: (i, 0)),
        out_shape=jax.ShapeDtypeStruct((T, D), F32),
        scratch_shapes=[pltpu.VMEM((2, TOP_K, tc * TOKEN_ROWS, LANES), F32), pltpu.SemaphoreType.DMA((2,))],
        compiler_params=pltpu.CompilerParams(dimension_semantics=("arbitrary",), vmem_limit_bytes=VMEM_LIMIT),
        name="final",
    )(dest3, dest3, x1, info, y, p2, *consts)


def _pad_heads(a, width):
    lead = a.shape[:-1]
    a = a.reshape(lead + (MLA_HEADS, width))
    a = jnp.pad(a, [(0, 0)] * len(lead) + [(0, 0), (0, LANES - width)])
    return a.reshape(lead + (HP,))


def _layer_weights(w_in, q_norm_g, w_q_up, kv_norm_g, w_kv_up, gm_ln_g, gm_ln_b, gm_w_s, gm_b_s,
                   mla_out_g, gm_out_g, w_o, ln1_g, ln1_b):
    D = w_in.shape[0]
    half = QK_ROPE // 2
    c1, c2, c3 = Q_RANK, Q_RANK + KV_RANK, Q_RANK + KV_RANK + QK_ROPE
    zeros = lambda *s: jnp.zeros(s, F32)
    kr = jnp.concatenate([zeros(D, QK_NOPE), w_in[:, c2:c3], zeros(D, LANES - QK_NOPE - QK_ROPE)], axis=1)
    win = jnp.concatenate([w_in[:, :c2], kr, w_in[:, c3:]], axis=1).astype(BF16)
    wq = _pad_heads(w_q_up, QK_NOPE + QK_ROPE).astype(BF16)

    wkv3 = w_kv_up.reshape(KV_RANK, MLA_HEADS, QK_NOPE + V_HEAD)
    wk = _pad_heads(wkv3[..., :QK_NOPE].reshape(KV_RANK, -1), QK_NOPE).astype(BF16)
    wv = wkv3[..., QK_NOPE:].reshape(KV_RANK, -1).astype(BF16)

    inv = (ROPE_THETA ** (-jnp.arange(0, QK_ROPE, 2, dtype=F32) / QK_ROPE))[:, None]
    eye = jnp.eye(half, dtype=F32)
    first = jnp.pad(eye, ((0, 0), (QK_NOPE, LANES - QK_NOPE - half)))
    second = jnp.pad(eye, ((0, 0), (QK_NOPE + half, LANES - QK_NOPE - QK_ROPE)))
    zero = jnp.zeros_like(first)
    cos_rows = jnp.concatenate([first + second, zero, zero], axis=1)
    sin_rows = jnp.concatenate([zero, -first, second], axis=1)
    rope = jnp.concatenate([cos_rows, cos_rows, sin_rows, sin_rows], axis=0).astype(BF16)
    lane = jnp.arange(LANES)
    one = jnp.where((lane >= QK_NOPE) & (lane < QK_NOPE + QK_ROPE), 0.0, 1.0)[None, :]

    grp = jnp.arange(GM_OUT) // GM_CH
    gavg = jnp.where(grp[:, None] == grp[None, :], 1.0 / GM_CH, 0.0).astype(BF16)
    bias = jnp.repeat(gm_b_s.T, GM_CH, axis=1)

    woa = w_o[:MLA_OUT].astype(BF16)
    wog = w_o[MLA_OUT:].astype(BF16)
    return dict(win=win, qg=q_norm_g[None, :], wq=wq, kvg=kv_norm_g[None, :], wk=wk, wv=wv, inv=inv, rope=rope, one=one,
                lng=gm_ln_g[None, :], lnb=gm_ln_b[None, :], gavg=gavg, ws=gm_w_s, bias=bias, gog=gm_out_g[None, :],
                woa=woa, wog=wog, mog=mla_out_g[:, None], l1g=ln1_g[None, :], l1b=ln1_b[None, :])


def _moe(x1, w_rg, b_rg, w_re, b_re, w_gate, w_up, w_down):
    T, D = x1.shape
    pad = jnp.zeros((D, LANES - N_GROUPS - N_EXPERTS), F32)
    wr = jnp.concatenate([w_rg, w_re, pad], axis=1)
    br = jnp.concatenate([b_rg, b_re, pad[0]])[None, :]
    info, info_t, cnt = _route(x1, wr, br)

    bm = EXPERT_ROWS
    n_blocks = (T * TOP_K) // bm + N_EXPERTS
    counts = cnt[0, R_OFF:R_OFF + N_EXPERTS].astype(jnp.int32)
    padded = (counts + bm - 1) // bm * bm
    pad_ends = jnp.cumsum(padded)
    pad_starts = pad_ends - padded
    def dest_rows(e_lane, r_lane):
        e = info_t[:, e_lane, :].astype(jnp.int32)
        ids = jnp.arange(N_EXPERTS)[:, None, None]
        seg_start = jnp.sum(jnp.where(e[None] == ids, pad_starts[:, None, None], 0), axis=0)
        return ((seg_start + info_t[:, r_lane, :].astype(jnp.int32)) * TOKEN_ROWS).reshape(T // MOVE_ROWS, MOVE_ROWS)

    dest = jnp.concatenate([dest_rows(I_E0, I_R0), dest_rows(I_E1, I_R1)], axis=1)[:, None, :]
    block_start = jnp.arange(n_blocks, dtype=jnp.int32) * bm
    block_expert = jnp.minimum(jnp.sum(pad_ends[None, :] <= block_start[:, None], axis=1),
                               N_EXPERTS - 1).astype(jnp.int32)

    blk = jnp.arange(n_blocks)
    later = (blk[None, :] > blk[:, None]) & (block_expert[None, :] != block_expert[:, None])
    next_expert = jnp.min(jnp.where(later, block_expert[None, :], N_EXPERTS), axis=1)
    next_expert = jnp.where(next_expert == N_EXPERTS, -1, next_expert).astype(jnp.int32)
    n_used = (pad_ends[-1:] // bm).astype(jnp.int32)

    seg = jnp.stack([pad_ends, padded, jnp.broadcast_to(n_used, (N_EXPERTS,))]).astype(jnp.int32)
    buf = _dispatch(seg, dest, x1, n_blocks * bm)
    y = _experts(block_expert, next_expert, n_used, buf, w_gate, w_up, w_down)
    return info, dest, y


def kernel(x, p, positions, w_in, q_norm_g, w_q_up, kv_norm_g, w_kv_up, gm_ln_g, gm_ln_b, gm_w_s, gm_b_s, mla_out_g, gm_out_g, w_o, ln1_g, ln1_b, w_rg, b_rg, w_re, b_re, w_gate, w_up, w_down, ln2_g, ln2_b, w_pg, b_pg, w_pp, ln3_g, ln3_b):
    B, S, D = x.shape
    T = B * S
    assert S % PREP_ROWS == 0 and PREP_ROWS % ATTN_ROWS == 0 and PREP_ROWS % CHUNK == 0 and S % (ATTN_TILES * ATTN_ROWS) == 0 and ATTN_TILES % 2 == 0
    assert T % ROUTE_ROWS == 0 and T % MOVE_ROWS == 0 and (T * TOP_K) % EXPERT_ROWS == 0
    assert D == TOKEN_ROWS * LANES and MOVE_ROWS % MOVE_UNROLL == 0
    pos4 = positions.reshape(B, S // PREP_ROWS, 1, PREP_ROWS)
    for i in range(DEPTH):
        w = _layer_weights(w_in[i], q_norm_g[i], w_q_up[i], kv_norm_g[i], w_kv_up[i], gm_ln_g[i], gm_ln_b[i],
                           gm_w_s[i], gm_b_s[i], mla_out_g[i], gm_out_g[i], w_o[i], ln1_g[i], ln1_b[i])
        q, k, vt, g = _prep(x, pos4, w)
        x1 = _attn(q, k, vt, g, x, w).reshape(T, D)
        info, dest, y = _moe(x1, w_rg[i], b_rg[i], w_re[i], b_re[i], w_gate[i], w_up[i], w_down[i])
        wf = dict(wpg=w_pg[i].astype(BF16), bpg=b_pg[i][None, :], wpp=w_pp[i].astype(BF16),
                  l2g=ln2_g[i][None, :], l2b=ln2_b[i][None, :], l3g=ln3_g[i][None, :], l3b=ln3_b[i][None, :])
        x = _final(dest, x1, info, y, p[i].reshape(T, -1), wf).reshape(B, S, D)
    return x
```

```python
import functools

import jax
import jax.numpy as jnp
from jax import lax
from jax.experimental import pallas as pl
from jax.experimental.pallas import tpu as pltpu

F32 = jnp.float32
BF16 = jnp.bfloat16

MLA_HEADS = 8
QK_NOPE = 64
QK_ROPE = 32
V_HEAD = 64
Q_RANK = 256
KV_RANK = 128
ROPE_THETA = 10000.0
MLA_OUT = MLA_HEADS * V_HEAD
GM_GROUPS = 8
GM_CH = 64
GM_OUT = GM_GROUPS * GM_CH
CHUNK = 128
N_GROUPS = 4
EXP_PER_GROUP = 8
N_EXPERTS = N_GROUPS * EXP_PER_GROUP
TOP_K = 2
EPS = 1e-6
DEPTH = 1
ALPHA = (2.0 * DEPTH) ** 0.25
SM_SCALE = (QK_NOPE + QK_ROPE) ** -0.5
LOG2E = 1.4426950408889634

LANES = 128
SUBLANES = 8
TOKEN_ROWS = 8
ONES_ROWS = 16
VMEM_LIMIT = 56 * 1024 * 1024

PREP_ROWS = 512
ATTN_ROWS = 256
ATTN_TILES = 4
ROUTE_ROWS = 512
ROUTE_SUB = 512
MOVE_ROWS = 256
MOVE_UNROLL = 8
EXPERT_ROWS = 256

C_Q = 0
C_KV = C_Q + Q_RANK
C_KR = C_KV + KV_RANK
C_U = C_KR + LANES
C_V = C_U + GM_OUT
C_END = C_V + GM_OUT
HP = MLA_HEADS * LANES

I_E0, I_E1, I_R0, I_R1, I_G0, I_G1 = range(6)
R_OFF = N_GROUPS


def _rms(v, g):
    return v * lax.rsqrt(jnp.mean(v * v, axis=-1, keepdims=True) + EPS) * g


def _ln(v, g, b):
    mu = jnp.mean(v, axis=-1, keepdims=True)
    d = v - mu
    var = jnp.mean(d * d, axis=-1, keepdims=True)
    return d * lax.rsqrt(var + EPS) * g + b


def _dot(a, b):
    return jnp.dot(a, b, preferred_element_type=F32)


def _prep_kernel(x_ref, pos_ref, win_ref, qg_ref, wq_ref, kvg_ref, wk_ref, wv_ref, inv_ref, rope_ref, one_ref,
                 lng_ref, lnb_ref, gavg_ref, ws_ref, bias_ref, gog_ref,
                 q_ref, k_ref, vt_ref, g_ref):
    rows = x_ref.shape[1]
    h = _dot(x_ref[0].astype(BF16), win_ref[...])

    ang = inv_ref[...] * pos_ref[0, 0].astype(F32)
    parts = []
    for t in (jnp.cos(ang), jnp.sin(ang)):
        hi = t.astype(BF16).astype(F32)
        parts += [hi, t - hi]
    tabs = _dot(jnp.concatenate(parts, axis=0).T.astype(BF16), rope_ref[...])
    cos_t = tabs[:, :LANES] + one_ref[...]
    sin_a = tabs[:, LANES:2 * LANES]
    sin_b = tabs[:, 2 * LANES:]
    half = QK_ROPE // 2

    def rotate(v):
        return v * cos_t + pltpu.roll(v, LANES - half, 1) * sin_a + pltpu.roll(v, half, 1) * sin_b

    cq = _rms(h[:, C_Q:C_Q + Q_RANK], qg_ref[...]).astype(BF16)
    q2 = _dot(cq, wq_ref[...])
    for hd in range(MLA_HEADS):
        lo = hd * LANES
        q_ref[0, :, lo:lo + LANES] = (rotate(q2[:, lo:lo + LANES]) * (SM_SCALE * LOG2E)).astype(BF16)

    ckv = _rms(h[:, C_KV:C_KV + KV_RANK], kvg_ref[...]).astype(BF16)
    kp = _dot(ckv, wk_ref[...])
    kr = rotate(h[:, C_KR:C_KR + LANES])
    for hd in range(MLA_HEADS):
        lo = hd * LANES
        k_ref[0, :, lo:lo + LANES] = (kp[:, lo:lo + LANES] + kr).astype(BF16)
    vp = _dot(ckv, wv_ref[...])
    for kb in range(rows // ATTN_ROWS):
        vt_ref[0, kb] = vp[kb * ATTN_ROWS:(kb + 1) * ATTN_ROWS].T.astype(BF16)

    u = jax.nn.gelu(h[:, C_U:C_U + GM_OUT])
    vv = jax.nn.gelu(h[:, C_V:C_V + GM_OUT])
    mu = _dot(vv.astype(BF16), gavg_ref[...])
    d = vv - mu
    var = _dot((d * d).astype(BF16), gavg_ref[...])
    vn = (d * lax.rsqrt(var + EPS) * lng_ref[...] + lnb_ref[...]).astype(BF16)

    tri = lax.broadcasted_iota(jnp.int32, (CHUNK, CHUNK), 0) >= lax.broadcasted_iota(jnp.int32, (CHUNK, CHUNK), 1)
    wm = [jnp.where(tri, ws_ref[g], 0.0).astype(BF16) for g in range(GM_GROUPS)]
    low_half = lax.broadcasted_iota(jnp.int32, (CHUNK, LANES), 1) < GM_CH
    for c in range(rows // CHUNK):
        r0 = c * CHUNK
        parts = []
        for pr in range(GM_GROUPS // 2):
            tile = vn[r0:r0 + CHUNK, pr * LANES:(pr + 1) * LANES]
            parts.append(jnp.where(low_half, _dot(wm[2 * pr], tile), _dot(wm[2 * pr + 1], tile)))
        sg = jnp.concatenate(parts, axis=1) + bias_ref[...]
        gm = u[r0:r0 + CHUNK] * sg
        g_ref[0, r0:r0 + CHUNK, :] = _rms(gm, gog_ref[...]).astype(BF16)


def _prep(x, pos4, w):
    B, S, D = x.shape
    ts = PREP_ROWS
    full = lambda a: pl.BlockSpec(a.shape, lambda b, i: (0,) * a.ndim)
    consts = [w["win"], w["qg"], w["wq"], w["kvg"], w["wk"], w["wv"], w["inv"], w["rope"], w["one"],
              w["lng"], w["lnb"], w["gavg"], w["ws"], w["bias"], w["gog"]]
    return pl.pallas_call(
        _prep_kernel,
        grid=(B, S // ts),
        in_specs=[pl.BlockSpec((1, ts, D), lambda b, i: (b, i, 0)),
                  pl.BlockSpec((1, 1, 1, ts), lambda b, i: (b, i, 0, 0))] + [full(a) for a in consts],
        out_specs=[pl.BlockSpec((1, ts, HP), lambda b, i: (b, i, 0)),
                   pl.BlockSpec((1, ts, HP), lambda b, i: (b, i, 0)),
                   pl.BlockSpec((1, ts // ATTN_ROWS, MLA_OUT, ATTN_ROWS), lambda b, i: (b, i, 0, 0)),
                   pl.BlockSpec((1, ts, GM_OUT), lambda b, i: (b, i, 0))],
        out_shape=[jax.ShapeDtypeStruct((B, S, HP), BF16)] * 2
        + [jax.ShapeDtypeStruct((B, S // ATTN_ROWS, MLA_OUT, ATTN_ROWS), BF16),
           jax.ShapeDtypeStruct((B, S, GM_OUT), BF16)],
        compiler_params=pltpu.CompilerParams(dimension_semantics=("parallel", "parallel"),
                                             vmem_limit_bytes=VMEM_LIMIT),
        name="prep",
    )(x, pos4, *consts)


def _attn_kernel(q_ref, k_ref, vt_ref, g_ref, x_ref, woa_ref, wog_ref, mog_ref, l1g_ref, l1b_ref,
                 o_ref, m_scr, acc_scr, sa_scr, sb_scr):
    pid = pl.program_id(1)
    tq = ATTN_ROWS
    tk = tq
    key = lax.broadcasted_iota(jnp.int32, (tk, tq), 0)
    qry = lax.broadcasted_iota(jnp.int32, (tk, tq), 1)
    diag_mask = key <= qry
    ones = jnp.ones((ONES_ROWS, tk), BF16)

    def tile(t):
        r0 = t * tq
        i = ATTN_TILES * pid + t
        odd = t % 2 == 1
        m_scr[...] = jnp.full(m_scr.shape, -1e30, F32)
        acc_scr[...] = jnp.zeros(acc_scr.shape, F32)

        def scores(j, s_scr):
            k0 = pl.multiple_of(j * tk, tk)
            for hd in range(MLA_HEADS):
                lo = hd * LANES
                qh = q_ref[0, r0:r0 + tq, lo:lo + LANES]
                kj = k_ref[0, pl.ds(k0, tk), lo:lo + LANES]
                s_scr[hd] = lax.dot_general(kj, qh, (((1,), (1,)), ((), ())), preferred_element_type=F32)

        def update(j, s_scr, masked):
            for hd in range(MLA_HEADS):
                s = s_scr[hd]
                vt = vt_ref[0, j, hd * V_HEAD:(hd + 1) * V_HEAD, :]
                if masked:
                    s = jnp.where(diag_mask, s, -1e30)
                m_prev = m_scr[hd]
                m_new = jnp.maximum(m_prev, jnp.max(s, axis=0, keepdims=True))
                p = jnp.exp2(s - m_new).astype(BF16)
                scale = jnp.exp2(m_prev - m_new)
                acc_scr[hd] = scale * acc_scr[hd] + _dot(jnp.concatenate([vt, ones], axis=0), p)
                m_scr[hd] = m_new

        def pair(jj, c):
            j = 2 * jj
            scores(j + 1, sb_scr)
            update(j, sa_scr, False)
            scores(j + 2, sa_scr)
            update(j + 1, sb_scr, False)
            return c

        scores(0, sa_scr)
        lax.fori_loop(0, (ATTN_TILES // 2) * pid + t // 2, pair, 0)
        if odd:
            scores(i, sb_scr)
            update(i - 1, sa_scr, False)
            update(i, sb_scr, True)
        else:
            update(i, sa_scr, True)

        at = jnp.concatenate([acc_scr[hd, :V_HEAD] / acc_scr[hd, V_HEAD:V_HEAD + 1] for hd in range(MLA_HEADS)],
                             axis=0)
        at = at * lax.rsqrt(jnp.mean(at * at, axis=0, keepdims=True) + EPS) * mog_ref[...]
        mix = _dot(at.T.astype(BF16), woa_ref[...]) + _dot(g_ref[0, r0:r0 + tq, :], wog_ref[...])
        o_ref[0, r0:r0 + tq, :] = _ln(ALPHA * x_ref[0, r0:r0 + tq, :] + mix, l1g_ref[...], l1b_ref[...])

    for t in range(ATTN_TILES):
        tile(t)


def _attn(q, k, vt, g, x, w):
    B, S, D = x.shape
    tq = ATTN_ROWS
    rows = ATTN_TILES * tq
    full = lambda a: pl.BlockSpec(a.shape, lambda b, i: (0,) * a.ndim)
    consts = [w["woa"], w["wog"], w["mog"], w["l1g"], w["l1b"]]
    return pl.pallas_call(
        _attn_kernel,
        grid=(B, S // rows),
        in_specs=[pl.BlockSpec((1, rows, HP), lambda b, i: (b, i, 0)),
                  pl.BlockSpec((1, S, HP), lambda b, i: (b, 0, 0)),
                  pl.BlockSpec((1,) + vt.shape[1:], lambda b, i: (b, 0, 0, 0)),
                  pl.BlockSpec((1, rows, GM_OUT), lambda b, i: (b, i, 0)),
                  pl.BlockSpec((1, rows, D), lambda b, i: (b, i, 0))] + [full(a) for a in consts],
        out_specs=pl.BlockSpec((1, rows, D), lambda b, i: (b, i, 0)),
        out_shape=jax.ShapeDtypeStruct((B, S, D), F32),
        scratch_shapes=[pltpu.VMEM((MLA_HEADS, 1, tq), F32),
                        pltpu.VMEM((MLA_HEADS, V_HEAD + ONES_ROWS, tq), F32),
                        pltpu.VMEM((MLA_HEADS, tq, tq), F32), pltpu.VMEM((MLA_HEADS, tq, tq), F32)],
        compiler_params=pltpu.CompilerParams(dimension_semantics=("parallel", "parallel"),
                                             vmem_limit_bytes=VMEM_LIMIT),
        name="attn",
    )(q, k, vt, g, x, *consts)


def _route_kernel(x_ref, wr_ref, br_ref, info_ref, infot_ref, cnt_ref, carry_scr, tri_scr):
    step = pl.program_id(0)
    sub = tri_scr.shape[0]

    @pl.when(step == 0)
    def _():
        carry_scr[...] = jnp.zeros_like(carry_scr)
        r = lax.broadcasted_iota(jnp.int32, (sub, sub), 0)
        c = lax.broadcasted_iota(jnp.int32, (sub, sub), 1)
        tri_scr[...] = jnp.where(c < r, 1.0, 0.0).astype(BF16)

    wr = wr_ref[...]
    wh = wr.astype(BF16)
    wl = (wr - wh.astype(F32)).astype(BF16)
    lane = lax.broadcasted_iota(jnp.int32, (sub, LANES), 1)
    neg = jnp.float32(-jnp.inf)
    carry = carry_scr[...]

    for h in range(x_ref.shape[0] // sub):
        r0_, r1_ = h * sub, (h + 1) * sub
        x = x_ref[r0_:r1_, :]
        xh = x.astype(BF16)
        xl = (x - xh.astype(F32)).astype(BF16)
        logits = _dot(xh, wh) + _dot(xl, wh) + _dot(xh, wl) + br_ref[...]

        is_g = lane < N_GROUPS
        lg = jnp.where(is_g, logits, neg)
        gmax = jnp.max(lg, axis=-1, keepdims=True)
        g_idx = jnp.min(jnp.where(lg == gmax, lane, LANES), axis=-1, keepdims=True)
        g_den = jnp.sum(jnp.where(is_g, jnp.exp(lg - gmax), 0.0), axis=-1, keepdims=True)
        g_p = 1.0 / g_den

        in_grp = (lane >= R_OFF) & (lane < R_OFF + N_EXPERTS) & (((lane - R_OFF) >> 3) == g_idx)
        le = jnp.where(in_grp, logits, neg)
        m1 = jnp.max(le, axis=-1, keepdims=True)
        i1 = jnp.min(jnp.where(le == m1, lane, LANES), axis=-1, keepdims=True)
        le2 = jnp.where(lane == i1, neg, le)
        m2 = jnp.max(le2, axis=-1, keepdims=True)
        i2 = jnp.min(jnp.where(le2 == m2, lane, LANES), axis=-1, keepdims=True)
        e2 = jnp.exp(m2 - m1)
        gate0 = g_p / (1.0 + e2)
        gate1 = g_p * e2 / (1.0 + e2)

        hit1 = lane == i1
        hit2 = lane == i2
        onehot = jnp.where(hit1 | hit2, 1.0, 0.0)
        before = _dot(tri_scr[...], onehot.astype(BF16)) + carry
        rank0 = jnp.sum(jnp.where(hit1, before, 0.0), axis=-1, keepdims=True)
        rank1 = jnp.sum(jnp.where(hit2, before, 0.0), axis=-1, keepdims=True)
        carry = carry + jnp.sum(onehot, axis=0, keepdims=True)

        info = jnp.where(lane == I_E0, (i1 - R_OFF).astype(F32), 0.0)
        info = jnp.where(lane == I_E1, (i2 - R_OFF).astype(F32), info)
        info = jnp.where(lane == I_R0, rank0, info)
        info = jnp.where(lane == I_R1, rank1, info)
        info = jnp.where(lane == I_G0, gate0, info)
        info = jnp.where(lane == I_G1, gate1, info)
        info_ref[r0_:r1_, :] = info
        infot_ref[0, :, r0_:r1_] = info.T[:SUBLANES]

    carry_scr[...] = carry
    cnt_ref[...] = carry


def _route(x1, wr, br):
    T, D = x1.shape
    tt = ROUTE_ROWS
    return pl.pallas_call(
        _route_kernel,
        grid=(T // tt,),
        in_specs=[pl.BlockSpec((tt, D), lambda i: (i, 0)),
                  pl.BlockSpec(wr.shape, lambda i: (0, 0)),
                  pl.BlockSpec(br.shape, lambda i: (0, 0))],
        out_specs=[pl.BlockSpec((tt, LANES), lambda i: (i, 0)),
                   pl.BlockSpec((1, SUBLANES, tt), lambda i: (i, 0, 0)),
                   pl.BlockSpec((1, LANES), lambda i: (0, 0))],
        out_shape=[jax.ShapeDtypeStruct((T, LANES), F32), jax.ShapeDtypeStruct((T // tt, SUBLANES, tt), F32),
                   jax.ShapeDtypeStruct((1, LANES), F32)],
        scratch_shapes=[pltpu.VMEM((1, LANES), F32), pltpu.VMEM((ROUTE_SUB, ROUTE_SUB), BF16)],
        compiler_params=pltpu.CompilerParams(dimension_semantics=("arbitrary",), vmem_limit_bytes=VMEM_LIMIT),
        name="route",
    )(x1, wr, br)


def _to_token_tiles(dst_ref, val):
    dst_ref[...] = val.astype(BF16).reshape(dst_ref.shape)


def _from_token_tiles(src_ref, rows):
    return src_ref[...].reshape(rows, TOKEN_ROWS * LANES)


def _to_token_tiles_f32(dst_ref, val):
    rows = val.shape[0]
    for c in range(TOKEN_ROWS):
        dst_ref[pl.ds(c, rows, stride=TOKEN_ROWS), :] = val[:, c * LANES:(c + 1) * LANES]


def _from_token_tiles_f32(src_ref, rows):
    return jnp.concatenate([src_ref[pl.ds(c, rows, stride=TOKEN_ROWS), :] for c in range(TOKEN_ROWS)], axis=1)


def _tile_copy(src_ref, src_row, dst_ref, dst_row, sem):
    return pltpu.make_async_copy(src_ref.at[pl.ds(pl.multiple_of(src_row, TOKEN_ROWS), TOKEN_ROWS)],
                                 dst_ref.at[pl.ds(pl.multiple_of(dst_row, TOKEN_ROWS), TOKEN_ROWS)], sem)


def _dispatch_kernel(seg_ref, dest_ref, x0_ref, xn_ref, buf_ref, stage_scr, zero_scr, sem, zero_sem, *, n_steps):
    i = pl.program_id(0)
    rows = xn_ref.shape[0]
    cur = i % 3
    nxt = (i + 1) % 3

    @pl.when(i == 0)
    def _():
        zero_scr[...] = jnp.zeros(zero_scr.shape, BF16)

        block = EXPERT_ROWS * TOKEN_ROWS
        n_blocks = buf_ref.shape[0] // block

        def clear_rows(first):
            return pltpu.make_async_copy(zero_scr, buf_ref.at[pl.ds(pl.multiple_of(first, SUBLANES), block)], zero_sem)

        def clear(e):
            return clear_rows((seg_ref[0, e] - EXPERT_ROWS) * TOKEN_ROWS)

        def start_tail(b, c):
            clear_rows(b * block).start()
            return c

        def wait_tail(b, c):
            clear_rows(b * block).wait()
            return c

        for e in range(N_EXPERTS):
            pl.when(seg_ref[1, e] > 0)(lambda e=e: clear(e).start())
        lax.fori_loop(seg_ref[2, 0], n_blocks, start_tail, 0)
        for e in range(N_EXPERTS):
            pl.when(seg_ref[1, e] > 0)(lambda e=e: clear(e).wait())
        lax.fori_loop(seg_ref[2, 0], n_blocks, wait_tail, 0)

        _to_token_tiles(stage_scr.at[0], x0_ref[...])

    def drain(s):
        for _ in range(TOP_K):
            pltpu.make_async_copy(stage_scr.at[s], stage_scr.at[s], sem.at[s]).wait()

    @pl.when(i >= 2)
    def _():
        drain(nxt)

    _to_token_tiles(stage_scr.at[nxt], xn_ref[...])
    for r in range(rows):
        for kk in range(TOP_K):
            _tile_copy(stage_scr.at[cur], r * TOKEN_ROWS, buf_ref, dest_ref[0, 0, kk * rows + r],
                       sem.at[cur]).start(priority=kk)

    @pl.when(i == n_steps - 1)
    def _():
        drain(cur)
        if n_steps >= 2:
            drain((i + 2) % 3)


def _dispatch(seg, dest3, x1, n_rows):
    T, D = x1.shape
    td = MOVE_ROWS
    n_steps = T // td
    grid_spec = pltpu.PrefetchScalarGridSpec(
        num_scalar_prefetch=1,
        grid=(n_steps,),
        in_specs=[pl.BlockSpec((1, 1, TOP_K * td), lambda i, seg: (i, 0, 0), memory_space=pltpu.SMEM),
                  pl.BlockSpec((td, D), lambda i, seg: (0, 0)),
                  pl.BlockSpec((td, D), lambda i, seg: (jnp.minimum(i + 1, n_steps - 1), 0))],
        out_specs=pl.BlockSpec(memory_space=pl.ANY),
        scratch_shapes=[pltpu.VMEM((3, td * TOKEN_ROWS, LANES), BF16),
                        pltpu.VMEM((EXPERT_ROWS * TOKEN_ROWS, LANES), BF16),
                        pltpu.SemaphoreType.DMA((3,)), pltpu.SemaphoreType.DMA(())],
    )
    return pl.pallas_call(
        functools.partial(_dispatch_kernel, n_steps=n_steps),
        grid_spec=grid_spec,
        out_shape=jax.ShapeDtypeStruct((n_rows * TOKEN_ROWS, LANES), BF16),
        compiler_params=pltpu.CompilerParams(dimension_semantics=("arbitrary",), vmem_limit_bytes=VMEM_LIMIT),
        name="dispatch",
    )(seg, dest3, x1, x1)


def _expert_kernel(be_ref, ne_ref, nu_ref, buf0_ref, bufa_ref, bufb_ref, wg_hbm, wu_hbm, wd_hbm, y_ref,
                   sg_scr, su_scr, sd_scr, wg_scr, wu_scr, wd_scr, xa_scr, xb_scr, cur_ref, sem):
    step = pl.program_id(0)
    bm = EXPERT_ROWS
    half = bm * TOKEN_ROWS

    def fetch(expert, s):
        return (pltpu.make_async_copy(wg_hbm.at[expert], sg_scr.at[s], sem.at[s, 0]),
                pltpu.make_async_copy(wu_hbm.at[expert], su_scr.at[s], sem.at[s, 1]),
                pltpu.make_async_copy(wd_hbm.at[expert], sd_scr.at[s], sem.at[s, 2]))

    @pl.when(step == 0)
    def _():
        cur_ref[0] = 0
        for c in fetch(be_ref[0], 0):
            c.start()
        xa_scr[...] = _from_token_tiles(buf0_ref, bm)

    def load_weights(blk):
        e = be_ref[blk]

        @pl.when((blk == 0) | (be_ref[jnp.maximum(blk - 1, 0)] != e))
        def _():
            s = cur_ref[0]
            for c in fetch(e, s):
                c.wait()
            wg_scr[...] = sg_scr[s].astype(BF16)
            wu_scr[...] = su_scr[s].astype(BF16)
            wd_scr[...] = sd_scr[s].astype(BF16)
            nxt = ne_ref[blk]

            @pl.when(nxt >= 0)
            def _():
                for c in fetch(nxt, 1 - s):
                    c.start()

            cur_ref[0] = 1 - s

    def run(blk, x_scr, nxt_ref, nxt_scr, out_rows):
        load_weights(blk)

        @pl.when(blk < nu_ref[0])
        def _():
            nxt_scr[...] = _from_token_tiles(nxt_ref, bm)
            xb = x_scr[...]
            hidden = jax.nn.silu(_dot(xb, wg_scr[...])) * _dot(xb, wu_scr[...])
            _to_token_tiles_f32(y_ref.at[out_rows], _dot(hidden.astype(BF16), wd_scr[...]))

        @pl.when(blk >= nu_ref[0])
        def _():
            y_ref[out_rows, :] = jnp.zeros((half, LANES), F32)

    run(2 * step, xa_scr, bufa_ref, xb_scr, pl.ds(0, half))
    run(2 * step + 1, xb_scr, bufb_ref, xa_scr, pl.ds(half, half))


def _experts(block_expert, next_expert, n_used, buf, w_gate, w_up, w_down):
    bm = EXPERT_ROWS
    D, ff = w_gate.shape[1:]
    n_blocks = buf.shape[0] // (bm * TOKEN_ROWS)
    assert n_blocks % 2 == 0
    last = n_blocks - 1
    grid_spec = pltpu.PrefetchScalarGridSpec(
        num_scalar_prefetch=3,
        grid=(n_blocks // 2,),
        in_specs=[pl.BlockSpec((bm * TOKEN_ROWS, LANES), lambda s, *_: (0, 0)),
                  pl.BlockSpec((bm * TOKEN_ROWS, LANES), lambda s, *_: (2 * s + 1, 0)),
                  pl.BlockSpec((bm * TOKEN_ROWS, LANES), lambda s, *_: (jnp.minimum(2 * s + 2, last), 0)),
                  pl.BlockSpec(memory_space=pl.ANY),
                  pl.BlockSpec(memory_space=pl.ANY),
                  pl.BlockSpec(memory_space=pl.ANY)],
        out_specs=pl.BlockSpec((2 * bm * TOKEN_ROWS, LANES), lambda s, *_: (s, 0)),
        scratch_shapes=[pltpu.VMEM((2, D, ff), F32), pltpu.VMEM((2, D, ff), F32), pltpu.VMEM((2, ff, D), F32),
                        pltpu.VMEM((D, ff), BF16), pltpu.VMEM((D, ff), BF16), pltpu.VMEM((ff, D), BF16),
                        pltpu.VMEM((bm, D), BF16), pltpu.VMEM((bm, D), BF16),
                        pltpu.SMEM((1,), jnp.int32), pltpu.SemaphoreType.DMA((2, 3))],
    )
    return pl.pallas_call(
        _expert_kernel,
        grid_spec=grid_spec,
        out_shape=jax.ShapeDtypeStruct(buf.shape, F32),
        compiler_params=pltpu.CompilerParams(dimension_semantics=("arbitrary",), vmem_limit_bytes=VMEM_LIMIT),
        name="experts",
    )(block_expert, next_expert, n_used, buf, buf, buf, w_gate, w_up, w_down)


def _final_kernel(dcur_ref, dnxt_ref, x_ref, info_ref, y_ref, p_ref, wpg_ref, bpg_ref, wpp_ref,
                  l2g_ref, l2b_ref, l3g_ref, l3b_ref, o_ref, rows_scr, sem):
    i = pl.program_id(0)
    last = pl.num_programs(0) - 1
    rows = x_ref.shape[0]
    slot = i % 2

    def row_copy(dref, s, r, kk):
        return _tile_copy(y_ref, dref[0, 0, kk * rows + r], rows_scr.at[s, kk], r * TOKEN_ROWS, sem.at[s])

    def landed(s):
        pltpu.make_async_copy(rows_scr.at[s], rows_scr.at[s], sem.at[s]).wait()

    @pl.when(i == 0)
    def _():
        def start(c, carry):
            for u in range(MOVE_UNROLL):
                for kk in range(TOP_K):
                    row_copy(dcur_ref, 0, c * MOVE_UNROLL + u, kk).start(priority=kk)
            return carry

        lax.fori_loop(0, rows // MOVE_UNROLL, start, 0)

    landed(slot)
    info = info_ref[...]
    gate0 = info[:, I_G0:I_G0 + 1]
    gate1 = info[:, I_G1:I_G1 + 1]
    moe = (_from_token_tiles_f32(rows_scr.at[slot, 0], rows) * gate0
           + _from_token_tiles_f32(rows_scr.at[slot, 1], rows) * gate1)

    for r in range(rows):
        for kk in range(TOP_K):
            row_copy(dnxt_ref, 1 - slot, r, kk).start(priority=kk)

    pp = _dot(p_ref[...].astype(BF16), wpp_ref[...])
    x2 = _ln(ALPHA * x_ref[...] + moe, l2g_ref[...], l2b_ref[...])
    gate = jax.nn.sigmoid(_dot(x2.astype(BF16), wpg_ref[...]) + bpg_ref[...])
    o_ref[...] = _ln(ALPHA * x2 + gate * pp, l3g_ref[...], l3b_ref[...])

    @pl.when(i == last)
    def _():
        landed(1 - slot)


def _final(dest3, x1, info, y, p2, w):
    T, D = x1.shape
    tc = MOVE_ROWS
    pd = p2.shape[1]
    full = lambda a: pl.BlockSpec(a.shape, lambda i: (0,) * a.ndim)
    consts = [w["wpg"], w["bpg"], w["wpp"], w["l2g"], w["l2b"], w["l3g"], w["l3b"]]
    last = T // tc - 1
    return pl.pallas_call(
        _final_kernel,
        grid=(T // tc,),
        in_specs=[pl.BlockSpec((1, 1, TOP_K * tc), lambda i: (i, 0, 0), memory_space=pltpu.SMEM),
                  pl.BlockSpec((1, 1, TOP_K * tc), lambda i: (jnp.minimum(i + 1, last), 0, 0), memory_space=pltpu.SMEM),
                  pl.BlockSpec((tc, D), lambda i: (i, 0)),
                  pl.BlockSpec((tc, LANES), lambda i: (i, 0)),
                  pl.BlockSpec(memory_space=pl.ANY),
                  pl.BlockSpec((tc, pd), lambda i: (i, 0))] + [full(a) for a in consts],
        out_specs=pl.BlockSpec((tc, D), lambda i: (i, 0)),
        out_shape=jax.ShapeDtypeStruct((T, D), F32),
        scratch_shapes=[pltpu.VMEM((2, TOP_K, tc * TOKEN_ROWS, LANES), F32), pltpu.SemaphoreType.DMA((2,))],
        compiler_params=pltpu.CompilerParams(dimension_semantics=("arbitrary",), vmem_limit_bytes=VMEM_LIMIT),
        name="final",
    )(dest3, dest3, x1, info, y, p2, *consts)


def _pad_heads(a, width):
    lead = a.shape[:-1]
    a = a.reshape(lead + (MLA_HEADS, width))
    a = jnp.pad(a, [(0, 0)] * len(lead) + [(0, 0), (0, LANES - width)])
    return a.reshape(lead + (HP,))


def _layer_weights(w_in, q_norm_g, w_q_up, kv_norm_g, w_kv_up, gm_ln_g, gm_ln_b, gm_w_s, gm_b_s,
                   mla_out_g, gm_out_g, w_o, ln1_g, ln1_b):
    D = w_in.shape[0]
    half = QK_ROPE // 2
    c1, c2, c3 = Q_RANK, Q_RANK + KV_RANK, Q_RANK + KV_RANK + QK_ROPE
    zeros = lambda *s: jnp.zeros(s, F32)
    kr = jnp.concatenate([zeros(D, QK_NOPE), w_in[:, c2:c3], zeros(D, LANES - QK_NOPE - QK_ROPE)], axis=1)
    win = jnp.concatenate([w_in[:, :c2], kr, w_in[:, c3:]], axis=1).astype(BF16)
    wq = _pad_heads(w_q_up, QK_NOPE + QK_ROPE).astype(BF16)

    wkv3 = w_kv_up.reshape(KV_RANK, MLA_HEADS, QK_NOPE + V_HEAD)
    wk = _pad_heads(wkv3[..., :QK_NOPE].reshape(KV_RANK, -1), QK_NOPE).astype(BF16)
    wv = wkv3[..., QK_NOPE:].reshape(KV_RANK, -1).astype(BF16)

    inv = (ROPE_THETA ** (-jnp.arange(0, QK_ROPE, 2, dtype=F32) / QK_ROPE))[:, None]
    eye = jnp.eye(half, dtype=F32)
    first = jnp.pad(eye, ((0, 0), (QK_NOPE, LANES - QK_NOPE - half)))
    second = jnp.pad(eye, ((0, 0), (QK_NOPE + half, LANES - QK_NOPE - QK_ROPE)))
    zero = jnp.zeros_like(first)
    cos_rows = jnp.concatenate([first + second, zero, zero], axis=1)
    sin_rows = jnp.concatenate([zero, -first, second], axis=1)
    rope = jnp.concatenate([cos_rows, cos_rows, sin_rows, sin_rows], axis=0).astype(BF16)
    lane = jnp.arange(LANES)
    one = jnp.where((lane >= QK_NOPE) & (lane < QK_NOPE + QK_ROPE), 0.0, 1.0)[None, :]

    grp = jnp.arange(GM_OUT) // GM_CH
    gavg = jnp.where(grp[:, None] == grp[None, :], 1.0 / GM_CH, 0.0).astype(BF16)
    bias = jnp.repeat(gm_b_s.T, GM_CH, axis=1)

    woa = w_o[:MLA_OUT].astype(BF16)
    wog = w_o[MLA_OUT:].astype(BF16)
    return dict(win=win, qg=q_norm_g[None, :], wq=wq, kvg=kv_norm_g[None, :], wk=wk, wv=wv, inv=inv, rope=rope, one=one,
                lng=gm_ln_g[None, :], lnb=gm_ln_b[None, :], gavg=gavg, ws=gm_w_s, bias=bias, gog=gm_out_g[None, :],
                woa=woa, wog=wog, mog=mla_out_g[:, None], l1g=ln1_g[None, :], l1b=ln1_b[None, :])


def _moe(x1, w_rg, b_rg, w_re, b_re, w_gate, w_up, w_down):
    T, D = x1.shape
    pad = jnp.zeros((D, LANES - N_GROUPS - N_EXPERTS), F32)
    wr = jnp.concatenate([w_rg, w_re, pad], axis=1)
    br = jnp.concatenate([b_rg, b_re, pad[0]])[None, :]
    info, info_t, cnt = _route(x1, wr, br)

    bm = EXPERT_ROWS
    n_blocks = (T * TOP_K) // bm + N_EXPERTS
    counts = cnt[0, R_OFF:R_OFF + N_EXPERTS].astype(jnp.int32)
    padded = (counts + bm - 1) // bm * bm
    pad_ends = jnp.cumsum(padded)
    pad_starts = pad_ends - padded
    def dest_rows(e_lane, r_lane):
        e = info_t[:, e_lane, :].astype(jnp.int32)
        ids = jnp.arange(N_EXPERTS)[:, None, None]
        seg_start = jnp.sum(jnp.where(e[None] == ids, pad_starts[:, None, None], 0), axis=0)
        return ((seg_start + info_t[:, r_lane, :].astype(jnp.int32)) * TOKEN_ROWS).reshape(T // MOVE_ROWS, MOVE_ROWS)

    dest = jnp.concatenate([dest_rows(I_E0, I_R0), dest_rows(I_E1, I_R1)], axis=1)[:, None, :]
    block_start = jnp.arange(n_blocks, dtype=jnp.int32) * bm
    block_expert = jnp.minimum(jnp.sum(pad_ends[None, :] <= block_start[:, None], axis=1),
                               N_EXPERTS - 1).astype(jnp.int32)

    blk = jnp.arange(n_blocks)
    later = (blk[None, :] > blk[:, None]) & (block_expert[None, :] != block_expert[:, None])
    next_expert = jnp.min(jnp.where(later, block_expert[None, :], N_EXPERTS), axis=1)
    next_expert = jnp.where(next_expert == N_EXPERTS, -1, next_expert).astype(jnp.int32)
    n_used = (pad_ends[-1:] // bm).astype(jnp.int32)

    seg = jnp.stack([pad_ends, padded, jnp.broadcast_to(n_used, (N_EXPERTS,))]).astype(jnp.int32)
    buf = _dispatch(seg, dest, x1, n_blocks * bm)
    y = _experts(block_expert, next_expert, n_used, buf, w_gate, w_up, w_down)
    return info, dest, y


def kernel(x, p, positions, w_in, q_norm_g, w_q_up, kv_norm_g, w_kv_up, gm_ln_g, gm_ln_b, gm_w_s, gm_b_s, mla_out_g, gm_out_g, w_o, ln1_g, ln1_b, w_rg, b_rg, w_re, b_re, w_gate, w_up, w_down, ln2_g, ln2_b, w_pg, b_pg, w_pp, ln3_g, ln3_b):
    B, S, D = x.shape
    T = B * S
    assert S % PREP_ROWS == 0 and PREP_ROWS % ATTN_ROWS == 0 and PREP_ROWS % CHUNK == 0 and S % (ATTN_TILES * ATTN_ROWS) == 0 and ATTN_TILES % 2 == 0
    assert T % ROUTE_ROWS == 0 and T % MOVE_ROWS == 0 and (T * TOP_K) % EXPERT_ROWS == 0
    assert D == TOKEN_ROWS * LANES and MOVE_ROWS % MOVE_UNROLL == 0
    pos4 = positions.reshape(B, S // PREP_ROWS, 1, PREP_ROWS)
    for i in range(DEPTH):
        w = _layer_weights(w_in[i], q_norm_g[i], w_q_up[i], kv_norm_g[i], w_kv_up[i], gm_ln_g[i], gm_ln_b[i],
                           gm_w_s[i], gm_b_s[i], mla_out_g[i], gm_out_g[i], w_o[i], ln1_g[i], ln1_b[i])
        q, k, vt, g = _prep(x, pos4, w)
        x1 = _attn(q, k, vt, g, x, w).reshape(T, D)
        info, dest, y = _moe(x1, w_rg[i], b_rg[i], w_re[i], b_re[i], w_gate[i], w_up[i], w_down[i])
        wf = dict(wpg=w_pg[i].astype(BF16), bpg=b_pg[i][None, :], wpp=w_pp[i].astype(BF16),
                  l2g=ln2_g[i][None, :], l2b=ln2_b[i][None, :], l3g=ln3_g[i][None, :], l3b=ln3_b[i][None, :])
        x = _final(dest, x1, info, y, p[i].reshape(T, -1), wf).reshape(B, S, D)
    return x
```

```python
import functools

import jax
import jax.numpy as jnp
from jax import lax
from jax.experimental import pallas as pl
from jax.experimental.pallas import tpu as pltpu

F32 = jnp.float32
BF16 = jnp.bfloat16

MLA_HEADS = 8
QK_NOPE = 64
QK_ROPE = 32
V_HEAD = 64
Q_RANK = 256
KV_RANK = 128
ROPE_THETA = 10000.0
MLA_OUT = MLA_HEADS * V_HEAD
GM_GROUPS = 8
GM_CH = 64
GM_OUT = GM_GROUPS * GM_CH
CHUNK = 128
N_GROUPS = 4
EXP_PER_GROUP = 8
N_EXPERTS = N_GROUPS * EXP_PER_GROUP
TOP_K = 2
EPS = 1e-6
DEPTH = 1
ALPHA = (2.0 * DEPTH) ** 0.25
SM_SCALE = (QK_NOPE + QK_ROPE) ** -0.5
LOG2E = 1.4426950408889634
MASK_VALUE = -1e30

LANES = 128
SUBLANES = 8
TOKEN_ROWS = 8
ONES_ROWS = 16
VMEM_LIMIT = 56 * 1024 * 1024

PREP_ROWS = 512
ATTN_ROWS = 256
ATTN_TILES = 4
ROUTE_ROWS = 512
ROUTE_SUB = 512
MOVE_ROWS = 256
EXPERT_ROWS = 256
RUN_ROWS = 16
RUN_COPIES = N_EXPERTS + TOP_K * MOVE_ROWS // RUN_ROWS

C_Q = 0
C_KV = C_Q + Q_RANK
C_KR = C_KV + KV_RANK
C_U = C_KR + LANES
C_V = C_U + GM_OUT
C_END = C_V + GM_OUT
HP = MLA_HEADS * LANES

I_E0, I_E1, I_R0, I_R1, I_G0, I_G1 = range(6)
R_OFF = N_GROUPS


def _rms(v, g):
    return v * lax.rsqrt(jnp.mean(v * v, axis=-1, keepdims=True) + EPS) * g


def _ln(v, g, b):
    mu = jnp.mean(v, axis=-1, keepdims=True)
    d = v - mu
    var = jnp.mean(d * d, axis=-1, keepdims=True)
    return d * lax.rsqrt(var + EPS) * g + b


def _dot(a, b):
    return jnp.dot(a, b, preferred_element_type=F32)


def _prep_kernel(x_ref, pos_ref, win_ref, qg_ref, wq_ref, kvg_ref, wk_ref, wv_ref, inv_ref, rope_ref, one_ref,
                 lng_ref, lnb_ref, gavg_ref, ws_ref, bias_ref, gog_ref,
                 q_ref, k_ref, vt_ref, g_ref):
    rows = x_ref.shape[1]
    h = _dot(x_ref[0].astype(BF16), win_ref[...])

    ang = inv_ref[...] * pos_ref[0, 0].astype(F32)
    parts = []
    for t in (jnp.cos(ang), jnp.sin(ang)):
        hi = t.astype(BF16).astype(F32)
        parts += [hi, t - hi]
    tabs = _dot(jnp.concatenate(parts, axis=0).T.astype(BF16), rope_ref[...])
    cos_t = tabs[:, :LANES] + one_ref[...]
    sin_a = tabs[:, LANES:2 * LANES]
    sin_b = tabs[:, 2 * LANES:]
    half = QK_ROPE // 2

    def rotate(v):
        return v * cos_t + pltpu.roll(v, LANES - half, 1) * sin_a + pltpu.roll(v, half, 1) * sin_b

    cq = _rms(h[:, C_Q:C_Q + Q_RANK], qg_ref[...]).astype(BF16)
    q2 = _dot(cq, wq_ref[...])
    for hd in range(MLA_HEADS):
        lo = hd * LANES
        q_ref[0, :, lo:lo + LANES] = (rotate(q2[:, lo:lo + LANES]) * (SM_SCALE * LOG2E)).astype(BF16)

    ckv = _rms(h[:, C_KV:C_KV + KV_RANK], kvg_ref[...]).astype(BF16)
    kp = _dot(ckv, wk_ref[...])
    kr = rotate(h[:, C_KR:C_KR + LANES])
    for hd in range(MLA_HEADS):
        lo = hd * LANES
        k_ref[0, :, lo:lo + LANES] = (kp[:, lo:lo + LANES] + kr).astype(BF16)
    vp = _dot(ckv, wv_ref[...])
    for kb in range(rows // ATTN_ROWS):
        vt_ref[0, kb] = vp[kb * ATTN_ROWS:(kb + 1) * ATTN_ROWS].T.astype(BF16)

    u = jax.nn.gelu(h[:, C_U:C_U + GM_OUT])
    vv = jax.nn.gelu(h[:, C_V:C_V + GM_OUT])
    mu = _dot(vv.astype(BF16), gavg_ref[...])
    d = vv - mu
    var = _dot((d * d).astype(BF16), gavg_ref[...])
    vn = (d * lax.rsqrt(var + EPS) * lng_ref[...] + lnb_ref[...]).astype(BF16)

    tri = lax.broadcasted_iota(jnp.int32, (CHUNK, CHUNK), 0) >= lax.broadcasted_iota(jnp.int32, (CHUNK, CHUNK), 1)
    wm = [jnp.where(tri, ws_ref[g], 0.0).astype(BF16) for g in range(GM_GROUPS)]
    low_half = lax.broadcasted_iota(jnp.int32, (CHUNK, LANES), 1) < GM_CH
    for c in range(rows // CHUNK):
        r0 = c * CHUNK
        parts = []
        for pr in range(GM_GROUPS // 2):
            tile = vn[r0:r0 + CHUNK, pr * LANES:(pr + 1) * LANES]
            parts.append(jnp.where(low_half, _dot(wm[2 * pr], tile), _dot(wm[2 * pr + 1], tile)))
        sg = jnp.concatenate(parts, axis=1) + bias_ref[...]
        gm = u[r0:r0 + CHUNK] * sg
        g_ref[0, r0:r0 + CHUNK, :] = _rms(gm, gog_ref[...]).astype(BF16)


def _prep(x, pos4, w):
    B, S, D = x.shape
    ts = PREP_ROWS
    full = lambda a: pl.BlockSpec(a.shape, lambda b, i: (0,) * a.ndim)
    consts = [w["win"], w["qg"], w["wq"], w["kvg"], w["wk"], w["wv"], w["inv"], w["rope"], w["one"],
              w["lng"], w["lnb"], w["gavg"], w["ws"], w["bias"], w["gog"]]
    return pl.pallas_call(
        _prep_kernel,
        grid=(B, S // ts),
        in_specs=[pl.BlockSpec((1, ts, D), lambda b, i: (b, i, 0)),
                  pl.BlockSpec((1, 1, 1, ts), lambda b, i: (b, i, 0, 0))] + [full(a) for a in consts],
        out_specs=[pl.BlockSpec((1, ts, HP), lambda b, i: (b, i, 0)),
                   pl.BlockSpec((1, ts, HP), lambda b, i: (b, i, 0)),
                   pl.BlockSpec((1, ts // ATTN_ROWS, MLA_OUT, ATTN_ROWS), lambda b, i: (b, i, 0, 0)),
                   pl.BlockSpec((1, ts, GM_OUT), lambda b, i: (b, i, 0))],
        out_shape=[jax.ShapeDtypeStruct((B, S, HP), BF16)] * 2
        + [jax.ShapeDtypeStruct((B, S // ATTN_ROWS, MLA_OUT, ATTN_ROWS), BF16),
           jax.ShapeDtypeStruct((B, S, GM_OUT), BF16)],
        compiler_params=pltpu.CompilerParams(dimension_semantics=("parallel", "parallel"),
                                             vmem_limit_bytes=VMEM_LIMIT),
        name="prep",
    )(x, pos4, *consts)


def _attn_kernel(q_ref, k_ref, vt_ref, g_ref, x_ref, woa_ref, wog_ref, mog_ref, l1g_ref, l1b_ref,
                 o_ref, m_scr, acc_scr, sa_scr, sb_scr):
    pid = pl.program_id(1)
    tq = ATTN_ROWS
    tk = tq
    key = lax.broadcasted_iota(jnp.int32, (tk, tq), 0)
    qry = lax.broadcasted_iota(jnp.int32, (tk, tq), 1)
    diag_mask = key <= qry
    ones = jnp.ones((ONES_ROWS, tk), BF16)

    def tile(t):
        r0 = t * tq
        i = ATTN_TILES * pid + t
        odd = t % 2 == 1
        m_scr[...] = jnp.full(m_scr.shape, MASK_VALUE, F32)
        acc_scr[...] = jnp.zeros(acc_scr.shape, F32)

        def scores(j, s_scr):
            k0 = pl.multiple_of(j * tk, tk)
            for hd in range(MLA_HEADS):
                lo = hd * LANES
                qh = q_ref[0, r0:r0 + tq, lo:lo + LANES]
                kj = k_ref[0, pl.ds(k0, tk), lo:lo + LANES]
                s_scr[hd] = lax.dot_general(kj, qh, (((1,), (1,)), ((), ())), preferred_element_type=F32)

        def update(j, s_scr, masked):
            for hd in range(MLA_HEADS):
                s = s_scr[hd]
                vt = vt_ref[0, j, hd * V_HEAD:(hd + 1) * V_HEAD, :]
                if masked:
                    s = jnp.where(diag_mask, s, MASK_VALUE)
                m_prev = m_scr[hd]
                m_new = jnp.maximum(m_prev, jnp.max(s, axis=0, keepdims=True))
                p = jnp.exp2(s - m_new).astype(BF16)
                scale = jnp.exp2(m_prev - m_new)
                acc_scr[hd] = scale * acc_scr[hd] + _dot(jnp.concatenate([vt, ones], axis=0), p)
                m_scr[hd] = m_new

        def pair(jj, c):
            j = 2 * jj
            scores(j + 1, sb_scr)
            update(j, sa_scr, False)
            scores(j + 2, sa_scr)
            update(j + 1, sb_scr, False)
            return c

        scores(0, sa_scr)
        lax.fori_loop(0, (ATTN_TILES // 2) * pid + t // 2, pair, 0)
        if odd:
            scores(i, sb_scr)
            update(i - 1, sa_scr, False)
            update(i, sb_scr, True)
        else:
            update(i, sa_scr, True)

        at = jnp.concatenate([acc_scr[hd, :V_HEAD] / acc_scr[hd, V_HEAD:V_HEAD + 1] for hd in range(MLA_HEADS)],
                             axis=0)
        at = at * lax.rsqrt(jnp.mean(at * at, axis=0, keepdims=True) + EPS) * mog_ref[...]
        mix = _dot(at.T.astype(BF16), woa_ref[...]) + _dot(g_ref[0, r0:r0 + tq, :], wog_ref[...])
        o_ref[0, r0:r0 + tq, :] = _ln(ALPHA * x_ref[0, r0:r0 + tq, :] + mix, l1g_ref[...], l1b_ref[...])

    for t in range(ATTN_TILES):
        tile(t)


def _attn(q, k, vt, g, x, w):
    B, S, D = x.shape
    tq = ATTN_ROWS
    rows = ATTN_TILES * tq
    full = lambda a: pl.BlockSpec(a.shape, lambda b, i: (0,) * a.ndim)
    consts = [w["woa"], w["wog"], w["mog"], w["l1g"], w["l1b"]]
    return pl.pallas_call(
        _attn_kernel,
        grid=(B, S // rows),
        in_specs=[pl.BlockSpec((1, rows, HP), lambda b, i: (b, i, 0)),
                  pl.BlockSpec((1, S, HP), lambda b, i: (b, 0, 0)),
                  pl.BlockSpec((1,) + vt.shape[1:], lambda b, i: (b, 0, 0, 0)),
                  pl.BlockSpec((1, rows, GM_OUT), lambda b, i: (b, i, 0)),
                  pl.BlockSpec((1, rows, D), lambda b, i: (b, i, 0))] + [full(a) for a in consts],
        out_specs=pl.BlockSpec((1, rows, D), lambda b, i: (b, i, 0)),
        out_shape=jax.ShapeDtypeStruct((B, S, D), F32),
        scratch_shapes=[pltpu.VMEM((MLA_HEADS, 1, tq), F32),
                        pltpu.VMEM((MLA_HEADS, V_HEAD + ONES_ROWS, tq), F32),
                        pltpu.VMEM((MLA_HEADS, tq, tq), F32), pltpu.VMEM((MLA_HEADS, tq, tq), F32)],
        compiler_params=pltpu.CompilerParams(dimension_semantics=("parallel", "parallel"),
                                             vmem_limit_bytes=VMEM_LIMIT),
        name="attn",
    )(q, k, vt, g, x, *consts)


def _route_kernel(x_ref, wr_ref, br_ref, info_ref, infot_ref, stat_ref, cnt_ref, carry_scr, tri_scr):
    step = pl.program_id(0)
    sub = tri_scr.shape[0]

    @pl.when(step == 0)
    def _():
        carry_scr[...] = jnp.zeros_like(carry_scr)
        r = lax.broadcasted_iota(jnp.int32, (sub, sub), 0)
        c = lax.broadcasted_iota(jnp.int32, (sub, sub), 1)
        tri_scr[...] = jnp.where(c < r, 1.0, 0.0).astype(BF16)

    wr = wr_ref[...]
    wh = wr.astype(BF16)
    wl = (wr - wh.astype(F32)).astype(BF16)
    lane = lax.broadcasted_iota(jnp.int32, (sub, LANES), 1)
    neg = jnp.float32(-jnp.inf)
    carry = carry_scr[...]

    for h in range(x_ref.shape[0] // sub):
        r0_, r1_ = h * sub, (h + 1) * sub
        x = x_ref[r0_:r1_, :]
        xh = x.astype(BF16)
        xl = (x - xh.astype(F32)).astype(BF16)
        logits = _dot(xh, wh) + _dot(xl, wh) + _dot(xh, wl) + br_ref[...]

        is_g = lane < N_GROUPS
        lg = jnp.where(is_g, logits, neg)
        gmax = jnp.max(lg, axis=-1, keepdims=True)
        g_idx = jnp.min(jnp.where(lg == gmax, lane, LANES), axis=-1, keepdims=True)
        g_den = jnp.sum(jnp.where(is_g, jnp.exp(lg - gmax), 0.0), axis=-1, keepdims=True)
        g_p = 1.0 / g_den

        in_grp = (lane >= R_OFF) & (lane < R_OFF + N_EXPERTS) & (((lane - R_OFF) >> 3) == g_idx)
        le = jnp.where(in_grp, logits, neg)
        m1 = jnp.max(le, axis=-1, keepdims=True)
        i1 = jnp.min(jnp.where(le == m1, lane, LANES), axis=-1, keepdims=True)
        le2 = jnp.where(lane == i1, neg, le)
        m2 = jnp.max(le2, axis=-1, keepdims=True)
        i2 = jnp.min(jnp.where(le2 == m2, lane, LANES), axis=-1, keepdims=True)
        e2 = jnp.exp(m2 - m1)
        gate0 = g_p / (1.0 + e2)
        gate1 = g_p * e2 / (1.0 + e2)

        hit1 = lane == i1
        hit2 = lane == i2
        onehot = jnp.where(hit1 | hit2, 1.0, 0.0)
        before = _dot(tri_scr[...], onehot.astype(BF16)) + carry
        rank0 = jnp.sum(jnp.where(hit1, before, 0.0), axis=-1, keepdims=True)
        rank1 = jnp.sum(jnp.where(hit2, before, 0.0), axis=-1, keepdims=True)
        for mt in range(sub // MOVE_ROWS):
            inside = jnp.sum(onehot[mt * MOVE_ROWS:(mt + 1) * MOVE_ROWS], axis=0, keepdims=True)
            stat_ref[r0_ // MOVE_ROWS + mt] = jnp.concatenate(
                [carry, inside, jnp.zeros((SUBLANES - 2, LANES), F32)], axis=0)
            carry = carry + inside

        info = jnp.where(lane == I_E0, (i1 - R_OFF).astype(F32), 0.0)
        info = jnp.where(lane == I_E1, (i2 - R_OFF).astype(F32), info)
        info = jnp.where(lane == I_R0, rank0, info)
        info = jnp.where(lane == I_R1, rank1, info)
        info = jnp.where(lane == I_G0, gate0, info)
        info = jnp.where(lane == I_G1, gate1, info)
        info_ref[r0_:r1_, :] = info
        infot_ref[0, :, r0_:r1_] = info.T[:SUBLANES]

    carry_scr[...] = carry
    cnt_ref[...] = carry


def _route(x1, wr, br):
    T, D = x1.shape
    tt = ROUTE_ROWS
    return pl.pallas_call(
        _route_kernel,
        grid=(T // tt,),
        in_specs=[pl.BlockSpec((tt, D), lambda i: (i, 0)),
                  pl.BlockSpec(wr.shape, lambda i: (0, 0)),
                  pl.BlockSpec(br.shape, lambda i: (0, 0))],
        out_specs=[pl.BlockSpec((tt, LANES), lambda i: (i, 0)),
                   pl.BlockSpec((1, SUBLANES, tt), lambda i: (i, 0, 0)),
                   pl.BlockSpec((tt // MOVE_ROWS, SUBLANES, LANES), lambda i: (i, 0, 0)),
                   pl.BlockSpec((1, LANES), lambda i: (0, 0))],
        out_shape=[jax.ShapeDtypeStruct((T, LANES), F32), jax.ShapeDtypeStruct((T // tt, SUBLANES, tt), F32),
                   jax.ShapeDtypeStruct((T // MOVE_ROWS, SUBLANES, LANES), F32),
                   jax.ShapeDtypeStruct((1, LANES), F32)],
        scratch_shapes=[pltpu.VMEM((1, LANES), F32), pltpu.VMEM((ROUTE_SUB, ROUTE_SUB), BF16)],
        compiler_params=pltpu.CompilerParams(dimension_semantics=("arbitrary",), vmem_limit_bytes=VMEM_LIMIT),
        name="route",
    )(x1, wr, br)


def _to_token_tiles(dst_ref, val):
    dst_ref[...] = val.astype(BF16).reshape(dst_ref.shape)


def _from_token_tiles(src_ref, rows):
    return src_ref[...].reshape(rows, TOKEN_ROWS * LANES)


def _to_token_tiles_f32(dst_ref, val):
    rows = val.shape[0]
    for c in range(TOKEN_ROWS):
        dst_ref[pl.ds(c, rows, stride=TOKEN_ROWS), :] = val[:, c * LANES:(c + 1) * LANES]


def _from_token_tiles_f32(src_ref, rows):
    return jnp.concatenate([src_ref[pl.ds(c, rows, stride=TOKEN_ROWS), :] for c in range(TOKEN_ROWS)], axis=1)


def _tile_copy(src_ref, src_row, dst_ref, dst_row, sem):
    return pltpu.make_async_copy(src_ref.at[pl.ds(pl.multiple_of(src_row, TOKEN_ROWS), TOKEN_ROWS)],
                                 dst_ref.at[pl.ds(pl.multiple_of(dst_row, TOKEN_ROWS), TOKEN_ROWS)], sem)


def _dispatch_kernel(seg_ref, dest_ref, x0_ref, xn_ref, buf_ref, stage_scr, zero_scr, sem, zero_sem, *, n_steps):
    i = pl.program_id(0)
    rows = xn_ref.shape[0]
    cur = i % 3
    nxt = (i + 1) % 3

    @pl.when(i == 0)
    def _():
        zero_scr[...] = jnp.zeros(zero_scr.shape, BF16)

        block = EXPERT_ROWS * TOKEN_ROWS
        n_blocks = buf_ref.shape[0] // block

        def clear_rows(first):
            return pltpu.make_async_copy(zero_scr, buf_ref.at[pl.ds(pl.multiple_of(first, SUBLANES), block)], zero_sem)

        def clear(e):
            return clear_rows((seg_ref[0, e] - EXPERT_ROWS) * TOKEN_ROWS)

        def start_tail(b, c):
            clear_rows(b * block).start()
            return c

        def wait_tail(b, c):
            clear_rows(b * block).wait()
            return c

        for e in range(N_EXPERTS):
            pl.when(seg_ref[1, e] > 0)(lambda e=e: clear(e).start())
        lax.fori_loop(seg_ref[2, 0], n_blocks, start_tail, 0)
        for e in range(N_EXPERTS):
            pl.when(seg_ref[1, e] > 0)(lambda e=e: clear(e).wait())
        lax.fori_loop(seg_ref[2, 0], n_blocks, wait_tail, 0)

        _to_token_tiles(stage_scr.at[0], x0_ref[...])

    def drain(s):
        for _ in range(TOP_K):
            pltpu.make_async_copy(stage_scr.at[s], stage_scr.at[s], sem.at[s]).wait()

    @pl.when(i >= 2)
    def _():
        drain(nxt)

    _to_token_tiles(stage_scr.at[nxt], xn_ref[...])
    for r in range(rows):
        for kk in range(TOP_K):
            _tile_copy(stage_scr.at[cur], r * TOKEN_ROWS, buf_ref, dest_ref[0, 0, kk * rows + r],
                       sem.at[cur]).start(priority=kk)

    @pl.when(i == n_steps - 1)
    def _():
        drain(cur)
        if n_steps >= 2:
            drain((i + 2) % 3)


def _dispatch(seg, dest3, x1, n_rows):
    T, D = x1.shape
    td = MOVE_ROWS
    n_steps = T // td
    grid_spec = pltpu.PrefetchScalarGridSpec(
        num_scalar_prefetch=1,
        grid=(n_steps,),
        in_specs=[pl.BlockSpec((1, 1, TOP_K * td), lambda i, seg: (i, 0, 0), memory_space=pltpu.SMEM),
                  pl.BlockSpec((td, D), lambda i, seg: (0, 0)),
                  pl.BlockSpec((td, D), lambda i, seg: (jnp.minimum(i + 1, n_steps - 1), 0))],
        out_specs=pl.BlockSpec(memory_space=pl.ANY),
        scratch_shapes=[pltpu.VMEM((3, td * TOKEN_ROWS, LANES), BF16),
                        pltpu.VMEM((EXPERT_ROWS * TOKEN_ROWS, LANES), BF16),
                        pltpu.SemaphoreType.DMA((3,)), pltpu.SemaphoreType.DMA(())],
    )
    return pl.pallas_call(
        functools.partial(_dispatch_kernel, n_steps=n_steps),
        grid_spec=grid_spec,
        out_shape=jax.ShapeDtypeStruct((n_rows * TOKEN_ROWS, LANES), BF16),
        compiler_params=pltpu.CompilerParams(dimension_semantics=("arbitrary",), vmem_limit_bytes=VMEM_LIMIT),
        name="dispatch",
    )(seg, dest3, x1, x1)


def _expert_kernel(be_ref, ne_ref, nu_ref, buf0_ref, bufa_ref, bufb_ref, wg_hbm, wu_hbm, wd_hbm, y_ref,
                   sg_scr, su_scr, sd_scr, wg_scr, wu_scr, wd_scr, xa_scr, xb_scr, cur_ref, sem):
    step = pl.program_id(0)
    bm = EXPERT_ROWS
    half = bm * TOKEN_ROWS

    def fetch(expert, s):
        return (pltpu.make_async_copy(wg_hbm.at[expert], sg_scr.at[s], sem.at[s, 0]),
                pltpu.make_async_copy(wu_hbm.at[expert], su_scr.at[s], sem.at[s, 1]),
                pltpu.make_async_copy(wd_hbm.at[expert], sd_scr.at[s], sem.at[s, 2]))

    @pl.when(step == 0)
    def _():
        cur_ref[0] = 0
        for c in fetch(be_ref[0], 0):
            c.start()
        xa_scr[...] = _from_token_tiles(buf0_ref, bm)

    def load_weights(blk):
        e = be_ref[blk]

        @pl.when((blk == 0) | (be_ref[jnp.maximum(blk - 1, 0)] != e))
        def _():
            s = cur_ref[0]
            for c in fetch(e, s):
                c.wait()
            wg_scr[...] = sg_scr[s].astype(BF16)
            wu_scr[...] = su_scr[s].astype(BF16)
            wd_scr[...] = sd_scr[s].astype(BF16)
            nxt = ne_ref[blk]

            @pl.when(nxt >= 0)
            def _():
                for c in fetch(nxt, 1 - s):
                    c.start()

            cur_ref[0] = 1 - s

    def run(blk, x_scr, nxt_ref, nxt_scr, out_rows):
        load_weights(blk)

        @pl.when(blk < nu_ref[0])
        def _():
            nxt_scr[...] = _from_token_tiles(nxt_ref, bm)
            xb = x_scr[...]
            hidden = jax.nn.silu(_dot(xb, wg_scr[...])) * _dot(xb, wu_scr[...])
            _to_token_tiles_f32(y_ref.at[out_rows], _dot(hidden.astype(BF16), wd_scr[...]))

        @pl.when(blk >= nu_ref[0])
        def _():
            y_ref[out_rows, :] = jnp.zeros((half, LANES), F32)

    run(2 * step, xa_scr, bufa_ref, xb_scr, pl.ds(0, half))
    run(2 * step + 1, xb_scr, bufb_ref, xa_scr, pl.ds(half, half))


def _experts(block_expert, next_expert, n_used, buf, w_gate, w_up, w_down):
    bm = EXPERT_ROWS
    D, ff = w_gate.shape[1:]
    n_blocks = buf.shape[0] // (bm * TOKEN_ROWS)
    assert n_blocks % 2 == 0
    last = n_blocks - 1
    grid_spec = pltpu.PrefetchScalarGridSpec(
        num_scalar_prefetch=3,
        grid=(n_blocks // 2,),
        in_specs=[pl.BlockSpec((bm * TOKEN_ROWS, LANES), lambda s, *_: (0, 0)),
                  pl.BlockSpec((bm * TOKEN_ROWS, LANES), lambda s, *_: (2 * s + 1, 0)),
                  pl.BlockSpec((bm * TOKEN_ROWS, LANES), lambda s, *_: (jnp.minimum(2 * s + 2, last), 0)),
                  pl.BlockSpec(memory_space=pl.ANY),
                  pl.BlockSpec(memory_space=pl.ANY),
                  pl.BlockSpec(memory_space=pl.ANY)],
        out_specs=pl.BlockSpec((2 * bm * TOKEN_ROWS, LANES), lambda s, *_: (s, 0)),
        scratch_shapes=[pltpu.VMEM((2, D, ff), F32), pltpu.VMEM((2, D, ff), F32), pltpu.VMEM((2, ff, D), F32),
                        pltpu.VMEM((D, ff), BF16), pltpu.VMEM((D, ff), BF16), pltpu.VMEM((ff, D), BF16),
                        pltpu.VMEM((bm, D), BF16), pltpu.VMEM((bm, D), BF16),
                        pltpu.SMEM((1,), jnp.int32), pltpu.SemaphoreType.DMA((2, 3))],
    )
    return pl.pallas_call(
        _expert_kernel,
        grid_spec=grid_spec,
        out_shape=jax.ShapeDtypeStruct(buf.shape, F32),
        compiler_params=pltpu.CompilerParams(dimension_semantics=("arbitrary",), vmem_limit_bytes=VMEM_LIMIT),
        name="experts",
    )(block_expert, next_expert, n_used, buf, buf, buf, w_gate, w_up, w_down)


def _final_kernel(scur_ref, snxt_ref, pos_ref, x_ref, info_ref, y_ref, p_ref, wpg_ref, bpg_ref, wpp_ref,
                  l2g_ref, l2b_ref, l3g_ref, l3b_ref, o_ref, stage_scr, rows_scr, sem):
    i = pl.program_id(0)
    last = pl.num_programs(0) - 1
    rows = x_ref.shape[0]
    slot = i % 2
    run = RUN_ROWS * TOKEN_ROWS

    def run_copy(sref, s, j):
        src = pl.multiple_of(sref[0, 0, j], TOKEN_ROWS)
        return pltpu.make_async_copy(y_ref.at[pl.ds(src, run)], stage_scr.at[s, pl.ds(j * run, run)], sem.at[s])

    def landed(s):
        pltpu.make_async_copy(stage_scr.at[s], stage_scr.at[s], sem.at[s]).wait()

    @pl.when(i == 0)
    def _():
        for j in range(RUN_COPIES):
            run_copy(scur_ref, 0, j).start(priority=j % 2)

    landed(slot)
    for r in range(rows):
        for kk in range(TOP_K):
            pos = pl.multiple_of(pos_ref[0, 0, kk * rows + r], TOKEN_ROWS)
            rows_scr[kk, r * TOKEN_ROWS:(r + 1) * TOKEN_ROWS, :] = stage_scr[slot, pl.ds(pos, TOKEN_ROWS), :]

    for j in range(RUN_COPIES):
        run_copy(snxt_ref, 1 - slot, j).start(priority=j % 2)

    info = info_ref[...]
    gate0 = info[:, I_G0:I_G0 + 1]
    gate1 = info[:, I_G1:I_G1 + 1]
    moe = (_from_token_tiles_f32(rows_scr.at[0], rows) * gate0
           + _from_token_tiles_f32(rows_scr.at[1], rows) * gate1)
    pp = _dot(p_ref[...].astype(BF16), wpp_ref[...])
    x2 = _ln(ALPHA * x_ref[...] + moe, l2g_ref[...], l2b_ref[...])
    gate = jax.nn.sigmoid(_dot(x2.astype(BF16), wpg_ref[...]) + bpg_ref[...])
    o_ref[...] = _ln(ALPHA * x2 + gate * pp, l3g_ref[...], l3b_ref[...])

    @pl.when(i == last)
    def _():
        landed(1 - slot)


def _final(src3, pos3, x1, info, y, p2, w):
    T, D = x1.shape
    tc = MOVE_ROWS
    pd = p2.shape[1]
    full = lambda a: pl.BlockSpec(a.shape, lambda i: (0,) * a.ndim)
    consts = [w["wpg"], w["bpg"], w["wpp"], w["l2g"], w["l2b"], w["l3g"], w["l3b"]]
    last = T // tc - 1
    return pl.pallas_call(
        _final_kernel,
        grid=(T // tc,),
        in_specs=[pl.BlockSpec((1, 1, RUN_COPIES), lambda i: (i, 0, 0), memory_space=pltpu.SMEM),
                  pl.BlockSpec((1, 1, RUN_COPIES), lambda i: (jnp.minimum(i + 1, last), 0, 0), memory_space=pltpu.SMEM),
                  pl.BlockSpec((1, 1, TOP_K * tc), lambda i: (i, 0, 0), memory_space=pltpu.SMEM),
                  pl.BlockSpec((tc, D), lambda i: (i, 0)),
                  pl.BlockSpec((tc, LANES), lambda i: (i, 0)),
                  pl.BlockSpec(memory_space=pl.ANY),
                  pl.BlockSpec((tc, pd), lambda i: (i, 0))] + [full(a) for a in consts],
        out_specs=pl.BlockSpec((tc, D), lambda i: (i, 0)),
        out_shape=jax.ShapeDtypeStruct((T, D), F32),
        scratch_shapes=[pltpu.VMEM((2, RUN_COPIES * RUN_ROWS * TOKEN_ROWS, LANES), F32),
                        pltpu.VMEM((TOP_K, tc * TOKEN_ROWS, LANES), F32), pltpu.SemaphoreType.DMA((2,))],
        compiler_params=pltpu.CompilerParams(dimension_semantics=("arbitrary",), vmem_limit_bytes=VMEM_LIMIT),
        name="final",
    )(src3, src3, pos3, x1, info, y, p2, *consts)


def _pad_heads(a, width):
    lead = a.shape[:-1]
    a = a.reshape(lead + (MLA_HEADS, width))
    a = jnp.pad(a, [(0, 0)] * len(lead) + [(0, 0), (0, LANES - width)])
    return a.reshape(lead + (HP,))


def _layer_weights(w_in, q_norm_g, w_q_up, kv_norm_g, w_kv_up, gm_ln_g, gm_ln_b, gm_w_s, gm_b_s,
                   mla_out_g, gm_out_g, w_o, ln1_g, ln1_b):
    D = w_in.shape[0]
    half = QK_ROPE // 2
    c1, c2, c3 = Q_RANK, Q_RANK + KV_RANK, Q_RANK + KV_RANK + QK_ROPE
    zeros = lambda *s: jnp.zeros(s, F32)
    kr = jnp.concatenate([zeros(D, QK_NOPE), w_in[:, c2:c3], zeros(D, LANES - QK_NOPE - QK_ROPE)], axis=1)
    win = jnp.concatenate([w_in[:, :c2], kr, w_in[:, c3:]], axis=1).astype(BF16)
    wq = _pad_heads(w_q_up, QK_NOPE + QK_ROPE).astype(BF16)

    wkv3 = w_kv_up.reshape(KV_RANK, MLA_HEADS, QK_NOPE + V_HEAD)
    wk = _pad_heads(wkv3[..., :QK_NOPE].reshape(KV_RANK, -1), QK_NOPE).astype(BF16)
    wv = wkv3[..., QK_NOPE:].reshape(KV_RANK, -1).astype(BF16)

    inv = (ROPE_THETA ** (-jnp.arange(0, QK_ROPE, 2, dtype=F32) / QK_ROPE))[:, None]
    eye = jnp.eye(half, dtype=F32)
    first = jnp.pad(eye, ((0, 0), (QK_NOPE, LANES - QK_NOPE - half)))
    second = jnp.pad(eye, ((0, 0), (QK_NOPE + half, LANES - QK_NOPE - QK_ROPE)))
    zero = jnp.zeros_like(first)
    cos_rows = jnp.concatenate([first + second, zero, zero], axis=1)
    sin_rows = jnp.concatenate([zero, -first, second], axis=1)
    rope = jnp.concatenate([cos_rows, cos_rows, sin_rows, sin_rows], axis=0).astype(BF16)
    lane = jnp.arange(LANES)
    one = jnp.where((lane >= QK_NOPE) & (lane < QK_NOPE + QK_ROPE), 0.0, 1.0)[None, :]

    grp = jnp.arange(GM_OUT) // GM_CH
    gavg = jnp.where(grp[:, None] == grp[None, :], 1.0 / GM_CH, 0.0).astype(BF16)
    bias = jnp.repeat(gm_b_s.T, GM_CH, axis=1)

    woa = w_o[:MLA_OUT].astype(BF16)
    wog = w_o[MLA_OUT:].astype(BF16)
    return dict(win=win, qg=q_norm_g[None, :], wq=wq, kvg=kv_norm_g[None, :], wk=wk, wv=wv, inv=inv, rope=rope, one=one,
                lng=gm_ln_g[None, :], lnb=gm_ln_b[None, :], gavg=gavg, ws=gm_w_s, bias=bias, gog=gm_out_g[None, :],
                woa=woa, wog=wog, mog=mla_out_g[:, None], l1g=ln1_g[None, :], l1b=ln1_b[None, :])


def _moe(x1, w_rg, b_rg, w_re, b_re, w_gate, w_up, w_down):
    T, D = x1.shape
    pad = jnp.zeros((D, LANES - N_GROUPS - N_EXPERTS), F32)
    wr = jnp.concatenate([w_rg, w_re, pad], axis=1)
    br = jnp.concatenate([b_rg, b_re, pad[0]])[None, :]
    info, info_t, stats, cnt = _route(x1, wr, br)

    bm = EXPERT_ROWS
    n_blocks = (T * TOP_K) // bm + N_EXPERTS + 2
    counts = cnt[0, R_OFF:R_OFF + N_EXPERTS].astype(jnp.int32)
    padded = (counts + bm - 1) // bm * bm
    pad_ends = jnp.cumsum(padded)
    pad_starts = pad_ends - padded
    def dest_rows(e_lane, r_lane):
        e = info_t[:, e_lane, :].astype(jnp.int32)
        ids = jnp.arange(N_EXPERTS)[:, None, None]
        seg_start = jnp.sum(jnp.where(e[None] == ids, pad_starts[:, None, None], 0), axis=0)
        return ((seg_start + info_t[:, r_lane, :].astype(jnp.int32)) * TOKEN_ROWS).reshape(T // MOVE_ROWS, MOVE_ROWS)

    dest = jnp.concatenate([dest_rows(I_E0, I_R0), dest_rows(I_E1, I_R1)], axis=1)[:, None, :]

    before = stats[:, 0, R_OFF:R_OFF + N_EXPERTS].astype(jnp.int32)
    inside = stats[:, 1, R_OFF:R_OFF + N_EXPERTS].astype(jnp.int32)
    copies = (inside + RUN_ROWS - 1) // RUN_ROWS
    copy_end = jnp.cumsum(copies, axis=1)
    copy_base = copy_end - copies
    j = jnp.arange(RUN_COPIES, dtype=jnp.int32)[None, :, None]
    owner = jnp.minimum(jnp.sum(copy_end[:, None, :] <= j, axis=2), N_EXPERTS - 1)
    is_owner = owner[:, :, None] == jnp.arange(N_EXPERTS)
    pick = lambda tab: jnp.sum(jnp.where(is_owner, tab[:, None, :], 0), axis=2)
    run_start = pick(pad_starts[None, :] + before) + RUN_ROWS * (j[:, :, 0] - pick(copy_base))
    used = j[:, :, 0] < copy_end[:, -1:]
    src = (jnp.where(used, run_start, 0) * TOKEN_ROWS)[:, None, :]

    slot_off = RUN_ROWS * copy_base - before
    tile_of = lambda a: a.reshape(T // MOVE_ROWS, MOVE_ROWS)

    def stage_rows(e_lane, r_lane):
        e = tile_of(info_t[:, e_lane, :].astype(jnp.int32))
        off = jnp.sum(jnp.where(e[:, :, None] == jnp.arange(N_EXPERTS), slot_off[:, None, :], 0), axis=2)
        return (off + tile_of(info_t[:, r_lane, :].astype(jnp.int32))) * TOKEN_ROWS

    pos = jnp.concatenate([stage_rows(I_E0, I_R0), stage_rows(I_E1, I_R1)], axis=1)[:, None, :]
    block_start = jnp.arange(n_blocks, dtype=jnp.int32) * bm
    block_expert = jnp.minimum(jnp.sum(pad_ends[None, :] <= block_start[:, None], axis=1),
                               N_EXPERTS - 1).astype(jnp.int32)

    blk = jnp.arange(n_blocks)
    later = (blk[None, :] > blk[:, None]) & (block_expert[None, :] != block_expert[:, None])
    next_expert = jnp.min(jnp.where(later, block_expert[None, :], N_EXPERTS), axis=1)
    next_expert = jnp.where(next_expert == N_EXPERTS, -1, next_expert).astype(jnp.int32)
    n_used = (pad_ends[-1:] // bm).astype(jnp.int32)

    seg = jnp.stack([pad_ends, padded, jnp.broadcast_to(n_used, (N_EXPERTS,))]).astype(jnp.int32)
    buf = _dispatch(seg, dest, x1, n_blocks * bm)
    y = _experts(block_expert, next_expert, n_used, buf, w_gate, w_up, w_down)
    return info, src, pos, y


def kernel(x, p, positions, w_in, q_norm_g, w_q_up, kv_norm_g, w_kv_up, gm_ln_g, gm_ln_b, gm_w_s, gm_b_s, mla_out_g, gm_out_g, w_o, ln1_g, ln1_b, w_rg, b_rg, w_re, b_re, w_gate, w_up, w_down, ln2_g, ln2_b, w_pg, b_pg, w_pp, ln3_g, ln3_b):
    B, S, D = x.shape
    T = B * S
    assert S % PREP_ROWS == 0 and PREP_ROWS % ATTN_ROWS == 0 and PREP_ROWS % CHUNK == 0 and S % (ATTN_TILES * ATTN_ROWS) == 0 and ATTN_TILES % 2 == 0
    assert T % ROUTE_ROWS == 0 and T % MOVE_ROWS == 0 and (T * TOP_K) % EXPERT_ROWS == 0
    assert D == TOKEN_ROWS * LANES and ROUTE_SUB % MOVE_ROWS == 0
    pos4 = positions.reshape(B, S // PREP_ROWS, 1, PREP_ROWS)
    for i in range(DEPTH):
        w = _layer_weights(w_in[i], q_norm_g[i], w_q_up[i], kv_norm_g[i], w_kv_up[i], gm_ln_g[i], gm_ln_b[i],
                           gm_w_s[i], gm_b_s[i], mla_out_g[i], gm_out_g[i], w_o[i], ln1_g[i], ln1_b[i])
        q, k, vt, g = _prep(x, pos4, w)
        x1 = _attn(q, k, vt, g, x, w).reshape(T, D)
        info, src, pos, y = _moe(x1, w_rg[i], b_rg[i], w_re[i], b_re[i], w_gate[i], w_up[i], w_down[i])
        wf = dict(wpg=w_pg[i].astype(BF16), bpg=b_pg[i][None, :], wpp=w_pp[i].astype(BF16),
                  l2g=ln2_g[i][None, :], l2b=ln2_b[i][None, :], l3g=ln3_g[i][None, :], l3b=ln3_b[i][None, :])
        x = _final(src, pos, x1, info, y, p[i].reshape(T, -1), wf).reshape(B, S, D)
    return x
```

```python
import functools

import jax
import jax.numpy as jnp
from jax import lax
from jax.experimental import pallas as pl
from jax.experimental.pallas import tpu as pltpu

F32 = jnp.float32
BF16 = jnp.bfloat16

MLA_HEADS = 8
QK_NOPE = 64
QK_ROPE = 32
V_HEAD = 64
Q_RANK = 256
KV_RANK = 128
ROPE_THETA = 10000.0
MLA_OUT = MLA_HEADS * V_HEAD
GM_GROUPS = 8
GM_CH = 64
GM_OUT = GM_GROUPS * GM_CH
CHUNK = 128
N_GROUPS = 4
EXP_PER_GROUP = 8
N_EXPERTS = N_GROUPS * EXP_PER_GROUP
TOP_K = 2
EPS = 1e-6
DEPTH = 1
ALPHA = (2.0 * DEPTH) ** 0.25
SM_SCALE = (QK_NOPE + QK_ROPE) ** -0.5
LOG2E = 1.4426950408889634
MASK_VALUE = -1e30

LANES = 128
SUBLANES = 8
TOKEN_ROWS = 8
ONES_ROWS = 16
VMEM_LIMIT = 56 * 1024 * 1024

PREP_ROWS = 512
ATTN_ROWS = 256
ATTN_TILES = 4
ROUTE_ROWS = 512
ROUTE_SUB = 512
MOVE_ROWS = 256
EXPERT_ROWS = 256
RUN_ROWS = 4
RUN_COPIES = N_EXPERTS + TOP_K * MOVE_ROWS // RUN_ROWS

C_Q = 0
C_KV = C_Q + Q_RANK
C_KR = C_KV + KV_RANK
C_U = C_KR + LANES
C_V = C_U + GM_OUT
C_END = C_V + GM_OUT
HP = MLA_HEADS * LANES

I_E0, I_E1, I_R0, I_R1, I_G0, I_G1 = range(6)
R_OFF = N_GROUPS


def _rms(v, g):
    return v * lax.rsqrt(jnp.mean(v * v, axis=-1, keepdims=True) + EPS) * g


def _ln(v, g, b):
    mu = jnp.mean(v, axis=-1, keepdims=True)
    d = v - mu
    var = jnp.mean(d * d, axis=-1, keepdims=True)
    return d * lax.rsqrt(var + EPS) * g + b


def _dot(a, b):
    return jnp.dot(a, b, preferred_element_type=F32)


def _prep_kernel(x_ref, pos_ref, win_ref, qg_ref, wq_ref, kvg_ref, wk_ref, wv_ref, inv_ref, rope_ref, one_ref,
                 lng_ref, lnb_ref, gavg_ref, ws_ref, bias_ref, gog_ref,
                 q_ref, k_ref, vt_ref, g_ref):
    rows = x_ref.shape[1]
    h = _dot(x_ref[0].astype(BF16), win_ref[...])

    ang = inv_ref[...] * pos_ref[0, 0].astype(F32)
    parts = []
    for t in (jnp.cos(ang), jnp.sin(ang)):
        hi = t.astype(BF16).astype(F32)
        parts += [hi, t - hi]
    tabs = _dot(jnp.concatenate(parts, axis=0).T.astype(BF16), rope_ref[...])
    cos_t = tabs[:, :LANES] + one_ref[...]
    sin_a = tabs[:, LANES:2 * LANES]
    sin_b = tabs[:, 2 * LANES:]
    half = QK_ROPE // 2

    def rotate(v):
        return v * cos_t + pltpu.roll(v, LANES - half, 1) * sin_a + pltpu.roll(v, half, 1) * sin_b

    cq = _rms(h[:, C_Q:C_Q + Q_RANK], qg_ref[...]).astype(BF16)
    q2 = _dot(cq, wq_ref[...])
    for hd in range(MLA_HEADS):
        lo = hd * LANES
        q_ref[0, :, lo:lo + LANES] = (rotate(q2[:, lo:lo + LANES]) * (SM_SCALE * LOG2E)).astype(BF16)

    ckv = _rms(h[:, C_KV:C_KV + KV_RANK], kvg_ref[...]).astype(BF16)
    kp = _dot(ckv, wk_ref[...])
    kr = rotate(h[:, C_KR:C_KR + LANES])
    for hd in range(MLA_HEADS):
        lo = hd * LANES
        k_ref[0, :, lo:lo + LANES] = (kp[:, lo:lo + LANES] + kr).astype(BF16)
    vp = _dot(ckv, wv_ref[...])
    for kb in range(rows // ATTN_ROWS):
        vt_ref[0, kb] = vp[kb * ATTN_ROWS:(kb + 1) * ATTN_ROWS].T.astype(BF16)

    u = jax.nn.gelu(h[:, C_U:C_U + GM_OUT])
    vv = jax.nn.gelu(h[:, C_V:C_V + GM_OUT])
    mu = _dot(vv.astype(BF16), gavg_ref[...])
    d = vv - mu
    var = _dot((d * d).astype(BF16), gavg_ref[...])
    vn = (d * lax.rsqrt(var + EPS) * lng_ref[...] + lnb_ref[...]).astype(BF16)

    tri = lax.broadcasted_iota(jnp.int32, (CHUNK, CHUNK), 0) >= lax.broadcasted_iota(jnp.int32, (CHUNK, CHUNK), 1)
    wm = [jnp.where(tri, ws_ref[g], 0.0).astype(BF16) for g in range(GM_GROUPS)]
    low_half = lax.broadcasted_iota(jnp.int32, (CHUNK, LANES), 1) < GM_CH
    for c in range(rows // CHUNK):
        r0 = c * CHUNK
        parts = []
        for pr in range(GM_GROUPS // 2):
            tile = vn[r0:r0 + CHUNK, pr * LANES:(pr + 1) * LANES]
            parts.append(jnp.where(low_half, _dot(wm[2 * pr], tile), _dot(wm[2 * pr + 1], tile)))
        sg = jnp.concatenate(parts, axis=1) + bias_ref[...]
        gm = u[r0:r0 + CHUNK] * sg
        g_ref[0, r0:r0 + CHUNK, :] = _rms(gm, gog_ref[...]).astype(BF16)


def _prep(x, pos4, w):
    B, S, D = x.shape
    ts = PREP_ROWS
    full = lambda a: pl.BlockSpec(a.shape, lambda b, i: (0,) * a.ndim)
    consts = [w["win"], w["qg"], w["wq"], w["kvg"], w["wk"], w["wv"], w["inv"], w["rope"], w["one"],
              w["lng"], w["lnb"], w["gavg"], w["ws"], w["bias"], w["gog"]]
    return pl.pallas_call(
        _prep_kernel,
        grid=(B, S // ts),
        in_specs=[pl.BlockSpec((1, ts, D), lambda b, i: (b, i, 0)),
                  pl.BlockSpec((1, 1, 1, ts), lambda b, i: (b, i, 0, 0))] + [full(a) for a in consts],
        out_specs=[pl.BlockSpec((1, ts, HP), lambda b, i: (b, i, 0)),
                   pl.BlockSpec((1, ts, HP), lambda b, i: (b, i, 0)),
                   pl.BlockSpec((1, ts // ATTN_ROWS, MLA_OUT, ATTN_ROWS), lambda b, i: (b, i, 0, 0)),
                   pl.BlockSpec((1, ts, GM_OUT), lambda b, i: (b, i, 0))],
        out_shape=[jax.ShapeDtypeStruct((B, S, HP), BF16)] * 2
        + [jax.ShapeDtypeStruct((B, S // ATTN_ROWS, MLA_OUT, ATTN_ROWS), BF16),
           jax.ShapeDtypeStruct((B, S, GM_OUT), BF16)],
        compiler_params=pltpu.CompilerParams(dimension_semantics=("parallel", "parallel"),
                                             vmem_limit_bytes=VMEM_LIMIT),
        name="prep",
    )(x, pos4, *consts)


def _attn_kernel(q_ref, k_ref, vt_ref, g_ref, x_ref, woa_ref, wog_ref, mog_ref, l1g_ref, l1b_ref,
                 o_ref, m_scr, acc_scr, sa_scr, sb_scr):
    pid = pl.program_id(1)
    tq = ATTN_ROWS
    tk = tq
    key = lax.broadcasted_iota(jnp.int32, (tk, tq), 0)
    qry = lax.broadcasted_iota(jnp.int32, (tk, tq), 1)
    diag_mask = key <= qry
    ones = jnp.ones((ONES_ROWS, tk), BF16)

    def tile(t):
        r0 = t * tq
        i = ATTN_TILES * pid + t
        odd = t % 2 == 1
        m_scr[...] = jnp.full(m_scr.shape, MASK_VALUE, F32)
        acc_scr[...] = jnp.zeros(acc_scr.shape, F32)

        def scores(j, s_scr):
            k0 = pl.multiple_of(j * tk, tk)
            for hd in range(MLA_HEADS):
                lo = hd * LANES
                qh = q_ref[0, r0:r0 + tq, lo:lo + LANES]
                kj = k_ref[0, pl.ds(k0, tk), lo:lo + LANES]
                s_scr[hd] = lax.dot_general(kj, qh, (((1,), (1,)), ((), ())), preferred_element_type=F32)

        def update(j, s_scr, masked):
            for hd in range(MLA_HEADS):
                s = s_scr[hd]
                vt = vt_ref[0, j, hd * V_HEAD:(hd + 1) * V_HEAD, :]
                if masked:
                    s = jnp.where(diag_mask, s, MASK_VALUE)
                m_prev = m_scr[hd]
                m_new = jnp.maximum(m_prev, jnp.max(s, axis=0, keepdims=True))
                p = jnp.exp2(s - m_new).astype(BF16)
                scale = jnp.exp2(m_prev - m_new)
                acc_scr[hd] = scale * acc_scr[hd] + _dot(jnp.concatenate([vt, ones], axis=0), p)
                m_scr[hd] = m_new

        def pair(jj, c):
            j = 2 * jj
            scores(j + 1, sb_scr)
            update(j, sa_scr, False)
            scores(j + 2, sa_scr)
            update(j + 1, sb_scr, False)
            return c

        scores(0, sa_scr)
        lax.fori_loop(0, (ATTN_TILES // 2) * pid + t // 2, pair, 0)
        if odd:
            scores(i, sb_scr)
            update(i - 1, sa_scr, False)
            update(i, sb_scr, True)
        else:
            update(i, sa_scr, True)

        at = jnp.concatenate([acc_scr[hd, :V_HEAD] / acc_scr[hd, V_HEAD:V_HEAD + 1] for hd in range(MLA_HEADS)],
                             axis=0)
        at = at * lax.rsqrt(jnp.mean(at * at, axis=0, keepdims=True) + EPS) * mog_ref[...]
        mix = _dot(at.T.astype(BF16), woa_ref[...]) + _dot(g_ref[0, r0:r0 + tq, :], wog_ref[...])
        o_ref[0, r0:r0 + tq, :] = _ln(ALPHA * x_ref[0, r0:r0 + tq, :] + mix, l1g_ref[...], l1b_ref[...])

    for t in range(ATTN_TILES):
        tile(t)


def _attn(q, k, vt, g, x, w):
    B, S, D = x.shape
    tq = ATTN_ROWS
    rows = ATTN_TILES * tq
    full = lambda a: pl.BlockSpec(a.shape, lambda b, i: (0,) * a.ndim)
    consts = [w["woa"], w["wog"], w["mog"], w["l1g"], w["l1b"]]
    return pl.pallas_call(
        _attn_kernel,
        grid=(B, S // rows),
        in_specs=[pl.BlockSpec((1, rows, HP), lambda b, i: (b, i, 0)),
                  pl.BlockSpec((1, S, HP), lambda b, i: (b, 0, 0)),
                  pl.BlockSpec((1,) + vt.shape[1:], lambda b, i: (b, 0, 0, 0)),
                  pl.BlockSpec((1, rows, GM_OUT), lambda b, i: (b, i, 0)),
                  pl.BlockSpec((1, rows, D), lambda b, i: (b, i, 0))] + [full(a) for a in consts],
        out_specs=pl.BlockSpec((1, rows, D), lambda b, i: (b, i, 0)),
        out_shape=jax.ShapeDtypeStruct((B, S, D), F32),
        scratch_shapes=[pltpu.VMEM((MLA_HEADS, 1, tq), F32),
                        pltpu.VMEM((MLA_HEADS, V_HEAD + ONES_ROWS, tq), F32),
                        pltpu.VMEM((MLA_HEADS, tq, tq), F32), pltpu.VMEM((MLA_HEADS, tq, tq), F32)],
        compiler_params=pltpu.CompilerParams(dimension_semantics=("parallel", "parallel"),
                                             vmem_limit_bytes=VMEM_LIMIT),
        name="attn",
    )(q, k, vt, g, x, *consts)


def _route_kernel(x_ref, wr_ref, br_ref, info_ref, infot_ref, stat_ref, cnt_ref, carry_scr, tri_scr):
    step = pl.program_id(0)
    sub = tri_scr.shape[0]

    @pl.when(step == 0)
    def _():
        carry_scr[...] = jnp.zeros_like(carry_scr)
        r = lax.broadcasted_iota(jnp.int32, (sub, sub), 0)
        c = lax.broadcasted_iota(jnp.int32, (sub, sub), 1)
        tri_scr[...] = jnp.where(c < r, 1.0, 0.0).astype(BF16)

    wr = wr_ref[...]
    wh = wr.astype(BF16)
    wl = (wr - wh.astype(F32)).astype(BF16)
    lane = lax.broadcasted_iota(jnp.int32, (sub, LANES), 1)
    neg = jnp.float32(-jnp.inf)
    carry = carry_scr[...]

    for h in range(x_ref.shape[0] // sub):
        r0_, r1_ = h * sub, (h + 1) * sub
        x = x_ref[r0_:r1_, :]
        xh = x.astype(BF16)
        xl = (x - xh.astype(F32)).astype(BF16)
        logits = _dot(xh, wh) + _dot(xl, wh) + _dot(xh, wl) + br_ref[...]

        is_g = lane < N_GROUPS
        lg = jnp.where(is_g, logits, neg)
        gmax = jnp.max(lg, axis=-1, keepdims=True)
        g_idx = jnp.min(jnp.where(lg == gmax, lane, LANES), axis=-1, keepdims=True)
        g_den = jnp.sum(jnp.where(is_g, jnp.exp(lg - gmax), 0.0), axis=-1, keepdims=True)
        g_p = 1.0 / g_den

        in_grp = (lane >= R_OFF) & (lane < R_OFF + N_EXPERTS) & (((lane - R_OFF) >> 3) == g_idx)
        le = jnp.where(in_grp, logits, neg)
        m1 = jnp.max(le, axis=-1, keepdims=True)
        i1 = jnp.min(jnp.where(le == m1, lane, LANES), axis=-1, keepdims=True)
        le2 = jnp.where(lane == i1, neg, le)
        m2 = jnp.max(le2, axis=-1, keepdims=True)
        i2 = jnp.min(jnp.where(le2 == m2, lane, LANES), axis=-1, keepdims=True)
        e2 = jnp.exp(m2 - m1)
        gate0 = g_p / (1.0 + e2)
        gate1 = g_p * e2 / (1.0 + e2)

        hit1 = lane == i1
        hit2 = lane == i2
        onehot = jnp.where(hit1 | hit2, 1.0, 0.0)
        before = _dot(tri_scr[...], onehot.astype(BF16)) + carry
        rank0 = jnp.sum(jnp.where(hit1, before, 0.0), axis=-1, keepdims=True)
        rank1 = jnp.sum(jnp.where(hit2, before, 0.0), axis=-1, keepdims=True)
        for mt in range(sub // MOVE_ROWS):
            inside = jnp.sum(onehot[mt * MOVE_ROWS:(mt + 1) * MOVE_ROWS], axis=0, keepdims=True)
            stat_ref[r0_ // MOVE_ROWS + mt] = jnp.concatenate(
                [carry, inside, jnp.zeros((SUBLANES - 2, LANES), F32)], axis=0)
            carry = carry + inside

        info = jnp.where(lane == I_E0, (i1 - R_OFF).astype(F32), 0.0)
        info = jnp.where(lane == I_E1, (i2 - R_OFF).astype(F32), info)
        info = jnp.where(lane == I_R0, rank0, info)
        info = jnp.where(lane == I_R1, rank1, info)
        info = jnp.where(lane == I_G0, gate0, info)
        info = jnp.where(lane == I_G1, gate1, info)
        info_ref[r0_:r1_, :] = info
        infot_ref[0, :, r0_:r1_] = info.T[:SUBLANES]

    carry_scr[...] = carry
    cnt_ref[...] = carry


def _route(x1, wr, br):
    T, D = x1.shape
    tt = ROUTE_ROWS
    return pl.pallas_call(
        _route_kernel,
        grid=(T // tt,),
        in_specs=[pl.BlockSpec((tt, D), lambda i: (i, 0)),
                  pl.BlockSpec(wr.shape, lambda i: (0, 0)),
                  pl.BlockSpec(br.shape, lambda i: (0, 0))],
        out_specs=[pl.BlockSpec((tt, LANES), lambda i: (i, 0)),
                   pl.BlockSpec((1, SUBLANES, tt), lambda i: (i, 0, 0)),
                   pl.BlockSpec((tt // MOVE_ROWS, SUBLANES, LANES), lambda i: (i, 0, 0)),
                   pl.BlockSpec((1, LANES), lambda i: (0, 0))],
        out_shape=[jax.ShapeDtypeStruct((T, LANES), F32), jax.ShapeDtypeStruct((T // tt, SUBLANES, tt), F32),
                   jax.ShapeDtypeStruct((T // MOVE_ROWS, SUBLANES, LANES), F32),
                   jax.ShapeDtypeStruct((1, LANES), F32)],
        scratch_shapes=[pltpu.VMEM((1, LANES), F32), pltpu.VMEM((ROUTE_SUB, ROUTE_SUB), BF16)],
        compiler_params=pltpu.CompilerParams(dimension_semantics=("arbitrary",), vmem_limit_bytes=VMEM_LIMIT),
        name="route",
    )(x1, wr, br)


def _to_token_tiles(dst_ref, val):
    dst_ref[...] = val.astype(BF16).reshape(dst_ref.shape)


def _from_token_tiles(src_ref, rows):
    return src_ref[...].reshape(rows, TOKEN_ROWS * LANES)


def _to_token_tiles_f32(dst_ref, val):
    rows = val.shape[0]
    for c in range(TOKEN_ROWS):
        dst_ref[pl.ds(c, rows, stride=TOKEN_ROWS), :] = val[:, c * LANES:(c + 1) * LANES]


def _from_token_tiles_f32(src_ref, rows):
    return jnp.concatenate([src_ref[pl.ds(c, rows, stride=TOKEN_ROWS), :] for c in range(TOKEN_ROWS)], axis=1)


def _tile_copy(src_ref, src_row, dst_ref, dst_row, sem):
    return pltpu.make_async_copy(src_ref.at[pl.ds(pl.multiple_of(src_row, TOKEN_ROWS), TOKEN_ROWS)],
                                 dst_ref.at[pl.ds(pl.multiple_of(dst_row, TOKEN_ROWS), TOKEN_ROWS)], sem)


def _dispatch_kernel(seg_ref, dest_ref, x0_ref, xn_ref, buf_ref, stage_scr, zero_scr, sem, zero_sem, *, n_steps):
    i = pl.program_id(0)
    rows = xn_ref.shape[0]
    cur = i % 3
    nxt = (i + 1) % 3

    @pl.when(i == 0)
    def _():
        zero_scr[...] = jnp.zeros(zero_scr.shape, BF16)

        block = EXPERT_ROWS * TOKEN_ROWS
        n_blocks = buf_ref.shape[0] // block

        def clear_rows(first):
            return pltpu.make_async_copy(zero_scr, buf_ref.at[pl.ds(pl.multiple_of(first, SUBLANES), block)], zero_sem)

        def clear(e):
            return clear_rows((seg_ref[0, e] - EXPERT_ROWS) * TOKEN_ROWS)

        def start_tail(b, c):
            clear_rows(b * block).start()
            return c

        def wait_tail(b, c):
            clear_rows(b * block).wait()
            return c

        for e in range(N_EXPERTS):
            pl.when(seg_ref[1, e] > 0)(lambda e=e: clear(e).start())
        lax.fori_loop(seg_ref[2, 0], n_blocks, start_tail, 0)
        for e in range(N_EXPERTS):
            pl.when(seg_ref[1, e] > 0)(lambda e=e: clear(e).wait())
        lax.fori_loop(seg_ref[2, 0], n_blocks, wait_tail, 0)

        _to_token_tiles(stage_scr.at[0], x0_ref[...])

    def drain(s):
        for _ in range(TOP_K):
            pltpu.make_async_copy(stage_scr.at[s], stage_scr.at[s], sem.at[s]).wait()

    @pl.when(i >= 2)
    def _():
        drain(nxt)

    _to_token_tiles(stage_scr.at[nxt], xn_ref[...])
    for r in range(rows):
        for kk in range(TOP_K):
            _tile_copy(stage_scr.at[cur], r * TOKEN_ROWS, buf_ref, dest_ref[0, 0, kk * rows + r],
                       sem.at[cur]).start(priority=kk)

    @pl.when(i == n_steps - 1)
    def _():
        drain(cur)
        if n_steps >= 2:
            drain((i + 2) % 3)


def _dispatch(seg, dest3, x1, n_rows):
    T, D = x1.shape
    td = MOVE_ROWS
    n_steps = T // td
    grid_spec = pltpu.PrefetchScalarGridSpec(
        num_scalar_prefetch=1,
        grid=(n_steps,),
        in_specs=[pl.BlockSpec((1, 1, TOP_K * td), lambda i, seg: (i, 0, 0), memory_space=pltpu.SMEM),
                  pl.BlockSpec((td, D), lambda i, seg: (0, 0)),
                  pl.BlockSpec((td, D), lambda i, seg: (jnp.minimum(i + 1, n_steps - 1), 0))],
        out_specs=pl.BlockSpec(memory_space=pl.ANY),
        scratch_shapes=[pltpu.VMEM((3, td * TOKEN_ROWS, LANES), BF16),
                        pltpu.VMEM((EXPERT_ROWS * TOKEN_ROWS, LANES), BF16),
                        pltpu.SemaphoreType.DMA((3,)), pltpu.SemaphoreType.DMA(())],
    )
    return pl.pallas_call(
        functools.partial(_dispatch_kernel, n_steps=n_steps),
        grid_spec=grid_spec,
        out_shape=jax.ShapeDtypeStruct((n_rows * TOKEN_ROWS, LANES), BF16),
        compiler_params=pltpu.CompilerParams(dimension_semantics=("arbitrary",), vmem_limit_bytes=VMEM_LIMIT),
        name="dispatch",
    )(seg, dest3, x1, x1)


def _expert_kernel(be_ref, ne_ref, nu_ref, buf0_ref, bufa_ref, bufb_ref, wg_hbm, wu_hbm, wd_hbm, y_ref,
                   sg_scr, su_scr, sd_scr, wg_scr, wu_scr, wd_scr, xa_scr, xb_scr, cur_ref, sem):
    step = pl.program_id(0)
    bm = EXPERT_ROWS
    half = bm * TOKEN_ROWS

    def fetch(expert, s):
        return (pltpu.make_async_copy(wg_hbm.at[expert], sg_scr.at[s], sem.at[s, 0]),
                pltpu.make_async_copy(wu_hbm.at[expert], su_scr.at[s], sem.at[s, 1]),
                pltpu.make_async_copy(wd_hbm.at[expert], sd_scr.at[s], sem.at[s, 2]))

    @pl.when(step == 0)
    def _():
        cur_ref[0] = 0
        for c in fetch(be_ref[0], 0):
            c.start()
        xa_scr[...] = _from_token_tiles(buf0_ref, bm)

    def load_weights(blk):
        e = be_ref[blk]

        @pl.when((blk == 0) | (be_ref[jnp.maximum(blk - 1, 0)] != e))
        def _():
            s = cur_ref[0]
            for c in fetch(e, s):
                c.wait()
            wg_scr[...] = sg_scr[s].astype(BF16)
            wu_scr[...] = su_scr[s].astype(BF16)
            wd_scr[...] = sd_scr[s].astype(BF16)
            nxt = ne_ref[blk]

            @pl.when(nxt >= 0)
            def _():
                for c in fetch(nxt, 1 - s):
                    c.start()

            cur_ref[0] = 1 - s

    def run(blk, x_scr, nxt_ref, nxt_scr, out_rows):
        load_weights(blk)

        @pl.when(blk < nu_ref[0])
        def _():
            nxt_scr[...] = _from_token_tiles(nxt_ref, bm)
            xb = x_scr[...]
            hidden = jax.nn.silu(_dot(xb, wg_scr[...])) * _dot(xb, wu_scr[...])
            _to_token_tiles_f32(y_ref.at[out_rows], _dot(hidden.astype(BF16), wd_scr[...]))

        @pl.when(blk >= nu_ref[0])
        def _():
            y_ref[out_rows, :] = jnp.zeros((half, LANES), F32)

    run(2 * step, xa_scr, bufa_ref, xb_scr, pl.ds(0, half))
    run(2 * step + 1, xb_scr, bufb_ref, xa_scr, pl.ds(half, half))


def _experts(block_expert, next_expert, n_used, buf, w_gate, w_up, w_down):
    bm = EXPERT_ROWS
    D, ff = w_gate.shape[1:]
    n_blocks = buf.shape[0] // (bm * TOKEN_ROWS)
    assert n_blocks % 2 == 0
    last = n_blocks - 1
    grid_spec = pltpu.PrefetchScalarGridSpec(
        num_scalar_prefetch=3,
        grid=(n_blocks // 2,),
        in_specs=[pl.BlockSpec((bm * TOKEN_ROWS, LANES), lambda s, *_: (0, 0)),
                  pl.BlockSpec((bm * TOKEN_ROWS, LANES), lambda s, *_: (2 * s + 1, 0)),
                  pl.BlockSpec((bm * TOKEN_ROWS, LANES), lambda s, *_: (jnp.minimum(2 * s + 2, last), 0)),
                  pl.BlockSpec(memory_space=pl.ANY),
                  pl.BlockSpec(memory_space=pl.ANY),
                  pl.BlockSpec(memory_space=pl.ANY)],
        out_specs=pl.BlockSpec((2 * bm * TOKEN_ROWS, LANES), lambda s, *_: (s, 0)),
        scratch_shapes=[pltpu.VMEM((2, D, ff), F32), pltpu.VMEM((2, D, ff), F32), pltpu.VMEM((2, ff, D), F32),
                        pltpu.VMEM((D, ff), BF16), pltpu.VMEM((D, ff), BF16), pltpu.VMEM((ff, D), BF16),
                        pltpu.VMEM((bm, D), BF16), pltpu.VMEM((bm, D), BF16),
                        pltpu.SMEM((1,), jnp.int32), pltpu.SemaphoreType.DMA((2, 3))],
    )
    return pl.pallas_call(
        _expert_kernel,
        grid_spec=grid_spec,
        out_shape=jax.ShapeDtypeStruct(buf.shape, F32),
        compiler_params=pltpu.CompilerParams(dimension_semantics=("arbitrary",), vmem_limit_bytes=VMEM_LIMIT),
        name="experts",
    )(block_expert, next_expert, n_used, buf, buf, buf, w_gate, w_up, w_down)


def _final_kernel(scur_ref, snxt_ref, pos_ref, x_ref, info_ref, y_ref, p_ref, wpg_ref, bpg_ref, wpp_ref,
                  l2g_ref, l2b_ref, l3g_ref, l3b_ref, o_ref, stage_scr, rows_scr, sem):
    i = pl.program_id(0)
    last = pl.num_programs(0) - 1
    rows = x_ref.shape[0]
    slot = i % 2
    run = RUN_ROWS * TOKEN_ROWS

    def run_copy(sref, s, j):
        src = pl.multiple_of(sref[0, 0, j], TOKEN_ROWS)
        return pltpu.make_async_copy(y_ref.at[pl.ds(src, run)], stage_scr.at[s, pl.ds(j * run, run)], sem.at[s])

    def landed(s):
        pltpu.make_async_copy(stage_scr.at[s], stage_scr.at[s], sem.at[s]).wait()

    @pl.when(i == 0)
    def _():
        for j in range(RUN_COPIES):
            run_copy(scur_ref, 0, j).start(priority=j % 2)

    landed(slot)
    for r in range(rows):
        for kk in range(TOP_K):
            pos = pl.multiple_of(pos_ref[0, 0, kk * rows + r], TOKEN_ROWS)
            rows_scr[kk, r * TOKEN_ROWS:(r + 1) * TOKEN_ROWS, :] = stage_scr[slot, pl.ds(pos, TOKEN_ROWS), :]

    for j in range(RUN_COPIES):
        run_copy(snxt_ref, 1 - slot, j).start(priority=j % 2)

    info = info_ref[...]
    gate0 = info[:, I_G0:I_G0 + 1]
    gate1 = info[:, I_G1:I_G1 + 1]
    moe = (_from_token_tiles_f32(rows_scr.at[0], rows) * gate0
           + _from_token_tiles_f32(rows_scr.at[1], rows) * gate1)
    pp = _dot(p_ref[...].astype(BF16), wpp_ref[...])
    x2 = _ln(ALPHA * x_ref[...] + moe, l2g_ref[...], l2b_ref[...])
    gate = jax.nn.sigmoid(_dot(x2.astype(BF16), wpg_ref[...]) + bpg_ref[...])
    o_ref[...] = _ln(ALPHA * x2 + gate * pp, l3g_ref[...], l3b_ref[...])

    @pl.when(i == last)
    def _():
        landed(1 - slot)


def _final(src3, pos3, x1, info, y, p2, w):
    T, D = x1.shape
    tc = MOVE_ROWS
    pd = p2.shape[1]
    full = lambda a: pl.BlockSpec(a.shape, lambda i: (0,) * a.ndim)
    consts = [w["wpg"], w["bpg"], w["wpp"], w["l2g"], w["l2b"], w["l3g"], w["l3b"]]
    last = T // tc - 1
    return pl.pallas_call(
        _final_kernel,
        grid=(T // tc,),
        in_specs=[pl.BlockSpec((1, 1, RUN_COPIES), lambda i: (i, 0, 0), memory_space=pltpu.SMEM),
                  pl.BlockSpec((1, 1, RUN_COPIES), lambda i: (jnp.minimum(i + 1, last), 0, 0), memory_space=pltpu.SMEM),
                  pl.BlockSpec((1, 1, TOP_K * tc), lambda i: (i, 0, 0), memory_space=pltpu.SMEM),
                  pl.BlockSpec((tc, D), lambda i: (i, 0)),
                  pl.BlockSpec((tc, LANES), lambda i: (i, 0)),
                  pl.BlockSpec(memory_space=pl.ANY),
                  pl.BlockSpec((tc, pd), lambda i: (i, 0))] + [full(a) for a in consts],
        out_specs=pl.BlockSpec((tc, D), lambda i: (i, 0)),
        out_shape=jax.ShapeDtypeStruct((T, D), F32),
        scratch_shapes=[pltpu.VMEM((2, RUN_COPIES * RUN_ROWS * TOKEN_ROWS, LANES), F32),
                        pltpu.VMEM((TOP_K, tc * TOKEN_ROWS, LANES), F32), pltpu.SemaphoreType.DMA((2,))],
        compiler_params=pltpu.CompilerParams(dimension_semantics=("arbitrary",), vmem_limit_bytes=VMEM_LIMIT),
        name="final",
    )(src3, src3, pos3, x1, info, y, p2, *consts)


def _pad_heads(a, width):
    lead = a.shape[:-1]
    a = a.reshape(lead + (MLA_HEADS, width))
    a = jnp.pad(a, [(0, 0)] * len(lead) + [(0, 0), (0, LANES - width)])
    return a.reshape(lead + (HP,))


def _layer_weights(w_in, q_norm_g, w_q_up, kv_norm_g, w_kv_up, gm_ln_g, gm_ln_b, gm_w_s, gm_b_s,
                   mla_out_g, gm_out_g, w_o, ln1_g, ln1_b):
    D = w_in.shape[0]
    half = QK_ROPE // 2
    c1, c2, c3 = Q_RANK, Q_RANK + KV_RANK, Q_RANK + KV_RANK + QK_ROPE
    zeros = lambda *s: jnp.zeros(s, F32)
    kr = jnp.concatenate([zeros(D, QK_NOPE), w_in[:, c2:c3], zeros(D, LANES - QK_NOPE - QK_ROPE)], axis=1)
    win = jnp.concatenate([w_in[:, :c2], kr, w_in[:, c3:]], axis=1).astype(BF16)
    wq = _pad_heads(w_q_up, QK_NOPE + QK_ROPE).astype(BF16)

    wkv3 = w_kv_up.reshape(KV_RANK, MLA_HEADS, QK_NOPE + V_HEAD)
    wk = _pad_heads(wkv3[..., :QK_NOPE].reshape(KV_RANK, -1), QK_NOPE).astype(BF16)
    wv = wkv3[..., QK_NOPE:].reshape(KV_RANK, -1).astype(BF16)

    inv = (ROPE_THETA ** (-jnp.arange(0, QK_ROPE, 2, dtype=F32) / QK_ROPE))[:, None]
    eye = jnp.eye(half, dtype=F32)
    first = jnp.pad(eye, ((0, 0), (QK_NOPE, LANES - QK_NOPE - half)))
    second = jnp.pad(eye, ((0, 0), (QK_NOPE + half, LANES - QK_NOPE - QK_ROPE)))
    zero = jnp.zeros_like(first)
    cos_rows = jnp.concatenate([first + second, zero, zero], axis=1)
    sin_rows = jnp.concatenate([zero, -first, second], axis=1)
    rope = jnp.concatenate([cos_rows, cos_rows, sin_rows, sin_rows], axis=0).astype(BF16)
    lane = jnp.arange(LANES)
    one = jnp.where((lane >= QK_NOPE) & (lane < QK_NOPE + QK_ROPE), 0.0, 1.0)[None, :]

    grp = jnp.arange(GM_OUT) // GM_CH
    gavg = jnp.where(grp[:, None] == grp[None, :], 1.0 / GM_CH, 0.0).astype(BF16)
    bias = jnp.repeat(gm_b_s.T, GM_CH, axis=1)

    woa = w_o[:MLA_OUT].astype(BF16)
    wog = w_o[MLA_OUT:].astype(BF16)
    return dict(win=win, qg=q_norm_g[None, :], wq=wq, kvg=kv_norm_g[None, :], wk=wk, wv=wv, inv=inv, rope=rope, one=one,
                lng=gm_ln_g[None, :], lnb=gm_ln_b[None, :], gavg=gavg, ws=gm_w_s, bias=bias, gog=gm_out_g[None, :],
                woa=woa, wog=wog, mog=mla_out_g[:, None], l1g=ln1_g[None, :], l1b=ln1_b[None, :])


def _moe(x1, w_rg, b_rg, w_re, b_re, w_gate, w_up, w_down):
    T, D = x1.shape
    pad = jnp.zeros((D, LANES - N_GROUPS - N_EXPERTS), F32)
    wr = jnp.concatenate([w_rg, w_re, pad], axis=1)
    br = jnp.concatenate([b_rg, b_re, pad[0]])[None, :]
    info, info_t, stats, cnt = _route(x1, wr, br)

    bm = EXPERT_ROWS
    n_blocks = (T * TOP_K) // bm + N_EXPERTS + 2
    counts = cnt[0, R_OFF:R_OFF + N_EXPERTS].astype(jnp.int32)
    padded = (counts + bm - 1) // bm * bm
    pad_ends = jnp.cumsum(padded)
    pad_starts = pad_ends - padded
    def dest_rows(e_lane, r_lane):
        e = info_t[:, e_lane, :].astype(jnp.int32)
        ids = jnp.arange(N_EXPERTS)[:, None, None]
        seg_start = jnp.sum(jnp.where(e[None] == ids, pad_starts[:, None, None], 0), axis=0)
        return ((seg_start + info_t[:, r_lane, :].astype(jnp.int32)) * TOKEN_ROWS).reshape(T // MOVE_ROWS, MOVE_ROWS)

    dest = jnp.concatenate([dest_rows(I_E0, I_R0), dest_rows(I_E1, I_R1)], axis=1)[:, None, :]

    before = stats[:, 0, R_OFF:R_OFF + N_EXPERTS].astype(jnp.int32)
    inside = stats[:, 1, R_OFF:R_OFF + N_EXPERTS].astype(jnp.int32)
    copies = (inside + RUN_ROWS - 1) // RUN_ROWS
    copy_end = jnp.cumsum(copies, axis=1)
    copy_base = copy_end - copies
    j = jnp.arange(RUN_COPIES, dtype=jnp.int32)[None, :, None]
    owner = jnp.minimum(jnp.sum(copy_end[:, None, :] <= j, axis=2), N_EXPERTS - 1)
    is_owner = owner[:, :, None] == jnp.arange(N_EXPERTS)
    pick = lambda tab: jnp.sum(jnp.where(is_owner, tab[:, None, :], 0), axis=2)
    run_start = pick(pad_starts[None, :] + before) + RUN_ROWS * (j[:, :, 0] - pick(copy_base))
    used = j[:, :, 0] < copy_end[:, -1:]
    src = (jnp.where(used, run_start, 0) * TOKEN_ROWS)[:, None, :]

    slot_off = RUN_ROWS * copy_base - before
    tile_of = lambda a: a.reshape(T // MOVE_ROWS, MOVE_ROWS)

    def stage_rows(e_lane, r_lane):
        e = tile_of(info_t[:, e_lane, :].astype(jnp.int32))
        off = jnp.sum(jnp.where(e[:, :, None] == jnp.arange(N_EXPERTS), slot_off[:, None, :], 0), axis=2)
        return (off + tile_of(info_t[:, r_lane, :].astype(jnp.int32))) * TOKEN_ROWS

    pos = jnp.concatenate([stage_rows(I_E0, I_R0), stage_rows(I_E1, I_R1)], axis=1)[:, None, :]
    block_start = jnp.arange(n_blocks, dtype=jnp.int32) * bm
    block_expert = jnp.minimum(jnp.sum(pad_ends[None, :] <= block_start[:, None], axis=1),
                               N_EXPERTS - 1).astype(jnp.int32)

    blk = jnp.arange(n_blocks)
    later = (blk[None, :] > blk[:, None]) & (block_expert[None, :] != block_expert[:, None])
    next_expert = jnp.min(jnp.where(later, block_expert[None, :], N_EXPERTS), axis=1)
    next_expert = jnp.where(next_expert == N_EXPERTS, -1, next_expert).astype(jnp.int32)
    n_used = (pad_ends[-1:] // bm).astype(jnp.int32)

    seg = jnp.stack([pad_ends, padded, jnp.broadcast_to(n_used, (N_EXPERTS,))]).astype(jnp.int32)
    buf = _dispatch(seg, dest, x1, n_blocks * bm)
    y = _experts(block_expert, next_expert, n_used, buf, w_gate, w_up, w_down)
    return info, src, pos, y


def kernel(x, p, positions, w_in, q_norm_g, w_q_up, kv_norm_g, w_kv_up, gm_ln_g, gm_ln_b, gm_w_s, gm_b_s, mla_out_g, gm_out_g, w_o, ln1_g, ln1_b, w_rg, b_rg, w_re, b_re, w_gate, w_up, w_down, ln2_g, ln2_b, w_pg, b_pg, w_pp, ln3_g, ln3_b):
    B, S, D = x.shape
    T = B * S
    assert S % PREP_ROWS == 0 and PREP_ROWS % ATTN_ROWS == 0 and PREP_ROWS % CHUNK == 0 and S % (ATTN_TILES * ATTN_ROWS) == 0 and ATTN_TILES % 2 == 0
    assert T % ROUTE_ROWS == 0 and T % MOVE_ROWS == 0 and (T * TOP_K) % EXPERT_ROWS == 0
    assert D == TOKEN_ROWS * LANES and ROUTE_SUB % MOVE_ROWS == 0
    pos4 = positions.reshape(B, S // PREP_ROWS, 1, PREP_ROWS)
    for i in range(DEPTH):
        w = _layer_weights(w_in[i], q_norm_g[i], w_q_up[i], kv_norm_g[i], w_kv_up[i], gm_ln_g[i], gm_ln_b[i],
                           gm_w_s[i], gm_b_s[i], mla_out_g[i], gm_out_g[i], w_o[i], ln1_g[i], ln1_b[i])
        q, k, vt, g = _prep(x, pos4, w)
        x1 = _attn(q, k, vt, g, x, w).reshape(T, D)
        info, src, pos, y = _moe(x1, w_rg[i], b_rg[i], w_re[i], b_re[i], w_gate[i], w_up[i], w_down[i])
        wf = dict(wpg=w_pg[i].astype(BF16), bpg=b_pg[i][None, :], wpp=w_pp[i].astype(BF16),
                  l2g=ln2_g[i][None, :], l2b=ln2_b[i][None, :], l3g=ln3_g[i][None, :], l3b=ln3_b[i][None, :])
        x = _final(src, pos, x1, info, y, p[i].reshape(T, -1), wf).reshape(B, S, D)
    return x
```

```python
import functools

import jax
import jax.numpy as jnp
from jax import lax
from jax.experimental import pallas as pl
from jax.experimental.pallas import tpu as pltpu

F32 = jnp.float32
BF16 = jnp.bfloat16

MLA_HEADS = 8
QK_NOPE = 64
QK_ROPE = 32
V_HEAD = 64
Q_RANK = 256
KV_RANK = 128
ROPE_THETA = 10000.0
MLA_OUT = MLA_HEADS * V_HEAD
GM_GROUPS = 8
GM_CH = 64
GM_OUT = GM_GROUPS * GM_CH
CHUNK = 128
N_GROUPS = 4
EXP_PER_GROUP = 8
N_EXPERTS = N_GROUPS * EXP_PER_GROUP
TOP_K = 2
EPS = 1e-6
DEPTH = 1
ALPHA = (2.0 * DEPTH) ** 0.25
SM_SCALE = (QK_NOPE + QK_ROPE) ** -0.5
LOG2E = 1.4426950408889634
MASK_VALUE = -1e30

LANES = 128
SUBLANES = 8
TOKEN_ROWS = 8
ONES_ROWS = 16
VMEM_LIMIT = 56 * 1024 * 1024

PREP_ROWS = 512
ATTN_ROWS = 256
ATTN_TILES = 4
ROUTE_ROWS = 512
MOVE_ROWS = 256
MOVE_UNROLL = 8
EXPERT_ROWS = 256

C_Q = 0
C_KV = C_Q + Q_RANK
C_KR = C_KV + KV_RANK
C_U = C_KR + LANES
C_V = C_U + GM_OUT
C_END = C_V + GM_OUT
HP = MLA_HEADS * LANES

I_E0, I_E1, I_R0, I_R1, I_G0, I_G1 = range(6)
R_OFF = N_GROUPS


def _rms(v, g):
    return v * lax.rsqrt(jnp.mean(v * v, axis=-1, keepdims=True) + EPS) * g


def _ln(v, g, b):
    mu = jnp.mean(v, axis=-1, keepdims=True)
    d = v - mu
    var = jnp.mean(d * d, axis=-1, keepdims=True)
    return d * lax.rsqrt(var + EPS) * g + b


def _dot(a, b):
    return jnp.dot(a, b, preferred_element_type=F32)


def _prep_kernel(x_ref, pos_ref, win_ref, qg_ref, wq_ref, kvg_ref, wk_ref, wv_ref, inv_ref, rope_ref, one_ref,
                 lng_ref, lnb_ref, gavg_ref, ws_ref, bias_ref, gog_ref,
                 q_ref, k_ref, vt_ref, g_ref):
    rows = x_ref.shape[1]
    h = _dot(x_ref[0].astype(BF16), win_ref[...])

    ang = inv_ref[...] * pos_ref[0, 0].astype(F32)
    parts = []
    for t in (jnp.cos(ang), jnp.sin(ang)):
        hi = t.astype(BF16).astype(F32)
        parts += [hi, t - hi]
    tabs = _dot(jnp.concatenate(parts, axis=0).T.astype(BF16), rope_ref[...])
    cos_t = tabs[:, :LANES] + one_ref[...]
    sin_a = tabs[:, LANES:2 * LANES]
    sin_b = tabs[:, 2 * LANES:]
    half = QK_ROPE // 2

    def rotate(v):
        return v * cos_t + pltpu.roll(v, LANES - half, 1) * sin_a + pltpu.roll(v, half, 1) * sin_b

    cq = _rms(h[:, C_Q:C_Q + Q_RANK], qg_ref[...]).astype(BF16)
    q2 = _dot(cq, wq_ref[...])
    for hd in range(MLA_HEADS):
        lo = hd * LANES
        q_ref[0, :, lo:lo + LANES] = (rotate(q2[:, lo:lo + LANES]) * (SM_SCALE * LOG2E)).astype(BF16)

    ckv = _rms(h[:, C_KV:C_KV + KV_RANK], kvg_ref[...]).astype(BF16)
    kp = _dot(ckv, wk_ref[...])
    kr = rotate(h[:, C_KR:C_KR + LANES])
    for hd in range(MLA_HEADS):
        lo = hd * LANES
        k_ref[0, :, lo:lo + LANES] = (kp[:, lo:lo + LANES] + kr).astype(BF16)
    vp = _dot(ckv, wv_ref[...])
    for kb in range(rows // ATTN_ROWS):
        vt_ref[0, kb] = vp[kb * ATTN_ROWS:(kb + 1) * ATTN_ROWS].T.astype(BF16)

    u = jax.nn.gelu(h[:, C_U:C_U + GM_OUT])
    vv = jax.nn.gelu(h[:, C_V:C_V + GM_OUT])
    mu = _dot(vv.astype(BF16), gavg_ref[...])
    d = vv - mu
    var = _dot((d * d).astype(BF16), gavg_ref[...])
    vn = (d * lax.rsqrt(var + EPS) * lng_ref[...] + lnb_ref[...]).astype(BF16)

    tri = lax.broadcasted_iota(jnp.int32, (CHUNK, CHUNK), 0) >= lax.broadcasted_iota(jnp.int32, (CHUNK, CHUNK), 1)
    wm = [jnp.where(tri, ws_ref[g], 0.0).astype(BF16) for g in range(GM_GROUPS)]
    low_half = lax.broadcasted_iota(jnp.int32, (CHUNK, LANES), 1) < GM_CH
    for c in range(rows // CHUNK):
        r0 = c * CHUNK
        parts = []
        for pr in range(GM_GROUPS // 2):
            tile = vn[r0:r0 + CHUNK, pr * LANES:(pr + 1) * LANES]
            parts.append(jnp.where(low_half, _dot(wm[2 * pr], tile), _dot(wm[2 * pr + 1], tile)))
        sg = jnp.concatenate(parts, axis=1) + bias_ref[...]
        gm = u[r0:r0 + CHUNK] * sg
        g_ref[0, r0:r0 + CHUNK, :] = _rms(gm, gog_ref[...]).astype(BF16)


def _prep(x, pos4, w):
    B, S, D = x.shape
    ts = PREP_ROWS
    full = lambda a: pl.BlockSpec(a.shape, lambda b, i: (0,) * a.ndim)
    consts = [w["win"], w["qg"], w["wq"], w["kvg"], w["wk"], w["wv"], w["inv"], w["rope"], w["one"],
              w["lng"], w["lnb"], w["gavg"], w["ws"], w["bias"], w["gog"]]
    return pl.pallas_call(
        _prep_kernel,
        grid=(B, S // ts),
        in_specs=[pl.BlockSpec((1, ts, D), lambda b, i: (b, i, 0)),
                  pl.BlockSpec((1, 1, 1, ts), lambda b, i: (b, i, 0, 0))] + [full(a) for a in consts],
        out_specs=[pl.BlockSpec((1, ts, HP), lambda b, i: (b, i, 0)),
                   pl.BlockSpec((1, ts, HP), lambda b, i: (b, i, 0)),
                   pl.BlockSpec((1, ts // ATTN_ROWS, MLA_OUT, ATTN_ROWS), lambda b, i: (b, i, 0, 0)),
                   pl.BlockSpec((1, ts, GM_OUT), lambda b, i: (b, i, 0))],
        out_shape=[jax.ShapeDtypeStruct((B, S, HP), BF16)] * 2
        + [jax.ShapeDtypeStruct((B, S // ATTN_ROWS, MLA_OUT, ATTN_ROWS), BF16),
           jax.ShapeDtypeStruct((B, S, GM_OUT), BF16)],
        compiler_params=pltpu.CompilerParams(dimension_semantics=("parallel", "parallel"),
                                             vmem_limit_bytes=VMEM_LIMIT),
        name="prep",
    )(x, pos4, *consts)


def _attn_kernel(q_ref, k_ref, vt_ref, g_ref, x_ref, woa_ref, wog_ref, mog_ref, l1g_ref, l1b_ref,
                 o_ref, m_scr, acc_scr, sa_scr, sb_scr):
    pid = pl.program_id(1)
    tq = ATTN_ROWS
    tk = tq
    key = lax.broadcasted_iota(jnp.int32, (tk, tq), 0)
    qry = lax.broadcasted_iota(jnp.int32, (tk, tq), 1)
    diag_mask = key <= qry
    ones = jnp.ones((ONES_ROWS, tk), BF16)

    def tile(t):
        r0 = t * tq
        i = ATTN_TILES * pid + t
        odd = t % 2 == 1
        m_scr[...] = jnp.full(m_scr.shape, MASK_VALUE, F32)
        acc_scr[...] = jnp.zeros(acc_scr.shape, F32)

        def scores(j, s_scr):
            k0 = pl.multiple_of(j * tk, tk)
            for hd in range(MLA_HEADS):
                lo = hd * LANES
                qh = q_ref[0, r0:r0 + tq, lo:lo + LANES]
                kj = k_ref[0, pl.ds(k0, tk), lo:lo + LANES]
                s_scr[hd] = lax.dot_general(kj, qh, (((1,), (1,)), ((), ())), preferred_element_type=F32)

        def update(j, s_scr, masked):
            for hd in range(MLA_HEADS):
                s = s_scr[hd]
                vt = vt_ref[0, j, hd * V_HEAD:(hd + 1) * V_HEAD, :]
                if masked:
                    s = jnp.where(diag_mask, s, MASK_VALUE)
                m_prev = m_scr[hd]
                m_new = jnp.maximum(m_prev, jnp.max(s, axis=0, keepdims=True))
                p = jnp.exp2(s - m_new).astype(BF16)
                scale = jnp.exp2(m_prev - m_new)
                acc_scr[hd] = scale * acc_scr[hd] + _dot(jnp.concatenate([vt, ones], axis=0), p)
                m_scr[hd] = m_new

        def pair(jj, c):
            j = 2 * jj
            scores(j + 1, sb_scr)
            update(j, sa_scr, False)
            scores(j + 2, sa_scr)
            update(j + 1, sb_scr, False)
            return c

        scores(0, sa_scr)
        lax.fori_loop(0, (ATTN_TILES // 2) * pid + t // 2, pair, 0)
        if odd:
            scores(i, sb_scr)
            update(i - 1, sa_scr, False)
            update(i, sb_scr, True)
        else:
            update(i, sa_scr, True)

        at = jnp.concatenate([acc_scr[hd, :V_HEAD] / acc_scr[hd, V_HEAD:V_HEAD + 1] for hd in range(MLA_HEADS)],
                             axis=0)
        at = at * lax.rsqrt(jnp.mean(at * at, axis=0, keepdims=True) + EPS) * mog_ref[...]
        mix = _dot(at.T.astype(BF16), woa_ref[...]) + _dot(g_ref[0, r0:r0 + tq, :], wog_ref[...])
        o_ref[0, r0:r0 + tq, :] = _ln(ALPHA * x_ref[0, r0:r0 + tq, :] + mix, l1g_ref[...], l1b_ref[...])

    for t in range(ATTN_TILES):
        tile(t)


def _attn(q, k, vt, g, x, w):
    B, S, D = x.shape
    tq = ATTN_ROWS
    rows = ATTN_TILES * tq
    full = lambda a: pl.BlockSpec(a.shape, lambda b, i: (0,) * a.ndim)
    consts = [w["woa"], w["wog"], w["mog"], w["l1g"], w["l1b"]]
    return pl.pallas_call(
        _attn_kernel,
        grid=(B, S // rows),
        in_specs=[pl.BlockSpec((1, rows, HP), lambda b, i: (b, i, 0)),
                  pl.BlockSpec((1, S, HP), lambda b, i: (b, 0, 0)),
                  pl.BlockSpec((1,) + vt.shape[1:], lambda b, i: (b, 0, 0, 0)),
                  pl.BlockSpec((1, rows, GM_OUT), lambda b, i: (b, i, 0)),
                  pl.BlockSpec((1, rows, D), lambda b, i: (b, i, 0))] + [full(a) for a in consts],
        out_specs=pl.BlockSpec((1, rows, D), lambda b, i: (b, i, 0)),
        out_shape=jax.ShapeDtypeStruct((B, S, D), F32),
        scratch_shapes=[pltpu.VMEM((MLA_HEADS, 1, tq), F32),
                        pltpu.VMEM((MLA_HEADS, V_HEAD + ONES_ROWS, tq), F32),
                        pltpu.VMEM((MLA_HEADS, tq, tq), F32), pltpu.VMEM((MLA_HEADS, tq, tq), F32)],
        compiler_params=pltpu.CompilerParams(dimension_semantics=("parallel", "parallel"),
                                             vmem_limit_bytes=VMEM_LIMIT),
        name="attn",
    )(q, k, vt, g, x, *consts)


def _route_kernel(x_ref, wr_ref, br_ref, info_ref, infot_ref, cnt_ref, carry_scr, tri_scr):
    step = pl.program_id(0)
    tt = x_ref.shape[0]

    @pl.when(step == 0)
    def _():
        carry_scr[...] = jnp.zeros_like(carry_scr)
        r = lax.broadcasted_iota(jnp.int32, (tt, tt), 0)
        c = lax.broadcasted_iota(jnp.int32, (tt, tt), 1)
        tri_scr[...] = jnp.where(c < r, 1.0, 0.0).astype(BF16)

    x = x_ref[...]
    xh = x.astype(BF16)
    xl = (x - xh.astype(F32)).astype(BF16)
    wr = wr_ref[...]
    wh = wr.astype(BF16)
    wl = (wr - wh.astype(F32)).astype(BF16)
    logits = _dot(xh, wh) + _dot(xl, wh) + _dot(xh, wl) + br_ref[...]

    lane = lax.broadcasted_iota(jnp.int32, (tt, LANES), 1)
    neg = jnp.float32(-jnp.inf)

    is_g = lane < N_GROUPS
    lg = jnp.where(is_g, logits, neg)
    gmax = jnp.max(lg, axis=-1, keepdims=True)
    g_idx = jnp.min(jnp.where(lg == gmax, lane, LANES), axis=-1, keepdims=True)
    g_den = jnp.sum(jnp.where(is_g, jnp.exp(lg - gmax), 0.0), axis=-1, keepdims=True)
    g_p = 1.0 / g_den

    in_grp = (lane >= R_OFF) & (lane < R_OFF + N_EXPERTS) & (((lane - R_OFF) >> 3) == g_idx)
    le = jnp.where(in_grp, logits, neg)
    m1 = jnp.max(le, axis=-1, keepdims=True)
    i1 = jnp.min(jnp.where(le == m1, lane, LANES), axis=-1, keepdims=True)
    le2 = jnp.where(lane == i1, neg, le)
    m2 = jnp.max(le2, axis=-1, keepdims=True)
    i2 = jnp.min(jnp.where(le2 == m2, lane, LANES), axis=-1, keepdims=True)
    e2 = jnp.exp(m2 - m1)
    gate0 = g_p / (1.0 + e2)
    gate1 = g_p * e2 / (1.0 + e2)

    hit1 = lane == i1
    hit2 = lane == i2
    onehot = jnp.where(hit1 | hit2, 1.0, 0.0)
    before = _dot(tri_scr[...], onehot.astype(BF16)) + carry_scr[...]
    rank0 = jnp.sum(jnp.where(hit1, before, 0.0), axis=-1, keepdims=True)
    rank1 = jnp.sum(jnp.where(hit2, before, 0.0), axis=-1, keepdims=True)
    carry_scr[...] = carry_scr[...] + jnp.sum(onehot, axis=0, keepdims=True)
    cnt_ref[...] = carry_scr[...]

    info = jnp.where(lane == I_E0, (i1 - R_OFF).astype(F32), 0.0)
    info = jnp.where(lane == I_E1, (i2 - R_OFF).astype(F32), info)
    info = jnp.where(lane == I_R0, rank0, info)
    info = jnp.where(lane == I_R1, rank1, info)
    info = jnp.where(lane == I_G0, gate0, info)
    info = jnp.where(lane == I_G1, gate1, info)
    info_ref[...] = info
    infot_ref[0] = info.T[:SUBLANES]


def _route(x1, wr, br):
    T, D = x1.shape
    tt = ROUTE_ROWS
    return pl.pallas_call(
        _route_kernel,
        grid=(T // tt,),
        in_specs=[pl.BlockSpec((tt, D), lambda i: (i, 0)),
                  pl.BlockSpec(wr.shape, lambda i: (0, 0)),
                  pl.BlockSpec(br.shape, lambda i: (0, 0))],
        out_specs=[pl.BlockSpec((tt, LANES), lambda i: (i, 0)),
                   pl.BlockSpec((1, SUBLANES, tt), lambda i: (i, 0, 0)),
                   pl.BlockSpec((1, LANES), lambda i: (0, 0))],
        out_shape=[jax.ShapeDtypeStruct((T, LANES), F32), jax.ShapeDtypeStruct((T // tt, SUBLANES, tt), F32),
                   jax.ShapeDtypeStruct((1, LANES), F32)],
        scratch_shapes=[pltpu.VMEM((1, LANES), F32), pltpu.VMEM((tt, tt), BF16)],
        compiler_params=pltpu.CompilerParams(dimension_semantics=("arbitrary",), vmem_limit_bytes=VMEM_LIMIT),
        name="route",
    )(x1, wr, br)


def _to_token_tiles(dst_ref, val):
    dst_ref[...] = val.astype(BF16).reshape(dst_ref.shape)


def _from_token_tiles(src_ref, rows):
    return src_ref[...].reshape(rows, TOKEN_ROWS * LANES)


def _to_token_tiles_f32(dst_ref, val):
    rows = val.shape[0]
    for c in range(TOKEN_ROWS):
        dst_ref[pl.ds(c, rows, stride=TOKEN_ROWS), :] = val[:, c * LANES:(c + 1) * LANES]


def _from_token_tiles_f32(src_ref, rows):
    return jnp.concatenate([src_ref[pl.ds(c, rows, stride=TOKEN_ROWS), :] for c in range(TOKEN_ROWS)], axis=1)


def _tile_copy(src_ref, src_row, dst_ref, dst_row, sem):
    return pltpu.make_async_copy(src_ref.at[pl.ds(pl.multiple_of(src_row, TOKEN_ROWS), TOKEN_ROWS)],
                                 dst_ref.at[pl.ds(pl.multiple_of(dst_row, TOKEN_ROWS), TOKEN_ROWS)], sem)


def _dispatch_kernel(seg_ref, dest_ref, x0_ref, xn_ref, buf_ref, stage_scr, zero_scr, sem, zero_sem, *, n_steps):
    i = pl.program_id(0)
    rows = xn_ref.shape[0]
    cur = i % 3
    nxt = (i + 1) % 3

    @pl.when(i == 0)
    def _():
        zero_scr[...] = jnp.zeros(zero_scr.shape, BF16)

        block = EXPERT_ROWS * TOKEN_ROWS
        n_blocks = buf_ref.shape[0] // block

        def clear_rows(first):
            return pltpu.make_async_copy(zero_scr, buf_ref.at[pl.ds(pl.multiple_of(first, SUBLANES), block)], zero_sem)

        def clear(e):
            return clear_rows((seg_ref[0, e] - EXPERT_ROWS) * TOKEN_ROWS)

        def start_tail(b, c):
            clear_rows(b * block).start()
            return c

        def wait_tail(b, c):
            clear_rows(b * block).wait()
            return c

        for e in range(N_EXPERTS):
            pl.when(seg_ref[1, e] > 0)(lambda e=e: clear(e).start())
        lax.fori_loop(seg_ref[2, 0], n_blocks, start_tail, 0)
        for e in range(N_EXPERTS):
            pl.when(seg_ref[1, e] > 0)(lambda e=e: clear(e).wait())
        lax.fori_loop(seg_ref[2, 0], n_blocks, wait_tail, 0)

        _to_token_tiles(stage_scr.at[0], x0_ref[...])

    def drain(s):
        for _ in range(TOP_K):
            pltpu.make_async_copy(stage_scr.at[s], stage_scr.at[s], sem.at[s]).wait()

    @pl.when(i >= 2)
    def _():
        drain(nxt)

    _to_token_tiles(stage_scr.at[nxt], xn_ref[...])
    for r in range(rows):
        for kk in range(TOP_K):
            _tile_copy(stage_scr.at[cur], r * TOKEN_ROWS, buf_ref, dest_ref[0, 0, kk * rows + r],
                       sem.at[cur]).start(priority=kk)

    @pl.when(i == n_steps - 1)
    def _():
        drain(cur)
        if n_steps >= 2:
            drain((i + 2) % 3)


def _dispatch(seg, dest3, x1, n_rows):
    T, D = x1.shape
    td = MOVE_ROWS
    n_steps = T // td
    grid_spec = pltpu.PrefetchScalarGridSpec(
        num_scalar_prefetch=1,
        grid=(n_steps,),
        in_specs=[pl.BlockSpec((1, 1, TOP_K * td), lambda i, seg: (i, 0, 0), memory_space=pltpu.SMEM),
                  pl.BlockSpec((td, D), lambda i, seg: (0, 0)),
                  pl.BlockSpec((td, D), lambda i, seg: (jnp.minimum(i + 1, n_steps - 1), 0))],
        out_specs=pl.BlockSpec(memory_space=pl.ANY),
        scratch_shapes=[pltpu.VMEM((3, td * TOKEN_ROWS, LANES), BF16),
                        pltpu.VMEM((EXPERT_ROWS * TOKEN_ROWS, LANES), BF16),
                        pltpu.SemaphoreType.DMA((3,)), pltpu.SemaphoreType.DMA(())],
    )
    return pl.pallas_call(
        functools.partial(_dispatch_kernel, n_steps=n_steps),
        grid_spec=grid_spec,
        out_shape=jax.ShapeDtypeStruct((n_rows * TOKEN_ROWS, LANES), BF16),
        compiler_params=pltpu.CompilerParams(dimension_semantics=("arbitrary",), vmem_limit_bytes=VMEM_LIMIT),
        name="dispatch",
    )(seg, dest3, x1, x1)


def _expert_kernel(be_ref, ne_ref, nu_ref, buf0_ref, bufa_ref, bufb_ref, wg_hbm, wu_hbm, wd_hbm, y_ref,
                   sg_scr, su_scr, sd_scr, wg_scr, wu_scr, wd_scr, xa_scr, xb_scr, cur_ref, sem):
    step = pl.program_id(0)
    bm = EXPERT_ROWS
    half = bm * TOKEN_ROWS

    def fetch(expert, s):
        return (pltpu.make_async_copy(wg_hbm.at[expert], sg_scr.at[s], sem.at[s, 0]),
                pltpu.make_async_copy(wu_hbm.at[expert], su_scr.at[s], sem.at[s, 1]),
                pltpu.make_async_copy(wd_hbm.at[expert], sd_scr.at[s], sem.at[s, 2]))

    @pl.when(step == 0)
    def _():
        cur_ref[0] = 0
        for c in fetch(be_ref[0], 0):
            c.start()
        xa_scr[...] = _from_token_tiles(buf0_ref, bm)

    def load_weights(blk):
        e = be_ref[blk]

        @pl.when((blk == 0) | (be_ref[jnp.maximum(blk - 1, 0)] != e))
        def _():
            s = cur_ref[0]
            for c in fetch(e, s):
                c.wait()
            wg_scr[...] = sg_scr[s].astype(BF16)
            wu_scr[...] = su_scr[s].astype(BF16)
            wd_scr[...] = sd_scr[s].astype(BF16)
            nxt = ne_ref[blk]

            @pl.when(nxt >= 0)
            def _():
                for c in fetch(nxt, 1 - s):
                    c.start()

            cur_ref[0] = 1 - s

    def run(blk, x_scr, nxt_ref, nxt_scr, out_rows):
        load_weights(blk)

        @pl.when(blk < nu_ref[0])
        def _():
            nxt_scr[...] = _from_token_tiles(nxt_ref, bm)
            xb = x_scr[...]
            hidden = jax.nn.silu(_dot(xb, wg_scr[...])) * _dot(xb, wu_scr[...])
            _to_token_tiles_f32(y_ref.at[out_rows], _dot(hidden.astype(BF16), wd_scr[...]))

        @pl.when(blk >= nu_ref[0])
        def _():
            y_ref[out_rows, :] = jnp.zeros((half, LANES), F32)

    run(2 * step, xa_scr, bufa_ref, xb_scr, pl.ds(0, half))
    run(2 * step + 1, xb_scr, bufb_ref, xa_scr, pl.ds(half, half))


def _experts(block_expert, next_expert, n_used, buf, w_gate, w_up, w_down):
    bm = EXPERT_ROWS
    D, ff = w_gate.shape[1:]
    n_blocks = buf.shape[0] // (bm * TOKEN_ROWS)
    assert n_blocks % 2 == 0
    last = n_blocks - 1
    grid_spec = pltpu.PrefetchScalarGridSpec(
        num_scalar_prefetch=3,
        grid=(n_blocks // 2,),
        in_specs=[pl.BlockSpec((bm * TOKEN_ROWS, LANES), lambda s, *_: (0, 0)),
                  pl.BlockSpec((bm * TOKEN_ROWS, LANES), lambda s, *_: (2 * s + 1, 0)),
                  pl.BlockSpec((bm * TOKEN_ROWS, LANES), lambda s, *_: (jnp.minimum(2 * s + 2, last), 0)),
                  pl.BlockSpec(memory_space=pl.ANY),
                  pl.BlockSpec(memory_space=pl.ANY),
                  pl.BlockSpec(memory_space=pl.ANY)],
        out_specs=pl.BlockSpec((2 * bm * TOKEN_ROWS, LANES), lambda s, *_: (s, 0)),
        scratch_shapes=[pltpu.VMEM((2, D, ff), F32), pltpu.VMEM((2, D, ff), F32), pltpu.VMEM((2, ff, D), F32),
                        pltpu.VMEM((D, ff), BF16), pltpu.VMEM((D, ff), BF16), pltpu.VMEM((ff, D), BF16),
                        pltpu.VMEM((bm, D), BF16), pltpu.VMEM((bm, D), BF16),
                        pltpu.SMEM((1,), jnp.int32), pltpu.SemaphoreType.DMA((2, 3))],
    )
    return pl.pallas_call(
        _expert_kernel,
        grid_spec=grid_spec,
        out_shape=jax.ShapeDtypeStruct(buf.shape, F32),
        compiler_params=pltpu.CompilerParams(dimension_semantics=("arbitrary",), vmem_limit_bytes=VMEM_LIMIT),
        name="experts",
    )(block_expert, next_expert, n_used, buf, buf, buf, w_gate, w_up, w_down)


def _final_kernel(dcur_ref, dnxt_ref, x_ref, info_ref, y_ref, p_ref, wpg_ref, bpg_ref, wpp_ref,
                  l2g_ref, l2b_ref, l3g_ref, l3b_ref, o_ref, rows_scr, sem):
    i = pl.program_id(0)
    last = pl.num_programs(0) - 1
    rows = x_ref.shape[0]
    slot = i % 2

    def row_copy(dref, s, r, kk):
        return _tile_copy(y_ref, dref[0, 0, kk * rows + r], rows_scr.at[s, kk], r * TOKEN_ROWS, sem.at[s])

    def landed(s):
        pltpu.make_async_copy(rows_scr.at[s], rows_scr.at[s], sem.at[s]).wait()

    @pl.when(i == 0)
    def _():
        def start(c, carry):
            for u in range(MOVE_UNROLL):
                for kk in range(TOP_K):
                    row_copy(dcur_ref, 0, c * MOVE_UNROLL + u, kk).start(priority=kk)
            return carry

        lax.fori_loop(0, rows // MOVE_UNROLL, start, 0)

    landed(slot)
    info = info_ref[...]
    gate0 = info[:, I_G0:I_G0 + 1]
    gate1 = info[:, I_G1:I_G1 + 1]
    moe = (_from_token_tiles_f32(rows_scr.at[slot, 0], rows) * gate0
           + _from_token_tiles_f32(rows_scr.at[slot, 1], rows) * gate1)

    for r in range(rows):
        for kk in range(TOP_K):
            row_copy(dnxt_ref, 1 - slot, r, kk).start(priority=kk)

    pp = _dot(p_ref[...].astype(BF16), wpp_ref[...])
    x2 = _ln(ALPHA * x_ref[...] + moe, l2g_ref[...], l2b_ref[...])
    gate = jax.nn.sigmoid(_dot(x2.astype(BF16), wpg_ref[...]) + bpg_ref[...])
    o_ref[...] = _ln(ALPHA * x2 + gate * pp, l3g_ref[...], l3b_ref[...])

    @pl.when(i == last)
    def _():
        landed(1 - slot)


def _final(dest3, x1, info, y, p2, w):
    T, D = x1.shape
    tc = MOVE_ROWS
    pd = p2.shape[1]
    full = lambda a: pl.BlockSpec(a.shape, lambda i: (0,) * a.ndim)
    consts = [w["wpg"], w["bpg"], w["wpp"], w["l2g"], w["l2b"], w["l3g"], w["l3b"]]
    last = T // tc - 1
    return pl.pallas_call(
        _final_kernel,
        grid=(T // tc,),
        in_specs=[pl.BlockSpec((1, 1, TOP_K * tc), lambda i: (i, 0, 0), memory_space=pltpu.SMEM),
                  pl.BlockSpec((1, 1, TOP_K * tc), lambda i: (jnp.minimum(i + 1, last), 0, 0), memory_space=pltpu.SMEM),
                  pl.BlockSpec((tc, D), lambda i: (i, 0)),
                  pl.BlockSpec((tc, LANES), lambda i: (i, 0)),
                  pl.BlockSpec(memory_space=pl.ANY),
                  pl.BlockSpec((tc, pd), lambda i: (i, 0))] + [full(a) for a in consts],
        out_specs=pl.BlockSpec((tc, D), lambda i: (i, 0)),
        out_shape=jax.ShapeDtypeStruct((T, D), F32),
        scratch_shapes=[pltpu.VMEM((2, TOP_K, tc * TOKEN_ROWS, LANES), F32), pltpu.SemaphoreType.DMA((2,))],
        compiler_params=pltpu.CompilerParams(dimension_semantics=("arbitrary",), vmem_limit_bytes=VMEM_LIMIT),
        name="final",
    )(dest3, dest3, x1, info, y, p2, *consts)


def _pad_heads(a, width):
    lead = a.shape[:-1]
    a = a.reshape(lead + (MLA_HEADS, width))
    a = jnp.pad(a, [(0, 0)] * len(lead) + [(0, 0), (0, LANES - width)])
    return a.reshape(lead + (HP,))


def _layer_weights(w_in, q_norm_g, w_q_up, kv_norm_g, w_kv_up, gm_ln_g, gm_ln_b, gm_w_s, gm_b_s,
                   mla_out_g, gm_out_g, w_o, ln1_g, ln1_b):
    D = w_in.shape[0]
    half = QK_ROPE // 2
    c1, c2, c3 = Q_RANK, Q_RANK + KV_RANK, Q_RANK + KV_RANK + QK_ROPE
    zeros = lambda *s: jnp.zeros(s, F32)
    kr = jnp.concatenate([zeros(D, QK_NOPE), w_in[:, c2:c3], zeros(D, LANES - QK_NOPE - QK_ROPE)], axis=1)
    win = jnp.concatenate([w_in[:, :c2], kr, w_in[:, c3:]], axis=1).astype(BF16)
    wq = _pad_heads(w_q_up, QK_NOPE + QK_ROPE).astype(BF16)

    wkv3 = w_kv_up.reshape(KV_RANK, MLA_HEADS, QK_NOPE + V_HEAD)
    wk = _pad_heads(wkv3[..., :QK_NOPE].reshape(KV_RANK, -1), QK_NOPE).astype(BF16)
    wv = wkv3[..., QK_NOPE:].reshape(KV_RANK, -1).astype(BF16)

    inv = (ROPE_THETA ** (-jnp.arange(0, QK_ROPE, 2, dtype=F32) / QK_ROPE))[:, None]
    eye = jnp.eye(half, dtype=F32)
    first = jnp.pad(eye, ((0, 0), (QK_NOPE, LANES - QK_NOPE - half)))
    second = jnp.pad(eye, ((0, 0), (QK_NOPE + half, LANES - QK_NOPE - QK_ROPE)))
    zero = jnp.zeros_like(first)
    cos_rows = jnp.concatenate([first + second, zero, zero], axis=1)
    sin_rows = jnp.concatenate([zero, -first, second], axis=1)
    rope = jnp.concatenate([cos_rows, cos_rows, sin_rows, sin_rows], axis=0).astype(BF16)
    lane = jnp.arange(LANES)
    one = jnp.where((lane >= QK_NOPE) & (lane < QK_NOPE + QK_ROPE), 0.0, 1.0)[None, :]

    grp = jnp.arange(GM_OUT) // GM_CH
    gavg = jnp.where(grp[:, None] == grp[None, :], 1.0 / GM_CH, 0.0).astype(BF16)
    bias = jnp.repeat(gm_b_s.T, GM_CH, axis=1)

    woa = w_o[:MLA_OUT].astype(BF16)
    wog = w_o[MLA_OUT:].astype(BF16)
    return dict(win=win, qg=q_norm_g[None, :], wq=wq, kvg=kv_norm_g[None, :], wk=wk, wv=wv, inv=inv, rope=rope, one=one,
                lng=gm_ln_g[None, :], lnb=gm_ln_b[None, :], gavg=gavg, ws=gm_w_s, bias=bias, gog=gm_out_g[None, :],
                woa=woa, wog=wog, mog=mla_out_g[:, None], l1g=ln1_g[None, :], l1b=ln1_b[None, :])


def _moe(x1, w_rg, b_rg, w_re, b_re, w_gate, w_up, w_down):
    T, D = x1.shape
    pad = jnp.zeros((D, LANES - N_GROUPS - N_EXPERTS), F32)
    wr = jnp.concatenate([w_rg, w_re, pad], axis=1)
    br = jnp.concatenate([b_rg, b_re, pad[0]])[None, :]
    info, info_t, cnt = _route(x1, wr, br)

    bm = EXPERT_ROWS
    n_blocks = (T * TOP_K) // bm + N_EXPERTS
    counts = cnt[0, R_OFF:R_OFF + N_EXPERTS].astype(jnp.int32)
    padded = (counts + bm - 1) // bm * bm
    pad_ends = jnp.cumsum(padded)
    pad_starts = pad_ends - padded
    def dest_rows(e_lane, r_lane):
        e = info_t[:, e_lane, :].astype(jnp.int32)
        ids = jnp.arange(N_EXPERTS)[:, None, None]
        seg_start = jnp.sum(jnp.where(e[None] == ids, pad_starts[:, None, None], 0), axis=0)
        return ((seg_start + info_t[:, r_lane, :].astype(jnp.int32)) * TOKEN_ROWS).reshape(T // MOVE_ROWS, MOVE_ROWS)

    dest = jnp.concatenate([dest_rows(I_E0, I_R0), dest_rows(I_E1, I_R1)], axis=1)[:, None, :]
    block_start = jnp.arange(n_blocks, dtype=jnp.int32) * bm
    block_expert = jnp.minimum(jnp.sum(pad_ends[None, :] <= block_start[:, None], axis=1),
                               N_EXPERTS - 1).astype(jnp.int32)

    blk = jnp.arange(n_blocks)
    later = (blk[None, :] > blk[:, None]) & (block_expert[None, :] != block_expert[:, None])
    next_expert = jnp.min(jnp.where(later, block_expert[None, :], N_EXPERTS), axis=1)
    next_expert = jnp.where(next_expert == N_EXPERTS, -1, next_expert).astype(jnp.int32)
    n_used = (pad_ends[-1:] // bm).astype(jnp.int32)

    seg = jnp.stack([pad_ends, padded, jnp.broadcast_to(n_used, (N_EXPERTS,))]).astype(jnp.int32)
    buf = _dispatch(seg, dest, x1, n_blocks * bm)
    y = _experts(block_expert, next_expert, n_used, buf, w_gate, w_up, w_down)
    return info, dest, y


def kernel(x, p, positions, w_in, q_norm_g, w_q_up, kv_norm_g, w_kv_up, gm_ln_g, gm_ln_b, gm_w_s, gm_b_s, mla_out_g, gm_out_g, w_o, ln1_g, ln1_b, w_rg, b_rg, w_re, b_re, w_gate, w_up, w_down, ln2_g, ln2_b, w_pg, b_pg, w_pp, ln3_g, ln3_b):
    B, S, D = x.shape
    T = B * S
    assert S % PREP_ROWS == 0 and PREP_ROWS % ATTN_ROWS == 0 and PREP_ROWS % CHUNK == 0
    assert S % (ATTN_TILES * ATTN_ROWS) == 0 and ATTN_TILES % 2 == 0
    assert T % ROUTE_ROWS == 0 and T % MOVE_ROWS == 0 and (T * TOP_K) % EXPERT_ROWS == 0
    assert D == TOKEN_ROWS * LANES and MOVE_ROWS % MOVE_UNROLL == 0
    pos4 = positions.reshape(B, S // PREP_ROWS, 1, PREP_ROWS)
    for i in range(DEPTH):
        w = _layer_weights(w_in[i], q_norm_g[i], w_q_up[i], kv_norm_g[i], w_kv_up[i], gm_ln_g[i], gm_ln_b[i],
                           gm_w_s[i], gm_b_s[i], mla_out_g[i], gm_out_g[i], w_o[i], ln1_g[i], ln1_b[i])
        q, k, vt, g = _prep(x, pos4, w)
        x1 = _attn(q, k, vt, g, x, w).reshape(T, D)
        info, dest, y = _moe(x1, w_rg[i], b_rg[i], w_re[i], b_re[i], w_gate[i], w_up[i], w_down[i])
        wf = dict(wpg=w_pg[i].astype(BF16), bpg=b_pg[i][None, :], wpp=w_pp[i].astype(BF16),
                  l2g=ln2_g[i][None, :], l2b=ln2_b[i][None, :], l3g=ln3_g[i][None, :], l3b=ln3_b[i][None, :])
        x = _final(dest, x1, info, y, p[i].reshape(T, -1), wf).reshape(B, S, D)
    return x
```

```python
import functools

import jax
import jax.numpy as jnp
from jax import lax
from jax.experimental import pallas as pl
from jax.experimental.pallas import tpu as pltpu

F32 = jnp.float32
BF16 = jnp.bfloat16

MLA_HEADS = 8
QK_NOPE = 64
QK_ROPE = 32
V_HEAD = 64
Q_RANK = 256
KV_RANK = 128
ROPE_THETA = 10000.0
MLA_OUT = MLA_HEADS * V_HEAD
GM_GROUPS = 8
GM_CH = 64
GM_OUT = GM_GROUPS * GM_CH
CHUNK = 128
N_GROUPS = 4
EXP_PER_GROUP = 8
N_EXPERTS = N_GROUPS * EXP_PER_GROUP
TOP_K = 2
EPS = 1e-6
DEPTH = 1
ALPHA = (2.0 * DEPTH) ** 0.25
SM_SCALE = (QK_NOPE + QK_ROPE) ** -0.5
LOG2E = 1.4426950408889634
MASK_VALUE = -1e30

LANES = 128
SUBLANES = 8
TOKEN_ROWS = 8
ONES_ROWS = 16
VMEM_LIMIT = 56 * 1024 * 1024

PREP_ROWS = 512
ATTN_ROWS = 256
ATTN_TILES = 4
ROUTE_ROWS = 512
MOVE_ROWS = 256
MOVE_UNROLL = 8
EXPERT_ROWS = 256

C_Q = 0
C_KV = C_Q + Q_RANK
C_KR = C_KV + KV_RANK
C_U = C_KR + LANES
C_V = C_U + GM_OUT
C_END = C_V + GM_OUT
HP = MLA_HEADS * LANES

I_E0, I_E1, I_R0, I_R1, I_G0, I_G1 = range(6)
R_OFF = N_GROUPS


def _rms(v, g):
    return v * lax.rsqrt(jnp.mean(v * v, axis=-1, keepdims=True) + EPS) * g


def _ln(v, g, b):
    mu = jnp.mean(v, axis=-1, keepdims=True)
    d = v - mu
    var = jnp.mean(d * d, axis=-1, keepdims=True)
    return d * lax.rsqrt(var + EPS) * g + b


def _dot(a, b):
    return jnp.dot(a, b, preferred_element_type=F32)


def _prep_kernel(x_ref, pos_ref, win_ref, qg_ref, wq_ref, kvg_ref, wk_ref, wv_ref, inv_ref, rope_ref, one_ref,
                 lng_ref, lnb_ref, gavg_ref, ws_ref, bias_ref, gog_ref,
                 q_ref, k_ref, vt_ref, g_ref):
    rows = x_ref.shape[1]
    h = _dot(x_ref[0].astype(BF16), win_ref[...])

    ang = inv_ref[...] * pos_ref[0, 0].astype(F32)
    parts = []
    for t in (jnp.cos(ang), jnp.sin(ang)):
        hi = t.astype(BF16).astype(F32)
        parts += [hi, t - hi]
    tabs = _dot(jnp.concatenate(parts, axis=0).T.astype(BF16), rope_ref[...])
    cos_t = tabs[:, :LANES] + one_ref[...]
    sin_a = tabs[:, LANES:2 * LANES]
    sin_b = tabs[:, 2 * LANES:]
    half = QK_ROPE // 2

    def rotate(v):
        return v * cos_t + pltpu.roll(v, LANES - half, 1) * sin_a + pltpu.roll(v, half, 1) * sin_b

    cq = _rms(h[:, C_Q:C_Q + Q_RANK], qg_ref[...]).astype(BF16)
    q2 = _dot(cq, wq_ref[...])
    for hd in range(MLA_HEADS):
        lo = hd * LANES
        q_ref[0, :, lo:lo + LANES] = (rotate(q2[:, lo:lo + LANES]) * (SM_SCALE * LOG2E)).astype(BF16)

    ckv = _rms(h[:, C_KV:C_KV + KV_RANK], kvg_ref[...]).astype(BF16)
    kp = _dot(ckv, wk_ref[...])
    kr = rotate(h[:, C_KR:C_KR + LANES])
    for hd in range(MLA_HEADS):
        lo = hd * LANES
        k_ref[0, :, lo:lo + LANES] = (kp[:, lo:lo + LANES] + kr).astype(BF16)
    vp = _dot(ckv, wv_ref[...])
    for kb in range(rows // ATTN_ROWS):
        vt_ref[0, kb] = vp[kb * ATTN_ROWS:(kb + 1) * ATTN_ROWS].T.astype(BF16)

    u = jax.nn.gelu(h[:, C_U:C_U + GM_OUT])
    vv = jax.nn.gelu(h[:, C_V:C_V + GM_OUT])
    mu = _dot(vv.astype(BF16), gavg_ref[...])
    d = vv - mu
    var = _dot((d * d).astype(BF16), gavg_ref[...])
    vn = (d * lax.rsqrt(var + EPS) * lng_ref[...] + lnb_ref[...]).astype(BF16)

    tri = lax.broadcasted_iota(jnp.int32, (CHUNK, CHUNK), 0) >= lax.broadcasted_iota(jnp.int32, (CHUNK, CHUNK), 1)
    wm = [jnp.where(tri, ws_ref[g], 0.0).astype(BF16) for g in range(GM_GROUPS)]
    low_half = lax.broadcasted_iota(jnp.int32, (CHUNK, LANES), 1) < GM_CH
    for c in range(rows // CHUNK):
        r0 = c * CHUNK
        parts = []
        for pr in range(GM_GROUPS // 2):
            tile = vn[r0:r0 + CHUNK, pr * LANES:(pr + 1) * LANES]
            parts.append(jnp.where(low_half, _dot(wm[2 * pr], tile), _dot(wm[2 * pr + 1], tile)))
        sg = jnp.concatenate(parts, axis=1) + bias_ref[...]
        gm = u[r0:r0 + CHUNK] * sg
        g_ref[0, r0:r0 + CHUNK, :] = _rms(gm, gog_ref[...]).astype(BF16)


def _prep(x, pos4, w):
    B, S, D = x.shape
    ts = PREP_ROWS
    full = lambda a: pl.BlockSpec(a.shape, lambda b, i: (0,) * a.ndim)
    consts = [w["win"], w["qg"], w["wq"], w["kvg"], w["wk"], w["wv"], w["inv"], w["rope"], w["one"],
              w["lng"], w["lnb"], w["gavg"], w["ws"], w["bias"], w["gog"]]
    return pl.pallas_call(
        _prep_kernel,
        grid=(B, S // ts),
        in_specs=[pl.BlockSpec((1, ts, D), lambda b, i: (b, i, 0)),
                  pl.BlockSpec((1, 1, 1, ts), lambda b, i: (b, i, 0, 0))] + [full(a) for a in consts],
        out_specs=[pl.BlockSpec((1, ts, HP), lambda b, i: (b, i, 0)),
                   pl.BlockSpec((1, ts, HP), lambda b, i: (b, i, 0)),
                   pl.BlockSpec((1, ts // ATTN_ROWS, MLA_OUT, ATTN_ROWS), lambda b, i: (b, i, 0, 0)),
                   pl.BlockSpec((1, ts, GM_OUT), lambda b, i: (b, i, 0))],
        out_shape=[jax.ShapeDtypeStruct((B, S, HP), BF16)] * 2
        + [jax.ShapeDtypeStruct((B, S // ATTN_ROWS, MLA_OUT, ATTN_ROWS), BF16),
           jax.ShapeDtypeStruct((B, S, GM_OUT), BF16)],
        compiler_params=pltpu.CompilerParams(dimension_semantics=("parallel", "parallel"),
                                             vmem_limit_bytes=VMEM_LIMIT),
        name="prep",
    )(x, pos4, *consts)


def _attn_kernel(q_ref, k_ref, vt_ref, g_ref, x_ref, woa_ref, wog_ref, mog_ref, l1g_ref, l1b_ref,
                 o_ref, m_scr, acc_scr, sa_scr, sb_scr):
    pid = pl.program_id(1)
    tq = ATTN_ROWS
    tk = tq
    key = lax.broadcasted_iota(jnp.int32, (tk, tq), 0)
    qry = lax.broadcasted_iota(jnp.int32, (tk, tq), 1)
    diag_mask = key <= qry
    ones = jnp.ones((ONES_ROWS, tk), BF16)

    def tile(t):
        r0 = t * tq
        i = ATTN_TILES * pid + t
        odd = t % 2 == 1
        m_scr[...] = jnp.full(m_scr.shape, MASK_VALUE, F32)
        acc_scr[...] = jnp.zeros(acc_scr.shape, F32)

        def scores(j, s_scr):
            k0 = pl.multiple_of(j * tk, tk)
            for hd in range(MLA_HEADS):
                lo = hd * LANES
                qh = q_ref[0, r0:r0 + tq, lo:lo + LANES]
                kj = k_ref[0, pl.ds(k0, tk), lo:lo + LANES]
                s_scr[hd] = lax.dot_general(kj, qh, (((1,), (1,)), ((), ())), preferred_element_type=F32)

        def update(j, s_scr, masked):
            for hd in range(MLA_HEADS):
                s = s_scr[hd]
                vt = vt_ref[0, j, hd * V_HEAD:(hd + 1) * V_HEAD, :]
                if masked:
                    s = jnp.where(diag_mask, s, MASK_VALUE)
                m_prev = m_scr[hd]
                m_new = jnp.maximum(m_prev, jnp.max(s, axis=0, keepdims=True))
                p = jnp.exp2(s - m_new).astype(BF16)
                scale = jnp.exp2(m_prev - m_new)
                acc_scr[hd] = scale * acc_scr[hd] + _dot(jnp.concatenate([vt, ones], axis=0), p)
                m_scr[hd] = m_new

        def pair(jj, c):
            j = 2 * jj
            scores(j + 1, sb_scr)
            update(j, sa_scr, False)
            scores(j + 2, sa_scr)
            update(j + 1, sb_scr, False)
            return c

        scores(0, sa_scr)
        lax.fori_loop(0, (ATTN_TILES // 2) * pid + t // 2, pair, 0)
        if odd:
            scores(i, sb_scr)
            update(i - 1, sa_scr, False)
            update(i, sb_scr, True)
        else:
            update(i, sa_scr, True)

        at = jnp.concatenate([acc_scr[hd, :V_HEAD] / acc_scr[hd, V_HEAD:V_HEAD + 1] for hd in range(MLA_HEADS)],
                             axis=0)
        at = at * lax.rsqrt(jnp.mean(at * at, axis=0, keepdims=True) + EPS) * mog_ref[...]
        mix = _dot(at.T.astype(BF16), woa_ref[...]) + _dot(g_ref[0, r0:r0 + tq, :], wog_ref[...])
        o_ref[0, r0:r0 + tq, :] = _ln(ALPHA * x_ref[0, r0:r0 + tq, :] + mix, l1g_ref[...], l1b_ref[...])

    for t in range(ATTN_TILES):
        tile(t)


def _attn(q, k, vt, g, x, w):
    B, S, D = x.shape
    tq = ATTN_ROWS
    rows = ATTN_TILES * tq
    full = lambda a: pl.BlockSpec(a.shape, lambda b, i: (0,) * a.ndim)
    consts = [w["woa"], w["wog"], w["mog"], w["l1g"], w["l1b"]]
    return pl.pallas_call(
        _attn_kernel,
        grid=(B, S // rows),
        in_specs=[pl.BlockSpec((1, rows, HP), lambda b, i: (b, i, 0)),
                  pl.BlockSpec((1, S, HP), lambda b, i: (b, 0, 0)),
                  pl.BlockSpec((1,) + vt.shape[1:], lambda b, i: (b, 0, 0, 0)),
                  pl.BlockSpec((1, rows, GM_OUT), lambda b, i: (b, i, 0)),
                  pl.BlockSpec((1, rows, D), lambda b, i: (b, i, 0))] + [full(a) for a in consts],
        out_specs=pl.BlockSpec((1, rows, D), lambda b, i: (b, i, 0)),
        out_shape=jax.ShapeDtypeStruct((B, S, D), F32),
        scratch_shapes=[pltpu.VMEM((MLA_HEADS, 1, tq), F32),
                        pltpu.VMEM((MLA_HEADS, V_HEAD + ONES_ROWS, tq), F32),
                        pltpu.VMEM((MLA_HEADS, tq, tq), F32), pltpu.VMEM((MLA_HEADS, tq, tq), F32)],
        compiler_params=pltpu.CompilerParams(dimension_semantics=("parallel", "parallel"),
                                             vmem_limit_bytes=VMEM_LIMIT),
        name="attn",
    )(q, k, vt, g, x, *consts)


def _route_kernel(x_ref, wr_ref, br_ref, info_ref, infot_ref, cnt_ref, carry_scr, tri_scr):
    step = pl.program_id(0)
    tt = x_ref.shape[0]

    @pl.when(step == 0)
    def _():
        carry_scr[...] = jnp.zeros_like(carry_scr)
        r = lax.broadcasted_iota(jnp.int32, (tt, tt), 0)
        c = lax.broadcasted_iota(jnp.int32, (tt, tt), 1)
        tri_scr[...] = jnp.where(c < r, 1.0, 0.0).astype(BF16)

    x = x_ref[...]
    xh = x.astype(BF16)
    xl = (x - xh.astype(F32)).astype(BF16)
    wr = wr_ref[...]
    wh = wr.astype(BF16)
    wl = (wr - wh.astype(F32)).astype(BF16)
    logits = _dot(xh, wh) + _dot(xl, wh) + _dot(xh, wl) + br_ref[...]

    lane = lax.broadcasted_iota(jnp.int32, (tt, LANES), 1)
    neg = jnp.float32(-jnp.inf)

    is_g = lane < N_GROUPS
    lg = jnp.where(is_g, logits, neg)
    gmax = jnp.max(lg, axis=-1, keepdims=True)
    g_idx = jnp.min(jnp.where(lg == gmax, lane, LANES), axis=-1, keepdims=True)
    g_den = jnp.sum(jnp.where(is_g, jnp.exp(lg - gmax), 0.0), axis=-1, keepdims=True)
    g_p = 1.0 / g_den

    in_grp = (lane >= R_OFF) & (lane < R_OFF + N_EXPERTS) & (((lane - R_OFF) >> 3) == g_idx)
    le = jnp.where(in_grp, logits, neg)
    m1 = jnp.max(le, axis=-1, keepdims=True)
    i1 = jnp.min(jnp.where(le == m1, lane, LANES), axis=-1, keepdims=True)
    le2 = jnp.where(lane == i1, neg, le)
    m2 = jnp.max(le2, axis=-1, keepdims=True)
    i2 = jnp.min(jnp.where(le2 == m2, lane, LANES), axis=-1, keepdims=True)
    e2 = jnp.exp(m2 - m1)
    gate0 = g_p / (1.0 + e2)
    gate1 = g_p * e2 / (1.0 + e2)

    hit1 = lane == i1
    hit2 = lane == i2
    onehot = jnp.where(hit1 | hit2, 1.0, 0.0)
    before = _dot(tri_scr[...], onehot.astype(BF16)) + carry_scr[...]
    rank0 = jnp.sum(jnp.where(hit1, before, 0.0), axis=-1, keepdims=True)
    rank1 = jnp.sum(jnp.where(hit2, before, 0.0), axis=-1, keepdims=True)
    carry_scr[...] = carry_scr[...] + jnp.sum(onehot, axis=0, keepdims=True)
    cnt_ref[...] = carry_scr[...]

    info = jnp.where(lane == I_E0, (i1 - R_OFF).astype(F32), 0.0)
    info = jnp.where(lane == I_E1, (i2 - R_OFF).astype(F32), info)
    info = jnp.where(lane == I_R0, rank0, info)
    info = jnp.where(lane == I_R1, rank1, info)
    info = jnp.where(lane == I_G0, gate0, info)
    info = jnp.where(lane == I_G1, gate1, info)
    info_ref[...] = info
    infot_ref[0] = info.T[:SUBLANES]


def _route(x1, wr, br):
    T, D = x1.shape
    tt = ROUTE_ROWS
    return pl.pallas_call(
        _route_kernel,
        grid=(T // tt,),
        in_specs=[pl.BlockSpec((tt, D), lambda i: (i, 0)),
                  pl.BlockSpec(wr.shape, lambda i: (0, 0)),
                  pl.BlockSpec(br.shape, lambda i: (0, 0))],
        out_specs=[pl.BlockSpec((tt, LANES), lambda i: (i, 0)),
                   pl.BlockSpec((1, SUBLANES, tt), lambda i: (i, 0, 0)),
                   pl.BlockSpec((1, LANES), lambda i: (0, 0))],
        out_shape=[jax.ShapeDtypeStruct((T, LANES), F32), jax.ShapeDtypeStruct((T // tt, SUBLANES, tt), F32),
                   jax.ShapeDtypeStruct((1, LANES), F32)],
        scratch_shapes=[pltpu.VMEM((1, LANES), F32), pltpu.VMEM((tt, tt), BF16)],
        compiler_params=pltpu.CompilerParams(dimension_semantics=("arbitrary",), vmem_limit_bytes=VMEM_LIMIT),
        name="route",
    )(x1, wr, br)


def _to_token_tiles(dst_ref, val):
    dst_ref[...] = val.astype(BF16).reshape(dst_ref.shape)


def _from_token_tiles(src_ref, rows):
    return src_ref[...].reshape(rows, TOKEN_ROWS * LANES)


def _to_token_tiles_f32(dst_ref, val):
    rows = val.shape[0]
    for c in range(TOKEN_ROWS):
        dst_ref[pl.ds(c, rows, stride=TOKEN_ROWS), :] = val[:, c * LANES:(c + 1) * LANES]


def _from_token_tiles_f32(src_ref, rows):
    return jnp.concatenate([src_ref[pl.ds(c, rows, stride=TOKEN_ROWS), :] for c in range(TOKEN_ROWS)], axis=1)


def _tile_copy(src_ref, src_row, dst_ref, dst_row, sem):
    return pltpu.make_async_copy(src_ref.at[pl.ds(pl.multiple_of(src_row, TOKEN_ROWS), TOKEN_ROWS)],
                                 dst_ref.at[pl.ds(pl.multiple_of(dst_row, TOKEN_ROWS), TOKEN_ROWS)], sem)


def _dispatch_kernel(seg_ref, dest_ref, x0_ref, xn_ref, buf_ref, stage_scr, zero_scr, sem, zero_sem, *, n_steps):
    i = pl.program_id(0)
    rows = xn_ref.shape[0]
    cur = i % 3
    nxt = (i + 1) % 3

    @pl.when(i == 0)
    def _():
        zero_scr[...] = jnp.zeros(zero_scr.shape, BF16)

        block = EXPERT_ROWS * TOKEN_ROWS
        n_blocks = buf_ref.shape[0] // block

        def clear_rows(first):
            return pltpu.make_async_copy(zero_scr, buf_ref.at[pl.ds(pl.multiple_of(first, SUBLANES), block)], zero_sem)

        def clear(e):
            return clear_rows((seg_ref[0, e] - EXPERT_ROWS) * TOKEN_ROWS)

        def start_tail(b, c):
            clear_rows(b * block).start()
            return c

        def wait_tail(b, c):
            clear_rows(b * block).wait()
            return c

        for e in range(N_EXPERTS):
            pl.when(seg_ref[1, e] > 0)(lambda e=e: clear(e).start())
        lax.fori_loop(seg_ref[2, 0], n_blocks, start_tail, 0)
        for e in range(N_EXPERTS):
            pl.when(seg_ref[1, e] > 0)(lambda e=e: clear(e).wait())
        lax.fori_loop(seg_ref[2, 0], n_blocks, wait_tail, 0)

        _to_token_tiles(stage_scr.at[0], x0_ref[...])

    def drain(s):
        for _ in range(TOP_K):
            pltpu.make_async_copy(stage_scr.at[s], stage_scr.at[s], sem.at[s]).wait()

    @pl.when(i >= 2)
    def _():
        drain(nxt)

    _to_token_tiles(stage_scr.at[nxt], xn_ref[...])
    for r in range(rows):
        for kk in range(TOP_K):
            _tile_copy(stage_scr.at[cur], r * TOKEN_ROWS, buf_ref, dest_ref[0, 0, kk * rows + r],
                       sem.at[cur]).start(priority=kk)

    @pl.when(i == n_steps - 1)
    def _():
        drain(cur)
        if n_steps >= 2:
            drain((i + 2) % 3)


def _dispatch(seg, dest3, x1, n_rows):
    T, D = x1.shape
    td = MOVE_ROWS
    n_steps = T // td
    grid_spec = pltpu.PrefetchScalarGridSpec(
        num_scalar_prefetch=1,
        grid=(n_steps,),
        in_specs=[pl.BlockSpec((1, 1, TOP_K * td), lambda i, seg: (i, 0, 0), memory_space=pltpu.SMEM),
                  pl.BlockSpec((td, D), lambda i, seg: (0, 0)),
                  pl.BlockSpec((td, D), lambda i, seg: (jnp.minimum(i + 1, n_steps - 1), 0))],
        out_specs=pl.BlockSpec(memory_space=pl.ANY),
        scratch_shapes=[pltpu.VMEM((3, td * TOKEN_ROWS, LANES), BF16),
                        pltpu.VMEM((EXPERT_ROWS * TOKEN_ROWS, LANES), BF16),
                        pltpu.SemaphoreType.DMA((3,)), pltpu.SemaphoreType.DMA(())],
    )
    return pl.pallas_call(
        functools.partial(_dispatch_kernel, n_steps=n_steps),
        grid_spec=grid_spec,
        out_shape=jax.ShapeDtypeStruct((n_rows * TOKEN_ROWS, LANES), BF16),
        compiler_params=pltpu.CompilerParams(dimension_semantics=("arbitrary",), vmem_limit_bytes=VMEM_LIMIT),
        name="dispatch",
    )(seg, dest3, x1, x1)


def _expert_kernel(be_ref, ne_ref, nu_ref, buf0_ref, bufa_ref, bufb_ref, wg_hbm, wu_hbm, wd_hbm, y_ref,
                   sg_scr, su_scr, sd_scr, wg_scr, wu_scr, wd_scr, xa_scr, xb_scr, cur_ref, sem):
    step = pl.program_id(0)
    bm = EXPERT_ROWS
    half = bm * TOKEN_ROWS

    def fetch(expert, s):
        return (pltpu.make_async_copy(wg_hbm.at[expert], sg_scr.at[s], sem.at[s, 0]),
                pltpu.make_async_copy(wu_hbm.at[expert], su_scr.at[s], sem.at[s, 1]),
                pltpu.make_async_copy(wd_hbm.at[expert], sd_scr.at[s], sem.at[s, 2]))

    @pl.when(step == 0)
    def _():
        cur_ref[0] = 0
        for c in fetch(be_ref[0], 0):
            c.start()
        xa_scr[...] = _from_token_tiles(buf0_ref, bm)

    def load_weights(blk):
        e = be_ref[blk]

        @pl.when((blk == 0) | (be_ref[jnp.maximum(blk - 1, 0)] != e))
        def _():
            s = cur_ref[0]
            for c in fetch(e, s):
                c.wait()
            wg_scr[...] = sg_scr[s].astype(BF16)
            wu_scr[...] = su_scr[s].astype(BF16)
            wd_scr[...] = sd_scr[s].astype(BF16)
            nxt = ne_ref[blk]

            @pl.when(nxt >= 0)
            def _():
                for c in fetch(nxt, 1 - s):
                    c.start()

            cur_ref[0] = 1 - s

    def compute(x_scr, nxt_ref, nxt_scr, out_rows):
        nxt_scr[...] = _from_token_tiles(nxt_ref, bm)
        xb = x_scr[...]
        hidden = jax.nn.silu(_dot(xb, wg_scr[...])) * _dot(xb, wu_scr[...])
        _to_token_tiles_f32(y_ref.at[out_rows], _dot(hidden.astype(BF16), wd_scr[...]))

    def run(blk, x_scr, nxt_ref, nxt_scr, out_rows):
        @pl.when(blk < nu_ref[0])
        def _():
            compute(x_scr, nxt_ref, nxt_scr, out_rows)

        @pl.when(blk >= nu_ref[0])
        def _():
            y_ref[out_rows, :] = jnp.zeros((half, LANES), F32)

    blk_a, blk_b = 2 * step, 2 * step + 1
    rows_a, rows_b = pl.ds(0, half), pl.ds(half, half)
    load_weights(blk_a)
    same = (be_ref[blk_a] == be_ref[blk_b]) & (blk_b < nu_ref[0])

    @pl.when(same)
    def _():
        compute(xa_scr, bufa_ref, xb_scr, rows_a)
        compute(xb_scr, bufb_ref, xa_scr, rows_b)

    @pl.when(jnp.logical_not(same))
    def _():
        run(blk_a, xa_scr, bufa_ref, xb_scr, rows_a)
        load_weights(blk_b)
        run(blk_b, xb_scr, bufb_ref, xa_scr, rows_b)


def _experts(block_expert, next_expert, n_used, buf, w_gate, w_up, w_down):
    bm = EXPERT_ROWS
    D, ff = w_gate.shape[1:]
    n_blocks = buf.shape[0] // (bm * TOKEN_ROWS)
    assert n_blocks % 2 == 0
    last = n_blocks - 1
    grid_spec = pltpu.PrefetchScalarGridSpec(
        num_scalar_prefetch=3,
        grid=(n_blocks // 2,),
        in_specs=[pl.BlockSpec((bm * TOKEN_ROWS, LANES), lambda s, *_: (0, 0)),
                  pl.BlockSpec((bm * TOKEN_ROWS, LANES), lambda s, *_: (2 * s + 1, 0)),
                  pl.BlockSpec((bm * TOKEN_ROWS, LANES), lambda s, *_: (jnp.minimum(2 * s + 2, last), 0)),
                  pl.BlockSpec(memory_space=pl.ANY),
                  pl.BlockSpec(memory_space=pl.ANY),
                  pl.BlockSpec(memory_space=pl.ANY)],
        out_specs=pl.BlockSpec((2 * bm * TOKEN_ROWS, LANES), lambda s, *_: (s, 0)),
        scratch_shapes=[pltpu.VMEM((2, D, ff), F32), pltpu.VMEM((2, D, ff), F32), pltpu.VMEM((2, ff, D), F32),
                        pltpu.VMEM((D, ff), BF16), pltpu.VMEM((D, ff), BF16), pltpu.VMEM((ff, D), BF16),
                        pltpu.VMEM((bm, D), BF16), pltpu.VMEM((bm, D), BF16),
                        pltpu.SMEM((1,), jnp.int32), pltpu.SemaphoreType.DMA((2, 3))],
    )
    return pl.pallas_call(
        _expert_kernel,
        grid_spec=grid_spec,
        out_shape=jax.ShapeDtypeStruct(buf.shape, F32),
        compiler_params=pltpu.CompilerParams(dimension_semantics=("arbitrary",), vmem_limit_bytes=VMEM_LIMIT),
        name="experts",
    )(block_expert, next_expert, n_used, buf, buf, buf, w_gate, w_up, w_down)


def _final_kernel(dcur_ref, dnxt_ref, x_ref, info_ref, y_ref, p_ref, wpg_ref, bpg_ref, wpp_ref,
                  l2g_ref, l2b_ref, l3g_ref, l3b_ref, o_ref, rows_scr, sem):
    i = pl.program_id(0)
    last = pl.num_programs(0) - 1
    rows = x_ref.shape[0]
    slot = i % 2

    def row_copy(dref, s, r, kk):
        return _tile_copy(y_ref, dref[0, 0, kk * rows + r], rows_scr.at[s, kk], r * TOKEN_ROWS, sem.at[s])

    def landed(s):
        pltpu.make_async_copy(rows_scr.at[s], rows_scr.at[s], sem.at[s]).wait()

    @pl.when(i == 0)
    def _():
        def start(c, carry):
            for u in range(MOVE_UNROLL):
                for kk in range(TOP_K):
                    row_copy(dcur_ref, 0, c * MOVE_UNROLL + u, kk).start(priority=kk)
            return carry

        lax.fori_loop(0, rows // MOVE_UNROLL, start, 0)

    landed(slot)
    info = info_ref[...]
    gate0 = info[:, I_G0:I_G0 + 1]
    gate1 = info[:, I_G1:I_G1 + 1]
    moe = (_from_token_tiles_f32(rows_scr.at[slot, 0], rows) * gate0
           + _from_token_tiles_f32(rows_scr.at[slot, 1], rows) * gate1)

    for r in range(rows):
        for kk in range(TOP_K):
            row_copy(dnxt_ref, 1 - slot, r, kk).start(priority=kk)

    pp = _dot(p_ref[...].astype(BF16), wpp_ref[...])
    x2 = _ln(ALPHA * x_ref[...] + moe, l2g_ref[...], l2b_ref[...])
    gate = jax.nn.sigmoid(_dot(x2.astype(BF16), wpg_ref[...]) + bpg_ref[...])
    o_ref[...] = _ln(ALPHA * x2 + gate * pp, l3g_ref[...], l3b_ref[...])

    @pl.when(i == last)
    def _():
        landed(1 - slot)


def _final(dest3, x1, info, y, p2, w):
    T, D = x1.shape
    tc = MOVE_ROWS
    pd = p2.shape[1]
    full = lambda a: pl.BlockSpec(a.shape, lambda i: (0,) * a.ndim)
    consts = [w["wpg"], w["bpg"], w["wpp"], w["l2g"], w["l2b"], w["l3g"], w["l3b"]]
    last = T // tc - 1
    return pl.pallas_call(
        _final_kernel,
        grid=(T // tc,),
        in_specs=[pl.BlockSpec((1, 1, TOP_K * tc), lambda i: (i, 0, 0), memory_space=pltpu.SMEM),
                  pl.BlockSpec((1, 1, TOP_K * tc), lambda i: (jnp.minimum(i + 1, last), 0, 0), memory_space=pltpu.SMEM),
                  pl.BlockSpec((tc, D), lambda i: (i, 0)),
                  pl.BlockSpec((tc, LANES), lambda i: (i, 0)),
                  pl.BlockSpec(memory_space=pl.ANY),
                  pl.BlockSpec((tc, pd), lambda i: (i, 0))] + [full(a) for a in consts],
        out_specs=pl.BlockSpec((tc, D), lambda i: (i, 0)),
        out_shape=jax.ShapeDtypeStruct((T, D), F32),
        scratch_shapes=[pltpu.VMEM((2, TOP_K, tc * TOKEN_ROWS, LANES), F32), pltpu.SemaphoreType.DMA((2,))],
        compiler_params=pltpu.CompilerParams(dimension_semantics=("arbitrary",), vmem_limit_bytes=VMEM_LIMIT),
        name="final",
    )(dest3, dest3, x1, info, y, p2, *consts)


def _pad_heads(a, width):
    lead = a.shape[:-1]
    a = a.reshape(lead + (MLA_HEADS, width))
    a = jnp.pad(a, [(0, 0)] * len(lead) + [(0, 0), (0, LANES - width)])
    return a.reshape(lead + (HP,))


def _layer_weights(w_in, q_norm_g, w_q_up, kv_norm_g, w_kv_up, gm_ln_g, gm_ln_b, gm_w_s, gm_b_s,
                   mla_out_g, gm_out_g, w_o, ln1_g, ln1_b):
    D = w_in.shape[0]
    half = QK_ROPE // 2
    c1, c2, c3 = Q_RANK, Q_RANK + KV_RANK, Q_RANK + KV_RANK + QK_ROPE
    zeros = lambda *s: jnp.zeros(s, F32)
    kr = jnp.concatenate([zeros(D, QK_NOPE), w_in[:, c2:c3], zeros(D, LANES - QK_NOPE - QK_ROPE)], axis=1)
    win = jnp.concatenate([w_in[:, :c2], kr, w_in[:, c3:]], axis=1).astype(BF16)
    wq = _pad_heads(w_q_up, QK_NOPE + QK_ROPE).astype(BF16)

    wkv3 = w_kv_up.reshape(KV_RANK, MLA_HEADS, QK_NOPE + V_HEAD)
    wk = _pad_heads(wkv3[..., :QK_NOPE].reshape(KV_RANK, -1), QK_NOPE).astype(BF16)
    wv = wkv3[..., QK_NOPE:].reshape(KV_RANK, -1).astype(BF16)

    inv = (ROPE_THETA ** (-jnp.arange(0, QK_ROPE, 2, dtype=F32) / QK_ROPE))[:, None]
    eye = jnp.eye(half, dtype=F32)
    first = jnp.pad(eye, ((0, 0), (QK_NOPE, LANES - QK_NOPE - half)))
    second = jnp.pad(eye, ((0, 0), (QK_NOPE + half, LANES - QK_NOPE - QK_ROPE)))
    zero = jnp.zeros_like(first)
    cos_rows = jnp.concatenate([first + second, zero, zero], axis=1)
    sin_rows = jnp.concatenate([zero, -first, second], axis=1)
    rope = jnp.concatenate([cos_rows, cos_rows, sin_rows, sin_rows], axis=0).astype(BF16)
    lane = jnp.arange(LANES)
    one = jnp.where((lane >= QK_NOPE) & (lane < QK_NOPE + QK_ROPE), 0.0, 1.0)[None, :]

    grp = jnp.arange(GM_OUT) // GM_CH
    gavg = jnp.where(grp[:, None] == grp[None, :], 1.0 / GM_CH, 0.0).astype(BF16)
    bias = jnp.repeat(gm_b_s.T, GM_CH, axis=1)

    woa = w_o[:MLA_OUT].astype(BF16)
    wog = w_o[MLA_OUT:].astype(BF16)
    return dict(win=win, qg=q_norm_g[None, :], wq=wq, kvg=kv_norm_g[None, :], wk=wk, wv=wv, inv=inv, rope=rope, one=one,
                lng=gm_ln_g[None, :], lnb=gm_ln_b[None, :], gavg=gavg, ws=gm_w_s, bias=bias, gog=gm_out_g[None, :],
                woa=woa, wog=wog, mog=mla_out_g[:, None], l1g=ln1_g[None, :], l1b=ln1_b[None, :])


def _moe(x1, w_rg, b_rg, w_re, b_re, w_gate, w_up, w_down):
    T, D = x1.shape
    pad = jnp.zeros((D, LANES - N_GROUPS - N_EXPERTS), F32)
    wr = jnp.concatenate([w_rg, w_re, pad], axis=1)
    br = jnp.concatenate([b_rg, b_re, pad[0]])[None, :]
    info, info_t, cnt = _route(x1, wr, br)

    bm = EXPERT_ROWS
    n_blocks = (T * TOP_K) // bm + N_EXPERTS
    counts = cnt[0, R_OFF:R_OFF + N_EXPERTS].astype(jnp.int32)
    padded = (counts + bm - 1) // bm * bm
    pad_ends = jnp.cumsum(padded)
    pad_starts = pad_ends - padded
    def dest_rows(e_lane, r_lane):
        e = info_t[:, e_lane, :].astype(jnp.int32)
        ids = jnp.arange(N_EXPERTS)[:, None, None]
        seg_start = jnp.sum(jnp.where(e[None] == ids, pad_starts[:, None, None], 0), axis=0)
        return ((seg_start + info_t[:, r_lane, :].astype(jnp.int32)) * TOKEN_ROWS).reshape(T // MOVE_ROWS, MOVE_ROWS)

    dest = jnp.concatenate([dest_rows(I_E0, I_R0), dest_rows(I_E1, I_R1)], axis=1)[:, None, :]
    block_start = jnp.arange(n_blocks, dtype=jnp.int32) * bm
    block_expert = jnp.minimum(jnp.sum(pad_ends[None, :] <= block_start[:, None], axis=1),
                               N_EXPERTS - 1).astype(jnp.int32)

    blk = jnp.arange(n_blocks)
    later = (blk[None, :] > blk[:, None]) & (block_expert[None, :] != block_expert[:, None])
    next_expert = jnp.min(jnp.where(later, block_expert[None, :], N_EXPERTS), axis=1)
    next_expert = jnp.where(next_expert == N_EXPERTS, -1, next_expert).astype(jnp.int32)
    n_used = (pad_ends[-1:] // bm).astype(jnp.int32)

    seg = jnp.stack([pad_ends, padded, jnp.broadcast_to(n_used, (N_EXPERTS,))]).astype(jnp.int32)
    buf = _dispatch(seg, dest, x1, n_blocks * bm)
    y = _experts(block_expert, next_expert, n_used, buf, w_gate, w_up, w_down)
    return info, dest, y


def kernel(x, p, positions, w_in, q_norm_g, w_q_up, kv_norm_g, w_kv_up, gm_ln_g, gm_ln_b, gm_w_s, gm_b_s, mla_out_g, gm_out_g, w_o, ln1_g, ln1_b, w_rg, b_rg, w_re, b_re, w_gate, w_up, w_down, ln2_g, ln2_b, w_pg, b_pg, w_pp, ln3_g, ln3_b):
    B, S, D = x.shape
    T = B * S
    assert S % PREP_ROWS == 0 and PREP_ROWS % ATTN_ROWS == 0 and PREP_ROWS % CHUNK == 0
    assert S % (ATTN_TILES * ATTN_ROWS) == 0 and ATTN_TILES % 2 == 0
    assert T % ROUTE_ROWS == 0 and T % MOVE_ROWS == 0 and (T * TOP_K) % EXPERT_ROWS == 0
    assert D == TOKEN_ROWS * LANES and MOVE_ROWS % MOVE_UNROLL == 0
    pos4 = positions.reshape(B, S // PREP_ROWS, 1, PREP_ROWS)
    for i in range(DEPTH):
        w = _layer_weights(w_in[i], q_norm_g[i], w_q_up[i], kv_norm_g[i], w_kv_up[i], gm_ln_g[i], gm_ln_b[i],
                           gm_w_s[i], gm_b_s[i], mla_out_g[i], gm_out_g[i], w_o[i], ln1_g[i], ln1_b[i])
        q, k, vt, g = _prep(x, pos4, w)
        x1 = _attn(q, k, vt, g, x, w).reshape(T, D)
        info, dest, y = _moe(x1, w_rg[i], b_rg[i], w_re[i], b_re[i], w_gate[i], w_up[i], w_down[i])
        wf = dict(wpg=w_pg[i].astype(BF16), bpg=b_pg[i][None, :], wpp=w_pp[i].astype(BF16),
                  l2g=ln2_g[i][None, :], l2b=ln2_b[i][None, :], l3g=ln3_g[i][None, :], l3b=ln3_b[i][None, :])
        x = _final(dest, x1, info, y, p[i].reshape(T, -1), wf).reshape(B, S, D)
    return x
```

```python
import functools

import jax
import jax.numpy as jnp
from jax import lax
from jax.experimental import pallas as pl
from jax.experimental.pallas import tpu as pltpu

F32 = jnp.float32
BF16 = jnp.bfloat16

MLA_HEADS = 8
QK_NOPE = 64
QK_ROPE = 32
V_HEAD = 64
Q_RANK = 256
KV_RANK = 128
ROPE_THETA = 10000.0
MLA_OUT = MLA_HEADS * V_HEAD
GM_GROUPS = 8
GM_CH = 64
GM_OUT = GM_GROUPS * GM_CH
CHUNK = 128
N_GROUPS = 4
EXP_PER_GROUP = 8
N_EXPERTS = N_GROUPS * EXP_PER_GROUP
TOP_K = 2
EPS = 1e-6
DEPTH = 1
ALPHA = (2.0 * DEPTH) ** 0.25
SM_SCALE = (QK_NOPE + QK_ROPE) ** -0.5
LOG2E = 1.4426950408889634
MASK_VALUE = -1e30

LANES = 128
SUBLANES = 8
TOKEN_ROWS = 8
ONES_ROWS = 16
VMEM_LIMIT = 56 * 1024 * 1024

PREP_ROWS = 512
ATTN_ROWS = 256
ATTN_TILES = 4
ROUTE_ROWS = 512
MOVE_ROWS = 256
MOVE_UNROLL = 8
EXPERT_ROWS = 256

C_Q = 0
C_KV = C_Q + Q_RANK
C_KR = C_KV + KV_RANK
C_U = C_KR + LANES
C_V = C_U + GM_OUT
C_END = C_V + GM_OUT
HP = MLA_HEADS * LANES

I_E0, I_E1, I_R0, I_R1, I_G0, I_G1 = range(6)
ROUTE_OUT = 48


def _rms(v, g):
    return v * lax.rsqrt(jnp.mean(v * v, axis=-1, keepdims=True) + EPS) * g


def _ln(v, g, b):
    mu = jnp.mean(v, axis=-1, keepdims=True)
    d = v - mu
    var = jnp.mean(d * d, axis=-1, keepdims=True)
    return d * lax.rsqrt(var + EPS) * g + b


def _dot(a, b):
    return jnp.dot(a, b, preferred_element_type=F32)


def _prep_kernel(x_ref, pos_ref, win_ref, qg_ref, wq_ref, kvg_ref, wk_ref, wv_ref, inv_ref, rope_ref, one_ref,
                 lng_ref, lnb_ref, gavg_ref, ws_ref, bias_ref, gog_ref,
                 q_ref, k_ref, vt_ref, g_ref):
    rows = x_ref.shape[1]
    h = _dot(x_ref[0].astype(BF16), win_ref[...])

    ang = inv_ref[...] * pos_ref[0, 0].astype(F32)
    parts = []
    for t in (jnp.cos(ang), jnp.sin(ang)):
        hi = t.astype(BF16).astype(F32)
        parts += [hi, t - hi]
    tabs = _dot(jnp.concatenate(parts, axis=0).T.astype(BF16), rope_ref[...])
    cos_t = tabs[:, :LANES] + one_ref[...]
    sin_a = tabs[:, LANES:2 * LANES]
    sin_b = tabs[:, 2 * LANES:]
    half = QK_ROPE // 2

    def rotate(v):
        return v * cos_t + pltpu.roll(v, LANES - half, 1) * sin_a + pltpu.roll(v, half, 1) * sin_b

    cq = _rms(h[:, C_Q:C_Q + Q_RANK], qg_ref[...]).astype(BF16)
    q2 = _dot(cq, wq_ref[...])
    for hd in range(MLA_HEADS):
        lo = hd * LANES
        q_ref[0, :, lo:lo + LANES] = (rotate(q2[:, lo:lo + LANES]) * (SM_SCALE * LOG2E)).astype(BF16)

    ckv = _rms(h[:, C_KV:C_KV + KV_RANK], kvg_ref[...]).astype(BF16)
    kp = _dot(ckv, wk_ref[...])
    kr = rotate(h[:, C_KR:C_KR + LANES])
    for hd in range(MLA_HEADS):
        lo = hd * LANES
        k_ref[0, :, lo:lo + LANES] = (kp[:, lo:lo + LANES] + kr).astype(BF16)
    vp = _dot(ckv, wv_ref[...])
    for kb in range(rows // ATTN_ROWS):
        vt_ref[0, kb] = vp[kb * ATTN_ROWS:(kb + 1) * ATTN_ROWS].T.astype(BF16)

    u = jax.nn.gelu(h[:, C_U:C_U + GM_OUT])
    vv = jax.nn.gelu(h[:, C_V:C_V + GM_OUT])
    mu = _dot(vv.astype(BF16), gavg_ref[...])
    d = vv - mu
    var = _dot((d * d).astype(BF16), gavg_ref[...])
    vn = (d * lax.rsqrt(var + EPS) * lng_ref[...] + lnb_ref[...]).astype(BF16)

    tri = lax.broadcasted_iota(jnp.int32, (CHUNK, CHUNK), 0) >= lax.broadcasted_iota(jnp.int32, (CHUNK, CHUNK), 1)
    wm = [jnp.where(tri, ws_ref[g], 0.0).astype(BF16) for g in range(GM_GROUPS)]
    low_half = lax.broadcasted_iota(jnp.int32, (CHUNK, LANES), 1) < GM_CH
    for c in range(rows // CHUNK):
        r0 = c * CHUNK
        parts = []
        for pr in range(GM_GROUPS // 2):
            tile = vn[r0:r0 + CHUNK, pr * LANES:(pr + 1) * LANES]
            parts.append(jnp.where(low_half, _dot(wm[2 * pr], tile), _dot(wm[2 * pr + 1], tile)))
        sg = jnp.concatenate(parts, axis=1) + bias_ref[...]
        gm = u[r0:r0 + CHUNK] * sg
        g_ref[0, r0:r0 + CHUNK, :] = _rms(gm, gog_ref[...]).astype(BF16)


def _prep(x, pos4, w):
    B, S, D = x.shape
    ts = PREP_ROWS
    full = lambda a: pl.BlockSpec(a.shape, lambda b, i: (0,) * a.ndim)
    consts = [w["win"], w["qg"], w["wq"], w["kvg"], w["wk"], w["wv"], w["inv"], w["rope"], w["one"],
              w["lng"], w["lnb"], w["gavg"], w["ws"], w["bias"], w["gog"]]
    return pl.pallas_call(
        _prep_kernel,
        grid=(B, S // ts),
        in_specs=[pl.BlockSpec((1, ts, D), lambda b, i: (b, i, 0)),
                  pl.BlockSpec((1, 1, 1, ts), lambda b, i: (b, i, 0, 0))] + [full(a) for a in consts],
        out_specs=[pl.BlockSpec((1, ts, HP), lambda b, i: (b, i, 0)),
                   pl.BlockSpec((1, ts, HP), lambda b, i: (b, i, 0)),
                   pl.BlockSpec((1, ts // ATTN_ROWS, MLA_OUT, ATTN_ROWS), lambda b, i: (b, i, 0, 0)),
                   pl.BlockSpec((1, ts, GM_OUT), lambda b, i: (b, i, 0))],
        out_shape=[jax.ShapeDtypeStruct((B, S, HP), BF16)] * 2
        + [jax.ShapeDtypeStruct((B, S // ATTN_ROWS, MLA_OUT, ATTN_ROWS), BF16),
           jax.ShapeDtypeStruct((B, S, GM_OUT), BF16)],
        compiler_params=pltpu.CompilerParams(dimension_semantics=("parallel", "parallel"),
                                             vmem_limit_bytes=VMEM_LIMIT),
        name="prep",
    )(x, pos4, *consts)


def _attn_kernel(q_ref, k_ref, vt_ref, g_ref, x_ref, woa_ref, wog_ref, mog_ref, l1g_ref, l1b_ref,
                 o_ref, m_scr, acc_scr, sa_scr, sb_scr):
    pid = pl.program_id(1)
    tq = ATTN_ROWS
    tk = tq
    key = lax.broadcasted_iota(jnp.int32, (tk, tq), 0)
    qry = lax.broadcasted_iota(jnp.int32, (tk, tq), 1)
    diag_mask = key <= qry
    ones = jnp.ones((ONES_ROWS, tk), BF16)

    def tile(t):
        r0 = t * tq
        i = ATTN_TILES * pid + t
        odd = t % 2 == 1
        m_scr[...] = jnp.full(m_scr.shape, MASK_VALUE, F32)
        acc_scr[...] = jnp.zeros(acc_scr.shape, F32)

        def scores(j, s_scr):
            k0 = pl.multiple_of(j * tk, tk)
            for hd in range(MLA_HEADS):
                lo = hd * LANES
                qh = q_ref[0, r0:r0 + tq, lo:lo + LANES]
                kj = k_ref[0, pl.ds(k0, tk), lo:lo + LANES]
                s_scr[hd] = lax.dot_general(kj, qh, (((1,), (1,)), ((), ())), preferred_element_type=F32)

        def update(j, s_scr, masked):
            for hd in range(MLA_HEADS):
                s = s_scr[hd]
                vt = vt_ref[0, j, hd * V_HEAD:(hd + 1) * V_HEAD, :]
                if masked:
                    s = jnp.where(diag_mask, s, MASK_VALUE)
                m_prev = m_scr[hd]
                m_new = jnp.maximum(m_prev, jnp.max(s, axis=0, keepdims=True))
                p = jnp.exp2(s - m_new).astype(BF16)
                scale = jnp.exp2(m_prev - m_new)
                acc_scr[hd] = scale * acc_scr[hd] + _dot(jnp.concatenate([vt, ones], axis=0), p)
                m_scr[hd] = m_new

        def pair(jj, c):
            j = 2 * jj
            scores(j + 1, sb_scr)
            update(j, sa_scr, False)
            scores(j + 2, sa_scr)
            update(j + 1, sb_scr, False)
            return c

        scores(0, sa_scr)
        lax.fori_loop(0, (ATTN_TILES // 2) * pid + t // 2, pair, 0)
        if odd:
            scores(i, sb_scr)
            update(i - 1, sa_scr, False)
            update(i, sb_scr, True)
        else:
            update(i, sa_scr, True)

        at = jnp.concatenate([acc_scr[hd, :V_HEAD] / acc_scr[hd, V_HEAD:V_HEAD + 1] for hd in range(MLA_HEADS)],
                             axis=0)
        at = at * lax.rsqrt(jnp.mean(at * at, axis=0, keepdims=True) + EPS) * mog_ref[...]
        mix = _dot(at.T.astype(BF16), woa_ref[...]) + _dot(g_ref[0, r0:r0 + tq, :], wog_ref[...])
        o_ref[0, r0:r0 + tq, :] = _ln(ALPHA * x_ref[0, r0:r0 + tq, :] + mix, l1g_ref[...], l1b_ref[...])

    for t in range(ATTN_TILES):
        tile(t)


def _attn(q, k, vt, g, x, w):
    B, S, D = x.shape
    tq = ATTN_ROWS
    rows = ATTN_TILES * tq
    full = lambda a: pl.BlockSpec(a.shape, lambda b, i: (0,) * a.ndim)
    consts = [w["woa"], w["wog"], w["mog"], w["l1g"], w["l1b"]]
    return pl.pallas_call(
        _attn_kernel,
        grid=(B, S // rows),
        in_specs=[pl.BlockSpec((1, rows, HP), lambda b, i: (b, i, 0)),
                  pl.BlockSpec((1, S, HP), lambda b, i: (b, 0, 0)),
                  pl.BlockSpec((1,) + vt.shape[1:], lambda b, i: (b, 0, 0, 0)),
                  pl.BlockSpec((1, rows, GM_OUT), lambda b, i: (b, i, 0)),
                  pl.BlockSpec((1, rows, D), lambda b, i: (b, i, 0))] + [full(a) for a in consts],
        out_specs=pl.BlockSpec((1, rows, D), lambda b, i: (b, i, 0)),
        out_shape=jax.ShapeDtypeStruct((B, S, D), F32),
        scratch_shapes=[pltpu.VMEM((MLA_HEADS, 1, tq), F32),
                        pltpu.VMEM((MLA_HEADS, V_HEAD + ONES_ROWS, tq), F32),
                        pltpu.VMEM((MLA_HEADS, tq, tq), F32), pltpu.VMEM((MLA_HEADS, tq, tq), F32)],
        compiler_params=pltpu.CompilerParams(dimension_semantics=("parallel", "parallel"),
                                             vmem_limit_bytes=VMEM_LIMIT),
        name="attn",
    )(q, k, vt, g, x, *consts)


def _route_kernel(x_ref, wr_ref, br_ref, info_ref, infot_ref, cnt_ref, carry_scr, tri_scr):
    step = pl.program_id(0)
    tt = x_ref.shape[0]

    @pl.when(step == 0)
    def _():
        carry_scr[...] = jnp.zeros_like(carry_scr)
        s = lax.broadcasted_iota(jnp.int32, (tt, tt), 0)
        t = lax.broadcasted_iota(jnp.int32, (tt, tt), 1)
        tri_scr[...] = jnp.where(s < t, 1.0, 0.0).astype(BF16)

    x = x_ref[...]
    xh = x.astype(BF16)
    xl = (x - xh.astype(F32)).astype(BF16)
    wr = wr_ref[...]
    wh = wr.astype(BF16)
    wl = (wr - wh.astype(F32)).astype(BF16)
    nt = (((1,), (1,)), ((), ()))
    logits = (lax.dot_general(wh, xh, nt, preferred_element_type=F32)
              + lax.dot_general(wh, xl, nt, preferred_element_type=F32)
              + lax.dot_general(wl, xh, nt, preferred_element_type=F32)) + br_ref[...]
    neg = jnp.float32(-jnp.inf)

    lg = logits[N_EXPERTS:N_EXPERTS + SUBLANES]
    grow = lax.broadcasted_iota(jnp.int32, lg.shape, 0)
    lg = jnp.where(grow < N_GROUPS, lg, neg)
    gmax = jnp.max(lg, axis=0, keepdims=True)
    g_idx = jnp.min(jnp.where(lg == gmax, grow, SUBLANES), axis=0, keepdims=True)
    g_p = 1.0 / jnp.sum(jnp.exp(lg - gmax), axis=0, keepdims=True)

    le = logits[:N_EXPERTS]
    row = lax.broadcasted_iota(jnp.int32, le.shape, 0)
    le = jnp.where((row >> 3) == g_idx, le, neg)
    m1 = jnp.max(le, axis=0, keepdims=True)
    i1 = jnp.min(jnp.where(le == m1, row, N_EXPERTS), axis=0, keepdims=True)
    le2 = jnp.where(row == i1, neg, le)
    m2 = jnp.max(le2, axis=0, keepdims=True)
    i2 = jnp.min(jnp.where(le2 == m2, row, N_EXPERTS), axis=0, keepdims=True)
    e2 = jnp.exp(m2 - m1)
    gate0 = g_p / (1.0 + e2)
    gate1 = g_p * e2 / (1.0 + e2)

    hit1 = row == i1
    hit2 = row == i2
    onehot = jnp.where(hit1 | hit2, 1.0, 0.0)
    before = _dot(onehot.astype(BF16), tri_scr[...]) + carry_scr[...]
    rank0 = jnp.sum(jnp.where(hit1, before, 0.0), axis=0, keepdims=True)
    rank1 = jnp.sum(jnp.where(hit2, before, 0.0), axis=0, keepdims=True)
    carry_scr[...] = carry_scr[...] + jnp.sum(onehot, axis=1, keepdims=True)
    cnt_ref[...] = jnp.broadcast_to(carry_scr[...], cnt_ref.shape)

    fields = jnp.concatenate([i1.astype(F32), i2.astype(F32), rank0, rank1, gate0, gate1,
                              jnp.zeros((SUBLANES - 6, tt), F32)], axis=0)
    infot_ref[0] = fields
    info_ref[...] = jnp.concatenate([fields, jnp.zeros((LANES - SUBLANES, tt), F32)], axis=0).T


def _route(x1, wr, br):
    T, D = x1.shape
    tt = ROUTE_ROWS
    return pl.pallas_call(
        _route_kernel,
        grid=(T // tt,),
        in_specs=[pl.BlockSpec((tt, D), lambda i: (i, 0)),
                  pl.BlockSpec(wr.shape, lambda i: (0, 0)),
                  pl.BlockSpec(br.shape, lambda i: (0, 0))],
        out_specs=[pl.BlockSpec((tt, LANES), lambda i: (i, 0)),
                   pl.BlockSpec((1, SUBLANES, tt), lambda i: (i, 0, 0)),
                   pl.BlockSpec((N_EXPERTS, LANES), lambda i: (0, 0))],
        out_shape=[jax.ShapeDtypeStruct((T, LANES), F32), jax.ShapeDtypeStruct((T // tt, SUBLANES, tt), F32),
                   jax.ShapeDtypeStruct((N_EXPERTS, LANES), F32)],
        scratch_shapes=[pltpu.VMEM((N_EXPERTS, 1), F32), pltpu.VMEM((tt, tt), BF16)],
        compiler_params=pltpu.CompilerParams(dimension_semantics=("arbitrary",), vmem_limit_bytes=VMEM_LIMIT),
        name="route",
    )(x1, wr, br)


def _to_token_tiles(dst_ref, val):
    dst_ref[...] = val.astype(BF16).reshape(dst_ref.shape)


def _from_token_tiles(src_ref, rows):
    return src_ref[...].reshape(rows, TOKEN_ROWS * LANES)


def _to_token_tiles_f32(dst_ref, val):
    rows = val.shape[0]
    for c in range(TOKEN_ROWS):
        dst_ref[pl.ds(c, rows, stride=TOKEN_ROWS), :] = val[:, c * LANES:(c + 1) * LANES]


def _from_token_tiles_f32(src_ref, rows):
    return jnp.concatenate([src_ref[pl.ds(c, rows, stride=TOKEN_ROWS), :] for c in range(TOKEN_ROWS)], axis=1)


def _tile_copy(src_ref, src_row, dst_ref, dst_row, sem):
    return pltpu.make_async_copy(src_ref.at[pl.ds(pl.multiple_of(src_row, TOKEN_ROWS), TOKEN_ROWS)],
                                 dst_ref.at[pl.ds(pl.multiple_of(dst_row, TOKEN_ROWS), TOKEN_ROWS)], sem)


def _dispatch_kernel(seg_ref, dest_ref, x0_ref, xn_ref, buf_ref, stage_scr, zero_scr, sem, zero_sem, *, n_steps):
    i = pl.program_id(0)
    rows = xn_ref.shape[0]
    cur = i % 3
    nxt = (i + 1) % 3

    @pl.when(i == 0)
    def _():
        zero_scr[...] = jnp.zeros(zero_scr.shape, BF16)

        block = EXPERT_ROWS * TOKEN_ROWS
        n_blocks = buf_ref.shape[0] // block

        def clear_rows(first):
            return pltpu.make_async_copy(zero_scr, buf_ref.at[pl.ds(pl.multiple_of(first, SUBLANES), block)], zero_sem)

        def clear(e):
            return clear_rows((seg_ref[0, e] - EXPERT_ROWS) * TOKEN_ROWS)

        def start_tail(b, c):
            clear_rows(b * block).start()
            return c

        def wait_tail(b, c):
            clear_rows(b * block).wait()
            return c

        for e in range(N_EXPERTS):
            pl.when(seg_ref[1, e] > 0)(lambda e=e: clear(e).start())
        lax.fori_loop(seg_ref[2, 0], n_blocks, start_tail, 0)
        for e in range(N_EXPERTS):
            pl.when(seg_ref[1, e] > 0)(lambda e=e: clear(e).wait())
        lax.fori_loop(seg_ref[2, 0], n_blocks, wait_tail, 0)

        _to_token_tiles(stage_scr.at[0], x0_ref[...])

    def drain(s):
        for _ in range(TOP_K):
            pltpu.make_async_copy(stage_scr.at[s], stage_scr.at[s], sem.at[s]).wait()

    @pl.when(i >= 2)
    def _():
        drain(nxt)

    _to_token_tiles(stage_scr.at[nxt], xn_ref[...])
    for r in range(rows):
        for kk in range(TOP_K):
            _tile_copy(stage_scr.at[cur], r * TOKEN_ROWS, buf_ref, dest_ref[0, 0, kk * rows + r],
                       sem.at[cur]).start(priority=kk)

    @pl.when(i == n_steps - 1)
    def _():
        drain(cur)
        if n_steps >= 2:
            drain((i + 2) % 3)


def _dispatch(seg, dest3, x1, n_rows):
    T, D = x1.shape
    td = MOVE_ROWS
    n_steps = T // td
    grid_spec = pltpu.PrefetchScalarGridSpec(
        num_scalar_prefetch=1,
        grid=(n_steps,),
        in_specs=[pl.BlockSpec((1, 1, TOP_K * td), lambda i, seg: (i, 0, 0), memory_space=pltpu.SMEM),
                  pl.BlockSpec((td, D), lambda i, seg: (0, 0)),
                  pl.BlockSpec((td, D), lambda i, seg: (jnp.minimum(i + 1, n_steps - 1), 0))],
        out_specs=pl.BlockSpec(memory_space=pl.ANY),
        scratch_shapes=[pltpu.VMEM((3, td * TOKEN_ROWS, LANES), BF16),
                        pltpu.VMEM((EXPERT_ROWS * TOKEN_ROWS, LANES), BF16),
                        pltpu.SemaphoreType.DMA((3,)), pltpu.SemaphoreType.DMA(())],
    )
    return pl.pallas_call(
        functools.partial(_dispatch_kernel, n_steps=n_steps),
        grid_spec=grid_spec,
        out_shape=jax.ShapeDtypeStruct((n_rows * TOKEN_ROWS, LANES), BF16),
        compiler_params=pltpu.CompilerParams(dimension_semantics=("arbitrary",), vmem_limit_bytes=VMEM_LIMIT),
        name="dispatch",
    )(seg, dest3, x1, x1)


def _expert_kernel(be_ref, ne_ref, nu_ref, buf0_ref, bufa_ref, bufb_ref, wg_hbm, wu_hbm, wd_hbm, y_ref,
                   sg_scr, su_scr, sd_scr, wg_scr, wu_scr, wd_scr, xa_scr, xb_scr, cur_ref, sem):
    step = pl.program_id(0)
    bm = EXPERT_ROWS
    half = bm * TOKEN_ROWS

    def fetch(expert, s):
        return (pltpu.make_async_copy(wg_hbm.at[expert], sg_scr.at[s], sem.at[s, 0]),
                pltpu.make_async_copy(wu_hbm.at[expert], su_scr.at[s], sem.at[s, 1]),
                pltpu.make_async_copy(wd_hbm.at[expert], sd_scr.at[s], sem.at[s, 2]))

    @pl.when(step == 0)
    def _():
        cur_ref[0] = 0
        for c in fetch(be_ref[0], 0):
            c.start()
        xa_scr[...] = _from_token_tiles(buf0_ref, bm)

    def load_weights(blk):
        e = be_ref[blk]

        @pl.when((blk == 0) | (be_ref[jnp.maximum(blk - 1, 0)] != e))
        def _():
            s = cur_ref[0]
            for c in fetch(e, s):
                c.wait()
            wg_scr[...] = sg_scr[s].astype(BF16)
            wu_scr[...] = su_scr[s].astype(BF16)
            wd_scr[...] = sd_scr[s].astype(BF16)
            nxt = ne_ref[blk]

            @pl.when(nxt >= 0)
            def _():
                for c in fetch(nxt, 1 - s):
                    c.start()

            cur_ref[0] = 1 - s

    def compute(x_scr, nxt_ref, nxt_scr, out_rows):
        nxt_scr[...] = _from_token_tiles(nxt_ref, bm)
        xb = x_scr[...]
        hidden = jax.nn.silu(_dot(xb, wg_scr[...])) * _dot(xb, wu_scr[...])
        _to_token_tiles_f32(y_ref.at[out_rows], _dot(hidden.astype(BF16), wd_scr[...]))

    def run(blk, x_scr, nxt_ref, nxt_scr, out_rows):
        @pl.when(blk < nu_ref[0])
        def _():
            compute(x_scr, nxt_ref, nxt_scr, out_rows)

        @pl.when(blk >= nu_ref[0])
        def _():
            y_ref[out_rows, :] = jnp.zeros((half, LANES), F32)

    blk_a, blk_b = 2 * step, 2 * step + 1
    rows_a, rows_b = pl.ds(0, half), pl.ds(half, half)
    load_weights(blk_a)
    same = (be_ref[blk_a] == be_ref[blk_b]) & (blk_b < nu_ref[0])

    @pl.when(same)
    def _():
        compute(xa_scr, bufa_ref, xb_scr, rows_a)
        compute(xb_scr, bufb_ref, xa_scr, rows_b)

    @pl.when(jnp.logical_not(same))
    def _():
        run(blk_a, xa_scr, bufa_ref, xb_scr, rows_a)
        load_weights(blk_b)
        run(blk_b, xb_scr, bufb_ref, xa_scr, rows_b)


def _experts(block_expert, next_expert, n_used, buf, w_gate, w_up, w_down):
    bm = EXPERT_ROWS
    D, ff = w_gate.shape[1:]
    n_blocks = buf.shape[0] // (bm * TOKEN_ROWS)
    assert n_blocks % 2 == 0
    last = n_blocks - 1
    grid_spec = pltpu.PrefetchScalarGridSpec(
        num_scalar_prefetch=3,
        grid=(n_blocks // 2,),
        in_specs=[pl.BlockSpec((bm * TOKEN_ROWS, LANES), lambda s, *_: (0, 0)),
                  pl.BlockSpec((bm * TOKEN_ROWS, LANES), lambda s, *_: (2 * s + 1, 0)),
                  pl.BlockSpec((bm * TOKEN_ROWS, LANES), lambda s, *_: (jnp.minimum(2 * s + 2, last), 0)),
                  pl.BlockSpec(memory_space=pl.ANY),
                  pl.BlockSpec(memory_space=pl.ANY),
                  pl.BlockSpec(memory_space=pl.ANY)],
        out_specs=pl.BlockSpec((2 * bm * TOKEN_ROWS, LANES), lambda s, *_: (s, 0)),
        scratch_shapes=[pltpu.VMEM((2, D, ff), F32), pltpu.VMEM((2, D, ff), F32), pltpu.VMEM((2, ff, D), F32),
                        pltpu.VMEM((D, ff), BF16), pltpu.VMEM((D, ff), BF16), pltpu.VMEM((ff, D), BF16),
                        pltpu.VMEM((bm, D), BF16), pltpu.VMEM((bm, D), BF16),
                        pltpu.SMEM((1,), jnp.int32), pltpu.SemaphoreType.DMA((2, 3))],
    )
    return pl.pallas_call(
        _expert_kernel,
        grid_spec=grid_spec,
        out_shape=jax.ShapeDtypeStruct(buf.shape, F32),
        compiler_params=pltpu.CompilerParams(dimension_semantics=("arbitrary",), vmem_limit_bytes=VMEM_LIMIT),
        name="experts",
    )(block_expert, next_expert, n_used, buf, buf, buf, w_gate, w_up, w_down)


def _final_kernel(dcur_ref, dnxt_ref, x_ref, info_ref, y_ref, p_ref, wpg_ref, bpg_ref, wpp_ref,
                  l2g_ref, l2b_ref, l3g_ref, l3b_ref, o_ref, rows_scr, sem):
    i = pl.program_id(0)
    last = pl.num_programs(0) - 1
    rows = x_ref.shape[0]
    slot = i % 2

    def row_copy(dref, s, r, kk):
        return _tile_copy(y_ref, dref[0, 0, kk * rows + r], rows_scr.at[s, kk], r * TOKEN_ROWS, sem.at[s])

    def landed(s):
        pltpu.make_async_copy(rows_scr.at[s], rows_scr.at[s], sem.at[s]).wait()

    @pl.when(i == 0)
    def _():
        def start(c, carry):
            for u in range(MOVE_UNROLL):
                for kk in range(TOP_K):
                    row_copy(dcur_ref, 0, c * MOVE_UNROLL + u, kk).start(priority=kk)
            return carry

        lax.fori_loop(0, rows // MOVE_UNROLL, start, 0)

    landed(slot)
    info = info_ref[...]
    gate0 = info[:, I_G0:I_G0 + 1]
    gate1 = info[:, I_G1:I_G1 + 1]
    moe = (_from_token_tiles_f32(rows_scr.at[slot, 0], rows) * gate0
           + _from_token_tiles_f32(rows_scr.at[slot, 1], rows) * gate1)

    for r in range(rows):
        for kk in range(TOP_K):
            row_copy(dnxt_ref, 1 - slot, r, kk).start(priority=kk)

    pp = _dot(p_ref[...].astype(BF16), wpp_ref[...])
    x2 = _ln(ALPHA * x_ref[...] + moe, l2g_ref[...], l2b_ref[...])
    gate = jax.nn.sigmoid(_dot(x2.astype(BF16), wpg_ref[...]) + bpg_ref[...])
    o_ref[...] = _ln(ALPHA * x2 + gate * pp, l3g_ref[...], l3b_ref[...])

    @pl.when(i == last)
    def _():
        landed(1 - slot)


def _final(dest3, x1, info, y, p2, w):
    T, D = x1.shape
    tc = MOVE_ROWS
    pd = p2.shape[1]
    full = lambda a: pl.BlockSpec(a.shape, lambda i: (0,) * a.ndim)
    consts = [w["wpg"], w["bpg"], w["wpp"], w["l2g"], w["l2b"], w["l3g"], w["l3b"]]
    last = T // tc - 1
    return pl.pallas_call(
        _final_kernel,
        grid=(T // tc,),
        in_specs=[pl.BlockSpec((1, 1, TOP_K * tc), lambda i: (i, 0, 0), memory_space=pltpu.SMEM),
                  pl.BlockSpec((1, 1, TOP_K * tc), lambda i: (jnp.minimum(i + 1, last), 0, 0), memory_space=pltpu.SMEM),
                  pl.BlockSpec((tc, D), lambda i: (i, 0)),
                  pl.BlockSpec((tc, LANES), lambda i: (i, 0)),
                  pl.BlockSpec(memory_space=pl.ANY),
                  pl.BlockSpec((tc, pd), lambda i: (i, 0))] + [full(a) for a in consts],
        out_specs=pl.BlockSpec((tc, D), lambda i: (i, 0)),
        out_shape=jax.ShapeDtypeStruct((T, D), F32),
        scratch_shapes=[pltpu.VMEM((2, TOP_K, tc * TOKEN_ROWS, LANES), F32), pltpu.SemaphoreType.DMA((2,))],
        compiler_params=pltpu.CompilerParams(dimension_semantics=("arbitrary",), vmem_limit_bytes=VMEM_LIMIT),
        name="final",
    )(dest3, dest3, x1, info, y, p2, *consts)


def _pad_heads(a, width):
    lead = a.shape[:-1]
    a = a.reshape(lead + (MLA_HEADS, width))
    a = jnp.pad(a, [(0, 0)] * len(lead) + [(0, 0), (0, LANES - width)])
    return a.reshape(lead + (HP,))


def _layer_weights(w_in, q_norm_g, w_q_up, kv_norm_g, w_kv_up, gm_ln_g, gm_ln_b, gm_w_s, gm_b_s,
                   mla_out_g, gm_out_g, w_o, ln1_g, ln1_b):
    D = w_in.shape[0]
    half = QK_ROPE // 2
    c1, c2, c3 = Q_RANK, Q_RANK + KV_RANK, Q_RANK + KV_RANK + QK_ROPE
    zeros = lambda *s: jnp.zeros(s, F32)
    kr = jnp.concatenate([zeros(D, QK_NOPE), w_in[:, c2:c3], zeros(D, LANES - QK_NOPE - QK_ROPE)], axis=1)
    win = jnp.concatenate([w_in[:, :c2], kr, w_in[:, c3:]], axis=1).astype(BF16)
    wq = _pad_heads(w_q_up, QK_NOPE + QK_ROPE).astype(BF16)

    wkv3 = w_kv_up.reshape(KV_RANK, MLA_HEADS, QK_NOPE + V_HEAD)
    wk = _pad_heads(wkv3[..., :QK_NOPE].reshape(KV_RANK, -1), QK_NOPE).astype(BF16)
    wv = wkv3[..., QK_NOPE:].reshape(KV_RANK, -1).astype(BF16)

    inv = (ROPE_THETA ** (-jnp.arange(0, QK_ROPE, 2, dtype=F32) / QK_ROPE))[:, None]
    eye = jnp.eye(half, dtype=F32)
    first = jnp.pad(eye, ((0, 0), (QK_NOPE, LANES - QK_NOPE - half)))
    second = jnp.pad(eye, ((0, 0), (QK_NOPE + half, LANES - QK_NOPE - QK_ROPE)))
    zero = jnp.zeros_like(first)
    cos_rows = jnp.concatenate([first + second, zero, zero], axis=1)
    sin_rows = jnp.concatenate([zero, -first, second], axis=1)
    rope = jnp.concatenate([cos_rows, cos_rows, sin_rows, sin_rows], axis=0).astype(BF16)
    lane = jnp.arange(LANES)
    one = jnp.where((lane >= QK_NOPE) & (lane < QK_NOPE + QK_ROPE), 0.0, 1.0)[None, :]

    grp = jnp.arange(GM_OUT) // GM_CH
    gavg = jnp.where(grp[:, None] == grp[None, :], 1.0 / GM_CH, 0.0).astype(BF16)
    bias = jnp.repeat(gm_b_s.T, GM_CH, axis=1)

    woa = w_o[:MLA_OUT].astype(BF16)
    wog = w_o[MLA_OUT:].astype(BF16)
    return dict(win=win, qg=q_norm_g[None, :], wq=wq, kvg=kv_norm_g[None, :], wk=wk, wv=wv, inv=inv, rope=rope, one=one,
                lng=gm_ln_g[None, :], lnb=gm_ln_b[None, :], gavg=gavg, ws=gm_w_s, bias=bias, gog=gm_out_g[None, :],
                woa=woa, wog=wog, mog=mla_out_g[:, None], l1g=ln1_g[None, :], l1b=ln1_b[None, :])


def _moe(x1, w_rg, b_rg, w_re, b_re, w_gate, w_up, w_down):
    T, D = x1.shape
    pad = ROUTE_OUT - N_EXPERTS - N_GROUPS
    wr = jnp.concatenate([w_re.T, w_rg.T, jnp.zeros((pad, D), F32)], axis=0)
    br = jnp.concatenate([b_re, b_rg, jnp.zeros((pad,), F32)])[:, None]
    info, info_t, cnt = _route(x1, wr, br)

    bm = EXPERT_ROWS
    n_blocks = (T * TOP_K) // bm + N_EXPERTS
    counts = cnt[:, 0].astype(jnp.int32)
    padded = (counts + bm - 1) // bm * bm
    pad_ends = jnp.cumsum(padded)
    pad_starts = pad_ends - padded
    def dest_rows(e_lane, r_lane):
        e = info_t[:, e_lane, :].astype(jnp.int32)
        ids = jnp.arange(N_EXPERTS)[:, None, None]
        seg_start = jnp.sum(jnp.where(e[None] == ids, pad_starts[:, None, None], 0), axis=0)
        return ((seg_start + info_t[:, r_lane, :].astype(jnp.int32)) * TOKEN_ROWS).reshape(T // MOVE_ROWS, MOVE_ROWS)

    dest = jnp.concatenate([dest_rows(I_E0, I_R0), dest_rows(I_E1, I_R1)], axis=1)[:, None, :]
    block_start = jnp.arange(n_blocks, dtype=jnp.int32) * bm
    block_expert = jnp.minimum(jnp.sum(pad_ends[None, :] <= block_start[:, None], axis=1),
                               N_EXPERTS - 1).astype(jnp.int32)

    blk = jnp.arange(n_blocks)
    later = (blk[None, :] > blk[:, None]) & (block_expert[None, :] != block_expert[:, None])
    next_expert = jnp.min(jnp.where(later, block_expert[None, :], N_EXPERTS), axis=1)
    next_expert = jnp.where(next_expert == N_EXPERTS, -1, next_expert).astype(jnp.int32)
    n_used = (pad_ends[-1:] // bm).astype(jnp.int32)

    seg = jnp.stack([pad_ends, padded, jnp.broadcast_to(n_used, (N_EXPERTS,))]).astype(jnp.int32)
    buf = _dispatch(seg, dest, x1, n_blocks * bm)
    y = _experts(block_expert, next_expert, n_used, buf, w_gate, w_up, w_down)
    return info, dest, y


def kernel(x, p, positions, w_in, q_norm_g, w_q_up, kv_norm_g, w_kv_up, gm_ln_g, gm_ln_b, gm_w_s, gm_b_s, mla_out_g, gm_out_g, w_o, ln1_g, ln1_b, w_rg, b_rg, w_re, b_re, w_gate, w_up, w_down, ln2_g, ln2_b, w_pg, b_pg, w_pp, ln3_g, ln3_b):
    B, S, D = x.shape
    T = B * S
    assert S % PREP_ROWS == 0 and PREP_ROWS % ATTN_ROWS == 0 and PREP_ROWS % CHUNK == 0
    assert S % (ATTN_TILES * ATTN_ROWS) == 0 and ATTN_TILES % 2 == 0
    assert T % ROUTE_ROWS == 0 and T % MOVE_ROWS == 0 and (T * TOP_K) % EXPERT_ROWS == 0
    assert D == TOKEN_ROWS * LANES and MOVE_ROWS % MOVE_UNROLL == 0
    pos4 = positions.reshape(B, S // PREP_ROWS, 1, PREP_ROWS)
    for i in range(DEPTH):
        w = _layer_weights(w_in[i], q_norm_g[i], w_q_up[i], kv_norm_g[i], w_kv_up[i], gm_ln_g[i], gm_ln_b[i],
                           gm_w_s[i], gm_b_s[i], mla_out_g[i], gm_out_g[i], w_o[i], ln1_g[i], ln1_b[i])
        q, k, vt, g = _prep(x, pos4, w)
        x1 = _attn(q, k, vt, g, x, w).reshape(T, D)
        info, dest, y = _moe(x1, w_rg[i], b_rg[i], w_re[i], b_re[i], w_gate[i], w_up[i], w_down[i])
        wf = dict(wpg=w_pg[i].astype(BF16), bpg=b_pg[i][None, :], wpp=w_pp[i].astype(BF16),
                  l2g=ln2_g[i][None, :], l2b=ln2_b[i][None, :], l3g=ln3_g[i][None, :], l3b=ln3_b[i][None, :])
        x = _final(dest, x1, info, y, p[i].reshape(T, -1), wf).reshape(B, S, D)
    return x
```

```python
import functools

import jax
import jax.numpy as jnp
from jax import lax
from jax.experimental import pallas as pl
from jax.experimental.pallas import tpu as pltpu

F32 = jnp.float32
BF16 = jnp.bfloat16

MLA_HEADS = 8
QK_NOPE = 64
QK_ROPE = 32
V_HEAD = 64
Q_RANK = 256
KV_RANK = 128
ROPE_THETA = 10000.0
MLA_OUT = MLA_HEADS * V_HEAD
GM_GROUPS = 8
GM_CH = 64
GM_OUT = GM_GROUPS * GM_CH
CHUNK = 128
N_GROUPS = 4
EXP_PER_GROUP = 8
N_EXPERTS = N_GROUPS * EXP_PER_GROUP
TOP_K = 2
EPS = 1e-6
DEPTH = 1
ALPHA = (2.0 * DEPTH) ** 0.25
SM_SCALE = (QK_NOPE + QK_ROPE) ** -0.5
LOG2E = 1.4426950408889634
MASK_VALUE = -1e30

LANES = 128
SUBLANES = 8
TOKEN_ROWS = 8
ONES_ROWS = 16
VMEM_LIMIT = 56 * 1024 * 1024

PREP_ROWS = 512
ATTN_ROWS = 256
ATTN_TILES = 4
ROUTE_ROWS = 512
MOVE_ROWS = 256
MOVE_UNROLL = 8
EXPERT_ROWS = 256

C_Q = 0
C_KV = C_Q + Q_RANK
C_KR = C_KV + KV_RANK
C_U = C_KR + LANES
C_V = C_U + GM_OUT
C_END = C_V + GM_OUT
HP = MLA_HEADS * LANES

I_E0, I_E1, I_R0, I_R1, I_G0, I_G1 = range(6)
ROUTE_OUT = 48


def _rms(v, g):
    return v * lax.rsqrt(jnp.mean(v * v, axis=-1, keepdims=True) + EPS) * g


def _ln(v, g, b):
    mu = jnp.mean(v, axis=-1, keepdims=True)
    d = v - mu
    var = jnp.mean(d * d, axis=-1, keepdims=True)
    return d * lax.rsqrt(var + EPS) * g + b


def _dot(a, b):
    return jnp.dot(a, b, preferred_element_type=F32)


def _prep_kernel(x_ref, pos_ref, win_ref, qg_ref, wq_ref, kvg_ref, wk_ref, wv_ref, inv_ref, rope_ref, one_ref,
                 lng_ref, lnb_ref, gavg_ref, ws_ref, bias_ref, gog_ref,
                 q_ref, k_ref, vt_ref, g_ref):
    rows = x_ref.shape[1]
    h = _dot(x_ref[0].astype(BF16), win_ref[...])

    ang = inv_ref[...] * pos_ref[0, 0].astype(F32)
    parts = []
    for t in (jnp.cos(ang), jnp.sin(ang)):
        hi = t.astype(BF16).astype(F32)
        parts += [hi, t - hi]
    tabs = _dot(jnp.concatenate(parts, axis=0).T.astype(BF16), rope_ref[...])
    cos_t = tabs[:, :LANES] + one_ref[...]
    sin_a = tabs[:, LANES:2 * LANES]
    sin_b = tabs[:, 2 * LANES:]
    half = QK_ROPE // 2

    def rotate(v):
        return v * cos_t + pltpu.roll(v, LANES - half, 1) * sin_a + pltpu.roll(v, half, 1) * sin_b

    cq = _rms(h[:, C_Q:C_Q + Q_RANK], qg_ref[...]).astype(BF16)
    q2 = _dot(cq, wq_ref[...])
    for hd in range(MLA_HEADS):
        lo = hd * LANES
        q_ref[0, :, lo:lo + LANES] = (rotate(q2[:, lo:lo + LANES]) * (SM_SCALE * LOG2E)).astype(BF16)

    ckv = _rms(h[:, C_KV:C_KV + KV_RANK], kvg_ref[...]).astype(BF16)
    kp = _dot(ckv, wk_ref[...])
    kr = rotate(h[:, C_KR:C_KR + LANES])
    for hd in range(MLA_HEADS):
        lo = hd * LANES
        k_ref[0, :, lo:lo + LANES] = (kp[:, lo:lo + LANES] + kr).astype(BF16)
    vp = _dot(ckv, wv_ref[...])
    for kb in range(rows // ATTN_ROWS):
        vt_ref[0, kb] = vp[kb * ATTN_ROWS:(kb + 1) * ATTN_ROWS].T.astype(BF16)

    u = jax.nn.gelu(h[:, C_U:C_U + GM_OUT])
    vv = jax.nn.gelu(h[:, C_V:C_V + GM_OUT])
    mu = _dot(vv.astype(BF16), gavg_ref[...])
    d = vv - mu
    var = _dot((d * d).astype(BF16), gavg_ref[...])
    vn = (d * lax.rsqrt(var + EPS) * lng_ref[...] + lnb_ref[...]).astype(BF16)

    tri = lax.broadcasted_iota(jnp.int32, (CHUNK, CHUNK), 0) >= lax.broadcasted_iota(jnp.int32, (CHUNK, CHUNK), 1)
    wm = [jnp.where(tri, ws_ref[g], 0.0).astype(BF16) for g in range(GM_GROUPS)]
    low_half = lax.broadcasted_iota(jnp.int32, (CHUNK, LANES), 1) < GM_CH
    for c in range(rows // CHUNK):
        r0 = c * CHUNK
        parts = []
        for pr in range(GM_GROUPS // 2):
            tile = vn[r0:r0 + CHUNK, pr * LANES:(pr + 1) * LANES]
            parts.append(jnp.where(low_half, _dot(wm[2 * pr], tile), _dot(wm[2 * pr + 1], tile)))
        sg = jnp.concatenate(parts, axis=1) + bias_ref[...]
        gm = u[r0:r0 + CHUNK] * sg
        g_ref[0, r0:r0 + CHUNK, :] = _rms(gm, gog_ref[...]).astype(BF16)


def _prep(x, pos4, w):
    B, S, D = x.shape
    ts = PREP_ROWS
    full = lambda a: pl.BlockSpec(a.shape, lambda b, i: (0,) * a.ndim)
    consts = [w["win"], w["qg"], w["wq"], w["kvg"], w["wk"], w["wv"], w["inv"], w["rope"], w["one"],
              w["lng"], w["lnb"], w["gavg"], w["ws"], w["bias"], w["gog"]]
    return pl.pallas_call(
        _prep_kernel,
        grid=(B, S // ts),
        in_specs=[pl.BlockSpec((1, ts, D), lambda b, i: (b, i, 0)),
                  pl.BlockSpec((1, 1, 1, ts), lambda b, i: (b, i, 0, 0))] + [full(a) for a in consts],
        out_specs=[pl.BlockSpec((1, ts, HP), lambda b, i: (b, i, 0)),
                   pl.BlockSpec((1, ts, HP), lambda b, i: (b, i, 0)),
                   pl.BlockSpec((1, ts // ATTN_ROWS, MLA_OUT, ATTN_ROWS), lambda b, i: (b, i, 0, 0)),
                   pl.BlockSpec((1, ts, GM_OUT), lambda b, i: (b, i, 0))],
        out_shape=[jax.ShapeDtypeStruct((B, S, HP), BF16)] * 2
        + [jax.ShapeDtypeStruct((B, S // ATTN_ROWS, MLA_OUT, ATTN_ROWS), BF16),
           jax.ShapeDtypeStruct((B, S, GM_OUT), BF16)],
        compiler_params=pltpu.CompilerParams(dimension_semantics=("parallel", "parallel"),
                                             vmem_limit_bytes=VMEM_LIMIT),
        name="prep",
    )(x, pos4, *consts)


def _attn_kernel(q_ref, k_ref, vt_ref, g_ref, x_ref, woa_ref, wog_ref, mog_ref, l1g_ref, l1b_ref,
                 o_ref, m_scr, acc_scr, sa_scr, sb_scr):
    pid = pl.program_id(1)
    tq = ATTN_ROWS
    tk = tq
    key = lax.broadcasted_iota(jnp.int32, (tk, tq), 0)
    qry = lax.broadcasted_iota(jnp.int32, (tk, tq), 1)
    diag_mask = key <= qry
    ones = jnp.ones((ONES_ROWS, tk), BF16)

    def tile(t):
        r0 = t * tq
        i = ATTN_TILES * pid + t
        odd = t % 2 == 1
        m_scr[...] = jnp.full(m_scr.shape, MASK_VALUE, F32)
        acc_scr[...] = jnp.zeros(acc_scr.shape, F32)

        def scores(j, s_scr):
            k0 = pl.multiple_of(j * tk, tk)
            for hd in range(MLA_HEADS):
                lo = hd * LANES
                qh = q_ref[0, r0:r0 + tq, lo:lo + LANES]
                kj = k_ref[0, pl.ds(k0, tk), lo:lo + LANES]
                s_scr[hd] = lax.dot_general(kj, qh, (((1,), (1,)), ((), ())), preferred_element_type=F32)

        def update(j, s_scr, masked):
            for hd in range(MLA_HEADS):
                s = s_scr[hd]
                vt = vt_ref[0, j, hd * V_HEAD:(hd + 1) * V_HEAD, :]
                if masked:
                    s = jnp.where(diag_mask, s, MASK_VALUE)
                m_prev = m_scr[hd]
                m_new = jnp.maximum(m_prev, jnp.max(s, axis=0, keepdims=True))
                p = jnp.exp2(s - m_new).astype(BF16)
                scale = jnp.exp2(m_prev - m_new)
                acc_scr[hd] = scale * acc_scr[hd] + _dot(jnp.concatenate([vt, ones], axis=0), p)
                m_scr[hd] = m_new

        def pair(jj, c):
            j = 2 * jj
            scores(j + 1, sb_scr)
            update(j, sa_scr, False)
            scores(j + 2, sa_scr)
            update(j + 1, sb_scr, False)
            return c

        scores(0, sa_scr)
        lax.fori_loop(0, (ATTN_TILES // 2) * pid + t // 2, pair, 0)
        if odd:
            scores(i, sb_scr)
            update(i - 1, sa_scr, False)
            update(i, sb_scr, True)
        else:
            update(i, sa_scr, True)

        at = jnp.concatenate([acc_scr[hd, :V_HEAD] / acc_scr[hd, V_HEAD:V_HEAD + 1] for hd in range(MLA_HEADS)],
                             axis=0)
        at = at * lax.rsqrt(jnp.mean(at * at, axis=0, keepdims=True) + EPS) * mog_ref[...]
        mix = _dot(at.T.astype(BF16), woa_ref[...]) + _dot(g_ref[0, r0:r0 + tq, :], wog_ref[...])
        o_ref[0, r0:r0 + tq, :] = _ln(ALPHA * x_ref[0, r0:r0 + tq, :] + mix, l1g_ref[...], l1b_ref[...])

    for t in range(ATTN_TILES):
        tile(t)


def _attn(q, k, vt, g, x, w):
    B, S, D = x.shape
    tq = ATTN_ROWS
    rows = ATTN_TILES * tq
    full = lambda a: pl.BlockSpec(a.shape, lambda b, i: (0,) * a.ndim)
    consts = [w["woa"], w["wog"], w["mog"], w["l1g"], w["l1b"]]
    return pl.pallas_call(
        _attn_kernel,
        grid=(B, S // rows),
        in_specs=[pl.BlockSpec((1, rows, HP), lambda b, i: (b, i, 0)),
                  pl.BlockSpec((1, S, HP), lambda b, i: (b, 0, 0)),
                  pl.BlockSpec((1,) + vt.shape[1:], lambda b, i: (b, 0, 0, 0)),
                  pl.BlockSpec((1, rows, GM_OUT), lambda b, i: (b, i, 0)),
                  pl.BlockSpec((1, rows, D), lambda b, i: (b, i, 0))] + [full(a) for a in consts],
        out_specs=pl.BlockSpec((1, rows, D), lambda b, i: (b, i, 0)),
        out_shape=jax.ShapeDtypeStruct((B, S, D), F32),
        scratch_shapes=[pltpu.VMEM((MLA_HEADS, 1, tq), F32),
                        pltpu.VMEM((MLA_HEADS, V_HEAD + ONES_ROWS, tq), F32),
                        pltpu.VMEM((MLA_HEADS, tq, tq), F32), pltpu.VMEM((MLA_HEADS, tq, tq), F32)],
        compiler_params=pltpu.CompilerParams(dimension_semantics=("parallel", "parallel"),
                                             vmem_limit_bytes=VMEM_LIMIT),
        name="attn",
    )(q, k, vt, g, x, *consts)


def _route_kernel(x_ref, wr_ref, br_ref, info_ref, infot_ref, cnt_ref, carry_scr, tri_scr):
    step = pl.program_id(0)
    tt = x_ref.shape[0]

    @pl.when(step == 0)
    def _():
        carry_scr[...] = jnp.zeros_like(carry_scr)
        s = lax.broadcasted_iota(jnp.int32, (tt, tt), 0)
        t = lax.broadcasted_iota(jnp.int32, (tt, tt), 1)
        tri_scr[...] = jnp.where(s < t, 1.0, 0.0).astype(BF16)

    x = x_ref[...]
    xh = x.astype(BF16)
    xl = (x - xh.astype(F32)).astype(BF16)
    wr = wr_ref[...]
    wh = wr.astype(BF16)
    wl = (wr - wh.astype(F32)).astype(BF16)
    nt = (((1,), (1,)), ((), ()))
    logits = (lax.dot_general(wh, xh, nt, preferred_element_type=F32)
              + lax.dot_general(wh, xl, nt, preferred_element_type=F32)
              + lax.dot_general(wl, xh, nt, preferred_element_type=F32)) + br_ref[...]
    neg = jnp.float32(-jnp.inf)

    lg = logits[N_EXPERTS:N_EXPERTS + SUBLANES]
    grow = lax.broadcasted_iota(jnp.int32, lg.shape, 0)
    lg = jnp.where(grow < N_GROUPS, lg, neg)
    gmax = jnp.max(lg, axis=0, keepdims=True)
    g_idx = jnp.min(jnp.where(lg == gmax, grow, SUBLANES), axis=0, keepdims=True)
    g_p = 1.0 / jnp.sum(jnp.exp(lg - gmax), axis=0, keepdims=True)

    le = logits[:N_EXPERTS]
    row = lax.broadcasted_iota(jnp.int32, le.shape, 0)
    le = jnp.where((row >> 3) == g_idx, le, neg)
    m1 = jnp.max(le, axis=0, keepdims=True)
    i1 = jnp.min(jnp.where(le == m1, row, N_EXPERTS), axis=0, keepdims=True)
    le2 = jnp.where(row == i1, neg, le)
    m2 = jnp.max(le2, axis=0, keepdims=True)
    i2 = jnp.min(jnp.where(le2 == m2, row, N_EXPERTS), axis=0, keepdims=True)
    e2 = jnp.exp(m2 - m1)
    gate0 = g_p / (1.0 + e2)
    gate1 = g_p * e2 / (1.0 + e2)

    hit1 = row == i1
    hit2 = row == i2
    onehot = jnp.where(hit1 | hit2, 1.0, 0.0)
    before = _dot(onehot.astype(BF16), tri_scr[...]) + carry_scr[...]
    rank0 = jnp.sum(jnp.where(hit1, before, 0.0), axis=0, keepdims=True)
    rank1 = jnp.sum(jnp.where(hit2, before, 0.0), axis=0, keepdims=True)
    carry_scr[...] = carry_scr[...] + jnp.sum(onehot, axis=1, keepdims=True)
    cnt_ref[...] = jnp.broadcast_to(carry_scr[...], cnt_ref.shape)

    fields = jnp.concatenate([i1.astype(F32), i2.astype(F32), rank0, rank1, gate0, gate1,
                              jnp.zeros((SUBLANES - 6, tt), F32)], axis=0)
    infot_ref[0] = fields
    info_ref[...] = jnp.concatenate([fields, jnp.zeros((LANES - SUBLANES, tt), F32)], axis=0).T


def _route(x1, wr, br):
    T, D = x1.shape
    tt = ROUTE_ROWS
    return pl.pallas_call(
        _route_kernel,
        grid=(T // tt,),
        in_specs=[pl.BlockSpec((tt, D), lambda i: (i, 0)),
                  pl.BlockSpec(wr.shape, lambda i: (0, 0)),
                  pl.BlockSpec(br.shape, lambda i: (0, 0))],
        out_specs=[pl.BlockSpec((tt, LANES), lambda i: (i, 0)),
                   pl.BlockSpec((1, SUBLANES, tt), lambda i: (i, 0, 0)),
                   pl.BlockSpec((N_EXPERTS, LANES), lambda i: (0, 0))],
        out_shape=[jax.ShapeDtypeStruct((T, LANES), F32), jax.ShapeDtypeStruct((T // tt, SUBLANES, tt), F32),
                   jax.ShapeDtypeStruct((N_EXPERTS, LANES), F32)],
        scratch_shapes=[pltpu.VMEM((N_EXPERTS, 1), F32), pltpu.VMEM((tt, tt), BF16)],
        compiler_params=pltpu.CompilerParams(dimension_semantics=("arbitrary",), vmem_limit_bytes=VMEM_LIMIT),
        name="route",
    )(x1, wr, br)


def _to_token_tiles(dst_ref, val):
    dst_ref[...] = val.astype(BF16).reshape(dst_ref.shape)


def _from_token_tiles(src_ref, rows):
    return src_ref[...].reshape(rows, TOKEN_ROWS * LANES)


def _to_token_tiles_f32(dst_ref, val):
    rows = val.shape[0]
    for c in range(TOKEN_ROWS):
        dst_ref[pl.ds(c, rows, stride=TOKEN_ROWS), :] = val[:, c * LANES:(c + 1) * LANES]


def _from_token_tiles_f32(src_ref, rows):
    return jnp.concatenate([src_ref[pl.ds(c, rows, stride=TOKEN_ROWS), :] for c in range(TOKEN_ROWS)], axis=1)


def _tile_copy(src_ref, src_row, dst_ref, dst_row, sem):
    return pltpu.make_async_copy(src_ref.at[pl.ds(pl.multiple_of(src_row, TOKEN_ROWS), TOKEN_ROWS)],
                                 dst_ref.at[pl.ds(pl.multiple_of(dst_row, TOKEN_ROWS), TOKEN_ROWS)], sem)


def _dispatch_kernel(seg_ref, dest_ref, x0_ref, xn_ref, buf_ref, stage_scr, zero_scr, sem, zero_sem, *, n_steps):
    i = pl.program_id(0)
    rows = xn_ref.shape[0]
    cur = i % 3
    nxt = (i + 1) % 3

    @pl.when(i == 0)
    def _():
        zero_scr[...] = jnp.zeros(zero_scr.shape, BF16)

        block = EXPERT_ROWS * TOKEN_ROWS
        n_blocks = buf_ref.shape[0] // block

        def clear_rows(first):
            return pltpu.make_async_copy(zero_scr, buf_ref.at[pl.ds(pl.multiple_of(first, SUBLANES), block)], zero_sem)

        def clear(e):
            return clear_rows((seg_ref[0, e] - EXPERT_ROWS) * TOKEN_ROWS)

        def start_tail(b, c):
            clear_rows(b * block).start()
            return c

        def wait_tail(b, c):
            clear_rows(b * block).wait()
            return c

        for e in range(N_EXPERTS):
            pl.when(seg_ref[1, e] > 0)(lambda e=e: clear(e).start())
        lax.fori_loop(seg_ref[2, 0], n_blocks, start_tail, 0)
        for e in range(N_EXPERTS):
            pl.when(seg_ref[1, e] > 0)(lambda e=e: clear(e).wait())
        lax.fori_loop(seg_ref[2, 0], n_blocks, wait_tail, 0)

        _to_token_tiles(stage_scr.at[0], x0_ref[...])

    def drain(s):
        for _ in range(TOP_K):
            pltpu.make_async_copy(stage_scr.at[s], stage_scr.at[s], sem.at[s]).wait()

    @pl.when(i >= 2)
    def _():
        drain(nxt)

    _to_token_tiles(stage_scr.at[nxt], xn_ref[...])
    for r in range(rows):
        for kk in range(TOP_K):
            _tile_copy(stage_scr.at[cur], r * TOKEN_ROWS, buf_ref, dest_ref[0, 0, kk * rows + r],
                       sem.at[cur]).start(priority=kk)

    @pl.when(i == n_steps - 1)
    def _():
        drain(cur)
        if n_steps >= 2:
            drain((i + 2) % 3)


def _dispatch(seg, dest3, x1, n_rows):
    T, D = x1.shape
    td = MOVE_ROWS
    n_steps = T // td
    grid_spec = pltpu.PrefetchScalarGridSpec(
        num_scalar_prefetch=1,
        grid=(n_steps,),
        in_specs=[pl.BlockSpec((1, 1, TOP_K * td), lambda i, seg: (i, 0, 0), memory_space=pltpu.SMEM),
                  pl.BlockSpec((td, D), lambda i, seg: (0, 0)),
                  pl.BlockSpec((td, D), lambda i, seg: (jnp.minimum(i + 1, n_steps - 1), 0))],
        out_specs=pl.BlockSpec(memory_space=pl.ANY),
        scratch_shapes=[pltpu.VMEM((3, td * TOKEN_ROWS, LANES), BF16),
                        pltpu.VMEM((EXPERT_ROWS * TOKEN_ROWS, LANES), BF16),
                        pltpu.SemaphoreType.DMA((3,)), pltpu.SemaphoreType.DMA(())],
    )
    return pl.pallas_call(
        functools.partial(_dispatch_kernel, n_steps=n_steps),
        grid_spec=grid_spec,
        out_shape=jax.ShapeDtypeStruct((n_rows * TOKEN_ROWS, LANES), BF16),
        compiler_params=pltpu.CompilerParams(dimension_semantics=("arbitrary",), vmem_limit_bytes=VMEM_LIMIT),
        name="dispatch",
    )(seg, dest3, x1, x1)


def _expert_kernel(be_ref, ne_ref, nu_ref, buf0_ref, bufa_ref, bufb_ref, wg_hbm, wu_hbm, wd_hbm, y_ref,
                   sg_scr, su_scr, sd_scr, wg_scr, wu_scr, wd_scr, xa_scr, xb_scr, cur_ref, sem):
    step = pl.program_id(0)
    bm = EXPERT_ROWS
    half = bm * TOKEN_ROWS

    def fetch(expert, s):
        return (pltpu.make_async_copy(wg_hbm.at[expert], sg_scr.at[s], sem.at[s, 0]),
                pltpu.make_async_copy(wu_hbm.at[expert], su_scr.at[s], sem.at[s, 1]),
                pltpu.make_async_copy(wd_hbm.at[expert], sd_scr.at[s], sem.at[s, 2]))

    @pl.when(step == 0)
    def _():
        cur_ref[0] = 0
        for c in fetch(be_ref[0], 0):
            c.start()
        xa_scr[...] = _from_token_tiles(buf0_ref, bm)

    def load_weights(blk):
        e = be_ref[blk]

        @pl.when((blk == 0) | (be_ref[jnp.maximum(blk - 1, 0)] != e))
        def _():
            s = cur_ref[0]
            for c in fetch(e, s):
                c.wait()
            wg_scr[...] = sg_scr[s].astype(BF16)
            wu_scr[...] = su_scr[s].astype(BF16)
            wd_scr[...] = sd_scr[s].astype(BF16)
            nxt = ne_ref[blk]

            @pl.when(nxt >= 0)
            def _():
                for c in fetch(nxt, 1 - s):
                    c.start()

            cur_ref[0] = 1 - s

    def compute(x_scr, nxt_ref, nxt_scr, out_rows):
        nxt_scr[...] = _from_token_tiles(nxt_ref, bm)
        xb = x_scr[...]
        hidden = jax.nn.silu(_dot(xb, wg_scr[...])) * _dot(xb, wu_scr[...])
        _to_token_tiles_f32(y_ref.at[out_rows], _dot(hidden.astype(BF16), wd_scr[...]))

    def run(blk, x_scr, nxt_ref, nxt_scr, out_rows):
        @pl.when(blk < nu_ref[0])
        def _():
            compute(x_scr, nxt_ref, nxt_scr, out_rows)

        @pl.when(blk >= nu_ref[0])
        def _():
            y_ref[out_rows, :] = jnp.zeros((half, LANES), F32)

    blk_a, blk_b = 2 * step, 2 * step + 1
    rows_a, rows_b = pl.ds(0, half), pl.ds(half, half)
    load_weights(blk_a)
    same = (be_ref[blk_a] == be_ref[blk_b]) & (blk_b < nu_ref[0])

    @pl.when(same)
    def _():
        compute(xa_scr, bufa_ref, xb_scr, rows_a)
        compute(xb_scr, bufb_ref, xa_scr, rows_b)

    @pl.when(jnp.logical_not(same))
    def _():
        run(blk_a, xa_scr, bufa_ref, xb_scr, rows_a)
        load_weights(blk_b)
        run(blk_b, xb_scr, bufb_ref, xa_scr, rows_b)


def _experts(block_expert, next_expert, n_used, buf, w_gate, w_up, w_down):
    bm = EXPERT_ROWS
    D, ff = w_gate.shape[1:]
    n_blocks = buf.shape[0] // (bm * TOKEN_ROWS)
    assert n_blocks % 2 == 0
    last = n_blocks - 1
    grid_spec = pltpu.PrefetchScalarGridSpec(
        num_scalar_prefetch=3,
        grid=(n_blocks // 2,),
        in_specs=[pl.BlockSpec((bm * TOKEN_ROWS, LANES), lambda s, *_: (0, 0)),
                  pl.BlockSpec((bm * TOKEN_ROWS, LANES), lambda s, *_: (2 * s + 1, 0)),
                  pl.BlockSpec((bm * TOKEN_ROWS, LANES), lambda s, *_: (jnp.minimum(2 * s + 2, last), 0)),
                  pl.BlockSpec(memory_space=pl.ANY),
                  pl.BlockSpec(memory_space=pl.ANY),
                  pl.BlockSpec(memory_space=pl.ANY)],
        out_specs=pl.BlockSpec((2 * bm * TOKEN_ROWS, LANES), lambda s, *_: (s, 0)),
        scratch_shapes=[pltpu.VMEM((2, D, ff), F32), pltpu.VMEM((2, D, ff), F32), pltpu.VMEM((2, ff, D), F32),
                        pltpu.VMEM((D, ff), BF16), pltpu.VMEM((D, ff), BF16), pltpu.VMEM((ff, D), BF16),
                        pltpu.VMEM((bm, D), BF16), pltpu.VMEM((bm, D), BF16),
                        pltpu.SMEM((1,), jnp.int32), pltpu.SemaphoreType.DMA((2, 3))],
    )
    return pl.pallas_call(
        _expert_kernel,
        grid_spec=grid_spec,
        out_shape=jax.ShapeDtypeStruct(buf.shape, F32),
        compiler_params=pltpu.CompilerParams(dimension_semantics=("arbitrary",), vmem_limit_bytes=VMEM_LIMIT),
        name="experts",
    )(block_expert, next_expert, n_used, buf, buf, buf, w_gate, w_up, w_down)


def _final_kernel(d0_ref, d1_ref, d2_ref, x_ref, info_ref, y_ref, p_ref, wpg_ref, bpg_ref, wpp_ref,
                  l2g_ref, l2b_ref, l3g_ref, l3b_ref, o_ref, rows_scr, sem):
    i = pl.program_id(0)
    last = pl.num_programs(0) - 1
    rows = x_ref.shape[0]
    slot = i % 3
    ahead = (i + 2) % 3

    def row_copy(dref, s, r, kk):
        return _tile_copy(y_ref, dref[0, 0, kk * rows + r], rows_scr.at[s, kk], r * TOKEN_ROWS, sem.at[s])

    def landed(s):
        pltpu.make_async_copy(rows_scr.at[s], rows_scr.at[s], sem.at[s]).wait()

    @pl.when(i == 0)
    def _():
        def start(c, carry):
            for u in range(MOVE_UNROLL):
                for kk in range(TOP_K):
                    row_copy(d0_ref, 0, c * MOVE_UNROLL + u, kk).start(priority=kk)
                    row_copy(d1_ref, 1, c * MOVE_UNROLL + u, kk).start(priority=kk)
            return carry

        lax.fori_loop(0, rows // MOVE_UNROLL, start, 0)

    landed(slot)
    info = info_ref[...]
    gate0 = info[:, I_G0:I_G0 + 1]
    gate1 = info[:, I_G1:I_G1 + 1]
    moe = (_from_token_tiles_f32(rows_scr.at[slot, 0], rows) * gate0
           + _from_token_tiles_f32(rows_scr.at[slot, 1], rows) * gate1)

    for r in range(rows):
        for kk in range(TOP_K):
            row_copy(d2_ref, ahead, r, kk).start(priority=kk)

    pp = _dot(p_ref[...].astype(BF16), wpp_ref[...])
    x2 = _ln(ALPHA * x_ref[...] + moe, l2g_ref[...], l2b_ref[...])
    gate = jax.nn.sigmoid(_dot(x2.astype(BF16), wpg_ref[...]) + bpg_ref[...])
    o_ref[...] = _ln(ALPHA * x2 + gate * pp, l3g_ref[...], l3b_ref[...])

    @pl.when(i == last)
    def _():
        landed((i + 1) % 3)
        landed(ahead)


def _final(dest3, x1, info, y, p2, w):
    T, D = x1.shape
    tc = MOVE_ROWS
    pd = p2.shape[1]
    full = lambda a: pl.BlockSpec(a.shape, lambda i: (0,) * a.ndim)
    consts = [w["wpg"], w["bpg"], w["wpp"], w["l2g"], w["l2b"], w["l3g"], w["l3b"]]
    last = T // tc - 1
    assert last >= 2
    return pl.pallas_call(
        _final_kernel,
        grid=(T // tc,),
        in_specs=[pl.BlockSpec((1, 1, TOP_K * tc), lambda i: (i, 0, 0), memory_space=pltpu.SMEM),
                  pl.BlockSpec((1, 1, TOP_K * tc), lambda i: (jnp.minimum(i + 1, last), 0, 0), memory_space=pltpu.SMEM),
                  pl.BlockSpec((1, 1, TOP_K * tc), lambda i: (jnp.minimum(i + 2, last), 0, 0), memory_space=pltpu.SMEM),
                  pl.BlockSpec((tc, D), lambda i: (i, 0)),
                  pl.BlockSpec((tc, LANES), lambda i: (i, 0)),
                  pl.BlockSpec(memory_space=pl.ANY),
                  pl.BlockSpec((tc, pd), lambda i: (i, 0))] + [full(a) for a in consts],
        out_specs=pl.BlockSpec((tc, D), lambda i: (i, 0)),
        out_shape=jax.ShapeDtypeStruct((T, D), F32),
        scratch_shapes=[pltpu.VMEM((3, TOP_K, tc * TOKEN_ROWS, LANES), F32), pltpu.SemaphoreType.DMA((3,))],
        compiler_params=pltpu.CompilerParams(dimension_semantics=("arbitrary",), vmem_limit_bytes=VMEM_LIMIT),
        name="final",
    )(dest3, dest3, dest3, x1, info, y, p2, *consts)


def _pad_heads(a, width):
    lead = a.shape[:-1]
    a = a.reshape(lead + (MLA_HEADS, width))
    a = jnp.pad(a, [(0, 0)] * len(lead) + [(0, 0), (0, LANES - width)])
    return a.reshape(lead + (HP,))


def _layer_weights(w_in, q_norm_g, w_q_up, kv_norm_g, w_kv_up, gm_ln_g, gm_ln_b, gm_w_s, gm_b_s,
                   mla_out_g, gm_out_g, w_o, ln1_g, ln1_b):
    D = w_in.shape[0]
    half = QK_ROPE // 2
    c1, c2, c3 = Q_RANK, Q_RANK + KV_RANK, Q_RANK + KV_RANK + QK_ROPE
    zeros = lambda *s: jnp.zeros(s, F32)
    kr = jnp.concatenate([zeros(D, QK_NOPE), w_in[:, c2:c3], zeros(D, LANES - QK_NOPE - QK_ROPE)], axis=1)
    win = jnp.concatenate([w_in[:, :c2], kr, w_in[:, c3:]], axis=1).astype(BF16)
    wq = _pad_heads(w_q_up, QK_NOPE + QK_ROPE).astype(BF16)

    wkv3 = w_kv_up.reshape(KV_RANK, MLA_HEADS, QK_NOPE + V_HEAD)
    wk = _pad_heads(wkv3[..., :QK_NOPE].reshape(KV_RANK, -1), QK_NOPE).astype(BF16)
    wv = wkv3[..., QK_NOPE:].reshape(KV_RANK, -1).astype(BF16)

    inv = (ROPE_THETA ** (-jnp.arange(0, QK_ROPE, 2, dtype=F32) / QK_ROPE))[:, None]
    eye = jnp.eye(half, dtype=F32)
    first = jnp.pad(eye, ((0, 0), (QK_NOPE, LANES - QK_NOPE - half)))
    second = jnp.pad(eye, ((0, 0), (QK_NOPE + half, LANES - QK_NOPE - QK_ROPE)))
    zero = jnp.zeros_like(first)
    cos_rows = jnp.concatenate([first + second, zero, zero], axis=1)
    sin_rows = jnp.concatenate([zero, -first, second], axis=1)
    rope = jnp.concatenate([cos_rows, cos_rows, sin_rows, sin_rows], axis=0).astype(BF16)
    lane = jnp.arange(LANES)
    one = jnp.where((lane >= QK_NOPE) & (lane < QK_NOPE + QK_ROPE), 0.0, 1.0)[None, :]

    grp = jnp.arange(GM_OUT) // GM_CH
    gavg = jnp.where(grp[:, None] == grp[None, :], 1.0 / GM_CH, 0.0).astype(BF16)
    bias = jnp.repeat(gm_b_s.T, GM_CH, axis=1)

    woa = w_o[:MLA_OUT].astype(BF16)
    wog = w_o[MLA_OUT:].astype(BF16)
    return dict(win=win, qg=q_norm_g[None, :], wq=wq, kvg=kv_norm_g[None, :], wk=wk, wv=wv, inv=inv, rope=rope, one=one,
                lng=gm_ln_g[None, :], lnb=gm_ln_b[None, :], gavg=gavg, ws=gm_w_s, bias=bias, gog=gm_out_g[None, :],
                woa=woa, wog=wog, mog=mla_out_g[:, None], l1g=ln1_g[None, :], l1b=ln1_b[None, :])


def _moe(x1, w_rg, b_rg, w_re, b_re, w_gate, w_up, w_down):
    T, D = x1.shape
    pad = ROUTE_OUT - N_EXPERTS - N_GROUPS
    wr = jnp.concatenate([w_re.T, w_rg.T, jnp.zeros((pad, D), F32)], axis=0)
    br = jnp.concatenate([b_re, b_rg, jnp.zeros((pad,), F32)])[:, None]
    info, info_t, cnt = _route(x1, wr, br)

    bm = EXPERT_ROWS
    n_blocks = (T * TOP_K) // bm + N_EXPERTS
    counts = cnt[:, 0].astype(jnp.int32)
    padded = (counts + bm - 1) // bm * bm
    pad_ends = jnp.cumsum(padded)
    pad_starts = pad_ends - padded
    def dest_rows(e_lane, r_lane):
        e = info_t[:, e_lane, :].astype(jnp.int32)
        ids = jnp.arange(N_EXPERTS)[:, None, None]
        seg_start = jnp.sum(jnp.where(e[None] == ids, pad_starts[:, None, None], 0), axis=0)
        return ((seg_start + info_t[:, r_lane, :].astype(jnp.int32)) * TOKEN_ROWS).reshape(T // MOVE_ROWS, MOVE_ROWS)

    dest = jnp.concatenate([dest_rows(I_E0, I_R0), dest_rows(I_E1, I_R1)], axis=1)[:, None, :]
    block_start = jnp.arange(n_blocks, dtype=jnp.int32) * bm
    block_expert = jnp.minimum(jnp.sum(pad_ends[None, :] <= block_start[:, None], axis=1),
                               N_EXPERTS - 1).astype(jnp.int32)

    blk = jnp.arange(n_blocks)
    later = (blk[None, :] > blk[:, None]) & (block_expert[None, :] != block_expert[:, None])
    next_expert = jnp.min(jnp.where(later, block_expert[None, :], N_EXPERTS), axis=1)
    next_expert = jnp.where(next_expert == N_EXPERTS, -1, next_expert).astype(jnp.int32)
    n_used = (pad_ends[-1:] // bm).astype(jnp.int32)

    seg = jnp.stack([pad_ends, padded, jnp.broadcast_to(n_used, (N_EXPERTS,))]).astype(jnp.int32)
    buf = _dispatch(seg, dest, x1, n_blocks * bm)
    y = _experts(block_expert, next_expert, n_used, buf, w_gate, w_up, w_down)
    return info, dest, y


def kernel(x, p, positions, w_in, q_norm_g, w_q_up, kv_norm_g, w_kv_up, gm_ln_g, gm_ln_b, gm_w_s, gm_b_s, mla_out_g, gm_out_g, w_o, ln1_g, ln1_b, w_rg, b_rg, w_re, b_re, w_gate, w_up, w_down, ln2_g, ln2_b, w_pg, b_pg, w_pp, ln3_g, ln3_b):
    B, S, D = x.shape
    T = B * S
    assert S % PREP_ROWS == 0 and PREP_ROWS % ATTN_ROWS == 0 and PREP_ROWS % CHUNK == 0
    assert S % (ATTN_TILES * ATTN_ROWS) == 0 and ATTN_TILES % 2 == 0
    assert T % ROUTE_ROWS == 0 and T % MOVE_ROWS == 0 and (T * TOP_K) % EXPERT_ROWS == 0
    assert D == TOKEN_ROWS * LANES and MOVE_ROWS % MOVE_UNROLL == 0
    pos4 = positions.reshape(B, S // PREP_ROWS, 1, PREP_ROWS)
    for i in range(DEPTH):
        w = _layer_weights(w_in[i], q_norm_g[i], w_q_up[i], kv_norm_g[i], w_kv_up[i], gm_ln_g[i], gm_ln_b[i],
                           gm_w_s[i], gm_b_s[i], mla_out_g[i], gm_out_g[i], w_o[i], ln1_g[i], ln1_b[i])
        q, k, vt, g = _prep(x, pos4, w)
        x1 = _attn(q, k, vt, g, x, w).reshape(T, D)
        info, dest, y = _moe(x1, w_rg[i], b_rg[i], w_re[i], b_re[i], w_gate[i], w_up[i], w_down[i])
        wf = dict(wpg=w_pg[i].astype(BF16), bpg=b_pg[i][None, :], wpp=w_pp[i].astype(BF16),
                  l2g=ln2_g[i][None, :], l2b=ln2_b[i][None, :], l3g=ln3_g[i][None, :], l3b=ln3_b[i][None, :])
        x = _final(dest, x1, info, y, p[i].reshape(T, -1), wf).reshape(B, S, D)
    return x
```

```python
import functools

import jax
import jax.numpy as jnp
from jax import lax
from jax.experimental import pallas as pl
from jax.experimental.pallas import tpu as pltpu

F32 = jnp.float32
BF16 = jnp.bfloat16

MLA_HEADS = 8
QK_NOPE = 64
QK_ROPE = 32
V_HEAD = 64
Q_RANK = 256
KV_RANK = 128
ROPE_THETA = 10000.0
MLA_OUT = MLA_HEADS * V_HEAD
GM_GROUPS = 8
GM_CH = 64
GM_OUT = GM_GROUPS * GM_CH
CHUNK = 128
N_GROUPS = 4
EXP_PER_GROUP = 8
N_EXPERTS = N_GROUPS * EXP_PER_GROUP
TOP_K = 2
EPS = 1e-6
DEPTH = 1
ALPHA = (2.0 * DEPTH) ** 0.25
SM_SCALE = (QK_NOPE + QK_ROPE) ** -0.5
LOG2E = 1.4426950408889634
MASK_VALUE = -1e30

LANES = 128
SUBLANES = 8
TOKEN_ROWS = 8
ONES_ROWS = 16
VMEM_LIMIT = 56 * 1024 * 1024

PREP_ROWS = 512
ATTN_ROWS = 256
ATTN_TILES = 4
ROUTE_ROWS = 512
MOVE_ROWS = 256
MOVE_UNROLL = 8
GATHER_AHEAD = 3
EXPERT_ROWS = 256

C_Q = 0
C_KV = C_Q + Q_RANK
C_KR = C_KV + KV_RANK
C_U = C_KR + LANES
C_V = C_U + GM_OUT
C_END = C_V + GM_OUT
HP = MLA_HEADS * LANES

I_E0, I_E1, I_R0, I_R1, I_G0, I_G1 = range(6)
ROUTE_OUT = 48


def _rms(v, g):
    return v * lax.rsqrt(jnp.mean(v * v, axis=-1, keepdims=True) + EPS) * g


def _ln(v, g, b):
    mu = jnp.mean(v, axis=-1, keepdims=True)
    d = v - mu
    var = jnp.mean(d * d, axis=-1, keepdims=True)
    return d * lax.rsqrt(var + EPS) * g + b


def _dot(a, b):
    return jnp.dot(a, b, preferred_element_type=F32)


def _prep_kernel(x_ref, pos_ref, win_ref, qg_ref, wq_ref, kvg_ref, wk_ref, wv_ref, inv_ref, rope_ref, one_ref,
                 lng_ref, lnb_ref, gavg_ref, ws_ref, bias_ref, gog_ref,
                 q_ref, k_ref, vt_ref, g_ref):
    rows = x_ref.shape[1]
    h = _dot(x_ref[0].astype(BF16), win_ref[...])

    ang = inv_ref[...] * pos_ref[0, 0].astype(F32)
    parts = []
    for t in (jnp.cos(ang), jnp.sin(ang)):
        hi = t.astype(BF16).astype(F32)
        parts += [hi, t - hi]
    tabs = _dot(jnp.concatenate(parts, axis=0).T.astype(BF16), rope_ref[...])
    cos_t = tabs[:, :LANES] + one_ref[...]
    sin_a = tabs[:, LANES:2 * LANES]
    sin_b = tabs[:, 2 * LANES:]
    half = QK_ROPE // 2

    def rotate(v):
        return v * cos_t + pltpu.roll(v, LANES - half, 1) * sin_a + pltpu.roll(v, half, 1) * sin_b

    cq = _rms(h[:, C_Q:C_Q + Q_RANK], qg_ref[...]).astype(BF16)
    q2 = _dot(cq, wq_ref[...])
    for hd in range(MLA_HEADS):
        lo = hd * LANES
        q_ref[0, :, lo:lo + LANES] = (rotate(q2[:, lo:lo + LANES]) * (SM_SCALE * LOG2E)).astype(BF16)

    ckv = _rms(h[:, C_KV:C_KV + KV_RANK], kvg_ref[...]).astype(BF16)
    kp = _dot(ckv, wk_ref[...])
    kr = rotate(h[:, C_KR:C_KR + LANES])
    for hd in range(MLA_HEADS):
        lo = hd * LANES
        k_ref[0, :, lo:lo + LANES] = (kp[:, lo:lo + LANES] + kr).astype(BF16)
    vp = _dot(ckv, wv_ref[...])
    for kb in range(rows // ATTN_ROWS):
        vt_ref[0, kb] = vp[kb * ATTN_ROWS:(kb + 1) * ATTN_ROWS].T.astype(BF16)

    u = jax.nn.gelu(h[:, C_U:C_U + GM_OUT])
    vv = jax.nn.gelu(h[:, C_V:C_V + GM_OUT])
    mu = _dot(vv.astype(BF16), gavg_ref[...])
    d = vv - mu
    var = _dot((d * d).astype(BF16), gavg_ref[...])
    vn = (d * lax.rsqrt(var + EPS) * lng_ref[...] + lnb_ref[...]).astype(BF16)

    tri = lax.broadcasted_iota(jnp.int32, (CHUNK, CHUNK), 0) >= lax.broadcasted_iota(jnp.int32, (CHUNK, CHUNK), 1)
    wm = [jnp.where(tri, ws_ref[g], 0.0).astype(BF16) for g in range(GM_GROUPS)]
    low_half = lax.broadcasted_iota(jnp.int32, (CHUNK, LANES), 1) < GM_CH
    for c in range(rows // CHUNK):
        r0 = c * CHUNK
        parts = []
        for pr in range(GM_GROUPS // 2):
            tile = vn[r0:r0 + CHUNK, pr * LANES:(pr + 1) * LANES]
            parts.append(jnp.where(low_half, _dot(wm[2 * pr], tile), _dot(wm[2 * pr + 1], tile)))
        sg = jnp.concatenate(parts, axis=1) + bias_ref[...]
        gm = u[r0:r0 + CHUNK] * sg
        g_ref[0, r0:r0 + CHUNK, :] = _rms(gm, gog_ref[...]).astype(BF16)


def _prep(x, pos4, w):
    B, S, D = x.shape
    ts = PREP_ROWS
    full = lambda a: pl.BlockSpec(a.shape, lambda b, i: (0,) * a.ndim)
    consts = [w["win"], w["qg"], w["wq"], w["kvg"], w["wk"], w["wv"], w["inv"], w["rope"], w["one"],
              w["lng"], w["lnb"], w["gavg"], w["ws"], w["bias"], w["gog"]]
    return pl.pallas_call(
        _prep_kernel,
        grid=(B, S // ts),
        in_specs=[pl.BlockSpec((1, ts, D), lambda b, i: (b, i, 0)),
                  pl.BlockSpec((1, 1, 1, ts), lambda b, i: (b, i, 0, 0))] + [full(a) for a in consts],
        out_specs=[pl.BlockSpec((1, ts, HP), lambda b, i: (b, i, 0)),
                   pl.BlockSpec((1, ts, HP), lambda b, i: (b, i, 0)),
                   pl.BlockSpec((1, ts // ATTN_ROWS, MLA_OUT, ATTN_ROWS), lambda b, i: (b, i, 0, 0)),
                   pl.BlockSpec((1, ts, GM_OUT), lambda b, i: (b, i, 0))],
        out_shape=[jax.ShapeDtypeStruct((B, S, HP), BF16)] * 2
        + [jax.ShapeDtypeStruct((B, S // ATTN_ROWS, MLA_OUT, ATTN_ROWS), BF16),
           jax.ShapeDtypeStruct((B, S, GM_OUT), BF16)],
        compiler_params=pltpu.CompilerParams(dimension_semantics=("parallel", "parallel"),
                                             vmem_limit_bytes=VMEM_LIMIT),
        name="prep",
    )(x, pos4, *consts)


def _attn_kernel(q_ref, k_ref, vt_ref, g_ref, x_ref, woa_ref, wog_ref, mog_ref, l1g_ref, l1b_ref,
                 o_ref, m_scr, acc_scr, sa_scr, sb_scr):
    pid = pl.program_id(1)
    tq = ATTN_ROWS
    tk = tq
    key = lax.broadcasted_iota(jnp.int32, (tk, tq), 0)
    qry = lax.broadcasted_iota(jnp.int32, (tk, tq), 1)
    diag_mask = key <= qry
    ones = jnp.ones((ONES_ROWS, tk), BF16)

    def tile(t):
        r0 = t * tq
        i = ATTN_TILES * pid + t
        odd = t % 2 == 1
        m_scr[...] = jnp.full(m_scr.shape, MASK_VALUE, F32)
        acc_scr[...] = jnp.zeros(acc_scr.shape, F32)

        def scores(j, s_scr):
            k0 = pl.multiple_of(j * tk, tk)
            for hd in range(MLA_HEADS):
                lo = hd * LANES
                qh = q_ref[0, r0:r0 + tq, lo:lo + LANES]
                kj = k_ref[0, pl.ds(k0, tk), lo:lo + LANES]
                s_scr[hd] = lax.dot_general(kj, qh, (((1,), (1,)), ((), ())), preferred_element_type=F32)

        def update(j, s_scr, masked):
            for hd in range(MLA_HEADS):
                s = s_scr[hd]
                vt = vt_ref[0, j, hd * V_HEAD:(hd + 1) * V_HEAD, :]
                if masked:
                    s = jnp.where(diag_mask, s, MASK_VALUE)
                m_prev = m_scr[hd]
                m_new = jnp.maximum(m_prev, jnp.max(s, axis=0, keepdims=True))
                p = jnp.exp2(s - m_new).astype(BF16)
                scale = jnp.exp2(m_prev - m_new)
                acc_scr[hd] = scale * acc_scr[hd] + _dot(jnp.concatenate([vt, ones], axis=0), p)
                m_scr[hd] = m_new

        def pair(jj, c):
            j = 2 * jj
            scores(j + 1, sb_scr)
            update(j, sa_scr, False)
            scores(j + 2, sa_scr)
            update(j + 1, sb_scr, False)
            return c

        scores(0, sa_scr)
        lax.fori_loop(0, (ATTN_TILES // 2) * pid + t // 2, pair, 0)
        if odd:
            scores(i, sb_scr)
            update(i - 1, sa_scr, False)
            update(i, sb_scr, True)
        else:
            update(i, sa_scr, True)

        at = jnp.concatenate([acc_scr[hd, :V_HEAD] / acc_scr[hd, V_HEAD:V_HEAD + 1] for hd in range(MLA_HEADS)],
                             axis=0)
        at = at * lax.rsqrt(jnp.mean(at * at, axis=0, keepdims=True) + EPS) * mog_ref[...]
        mix = _dot(at.T.astype(BF16), woa_ref[...]) + _dot(g_ref[0, r0:r0 + tq, :], wog_ref[...])
        o_ref[0, r0:r0 + tq, :] = _ln(ALPHA * x_ref[0, r0:r0 + tq, :] + mix, l1g_ref[...], l1b_ref[...])

    for t in range(ATTN_TILES):
        tile(t)


def _attn(q, k, vt, g, x, w):
    B, S, D = x.shape
    tq = ATTN_ROWS
    rows = ATTN_TILES * tq
    full = lambda a: pl.BlockSpec(a.shape, lambda b, i: (0,) * a.ndim)
    consts = [w["woa"], w["wog"], w["mog"], w["l1g"], w["l1b"]]
    return pl.pallas_call(
        _attn_kernel,
        grid=(B, S // rows),
        in_specs=[pl.BlockSpec((1, rows, HP), lambda b, i: (b, i, 0)),
                  pl.BlockSpec((1, S, HP), lambda b, i: (b, 0, 0)),
                  pl.BlockSpec((1,) + vt.shape[1:], lambda b, i: (b, 0, 0, 0)),
                  pl.BlockSpec((1, rows, GM_OUT), lambda b, i: (b, i, 0)),
                  pl.BlockSpec((1, rows, D), lambda b, i: (b, i, 0))] + [full(a) for a in consts],
        out_specs=pl.BlockSpec((1, rows, D), lambda b, i: (b, i, 0)),
        out_shape=jax.ShapeDtypeStruct((B, S, D), F32),
        scratch_shapes=[pltpu.VMEM((MLA_HEADS, 1, tq), F32),
                        pltpu.VMEM((MLA_HEADS, V_HEAD + ONES_ROWS, tq), F32),
                        pltpu.VMEM((MLA_HEADS, tq, tq), F32), pltpu.VMEM((MLA_HEADS, tq, tq), F32)],
        compiler_params=pltpu.CompilerParams(dimension_semantics=("parallel", "parallel"),
                                             vmem_limit_bytes=VMEM_LIMIT),
        name="attn",
    )(q, k, vt, g, x, *consts)


def _route_kernel(x_ref, wr_ref, br_ref, info_ref, infot_ref, cnt_ref, carry_scr, tri_scr):
    step = pl.program_id(0)
    tt = x_ref.shape[0]

    @pl.when(step == 0)
    def _():
        carry_scr[...] = jnp.zeros_like(carry_scr)
        s = lax.broadcasted_iota(jnp.int32, (tt, tt), 0)
        t = lax.broadcasted_iota(jnp.int32, (tt, tt), 1)
        tri_scr[...] = jnp.where(s < t, 1.0, 0.0).astype(BF16)

    x = x_ref[...]
    xh = x.astype(BF16)
    xl = (x - xh.astype(F32)).astype(BF16)
    wr = wr_ref[...]
    wh = wr.astype(BF16)
    wl = (wr - wh.astype(F32)).astype(BF16)
    nt = (((1,), (1,)), ((), ()))
    logits = (lax.dot_general(wh, xh, nt, preferred_element_type=F32)
              + lax.dot_general(wh, xl, nt, preferred_element_type=F32)
              + lax.dot_general(wl, xh, nt, preferred_element_type=F32)) + br_ref[...]
    neg = jnp.float32(-jnp.inf)

    lg = logits[N_EXPERTS:N_EXPERTS + SUBLANES]
    grow = lax.broadcasted_iota(jnp.int32, lg.shape, 0)
    lg = jnp.where(grow < N_GROUPS, lg, neg)
    gmax = jnp.max(lg, axis=0, keepdims=True)
    g_idx = jnp.min(jnp.where(lg == gmax, grow, SUBLANES), axis=0, keepdims=True)
    g_p = 1.0 / jnp.sum(jnp.exp(lg - gmax), axis=0, keepdims=True)

    le = logits[:N_EXPERTS]
    row = lax.broadcasted_iota(jnp.int32, le.shape, 0)
    le = jnp.where((row >> 3) == g_idx, le, neg)
    m1 = jnp.max(le, axis=0, keepdims=True)
    i1 = jnp.min(jnp.where(le == m1, row, N_EXPERTS), axis=0, keepdims=True)
    le2 = jnp.where(row == i1, neg, le)
    m2 = jnp.max(le2, axis=0, keepdims=True)
    i2 = jnp.min(jnp.where(le2 == m2, row, N_EXPERTS), axis=0, keepdims=True)
    e2 = jnp.exp(m2 - m1)
    gate0 = g_p / (1.0 + e2)
    gate1 = g_p * e2 / (1.0 + e2)

    hit1 = row == i1
    hit2 = row == i2
    onehot = jnp.where(hit1 | hit2, 1.0, 0.0)
    before = _dot(onehot.astype(BF16), tri_scr[...]) + carry_scr[...]
    rank0 = jnp.sum(jnp.where(hit1, before, 0.0), axis=0, keepdims=True)
    rank1 = jnp.sum(jnp.where(hit2, before, 0.0), axis=0, keepdims=True)
    carry_scr[...] = carry_scr[...] + jnp.sum(onehot, axis=1, keepdims=True)
    cnt_ref[...] = jnp.broadcast_to(carry_scr[...], cnt_ref.shape)

    fields = jnp.concatenate([i1.astype(F32), i2.astype(F32), rank0, rank1, gate0, gate1,
                              jnp.zeros((SUBLANES - 6, tt), F32)], axis=0)
    infot_ref[0] = fields
    info_ref[...] = jnp.concatenate([fields, jnp.zeros((LANES - SUBLANES, tt), F32)], axis=0).T


def _route(x1, wr, br):
    T, D = x1.shape
    tt = ROUTE_ROWS
    return pl.pallas_call(
        _route_kernel,
        grid=(T // tt,),
        in_specs=[pl.BlockSpec((tt, D), lambda i: (i, 0)),
                  pl.BlockSpec(wr.shape, lambda i: (0, 0)),
                  pl.BlockSpec(br.shape, lambda i: (0, 0))],
        out_specs=[pl.BlockSpec((tt, LANES), lambda i: (i, 0)),
                   pl.BlockSpec((1, SUBLANES, tt), lambda i: (i, 0, 0)),
                   pl.BlockSpec((N_EXPERTS, LANES), lambda i: (0, 0))],
        out_shape=[jax.ShapeDtypeStruct((T, LANES), F32), jax.ShapeDtypeStruct((T // tt, SUBLANES, tt), F32),
                   jax.ShapeDtypeStruct((N_EXPERTS, LANES), F32)],
        scratch_shapes=[pltpu.VMEM((N_EXPERTS, 1), F32), pltpu.VMEM((tt, tt), BF16)],
        compiler_params=pltpu.CompilerParams(dimension_semantics=("arbitrary",), vmem_limit_bytes=VMEM_LIMIT),
        name="route",
    )(x1, wr, br)


def _to_token_tiles(dst_ref, val):
    dst_ref[...] = val.astype(BF16).reshape(dst_ref.shape)


def _from_token_tiles(src_ref, rows):
    return src_ref[...].reshape(rows, TOKEN_ROWS * LANES)


def _to_token_tiles_f32(dst_ref, val):
    rows = val.shape[0]
    for c in range(TOKEN_ROWS):
        dst_ref[pl.ds(c, rows, stride=TOKEN_ROWS), :] = val[:, c * LANES:(c + 1) * LANES]


def _from_token_tiles_f32(src_ref, rows):
    return jnp.concatenate([src_ref[pl.ds(c, rows, stride=TOKEN_ROWS), :] for c in range(TOKEN_ROWS)], axis=1)


def _tile_copy(src_ref, src_row, dst_ref, dst_row, sem):
    return pltpu.make_async_copy(src_ref.at[pl.ds(pl.multiple_of(src_row, TOKEN_ROWS), TOKEN_ROWS)],
                                 dst_ref.at[pl.ds(pl.multiple_of(dst_row, TOKEN_ROWS), TOKEN_ROWS)], sem)


def _dispatch_kernel(seg_ref, dest_ref, x0_ref, xn_ref, buf_ref, stage_scr, zero_scr, sem, zero_sem, *, n_steps):
    i = pl.program_id(0)
    rows = xn_ref.shape[0]
    cur = i % 3
    nxt = (i + 1) % 3

    @pl.when(i == 0)
    def _():
        zero_scr[...] = jnp.zeros(zero_scr.shape, BF16)

        block = EXPERT_ROWS * TOKEN_ROWS
        n_blocks = buf_ref.shape[0] // block

        def clear_rows(first):
            return pltpu.make_async_copy(zero_scr, buf_ref.at[pl.ds(pl.multiple_of(first, SUBLANES), block)], zero_sem)

        def clear(e):
            return clear_rows((seg_ref[0, e] - EXPERT_ROWS) * TOKEN_ROWS)

        def start_tail(b, c):
            clear_rows(b * block).start()
            return c

        def wait_tail(b, c):
            clear_rows(b * block).wait()
            return c

        for e in range(N_EXPERTS):
            pl.when(seg_ref[1, e] > 0)(lambda e=e: clear(e).start())
        lax.fori_loop(seg_ref[2, 0], n_blocks, start_tail, 0)
        for e in range(N_EXPERTS):
            pl.when(seg_ref[1, e] > 0)(lambda e=e: clear(e).wait())
        lax.fori_loop(seg_ref[2, 0], n_blocks, wait_tail, 0)

        _to_token_tiles(stage_scr.at[0], x0_ref[...])

    def drain(s):
        for _ in range(TOP_K):
            pltpu.make_async_copy(stage_scr.at[s], stage_scr.at[s], sem.at[s]).wait()

    @pl.when(i >= 2)
    def _():
        drain(nxt)

    _to_token_tiles(stage_scr.at[nxt], xn_ref[...])
    for r in range(rows):
        for kk in range(TOP_K):
            _tile_copy(stage_scr.at[cur], r * TOKEN_ROWS, buf_ref, dest_ref[0, 0, kk * rows + r],
                       sem.at[cur]).start(priority=kk)

    @pl.when(i == n_steps - 1)
    def _():
        drain(cur)
        if n_steps >= 2:
            drain((i + 2) % 3)


def _dispatch(seg, dest3, x1, n_rows):
    T, D = x1.shape
    td = MOVE_ROWS
    n_steps = T // td
    grid_spec = pltpu.PrefetchScalarGridSpec(
        num_scalar_prefetch=1,
        grid=(n_steps,),
        in_specs=[pl.BlockSpec((1, 1, TOP_K * td), lambda i, seg: (i, 0, 0), memory_space=pltpu.SMEM),
                  pl.BlockSpec((td, D), lambda i, seg: (0, 0)),
                  pl.BlockSpec((td, D), lambda i, seg: (jnp.minimum(i + 1, n_steps - 1), 0))],
        out_specs=pl.BlockSpec(memory_space=pl.ANY),
        scratch_shapes=[pltpu.VMEM((3, td * TOKEN_ROWS, LANES), BF16),
                        pltpu.VMEM((EXPERT_ROWS * TOKEN_ROWS, LANES), BF16),
                        pltpu.SemaphoreType.DMA((3,)), pltpu.SemaphoreType.DMA(())],
    )
    return pl.pallas_call(
        functools.partial(_dispatch_kernel, n_steps=n_steps),
        grid_spec=grid_spec,
        out_shape=jax.ShapeDtypeStruct((n_rows * TOKEN_ROWS, LANES), BF16),
        compiler_params=pltpu.CompilerParams(dimension_semantics=("arbitrary",), vmem_limit_bytes=VMEM_LIMIT),
        name="dispatch",
    )(seg, dest3, x1, x1)


def _expert_kernel(be_ref, ne_ref, nu_ref, buf0_ref, bufa_ref, bufb_ref, wg_hbm, wu_hbm, wd_hbm, y_ref,
                   sg_scr, su_scr, sd_scr, wg_scr, wu_scr, wd_scr, xa_scr, xb_scr, cur_ref, sem):
    step = pl.program_id(0)
    bm = EXPERT_ROWS
    half = bm * TOKEN_ROWS

    def fetch(expert, s):
        return (pltpu.make_async_copy(wg_hbm.at[expert], sg_scr.at[s], sem.at[s, 0]),
                pltpu.make_async_copy(wu_hbm.at[expert], su_scr.at[s], sem.at[s, 1]),
                pltpu.make_async_copy(wd_hbm.at[expert], sd_scr.at[s], sem.at[s, 2]))

    @pl.when(step == 0)
    def _():
        cur_ref[0] = 0
        for c in fetch(be_ref[0], 0):
            c.start()
        xa_scr[...] = _from_token_tiles(buf0_ref, bm)

    def load_weights(blk):
        e = be_ref[blk]

        @pl.when((blk == 0) | (be_ref[jnp.maximum(blk - 1, 0)] != e))
        def _():
            s = cur_ref[0]
            for c in fetch(e, s):
                c.wait()
            wg_scr[...] = sg_scr[s].astype(BF16)
            wu_scr[...] = su_scr[s].astype(BF16)
            wd_scr[...] = sd_scr[s].astype(BF16)
            nxt = ne_ref[blk]

            @pl.when(nxt >= 0)
            def _():
                for c in fetch(nxt, 1 - s):
                    c.start()

            cur_ref[0] = 1 - s

    def compute(x_scr, nxt_ref, nxt_scr, out_rows):
        nxt_scr[...] = _from_token_tiles(nxt_ref, bm)
        xb = x_scr[...]
        hidden = jax.nn.silu(_dot(xb, wg_scr[...])) * _dot(xb, wu_scr[...])
        _to_token_tiles_f32(y_ref.at[out_rows], _dot(hidden.astype(BF16), wd_scr[...]))

    def run(blk, x_scr, nxt_ref, nxt_scr, out_rows):
        @pl.when(blk < nu_ref[0])
        def _():
            compute(x_scr, nxt_ref, nxt_scr, out_rows)

        @pl.when(blk >= nu_ref[0])
        def _():
            y_ref[out_rows, :] = jnp.zeros((half, LANES), F32)

    blk_a, blk_b = 2 * step, 2 * step + 1
    rows_a, rows_b = pl.ds(0, half), pl.ds(half, half)
    load_weights(blk_a)
    same = (be_ref[blk_a] == be_ref[blk_b]) & (blk_b < nu_ref[0])

    @pl.when(same)
    def _():
        compute(xa_scr, bufa_ref, xb_scr, rows_a)
        compute(xb_scr, bufb_ref, xa_scr, rows_b)

    @pl.when(jnp.logical_not(same))
    def _():
        run(blk_a, xa_scr, bufa_ref, xb_scr, rows_a)
        load_weights(blk_b)
        run(blk_b, xb_scr, bufb_ref, xa_scr, rows_b)


def _experts(block_expert, next_expert, n_used, buf, w_gate, w_up, w_down):
    bm = EXPERT_ROWS
    D, ff = w_gate.shape[1:]
    n_blocks = buf.shape[0] // (bm * TOKEN_ROWS)
    assert n_blocks % 2 == 0
    last = n_blocks - 1
    grid_spec = pltpu.PrefetchScalarGridSpec(
        num_scalar_prefetch=3,
        grid=(n_blocks // 2,),
        in_specs=[pl.BlockSpec((bm * TOKEN_ROWS, LANES), lambda s, *_: (0, 0)),
                  pl.BlockSpec((bm * TOKEN_ROWS, LANES), lambda s, *_: (2 * s + 1, 0)),
                  pl.BlockSpec((bm * TOKEN_ROWS, LANES), lambda s, *_: (jnp.minimum(2 * s + 2, last), 0)),
                  pl.BlockSpec(memory_space=pl.ANY),
                  pl.BlockSpec(memory_space=pl.ANY),
                  pl.BlockSpec(memory_space=pl.ANY)],
        out_specs=pl.BlockSpec((2 * bm * TOKEN_ROWS, LANES), lambda s, *_: (s, 0)),
        scratch_shapes=[pltpu.VMEM((2, D, ff), F32), pltpu.VMEM((2, D, ff), F32), pltpu.VMEM((2, ff, D), F32),
                        pltpu.VMEM((D, ff), BF16), pltpu.VMEM((D, ff), BF16), pltpu.VMEM((ff, D), BF16),
                        pltpu.VMEM((bm, D), BF16), pltpu.VMEM((bm, D), BF16),
                        pltpu.SMEM((1,), jnp.int32), pltpu.SemaphoreType.DMA((2, 3))],
    )
    return pl.pallas_call(
        _expert_kernel,
        grid_spec=grid_spec,
        out_shape=jax.ShapeDtypeStruct(buf.shape, F32),
        compiler_params=pltpu.CompilerParams(dimension_semantics=("arbitrary",), vmem_limit_bytes=VMEM_LIMIT),
        name="experts",
    )(block_expert, next_expert, n_used, buf, buf, buf, w_gate, w_up, w_down)


def _final_kernel(*refs):
    dest_refs = refs[:GATHER_AHEAD + 1]
    (x_ref, info_ref, y_ref, p_ref, wpg_ref, bpg_ref, wpp_ref, l2g_ref, l2b_ref, l3g_ref, l3b_ref,
     o_ref, rows_scr, sem) = refs[GATHER_AHEAD + 1:]
    i = pl.program_id(0)
    last = pl.num_programs(0) - 1
    rows = x_ref.shape[0]
    ring = GATHER_AHEAD + 1
    slot = i % ring
    ahead = (i + GATHER_AHEAD) % ring

    def row_copy(dref, s, r, kk):
        return _tile_copy(y_ref, dref[0, 0, kk * rows + r], rows_scr.at[s, kk], r * TOKEN_ROWS, sem.at[s])

    def landed(s):
        pltpu.make_async_copy(rows_scr.at[s], rows_scr.at[s], sem.at[s]).wait()

    @pl.when(i == 0)
    def _():
        def start(c, carry):
            for u in range(MOVE_UNROLL):
                for kk in range(TOP_K):
                    for j in range(GATHER_AHEAD):
                        row_copy(dest_refs[j], j, c * MOVE_UNROLL + u, kk).start(priority=kk)
            return carry

        lax.fori_loop(0, rows // MOVE_UNROLL, start, 0)

    landed(slot)
    info = info_ref[...]
    gate0 = info[:, I_G0:I_G0 + 1]
    gate1 = info[:, I_G1:I_G1 + 1]
    moe = (_from_token_tiles_f32(rows_scr.at[slot, 0], rows) * gate0
           + _from_token_tiles_f32(rows_scr.at[slot, 1], rows) * gate1)

    for r in range(rows):
        for kk in range(TOP_K):
            row_copy(dest_refs[GATHER_AHEAD], ahead, r, kk).start(priority=kk)

    pp = _dot(p_ref[...].astype(BF16), wpp_ref[...])
    x2 = _ln(ALPHA * x_ref[...] + moe, l2g_ref[...], l2b_ref[...])
    gate = jax.nn.sigmoid(_dot(x2.astype(BF16), wpg_ref[...]) + bpg_ref[...])
    o_ref[...] = _ln(ALPHA * x2 + gate * pp, l3g_ref[...], l3b_ref[...])

    @pl.when(i == last)
    def _():
        for j in range(1, ring):
            landed((i + j) % ring)


def _final(dest3, x1, info, y, p2, w):
    T, D = x1.shape
    tc = MOVE_ROWS
    pd = p2.shape[1]
    full = lambda a: pl.BlockSpec(a.shape, lambda i: (0,) * a.ndim)
    consts = [w["wpg"], w["bpg"], w["wpp"], w["l2g"], w["l2b"], w["l3g"], w["l3b"]]
    last = T // tc - 1
    ring = GATHER_AHEAD + 1
    assert last >= GATHER_AHEAD
    dest_specs = [pl.BlockSpec((1, 1, TOP_K * tc), lambda i, j=j: (jnp.minimum(i + j, last), 0, 0),
                               memory_space=pltpu.SMEM) for j in range(ring)]
    return pl.pallas_call(
        _final_kernel,
        grid=(T // tc,),
        in_specs=dest_specs + [
                  pl.BlockSpec((tc, D), lambda i: (i, 0)),
                  pl.BlockSpec((tc, LANES), lambda i: (i, 0)),
                  pl.BlockSpec(memory_space=pl.ANY),
                  pl.BlockSpec((tc, pd), lambda i: (i, 0))] + [full(a) for a in consts],
        out_specs=pl.BlockSpec((tc, D), lambda i: (i, 0)),
        out_shape=jax.ShapeDtypeStruct((T, D), F32),
        scratch_shapes=[pltpu.VMEM((ring, TOP_K, tc * TOKEN_ROWS, LANES), F32), pltpu.SemaphoreType.DMA((ring,))],
        compiler_params=pltpu.CompilerParams(dimension_semantics=("arbitrary",), vmem_limit_bytes=VMEM_LIMIT),
        name="final",
    )(*([dest3] * ring), x1, info, y, p2, *consts)


def _pad_heads(a, width):
    lead = a.shape[:-1]
    a = a.reshape(lead + (MLA_HEADS, width))
    a = jnp.pad(a, [(0, 0)] * len(lead) + [(0, 0), (0, LANES - width)])
    return a.reshape(lead + (HP,))


def _layer_weights(w_in, q_norm_g, w_q_up, kv_norm_g, w_kv_up, gm_ln_g, gm_ln_b, gm_w_s, gm_b_s,
                   mla_out_g, gm_out_g, w_o, ln1_g, ln1_b):
    D = w_in.shape[0]
    half = QK_ROPE // 2
    c1, c2, c3 = Q_RANK, Q_RANK + KV_RANK, Q_RANK + KV_RANK + QK_ROPE
    zeros = lambda *s: jnp.zeros(s, F32)
    kr = jnp.concatenate([zeros(D, QK_NOPE), w_in[:, c2:c3], zeros(D, LANES - QK_NOPE - QK_ROPE)], axis=1)
    win = jnp.concatenate([w_in[:, :c2], kr, w_in[:, c3:]], axis=1).astype(BF16)
    wq = _pad_heads(w_q_up, QK_NOPE + QK_ROPE).astype(BF16)

    wkv3 = w_kv_up.reshape(KV_RANK, MLA_HEADS, QK_NOPE + V_HEAD)
    wk = _pad_heads(wkv3[..., :QK_NOPE].reshape(KV_RANK, -1), QK_NOPE).astype(BF16)
    wv = wkv3[..., QK_NOPE:].reshape(KV_RANK, -1).astype(BF16)

    inv = (ROPE_THETA ** (-jnp.arange(0, QK_ROPE, 2, dtype=F32) / QK_ROPE))[:, None]
    eye = jnp.eye(half, dtype=F32)
    first = jnp.pad(eye, ((0, 0), (QK_NOPE, LANES - QK_NOPE - half)))
    second = jnp.pad(eye, ((0, 0), (QK_NOPE + half, LANES - QK_NOPE - QK_ROPE)))
    zero = jnp.zeros_like(first)
    cos_rows = jnp.concatenate([first + second, zero, zero], axis=1)
    sin_rows = jnp.concatenate([zero, -first, second], axis=1)
    rope = jnp.concatenate([cos_rows, cos_rows, sin_rows, sin_rows], axis=0).astype(BF16)
    lane = jnp.arange(LANES)
    one = jnp.where((lane >= QK_NOPE) & (lane < QK_NOPE + QK_ROPE), 0.0, 1.0)[None, :]

    grp = jnp.arange(GM_OUT) // GM_CH
    gavg = jnp.where(grp[:, None] == grp[None, :], 1.0 / GM_CH, 0.0).astype(BF16)
    bias = jnp.repeat(gm_b_s.T, GM_CH, axis=1)

    woa = w_o[:MLA_OUT].astype(BF16)
    wog = w_o[MLA_OUT:].astype(BF16)
    return dict(win=win, qg=q_norm_g[None, :], wq=wq, kvg=kv_norm_g[None, :], wk=wk, wv=wv, inv=inv, rope=rope, one=one,
                lng=gm_ln_g[None, :], lnb=gm_ln_b[None, :], gavg=gavg, ws=gm_w_s, bias=bias, gog=gm_out_g[None, :],
                woa=woa, wog=wog, mog=mla_out_g[:, None], l1g=ln1_g[None, :], l1b=ln1_b[None, :])


def _moe(x1, w_rg, b_rg, w_re, b_re, w_gate, w_up, w_down):
    T, D = x1.shape
    pad = ROUTE_OUT - N_EXPERTS - N_GROUPS
    wr = jnp.concatenate([w_re.T, w_rg.T, jnp.zeros((pad, D), F32)], axis=0)
    br = jnp.concatenate([b_re, b_rg, jnp.zeros((pad,), F32)])[:, None]
    info, info_t, cnt = _route(x1, wr, br)

    bm = EXPERT_ROWS
    n_blocks = (T * TOP_K) // bm + N_EXPERTS
    counts = cnt[:, 0].astype(jnp.int32)
    padded = (counts + bm - 1) // bm * bm
    pad_ends = jnp.cumsum(padded)
    pad_starts = pad_ends - padded
    def dest_rows(e_lane, r_lane):
        e = info_t[:, e_lane, :].astype(jnp.int32)
        ids = jnp.arange(N_EXPERTS)[:, None, None]
        seg_start = jnp.sum(jnp.where(e[None] == ids, pad_starts[:, None, None], 0), axis=0)
        return ((seg_start + info_t[:, r_lane, :].astype(jnp.int32)) * TOKEN_ROWS).reshape(T // MOVE_ROWS, MOVE_ROWS)

    dest = jnp.concatenate([dest_rows(I_E0, I_R0), dest_rows(I_E1, I_R1)], axis=1)[:, None, :]
    block_start = jnp.arange(n_blocks, dtype=jnp.int32) * bm
    block_expert = jnp.minimum(jnp.sum(pad_ends[None, :] <= block_start[:, None], axis=1),
                               N_EXPERTS - 1).astype(jnp.int32)

    blk = jnp.arange(n_blocks)
    later = (blk[None, :] > blk[:, None]) & (block_expert[None, :] != block_expert[:, None])
    next_expert = jnp.min(jnp.where(later, block_expert[None, :], N_EXPERTS), axis=1)
    next_expert = jnp.where(next_expert == N_EXPERTS, -1, next_expert).astype(jnp.int32)
    n_used = (pad_ends[-1:] // bm).astype(jnp.int32)

    seg = jnp.stack([pad_ends, padded, jnp.broadcast_to(n_used, (N_EXPERTS,))]).astype(jnp.int32)
    buf = _dispatch(seg, dest, x1, n_blocks * bm)
    y = _experts(block_expert, next_expert, n_used, buf, w_gate, w_up, w_down)
    return info, dest, y


def kernel(x, p, positions, w_in, q_norm_g, w_q_up, kv_norm_g, w_kv_up, gm_ln_g, gm_ln_b, gm_w_s, gm_b_s, mla_out_g, gm_out_g, w_o, ln1_g, ln1_b, w_rg, b_rg, w_re, b_re, w_gate, w_up, w_down, ln2_g, ln2_b, w_pg, b_pg, w_pp, ln3_g, ln3_b):
    B, S, D = x.shape
    T = B * S
    assert S % PREP_ROWS == 0 and PREP_ROWS % ATTN_ROWS == 0 and PREP_ROWS % CHUNK == 0
    assert S % (ATTN_TILES * ATTN_ROWS) == 0 and ATTN_TILES % 2 == 0
    assert T % ROUTE_ROWS == 0 and T % MOVE_ROWS == 0 and (T * TOP_K) % EXPERT_ROWS == 0
    assert D == TOKEN_ROWS * LANES and MOVE_ROWS % MOVE_UNROLL == 0
    pos4 = positions.reshape(B, S // PREP_ROWS, 1, PREP_ROWS)
    for i in range(DEPTH):
        w = _layer_weights(w_in[i], q_norm_g[i], w_q_up[i], kv_norm_g[i], w_kv_up[i], gm_ln_g[i], gm_ln_b[i],
                           gm_w_s[i], gm_b_s[i], mla_out_g[i], gm_out_g[i], w_o[i], ln1_g[i], ln1_b[i])
        q, k, vt, g = _prep(x, pos4, w)
        x1 = _attn(q, k, vt, g, x, w).reshape(T, D)
        info, dest, y = _moe(x1, w_rg[i], b_rg[i], w_re[i], b_re[i], w_gate[i], w_up[i], w_down[i])
        wf = dict(wpg=w_pg[i].astype(BF16), bpg=b_pg[i][None, :], wpp=w_pp[i].astype(BF16),
                  l2g=ln2_g[i][None, :], l2b=ln2_b[i][None, :], l3g=ln3_g[i][None, :], l3b=ln3_b[i][None, :])
        x = _final(dest, x1, info, y, p[i].reshape(T, -1), wf).reshape(B, S, D)
    return x
```

```python
import functools

import jax
import jax.numpy as jnp
from jax import lax
from jax.experimental import pallas as pl
from jax.experimental.pallas import tpu as pltpu

F32 = jnp.float32
BF16 = jnp.bfloat16

MLA_HEADS = 8
QK_NOPE = 64
QK_ROPE = 32
V_HEAD = 64
Q_RANK = 256
KV_RANK = 128
ROPE_THETA = 10000.0
MLA_OUT = MLA_HEADS * V_HEAD
GM_GROUPS = 8
GM_CH = 64
GM_OUT = GM_GROUPS * GM_CH
CHUNK = 128
N_GROUPS = 4
EXP_PER_GROUP = 8
N_EXPERTS = N_GROUPS * EXP_PER_GROUP
TOP_K = 2
EPS = 1e-6
DEPTH = 1
ALPHA = (2.0 * DEPTH) ** 0.25
SM_SCALE = (QK_NOPE + QK_ROPE) ** -0.5
LOG2E = 1.4426950408889634
MASK_VALUE = -1e30

LANES = 128
SUBLANES = 8
TOKEN_ROWS = 8
ONES_ROWS = 16
VMEM_LIMIT = 56 * 1024 * 1024

PREP_ROWS = 512
ATTN_ROWS = 256
ATTN_TILES = 4
ROUTE_ROWS = 1024
MOVE_ROWS = 256
MOVE_UNROLL = 8
EXPERT_ROWS = 256

C_Q = 0
C_KV = C_Q + Q_RANK
C_KR = C_KV + KV_RANK
C_U = C_KR + LANES
C_V = C_U + GM_OUT
C_END = C_V + GM_OUT
HP = MLA_HEADS * LANES

I_E0, I_E1, I_R0, I_R1, I_G0, I_G1 = range(6)
ROUTE_OUT = 48


def _rms(v, g):
    return v * lax.rsqrt(jnp.mean(v * v, axis=-1, keepdims=True) + EPS) * g


def _ln(v, g, b):
    mu = jnp.mean(v, axis=-1, keepdims=True)
    d = v - mu
    var = jnp.mean(d * d, axis=-1, keepdims=True)
    return d * lax.rsqrt(var + EPS) * g + b


def _dot(a, b):
    return jnp.dot(a, b, preferred_element_type=F32)


def _prep_kernel(x_ref, pos_ref, win_ref, qg_ref, wq_ref, kvg_ref, wk_ref, wv_ref, inv_ref, rope_ref, one_ref,
                 lng_ref, lnb_ref, gavg_ref, ws_ref, bias_ref, gog_ref,
                 q_ref, k_ref, vt_ref, g_ref):
    rows = x_ref.shape[1]
    h = _dot(x_ref[0].astype(BF16), win_ref[...])

    ang = inv_ref[...] * pos_ref[0, 0].astype(F32)
    parts = []
    for t in (jnp.cos(ang), jnp.sin(ang)):
        hi = t.astype(BF16).astype(F32)
        parts += [hi, t - hi]
    tabs = _dot(jnp.concatenate(parts, axis=0).T.astype(BF16), rope_ref[...])
    cos_t = tabs[:, :LANES] + one_ref[...]
    sin_a = tabs[:, LANES:2 * LANES]
    sin_b = tabs[:, 2 * LANES:]
    half = QK_ROPE // 2

    def rotate(v):
        return v * cos_t + pltpu.roll(v, LANES - half, 1) * sin_a + pltpu.roll(v, half, 1) * sin_b

    cq = _rms(h[:, C_Q:C_Q + Q_RANK], qg_ref[...]).astype(BF16)
    q2 = _dot(cq, wq_ref[...])
    for hd in range(MLA_HEADS):
        lo = hd * LANES
        q_ref[0, :, lo:lo + LANES] = (rotate(q2[:, lo:lo + LANES]) * (SM_SCALE * LOG2E)).astype(BF16)

    ckv = _rms(h[:, C_KV:C_KV + KV_RANK], kvg_ref[...]).astype(BF16)
    kp = _dot(ckv, wk_ref[...])
    kr = rotate(h[:, C_KR:C_KR + LANES])
    for hd in range(MLA_HEADS):
        lo = hd * LANES
        k_ref[0, :, lo:lo + LANES] = (kp[:, lo:lo + LANES] + kr).astype(BF16)
    vp = _dot(ckv, wv_ref[...])
    for kb in range(rows // ATTN_ROWS):
        vt_ref[0, kb] = vp[kb * ATTN_ROWS:(kb + 1) * ATTN_ROWS].T.astype(BF16)

    u = jax.nn.gelu(h[:, C_U:C_U + GM_OUT])
    vv = jax.nn.gelu(h[:, C_V:C_V + GM_OUT])
    mu = _dot(vv.astype(BF16), gavg_ref[...])
    d = vv - mu
    var = _dot((d * d).astype(BF16), gavg_ref[...])
    vn = (d * lax.rsqrt(var + EPS) * lng_ref[...] + lnb_ref[...]).astype(BF16)

    tri = lax.broadcasted_iota(jnp.int32, (CHUNK, CHUNK), 0) >= lax.broadcasted_iota(jnp.int32, (CHUNK, CHUNK), 1)
    wm = [jnp.where(tri, ws_ref[g], 0.0).astype(BF16) for g in range(GM_GROUPS)]
    low_half = lax.broadcasted_iota(jnp.int32, (CHUNK, LANES), 1) < GM_CH
    for c in range(rows // CHUNK):
        r0 = c * CHUNK
        parts = []
        for pr in range(GM_GROUPS // 2):
            tile = vn[r0:r0 + CHUNK, pr * LANES:(pr + 1) * LANES]
            parts.append(jnp.where(low_half, _dot(wm[2 * pr], tile), _dot(wm[2 * pr + 1], tile)))
        sg = jnp.concatenate(parts, axis=1) + bias_ref[...]
        gm = u[r0:r0 + CHUNK] * sg
        g_ref[0, r0:r0 + CHUNK, :] = _rms(gm, gog_ref[...]).astype(BF16)


def _prep(x, pos4, w):
    B, S, D = x.shape
    ts = PREP_ROWS
    full = lambda a: pl.BlockSpec(a.shape, lambda b, i: (0,) * a.ndim)
    consts = [w["win"], w["qg"], w["wq"], w["kvg"], w["wk"], w["wv"], w["inv"], w["rope"], w["one"],
              w["lng"], w["lnb"], w["gavg"], w["ws"], w["bias"], w["gog"]]
    return pl.pallas_call(
        _prep_kernel,
        grid=(B, S // ts),
        in_specs=[pl.BlockSpec((1, ts, D), lambda b, i: (b, i, 0)),
                  pl.BlockSpec((1, 1, 1, ts), lambda b, i: (b, i, 0, 0))] + [full(a) for a in consts],
        out_specs=[pl.BlockSpec((1, ts, HP), lambda b, i: (b, i, 0)),
                   pl.BlockSpec((1, ts, HP), lambda b, i: (b, i, 0)),
                   pl.BlockSpec((1, ts // ATTN_ROWS, MLA_OUT, ATTN_ROWS), lambda b, i: (b, i, 0, 0)),
                   pl.BlockSpec((1, ts, GM_OUT), lambda b, i: (b, i, 0))],
        out_shape=[jax.ShapeDtypeStruct((B, S, HP), BF16)] * 2
        + [jax.ShapeDtypeStruct((B, S // ATTN_ROWS, MLA_OUT, ATTN_ROWS), BF16),
           jax.ShapeDtypeStruct((B, S, GM_OUT), BF16)],
        compiler_params=pltpu.CompilerParams(dimension_semantics=("parallel", "parallel"),
                                             vmem_limit_bytes=VMEM_LIMIT),
        name="prep",
    )(x, pos4, *consts)


def _attn_kernel(q_ref, k_ref, vt_ref, g_ref, x_ref, woa_ref, wog_ref, mog_ref, l1g_ref, l1b_ref,
                 o_ref, m_scr, acc_scr, sa_scr, sb_scr):
    pid = pl.program_id(1)
    tq = ATTN_ROWS
    tk = tq
    key = lax.broadcasted_iota(jnp.int32, (tk, tq), 0)
    qry = lax.broadcasted_iota(jnp.int32, (tk, tq), 1)
    diag_mask = key <= qry
    ones = jnp.ones((ONES_ROWS, tk), BF16)

    def tile(t):
        r0 = t * tq
        i = ATTN_TILES * pid + t
        odd = t % 2 == 1
        m_scr[...] = jnp.full(m_scr.shape, MASK_VALUE, F32)
        acc_scr[...] = jnp.zeros(acc_scr.shape, F32)

        def scores(j, s_scr):
            k0 = pl.multiple_of(j * tk, tk)
            for hd in range(MLA_HEADS):
                lo = hd * LANES
                qh = q_ref[0, r0:r0 + tq, lo:lo + LANES]
                kj = k_ref[0, pl.ds(k0, tk), lo:lo + LANES]
                s_scr[hd] = lax.dot_general(kj, qh, (((1,), (1,)), ((), ())), preferred_element_type=F32)

        def update(j, s_scr, masked):
            for hd in range(MLA_HEADS):
                s = s_scr[hd]
                vt = vt_ref[0, j, hd * V_HEAD:(hd + 1) * V_HEAD, :]
                if masked:
                    s = jnp.where(diag_mask, s, MASK_VALUE)
                m_prev = m_scr[hd]
                m_new = jnp.maximum(m_prev, jnp.max(s, axis=0, keepdims=True))
                p = jnp.exp2(s - m_new).astype(BF16)
                scale = jnp.exp2(m_prev - m_new)
                acc_scr[hd] = scale * acc_scr[hd] + _dot(jnp.concatenate([vt, ones], axis=0), p)
                m_scr[hd] = m_new

        def pair(jj, c):
            j = 2 * jj
            scores(j + 1, sb_scr)
            update(j, sa_scr, False)
            scores(j + 2, sa_scr)
            update(j + 1, sb_scr, False)
            return c

        scores(0, sa_scr)
        lax.fori_loop(0, (ATTN_TILES // 2) * pid + t // 2, pair, 0)
        if odd:
            scores(i, sb_scr)
            update(i - 1, sa_scr, False)
            update(i, sb_scr, True)
        else:
            update(i, sa_scr, True)

        at = jnp.concatenate([acc_scr[hd, :V_HEAD] / acc_scr[hd, V_HEAD:V_HEAD + 1] for hd in range(MLA_HEADS)],
                             axis=0)
        at = at * lax.rsqrt(jnp.mean(at * at, axis=0, keepdims=True) + EPS) * mog_ref[...]
        mix = _dot(at.T.astype(BF16), woa_ref[...]) + _dot(g_ref[0, r0:r0 + tq, :], wog_ref[...])
        o_ref[0, r0:r0 + tq, :] = _ln(ALPHA * x_ref[0, r0:r0 + tq, :] + mix, l1g_ref[...], l1b_ref[...])

    for t in range(ATTN_TILES):
        tile(t)


def _attn(q, k, vt, g, x, w):
    B, S, D = x.shape
    tq = ATTN_ROWS
    rows = ATTN_TILES * tq
    full = lambda a: pl.BlockSpec(a.shape, lambda b, i: (0,) * a.ndim)
    consts = [w["woa"], w["wog"], w["mog"], w["l1g"], w["l1b"]]
    return pl.pallas_call(
        _attn_kernel,
        grid=(B, S // rows),
        in_specs=[pl.BlockSpec((1, rows, HP), lambda b, i: (b, i, 0)),
                  pl.BlockSpec((1, S, HP), lambda b, i: (b, 0, 0)),
                  pl.BlockSpec((1,) + vt.shape[1:], lambda b, i: (b, 0, 0, 0)),
                  pl.BlockSpec((1, rows, GM_OUT), lambda b, i: (b, i, 0)),
                  pl.BlockSpec((1, rows, D), lambda b, i: (b, i, 0))] + [full(a) for a in consts],
        out_specs=pl.BlockSpec((1, rows, D), lambda b, i: (b, i, 0)),
        out_shape=jax.ShapeDtypeStruct((B, S, D), F32),
        scratch_shapes=[pltpu.VMEM((MLA_HEADS, 1, tq), F32),
                        pltpu.VMEM((MLA_HEADS, V_HEAD + ONES_ROWS, tq), F32),
                        pltpu.VMEM((MLA_HEADS, tq, tq), F32), pltpu.VMEM((MLA_HEADS, tq, tq), F32)],
        compiler_params=pltpu.CompilerParams(dimension_semantics=("parallel", "parallel"),
                                             vmem_limit_bytes=VMEM_LIMIT),
        name="attn",
    )(q, k, vt, g, x, *consts)


def _route_kernel(x_ref, wr_ref, br_ref, info_ref, infot_ref, cnt_ref, carry_scr, tri_scr):
    step = pl.program_id(0)
    tt = x_ref.shape[0]

    @pl.when(step == 0)
    def _():
        carry_scr[...] = jnp.zeros_like(carry_scr)
        s = lax.broadcasted_iota(jnp.int32, (tt, tt), 0)
        t = lax.broadcasted_iota(jnp.int32, (tt, tt), 1)
        tri_scr[...] = jnp.where(s < t, 1.0, 0.0).astype(BF16)

    x = x_ref[...]
    xh = x.astype(BF16)
    xl = (x - xh.astype(F32)).astype(BF16)
    wr = wr_ref[...]
    wh = wr.astype(BF16)
    wl = (wr - wh.astype(F32)).astype(BF16)
    nt = (((1,), (1,)), ((), ()))
    logits = (lax.dot_general(wh, xh, nt, preferred_element_type=F32)
              + lax.dot_general(wh, xl, nt, preferred_element_type=F32)
              + lax.dot_general(wl, xh, nt, preferred_element_type=F32)) + br_ref[...]
    neg = jnp.float32(-jnp.inf)

    lg = logits[N_EXPERTS:N_EXPERTS + SUBLANES]
    grow = lax.broadcasted_iota(jnp.int32, lg.shape, 0)
    lg = jnp.where(grow < N_GROUPS, lg, neg)
    gmax = jnp.max(lg, axis=0, keepdims=True)
    g_idx = jnp.min(jnp.where(lg == gmax, grow, SUBLANES), axis=0, keepdims=True)
    g_p = 1.0 / jnp.sum(jnp.exp(lg - gmax), axis=0, keepdims=True)

    le = logits[:N_EXPERTS]
    row = lax.broadcasted_iota(jnp.int32, le.shape, 0)
    le = jnp.where((row >> 3) == g_idx, le, neg)
    m1 = jnp.max(le, axis=0, keepdims=True)
    i1 = jnp.min(jnp.where(le == m1, row, N_EXPERTS), axis=0, keepdims=True)
    le2 = jnp.where(row == i1, neg, le)
    m2 = jnp.max(le2, axis=0, keepdims=True)
    i2 = jnp.min(jnp.where(le2 == m2, row, N_EXPERTS), axis=0, keepdims=True)
    e2 = jnp.exp(m2 - m1)
    gate0 = g_p / (1.0 + e2)
    gate1 = g_p * e2 / (1.0 + e2)

    hit1 = row == i1
    hit2 = row == i2
    onehot = jnp.where(hit1 | hit2, 1.0, 0.0)
    before = _dot(onehot.astype(BF16), tri_scr[...]) + carry_scr[...]
    rank0 = jnp.sum(jnp.where(hit1, before, 0.0), axis=0, keepdims=True)
    rank1 = jnp.sum(jnp.where(hit2, before, 0.0), axis=0, keepdims=True)
    carry_scr[...] = carry_scr[...] + jnp.sum(onehot, axis=1, keepdims=True)
    cnt_ref[...] = jnp.broadcast_to(carry_scr[...], cnt_ref.shape)

    fields = jnp.concatenate([i1.astype(F32), i2.astype(F32), rank0, rank1, gate0, gate1,
                              jnp.zeros((SUBLANES - 6, tt), F32)], axis=0)
    infot_ref[0] = fields
    info_ref[...] = jnp.concatenate([fields, jnp.zeros((LANES - SUBLANES, tt), F32)], axis=0).T


def _route(x1, wr, br):
    T, D = x1.shape
    tt = ROUTE_ROWS
    return pl.pallas_call(
        _route_kernel,
        grid=(T // tt,),
        in_specs=[pl.BlockSpec((tt, D), lambda i: (i, 0)),
                  pl.BlockSpec(wr.shape, lambda i: (0, 0)),
                  pl.BlockSpec(br.shape, lambda i: (0, 0))],
        out_specs=[pl.BlockSpec((tt, LANES), lambda i: (i, 0)),
                   pl.BlockSpec((1, SUBLANES, tt), lambda i: (i, 0, 0)),
                   pl.BlockSpec((N_EXPERTS, LANES), lambda i: (0, 0))],
        out_shape=[jax.ShapeDtypeStruct((T, LANES), F32), jax.ShapeDtypeStruct((T // tt, SUBLANES, tt), F32),
                   jax.ShapeDtypeStruct((N_EXPERTS, LANES), F32)],
        scratch_shapes=[pltpu.VMEM((N_EXPERTS, 1), F32), pltpu.VMEM((tt, tt), BF16)],
        compiler_params=pltpu.CompilerParams(dimension_semantics=("arbitrary",), vmem_limit_bytes=VMEM_LIMIT),
        name="route",
    )(x1, wr, br)


def _to_token_tiles(dst_ref, val):
    dst_ref[...] = val.astype(BF16).reshape(dst_ref.shape)


def _from_token_tiles(src_ref, rows):
    return src_ref[...].reshape(rows, TOKEN_ROWS * LANES)


def _to_token_tiles_f32(dst_ref, val):
    rows = val.shape[0]
    for c in range(TOKEN_ROWS):
        dst_ref[pl.ds(c, rows, stride=TOKEN_ROWS), :] = val[:, c * LANES:(c + 1) * LANES]


def _from_token_tiles_f32(src_ref, rows):
    return jnp.concatenate([src_ref[pl.ds(c, rows, stride=TOKEN_ROWS), :] for c in range(TOKEN_ROWS)], axis=1)


def _tile_copy(src_ref, src_row, dst_ref, dst_row, sem):
    return pltpu.make_async_copy(src_ref.at[pl.ds(pl.multiple_of(src_row, TOKEN_ROWS), TOKEN_ROWS)],
                                 dst_ref.at[pl.ds(pl.multiple_of(dst_row, TOKEN_ROWS), TOKEN_ROWS)], sem)


def _dispatch_kernel(seg_ref, dest_ref, x0_ref, xn_ref, buf_ref, stage_scr, zero_scr, sem, zero_sem, *, n_steps):
    i = pl.program_id(0)
    rows = xn_ref.shape[0]
    cur = i % 3
    nxt = (i + 1) % 3

    @pl.when(i == 0)
    def _():
        zero_scr[...] = jnp.zeros(zero_scr.shape, BF16)

        block = EXPERT_ROWS * TOKEN_ROWS
        n_blocks = buf_ref.shape[0] // block

        def clear_rows(first):
            return pltpu.make_async_copy(zero_scr, buf_ref.at[pl.ds(pl.multiple_of(first, SUBLANES), block)], zero_sem)

        def clear(e):
            return clear_rows((seg_ref[0, e] - EXPERT_ROWS) * TOKEN_ROWS)

        def start_tail(b, c):
            clear_rows(b * block).start()
            return c

        def wait_tail(b, c):
            clear_rows(b * block).wait()
            return c

        for e in range(N_EXPERTS):
            pl.when(seg_ref[1, e] > 0)(lambda e=e: clear(e).start())
        lax.fori_loop(seg_ref[2, 0], n_blocks, start_tail, 0)
        for e in range(N_EXPERTS):
            pl.when(seg_ref[1, e] > 0)(lambda e=e: clear(e).wait())
        lax.fori_loop(seg_ref[2, 0], n_blocks, wait_tail, 0)

        _to_token_tiles(stage_scr.at[0], x0_ref[...])

    def drain(s):
        for _ in range(TOP_K):
            pltpu.make_async_copy(stage_scr.at[s], stage_scr.at[s], sem.at[s]).wait()

    @pl.when(i >= 2)
    def _():
        drain(nxt)

    _to_token_tiles(stage_scr.at[nxt], xn_ref[...])
    for r in range(rows):
        for kk in range(TOP_K):
            _tile_copy(stage_scr.at[cur], r * TOKEN_ROWS, buf_ref, dest_ref[0, 0, kk * rows + r],
                       sem.at[cur]).start(priority=kk)

    @pl.when(i == n_steps - 1)
    def _():
        drain(cur)
        if n_steps >= 2:
            drain((i + 2) % 3)


def _dispatch(seg, dest3, x1, n_rows):
    T, D = x1.shape
    td = MOVE_ROWS
    n_steps = T // td
    grid_spec = pltpu.PrefetchScalarGridSpec(
        num_scalar_prefetch=1,
        grid=(n_steps,),
        in_specs=[pl.BlockSpec((1, 1, TOP_K * td), lambda i, seg: (i, 0, 0), memory_space=pltpu.SMEM),
                  pl.BlockSpec((td, D), lambda i, seg: (0, 0)),
                  pl.BlockSpec((td, D), lambda i, seg: (jnp.minimum(i + 1, n_steps - 1), 0))],
        out_specs=pl.BlockSpec(memory_space=pl.ANY),
        scratch_shapes=[pltpu.VMEM((3, td * TOKEN_ROWS, LANES), BF16),
                        pltpu.VMEM((EXPERT_ROWS * TOKEN_ROWS, LANES), BF16),
                        pltpu.SemaphoreType.DMA((3,)), pltpu.SemaphoreType.DMA(())],
    )
    return pl.pallas_call(
        functools.partial(_dispatch_kernel, n_steps=n_steps),
        grid_spec=grid_spec,
        out_shape=jax.ShapeDtypeStruct((n_rows * TOKEN_ROWS, LANES), BF16),
        compiler_params=pltpu.CompilerParams(dimension_semantics=("arbitrary",), vmem_limit_bytes=VMEM_LIMIT),
        name="dispatch",
    )(seg, dest3, x1, x1)


def _expert_kernel(be_ref, ne_ref, nu_ref, buf0_ref, bufa_ref, bufb_ref, wg_hbm, wu_hbm, wd_hbm, y_ref,
                   sg_scr, su_scr, sd_scr, wg_scr, wu_scr, wd_scr, xa_scr, xb_scr, cur_ref, sem):
    step = pl.program_id(0)
    bm = EXPERT_ROWS
    half = bm * TOKEN_ROWS

    def fetch(expert, s):
        return (pltpu.make_async_copy(wg_hbm.at[expert], sg_scr.at[s], sem.at[s, 0]),
                pltpu.make_async_copy(wu_hbm.at[expert], su_scr.at[s], sem.at[s, 1]),
                pltpu.make_async_copy(wd_hbm.at[expert], sd_scr.at[s], sem.at[s, 2]))

    @pl.when(step == 0)
    def _():
        cur_ref[0] = 0
        for c in fetch(be_ref[0], 0):
            c.start()
        xa_scr[...] = _from_token_tiles(buf0_ref, bm)

    def load_weights(blk):
        e = be_ref[blk]

        @pl.when((blk == 0) | (be_ref[jnp.maximum(blk - 1, 0)] != e))
        def _():
            s = cur_ref[0]
            for c in fetch(e, s):
                c.wait()
            wg_scr[...] = sg_scr[s].astype(BF16)
            wu_scr[...] = su_scr[s].astype(BF16)
            wd_scr[...] = sd_scr[s].astype(BF16)
            nxt = ne_ref[blk]

            @pl.when(nxt >= 0)
            def _():
                for c in fetch(nxt, 1 - s):
                    c.start()

            cur_ref[0] = 1 - s

    def compute(x_scr, nxt_ref, nxt_scr, out_rows):
        nxt_scr[...] = _from_token_tiles(nxt_ref, bm)
        xb = x_scr[...]
        hidden = jax.nn.silu(_dot(xb, wg_scr[...])) * _dot(xb, wu_scr[...])
        _to_token_tiles_f32(y_ref.at[out_rows], _dot(hidden.astype(BF16), wd_scr[...]))

    def run(blk, x_scr, nxt_ref, nxt_scr, out_rows):
        @pl.when(blk < nu_ref[0])
        def _():
            compute(x_scr, nxt_ref, nxt_scr, out_rows)

        @pl.when(blk >= nu_ref[0])
        def _():
            y_ref[out_rows, :] = jnp.zeros((half, LANES), F32)

    blk_a, blk_b = 2 * step, 2 * step + 1
    rows_a, rows_b = pl.ds(0, half), pl.ds(half, half)
    load_weights(blk_a)
    same = (be_ref[blk_a] == be_ref[blk_b]) & (blk_b < nu_ref[0])

    @pl.when(same)
    def _():
        compute(xa_scr, bufa_ref, xb_scr, rows_a)
        compute(xb_scr, bufb_ref, xa_scr, rows_b)

    @pl.when(jnp.logical_not(same))
    def _():
        run(blk_a, xa_scr, bufa_ref, xb_scr, rows_a)
        load_weights(blk_b)
        run(blk_b, xb_scr, bufb_ref, xa_scr, rows_b)


def _experts(block_expert, next_expert, n_used, buf, w_gate, w_up, w_down):
    bm = EXPERT_ROWS
    D, ff = w_gate.shape[1:]
    n_blocks = buf.shape[0] // (bm * TOKEN_ROWS)
    assert n_blocks % 2 == 0
    last = n_blocks - 1
    grid_spec = pltpu.PrefetchScalarGridSpec(
        num_scalar_prefetch=3,
        grid=(n_blocks // 2,),
        in_specs=[pl.BlockSpec((bm * TOKEN_ROWS, LANES), lambda s, *_: (0, 0)),
                  pl.BlockSpec((bm * TOKEN_ROWS, LANES), lambda s, *_: (2 * s + 1, 0)),
                  pl.BlockSpec((bm * TOKEN_ROWS, LANES), lambda s, *_: (jnp.minimum(2 * s + 2, last), 0)),
                  pl.BlockSpec(memory_space=pl.ANY),
                  pl.BlockSpec(memory_space=pl.ANY),
                  pl.BlockSpec(memory_space=pl.ANY)],
        out_specs=pl.BlockSpec((2 * bm * TOKEN_ROWS, LANES), lambda s, *_: (s, 0)),
        scratch_shapes=[pltpu.VMEM((2, D, ff), F32), pltpu.VMEM((2, D, ff), F32), pltpu.VMEM((2, ff, D), F32),
                        pltpu.VMEM((D, ff), BF16), pltpu.VMEM((D, ff), BF16), pltpu.VMEM((ff, D), BF16),
                        pltpu.VMEM((bm, D), BF16), pltpu.VMEM((bm, D), BF16),
                        pltpu.SMEM((1,), jnp.int32), pltpu.SemaphoreType.DMA((2, 3))],
    )
    return pl.pallas_call(
        _expert_kernel,
        grid_spec=grid_spec,
        out_shape=jax.ShapeDtypeStruct(buf.shape, F32),
        compiler_params=pltpu.CompilerParams(dimension_semantics=("arbitrary",), vmem_limit_bytes=VMEM_LIMIT),
        name="experts",
    )(block_expert, next_expert, n_used, buf, buf, buf, w_gate, w_up, w_down)


def _final_kernel(d0_ref, d1_ref, d2_ref, x_ref, info_ref, y_ref, p_ref, wpg_ref, bpg_ref, wpp_ref,
                  l2g_ref, l2b_ref, l3g_ref, l3b_ref, o_ref, rows_scr, sem):
    i = pl.program_id(0)
    last = pl.num_programs(0) - 1
    rows = x_ref.shape[0]
    slot = i % 3
    ahead = (i + 2) % 3

    def row_copy(dref, s, r, kk):
        return _tile_copy(y_ref, dref[0, 0, kk * rows + r], rows_scr.at[s, kk], r * TOKEN_ROWS, sem.at[s])

    def landed(s):
        pltpu.make_async_copy(rows_scr.at[s], rows_scr.at[s], sem.at[s]).wait()

    @pl.when(i == 0)
    def _():
        def start(c, carry):
            for u in range(MOVE_UNROLL):
                for kk in range(TOP_K):
                    row_copy(d0_ref, 0, c * MOVE_UNROLL + u, kk).start(priority=kk)
                    row_copy(d1_ref, 1, c * MOVE_UNROLL + u, kk).start(priority=kk)
            return carry

        lax.fori_loop(0, rows // MOVE_UNROLL, start, 0)

    landed(slot)
    info = info_ref[...]
    gate0 = info[:, I_G0:I_G0 + 1]
    gate1 = info[:, I_G1:I_G1 + 1]
    moe = (_from_token_tiles_f32(rows_scr.at[slot, 0], rows) * gate0
           + _from_token_tiles_f32(rows_scr.at[slot, 1], rows) * gate1)

    for r in range(rows):
        for kk in range(TOP_K):
            row_copy(d2_ref, ahead, r, kk).start(priority=kk)

    pp = _dot(p_ref[...].astype(BF16), wpp_ref[...])
    x2 = _ln(ALPHA * x_ref[...] + moe, l2g_ref[...], l2b_ref[...])
    gate = jax.nn.sigmoid(_dot(x2.astype(BF16), wpg_ref[...]) + bpg_ref[...])
    o_ref[...] = _ln(ALPHA * x2 + gate * pp, l3g_ref[...], l3b_ref[...])

    @pl.when(i == last)
    def _():
        landed((i + 1) % 3)
        landed(ahead)


def _final(dest3, x1, info, y, p2, w):
    T, D = x1.shape
    tc = MOVE_ROWS
    pd = p2.shape[1]
    full = lambda a: pl.BlockSpec(a.shape, lambda i: (0,) * a.ndim)
    consts = [w["wpg"], w["bpg"], w["wpp"], w["l2g"], w["l2b"], w["l3g"], w["l3b"]]
    last = T // tc - 1
    assert last >= 2
    return pl.pallas_call(
        _final_kernel,
        grid=(T // tc,),
        in_specs=[pl.BlockSpec((1, 1, TOP_K * tc), lambda i: (i, 0, 0), memory_space=pltpu.SMEM),
                  pl.BlockSpec((1, 1, TOP_K * tc), lambda i: (jnp.minimum(i + 1, last), 0, 0), memory_space=pltpu.SMEM),
                  pl.BlockSpec((1, 1, TOP_K * tc), lambda i: (jnp.minimum(i + 2, last), 0, 0), memory_space=pltpu.SMEM),
                  pl.BlockSpec((tc, D), lambda i: (i, 0)),
                  pl.BlockSpec((tc, LANES), lambda i: (i, 0)),
                  pl.BlockSpec(memory_space=pl.ANY),
                  pl.BlockSpec((tc, pd), lambda i: (i, 0))] + [full(a) for a in consts],
        out_specs=pl.BlockSpec((tc, D), lambda i: (i, 0)),
        out_shape=jax.ShapeDtypeStruct((T, D), F32),
        scratch_shapes=[pltpu.VMEM((3, TOP_K, tc * TOKEN_ROWS, LANES), F32), pltpu.SemaphoreType.DMA((3,))],
        compiler_params=pltpu.CompilerParams(dimension_semantics=("arbitrary",), vmem_limit_bytes=VMEM_LIMIT),
        name="final",
    )(dest3, dest3, dest3, x1, info, y, p2, *consts)


def _pad_heads(a, width):
    lead = a.shape[:-1]
    a = a.reshape(lead + (MLA_HEADS, width))
    a = jnp.pad(a, [(0, 0)] * len(lead) + [(0, 0), (0, LANES - width)])
    return a.reshape(lead + (HP,))


def _layer_weights(w_in, q_norm_g, w_q_up, kv_norm_g, w_kv_up, gm_ln_g, gm_ln_b, gm_w_s, gm_b_s,
                   mla_out_g, gm_out_g, w_o, ln1_g, ln1_b):
    D = w_in.shape[0]
    half = QK_ROPE // 2
    c1, c2, c3 = Q_RANK, Q_RANK + KV_RANK, Q_RANK + KV_RANK + QK_ROPE
    zeros = lambda *s: jnp.zeros(s, F32)
    kr = jnp.concatenate([zeros(D, QK_NOPE), w_in[:, c2:c3], zeros(D, LANES - QK_NOPE - QK_ROPE)], axis=1)
    win = jnp.concatenate([w_in[:, :c2], kr, w_in[:, c3:]], axis=1).astype(BF16)
    wq = _pad_heads(w_q_up, QK_NOPE + QK_ROPE).astype(BF16)

    wkv3 = w_kv_up.reshape(KV_RANK, MLA_HEADS, QK_NOPE + V_HEAD)
    wk = _pad_heads(wkv3[..., :QK_NOPE].reshape(KV_RANK, -1), QK_NOPE).astype(BF16)
    wv = wkv3[..., QK_NOPE:].reshape(KV_RANK, -1).astype(BF16)

    inv = (ROPE_THETA ** (-jnp.arange(0, QK_ROPE, 2, dtype=F32) / QK_ROPE))[:, None]
    eye = jnp.eye(half, dtype=F32)
    first = jnp.pad(eye, ((0, 0), (QK_NOPE, LANES - QK_NOPE - half)))
    second = jnp.pad(eye, ((0, 0), (QK_NOPE + half, LANES - QK_NOPE - QK_ROPE)))
    zero = jnp.zeros_like(first)
    cos_rows = jnp.concatenate([first + second, zero, zero], axis=1)
    sin_rows = jnp.concatenate([zero, -first, second], axis=1)
    rope = jnp.concatenate([cos_rows, cos_rows, sin_rows, sin_rows], axis=0).astype(BF16)
    lane = jnp.arange(LANES)
    one = jnp.where((lane >= QK_NOPE) & (lane < QK_NOPE + QK_ROPE), 0.0, 1.0)[None, :]

    grp = jnp.arange(GM_OUT) // GM_CH
    gavg = jnp.where(grp[:, None] == grp[None, :], 1.0 / GM_CH, 0.0).astype(BF16)
    bias = jnp.repeat(gm_b_s.T, GM_CH, axis=1)

    woa = w_o[:MLA_OUT].astype(BF16)
    wog = w_o[MLA_OUT:].astype(BF16)
    return dict(win=win, qg=q_norm_g[None, :], wq=wq, kvg=kv_norm_g[None, :], wk=wk, wv=wv, inv=inv, rope=rope, one=one,
                lng=gm_ln_g[None, :], lnb=gm_ln_b[None, :], gavg=gavg, ws=gm_w_s, bias=bias, gog=gm_out_g[None, :],
                woa=woa, wog=wog, mog=mla_out_g[:, None], l1g=ln1_g[None, :], l1b=ln1_b[None, :])


def _moe(x1, w_rg, b_rg, w_re, b_re, w_gate, w_up, w_down):
    T, D = x1.shape
    pad = ROUTE_OUT - N_EXPERTS - N_GROUPS
    wr = jnp.concatenate([w_re.T, w_rg.T, jnp.zeros((pad, D), F32)], axis=0)
    br = jnp.concatenate([b_re, b_rg, jnp.zeros((pad,), F32)])[:, None]
    info, info_t, cnt = _route(x1, wr, br)

    bm = EXPERT_ROWS
    n_blocks = (T * TOP_K) // bm + N_EXPERTS
    counts = cnt[:, 0].astype(jnp.int32)
    padded = (counts + bm - 1) // bm * bm
    pad_ends = jnp.cumsum(padded)
    pad_starts = pad_ends - padded
    def dest_rows(e_lane, r_lane):
        e = info_t[:, e_lane, :].astype(jnp.int32)
        ids = jnp.arange(N_EXPERTS)[:, None, None]
        seg_start = jnp.sum(jnp.where(e[None] == ids, pad_starts[:, None, None], 0), axis=0)
        return ((seg_start + info_t[:, r_lane, :].astype(jnp.int32)) * TOKEN_ROWS).reshape(T // MOVE_ROWS, MOVE_ROWS)

    dest = jnp.concatenate([dest_rows(I_E0, I_R0), dest_rows(I_E1, I_R1)], axis=1)[:, None, :]
    block_start = jnp.arange(n_blocks, dtype=jnp.int32) * bm
    block_expert = jnp.minimum(jnp.sum(pad_ends[None, :] <= block_start[:, None], axis=1),
                               N_EXPERTS - 1).astype(jnp.int32)

    blk = jnp.arange(n_blocks)
    later = (blk[None, :] > blk[:, None]) & (block_expert[None, :] != block_expert[:, None])
    next_expert = jnp.min(jnp.where(later, block_expert[None, :], N_EXPERTS), axis=1)
    next_expert = jnp.where(next_expert == N_EXPERTS, -1, next_expert).astype(jnp.int32)
    n_used = (pad_ends[-1:] // bm).astype(jnp.int32)

    seg = jnp.stack([pad_ends, padded, jnp.broadcast_to(n_used, (N_EXPERTS,))]).astype(jnp.int32)
    buf = _dispatch(seg, dest, x1, n_blocks * bm)
    y = _experts(block_expert, next_expert, n_used, buf, w_gate, w_up, w_down)
    return info, dest, y


def kernel(x, p, positions, w_in, q_norm_g, w_q_up, kv_norm_g, w_kv_up, gm_ln_g, gm_ln_b, gm_w_s, gm_b_s, mla_out_g, gm_out_g, w_o, ln1_g, ln1_b, w_rg, b_rg, w_re, b_re, w_gate, w_up, w_down, ln2_g, ln2_b, w_pg, b_pg, w_pp, ln3_g, ln3_b):
    B, S, D = x.shape
    T = B * S
    assert S % PREP_ROWS == 0 and PREP_ROWS % ATTN_ROWS == 0 and PREP_ROWS % CHUNK == 0
    assert S % (ATTN_TILES * ATTN_ROWS) == 0 and ATTN_TILES % 2 == 0
    assert T % ROUTE_ROWS == 0 and T % MOVE_ROWS == 0 and (T * TOP_K) % EXPERT_ROWS == 0
    assert D == TOKEN_ROWS * LANES and MOVE_ROWS % MOVE_UNROLL == 0
    pos4 = positions.reshape(B, S // PREP_ROWS, 1, PREP_ROWS)
    for i in range(DEPTH):
        w = _layer_weights(w_in[i], q_norm_g[i], w_q_up[i], kv_norm_g[i], w_kv_up[i], gm_ln_g[i], gm_ln_b[i],
                           gm_w_s[i], gm_b_s[i], mla_out_g[i], gm_out_g[i], w_o[i], ln1_g[i], ln1_b[i])
        q, k, vt, g = _prep(x, pos4, w)
        x1 = _attn(q, k, vt, g, x, w).reshape(T, D)
        info, dest, y = _moe(x1, w_rg[i], b_rg[i], w_re[i], b_re[i], w_gate[i], w_up[i], w_down[i])
        wf = dict(wpg=w_pg[i].astype(BF16), bpg=b_pg[i][None, :], wpp=w_pp[i].astype(BF16),
                  l2g=ln2_g[i][None, :], l2b=ln2_b[i][None, :], l3g=ln3_g[i][None, :], l3b=ln3_b[i][None, :])
        x = _final(dest, x1, info, y, p[i].reshape(T, -1), wf).reshape(B, S, D)
    return x
```

```python
import functools

import jax
import jax.numpy as jnp
from jax import lax
from jax.experimental import pallas as pl
from jax.experimental.pallas import tpu as pltpu

F32 = jnp.float32
BF16 = jnp.bfloat16

MLA_HEADS = 8
QK_NOPE = 64
QK_ROPE = 32
V_HEAD = 64
Q_RANK = 256
KV_RANK = 128
ROPE_THETA = 10000.0
MLA_OUT = MLA_HEADS * V_HEAD
GM_GROUPS = 8
GM_CH = 64
GM_OUT = GM_GROUPS * GM_CH
CHUNK = 128
N_GROUPS = 4
EXP_PER_GROUP = 8
N_EXPERTS = N_GROUPS * EXP_PER_GROUP
TOP_K = 2
EPS = 1e-6
DEPTH = 1
ALPHA = (2.0 * DEPTH) ** 0.25
SM_SCALE = (QK_NOPE + QK_ROPE) ** -0.5
LOG2E = 1.4426950408889634
MASK_VALUE = -1e30

LANES = 128
SUBLANES = 8
TOKEN_ROWS = 8
ONES_ROWS = 16
VMEM_LIMIT = 56 * 1024 * 1024

PREP_ROWS = 512
ATTN_ROWS = 256
ATTN_TILES = 4
ROUTE_ROWS = 2048
RANK_CHUNK = 256
MOVE_ROWS = 256
MOVE_UNROLL = 8
EXPERT_ROWS = 256

C_Q = 0
C_KV = C_Q + Q_RANK
C_KR = C_KV + KV_RANK
C_U = C_KR + LANES
C_V = C_U + GM_OUT
C_END = C_V + GM_OUT
HP = MLA_HEADS * LANES

I_E0, I_E1, I_R0, I_R1, I_G0, I_G1 = range(6)
ROUTE_OUT = 48


def _rms(v, g):
    return v * lax.rsqrt(jnp.mean(v * v, axis=-1, keepdims=True) + EPS) * g


def _ln(v, g, b):
    mu = jnp.mean(v, axis=-1, keepdims=True)
    d = v - mu
    var = jnp.mean(d * d, axis=-1, keepdims=True)
    return d * lax.rsqrt(var + EPS) * g + b


def _dot(a, b):
    return jnp.dot(a, b, preferred_element_type=F32)


def _prep_kernel(x_ref, pos_ref, win_ref, qg_ref, wq_ref, kvg_ref, wk_ref, wv_ref, inv_ref, rope_ref, one_ref,
                 lng_ref, lnb_ref, gavg_ref, ws_ref, bias_ref, gog_ref,
                 q_ref, k_ref, vt_ref, g_ref):
    rows = x_ref.shape[1]
    h = _dot(x_ref[0].astype(BF16), win_ref[...])

    ang = inv_ref[...] * pos_ref[0, 0].astype(F32)
    parts = []
    for t in (jnp.cos(ang), jnp.sin(ang)):
        hi = t.astype(BF16).astype(F32)
        parts += [hi, t - hi]
    tabs = _dot(jnp.concatenate(parts, axis=0).T.astype(BF16), rope_ref[...])
    cos_t = tabs[:, :LANES] + one_ref[...]
    sin_a = tabs[:, LANES:2 * LANES]
    sin_b = tabs[:, 2 * LANES:]
    half = QK_ROPE // 2

    def rotate(v):
        return v * cos_t + pltpu.roll(v, LANES - half, 1) * sin_a + pltpu.roll(v, half, 1) * sin_b

    cq = _rms(h[:, C_Q:C_Q + Q_RANK], qg_ref[...]).astype(BF16)
    q2 = _dot(cq, wq_ref[...])
    for hd in range(MLA_HEADS):
        lo = hd * LANES
        q_ref[0, :, lo:lo + LANES] = (rotate(q2[:, lo:lo + LANES]) * (SM_SCALE * LOG2E)).astype(BF16)

    ckv = _rms(h[:, C_KV:C_KV + KV_RANK], kvg_ref[...]).astype(BF16)
    kp = _dot(ckv, wk_ref[...])
    kr = rotate(h[:, C_KR:C_KR + LANES])
    for hd in range(MLA_HEADS):
        lo = hd * LANES
        k_ref[0, :, lo:lo + LANES] = (kp[:, lo:lo + LANES] + kr).astype(BF16)
    vp = _dot(ckv, wv_ref[...])
    for kb in range(rows // ATTN_ROWS):
        vt_ref[0, kb] = vp[kb * ATTN_ROWS:(kb + 1) * ATTN_ROWS].T.astype(BF16)

    u = jax.nn.gelu(h[:, C_U:C_U + GM_OUT])
    vv = jax.nn.gelu(h[:, C_V:C_V + GM_OUT])
    mu = _dot(vv.astype(BF16), gavg_ref[...])
    d = vv - mu
    var = _dot((d * d).astype(BF16), gavg_ref[...])
    vn = (d * lax.rsqrt(var + EPS) * lng_ref[...] + lnb_ref[...]).astype(BF16)

    tri = lax.broadcasted_iota(jnp.int32, (CHUNK, CHUNK), 0) >= lax.broadcasted_iota(jnp.int32, (CHUNK, CHUNK), 1)
    wm = [jnp.where(tri, ws_ref[g], 0.0).astype(BF16) for g in range(GM_GROUPS)]
    low_half = lax.broadcasted_iota(jnp.int32, (CHUNK, LANES), 1) < GM_CH
    for c in range(rows // CHUNK):
        r0 = c * CHUNK
        parts = []
        for pr in range(GM_GROUPS // 2):
            tile = vn[r0:r0 + CHUNK, pr * LANES:(pr + 1) * LANES]
            parts.append(jnp.where(low_half, _dot(wm[2 * pr], tile), _dot(wm[2 * pr + 1], tile)))
        sg = jnp.concatenate(parts, axis=1) + bias_ref[...]
        gm = u[r0:r0 + CHUNK] * sg
        g_ref[0, r0:r0 + CHUNK, :] = _rms(gm, gog_ref[...]).astype(BF16)


def _prep(x, pos4, w):
    B, S, D = x.shape
    ts = PREP_ROWS
    full = lambda a: pl.BlockSpec(a.shape, lambda b, i: (0,) * a.ndim)
    consts = [w["win"], w["qg"], w["wq"], w["kvg"], w["wk"], w["wv"], w["inv"], w["rope"], w["one"],
              w["lng"], w["lnb"], w["gavg"], w["ws"], w["bias"], w["gog"]]
    return pl.pallas_call(
        _prep_kernel,
        grid=(B, S // ts),
        in_specs=[pl.BlockSpec((1, ts, D), lambda b, i: (b, i, 0)),
                  pl.BlockSpec((1, 1, 1, ts), lambda b, i: (b, i, 0, 0))] + [full(a) for a in consts],
        out_specs=[pl.BlockSpec((1, ts, HP), lambda b, i: (b, i, 0)),
                   pl.BlockSpec((1, ts, HP), lambda b, i: (b, i, 0)),
                   pl.BlockSpec((1, ts // ATTN_ROWS, MLA_OUT, ATTN_ROWS), lambda b, i: (b, i, 0, 0)),
                   pl.BlockSpec((1, ts, GM_OUT), lambda b, i: (b, i, 0))],
        out_shape=[jax.ShapeDtypeStruct((B, S, HP), BF16)] * 2
        + [jax.ShapeDtypeStruct((B, S // ATTN_ROWS, MLA_OUT, ATTN_ROWS), BF16),
           jax.ShapeDtypeStruct((B, S, GM_OUT), BF16)],
        compiler_params=pltpu.CompilerParams(dimension_semantics=("parallel", "parallel"),
                                             vmem_limit_bytes=VMEM_LIMIT),
        name="prep",
    )(x, pos4, *consts)


def _attn_kernel(q_ref, k_ref, vt_ref, g_ref, x_ref, woa_ref, wog_ref, mog_ref, l1g_ref, l1b_ref,
                 o_ref, m_scr, acc_scr, sa_scr, sb_scr):
    pid = pl.program_id(1)
    tq = ATTN_ROWS
    tk = tq
    key = lax.broadcasted_iota(jnp.int32, (tk, tq), 0)
    qry = lax.broadcasted_iota(jnp.int32, (tk, tq), 1)
    diag_mask = key <= qry
    ones = jnp.ones((ONES_ROWS, tk), BF16)

    def tile(t):
        r0 = t * tq
        i = ATTN_TILES * pid + t
        odd = t % 2 == 1
        m_scr[...] = jnp.full(m_scr.shape, MASK_VALUE, F32)
        acc_scr[...] = jnp.zeros(acc_scr.shape, F32)

        def scores(j, s_scr):
            k0 = pl.multiple_of(j * tk, tk)
            for hd in range(MLA_HEADS):
                lo = hd * LANES
                qh = q_ref[0, r0:r0 + tq, lo:lo + LANES]
                kj = k_ref[0, pl.ds(k0, tk), lo:lo + LANES]
                s_scr[hd] = lax.dot_general(kj, qh, (((1,), (1,)), ((), ())), preferred_element_type=F32)

        def update(j, s_scr, masked):
            for hd in range(MLA_HEADS):
                s = s_scr[hd]
                vt = vt_ref[0, j, hd * V_HEAD:(hd + 1) * V_HEAD, :]
                if masked:
                    s = jnp.where(diag_mask, s, MASK_VALUE)
                m_prev = m_scr[hd]
                m_new = jnp.maximum(m_prev, jnp.max(s, axis=0, keepdims=True))
                p = jnp.exp2(s - m_new).astype(BF16)
                scale = jnp.exp2(m_prev - m_new)
                acc_scr[hd] = scale * acc_scr[hd] + _dot(jnp.concatenate([vt, ones], axis=0), p)
                m_scr[hd] = m_new

        def pair(jj, c):
            j = 2 * jj
            scores(j + 1, sb_scr)
            update(j, sa_scr, False)
            scores(j + 2, sa_scr)
            update(j + 1, sb_scr, False)
            return c

        scores(0, sa_scr)
        lax.fori_loop(0, (ATTN_TILES // 2) * pid + t // 2, pair, 0)
        if odd:
            scores(i, sb_scr)
            update(i - 1, sa_scr, False)
            update(i, sb_scr, True)
        else:
            update(i, sa_scr, True)

        at = jnp.concatenate([acc_scr[hd, :V_HEAD] / acc_scr[hd, V_HEAD:V_HEAD + 1] for hd in range(MLA_HEADS)],
                             axis=0)
        at = at * lax.rsqrt(jnp.mean(at * at, axis=0, keepdims=True) + EPS) * mog_ref[...]
        mix = _dot(at.T.astype(BF16), woa_ref[...]) + _dot(g_ref[0, r0:r0 + tq, :], wog_ref[...])
        o_ref[0, r0:r0 + tq, :] = _ln(ALPHA * x_ref[0, r0:r0 + tq, :] + mix, l1g_ref[...], l1b_ref[...])

    for t in range(ATTN_TILES):
        tile(t)


def _attn(q, k, vt, g, x, w):
    B, S, D = x.shape
    tq = ATTN_ROWS
    rows = ATTN_TILES * tq
    full = lambda a: pl.BlockSpec(a.shape, lambda b, i: (0,) * a.ndim)
    consts = [w["woa"], w["wog"], w["mog"], w["l1g"], w["l1b"]]
    return pl.pallas_call(
        _attn_kernel,
        grid=(B, S // rows),
        in_specs=[pl.BlockSpec((1, rows, HP), lambda b, i: (b, i, 0)),
                  pl.BlockSpec((1, S, HP), lambda b, i: (b, 0, 0)),
                  pl.BlockSpec((1,) + vt.shape[1:], lambda b, i: (b, 0, 0, 0)),
                  pl.BlockSpec((1, rows, GM_OUT), lambda b, i: (b, i, 0)),
                  pl.BlockSpec((1, rows, D), lambda b, i: (b, i, 0))] + [full(a) for a in consts],
        out_specs=pl.BlockSpec((1, rows, D), lambda b, i: (b, i, 0)),
        out_shape=jax.ShapeDtypeStruct((B, S, D), F32),
        scratch_shapes=[pltpu.VMEM((MLA_HEADS, 1, tq), F32),
                        pltpu.VMEM((MLA_HEADS, V_HEAD + ONES_ROWS, tq), F32),
                        pltpu.VMEM((MLA_HEADS, tq, tq), F32), pltpu.VMEM((MLA_HEADS, tq, tq), F32)],
        compiler_params=pltpu.CompilerParams(dimension_semantics=("parallel", "parallel"),
                                             vmem_limit_bytes=VMEM_LIMIT),
        name="attn",
    )(q, k, vt, g, x, *consts)


def _route_kernel(x_ref, wr_ref, br_ref, info_ref, infot_ref, cnt_ref, carry_scr, tri_scr):
    step = pl.program_id(0)
    tt = x_ref.shape[0]

    @pl.when(step == 0)
    def _():
        carry_scr[...] = jnp.zeros_like(carry_scr)
        s = lax.broadcasted_iota(jnp.int32, tri_scr.shape, 0)
        t = lax.broadcasted_iota(jnp.int32, tri_scr.shape, 1)
        tri_scr[...] = jnp.where(s < t, 1.0, 0.0).astype(BF16)

    x = x_ref[...]
    xh = x.astype(BF16)
    xl = (x - xh.astype(F32)).astype(BF16)
    wr = wr_ref[...]
    wh = wr.astype(BF16)
    wl = (wr - wh.astype(F32)).astype(BF16)
    nt = (((1,), (1,)), ((), ()))
    by_xh = lax.dot_general(jnp.concatenate([wh, wl], axis=0), xh, nt, preferred_element_type=F32)
    logits = (by_xh[:ROUTE_OUT] + lax.dot_general(wh, xl, nt, preferred_element_type=F32)
              + by_xh[ROUTE_OUT:]) + br_ref[...]
    neg = jnp.float32(-jnp.inf)

    lg = logits[N_EXPERTS:N_EXPERTS + SUBLANES]
    grow = lax.broadcasted_iota(jnp.int32, lg.shape, 0)
    lg = jnp.where(grow < N_GROUPS, lg, neg)
    gmax = jnp.max(lg, axis=0, keepdims=True)
    g_idx = jnp.min(jnp.where(lg == gmax, grow, SUBLANES), axis=0, keepdims=True)
    g_p = 1.0 / jnp.sum(jnp.exp(lg - gmax), axis=0, keepdims=True)

    le = logits[:N_EXPERTS]
    row = lax.broadcasted_iota(jnp.int32, le.shape, 0)
    le = jnp.where((row >> 3) == g_idx, le, neg)
    m1 = jnp.max(le, axis=0, keepdims=True)
    i1 = jnp.min(jnp.where(le == m1, row, N_EXPERTS), axis=0, keepdims=True)
    le2 = jnp.where(row == i1, neg, le)
    m2 = jnp.max(le2, axis=0, keepdims=True)
    i2 = jnp.min(jnp.where(le2 == m2, row, N_EXPERTS), axis=0, keepdims=True)
    e2 = jnp.exp(m2 - m1)
    gate0 = g_p / (1.0 + e2)
    gate1 = g_p * e2 / (1.0 + e2)

    hit1 = row == i1
    hit2 = row == i2
    onehot = jnp.where(hit1 | hit2, 1.0, 0.0)
    rc = RANK_CHUNK
    chunks = [onehot[:, c * rc:(c + 1) * rc] for c in range(tt // rc)]
    inside = _dot(jnp.concatenate(chunks, axis=0).astype(BF16), tri_scr[...])
    seen = carry_scr[...]
    parts = []
    for c, chunk in enumerate(chunks):
        parts.append(inside[c * N_EXPERTS:(c + 1) * N_EXPERTS] + seen)
        seen = seen + jnp.sum(chunk, axis=1, keepdims=True)
    before = jnp.concatenate(parts, axis=1)
    rank0 = jnp.sum(jnp.where(hit1, before, 0.0), axis=0, keepdims=True)
    rank1 = jnp.sum(jnp.where(hit2, before, 0.0), axis=0, keepdims=True)
    carry_scr[...] = seen
    cnt_ref[...] = jnp.broadcast_to(seen, cnt_ref.shape)

    fields = jnp.concatenate([i1.astype(F32), i2.astype(F32), rank0, rank1, gate0, gate1,
                              jnp.zeros((SUBLANES - 6, tt), F32)], axis=0)
    infot_ref[0] = fields
    info_ref[...] = jnp.concatenate([fields, jnp.zeros((LANES - SUBLANES, tt), F32)], axis=0).T


def _route(x1, wr, br):
    T, D = x1.shape
    tt = ROUTE_ROWS
    return pl.pallas_call(
        _route_kernel,
        grid=(T // tt,),
        in_specs=[pl.BlockSpec((tt, D), lambda i: (i, 0)),
                  pl.BlockSpec(wr.shape, lambda i: (0, 0)),
                  pl.BlockSpec(br.shape, lambda i: (0, 0))],
        out_specs=[pl.BlockSpec((tt, LANES), lambda i: (i, 0)),
                   pl.BlockSpec((1, SUBLANES, tt), lambda i: (i, 0, 0)),
                   pl.BlockSpec((N_EXPERTS, LANES), lambda i: (0, 0))],
        out_shape=[jax.ShapeDtypeStruct((T, LANES), F32), jax.ShapeDtypeStruct((T // tt, SUBLANES, tt), F32),
                   jax.ShapeDtypeStruct((N_EXPERTS, LANES), F32)],
        scratch_shapes=[pltpu.VMEM((N_EXPERTS, 1), F32), pltpu.VMEM((RANK_CHUNK, RANK_CHUNK), BF16)],
        compiler_params=pltpu.CompilerParams(dimension_semantics=("arbitrary",), vmem_limit_bytes=VMEM_LIMIT),
        name="route",
    )(x1, wr, br)


def _to_token_tiles(dst_ref, val):
    dst_ref[...] = val.astype(BF16).reshape(dst_ref.shape)


def _from_token_tiles(src_ref, rows):
    return src_ref[...].reshape(rows, TOKEN_ROWS * LANES)


def _to_token_tiles_f32(dst_ref, val):
    rows = val.shape[0]
    for c in range(TOKEN_ROWS):
        dst_ref[pl.ds(c, rows, stride=TOKEN_ROWS), :] = val[:, c * LANES:(c + 1) * LANES]


def _from_token_tiles_f32(src_ref, rows):
    return jnp.concatenate([src_ref[pl.ds(c, rows, stride=TOKEN_ROWS), :] for c in range(TOKEN_ROWS)], axis=1)


def _tile_copy(src_ref, src_row, dst_ref, dst_row, sem):
    return pltpu.make_async_copy(src_ref.at[pl.ds(pl.multiple_of(src_row, TOKEN_ROWS), TOKEN_ROWS)],
                                 dst_ref.at[pl.ds(pl.multiple_of(dst_row, TOKEN_ROWS), TOKEN_ROWS)], sem)


def _dispatch_kernel(seg_ref, dest_ref, x0_ref, xn_ref, buf_ref, stage_scr, zero_scr, sem, zero_sem, *, n_steps):
    i = pl.program_id(0)
    rows = xn_ref.shape[0]
    cur = i % 3
    nxt = (i + 1) % 3

    @pl.when(i == 0)
    def _():
        zero_scr[...] = jnp.zeros(zero_scr.shape, BF16)

        block = EXPERT_ROWS * TOKEN_ROWS
        n_blocks = buf_ref.shape[0] // block

        def clear_rows(first):
            return pltpu.make_async_copy(zero_scr, buf_ref.at[pl.ds(pl.multiple_of(first, SUBLANES), block)], zero_sem)

        def clear(e):
            return clear_rows((seg_ref[0, e] - EXPERT_ROWS) * TOKEN_ROWS)

        def start_tail(b, c):
            clear_rows(b * block).start()
            return c

        def wait_tail(b, c):
            clear_rows(b * block).wait()
            return c

        for e in range(N_EXPERTS):
            pl.when(seg_ref[1, e] > 0)(lambda e=e: clear(e).start())
        lax.fori_loop(seg_ref[2, 0], n_blocks, start_tail, 0)
        for e in range(N_EXPERTS):
            pl.when(seg_ref[1, e] > 0)(lambda e=e: clear(e).wait())
        lax.fori_loop(seg_ref[2, 0], n_blocks, wait_tail, 0)

        _to_token_tiles(stage_scr.at[0], x0_ref[...])

    def drain(s):
        for _ in range(TOP_K):
            pltpu.make_async_copy(stage_scr.at[s], stage_scr.at[s], sem.at[s]).wait()

    @pl.when(i >= 2)
    def _():
        drain(nxt)

    _to_token_tiles(stage_scr.at[nxt], xn_ref[...])
    for r in range(rows):
        for kk in range(TOP_K):
            _tile_copy(stage_scr.at[cur], r * TOKEN_ROWS, buf_ref, dest_ref[0, 0, kk * rows + r],
                       sem.at[cur]).start(priority=kk)

    @pl.when(i == n_steps - 1)
    def _():
        drain(cur)
        if n_steps >= 2:
            drain((i + 2) % 3)


def _dispatch(seg, dest3, x1, n_rows):
    T, D = x1.shape
    td = MOVE_ROWS
    n_steps = T // td
    grid_spec = pltpu.PrefetchScalarGridSpec(
        num_scalar_prefetch=1,
        grid=(n_steps,),
        in_specs=[pl.BlockSpec((1, 1, TOP_K * td), lambda i, seg: (i, 0, 0), memory_space=pltpu.SMEM),
                  pl.BlockSpec((td, D), lambda i, seg: (0, 0)),
                  pl.BlockSpec((td, D), lambda i, seg: (jnp.minimum(i + 1, n_steps - 1), 0))],
        out_specs=pl.BlockSpec(memory_space=pl.ANY),
        scratch_shapes=[pltpu.VMEM((3, td * TOKEN_ROWS, LANES), BF16),
                        pltpu.VMEM((EXPERT_ROWS * TOKEN_ROWS, LANES), BF16),
                        pltpu.SemaphoreType.DMA((3,)), pltpu.SemaphoreType.DMA(())],
    )
    return pl.pallas_call(
        functools.partial(_dispatch_kernel, n_steps=n_steps),
        grid_spec=grid_spec,
        out_shape=jax.ShapeDtypeStruct((n_rows * TOKEN_ROWS, LANES), BF16),
        compiler_params=pltpu.CompilerParams(dimension_semantics=("arbitrary",), vmem_limit_bytes=VMEM_LIMIT),
        name="dispatch",
    )(seg, dest3, x1, x1)


def _expert_kernel(be_ref, ne_ref, nu_ref, buf0_ref, bufa_ref, bufb_ref, wg_hbm, wu_hbm, wd_hbm, y_ref,
                   sg_scr, su_scr, sd_scr, wg_scr, wu_scr, wd_scr, xa_scr, xb_scr, cur_ref, sem):
    step = pl.program_id(0)
    bm = EXPERT_ROWS
    half = bm * TOKEN_ROWS

    def fetch(expert, s):
        return (pltpu.make_async_copy(wg_hbm.at[expert], sg_scr.at[s], sem.at[s, 0]),
                pltpu.make_async_copy(wu_hbm.at[expert], su_scr.at[s], sem.at[s, 1]),
                pltpu.make_async_copy(wd_hbm.at[expert], sd_scr.at[s], sem.at[s, 2]))

    @pl.when(step == 0)
    def _():
        cur_ref[0] = 0
        for c in fetch(be_ref[0], 0):
            c.start()
        xa_scr[...] = _from_token_tiles(buf0_ref, bm)

    def load_weights(blk):
        e = be_ref[blk]

        @pl.when((blk == 0) | (be_ref[jnp.maximum(blk - 1, 0)] != e))
        def _():
            s = cur_ref[0]
            for c in fetch(e, s):
                c.wait()
            wg_scr[...] = sg_scr[s].astype(BF16)
            wu_scr[...] = su_scr[s].astype(BF16)
            wd_scr[...] = sd_scr[s].astype(BF16)
            nxt = ne_ref[blk]

            @pl.when(nxt >= 0)
            def _():
                for c in fetch(nxt, 1 - s):
                    c.start()

            cur_ref[0] = 1 - s

    def compute(x_scr, nxt_ref, nxt_scr, out_rows):
        nxt_scr[...] = _from_token_tiles(nxt_ref, bm)
        xb = x_scr[...]
        hidden = jax.nn.silu(_dot(xb, wg_scr[...])) * _dot(xb, wu_scr[...])
        _to_token_tiles_f32(y_ref.at[out_rows], _dot(hidden.astype(BF16), wd_scr[...]))

    def run(blk, x_scr, nxt_ref, nxt_scr, out_rows):
        @pl.when(blk < nu_ref[0])
        def _():
            compute(x_scr, nxt_ref, nxt_scr, out_rows)

        @pl.when(blk >= nu_ref[0])
        def _():
            y_ref[out_rows, :] = jnp.zeros((half, LANES), F32)

    blk_a, blk_b = 2 * step, 2 * step + 1
    rows_a, rows_b = pl.ds(0, half), pl.ds(half, half)
    load_weights(blk_a)
    same = (be_ref[blk_a] == be_ref[blk_b]) & (blk_b < nu_ref[0])

    @pl.when(same)
    def _():
        compute(xa_scr, bufa_ref, xb_scr, rows_a)
        compute(xb_scr, bufb_ref, xa_scr, rows_b)

    @pl.when(jnp.logical_not(same))
    def _():
        run(blk_a, xa_scr, bufa_ref, xb_scr, rows_a)
        load_weights(blk_b)
        run(blk_b, xb_scr, bufb_ref, xa_scr, rows_b)


def _experts(block_expert, next_expert, n_used, buf, w_gate, w_up, w_down):
    bm = EXPERT_ROWS
    D, ff = w_gate.shape[1:]
    n_blocks = buf.shape[0] // (bm * TOKEN_ROWS)
    assert n_blocks % 2 == 0
    last = n_blocks - 1
    grid_spec = pltpu.PrefetchScalarGridSpec(
        num_scalar_prefetch=3,
        grid=(n_blocks // 2,),
        in_specs=[pl.BlockSpec((bm * TOKEN_ROWS, LANES), lambda s, *_: (0, 0)),
                  pl.BlockSpec((bm * TOKEN_ROWS, LANES), lambda s, *_: (2 * s + 1, 0)),
                  pl.BlockSpec((bm * TOKEN_ROWS, LANES), lambda s, *_: (jnp.minimum(2 * s + 2, last), 0)),
                  pl.BlockSpec(memory_space=pl.ANY),
                  pl.BlockSpec(memory_space=pl.ANY),
                  pl.BlockSpec(memory_space=pl.ANY)],
        out_specs=pl.BlockSpec((2 * bm * TOKEN_ROWS, LANES), lambda s, *_: (s, 0)),
        scratch_shapes=[pltpu.VMEM((2, D, ff), F32), pltpu.VMEM((2, D, ff), F32), pltpu.VMEM((2, ff, D), F32),
                        pltpu.VMEM((D, ff), BF16), pltpu.VMEM((D, ff), BF16), pltpu.VMEM((ff, D), BF16),
                        pltpu.VMEM((bm, D), BF16), pltpu.VMEM((bm, D), BF16),
                        pltpu.SMEM((1,), jnp.int32), pltpu.SemaphoreType.DMA((2, 3))],
    )
    return pl.pallas_call(
        _expert_kernel,
        grid_spec=grid_spec,
        out_shape=jax.ShapeDtypeStruct(buf.shape, F32),
        compiler_params=pltpu.CompilerParams(dimension_semantics=("arbitrary",), vmem_limit_bytes=VMEM_LIMIT),
        name="experts",
    )(block_expert, next_expert, n_used, buf, buf, buf, w_gate, w_up, w_down)


def _final_kernel(d0_ref, d1_ref, d2_ref, x_ref, info_ref, y_ref, p_ref, wpg_ref, bpg_ref, wpp_ref,
                  l2g_ref, l2b_ref, l3g_ref, l3b_ref, o_ref, rows_scr, sem):
    i = pl.program_id(0)
    last = pl.num_programs(0) - 1
    rows = x_ref.shape[0]
    slot = i % 3
    ahead = (i + 2) % 3

    def row_copy(dref, s, r, kk):
        return _tile_copy(y_ref, dref[0, 0, kk * rows + r], rows_scr.at[s, kk], r * TOKEN_ROWS, sem.at[s])

    def landed(s):
        pltpu.make_async_copy(rows_scr.at[s], rows_scr.at[s], sem.at[s]).wait()

    @pl.when(i == 0)
    def _():
        def start(c, carry):
            for u in range(MOVE_UNROLL):
                for kk in range(TOP_K):
                    row_copy(d0_ref, 0, c * MOVE_UNROLL + u, kk).start(priority=kk)
                    row_copy(d1_ref, 1, c * MOVE_UNROLL + u, kk).start(priority=kk)
            return carry

        lax.fori_loop(0, rows // MOVE_UNROLL, start, 0)

    landed(slot)
    info = info_ref[...]
    gate0 = info[:, I_G0:I_G0 + 1]
    gate1 = info[:, I_G1:I_G1 + 1]
    moe = (_from_token_tiles_f32(rows_scr.at[slot, 0], rows) * gate0
           + _from_token_tiles_f32(rows_scr.at[slot, 1], rows) * gate1)

    for r in range(rows):
        for kk in range(TOP_K):
            row_copy(d2_ref, ahead, r, kk).start(priority=kk)

    pp = _dot(p_ref[...].astype(BF16), wpp_ref[...])
    x2 = _ln(ALPHA * x_ref[...] + moe, l2g_ref[...], l2b_ref[...])
    gate = jax.nn.sigmoid(_dot(x2.astype(BF16), wpg_ref[...]) + bpg_ref[...])
    o_ref[...] = _ln(ALPHA * x2 + gate * pp, l3g_ref[...], l3b_ref[...])

    @pl.when(i == last)
    def _():
        landed((i + 1) % 3)
        landed(ahead)


def _final(dest3, x1, info, y, p2, w):
    T, D = x1.shape
    tc = MOVE_ROWS
    pd = p2.shape[1]
    full = lambda a: pl.BlockSpec(a.shape, lambda i: (0,) * a.ndim)
    consts = [w["wpg"], w["bpg"], w["wpp"], w["l2g"], w["l2b"], w["l3g"], w["l3b"]]
    last = T // tc - 1
    assert last >= 2
    return pl.pallas_call(
        _final_kernel,
        grid=(T // tc,),
        in_specs=[pl.BlockSpec((1, 1, TOP_K * tc), lambda i: (i, 0, 0), memory_space=pltpu.SMEM),
                  pl.BlockSpec((1, 1, TOP_K * tc), lambda i: (jnp.minimum(i + 1, last), 0, 0), memory_space=pltpu.SMEM),
                  pl.BlockSpec((1, 1, TOP_K * tc), lambda i: (jnp.minimum(i + 2, last), 0, 0), memory_space=pltpu.SMEM),
                  pl.BlockSpec((tc, D), lambda i: (i, 0)),
                  pl.BlockSpec((tc, LANES), lambda i: (i, 0)),
                  pl.BlockSpec(memory_space=pl.ANY),
                  pl.BlockSpec((tc, pd), lambda i: (i, 0))] + [full(a) for a in consts],
        out_specs=pl.BlockSpec((tc, D), lambda i: (i, 0)),
        out_shape=jax.ShapeDtypeStruct((T, D), F32),
        scratch_shapes=[pltpu.VMEM((3, TOP_K, tc * TOKEN_ROWS, LANES), F32), pltpu.SemaphoreType.DMA((3,))],
        compiler_params=pltpu.CompilerParams(dimension_semantics=("arbitrary",), vmem_limit_bytes=VMEM_LIMIT),
        name="final",
    )(dest3, dest3, dest3, x1, info, y, p2, *consts)


def _pad_heads(a, width):
    lead = a.shape[:-1]
    a = a.reshape(lead + (MLA_HEADS, width))
    a = jnp.pad(a, [(0, 0)] * len(lead) + [(0, 0), (0, LANES - width)])
    return a.reshape(lead + (HP,))


def _layer_weights(w_in, q_norm_g, w_q_up, kv_norm_g, w_kv_up, gm_ln_g, gm_ln_b, gm_w_s, gm_b_s,
                   mla_out_g, gm_out_g, w_o, ln1_g, ln1_b):
    D = w_in.shape[0]
    half = QK_ROPE // 2
    c1, c2, c3 = Q_RANK, Q_RANK + KV_RANK, Q_RANK + KV_RANK + QK_ROPE
    zeros = lambda *s: jnp.zeros(s, F32)
    kr = jnp.concatenate([zeros(D, QK_NOPE), w_in[:, c2:c3], zeros(D, LANES - QK_NOPE - QK_ROPE)], axis=1)
    win = jnp.concatenate([w_in[:, :c2], kr, w_in[:, c3:]], axis=1).astype(BF16)
    wq = _pad_heads(w_q_up, QK_NOPE + QK_ROPE).astype(BF16)

    wkv3 = w_kv_up.reshape(KV_RANK, MLA_HEADS, QK_NOPE + V_HEAD)
    wk = _pad_heads(wkv3[..., :QK_NOPE].reshape(KV_RANK, -1), QK_NOPE).astype(BF16)
    wv = wkv3[..., QK_NOPE:].reshape(KV_RANK, -1).astype(BF16)

    inv = (ROPE_THETA ** (-jnp.arange(0, QK_ROPE, 2, dtype=F32) / QK_ROPE))[:, None]
    eye = jnp.eye(half, dtype=F32)
    first = jnp.pad(eye, ((0, 0), (QK_NOPE, LANES - QK_NOPE - half)))
    second = jnp.pad(eye, ((0, 0), (QK_NOPE + half, LANES - QK_NOPE - QK_ROPE)))
    zero = jnp.zeros_like(first)
    cos_rows = jnp.concatenate([first + second, zero, zero], axis=1)
    sin_rows = jnp.concatenate([zero, -first, second], axis=1)
    rope = jnp.concatenate([cos_rows, cos_rows, sin_rows, sin_rows], axis=0).astype(BF16)
    lane = jnp.arange(LANES)
    one = jnp.where((lane >= QK_NOPE) & (lane < QK_NOPE + QK_ROPE), 0.0, 1.0)[None, :]

    grp = jnp.arange(GM_OUT) // GM_CH
    gavg = jnp.where(grp[:, None] == grp[None, :], 1.0 / GM_CH, 0.0).astype(BF16)
    bias = jnp.repeat(gm_b_s.T, GM_CH, axis=1)

    woa = w_o[:MLA_OUT].astype(BF16)
    wog = w_o[MLA_OUT:].astype(BF16)
    return dict(win=win, qg=q_norm_g[None, :], wq=wq, kvg=kv_norm_g[None, :], wk=wk, wv=wv, inv=inv, rope=rope, one=one,
                lng=gm_ln_g[None, :], lnb=gm_ln_b[None, :], gavg=gavg, ws=gm_w_s, bias=bias, gog=gm_out_g[None, :],
                woa=woa, wog=wog, mog=mla_out_g[:, None], l1g=ln1_g[None, :], l1b=ln1_b[None, :])


def _moe(x1, w_rg, b_rg, w_re, b_re, w_gate, w_up, w_down):
    T, D = x1.shape
    pad = ROUTE_OUT - N_EXPERTS - N_GROUPS
    wr = jnp.concatenate([w_re.T, w_rg.T, jnp.zeros((pad, D), F32)], axis=0)
    br = jnp.concatenate([b_re, b_rg, jnp.zeros((pad,), F32)])[:, None]
    info, info_t, cnt = _route(x1, wr, br)

    bm = EXPERT_ROWS
    n_blocks = (T * TOP_K) // bm + N_EXPERTS
    counts = cnt[:, 0].astype(jnp.int32)
    padded = (counts + bm - 1) // bm * bm
    pad_ends = jnp.cumsum(padded)
    pad_starts = pad_ends - padded
    def dest_rows(e_lane, r_lane):
        e = info_t[:, e_lane, :].astype(jnp.int32)
        ids = jnp.arange(N_EXPERTS)[:, None, None]
        seg_start = jnp.sum(jnp.where(e[None] == ids, pad_starts[:, None, None], 0), axis=0)
        return ((seg_start + info_t[:, r_lane, :].astype(jnp.int32)) * TOKEN_ROWS).reshape(T // MOVE_ROWS, MOVE_ROWS)

    dest = jnp.concatenate([dest_rows(I_E0, I_R0), dest_rows(I_E1, I_R1)], axis=1)[:, None, :]
    block_start = jnp.arange(n_blocks, dtype=jnp.int32) * bm
    block_expert = jnp.minimum(jnp.sum(pad_ends[None, :] <= block_start[:, None], axis=1),
                               N_EXPERTS - 1).astype(jnp.int32)

    blk = jnp.arange(n_blocks)
    later = (blk[None, :] > blk[:, None]) & (block_expert[None, :] != block_expert[:, None])
    next_expert = jnp.min(jnp.where(later, block_expert[None, :], N_EXPERTS), axis=1)
    next_expert = jnp.where(next_expert == N_EXPERTS, -1, next_expert).astype(jnp.int32)
    n_used = (pad_ends[-1:] // bm).astype(jnp.int32)

    seg = jnp.stack([pad_ends, padded, jnp.broadcast_to(n_used, (N_EXPERTS,))]).astype(jnp.int32)
    buf = _dispatch(seg, dest, x1, n_blocks * bm)
    y = _experts(block_expert, next_expert, n_used, buf, w_gate, w_up, w_down)
    return info, dest, y


def kernel(x, p, positions, w_in, q_norm_g, w_q_up, kv_norm_g, w_kv_up, gm_ln_g, gm_ln_b, gm_w_s, gm_b_s, mla_out_g, gm_out_g, w_o, ln1_g, ln1_b, w_rg, b_rg, w_re, b_re, w_gate, w_up, w_down, ln2_g, ln2_b, w_pg, b_pg, w_pp, ln3_g, ln3_b):
    B, S, D = x.shape
    T = B * S
    assert S % PREP_ROWS == 0 and PREP_ROWS % ATTN_ROWS == 0 and PREP_ROWS % CHUNK == 0
    assert S % (ATTN_TILES * ATTN_ROWS) == 0 and ATTN_TILES % 2 == 0
    assert T % ROUTE_ROWS == 0 and T % MOVE_ROWS == 0 and (T * TOP_K) % EXPERT_ROWS == 0
    assert D == TOKEN_ROWS * LANES and MOVE_ROWS % MOVE_UNROLL == 0
    pos4 = positions.reshape(B, S // PREP_ROWS, 1, PREP_ROWS)
    for i in range(DEPTH):
        w = _layer_weights(w_in[i], q_norm_g[i], w_q_up[i], kv_norm_g[i], w_kv_up[i], gm_ln_g[i], gm_ln_b[i],
                           gm_w_s[i], gm_b_s[i], mla_out_g[i], gm_out_g[i], w_o[i], ln1_g[i], ln1_b[i])
        q, k, vt, g = _prep(x, pos4, w)
        x1 = _attn(q, k, vt, g, x, w).reshape(T, D)
        info, dest, y = _moe(x1, w_rg[i], b_rg[i], w_re[i], b_re[i], w_gate[i], w_up[i], w_down[i])
        wf = dict(wpg=w_pg[i].astype(BF16), bpg=b_pg[i][None, :], wpp=w_pp[i].astype(BF16),
                  l2g=ln2_g[i][None, :], l2b=ln2_b[i][None, :], l3g=ln3_g[i][None, :], l3b=ln3_b[i][None, :])
        x = _final(dest, x1, info, y, p[i].reshape(T, -1), wf).reshape(B, S, D)
    return x
```

```python
import functools

import jax
import jax.numpy as jnp
from jax import lax
from jax.experimental import pallas as pl
from jax.experimental.pallas import tpu as pltpu

F32 = jnp.float32
BF16 = jnp.bfloat16

MLA_HEADS = 8
QK_NOPE = 64
QK_ROPE = 32
V_HEAD = 64
Q_RANK = 256
KV_RANK = 128
ROPE_THETA = 10000.0
MLA_OUT = MLA_HEADS * V_HEAD
GM_GROUPS = 8
GM_CH = 64
GM_OUT = GM_GROUPS * GM_CH
CHUNK = 128
N_GROUPS = 4
EXP_PER_GROUP = 8
N_EXPERTS = N_GROUPS * EXP_PER_GROUP
TOP_K = 2
EPS = 1e-6
DEPTH = 1
ALPHA = (2.0 * DEPTH) ** 0.25
SM_SCALE = (QK_NOPE + QK_ROPE) ** -0.5
LOG2E = 1.4426950408889634
MASK_VALUE = -1e30

LANES = 128
SUBLANES = 8
TOKEN_ROWS = 8
ONES_ROWS = 16
VMEM_LIMIT = 56 * 1024 * 1024

PREP_ROWS = 512
ATTN_ROWS = 256
ATTN_TILES = 4
ROUTE_ROWS = 2048
RANK_CHUNK = 256
MOVE_ROWS = 256
MOVE_UNROLL = 8
EXPERT_ROWS = 256

C_Q = 0
C_KV = C_Q + Q_RANK
C_KR = C_KV + KV_RANK
C_U = C_KR + LANES
C_V = C_U + GM_OUT
C_END = C_V + GM_OUT
HP = MLA_HEADS * LANES

I_E0, I_E1, I_R0, I_R1, I_G0, I_G1 = range(6)
ROUTE_OUT = 48


def _rms(v, g):
    return v * lax.rsqrt(jnp.mean(v * v, axis=-1, keepdims=True) + EPS) * g


def _ln(v, g, b):
    mu = jnp.mean(v, axis=-1, keepdims=True)
    d = v - mu
    var = jnp.mean(d * d, axis=-1, keepdims=True)
    return d * lax.rsqrt(var + EPS) * g + b


def _dot(a, b):
    return jnp.dot(a, b, preferred_element_type=F32)


def _prep_kernel(x_ref, pos_ref, win_ref, qg_ref, wq_ref, kvg_ref, wk_ref, wv_ref, inv_ref, rope_ref, one_ref,
                 lng_ref, lnb_ref, gavg_ref, ws_ref, bias_ref, gog_ref,
                 q_ref, k_ref, vt_ref, g_ref):
    rows = x_ref.shape[1]
    h = _dot(x_ref[0].astype(BF16), win_ref[...])

    ang = inv_ref[...] * pos_ref[0, 0].astype(F32)
    parts = []
    for t in (jnp.cos(ang), jnp.sin(ang)):
        hi = t.astype(BF16).astype(F32)
        parts += [hi, t - hi]
    tabs = _dot(jnp.concatenate(parts, axis=0).T.astype(BF16), rope_ref[...])
    cos_t = tabs[:, :LANES] + one_ref[...]
    sin_a = tabs[:, LANES:2 * LANES]
    sin_b = tabs[:, 2 * LANES:]
    half = QK_ROPE // 2

    def rotate(v):
        return v * cos_t + pltpu.roll(v, LANES - half, 1) * sin_a + pltpu.roll(v, half, 1) * sin_b

    cq = _rms(h[:, C_Q:C_Q + Q_RANK], qg_ref[...]).astype(BF16)
    q2 = _dot(cq, wq_ref[...])
    for hd in range(MLA_HEADS):
        lo = hd * LANES
        q_ref[0, :, lo:lo + LANES] = (rotate(q2[:, lo:lo + LANES]) * (SM_SCALE * LOG2E)).astype(BF16)

    ckv = _rms(h[:, C_KV:C_KV + KV_RANK], kvg_ref[...]).astype(BF16)
    kp = _dot(ckv, wk_ref[...])
    kr = rotate(h[:, C_KR:C_KR + LANES])
    for hd in range(MLA_HEADS):
        lo = hd * LANES
        k_ref[0, :, lo:lo + LANES] = (kp[:, lo:lo + LANES] + kr).astype(BF16)
    vp = _dot(ckv, wv_ref[...])
    for kb in range(rows // ATTN_ROWS):
        vt_ref[0, kb] = vp[kb * ATTN_ROWS:(kb + 1) * ATTN_ROWS].T.astype(BF16)

    u = jax.nn.gelu(h[:, C_U:C_U + GM_OUT])
    vv = jax.nn.gelu(h[:, C_V:C_V + GM_OUT])
    mu = _dot(vv.astype(BF16), gavg_ref[...])
    d = vv - mu
    var = _dot((d * d).astype(BF16), gavg_ref[...])
    vn = (d * lax.rsqrt(var + EPS) * lng_ref[...] + lnb_ref[...]).astype(BF16)

    tri = lax.broadcasted_iota(jnp.int32, (CHUNK, CHUNK), 0) >= lax.broadcasted_iota(jnp.int32, (CHUNK, CHUNK), 1)
    wm = [jnp.where(tri, ws_ref[g], 0.0).astype(BF16) for g in range(GM_GROUPS)]
    low_half = lax.broadcasted_iota(jnp.int32, (CHUNK, LANES), 1) < GM_CH
    for c in range(rows // CHUNK):
        r0 = c * CHUNK
        parts = []
        for pr in range(GM_GROUPS // 2):
            tile = vn[r0:r0 + CHUNK, pr * LANES:(pr + 1) * LANES]
            parts.append(jnp.where(low_half, _dot(wm[2 * pr], tile), _dot(wm[2 * pr + 1], tile)))
        sg = jnp.concatenate(parts, axis=1) + bias_ref[...]
        gm = u[r0:r0 + CHUNK] * sg
        g_ref[0, r0:r0 + CHUNK, :] = _rms(gm, gog_ref[...]).astype(BF16)


def _prep(x, pos4, w):
    B, S, D = x.shape
    ts = PREP_ROWS
    full = lambda a: pl.BlockSpec(a.shape, lambda b, i: (0,) * a.ndim)
    consts = [w["win"], w["qg"], w["wq"], w["kvg"], w["wk"], w["wv"], w["inv"], w["rope"], w["one"],
              w["lng"], w["lnb"], w["gavg"], w["ws"], w["bias"], w["gog"]]
    return pl.pallas_call(
        _prep_kernel,
        grid=(B, S // ts),
        in_specs=[pl.BlockSpec((1, ts, D), lambda b, i: (b, i, 0)),
                  pl.BlockSpec((1, 1, 1, ts), lambda b, i: (b, i, 0, 0))] + [full(a) for a in consts],
        out_specs=[pl.BlockSpec((1, ts, HP), lambda b, i: (b, i, 0)),
                   pl.BlockSpec((1, ts, HP), lambda b, i: (b, i, 0)),
                   pl.BlockSpec((1, ts // ATTN_ROWS, MLA_OUT, ATTN_ROWS), lambda b, i: (b, i, 0, 0)),
                   pl.BlockSpec((1, ts, GM_OUT), lambda b, i: (b, i, 0))],
        out_shape=[jax.ShapeDtypeStruct((B, S, HP), BF16)] * 2
        + [jax.ShapeDtypeStruct((B, S // ATTN_ROWS, MLA_OUT, ATTN_ROWS), BF16),
           jax.ShapeDtypeStruct((B, S, GM_OUT), BF16)],
        compiler_params=pltpu.CompilerParams(dimension_semantics=("parallel", "parallel"),
                                             vmem_limit_bytes=VMEM_LIMIT),
        name="prep",
    )(x, pos4, *consts)


def _attn_kernel(q_ref, k_ref, vt_ref, g_ref, x_ref, woa_ref, wog_ref, mog_ref, l1g_ref, l1b_ref,
                 o_ref, m_scr, acc_scr, sa_scr, sb_scr):
    pid = pl.program_id(1)
    tq = ATTN_ROWS
    tk = tq
    key = lax.broadcasted_iota(jnp.int32, (tk, tq), 0)
    qry = lax.broadcasted_iota(jnp.int32, (tk, tq), 1)
    diag_mask = key <= qry
    ones = jnp.ones((ONES_ROWS, tk), BF16)

    def tile(t):
        r0 = t * tq
        i = ATTN_TILES * pid + t
        odd = t % 2 == 1
        m_scr[...] = jnp.full(m_scr.shape, MASK_VALUE, F32)
        acc_scr[...] = jnp.zeros(acc_scr.shape, F32)

        def scores(j, s_scr):
            k0 = pl.multiple_of(j * tk, tk)
            for hd in range(MLA_HEADS):
                lo = hd * LANES
                qh = q_ref[0, r0:r0 + tq, lo:lo + LANES]
                kj = k_ref[0, pl.ds(k0, tk), lo:lo + LANES]
                s_scr[hd] = lax.dot_general(kj, qh, (((1,), (1,)), ((), ())), preferred_element_type=F32)

        def update(j, s_scr, masked):
            for hd in range(MLA_HEADS):
                s = s_scr[hd]
                vt = vt_ref[0, j, hd * V_HEAD:(hd + 1) * V_HEAD, :]
                if masked:
                    s = jnp.where(diag_mask, s, MASK_VALUE)
                m_prev = m_scr[hd]
                m_new = jnp.maximum(m_prev, jnp.max(s, axis=0, keepdims=True))
                p = jnp.exp2(s - m_new).astype(BF16)
                scale = jnp.exp2(m_prev - m_new)
                acc_scr[hd] = scale * acc_scr[hd] + _dot(jnp.concatenate([vt, ones], axis=0), p)
                m_scr[hd] = m_new

        def pair(jj, c):
            j = 2 * jj
            scores(j + 1, sb_scr)
            update(j, sa_scr, False)
            scores(j + 2, sa_scr)
            update(j + 1, sb_scr, False)
            return c

        scores(0, sa_scr)
        lax.fori_loop(0, (ATTN_TILES // 2) * pid + t // 2, pair, 0)
        if odd:
            scores(i, sb_scr)
            update(i - 1, sa_scr, False)
            update(i, sb_scr, True)
        else:
            update(i, sa_scr, True)

        at = jnp.concatenate([acc_scr[hd, :V_HEAD] / acc_scr[hd, V_HEAD:V_HEAD + 1] for hd in range(MLA_HEADS)],
                             axis=0)
        at = at * lax.rsqrt(jnp.mean(at * at, axis=0, keepdims=True) + EPS) * mog_ref[...]
        mix = _dot(at.T.astype(BF16), woa_ref[...]) + _dot(g_ref[0, r0:r0 + tq, :], wog_ref[...])
        o_ref[0, r0:r0 + tq, :] = _ln(ALPHA * x_ref[0, r0:r0 + tq, :] + mix, l1g_ref[...], l1b_ref[...])

    for t in range(ATTN_TILES):
        tile(t)


def _attn(q, k, vt, g, x, w):
    B, S, D = x.shape
    tq = ATTN_ROWS
    rows = ATTN_TILES * tq
    full = lambda a: pl.BlockSpec(a.shape, lambda b, i: (0,) * a.ndim)
    consts = [w["woa"], w["wog"], w["mog"], w["l1g"], w["l1b"]]
    return pl.pallas_call(
        _attn_kernel,
        grid=(B, S // rows),
        in_specs=[pl.BlockSpec((1, rows, HP), lambda b, i: (b, i, 0)),
                  pl.BlockSpec((1, S, HP), lambda b, i: (b, 0, 0)),
                  pl.BlockSpec((1,) + vt.shape[1:], lambda b, i: (b, 0, 0, 0)),
                  pl.BlockSpec((1, rows, GM_OUT), lambda b, i: (b, i, 0)),
                  pl.BlockSpec((1, rows, D), lambda b, i: (b, i, 0))] + [full(a) for a in consts],
        out_specs=pl.BlockSpec((1, rows, D), lambda b, i: (b, i, 0)),
        out_shape=jax.ShapeDtypeStruct((B, S, D), F32),
        scratch_shapes=[pltpu.VMEM((MLA_HEADS, 1, tq), F32),
                        pltpu.VMEM((MLA_HEADS, V_HEAD + ONES_ROWS, tq), F32),
                        pltpu.VMEM((MLA_HEADS, tq, tq), F32), pltpu.VMEM((MLA_HEADS, tq, tq), F32)],
        compiler_params=pltpu.CompilerParams(dimension_semantics=("parallel", "parallel"),
                                             vmem_limit_bytes=VMEM_LIMIT),
        name="attn",
    )(q, k, vt, g, x, *consts)


def _route_kernel(x_ref, wr_ref, br_ref, info_ref, infot_ref, cnt_ref, carry_scr, tri_scr):
    step = pl.program_id(0)
    tt = x_ref.shape[0]

    @pl.when(step == 0)
    def _():
        carry_scr[...] = jnp.zeros_like(carry_scr)
        s = lax.broadcasted_iota(jnp.int32, tri_scr.shape, 0)
        t = lax.broadcasted_iota(jnp.int32, tri_scr.shape, 1)
        tri_scr[...] = jnp.where(s < t, 1.0, 0.0).astype(BF16)

    x = x_ref[...]
    xh = x.astype(BF16)
    xl = (x - xh.astype(F32)).astype(BF16)
    wr = wr_ref[...]
    wh = wr.astype(BF16)
    wl = (wr - wh.astype(F32)).astype(BF16)
    nt = (((1,), (1,)), ((), ()))
    by_xh = lax.dot_general(jnp.concatenate([wh, wl], axis=0), xh, nt, preferred_element_type=F32)
    logits = (by_xh[:ROUTE_OUT] + lax.dot_general(wh, xl, nt, preferred_element_type=F32)
              + by_xh[ROUTE_OUT:]) + br_ref[...]
    neg = jnp.float32(-jnp.inf)

    lg = logits[N_EXPERTS:N_EXPERTS + SUBLANES]
    grow = lax.broadcasted_iota(jnp.int32, lg.shape, 0)
    lg = jnp.where(grow < N_GROUPS, lg, neg)
    gmax = jnp.max(lg, axis=0, keepdims=True)
    g_idx = jnp.min(jnp.where(lg == gmax, grow, SUBLANES), axis=0, keepdims=True)
    g_p = 1.0 / jnp.sum(jnp.exp(lg - gmax), axis=0, keepdims=True)

    le = logits[:N_EXPERTS]
    row = lax.broadcasted_iota(jnp.int32, le.shape, 0)
    le = jnp.where((row >> 3) == g_idx, le, neg)
    m1 = jnp.max(le, axis=0, keepdims=True)
    i1 = jnp.min(jnp.where(le == m1, row, N_EXPERTS), axis=0, keepdims=True)
    le2 = jnp.where(row == i1, neg, le)
    m2 = jnp.max(le2, axis=0, keepdims=True)
    i2 = jnp.min(jnp.where(le2 == m2, row, N_EXPERTS), axis=0, keepdims=True)
    e2 = jnp.exp(m2 - m1)
    gate0 = g_p / (1.0 + e2)
    gate1 = g_p * e2 / (1.0 + e2)

    hit1 = row == i1
    hit2 = row == i2
    onehot = jnp.where(hit1 | hit2, 1.0, 0.0)
    rc = RANK_CHUNK
    chunks = [onehot[:, c * rc:(c + 1) * rc] for c in range(tt // rc)]
    inside = _dot(jnp.concatenate(chunks, axis=0).astype(BF16), tri_scr[...])
    seen = carry_scr[...]
    parts = []
    for c, chunk in enumerate(chunks):
        parts.append(inside[c * N_EXPERTS:(c + 1) * N_EXPERTS] + seen)
        seen = seen + jnp.sum(chunk, axis=1, keepdims=True)
    before = jnp.concatenate(parts, axis=1)
    rank0 = jnp.sum(jnp.where(hit1, before, 0.0), axis=0, keepdims=True)
    rank1 = jnp.sum(jnp.where(hit2, before, 0.0), axis=0, keepdims=True)
    carry_scr[...] = seen
    cnt_ref[...] = jnp.broadcast_to(seen, cnt_ref.shape)

    fields = jnp.concatenate([i1.astype(F32), i2.astype(F32), rank0, rank1, gate0, gate1,
                              jnp.zeros((SUBLANES - 6, tt), F32)], axis=0)
    infot_ref[0] = fields
    info_ref[...] = jnp.concatenate([fields, jnp.zeros((LANES - SUBLANES, tt), F32)], axis=0).T


def _route(x1, wr, br):
    T, D = x1.shape
    tt = ROUTE_ROWS
    return pl.pallas_call(
        _route_kernel,
        grid=(T // tt,),
        in_specs=[pl.BlockSpec((tt, D), lambda i: (i, 0)),
                  pl.BlockSpec(wr.shape, lambda i: (0, 0)),
                  pl.BlockSpec(br.shape, lambda i: (0, 0))],
        out_specs=[pl.BlockSpec((tt, LANES), lambda i: (i, 0)),
                   pl.BlockSpec((1, SUBLANES, tt), lambda i: (i, 0, 0)),
                   pl.BlockSpec((N_EXPERTS, LANES), lambda i: (0, 0))],
        out_shape=[jax.ShapeDtypeStruct((T, LANES), F32), jax.ShapeDtypeStruct((T // tt, SUBLANES, tt), F32),
                   jax.ShapeDtypeStruct((N_EXPERTS, LANES), F32)],
        scratch_shapes=[pltpu.VMEM((N_EXPERTS, 1), F32), pltpu.VMEM((RANK_CHUNK, RANK_CHUNK), BF16)],
        compiler_params=pltpu.CompilerParams(dimension_semantics=("arbitrary",), vmem_limit_bytes=VMEM_LIMIT),
        name="route",
    )(x1, wr, br)


def _to_token_tiles(dst_ref, val):
    dst_ref[...] = val.astype(BF16).reshape(dst_ref.shape)


def _from_token_tiles(src_ref, rows):
    return src_ref[...].reshape(rows, TOKEN_ROWS * LANES)


def _to_token_tiles_f32(dst_ref, val):
    rows = val.shape[0]
    for c in range(TOKEN_ROWS):
        dst_ref[pl.ds(c, rows, stride=TOKEN_ROWS), :] = val[:, c * LANES:(c + 1) * LANES]


def _from_token_tiles_f32(src_ref, rows):
    return jnp.concatenate([src_ref[pl.ds(c, rows, stride=TOKEN_ROWS), :] for c in range(TOKEN_ROWS)], axis=1)


def _tile_copy(src_ref, src_row, dst_ref, dst_row, sem):
    return pltpu.make_async_copy(src_ref.at[pl.ds(pl.multiple_of(src_row, TOKEN_ROWS), TOKEN_ROWS)],
                                 dst_ref.at[pl.ds(pl.multiple_of(dst_row, TOKEN_ROWS), TOKEN_ROWS)], sem)


def _dispatch_kernel(seg_ref, dest_ref, x0_ref, xn_ref, buf_ref, stage_scr, zero_scr, sem, zero_sem, *, n_steps):
    i = pl.program_id(0)
    rows = xn_ref.shape[0]
    cur = i % 3
    nxt = (i + 1) % 3

    @pl.when(i == 0)
    def _():
        zero_scr[...] = jnp.zeros(zero_scr.shape, BF16)

        block = EXPERT_ROWS * TOKEN_ROWS
        n_blocks = buf_ref.shape[0] // block

        def clear_rows(first):
            return pltpu.make_async_copy(zero_scr, buf_ref.at[pl.ds(pl.multiple_of(first, SUBLANES), block)], zero_sem)

        def clear(e):
            return clear_rows((seg_ref[0, e] - EXPERT_ROWS) * TOKEN_ROWS)

        def start_tail(b, c):
            clear_rows(b * block).start()
            return c

        def wait_tail(b, c):
            clear_rows(b * block).wait()
            return c

        for e in range(N_EXPERTS):
            pl.when(seg_ref[1, e] > 0)(lambda e=e: clear(e).start())
        lax.fori_loop(seg_ref[2, 0], n_blocks, start_tail, 0)
        for e in range(N_EXPERTS):
            pl.when(seg_ref[1, e] > 0)(lambda e=e: clear(e).wait())
        lax.fori_loop(seg_ref[2, 0], n_blocks, wait_tail, 0)

        _to_token_tiles(stage_scr.at[0], x0_ref[...])

    def drain(s):
        for _ in range(TOP_K):
            pltpu.make_async_copy(stage_scr.at[s], stage_scr.at[s], sem.at[s]).wait()

    @pl.when(i >= 2)
    def _():
        drain(nxt)

    _to_token_tiles(stage_scr.at[nxt], xn_ref[...])
    for r in range(rows):
        for kk in range(TOP_K):
            _tile_copy(stage_scr.at[cur], r * TOKEN_ROWS, buf_ref, dest_ref[0, 0, kk * rows + r],
                       sem.at[cur]).start(priority=kk)

    @pl.when(i == n_steps - 1)
    def _():
        drain(cur)
        if n_steps >= 2:
            drain((i + 2) % 3)


def _dispatch(seg, dest3, x1, n_rows):
    T, D = x1.shape
    td = MOVE_ROWS
    n_steps = T // td
    grid_spec = pltpu.PrefetchScalarGridSpec(
        num_scalar_prefetch=1,
        grid=(n_steps,),
        in_specs=[pl.BlockSpec((1, 1, TOP_K * td), lambda i, seg: (i, 0, 0), memory_space=pltpu.SMEM),
                  pl.BlockSpec((td, D), lambda i, seg: (0, 0)),
                  pl.BlockSpec((td, D), lambda i, seg: (jnp.minimum(i + 1, n_steps - 1), 0))],
        out_specs=pl.BlockSpec(memory_space=pl.ANY),
        scratch_shapes=[pltpu.VMEM((3, td * TOKEN_ROWS, LANES), BF16),
                        pltpu.VMEM((EXPERT_ROWS * TOKEN_ROWS, LANES), BF16),
                        pltpu.SemaphoreType.DMA((3,)), pltpu.SemaphoreType.DMA(())],
    )
    return pl.pallas_call(
        functools.partial(_dispatch_kernel, n_steps=n_steps),
        grid_spec=grid_spec,
        out_shape=jax.ShapeDtypeStruct((n_rows * TOKEN_ROWS, LANES), BF16),
        compiler_params=pltpu.CompilerParams(dimension_semantics=("arbitrary",), vmem_limit_bytes=VMEM_LIMIT),
        name="dispatch",
    )(seg, dest3, x1, x1)


def _expert_kernel(be_ref, ne_ref, nu_ref, buf0_ref, bufa_ref, bufb_ref, wg_hbm, wu_hbm, wd_hbm, y_ref,
                   sg_scr, su_scr, sd_scr, wg_scr, wu_scr, wd_scr, xa_scr, xb_scr, cur_ref, sem):
    step = pl.program_id(0)
    bm = EXPERT_ROWS
    half = bm * TOKEN_ROWS

    def fetch(expert, s):
        return (pltpu.make_async_copy(wg_hbm.at[expert], sg_scr.at[s], sem.at[s, 0]),
                pltpu.make_async_copy(wu_hbm.at[expert], su_scr.at[s], sem.at[s, 1]),
                pltpu.make_async_copy(wd_hbm.at[expert], sd_scr.at[s], sem.at[s, 2]))

    @pl.when(step == 0)
    def _():
        cur_ref[0] = 0
        for c in fetch(be_ref[0], 0):
            c.start()
        xa_scr[...] = _from_token_tiles(buf0_ref, bm)

    def load_weights(blk):
        e = be_ref[blk]

        @pl.when((blk == 0) | (be_ref[jnp.maximum(blk - 1, 0)] != e))
        def _():
            s = cur_ref[0]
            for c in fetch(e, s):
                c.wait()
            wg_scr[...] = sg_scr[s].astype(BF16)
            wu_scr[...] = su_scr[s].astype(BF16)
            wd_scr[...] = sd_scr[s].astype(BF16)
            nxt = ne_ref[blk]

            @pl.when(nxt >= 0)
            def _():
                for c in fetch(nxt, 1 - s):
                    c.start(priority=1)

            cur_ref[0] = 1 - s

    def compute(x_scr, nxt_ref, nxt_scr, out_rows):
        nxt_scr[...] = _from_token_tiles(nxt_ref, bm)
        xb = x_scr[...]
        hidden = jax.nn.silu(_dot(xb, wg_scr[...])) * _dot(xb, wu_scr[...])
        _to_token_tiles_f32(y_ref.at[out_rows], _dot(hidden.astype(BF16), wd_scr[...]))

    def run(blk, x_scr, nxt_ref, nxt_scr, out_rows):
        @pl.when(blk < nu_ref[0])
        def _():
            compute(x_scr, nxt_ref, nxt_scr, out_rows)

        @pl.when(blk >= nu_ref[0])
        def _():
            y_ref[out_rows, :] = jnp.zeros((half, LANES), F32)

    blk_a, blk_b = 2 * step, 2 * step + 1
    rows_a, rows_b = pl.ds(0, half), pl.ds(half, half)
    load_weights(blk_a)
    same = (be_ref[blk_a] == be_ref[blk_b]) & (blk_b < nu_ref[0])

    @pl.when(same)
    def _():
        compute(xa_scr, bufa_ref, xb_scr, rows_a)
        compute(xb_scr, bufb_ref, xa_scr, rows_b)

    @pl.when(jnp.logical_not(same))
    def _():
        run(blk_a, xa_scr, bufa_ref, xb_scr, rows_a)
        load_weights(blk_b)
        run(blk_b, xb_scr, bufb_ref, xa_scr, rows_b)


def _experts(block_expert, next_expert, n_used, buf, w_gate, w_up, w_down):
    bm = EXPERT_ROWS
    D, ff = w_gate.shape[1:]
    n_blocks = buf.shape[0] // (bm * TOKEN_ROWS)
    assert n_blocks % 2 == 0
    last = n_blocks - 1
    grid_spec = pltpu.PrefetchScalarGridSpec(
        num_scalar_prefetch=3,
        grid=(n_blocks // 2,),
        in_specs=[pl.BlockSpec((bm * TOKEN_ROWS, LANES), lambda s, *_: (0, 0)),
                  pl.BlockSpec((bm * TOKEN_ROWS, LANES), lambda s, *_: (2 * s + 1, 0)),
                  pl.BlockSpec((bm * TOKEN_ROWS, LANES), lambda s, *_: (jnp.minimum(2 * s + 2, last), 0)),
                  pl.BlockSpec(memory_space=pl.ANY),
                  pl.BlockSpec(memory_space=pl.ANY),
                  pl.BlockSpec(memory_space=pl.ANY)],
        out_specs=pl.BlockSpec((2 * bm * TOKEN_ROWS, LANES), lambda s, *_: (s, 0)),
        scratch_shapes=[pltpu.VMEM((2, D, ff), F32), pltpu.VMEM((2, D, ff), F32), pltpu.VMEM((2, ff, D), F32),
                        pltpu.VMEM((D, ff), BF16), pltpu.VMEM((D, ff), BF16), pltpu.VMEM((ff, D), BF16),
                        pltpu.VMEM((bm, D), BF16), pltpu.VMEM((bm, D), BF16),
                        pltpu.SMEM((1,), jnp.int32), pltpu.SemaphoreType.DMA((2, 3))],
    )
    return pl.pallas_call(
        _expert_kernel,
        grid_spec=grid_spec,
        out_shape=jax.ShapeDtypeStruct(buf.shape, F32),
        compiler_params=pltpu.CompilerParams(dimension_semantics=("arbitrary",), vmem_limit_bytes=VMEM_LIMIT),
        name="experts",
    )(block_expert, next_expert, n_used, buf, buf, buf, w_gate, w_up, w_down)


def _final_kernel(d0_ref, d1_ref, d2_ref, x_ref, info_ref, y_ref, p_ref, wpg_ref, bpg_ref, wpp_ref,
                  l2g_ref, l2b_ref, l3g_ref, l3b_ref, o_ref, rows_scr, sem):
    i = pl.program_id(0)
    last = pl.num_programs(0) - 1
    rows = x_ref.shape[0]
    slot = i % 3
    ahead = (i + 2) % 3

    def row_copy(dref, s, r, kk):
        return _tile_copy(y_ref, dref[0, 0, kk * rows + r], rows_scr.at[s, kk], r * TOKEN_ROWS, sem.at[s])

    def landed(s):
        pltpu.make_async_copy(rows_scr.at[s], rows_scr.at[s], sem.at[s]).wait()

    @pl.when(i == 0)
    def _():
        def start(c, carry):
            for u in range(MOVE_UNROLL):
                for kk in range(TOP_K):
                    row_copy(d0_ref, 0, c * MOVE_UNROLL + u, kk).start(priority=kk)
                    row_copy(d1_ref, 1, c * MOVE_UNROLL + u, kk).start(priority=kk)
            return carry

        lax.fori_loop(0, rows // MOVE_UNROLL, start, 0)

    landed(slot)
    info = info_ref[...]
    gate0 = info[:, I_G0:I_G0 + 1]
    gate1 = info[:, I_G1:I_G1 + 1]
    moe = (_from_token_tiles_f32(rows_scr.at[slot, 0], rows) * gate0
           + _from_token_tiles_f32(rows_scr.at[slot, 1], rows) * gate1)

    for r in range(rows):
        for kk in range(TOP_K):
            row_copy(d2_ref, ahead, r, kk).start(priority=kk)

    pp = _dot(p_ref[...].astype(BF16), wpp_ref[...])
    x2 = _ln(ALPHA * x_ref[...] + moe, l2g_ref[...], l2b_ref[...])
    gate = jax.nn.sigmoid(_dot(x2.astype(BF16), wpg_ref[...]) + bpg_ref[...])
    o_ref[...] = _ln(ALPHA * x2 + gate * pp, l3g_ref[...], l3b_ref[...])

    @pl.when(i == last)
    def _():
        landed((i + 1) % 3)
        landed(ahead)


def _final(dest3, x1, info, y, p2, w):
    T, D = x1.shape
    tc = MOVE_ROWS
    pd = p2.shape[1]
    full = lambda a: pl.BlockSpec(a.shape, lambda i: (0,) * a.ndim)
    consts = [w["wpg"], w["bpg"], w["wpp"], w["l2g"], w["l2b"], w["l3g"], w["l3b"]]
    last = T // tc - 1
    assert last >= 2
    return pl.pallas_call(
        _final_kernel,
        grid=(T // tc,),
        in_specs=[pl.BlockSpec((1, 1, TOP_K * tc), lambda i: (i, 0, 0), memory_space=pltpu.SMEM),
                  pl.BlockSpec((1, 1, TOP_K * tc), lambda i: (jnp.minimum(i + 1, last), 0, 0), memory_space=pltpu.SMEM),
                  pl.BlockSpec((1, 1, TOP_K * tc), lambda i: (jnp.minimum(i + 2, last), 0, 0), memory_space=pltpu.SMEM),
                  pl.BlockSpec((tc, D), lambda i: (i, 0)),
                  pl.BlockSpec((tc, LANES), lambda i: (i, 0)),
                  pl.BlockSpec(memory_space=pl.ANY),
                  pl.BlockSpec((tc, pd), lambda i: (i, 0))] + [full(a) for a in consts],
        out_specs=pl.BlockSpec((tc, D), lambda i: (i, 0)),
        out_shape=jax.ShapeDtypeStruct((T, D), F32),
        scratch_shapes=[pltpu.VMEM((3, TOP_K, tc * TOKEN_ROWS, LANES), F32), pltpu.SemaphoreType.DMA((3,))],
        compiler_params=pltpu.CompilerParams(dimension_semantics=("arbitrary",), vmem_limit_bytes=VMEM_LIMIT),
        name="final",
    )(dest3, dest3, dest3, x1, info, y, p2, *consts)


def _pad_heads(a, width):
    lead = a.shape[:-1]
    a = a.reshape(lead + (MLA_HEADS, width))
    a = jnp.pad(a, [(0, 0)] * len(lead) + [(0, 0), (0, LANES - width)])
    return a.reshape(lead + (HP,))


def _layer_weights(w_in, q_norm_g, w_q_up, kv_norm_g, w_kv_up, gm_ln_g, gm_ln_b, gm_w_s, gm_b_s,
                   mla_out_g, gm_out_g, w_o, ln1_g, ln1_b):
    D = w_in.shape[0]
    half = QK_ROPE // 2
    c1, c2, c3 = Q_RANK, Q_RANK + KV_RANK, Q_RANK + KV_RANK + QK_ROPE
    zeros = lambda *s: jnp.zeros(s, F32)
    kr = jnp.concatenate([zeros(D, QK_NOPE), w_in[:, c2:c3], zeros(D, LANES - QK_NOPE - QK_ROPE)], axis=1)
    win = jnp.concatenate([w_in[:, :c2], kr, w_in[:, c3:]], axis=1).astype(BF16)
    wq = _pad_heads(w_q_up, QK_NOPE + QK_ROPE).astype(BF16)

    wkv3 = w_kv_up.reshape(KV_RANK, MLA_HEADS, QK_NOPE + V_HEAD)
    wk = _pad_heads(wkv3[..., :QK_NOPE].reshape(KV_RANK, -1), QK_NOPE).astype(BF16)
    wv = wkv3[..., QK_NOPE:].reshape(KV_RANK, -1).astype(BF16)

    inv = (ROPE_THETA ** (-jnp.arange(0, QK_ROPE, 2, dtype=F32) / QK_ROPE))[:, None]
    eye = jnp.eye(half, dtype=F32)
    first = jnp.pad(eye, ((0, 0), (QK_NOPE, LANES - QK_NOPE - half)))
    second = jnp.pad(eye, ((0, 0), (QK_NOPE + half, LANES - QK_NOPE - QK_ROPE)))
    zero = jnp.zeros_like(first)
    cos_rows = jnp.concatenate([first + second, zero, zero], axis=1)
    sin_rows = jnp.concatenate([zero, -first, second], axis=1)
    rope = jnp.concatenate([cos_rows, cos_rows, sin_rows, sin_rows], axis=0).astype(BF16)
    lane = jnp.arange(LANES)
    one = jnp.where((lane >= QK_NOPE) & (lane < QK_NOPE + QK_ROPE), 0.0, 1.0)[None, :]

    grp = jnp.arange(GM_OUT) // GM_CH
    gavg = jnp.where(grp[:, None] == grp[None, :], 1.0 / GM_CH, 0.0).astype(BF16)
    bias = jnp.repeat(gm_b_s.T, GM_CH, axis=1)

    woa = w_o[:MLA_OUT].astype(BF16)
    wog = w_o[MLA_OUT:].astype(BF16)
    return dict(win=win, qg=q_norm_g[None, :], wq=wq, kvg=kv_norm_g[None, :], wk=wk, wv=wv, inv=inv, rope=rope, one=one,
                lng=gm_ln_g[None, :], lnb=gm_ln_b[None, :], gavg=gavg, ws=gm_w_s, bias=bias, gog=gm_out_g[None, :],
                woa=woa, wog=wog, mog=mla_out_g[:, None], l1g=ln1_g[None, :], l1b=ln1_b[None, :])


def _moe(x1, w_rg, b_rg, w_re, b_re, w_gate, w_up, w_down):
    T, D = x1.shape
    pad = ROUTE_OUT - N_EXPERTS - N_GROUPS
    wr = jnp.concatenate([w_re.T, w_rg.T, jnp.zeros((pad, D), F32)], axis=0)
    br = jnp.concatenate([b_re, b_rg, jnp.zeros((pad,), F32)])[:, None]
    info, info_t, cnt = _route(x1, wr, br)

    bm = EXPERT_ROWS
    n_blocks = (T * TOP_K) // bm + N_EXPERTS
    counts = cnt[:, 0].astype(jnp.int32)
    padded = (counts + bm - 1) // bm * bm
    pad_ends = jnp.cumsum(padded)
    pad_starts = pad_ends - padded
    def dest_rows(e_lane, r_lane):
        e = info_t[:, e_lane, :].astype(jnp.int32)
        ids = jnp.arange(N_EXPERTS)[:, None, None]
        seg_start = jnp.sum(jnp.where(e[None] == ids, pad_starts[:, None, None], 0), axis=0)
        return ((seg_start + info_t[:, r_lane, :].astype(jnp.int32)) * TOKEN_ROWS).reshape(T // MOVE_ROWS, MOVE_ROWS)

    dest = jnp.concatenate([dest_rows(I_E0, I_R0), dest_rows(I_E1, I_R1)], axis=1)[:, None, :]
    block_start = jnp.arange(n_blocks, dtype=jnp.int32) * bm
    block_expert = jnp.minimum(jnp.sum(pad_ends[None, :] <= block_start[:, None], axis=1),
                               N_EXPERTS - 1).astype(jnp.int32)

    blk = jnp.arange(n_blocks)
    later = (blk[None, :] > blk[:, None]) & (block_expert[None, :] != block_expert[:, None])
    next_expert = jnp.min(jnp.where(later, block_expert[None, :], N_EXPERTS), axis=1)
    next_expert = jnp.where(next_expert == N_EXPERTS, -1, next_expert).astype(jnp.int32)
    n_used = (pad_ends[-1:] // bm).astype(jnp.int32)

    seg = jnp.stack([pad_ends, padded, jnp.broadcast_to(n_used, (N_EXPERTS,))]).astype(jnp.int32)
    buf = _dispatch(seg, dest, x1, n_blocks * bm)
    y = _experts(block_expert, next_expert, n_used, buf, w_gate, w_up, w_down)
    return info, dest, y


def kernel(x, p, positions, w_in, q_norm_g, w_q_up, kv_norm_g, w_kv_up, gm_ln_g, gm_ln_b, gm_w_s, gm_b_s, mla_out_g, gm_out_g, w_o, ln1_g, ln1_b, w_rg, b_rg, w_re, b_re, w_gate, w_up, w_down, ln2_g, ln2_b, w_pg, b_pg, w_pp, ln3_g, ln3_b):
    B, S, D = x.shape
    T = B * S
    assert S % PREP_ROWS == 0 and PREP_ROWS % ATTN_ROWS == 0 and PREP_ROWS % CHUNK == 0
    assert S % (ATTN_TILES * ATTN_ROWS) == 0 and ATTN_TILES % 2 == 0
    assert T % ROUTE_ROWS == 0 and T % MOVE_ROWS == 0 and (T * TOP_K) % EXPERT_ROWS == 0
    assert D == TOKEN_ROWS * LANES and MOVE_ROWS % MOVE_UNROLL == 0
    pos4 = positions.reshape(B, S // PREP_ROWS, 1, PREP_ROWS)
    for i in range(DEPTH):
        w = _layer_weights(w_in[i], q_norm_g[i], w_q_up[i], kv_norm_g[i], w_kv_up[i], gm_ln_g[i], gm_ln_b[i],
                           gm_w_s[i], gm_b_s[i], mla_out_g[i], gm_out_g[i], w_o[i], ln1_g[i], ln1_b[i])
        q, k, vt, g = _prep(x, pos4, w)
        x1 = _attn(q, k, vt, g, x, w).reshape(T, D)
        info, dest, y = _moe(x1, w_rg[i], b_rg[i], w_re[i], b_re[i], w_gate[i], w_up[i], w_down[i])
        wf = dict(wpg=w_pg[i].astype(BF16), bpg=b_pg[i][None, :], wpp=w_pp[i].astype(BF16),
                  l2g=ln2_g[i][None, :], l2b=ln2_b[i][None, :], l3g=ln3_g[i][None, :], l3b=ln3_b[i][None, :])
        x = _final(dest, x1, info, y, p[i].reshape(T, -1), wf).reshape(B, S, D)
    return x
```

```python
import functools

import jax
import jax.numpy as jnp
from jax import lax
from jax.experimental import pallas as pl
from jax.experimental.pallas import tpu as pltpu

F32 = jnp.float32
BF16 = jnp.bfloat16

MLA_HEADS = 8
QK_NOPE = 64
QK_ROPE = 32
V_HEAD = 64
Q_RANK = 256
KV_RANK = 128
ROPE_THETA = 10000.0
MLA_OUT = MLA_HEADS * V_HEAD
GM_GROUPS = 8
GM_CH = 64
GM_OUT = GM_GROUPS * GM_CH
CHUNK = 128
N_GROUPS = 4
EXP_PER_GROUP = 8
N_EXPERTS = N_GROUPS * EXP_PER_GROUP
GROUP_SHIFT = EXP_PER_GROUP.bit_length() - 1
assert EXP_PER_GROUP == 1 << GROUP_SHIFT
TOP_K = 2
EPS = 1e-6
DEPTH = 1
ALPHA = (2.0 * DEPTH) ** 0.25
SM_SCALE = (QK_NOPE + QK_ROPE) ** -0.5
LOG2E = 1.4426950408889634
MASK_VALUE = -1e30

LANES = 128
SUBLANES = 8
TOKEN_ROWS = 8
ONES_ROWS = 16
VMEM_LIMIT = 56 * 1024 * 1024

PREP_ROWS = 512
ATTN_ROWS = 256
ATTN_TILES = 4
ROUTE_ROWS = 2048
RANK_CHUNK = 256
MOVE_ROWS = 256
MOVE_UNROLL = 8
EXPERT_ROWS = 256

C_Q = 0
C_KV = C_Q + Q_RANK
C_KR = C_KV + KV_RANK
C_U = C_KR + LANES
C_V = C_U + GM_OUT
C_END = C_V + GM_OUT
HP = MLA_HEADS * LANES

I_E0, I_E1, I_R0, I_R1, I_G0, I_G1 = range(6)
ROUTE_OUT = 48


def _rms(v, g):
    return v * lax.rsqrt(jnp.mean(v * v, axis=-1, keepdims=True) + EPS) * g


def _ln(v, g, b):
    mu = jnp.mean(v, axis=-1, keepdims=True)
    d = v - mu
    var = jnp.mean(d * d, axis=-1, keepdims=True)
    return d * lax.rsqrt(var + EPS) * g + b


def _dot(a, b):
    return jnp.dot(a, b, preferred_element_type=F32)


def _prep_kernel(x_ref, pos_ref, win_ref, qg_ref, wq_ref, kvg_ref, wk_ref, wv_ref, inv_ref, rope_ref, one_ref,
                 lng_ref, lnb_ref, gavg_ref, ws_ref, bias_ref, gog_ref,
                 q_ref, k_ref, vt_ref, g_ref):
    rows = x_ref.shape[1]
    h = _dot(x_ref[0].astype(BF16), win_ref[...])

    ang = inv_ref[...] * pos_ref[0, 0].astype(F32)
    parts = []
    for t in (jnp.cos(ang), jnp.sin(ang)):
        hi = t.astype(BF16).astype(F32)
        parts += [hi, t - hi]
    tabs = _dot(jnp.concatenate(parts, axis=0).T.astype(BF16), rope_ref[...])
    cos_t = tabs[:, :LANES] + one_ref[...]
    sin_a = tabs[:, LANES:2 * LANES]
    sin_b = tabs[:, 2 * LANES:]
    half = QK_ROPE // 2

    def rotate(v):
        return v * cos_t + pltpu.roll(v, LANES - half, 1) * sin_a + pltpu.roll(v, half, 1) * sin_b

    cq = _rms(h[:, C_Q:C_Q + Q_RANK], qg_ref[...]).astype(BF16)
    q2 = _dot(cq, wq_ref[...])
    for hd in range(MLA_HEADS):
        lo = hd * LANES
        q_ref[0, :, lo:lo + LANES] = (rotate(q2[:, lo:lo + LANES]) * (SM_SCALE * LOG2E)).astype(BF16)

    ckv = _rms(h[:, C_KV:C_KV + KV_RANK], kvg_ref[...]).astype(BF16)
    kp = _dot(ckv, wk_ref[...])
    kr = rotate(h[:, C_KR:C_KR + LANES])
    for hd in range(MLA_HEADS):
        lo = hd * LANES
        k_ref[0, :, lo:lo + LANES] = (kp[:, lo:lo + LANES] + kr).astype(BF16)
    vp = _dot(ckv, wv_ref[...])
    for kb in range(rows // ATTN_ROWS):
        vt_ref[0, kb] = vp[kb * ATTN_ROWS:(kb + 1) * ATTN_ROWS].T.astype(BF16)

    u = jax.nn.gelu(h[:, C_U:C_U + GM_OUT])
    vv = jax.nn.gelu(h[:, C_V:C_V + GM_OUT])
    mu = _dot(vv.astype(BF16), gavg_ref[...])
    d = vv - mu
    var = _dot((d * d).astype(BF16), gavg_ref[...])
    vn = (d * lax.rsqrt(var + EPS) * lng_ref[...] + lnb_ref[...]).astype(BF16)

    tri = lax.broadcasted_iota(jnp.int32, (CHUNK, CHUNK), 0) >= lax.broadcasted_iota(jnp.int32, (CHUNK, CHUNK), 1)
    wm = [jnp.where(tri, ws_ref[g], 0.0).astype(BF16) for g in range(GM_GROUPS)]
    low_half = lax.broadcasted_iota(jnp.int32, (CHUNK, LANES), 1) < GM_CH
    for c in range(rows // CHUNK):
        r0 = c * CHUNK
        parts = []
        for pr in range(GM_GROUPS // 2):
            tile = vn[r0:r0 + CHUNK, pr * LANES:(pr + 1) * LANES]
            parts.append(jnp.where(low_half, _dot(wm[2 * pr], tile), _dot(wm[2 * pr + 1], tile)))
        sg = jnp.concatenate(parts, axis=1) + bias_ref[...]
        gm = u[r0:r0 + CHUNK] * sg
        g_ref[0, r0:r0 + CHUNK, :] = _rms(gm, gog_ref[...]).astype(BF16)


def _prep(x, pos4, w):
    B, S, D = x.shape
    ts = PREP_ROWS
    full = lambda a: pl.BlockSpec(a.shape, lambda b, i: (0,) * a.ndim)
    consts = [w["win"], w["qg"], w["wq"], w["kvg"], w["wk"], w["wv"], w["inv"], w["rope"], w["one"],
              w["lng"], w["lnb"], w["gavg"], w["ws"], w["bias"], w["gog"]]
    return pl.pallas_call(
        _prep_kernel,
        grid=(B, S // ts),
        in_specs=[pl.BlockSpec((1, ts, D), lambda b, i: (b, i, 0)),
                  pl.BlockSpec((1, 1, 1, ts), lambda b, i: (b, i, 0, 0))] + [full(a) for a in consts],
        out_specs=[pl.BlockSpec((1, ts, HP), lambda b, i: (b, i, 0)),
                   pl.BlockSpec((1, ts, HP), lambda b, i: (b, i, 0)),
                   pl.BlockSpec((1, ts // ATTN_ROWS, MLA_OUT, ATTN_ROWS), lambda b, i: (b, i, 0, 0)),
                   pl.BlockSpec((1, ts, GM_OUT), lambda b, i: (b, i, 0))],
        out_shape=[jax.ShapeDtypeStruct((B, S, HP), BF16)] * 2
        + [jax.ShapeDtypeStruct((B, S // ATTN_ROWS, MLA_OUT, ATTN_ROWS), BF16),
           jax.ShapeDtypeStruct((B, S, GM_OUT), BF16)],
        compiler_params=pltpu.CompilerParams(dimension_semantics=("parallel", "parallel"),
                                             vmem_limit_bytes=VMEM_LIMIT),
        name="prep",
    )(x, pos4, *consts)


def _attn_kernel(q_ref, k_ref, vt_ref, g_ref, x_ref, woa_ref, wog_ref, mog_ref, l1g_ref, l1b_ref,
                 o_ref, m_scr, acc_scr, sa_scr, sb_scr):
    pid = pl.program_id(1)
    tq = ATTN_ROWS
    tk = tq
    key = lax.broadcasted_iota(jnp.int32, (tk, tq), 0)
    qry = lax.broadcasted_iota(jnp.int32, (tk, tq), 1)
    diag_mask = key <= qry
    ones = jnp.ones((ONES_ROWS, tk), BF16)

    def tile(t):
        r0 = t * tq
        i = ATTN_TILES * pid + t
        odd = t % 2 == 1
        m_scr[...] = jnp.full(m_scr.shape, MASK_VALUE, F32)
        acc_scr[...] = jnp.zeros(acc_scr.shape, F32)

        def scores(j, s_scr):
            k0 = pl.multiple_of(j * tk, tk)
            for hd in range(MLA_HEADS):
                lo = hd * LANES
                qh = q_ref[0, r0:r0 + tq, lo:lo + LANES]
                kj = k_ref[0, pl.ds(k0, tk), lo:lo + LANES]
                s_scr[hd] = lax.dot_general(kj, qh, (((1,), (1,)), ((), ())), preferred_element_type=F32)

        def update(j, s_scr, masked):
            for hd in range(MLA_HEADS):
                s = s_scr[hd]
                vt = vt_ref[0, j, hd * V_HEAD:(hd + 1) * V_HEAD, :]
                if masked:
                    s = jnp.where(diag_mask, s, MASK_VALUE)
                m_prev = m_scr[hd]
                m_new = jnp.maximum(m_prev, jnp.max(s, axis=0, keepdims=True))
                p = jnp.exp2(s - m_new).astype(BF16)
                scale = jnp.exp2(m_prev - m_new)
                acc_scr[hd] = scale * acc_scr[hd] + _dot(jnp.concatenate([vt, ones], axis=0), p)
                m_scr[hd] = m_new

        def pair(jj, c):
            j = 2 * jj
            scores(j + 1, sb_scr)
            update(j, sa_scr, False)
            scores(j + 2, sa_scr)
            update(j + 1, sb_scr, False)
            return c

        scores(0, sa_scr)
        lax.fori_loop(0, (ATTN_TILES // 2) * pid + t // 2, pair, 0)
        if odd:
            scores(i, sb_scr)
            update(i - 1, sa_scr, False)
            update(i, sb_scr, True)
        else:
            update(i, sa_scr, True)

        at = jnp.concatenate([acc_scr[hd, :V_HEAD] / acc_scr[hd, V_HEAD:V_HEAD + 1] for hd in range(MLA_HEADS)],
                             axis=0)
        at = at * lax.rsqrt(jnp.mean(at * at, axis=0, keepdims=True) + EPS) * mog_ref[...]
        mix = _dot(at.T.astype(BF16), woa_ref[...]) + _dot(g_ref[0, r0:r0 + tq, :], wog_ref[...])
        o_ref[0, r0:r0 + tq, :] = _ln(ALPHA * x_ref[0, r0:r0 + tq, :] + mix, l1g_ref[...], l1b_ref[...])

    for t in range(ATTN_TILES):
        tile(t)


def _attn(q, k, vt, g, x, w):
    B, S, D = x.shape
    tq = ATTN_ROWS
    rows = ATTN_TILES * tq
    full = lambda a: pl.BlockSpec(a.shape, lambda b, i: (0,) * a.ndim)
    consts = [w["woa"], w["wog"], w["mog"], w["l1g"], w["l1b"]]
    return pl.pallas_call(
        _attn_kernel,
        grid=(B, S // rows),
        in_specs=[pl.BlockSpec((1, rows, HP), lambda b, i: (b, i, 0)),
                  pl.BlockSpec((1, S, HP), lambda b, i: (b, 0, 0)),
                  pl.BlockSpec((1,) + vt.shape[1:], lambda b, i: (b, 0, 0, 0)),
                  pl.BlockSpec((1, rows, GM_OUT), lambda b, i: (b, i, 0)),
                  pl.BlockSpec((1, rows, D), lambda b, i: (b, i, 0))] + [full(a) for a in consts],
        out_specs=pl.BlockSpec((1, rows, D), lambda b, i: (b, i, 0)),
        out_shape=jax.ShapeDtypeStruct((B, S, D), F32),
        scratch_shapes=[pltpu.VMEM((MLA_HEADS, 1, tq), F32),
                        pltpu.VMEM((MLA_HEADS, V_HEAD + ONES_ROWS, tq), F32),
                        pltpu.VMEM((MLA_HEADS, tq, tq), F32), pltpu.VMEM((MLA_HEADS, tq, tq), F32)],
        compiler_params=pltpu.CompilerParams(dimension_semantics=("parallel", "parallel"),
                                             vmem_limit_bytes=VMEM_LIMIT),
        name="attn",
    )(q, k, vt, g, x, *consts)


def _route_kernel(x_ref, wr_ref, br_ref, info_ref, infot_ref, cnt_ref, carry_scr, tri_scr):
    step = pl.program_id(0)
    tt = x_ref.shape[0]

    @pl.when(step == 0)
    def _():
        carry_scr[...] = jnp.zeros_like(carry_scr)
        s = lax.broadcasted_iota(jnp.int32, tri_scr.shape, 0)
        t = lax.broadcasted_iota(jnp.int32, tri_scr.shape, 1)
        tri_scr[...] = jnp.where(s < t, 1.0, 0.0).astype(BF16)

    x = x_ref[...]
    xh = x.astype(BF16)
    xl = (x - xh.astype(F32)).astype(BF16)
    wr = wr_ref[...]
    wh = wr.astype(BF16)
    wl = (wr - wh.astype(F32)).astype(BF16)
    nt = (((1,), (1,)), ((), ()))
    by_xh = lax.dot_general(jnp.concatenate([wh, wl], axis=0), xh, nt, preferred_element_type=F32)
    logits = (by_xh[:ROUTE_OUT] + lax.dot_general(wh, xl, nt, preferred_element_type=F32)
              + by_xh[ROUTE_OUT:]) + br_ref[...]
    neg = jnp.float32(-jnp.inf)

    lg = logits[N_EXPERTS:N_EXPERTS + SUBLANES]
    grow = lax.broadcasted_iota(jnp.int32, lg.shape, 0)
    lg = jnp.where(grow < N_GROUPS, lg, neg)
    gmax = jnp.max(lg, axis=0, keepdims=True)
    g_idx = jnp.min(jnp.where(lg == gmax, grow, SUBLANES), axis=0, keepdims=True)
    g_p = 1.0 / jnp.sum(jnp.exp(lg - gmax), axis=0, keepdims=True)

    le = logits[:N_EXPERTS]
    row = lax.broadcasted_iota(jnp.int32, le.shape, 0)
    le = jnp.where((row >> GROUP_SHIFT) == g_idx, le, neg)
    m1 = jnp.max(le, axis=0, keepdims=True)
    i1 = jnp.min(jnp.where(le == m1, row, N_EXPERTS), axis=0, keepdims=True)
    le2 = jnp.where(row == i1, neg, le)
    m2 = jnp.max(le2, axis=0, keepdims=True)
    i2 = jnp.min(jnp.where(le2 == m2, row, N_EXPERTS), axis=0, keepdims=True)
    e2 = jnp.exp(m2 - m1)
    gate0 = g_p / (1.0 + e2)
    gate1 = g_p * e2 / (1.0 + e2)

    hit1 = row == i1
    hit2 = row == i2
    onehot = jnp.where(hit1 | hit2, 1.0, 0.0)
    rc = RANK_CHUNK
    chunks = [onehot[:, c * rc:(c + 1) * rc] for c in range(tt // rc)]
    inside = _dot(jnp.concatenate(chunks, axis=0).astype(BF16), tri_scr[...])
    seen = carry_scr[...]
    parts = []
    for c, chunk in enumerate(chunks):
        parts.append(inside[c * N_EXPERTS:(c + 1) * N_EXPERTS] + seen)
        seen = seen + jnp.sum(chunk, axis=1, keepdims=True)
    before = jnp.concatenate(parts, axis=1)
    rank0 = jnp.sum(jnp.where(hit1, before, 0.0), axis=0, keepdims=True)
    rank1 = jnp.sum(jnp.where(hit2, before, 0.0), axis=0, keepdims=True)
    carry_scr[...] = seen
    cnt_ref[...] = jnp.broadcast_to(seen, cnt_ref.shape)

    fields = jnp.concatenate([i1.astype(F32), i2.astype(F32), rank0, rank1, gate0, gate1,
                              jnp.zeros((SUBLANES - 6, tt), F32)], axis=0)
    infot_ref[0] = fields
    info_ref[...] = jnp.concatenate([fields, jnp.zeros((LANES - SUBLANES, tt), F32)], axis=0).T


def _route(x1, wr, br):
    T, D = x1.shape
    tt = ROUTE_ROWS
    return pl.pallas_call(
        _route_kernel,
        grid=(T // tt,),
        in_specs=[pl.BlockSpec((tt, D), lambda i: (i, 0)),
                  pl.BlockSpec(wr.shape, lambda i: (0, 0)),
                  pl.BlockSpec(br.shape, lambda i: (0, 0))],
        out_specs=[pl.BlockSpec((tt, LANES), lambda i: (i, 0)),
                   pl.BlockSpec((1, SUBLANES, tt), lambda i: (i, 0, 0)),
                   pl.BlockSpec((N_EXPERTS, LANES), lambda i: (0, 0))],
        out_shape=[jax.ShapeDtypeStruct((T, LANES), F32), jax.ShapeDtypeStruct((T // tt, SUBLANES, tt), F32),
                   jax.ShapeDtypeStruct((N_EXPERTS, LANES), F32)],
        scratch_shapes=[pltpu.VMEM((N_EXPERTS, 1), F32), pltpu.VMEM((RANK_CHUNK, RANK_CHUNK), BF16)],
        compiler_params=pltpu.CompilerParams(dimension_semantics=("arbitrary",), vmem_limit_bytes=VMEM_LIMIT),
        name="route",
    )(x1, wr, br)


def _to_token_tiles(dst_ref, val):
    dst_ref[...] = val.astype(BF16).reshape(dst_ref.shape)


def _from_token_tiles(src_ref, rows):
    return src_ref[...].reshape(rows, TOKEN_ROWS * LANES)


def _to_token_tiles_f32(dst_ref, val):
    rows = val.shape[0]
    for c in range(TOKEN_ROWS):
        dst_ref[pl.ds(c, rows, stride=TOKEN_ROWS), :] = val[:, c * LANES:(c + 1) * LANES]


def _from_token_tiles_f32(src_ref, rows):
    return jnp.concatenate([src_ref[pl.ds(c, rows, stride=TOKEN_ROWS), :] for c in range(TOKEN_ROWS)], axis=1)


def _tile_copy(src_ref, src_row, dst_ref, dst_row, sem):
    return pltpu.make_async_copy(src_ref.at[pl.ds(pl.multiple_of(src_row, TOKEN_ROWS), TOKEN_ROWS)],
                                 dst_ref.at[pl.ds(pl.multiple_of(dst_row, TOKEN_ROWS), TOKEN_ROWS)], sem)


def _dispatch_kernel(seg_ref, dest_ref, x0_ref, xn_ref, buf_ref, stage_scr, zero_scr, sem, zero_sem, *, n_steps):
    i = pl.program_id(0)
    rows = xn_ref.shape[0]
    cur = i % 3
    nxt = (i + 1) % 3

    @pl.when(i == 0)
    def _():
        zero_scr[...] = jnp.zeros(zero_scr.shape, BF16)

        block = EXPERT_ROWS * TOKEN_ROWS
        n_blocks = buf_ref.shape[0] // block

        def clear_rows(first):
            return pltpu.make_async_copy(zero_scr, buf_ref.at[pl.ds(pl.multiple_of(first, SUBLANES), block)], zero_sem)

        def clear(e):
            return clear_rows((seg_ref[0, e] - EXPERT_ROWS) * TOKEN_ROWS)

        def start_tail(b, c):
            clear_rows(b * block).start()
            return c

        def wait_tail(b, c):
            clear_rows(b * block).wait()
            return c

        for e in range(N_EXPERTS):
            pl.when(seg_ref[1, e] > 0)(lambda e=e: clear(e).start())
        lax.fori_loop(seg_ref[2, 0], n_blocks, start_tail, 0)
        for e in range(N_EXPERTS):
            pl.when(seg_ref[1, e] > 0)(lambda e=e: clear(e).wait())
        lax.fori_loop(seg_ref[2, 0], n_blocks, wait_tail, 0)

        _to_token_tiles(stage_scr.at[0], x0_ref[...])

    def drain(s):
        for _ in range(TOP_K):
            pltpu.make_async_copy(stage_scr.at[s], stage_scr.at[s], sem.at[s]).wait()

    @pl.when(i >= 2)
    def _():
        drain(nxt)

    _to_token_tiles(stage_scr.at[nxt], xn_ref[...])
    for r in range(rows):
        for kk in range(TOP_K):
            _tile_copy(stage_scr.at[cur], r * TOKEN_ROWS, buf_ref, dest_ref[0, 0, kk * rows + r],
                       sem.at[cur]).start(priority=kk)

    @pl.when(i == n_steps - 1)
    def _():
        drain(cur)
        if n_steps >= 2:
            drain((i + 2) % 3)


def _dispatch(seg, dest3, x1, n_rows):
    T, D = x1.shape
    td = MOVE_ROWS
    n_steps = T // td
    grid_spec = pltpu.PrefetchScalarGridSpec(
        num_scalar_prefetch=1,
        grid=(n_steps,),
        in_specs=[pl.BlockSpec((1, 1, TOP_K * td), lambda i, seg: (i, 0, 0), memory_space=pltpu.SMEM),
                  pl.BlockSpec((td, D), lambda i, seg: (0, 0)),
                  pl.BlockSpec((td, D), lambda i, seg: (jnp.minimum(i + 1, n_steps - 1), 0))],
        out_specs=pl.BlockSpec(memory_space=pl.ANY),
        scratch_shapes=[pltpu.VMEM((3, td * TOKEN_ROWS, LANES), BF16),
                        pltpu.VMEM((EXPERT_ROWS * TOKEN_ROWS, LANES), BF16),
                        pltpu.SemaphoreType.DMA((3,)), pltpu.SemaphoreType.DMA(())],
    )
    return pl.pallas_call(
        functools.partial(_dispatch_kernel, n_steps=n_steps),
        grid_spec=grid_spec,
        out_shape=jax.ShapeDtypeStruct((n_rows * TOKEN_ROWS, LANES), BF16),
        compiler_params=pltpu.CompilerParams(dimension_semantics=("arbitrary",), vmem_limit_bytes=VMEM_LIMIT),
        name="dispatch",
    )(seg, dest3, x1, x1)


def _expert_kernel(be_ref, ne_ref, nu_ref, buf0_ref, bufa_ref, bufb_ref, wg_hbm, wu_hbm, wd_hbm, y_ref,
                   sg_scr, su_scr, sd_scr, wg_scr, wu_scr, wd_scr, xa_scr, xb_scr, cur_ref, sem):
    step = pl.program_id(0)
    bm = EXPERT_ROWS
    half = bm * TOKEN_ROWS

    def fetch(expert, s):
        return (pltpu.make_async_copy(wg_hbm.at[expert], sg_scr.at[s], sem.at[s, 0]),
                pltpu.make_async_copy(wu_hbm.at[expert], su_scr.at[s], sem.at[s, 1]),
                pltpu.make_async_copy(wd_hbm.at[expert], sd_scr.at[s], sem.at[s, 2]))

    @pl.when(step == 0)
    def _():
        cur_ref[0] = 0
        for c in fetch(be_ref[0], 0):
            c.start()
        xa_scr[...] = _from_token_tiles(buf0_ref, bm)

    def load_weights(blk):
        e = be_ref[blk]

        @pl.when((blk == 0) | (be_ref[jnp.maximum(blk - 1, 0)] != e))
        def _():
            s = cur_ref[0]
            for c in fetch(e, s):
                c.wait()
            wg_scr[...] = sg_scr[s].astype(BF16)
            wu_scr[...] = su_scr[s].astype(BF16)
            wd_scr[...] = sd_scr[s].astype(BF16)
            nxt = ne_ref[blk]

            @pl.when(nxt >= 0)
            def _():
                for c in fetch(nxt, 1 - s):
                    c.start()

            cur_ref[0] = 1 - s

    def compute(x_scr, nxt_ref, nxt_scr, out_rows):
        nxt_scr[...] = _from_token_tiles(nxt_ref, bm)
        xb = x_scr[...]
        hidden = jax.nn.silu(_dot(xb, wg_scr[...])) * _dot(xb, wu_scr[...])
        _to_token_tiles_f32(y_ref.at[out_rows], _dot(hidden.astype(BF16), wd_scr[...]))

    def run(blk, x_scr, nxt_ref, nxt_scr, out_rows):
        @pl.when(blk < nu_ref[0])
        def _():
            compute(x_scr, nxt_ref, nxt_scr, out_rows)

        @pl.when(blk >= nu_ref[0])
        def _():
            y_ref[out_rows, :] = jnp.zeros((half, LANES), F32)

    blk_a, blk_b = 2 * step, 2 * step + 1
    rows_a, rows_b = pl.ds(0, half), pl.ds(half, half)
    load_weights(blk_a)
    same = (be_ref[blk_a] == be_ref[blk_b]) & (blk_b < nu_ref[0])

    @pl.when(same)
    def _():
        compute(xa_scr, bufa_ref, xb_scr, rows_a)
        compute(xb_scr, bufb_ref, xa_scr, rows_b)

    @pl.when(jnp.logical_not(same))
    def _():
        run(blk_a, xa_scr, bufa_ref, xb_scr, rows_a)
        load_weights(blk_b)
        run(blk_b, xb_scr, bufb_ref, xa_scr, rows_b)


def _experts(block_expert, next_expert, n_used, buf, w_gate, w_up, w_down):
    bm = EXPERT_ROWS
    D, ff = w_gate.shape[1:]
    n_blocks = buf.shape[0] // (bm * TOKEN_ROWS)
    assert n_blocks % 2 == 0
    last = n_blocks - 1
    grid_spec = pltpu.PrefetchScalarGridSpec(
        num_scalar_prefetch=3,
        grid=(n_blocks // 2,),
        in_specs=[pl.BlockSpec((bm * TOKEN_ROWS, LANES), lambda s, *_: (0, 0)),
                  pl.BlockSpec((bm * TOKEN_ROWS, LANES), lambda s, *_: (2 * s + 1, 0)),
                  pl.BlockSpec((bm * TOKEN_ROWS, LANES), lambda s, *_: (jnp.minimum(2 * s + 2, last), 0)),
                  pl.BlockSpec(memory_space=pl.ANY),
                  pl.BlockSpec(memory_space=pl.ANY),
                  pl.BlockSpec(memory_space=pl.ANY)],
        out_specs=pl.BlockSpec((2 * bm * TOKEN_ROWS, LANES), lambda s, *_: (s, 0)),
        scratch_shapes=[pltpu.VMEM((2, D, ff), F32), pltpu.VMEM((2, D, ff), F32), pltpu.VMEM((2, ff, D), F32),
                        pltpu.VMEM((D, ff), BF16), pltpu.VMEM((D, ff), BF16), pltpu.VMEM((ff, D), BF16),
                        pltpu.VMEM((bm, D), BF16), pltpu.VMEM((bm, D), BF16),
                        pltpu.SMEM((1,), jnp.int32), pltpu.SemaphoreType.DMA((2, 3))],
    )
    return pl.pallas_call(
        _expert_kernel,
        grid_spec=grid_spec,
        out_shape=jax.ShapeDtypeStruct(buf.shape, F32),
        compiler_params=pltpu.CompilerParams(dimension_semantics=("arbitrary",), vmem_limit_bytes=VMEM_LIMIT),
        name="experts",
    )(block_expert, next_expert, n_used, buf, buf, buf, w_gate, w_up, w_down)


def _final_kernel(d0_ref, d1_ref, d2_ref, x_ref, info_ref, y_ref, p_ref, wpg_ref, bpg_ref, wpp_ref,
                  l2g_ref, l2b_ref, l3g_ref, l3b_ref, o_ref, rows_scr, sem):
    i = pl.program_id(0)
    last = pl.num_programs(0) - 1
    rows = x_ref.shape[0]
    slot = i % 3
    ahead = (i + 2) % 3

    def row_copy(dref, s, r, kk):
        return _tile_copy(y_ref, dref[0, 0, kk * rows + r], rows_scr.at[s, kk], r * TOKEN_ROWS, sem.at[s])

    def landed(s):
        pltpu.make_async_copy(rows_scr.at[s], rows_scr.at[s], sem.at[s]).wait()

    @pl.when(i == 0)
    def _():
        def start(c, carry):
            for u in range(MOVE_UNROLL):
                for kk in range(TOP_K):
                    row_copy(d0_ref, 0, c * MOVE_UNROLL + u, kk).start(priority=kk)
                    row_copy(d1_ref, 1, c * MOVE_UNROLL + u, kk).start(priority=kk)
            return carry

        lax.fori_loop(0, rows // MOVE_UNROLL, start, 0)

    landed(slot)
    info = info_ref[...]
    gate0 = info[:, I_G0:I_G0 + 1]
    gate1 = info[:, I_G1:I_G1 + 1]
    moe = (_from_token_tiles_f32(rows_scr.at[slot, 0], rows) * gate0
           + _from_token_tiles_f32(rows_scr.at[slot, 1], rows) * gate1)

    for r in range(rows):
        for kk in range(TOP_K):
            row_copy(d2_ref, ahead, r, kk).start(priority=kk)

    pp = _dot(p_ref[...].astype(BF16), wpp_ref[...])
    x2 = _ln(ALPHA * x_ref[...] + moe, l2g_ref[...], l2b_ref[...])
    gate = jax.nn.sigmoid(_dot(x2.astype(BF16), wpg_ref[...]) + bpg_ref[...])
    o_ref[...] = _ln(ALPHA * x2 + gate * pp, l3g_ref[...], l3b_ref[...])

    @pl.when(i == last)
    def _():
        landed((i + 1) % 3)
        landed(ahead)


def _final(dest3, x1, info, y, p2, w):
    T, D = x1.shape
    tc = MOVE_ROWS
    pd = p2.shape[1]
    full = lambda a: pl.BlockSpec(a.shape, lambda i: (0,) * a.ndim)
    consts = [w["wpg"], w["bpg"], w["wpp"], w["l2g"], w["l2b"], w["l3g"], w["l3b"]]
    last = T // tc - 1
    assert last >= 2
    return pl.pallas_call(
        _final_kernel,
        grid=(T // tc,),
        in_specs=[pl.BlockSpec((1, 1, TOP_K * tc), lambda i: (i, 0, 0), memory_space=pltpu.SMEM),
                  pl.BlockSpec((1, 1, TOP_K * tc), lambda i: (jnp.minimum(i + 1, last), 0, 0), memory_space=pltpu.SMEM),
                  pl.BlockSpec((1, 1, TOP_K * tc), lambda i: (jnp.minimum(i + 2, last), 0, 0), memory_space=pltpu.SMEM),
                  pl.BlockSpec((tc, D), lambda i: (i, 0)),
                  pl.BlockSpec((tc, LANES), lambda i: (i, 0)),
                  pl.BlockSpec(memory_space=pl.ANY),
                  pl.BlockSpec((tc, pd), lambda i: (i, 0))] + [full(a) for a in consts],
        out_specs=pl.BlockSpec((tc, D), lambda i: (i, 0)),
        out_shape=jax.ShapeDtypeStruct((T, D), F32),
        scratch_shapes=[pltpu.VMEM((3, TOP_K, tc * TOKEN_ROWS, LANES), F32), pltpu.SemaphoreType.DMA((3,))],
        compiler_params=pltpu.CompilerParams(dimension_semantics=("arbitrary",), vmem_limit_bytes=VMEM_LIMIT),
        name="final",
    )(dest3, dest3, dest3, x1, info, y, p2, *consts)


def _pad_heads(a, width):
    lead = a.shape[:-1]
    a = a.reshape(lead + (MLA_HEADS, width))
    a = jnp.pad(a, [(0, 0)] * len(lead) + [(0, 0), (0, LANES - width)])
    return a.reshape(lead + (HP,))


def _layer_weights(w_in, q_norm_g, w_q_up, kv_norm_g, w_kv_up, gm_ln_g, gm_ln_b, gm_w_s, gm_b_s,
                   mla_out_g, gm_out_g, w_o, ln1_g, ln1_b):
    D = w_in.shape[0]
    half = QK_ROPE // 2
    c1, c2, c3 = Q_RANK, Q_RANK + KV_RANK, Q_RANK + KV_RANK + QK_ROPE
    zeros = lambda *s: jnp.zeros(s, F32)
    kr = jnp.concatenate([zeros(D, QK_NOPE), w_in[:, c2:c3], zeros(D, LANES - QK_NOPE - QK_ROPE)], axis=1)
    win = jnp.concatenate([w_in[:, :c2], kr, w_in[:, c3:]], axis=1).astype(BF16)
    wq = _pad_heads(w_q_up, QK_NOPE + QK_ROPE).astype(BF16)

    wkv3 = w_kv_up.reshape(KV_RANK, MLA_HEADS, QK_NOPE + V_HEAD)
    wk = _pad_heads(wkv3[..., :QK_NOPE].reshape(KV_RANK, -1), QK_NOPE).astype(BF16)
    wv = wkv3[..., QK_NOPE:].reshape(KV_RANK, -1).astype(BF16)

    inv = (ROPE_THETA ** (-jnp.arange(0, QK_ROPE, 2, dtype=F32) / QK_ROPE))[:, None]
    eye = jnp.eye(half, dtype=F32)
    first = jnp.pad(eye, ((0, 0), (QK_NOPE, LANES - QK_NOPE - half)))
    second = jnp.pad(eye, ((0, 0), (QK_NOPE + half, LANES - QK_NOPE - QK_ROPE)))
    zero = jnp.zeros_like(first)
    cos_rows = jnp.concatenate([first + second, zero, zero], axis=1)
    sin_rows = jnp.concatenate([zero, -first, second], axis=1)
    rope = jnp.concatenate([cos_rows, cos_rows, sin_rows, sin_rows], axis=0).astype(BF16)
    lane = jnp.arange(LANES)
    one = jnp.where((lane >= QK_NOPE) & (lane < QK_NOPE + QK_ROPE), 0.0, 1.0)[None, :]

    grp = jnp.arange(GM_OUT) // GM_CH
    gavg = jnp.where(grp[:, None] == grp[None, :], 1.0 / GM_CH, 0.0).astype(BF16)
    bias = jnp.repeat(gm_b_s.T, GM_CH, axis=1)

    woa = w_o[:MLA_OUT].astype(BF16)
    wog = w_o[MLA_OUT:].astype(BF16)
    return dict(win=win, qg=q_norm_g[None, :], wq=wq, kvg=kv_norm_g[None, :], wk=wk, wv=wv, inv=inv, rope=rope, one=one,
                lng=gm_ln_g[None, :], lnb=gm_ln_b[None, :], gavg=gavg, ws=gm_w_s, bias=bias, gog=gm_out_g[None, :],
                woa=woa, wog=wog, mog=mla_out_g[:, None], l1g=ln1_g[None, :], l1b=ln1_b[None, :])


def _moe(x1, w_rg, b_rg, w_re, b_re, w_gate, w_up, w_down):
    T, D = x1.shape
    pad = ROUTE_OUT - N_EXPERTS - N_GROUPS
    wr = jnp.concatenate([w_re.T, w_rg.T, jnp.zeros((pad, D), F32)], axis=0)
    br = jnp.concatenate([b_re, b_rg, jnp.zeros((pad,), F32)])[:, None]
    info, info_t, cnt = _route(x1, wr, br)

    bm = EXPERT_ROWS
    n_blocks = (T * TOP_K) // bm + N_EXPERTS
    counts = cnt[:, 0].astype(jnp.int32)
    padded = (counts + bm - 1) // bm * bm
    pad_ends = jnp.cumsum(padded)
    pad_starts = pad_ends - padded
    def dest_rows(e_lane, r_lane):
        e = info_t[:, e_lane, :].astype(jnp.int32)
        ids = jnp.arange(N_EXPERTS)[:, None, None]
        seg_start = jnp.sum(jnp.where(e[None] == ids, pad_starts[:, None, None], 0), axis=0)
        return ((seg_start + info_t[:, r_lane, :].astype(jnp.int32)) * TOKEN_ROWS).reshape(T // MOVE_ROWS, MOVE_ROWS)

    dest = jnp.concatenate([dest_rows(I_E0, I_R0), dest_rows(I_E1, I_R1)], axis=1)[:, None, :]
    block_start = jnp.arange(n_blocks, dtype=jnp.int32) * bm
    block_expert = jnp.minimum(jnp.sum(pad_ends[None, :] <= block_start[:, None], axis=1),
                               N_EXPERTS - 1).astype(jnp.int32)

    blk = jnp.arange(n_blocks)
    later = (blk[None, :] > blk[:, None]) & (block_expert[None, :] != block_expert[:, None])
    next_expert = jnp.min(jnp.where(later, block_expert[None, :], N_EXPERTS), axis=1)
    next_expert = jnp.where(next_expert == N_EXPERTS, -1, next_expert).astype(jnp.int32)
    n_used = (pad_ends[-1:] // bm).astype(jnp.int32)

    seg = jnp.stack([pad_ends, padded, jnp.broadcast_to(n_used, (N_EXPERTS,))]).astype(jnp.int32)
    buf = _dispatch(seg, dest, x1, n_blocks * bm)
    y = _experts(block_expert, next_expert, n_used, buf, w_gate, w_up, w_down)
    return info, dest, y


def kernel(x, p, positions, w_in, q_norm_g, w_q_up, kv_norm_g, w_kv_up, gm_ln_g, gm_ln_b, gm_w_s, gm_b_s, mla_out_g, gm_out_g, w_o, ln1_g, ln1_b, w_rg, b_rg, w_re, b_re, w_gate, w_up, w_down, ln2_g, ln2_b, w_pg, b_pg, w_pp, ln3_g, ln3_b):
    B, S, D = x.shape
    T = B * S
    assert S % PREP_ROWS == 0 and PREP_ROWS % ATTN_ROWS == 0 and PREP_ROWS % CHUNK == 0
    assert S % (ATTN_TILES * ATTN_ROWS) == 0 and ATTN_TILES % 2 == 0
    assert T % ROUTE_ROWS == 0 and ROUTE_ROWS % RANK_CHUNK == 0
    assert T % MOVE_ROWS == 0 and (T * TOP_K) % EXPERT_ROWS == 0
    assert D == TOKEN_ROWS * LANES and MOVE_ROWS % MOVE_UNROLL == 0
    pos4 = positions.reshape(B, S // PREP_ROWS, 1, PREP_ROWS)
    for i in range(DEPTH):
        w = _layer_weights(w_in[i], q_norm_g[i], w_q_up[i], kv_norm_g[i], w_kv_up[i], gm_ln_g[i], gm_ln_b[i],
                           gm_w_s[i], gm_b_s[i], mla_out_g[i], gm_out_g[i], w_o[i], ln1_g[i], ln1_b[i])
        q, k, vt, g = _prep(x, pos4, w)
        x1 = _attn(q, k, vt, g, x, w).reshape(T, D)
        info, dest, y = _moe(x1, w_rg[i], b_rg[i], w_re[i], b_re[i], w_gate[i], w_up[i], w_down[i])
        wf = dict(wpg=w_pg[i].astype(BF16), bpg=b_pg[i][None, :], wpp=w_pp[i].astype(BF16),
                  l2g=ln2_g[i][None, :], l2b=ln2_b[i][None, :], l3g=ln3_g[i][None, :], l3b=ln3_b[i][None, :])
        x = _final(dest, x1, info, y, p[i].reshape(T, -1), wf).reshape(B, S, D)
    return x
```

```python
import functools

import jax
import jax.numpy as jnp
from jax import lax
from jax.experimental import pallas as pl
from jax.experimental.pallas import tpu as pltpu

F32 = jnp.float32
BF16 = jnp.bfloat16

MLA_HEADS = 8
QK_NOPE = 64
QK_ROPE = 32
V_HEAD = 64
Q_RANK = 256
KV_RANK = 128
ROPE_THETA = 10000.0
MLA_OUT = MLA_HEADS * V_HEAD
GM_GROUPS = 8
GM_CH = 64
GM_OUT = GM_GROUPS * GM_CH
CHUNK = 128
N_GROUPS = 4
EXP_PER_GROUP = 8
N_EXPERTS = N_GROUPS * EXP_PER_GROUP
GROUP_SHIFT = EXP_PER_GROUP.bit_length() - 1
assert EXP_PER_GROUP == 1 << GROUP_SHIFT
TOP_K = 2
EPS = 1e-6
DEPTH = 1
ALPHA = (2.0 * DEPTH) ** 0.25
SM_SCALE = (QK_NOPE + QK_ROPE) ** -0.5
LOG2E = 1.4426950408889634
MASK_VALUE = -1e30

LANES = 128
SUBLANES = 8
TOKEN_ROWS = 8
ONES_ROWS = 16
VMEM_LIMIT = 56 * 1024 * 1024

PREP_ROWS = 512
ATTN_ROWS = 256
ATTN_TILES = 4
ROUTE_ROWS = 2048
RANK_CHUNK = 256
MOVE_ROWS = 256
MOVE_UNROLL = 8
EXPERT_ROWS = 256

C_Q = 0
C_KV = C_Q + Q_RANK
C_KR = C_KV + KV_RANK
C_U = C_KR + LANES
C_V = C_U + GM_OUT
C_END = C_V + GM_OUT
HP = MLA_HEADS * LANES

I_E0, I_E1, I_R0, I_R1, I_G0, I_G1 = range(6)
ROUTE_OUT = 48


def _rms(v, g):
    return v * lax.rsqrt(jnp.mean(v * v, axis=-1, keepdims=True) + EPS) * g


def _ln(v, g, b):
    mu = jnp.mean(v, axis=-1, keepdims=True)
    d = v - mu
    var = jnp.mean(d * d, axis=-1, keepdims=True)
    return d * lax.rsqrt(var + EPS) * g + b


def _dot(a, b):
    return jnp.dot(a, b, preferred_element_type=F32)


def _prep_kernel(x_ref, pos_ref, win_ref, qg_ref, wq_ref, kvg_ref, wk_ref, wv_ref, inv_ref, rope_ref, one_ref,
                 lng_ref, lnb_ref, gavg_ref, ws_ref, bias_ref, gog_ref,
                 q_ref, k_ref, vt_ref, g_ref):
    rows = x_ref.shape[1]
    h = _dot(x_ref[0].astype(BF16), win_ref[...])

    ang = inv_ref[...] * pos_ref[0, 0].astype(F32)
    parts = []
    for t in (jnp.cos(ang), jnp.sin(ang)):
        hi = t.astype(BF16).astype(F32)
        parts += [hi, t - hi]
    tabs = _dot(jnp.concatenate(parts, axis=0).T.astype(BF16), rope_ref[...])
    cos_t = tabs[:, :LANES] + one_ref[...]
    sin_a = tabs[:, LANES:2 * LANES]
    sin_b = tabs[:, 2 * LANES:]
    half = QK_ROPE // 2

    def rotate(v):
        return v * cos_t + pltpu.roll(v, LANES - half, 1) * sin_a + pltpu.roll(v, half, 1) * sin_b

    cq = _rms(h[:, C_Q:C_Q + Q_RANK], qg_ref[...]).astype(BF16)
    q2 = _dot(cq, wq_ref[...])
    for hd in range(MLA_HEADS):
        lo = hd * LANES
        q_ref[0, :, lo:lo + LANES] = (rotate(q2[:, lo:lo + LANES]) * (SM_SCALE * LOG2E)).astype(BF16)

    ckv = _rms(h[:, C_KV:C_KV + KV_RANK], kvg_ref[...]).astype(BF16)
    kp = _dot(ckv, wk_ref[...])
    kr = rotate(h[:, C_KR:C_KR + LANES])
    for hd in range(MLA_HEADS):
        lo = hd * LANES
        k_ref[0, :, lo:lo + LANES] = (kp[:, lo:lo + LANES] + kr).astype(BF16)
    vp = _dot(ckv, wv_ref[...])
    for kb in range(rows // ATTN_ROWS):
        vt_ref[0, kb] = vp[kb * ATTN_ROWS:(kb + 1) * ATTN_ROWS].T.astype(BF16)

    u = jax.nn.gelu(h[:, C_U:C_U + GM_OUT])
    vv = jax.nn.gelu(h[:, C_V:C_V + GM_OUT])
    mu = _dot(vv.astype(BF16), gavg_ref[...])
    d = vv - mu
    var = _dot((d * d).astype(BF16), gavg_ref[...])
    vn = (d * lax.rsqrt(var + EPS) * lng_ref[...] + lnb_ref[...]).astype(BF16)

    tri = lax.broadcasted_iota(jnp.int32, (CHUNK, CHUNK), 0) >= lax.broadcasted_iota(jnp.int32, (CHUNK, CHUNK), 1)
    wm = [jnp.where(tri, ws_ref[g], 0.0).astype(BF16) for g in range(GM_GROUPS)]
    low_half = lax.broadcasted_iota(jnp.int32, (CHUNK, LANES), 1) < GM_CH
    for c in range(rows // CHUNK):
        r0 = c * CHUNK
        parts = []
        for pr in range(GM_GROUPS // 2):
            tile = vn[r0:r0 + CHUNK, pr * LANES:(pr + 1) * LANES]
            parts.append(jnp.where(low_half, _dot(wm[2 * pr], tile), _dot(wm[2 * pr + 1], tile)))
        sg = jnp.concatenate(parts, axis=1) + bias_ref[...]
        gm = u[r0:r0 + CHUNK] * sg
        g_ref[0, r0:r0 + CHUNK, :] = _rms(gm, gog_ref[...]).astype(BF16)


def _prep(x, pos4, w):
    B, S, D = x.shape
    ts = PREP_ROWS
    full = lambda a: pl.BlockSpec(a.shape, lambda b, i: (0,) * a.ndim)
    consts = [w["win"], w["qg"], w["wq"], w["kvg"], w["wk"], w["wv"], w["inv"], w["rope"], w["one"],
              w["lng"], w["lnb"], w["gavg"], w["ws"], w["bias"], w["gog"]]
    return pl.pallas_call(
        _prep_kernel,
        grid=(B, S // ts),
        in_specs=[pl.BlockSpec((1, ts, D), lambda b, i: (b, i, 0)),
                  pl.BlockSpec((1, 1, 1, ts), lambda b, i: (b, i, 0, 0))] + [full(a) for a in consts],
        out_specs=[pl.BlockSpec((1, ts, HP), lambda b, i: (b, i, 0)),
                   pl.BlockSpec((1, ts, HP), lambda b, i: (b, i, 0)),
                   pl.BlockSpec((1, ts // ATTN_ROWS, MLA_OUT, ATTN_ROWS), lambda b, i: (b, i, 0, 0)),
                   pl.BlockSpec((1, ts, GM_OUT), lambda b, i: (b, i, 0))],
        out_shape=[jax.ShapeDtypeStruct((B, S, HP), BF16)] * 2
        + [jax.ShapeDtypeStruct((B, S // ATTN_ROWS, MLA_OUT, ATTN_ROWS), BF16),
           jax.ShapeDtypeStruct((B, S, GM_OUT), BF16)],
        compiler_params=pltpu.CompilerParams(dimension_semantics=("parallel", "parallel"),
                                             vmem_limit_bytes=VMEM_LIMIT),
        name="prep",
    )(x, pos4, *consts)


def _attn_kernel(q_ref, k_ref, vt_ref, g_ref, x_ref, woa_ref, wog_ref, mog_ref, l1g_ref, l1b_ref,
                 o_ref, m_scr, acc_scr, sa_scr, sb_scr):
    pid = pl.program_id(1)
    tq = ATTN_ROWS
    tk = tq
    key = lax.broadcasted_iota(jnp.int32, (tk, tq), 0)
    qry = lax.broadcasted_iota(jnp.int32, (tk, tq), 1)
    diag_mask = key <= qry
    ones = jnp.ones((ONES_ROWS, tk), BF16)

    def tile(t):
        r0 = t * tq
        i = ATTN_TILES * pid + t
        odd = t % 2 == 1
        m_scr[...] = jnp.full(m_scr.shape, MASK_VALUE, F32)
        acc_scr[...] = jnp.zeros(acc_scr.shape, F32)

        def scores(j, s_scr):
            k0 = pl.multiple_of(j * tk, tk)
            for hd in range(MLA_HEADS):
                lo = hd * LANES
                qh = q_ref[0, r0:r0 + tq, lo:lo + LANES]
                kj = k_ref[0, pl.ds(k0, tk), lo:lo + LANES]
                s_scr[hd] = lax.dot_general(kj, qh, (((1,), (1,)), ((), ())), preferred_element_type=F32)

        def update(j, s_scr, masked):
            for hd in range(MLA_HEADS):
                s = s_scr[hd]
                vt = vt_ref[0, j, hd * V_HEAD:(hd + 1) * V_HEAD, :]
                if masked:
                    s = jnp.where(diag_mask, s, MASK_VALUE)
                m_prev = m_scr[hd]
                m_new = jnp.maximum(m_prev, jnp.max(s, axis=0, keepdims=True))
                p = jnp.exp2(s - m_new).astype(BF16)
                scale = jnp.exp2(m_prev - m_new)
                acc_scr[hd] = scale * acc_scr[hd] + _dot(jnp.concatenate([vt, ones], axis=0), p)
                m_scr[hd] = m_new

        def pair(jj, c):
            j = 2 * jj
            scores(j + 1, sb_scr)
            update(j, sa_scr, False)
            scores(j + 2, sa_scr)
            update(j + 1, sb_scr, False)
            return c

        scores(0, sa_scr)
        lax.fori_loop(0, (ATTN_TILES // 2) * pid + t // 2, pair, 0)
        if odd:
            scores(i, sb_scr)
            update(i - 1, sa_scr, False)
            update(i, sb_scr, True)
        else:
            update(i, sa_scr, True)

        at = jnp.concatenate([acc_scr[hd, :V_HEAD] / acc_scr[hd, V_HEAD:V_HEAD + 1] for hd in range(MLA_HEADS)],
                             axis=0)
        at = at * lax.rsqrt(jnp.mean(at * at, axis=0, keepdims=True) + EPS) * mog_ref[...]
        mix = _dot(at.T.astype(BF16), woa_ref[...]) + _dot(g_ref[0, r0:r0 + tq, :], wog_ref[...])
        o_ref[0, r0:r0 + tq, :] = _ln(ALPHA * x_ref[0, r0:r0 + tq, :] + mix, l1g_ref[...], l1b_ref[...])

    for t in range(ATTN_TILES):
        tile(t)


def _attn(q, k, vt, g, x, w):
    B, S, D = x.shape
    tq = ATTN_ROWS
    rows = ATTN_TILES * tq
    full = lambda a: pl.BlockSpec(a.shape, lambda b, i: (0,) * a.ndim)
    consts = [w["woa"], w["wog"], w["mog"], w["l1g"], w["l1b"]]
    return pl.pallas_call(
        _attn_kernel,
        grid=(B, S // rows),
        in_specs=[pl.BlockSpec((1, rows, HP), lambda b, i: (b, i, 0)),
                  pl.BlockSpec((1, S, HP), lambda b, i: (b, 0, 0)),
                  pl.BlockSpec((1,) + vt.shape[1:], lambda b, i: (b, 0, 0, 0)),
                  pl.BlockSpec((1, rows, GM_OUT), lambda b, i: (b, i, 0)),
                  pl.BlockSpec((1, rows, D), lambda b, i: (b, i, 0))] + [full(a) for a in consts],
        out_specs=pl.BlockSpec((1, rows, D), lambda b, i: (b, i, 0)),
        out_shape=jax.ShapeDtypeStruct((B, S, D), F32),
        scratch_shapes=[pltpu.VMEM((MLA_HEADS, 1, tq), F32),
                        pltpu.VMEM((MLA_HEADS, V_HEAD + ONES_ROWS, tq), F32),
                        pltpu.VMEM((MLA_HEADS, tq, tq), F32), pltpu.VMEM((MLA_HEADS, tq, tq), F32)],
        compiler_params=pltpu.CompilerParams(dimension_semantics=("parallel", "parallel"),
                                             vmem_limit_bytes=VMEM_LIMIT),
        name="attn",
    )(q, k, vt, g, x, *consts)


def _route_kernel(x_ref, wr_ref, br_ref, info_ref, infot_ref, cnt_ref, carry_scr, tri_scr):
    step = pl.program_id(0)
    tt = x_ref.shape[0]

    @pl.when(step == 0)
    def _():
        carry_scr[...] = jnp.zeros_like(carry_scr)
        s = lax.broadcasted_iota(jnp.int32, tri_scr.shape, 0)
        t = lax.broadcasted_iota(jnp.int32, tri_scr.shape, 1)
        tri_scr[...] = jnp.where(s < t, 1.0, 0.0).astype(BF16)

    x = x_ref[...]
    xh = x.astype(BF16)
    xl = (x - xh.astype(F32)).astype(BF16)
    wr = wr_ref[...]
    wh = wr.astype(BF16)
    wl = (wr - wh.astype(F32)).astype(BF16)
    nt = (((1,), (1,)), ((), ()))
    by_xh = lax.dot_general(jnp.concatenate([wh, wl], axis=0), xh, nt, preferred_element_type=F32)
    logits = (by_xh[:ROUTE_OUT] + lax.dot_general(wh, xl, nt, preferred_element_type=F32)
              + by_xh[ROUTE_OUT:]) + br_ref[...]
    neg = jnp.float32(-jnp.inf)

    lg = logits[N_EXPERTS:N_EXPERTS + SUBLANES]
    grow = lax.broadcasted_iota(jnp.int32, lg.shape, 0)
    lg = jnp.where(grow < N_GROUPS, lg, neg)
    gmax = jnp.max(lg, axis=0, keepdims=True)
    g_idx = jnp.min(jnp.where(lg == gmax, grow, SUBLANES), axis=0, keepdims=True)
    g_p = 1.0 / jnp.sum(jnp.exp(lg - gmax), axis=0, keepdims=True)

    le = logits[:N_EXPERTS]
    row = lax.broadcasted_iota(jnp.int32, le.shape, 0)
    le = jnp.where((row >> GROUP_SHIFT) == g_idx, le, neg)
    m1 = jnp.max(le, axis=0, keepdims=True)
    i1 = jnp.min(jnp.where(le == m1, row, N_EXPERTS), axis=0, keepdims=True)
    le2 = jnp.where(row == i1, neg, le)
    m2 = jnp.max(le2, axis=0, keepdims=True)
    i2 = jnp.min(jnp.where(le2 == m2, row, N_EXPERTS), axis=0, keepdims=True)
    e2 = jnp.exp(m2 - m1)
    gate0 = g_p / (1.0 + e2)
    gate1 = g_p * e2 / (1.0 + e2)

    hit1 = row == i1
    hit2 = row == i2
    onehot = jnp.where(hit1 | hit2, 1.0, 0.0)
    rc = RANK_CHUNK
    chunks = [onehot[:, c * rc:(c + 1) * rc] for c in range(tt // rc)]
    inside = _dot(jnp.concatenate(chunks, axis=0).astype(BF16), tri_scr[...])
    seen = carry_scr[...]
    parts = []
    for c, chunk in enumerate(chunks):
        parts.append(inside[c * N_EXPERTS:(c + 1) * N_EXPERTS] + seen)
        seen = seen + jnp.sum(chunk, axis=1, keepdims=True)
    before = jnp.concatenate(parts, axis=1)
    rank0 = jnp.sum(jnp.where(hit1, before, 0.0), axis=0, keepdims=True)
    rank1 = jnp.sum(jnp.where(hit2, before, 0.0), axis=0, keepdims=True)
    carry_scr[...] = seen
    cnt_ref[...] = jnp.broadcast_to(seen, cnt_ref.shape)

    fields = jnp.concatenate([i1.astype(F32), i2.astype(F32), rank0, rank1, gate0, gate1,
                              jnp.zeros((SUBLANES - 6, tt), F32)], axis=0)
    infot_ref[0] = fields
    info_ref[...] = jnp.concatenate([fields, jnp.zeros((LANES - SUBLANES, tt), F32)], axis=0).T


def _route(x1, wr, br):
    T, D = x1.shape
    tt = ROUTE_ROWS
    return pl.pallas_call(
        _route_kernel,
        grid=(T // tt,),
        in_specs=[pl.BlockSpec((tt, D), lambda i: (i, 0)),
                  pl.BlockSpec(wr.shape, lambda i: (0, 0)),
                  pl.BlockSpec(br.shape, lambda i: (0, 0))],
        out_specs=[pl.BlockSpec((tt, LANES), lambda i: (i, 0)),
                   pl.BlockSpec((1, SUBLANES, tt), lambda i: (i, 0, 0)),
                   pl.BlockSpec((N_EXPERTS, LANES), lambda i: (0, 0))],
        out_shape=[jax.ShapeDtypeStruct((T, LANES), F32), jax.ShapeDtypeStruct((T // tt, SUBLANES, tt), F32),
                   jax.ShapeDtypeStruct((N_EXPERTS, LANES), F32)],
        scratch_shapes=[pltpu.VMEM((N_EXPERTS, 1), F32), pltpu.VMEM((RANK_CHUNK, RANK_CHUNK), BF16)],
        compiler_params=pltpu.CompilerParams(dimension_semantics=("arbitrary",), vmem_limit_bytes=VMEM_LIMIT),
        name="route",
    )(x1, wr, br)


def _to_token_tiles(dst_ref, val):
    dst_ref[...] = val.astype(BF16).reshape(dst_ref.shape)


def _from_token_tiles(src_ref, rows):
    return src_ref[...].reshape(rows, TOKEN_ROWS * LANES)


def _to_token_tiles_f32(dst_ref, val):
    rows = val.shape[0]
    for c in range(TOKEN_ROWS):
        dst_ref[pl.ds(c, rows, stride=TOKEN_ROWS), :] = val[:, c * LANES:(c + 1) * LANES]


def _from_token_tiles_f32(src_ref, rows):
    return jnp.concatenate([src_ref[pl.ds(c, rows, stride=TOKEN_ROWS), :] for c in range(TOKEN_ROWS)], axis=1)


def _tile_copy(src_ref, src_row, dst_ref, dst_row, sem):
    return pltpu.make_async_copy(src_ref.at[pl.ds(pl.multiple_of(src_row, TOKEN_ROWS), TOKEN_ROWS)],
                                 dst_ref.at[pl.ds(pl.multiple_of(dst_row, TOKEN_ROWS), TOKEN_ROWS)], sem)


def _dispatch_kernel(seg_ref, dest_ref, x0_ref, xn_ref, buf_ref, stage_scr, zero_scr, sem, zero_sem, *, n_steps):
    i = pl.program_id(0)
    rows = xn_ref.shape[0]
    cur = i % 3
    nxt = (i + 1) % 3

    @pl.when(i == 0)
    def _():
        zero_scr[...] = jnp.zeros(zero_scr.shape, BF16)

        block = EXPERT_ROWS * TOKEN_ROWS
        n_blocks = buf_ref.shape[0] // block

        def clear_rows(first):
            return pltpu.make_async_copy(zero_scr, buf_ref.at[pl.ds(pl.multiple_of(first, SUBLANES), block)], zero_sem)

        def clear(e):
            return clear_rows((seg_ref[0, e] - EXPERT_ROWS) * TOKEN_ROWS)

        def start_tail(b, c):
            clear_rows(b * block).start()
            return c

        def wait_tail(b, c):
            clear_rows(b * block).wait()
            return c

        for e in range(N_EXPERTS):
            pl.when(seg_ref[1, e] > 0)(lambda e=e: clear(e).start())
        lax.fori_loop(seg_ref[2, 0], n_blocks, start_tail, 0)
        for e in range(N_EXPERTS):
            pl.when(seg_ref[1, e] > 0)(lambda e=e: clear(e).wait())
        lax.fori_loop(seg_ref[2, 0], n_blocks, wait_tail, 0)

        _to_token_tiles(stage_scr.at[0], x0_ref[...])

    def drain(s):
        for _ in range(TOP_K):
            pltpu.make_async_copy(stage_scr.at[s], stage_scr.at[s], sem.at[s]).wait()

    @pl.when(i >= 2)
    def _():
        drain(nxt)

    _to_token_tiles(stage_scr.at[nxt], xn_ref[...])
    for r in range(rows):
        for kk in range(TOP_K):
            _tile_copy(stage_scr.at[cur], r * TOKEN_ROWS, buf_ref, dest_ref[0, 0, kk * rows + r],
                       sem.at[cur]).start(priority=kk)

    @pl.when(i == n_steps - 1)
    def _():
        drain(cur)
        if n_steps >= 2:
            drain((i + 2) % 3)


def _dispatch(seg, dest3, x1, n_rows):
    T, D = x1.shape
    td = MOVE_ROWS
    n_steps = T // td
    grid_spec = pltpu.PrefetchScalarGridSpec(
        num_scalar_prefetch=1,
        grid=(n_steps,),
        in_specs=[pl.BlockSpec((1, 1, TOP_K * td), lambda i, seg: (i, 0, 0), memory_space=pltpu.SMEM),
                  pl.BlockSpec((td, D), lambda i, seg: (0, 0)),
                  pl.BlockSpec((td, D), lambda i, seg: (jnp.minimum(i + 1, n_steps - 1), 0))],
        out_specs=pl.BlockSpec(memory_space=pl.ANY),
        scratch_shapes=[pltpu.VMEM((3, td * TOKEN_ROWS, LANES), BF16),
                        pltpu.VMEM((EXPERT_ROWS * TOKEN_ROWS, LANES), BF16),
                        pltpu.SemaphoreType.DMA((3,)), pltpu.SemaphoreType.DMA(())],
    )
    return pl.pallas_call(
        functools.partial(_dispatch_kernel, n_steps=n_steps),
        grid_spec=grid_spec,
        out_shape=jax.ShapeDtypeStruct((n_rows * TOKEN_ROWS, LANES), BF16),
        compiler_params=pltpu.CompilerParams(dimension_semantics=("arbitrary",), vmem_limit_bytes=VMEM_LIMIT),
        name="dispatch",
    )(seg, dest3, x1, x1)


def _expert_kernel(be_ref, ne_ref, nu_ref, buf0_ref, bufa_ref, bufb_ref, wg_hbm, wu_hbm, wd_hbm, y_hbm,
                   sg_scr, su_scr, sd_scr, wg_scr, wu_scr, wd_scr, xa_scr, xb_scr, y_scr, cur_ref, sem, ysem):
    step = pl.program_id(0)
    last_step = pl.num_programs(0) - 1
    bm = EXPERT_ROWS
    half = bm * TOKEN_ROWS
    slot = step % 2
    y_ref = y_scr.at[slot]

    def write_back(s, at_step):
        first = pl.multiple_of(at_step * (2 * half), 2 * half)
        return pltpu.make_async_copy(y_scr.at[s], y_hbm.at[pl.ds(first, 2 * half)], ysem.at[s])

    @pl.when(step >= 2)
    def _():
        write_back(slot, step - 2).wait()

    def fetch(expert, s):
        return (pltpu.make_async_copy(wg_hbm.at[expert], sg_scr.at[s], sem.at[s, 0]),
                pltpu.make_async_copy(wu_hbm.at[expert], su_scr.at[s], sem.at[s, 1]),
                pltpu.make_async_copy(wd_hbm.at[expert], sd_scr.at[s], sem.at[s, 2]))

    @pl.when(step == 0)
    def _():
        cur_ref[0] = 0
        for c in fetch(be_ref[0], 0):
            c.start()
        xa_scr[...] = _from_token_tiles(buf0_ref, bm)

    def load_weights(blk):
        e = be_ref[blk]

        @pl.when((blk == 0) | (be_ref[jnp.maximum(blk - 1, 0)] != e))
        def _():
            s = cur_ref[0]
            for c in fetch(e, s):
                c.wait()
            wg_scr[...] = sg_scr[s].astype(BF16)
            wu_scr[...] = su_scr[s].astype(BF16)
            wd_scr[...] = sd_scr[s].astype(BF16)
            nxt = ne_ref[blk]

            @pl.when(nxt >= 0)
            def _():
                for c in fetch(nxt, 1 - s):
                    c.start()

            cur_ref[0] = 1 - s

    def compute(x_scr, nxt_ref, nxt_scr, out_rows):
        nxt_scr[...] = _from_token_tiles(nxt_ref, bm)
        xb = x_scr[...]
        hidden = jax.nn.silu(_dot(xb, wg_scr[...])) * _dot(xb, wu_scr[...])
        _to_token_tiles_f32(y_ref.at[out_rows], _dot(hidden.astype(BF16), wd_scr[...]))

    def run(blk, x_scr, nxt_ref, nxt_scr, out_rows):
        @pl.when(blk < nu_ref[0])
        def _():
            compute(x_scr, nxt_ref, nxt_scr, out_rows)

        @pl.when(blk >= nu_ref[0])
        def _():
            y_ref[out_rows, :] = jnp.zeros((half, LANES), F32)

    blk_a, blk_b = 2 * step, 2 * step + 1
    rows_a, rows_b = pl.ds(0, half), pl.ds(half, half)
    load_weights(blk_a)
    same = (be_ref[blk_a] == be_ref[blk_b]) & (blk_b < nu_ref[0])

    @pl.when(same)
    def _():
        compute(xa_scr, bufa_ref, xb_scr, rows_a)
        compute(xb_scr, bufb_ref, xa_scr, rows_b)

    @pl.when(jnp.logical_not(same))
    def _():
        run(blk_a, xa_scr, bufa_ref, xb_scr, rows_a)
        load_weights(blk_b)
        run(blk_b, xb_scr, bufb_ref, xa_scr, rows_b)

    write_back(slot, step).start(priority=1)

    @pl.when(step == last_step)
    def _():
        write_back(slot, step).wait()

        @pl.when(step >= 1)
        def _():
            write_back(1 - slot, step - 1).wait()


def _experts(block_expert, next_expert, n_used, buf, w_gate, w_up, w_down):
    bm = EXPERT_ROWS
    D, ff = w_gate.shape[1:]
    n_blocks = buf.shape[0] // (bm * TOKEN_ROWS)
    assert n_blocks % 2 == 0
    last = n_blocks - 1
    grid_spec = pltpu.PrefetchScalarGridSpec(
        num_scalar_prefetch=3,
        grid=(n_blocks // 2,),
        in_specs=[pl.BlockSpec((bm * TOKEN_ROWS, LANES), lambda s, *_: (0, 0)),
                  pl.BlockSpec((bm * TOKEN_ROWS, LANES), lambda s, *_: (2 * s + 1, 0)),
                  pl.BlockSpec((bm * TOKEN_ROWS, LANES), lambda s, *_: (jnp.minimum(2 * s + 2, last), 0)),
                  pl.BlockSpec(memory_space=pl.ANY),
                  pl.BlockSpec(memory_space=pl.ANY),
                  pl.BlockSpec(memory_space=pl.ANY)],
        out_specs=pl.BlockSpec(memory_space=pl.ANY),
        scratch_shapes=[pltpu.VMEM((2, D, ff), F32), pltpu.VMEM((2, D, ff), F32), pltpu.VMEM((2, ff, D), F32),
                        pltpu.VMEM((D, ff), BF16), pltpu.VMEM((D, ff), BF16), pltpu.VMEM((ff, D), BF16),
                        pltpu.VMEM((bm, D), BF16), pltpu.VMEM((bm, D), BF16),
                        pltpu.VMEM((2, 2 * bm * TOKEN_ROWS, LANES), F32),
                        pltpu.SMEM((1,), jnp.int32), pltpu.SemaphoreType.DMA((2, 3)), pltpu.SemaphoreType.DMA((2,))],
    )
    return pl.pallas_call(
        _expert_kernel,
        grid_spec=grid_spec,
        out_shape=jax.ShapeDtypeStruct(buf.shape, F32),
        compiler_params=pltpu.CompilerParams(dimension_semantics=("arbitrary",), vmem_limit_bytes=VMEM_LIMIT),
        name="experts",
    )(block_expert, next_expert, n_used, buf, buf, buf, w_gate, w_up, w_down)


def _final_kernel(d0_ref, d1_ref, d2_ref, x_ref, info_ref, y_ref, p_ref, wpg_ref, bpg_ref, wpp_ref,
                  l2g_ref, l2b_ref, l3g_ref, l3b_ref, o_ref, rows_scr, sem):
    i = pl.program_id(0)
    last = pl.num_programs(0) - 1
    rows = x_ref.shape[0]
    slot = i % 3
    ahead = (i + 2) % 3

    def row_copy(dref, s, r, kk):
        return _tile_copy(y_ref, dref[0, 0, kk * rows + r], rows_scr.at[s, kk], r * TOKEN_ROWS, sem.at[s])

    def landed(s):
        pltpu.make_async_copy(rows_scr.at[s], rows_scr.at[s], sem.at[s]).wait()

    @pl.when(i == 0)
    def _():
        def start(c, carry):
            for u in range(MOVE_UNROLL):
                for kk in range(TOP_K):
                    row_copy(d0_ref, 0, c * MOVE_UNROLL + u, kk).start(priority=kk)
                    row_copy(d1_ref, 1, c * MOVE_UNROLL + u, kk).start(priority=kk)
            return carry

        lax.fori_loop(0, rows // MOVE_UNROLL, start, 0)

    landed(slot)
    info = info_ref[...]
    gate0 = info[:, I_G0:I_G0 + 1]
    gate1 = info[:, I_G1:I_G1 + 1]
    moe = (_from_token_tiles_f32(rows_scr.at[slot, 0], rows) * gate0
           + _from_token_tiles_f32(rows_scr.at[slot, 1], rows) * gate1)

    for r in range(rows):
        for kk in range(TOP_K):
            row_copy(d2_ref, ahead, r, kk).start(priority=kk)

    pp = _dot(p_ref[...].astype(BF16), wpp_ref[...])
    x2 = _ln(ALPHA * x_ref[...] + moe, l2g_ref[...], l2b_ref[...])
    gate = jax.nn.sigmoid(_dot(x2.astype(BF16), wpg_ref[...]) + bpg_ref[...])
    o_ref[...] = _ln(ALPHA * x2 + gate * pp, l3g_ref[...], l3b_ref[...])

    @pl.when(i == last)
    def _():
        landed((i + 1) % 3)
        landed(ahead)


def _final(dest3, x1, info, y, p2, w):
    T, D = x1.shape
    tc = MOVE_ROWS
    pd = p2.shape[1]
    full = lambda a: pl.BlockSpec(a.shape, lambda i: (0,) * a.ndim)
    consts = [w["wpg"], w["bpg"], w["wpp"], w["l2g"], w["l2b"], w["l3g"], w["l3b"]]
    last = T // tc - 1
    assert last >= 2
    return pl.pallas_call(
        _final_kernel,
        grid=(T // tc,),
        in_specs=[pl.BlockSpec((1, 1, TOP_K * tc), lambda i: (i, 0, 0), memory_space=pltpu.SMEM),
                  pl.BlockSpec((1, 1, TOP_K * tc), lambda i: (jnp.minimum(i + 1, last), 0, 0), memory_space=pltpu.SMEM),
                  pl.BlockSpec((1, 1, TOP_K * tc), lambda i: (jnp.minimum(i + 2, last), 0, 0), memory_space=pltpu.SMEM),
                  pl.BlockSpec((tc, D), lambda i: (i, 0)),
                  pl.BlockSpec((tc, LANES), lambda i: (i, 0)),
                  pl.BlockSpec(memory_space=pl.ANY),
                  pl.BlockSpec((tc, pd), lambda i: (i, 0))] + [full(a) for a in consts],
        out_specs=pl.BlockSpec((tc, D), lambda i: (i, 0)),
        out_shape=jax.ShapeDtypeStruct((T, D), F32),
        scratch_shapes=[pltpu.VMEM((3, TOP_K, tc * TOKEN_ROWS, LANES), F32), pltpu.SemaphoreType.DMA((3,))],
        compiler_params=pltpu.CompilerParams(dimension_semantics=("arbitrary",), vmem_limit_bytes=VMEM_LIMIT),
        name="final",
    )(dest3, dest3, dest3, x1, info, y, p2, *consts)


def _pad_heads(a, width):
    lead = a.shape[:-1]
    a = a.reshape(lead + (MLA_HEADS, width))
    a = jnp.pad(a, [(0, 0)] * len(lead) + [(0, 0), (0, LANES - width)])
    return a.reshape(lead + (HP,))


def _layer_weights(w_in, q_norm_g, w_q_up, kv_norm_g, w_kv_up, gm_ln_g, gm_ln_b, gm_w_s, gm_b_s,
                   mla_out_g, gm_out_g, w_o, ln1_g, ln1_b):
    D = w_in.shape[0]
    half = QK_ROPE // 2
    c1, c2, c3 = Q_RANK, Q_RANK + KV_RANK, Q_RANK + KV_RANK + QK_ROPE
    zeros = lambda *s: jnp.zeros(s, F32)
    kr = jnp.concatenate([zeros(D, QK_NOPE), w_in[:, c2:c3], zeros(D, LANES - QK_NOPE - QK_ROPE)], axis=1)
    win = jnp.concatenate([w_in[:, :c2], kr, w_in[:, c3:]], axis=1).astype(BF16)
    wq = _pad_heads(w_q_up, QK_NOPE + QK_ROPE).astype(BF16)

    wkv3 = w_kv_up.reshape(KV_RANK, MLA_HEADS, QK_NOPE + V_HEAD)
    wk = _pad_heads(wkv3[..., :QK_NOPE].reshape(KV_RANK, -1), QK_NOPE).astype(BF16)
    wv = wkv3[..., QK_NOPE:].reshape(KV_RANK, -1).astype(BF16)

    inv = (ROPE_THETA ** (-jnp.arange(0, QK_ROPE, 2, dtype=F32) / QK_ROPE))[:, None]
    eye = jnp.eye(half, dtype=F32)
    first = jnp.pad(eye, ((0, 0), (QK_NOPE, LANES - QK_NOPE - half)))
    second = jnp.pad(eye, ((0, 0), (QK_NOPE + half, LANES - QK_NOPE - QK_ROPE)))
    zero = jnp.zeros_like(first)
    cos_rows = jnp.concatenate([first + second, zero, zero], axis=1)
    sin_rows = jnp.concatenate([zero, -first, second], axis=1)
    rope = jnp.concatenate([cos_rows, cos_rows, sin_rows, sin_rows], axis=0).astype(BF16)
    lane = jnp.arange(LANES)
    one = jnp.where((lane >= QK_NOPE) & (lane < QK_NOPE + QK_ROPE), 0.0, 1.0)[None, :]

    grp = jnp.arange(GM_OUT) // GM_CH
    gavg = jnp.where(grp[:, None] == grp[None, :], 1.0 / GM_CH, 0.0).astype(BF16)
    bias = jnp.repeat(gm_b_s.T, GM_CH, axis=1)

    woa = w_o[:MLA_OUT].astype(BF16)
    wog = w_o[MLA_OUT:].astype(BF16)
    return dict(win=win, qg=q_norm_g[None, :], wq=wq, kvg=kv_norm_g[None, :], wk=wk, wv=wv, inv=inv, rope=rope, one=one,
                lng=gm_ln_g[None, :], lnb=gm_ln_b[None, :], gavg=gavg, ws=gm_w_s, bias=bias, gog=gm_out_g[None, :],
                woa=woa, wog=wog, mog=mla_out_g[:, None], l1g=ln1_g[None, :], l1b=ln1_b[None, :])


def _moe(x1, w_rg, b_rg, w_re, b_re, w_gate, w_up, w_down):
    T, D = x1.shape
    pad = ROUTE_OUT - N_EXPERTS - N_GROUPS
    wr = jnp.concatenate([w_re.T, w_rg.T, jnp.zeros((pad, D), F32)], axis=0)
    br = jnp.concatenate([b_re, b_rg, jnp.zeros((pad,), F32)])[:, None]
    info, info_t, cnt = _route(x1, wr, br)

    bm = EXPERT_ROWS
    n_blocks = (T * TOP_K) // bm + N_EXPERTS
    counts = cnt[:, 0].astype(jnp.int32)
    padded = (counts + bm - 1) // bm * bm
    pad_ends = jnp.cumsum(padded)
    pad_starts = pad_ends - padded
    def dest_rows(e_lane, r_lane):
        e = info_t[:, e_lane, :].astype(jnp.int32)
        ids = jnp.arange(N_EXPERTS)[:, None, None]
        seg_start = jnp.sum(jnp.where(e[None] == ids, pad_starts[:, None, None], 0), axis=0)
        return ((seg_start + info_t[:, r_lane, :].astype(jnp.int32)) * TOKEN_ROWS).reshape(T // MOVE_ROWS, MOVE_ROWS)

    dest = jnp.concatenate([dest_rows(I_E0, I_R0), dest_rows(I_E1, I_R1)], axis=1)[:, None, :]
    block_start = jnp.arange(n_blocks, dtype=jnp.int32) * bm
    block_expert = jnp.minimum(jnp.sum(pad_ends[None, :] <= block_start[:, None], axis=1),
                               N_EXPERTS - 1).astype(jnp.int32)

    blk = jnp.arange(n_blocks)
    later = (blk[None, :] > blk[:, None]) & (block_expert[None, :] != block_expert[:, None])
    next_expert = jnp.min(jnp.where(later, block_expert[None, :], N_EXPERTS), axis=1)
    next_expert = jnp.where(next_expert == N_EXPERTS, -1, next_expert).astype(jnp.int32)
    n_used = (pad_ends[-1:] // bm).astype(jnp.int32)

    seg = jnp.stack([pad_ends, padded, jnp.broadcast_to(n_used, (N_EXPERTS,))]).astype(jnp.int32)
    buf = _dispatch(seg, dest, x1, n_blocks * bm)
    y = _experts(block_expert, next_expert, n_used, buf, w_gate, w_up, w_down)
    return info, dest, y


def kernel(x, p, positions, w_in, q_norm_g, w_q_up, kv_norm_g, w_kv_up, gm_ln_g, gm_ln_b, gm_w_s, gm_b_s, mla_out_g, gm_out_g, w_o, ln1_g, ln1_b, w_rg, b_rg, w_re, b_re, w_gate, w_up, w_down, ln2_g, ln2_b, w_pg, b_pg, w_pp, ln3_g, ln3_b):
    B, S, D = x.shape
    T = B * S
    assert S % PREP_ROWS == 0 and PREP_ROWS % ATTN_ROWS == 0 and PREP_ROWS % CHUNK == 0
    assert S % (ATTN_TILES * ATTN_ROWS) == 0 and ATTN_TILES % 2 == 0
    assert T % ROUTE_ROWS == 0 and ROUTE_ROWS % RANK_CHUNK == 0
    assert T % MOVE_ROWS == 0 and (T * TOP_K) % EXPERT_ROWS == 0
    assert D == TOKEN_ROWS * LANES and MOVE_ROWS % MOVE_UNROLL == 0
    pos4 = positions.reshape(B, S // PREP_ROWS, 1, PREP_ROWS)
    for i in range(DEPTH):
        w = _layer_weights(w_in[i], q_norm_g[i], w_q_up[i], kv_norm_g[i], w_kv_up[i], gm_ln_g[i], gm_ln_b[i],
                           gm_w_s[i], gm_b_s[i], mla_out_g[i], gm_out_g[i], w_o[i], ln1_g[i], ln1_b[i])
        q, k, vt, g = _prep(x, pos4, w)
        x1 = _attn(q, k, vt, g, x, w).reshape(T, D)
        info, dest, y = _moe(x1, w_rg[i], b_rg[i], w_re[i], b_re[i], w_gate[i], w_up[i], w_down[i])
        wf = dict(wpg=w_pg[i].astype(BF16), bpg=b_pg[i][None, :], wpp=w_pp[i].astype(BF16),
                  l2g=ln2_g[i][None, :], l2b=ln2_b[i][None, :], l3g=ln3_g[i][None, :], l3b=ln3_b[i][None, :])
        x = _final(dest, x1, info, y, p[i].reshape(T, -1), wf).reshape(B, S, D)
    return x
```

```python
import functools

import jax
import jax.numpy as jnp
from jax import lax
from jax.experimental import pallas as pl
from jax.experimental.pallas import tpu as pltpu

F32 = jnp.float32
BF16 = jnp.bfloat16

MLA_HEADS = 8
QK_NOPE = 64
QK_ROPE = 32
V_HEAD = 64
Q_RANK = 256
KV_RANK = 128
ROPE_THETA = 10000.0
MLA_OUT = MLA_HEADS * V_HEAD
GM_GROUPS = 8
GM_CH = 64
GM_OUT = GM_GROUPS * GM_CH
CHUNK = 128
N_GROUPS = 4
EXP_PER_GROUP = 8
N_EXPERTS = N_GROUPS * EXP_PER_GROUP
GROUP_SHIFT = EXP_PER_GROUP.bit_length() - 1
assert EXP_PER_GROUP == 1 << GROUP_SHIFT
TOP_K = 2
EPS = 1e-6
DEPTH = 1
ALPHA = (2.0 * DEPTH) ** 0.25
SM_SCALE = (QK_NOPE + QK_ROPE) ** -0.5
LOG2E = 1.4426950408889634
MASK_VALUE = -1e30

LANES = 128
SUBLANES = 8
TOKEN_ROWS = 8
ONES_ROWS = 16
VMEM_LIMIT = 56 * 1024 * 1024

PREP_ROWS = 512
ATTN_ROWS = 256
ATTN_TILES = 4
ROUTE_ROWS = 2048
RANK_CHUNK = 256
MOVE_ROWS = 256
MOVE_UNROLL = 8
EXPERT_ROWS = 256

C_Q = 0
C_KV = C_Q + Q_RANK
C_KR = C_KV + KV_RANK
C_U = C_KR + LANES
C_V = C_U + GM_OUT
C_END = C_V + GM_OUT
HP = MLA_HEADS * LANES

I_E0, I_E1, I_R0, I_R1, I_G0, I_G1 = range(6)
ROUTE_OUT = 48


def _rms(v, g):
    return v * lax.rsqrt(jnp.mean(v * v, axis=-1, keepdims=True) + EPS) * g


def _ln(v, g, b):
    mu = jnp.mean(v, axis=-1, keepdims=True)
    d = v - mu
    var = jnp.mean(d * d, axis=-1, keepdims=True)
    return d * lax.rsqrt(var + EPS) * g + b


def _dot(a, b):
    return jnp.dot(a, b, preferred_element_type=F32)


def _prep_kernel(x_ref, pos_ref, win_ref, qg_ref, wq_ref, kvg_ref, wk_ref, wv_ref, inv_ref, rope_ref, one_ref,
                 lng_ref, lnb_ref, gavg_ref, ws_ref, bias_ref, gog_ref,
                 q_ref, k_ref, vt_ref, g_ref):
    rows = x_ref.shape[1]
    h = _dot(x_ref[0].astype(BF16), win_ref[...])

    ang = inv_ref[...] * pos_ref[0, 0].astype(F32)
    parts = []
    for t in (jnp.cos(ang), jnp.sin(ang)):
        hi = t.astype(BF16).astype(F32)
        parts += [hi, t - hi]
    tabs = _dot(jnp.concatenate(parts, axis=0).T.astype(BF16), rope_ref[...])
    cos_t = tabs[:, :LANES] + one_ref[...]
    sin_a = tabs[:, LANES:2 * LANES]
    sin_b = tabs[:, 2 * LANES:]
    half = QK_ROPE // 2

    def rotate(v):
        return v * cos_t + pltpu.roll(v, LANES - half, 1) * sin_a + pltpu.roll(v, half, 1) * sin_b

    cq = _rms(h[:, C_Q:C_Q + Q_RANK], qg_ref[...]).astype(BF16)
    q2 = _dot(cq, wq_ref[...])
    for hd in range(MLA_HEADS):
        lo = hd * LANES
        q_ref[0, :, lo:lo + LANES] = (rotate(q2[:, lo:lo + LANES]) * (SM_SCALE * LOG2E)).astype(BF16)

    ckv = _rms(h[:, C_KV:C_KV + KV_RANK], kvg_ref[...]).astype(BF16)
    kp = _dot(ckv, wk_ref[...])
    kr = rotate(h[:, C_KR:C_KR + LANES])
    for hd in range(MLA_HEADS):
        lo = hd * LANES
        k_ref[0, :, lo:lo + LANES] = (kp[:, lo:lo + LANES] + kr).astype(BF16)
    vp = _dot(ckv, wv_ref[...])
    for kb in range(rows // ATTN_ROWS):
        vt_ref[0, kb] = vp[kb * ATTN_ROWS:(kb + 1) * ATTN_ROWS].T.astype(BF16)

    u = jax.nn.gelu(h[:, C_U:C_U + GM_OUT])
    vv = jax.nn.gelu(h[:, C_V:C_V + GM_OUT])
    mu = _dot(vv.astype(BF16), gavg_ref[...])
    d = vv - mu
    var = _dot((d * d).astype(BF16), gavg_ref[...])
    vn = (d * lax.rsqrt(var + EPS) * lng_ref[...] + lnb_ref[...]).astype(BF16)

    tri = lax.broadcasted_iota(jnp.int32, (CHUNK, CHUNK), 0) >= lax.broadcasted_iota(jnp.int32, (CHUNK, CHUNK), 1)
    wm = [jnp.where(tri, ws_ref[g], 0.0).astype(BF16) for g in range(GM_GROUPS)]
    low_half = lax.broadcasted_iota(jnp.int32, (CHUNK, LANES), 1) < GM_CH
    for c in range(rows // CHUNK):
        r0 = c * CHUNK
        parts = []
        for pr in range(GM_GROUPS // 2):
            tile = vn[r0:r0 + CHUNK, pr * LANES:(pr + 1) * LANES]
            parts.append(jnp.where(low_half, _dot(wm[2 * pr], tile), _dot(wm[2 * pr + 1], tile)))
        sg = jnp.concatenate(parts, axis=1) + bias_ref[...]
        gm = u[r0:r0 + CHUNK] * sg
        g_ref[0, r0:r0 + CHUNK, :] = _rms(gm, gog_ref[...]).astype(BF16)


def _prep(x, pos4, w):
    B, S, D = x.shape
    ts = PREP_ROWS
    full = lambda a: pl.BlockSpec(a.shape, lambda b, i: (0,) * a.ndim)
    consts = [w["win"], w["qg"], w["wq"], w["kvg"], w["wk"], w["wv"], w["inv"], w["rope"], w["one"],
              w["lng"], w["lnb"], w["gavg"], w["ws"], w["bias"], w["gog"]]
    return pl.pallas_call(
        _prep_kernel,
        grid=(B, S // ts),
        in_specs=[pl.BlockSpec((1, ts, D), lambda b, i: (b, i, 0)),
                  pl.BlockSpec((1, 1, 1, ts), lambda b, i: (b, i, 0, 0))] + [full(a) for a in consts],
        out_specs=[pl.BlockSpec((1, ts, HP), lambda b, i: (b, i, 0)),
                   pl.BlockSpec((1, ts, HP), lambda b, i: (b, i, 0)),
                   pl.BlockSpec((1, ts // ATTN_ROWS, MLA_OUT, ATTN_ROWS), lambda b, i: (b, i, 0, 0)),
                   pl.BlockSpec((1, ts, GM_OUT), lambda b, i: (b, i, 0))],
        out_shape=[jax.ShapeDtypeStruct((B, S, HP), BF16)] * 2
        + [jax.ShapeDtypeStruct((B, S // ATTN_ROWS, MLA_OUT, ATTN_ROWS), BF16),
           jax.ShapeDtypeStruct((B, S, GM_OUT), BF16)],
        compiler_params=pltpu.CompilerParams(dimension_semantics=("parallel", "parallel"),
                                             vmem_limit_bytes=VMEM_LIMIT),
        name="prep",
    )(x, pos4, *consts)


def _attn_kernel(q_ref, k_ref, vt_ref, g_ref, x_ref, woa_ref, wog_ref, mog_ref, l1g_ref, l1b_ref,
                 o_ref, m_scr, acc_scr, sa_scr, sb_scr):
    pid = pl.program_id(1)
    tq = ATTN_ROWS
    tk = tq
    key = lax.broadcasted_iota(jnp.int32, (tk, tq), 0)
    qry = lax.broadcasted_iota(jnp.int32, (tk, tq), 1)
    diag_mask = key <= qry
    ones = jnp.ones((ONES_ROWS, tk), BF16)

    def tile(t):
        r0 = t * tq
        i = ATTN_TILES * pid + t
        odd = t % 2 == 1
        m_scr[...] = jnp.full(m_scr.shape, MASK_VALUE, F32)
        acc_scr[...] = jnp.zeros(acc_scr.shape, F32)

        def scores(j, s_scr):
            k0 = pl.multiple_of(j * tk, tk)
            for hd in range(MLA_HEADS):
                lo = hd * LANES
                qh = q_ref[0, r0:r0 + tq, lo:lo + LANES]
                kj = k_ref[0, pl.ds(k0, tk), lo:lo + LANES]
                s_scr[hd] = lax.dot_general(kj, qh, (((1,), (1,)), ((), ())), preferred_element_type=F32)

        def update(j, s_scr, masked):
            for hd in range(MLA_HEADS):
                s = s_scr[hd]
                vt = vt_ref[0, j, hd * V_HEAD:(hd + 1) * V_HEAD, :]
                if masked:
                    s = jnp.where(diag_mask, s, MASK_VALUE)
                m_prev = m_scr[hd]
                m_new = jnp.maximum(m_prev, jnp.max(s, axis=0, keepdims=True))
                p = jnp.exp2(s - m_new).astype(BF16)
                scale = jnp.exp2(m_prev - m_new)
                acc_scr[hd] = scale * acc_scr[hd] + _dot(jnp.concatenate([vt, ones], axis=0), p)
                m_scr[hd] = m_new

        def pair(jj, c):
            j = 2 * jj
            scores(j + 1, sb_scr)
            update(j, sa_scr, False)
            scores(j + 2, sa_scr)
            update(j + 1, sb_scr, False)
            return c

        scores(0, sa_scr)
        lax.fori_loop(0, (ATTN_TILES // 2) * pid + t // 2, pair, 0)
        if odd:
            scores(i, sb_scr)
            update(i - 1, sa_scr, False)
            update(i, sb_scr, True)
        else:
            update(i, sa_scr, True)

        at = jnp.concatenate([acc_scr[hd, :V_HEAD] / acc_scr[hd, V_HEAD:V_HEAD + 1] for hd in range(MLA_HEADS)],
                             axis=0)
        at = at * lax.rsqrt(jnp.mean(at * at, axis=0, keepdims=True) + EPS) * mog_ref[...]
        mix = _dot(at.T.astype(BF16), woa_ref[...]) + _dot(g_ref[0, r0:r0 + tq, :], wog_ref[...])
        o_ref[0, r0:r0 + tq, :] = _ln(ALPHA * x_ref[0, r0:r0 + tq, :] + mix, l1g_ref[...], l1b_ref[...])

    for t in range(ATTN_TILES):
        tile(t)


def _attn(q, k, vt, g, x, w):
    B, S, D = x.shape
    tq = ATTN_ROWS
    rows = ATTN_TILES * tq
    full = lambda a: pl.BlockSpec(a.shape, lambda b, i: (0,) * a.ndim)
    consts = [w["woa"], w["wog"], w["mog"], w["l1g"], w["l1b"]]
    return pl.pallas_call(
        _attn_kernel,
        grid=(B, S // rows),
        in_specs=[pl.BlockSpec((1, rows, HP), lambda b, i: (b, i, 0)),
                  pl.BlockSpec((1, S, HP), lambda b, i: (b, 0, 0)),
                  pl.BlockSpec((1,) + vt.shape[1:], lambda b, i: (b, 0, 0, 0)),
                  pl.BlockSpec((1, rows, GM_OUT), lambda b, i: (b, i, 0)),
                  pl.BlockSpec((1, rows, D), lambda b, i: (b, i, 0))] + [full(a) for a in consts],
        out_specs=pl.BlockSpec((1, rows, D), lambda b, i: (b, i, 0)),
        out_shape=jax.ShapeDtypeStruct((B, S, D), F32),
        scratch_shapes=[pltpu.VMEM((MLA_HEADS, 1, tq), F32),
                        pltpu.VMEM((MLA_HEADS, V_HEAD + ONES_ROWS, tq), F32),
                        pltpu.VMEM((MLA_HEADS, tq, tq), F32), pltpu.VMEM((MLA_HEADS, tq, tq), F32)],
        compiler_params=pltpu.CompilerParams(dimension_semantics=("parallel", "parallel"),
                                             vmem_limit_bytes=VMEM_LIMIT),
        name="attn",
    )(q, k, vt, g, x, *consts)


def _route_kernel(x_hbm, wr_ref, br_ref, info_ref, infot_ref, cnt_ref, carry_scr, tri_scr, x_scr, xsem):
    step = pl.program_id(0)
    n_steps = pl.num_programs(0)
    tt = x_scr.shape[1]

    def fetch(t):
        first = pl.multiple_of(t * tt, tt)
        return pltpu.make_async_copy(x_hbm.at[pl.ds(first, tt)], x_scr.at[t % 3], xsem.at[t % 3])

    @pl.when(step == 0)
    def _():
        fetch(0).start()
        fetch(1).start()

    @pl.when(step + 2 < n_steps)
    def _():
        fetch(step + 2).start()

    fetch(step).wait()
    x_ref = x_scr.at[step % 3]

    @pl.when(step == 0)
    def _():
        carry_scr[...] = jnp.zeros_like(carry_scr)
        s = lax.broadcasted_iota(jnp.int32, tri_scr.shape, 0)
        t = lax.broadcasted_iota(jnp.int32, tri_scr.shape, 1)
        tri_scr[...] = jnp.where(s < t, 1.0, 0.0).astype(BF16)

    x = x_ref[...]
    xh = x.astype(BF16)
    xl = (x - xh.astype(F32)).astype(BF16)
    wr = wr_ref[...]
    wh = wr.astype(BF16)
    wl = (wr - wh.astype(F32)).astype(BF16)
    nt = (((1,), (1,)), ((), ()))
    by_xh = lax.dot_general(jnp.concatenate([wh, wl], axis=0), xh, nt, preferred_element_type=F32)
    logits = (by_xh[:ROUTE_OUT] + lax.dot_general(wh, xl, nt, preferred_element_type=F32)
              + by_xh[ROUTE_OUT:]) + br_ref[...]
    neg = jnp.float32(-jnp.inf)

    lg = logits[N_EXPERTS:N_EXPERTS + SUBLANES]
    grow = lax.broadcasted_iota(jnp.int32, lg.shape, 0)
    lg = jnp.where(grow < N_GROUPS, lg, neg)
    gmax = jnp.max(lg, axis=0, keepdims=True)
    g_idx = jnp.min(jnp.where(lg == gmax, grow, SUBLANES), axis=0, keepdims=True)
    g_p = 1.0 / jnp.sum(jnp.exp(lg - gmax), axis=0, keepdims=True)

    le = logits[:N_EXPERTS]
    row = lax.broadcasted_iota(jnp.int32, le.shape, 0)
    le = jnp.where((row >> GROUP_SHIFT) == g_idx, le, neg)
    m1 = jnp.max(le, axis=0, keepdims=True)
    i1 = jnp.min(jnp.where(le == m1, row, N_EXPERTS), axis=0, keepdims=True)
    le2 = jnp.where(row == i1, neg, le)
    m2 = jnp.max(le2, axis=0, keepdims=True)
    i2 = jnp.min(jnp.where(le2 == m2, row, N_EXPERTS), axis=0, keepdims=True)
    e2 = jnp.exp(m2 - m1)
    gate0 = g_p / (1.0 + e2)
    gate1 = g_p * e2 / (1.0 + e2)

    hit1 = row == i1
    hit2 = row == i2
    onehot = jnp.where(hit1 | hit2, 1.0, 0.0)
    rc = RANK_CHUNK
    chunks = [onehot[:, c * rc:(c + 1) * rc] for c in range(tt // rc)]
    inside = _dot(jnp.concatenate(chunks, axis=0).astype(BF16), tri_scr[...])
    seen = carry_scr[...]
    parts = []
    for c, chunk in enumerate(chunks):
        parts.append(inside[c * N_EXPERTS:(c + 1) * N_EXPERTS] + seen)
        seen = seen + jnp.sum(chunk, axis=1, keepdims=True)
    before = jnp.concatenate(parts, axis=1)
    rank0 = jnp.sum(jnp.where(hit1, before, 0.0), axis=0, keepdims=True)
    rank1 = jnp.sum(jnp.where(hit2, before, 0.0), axis=0, keepdims=True)
    carry_scr[...] = seen
    cnt_ref[...] = jnp.broadcast_to(seen, cnt_ref.shape)

    fields = jnp.concatenate([i1.astype(F32), i2.astype(F32), rank0, rank1, gate0, gate1,
                              jnp.zeros((SUBLANES - 6, tt), F32)], axis=0)
    infot_ref[0] = fields
    info_ref[...] = jnp.concatenate([fields, jnp.zeros((LANES - SUBLANES, tt), F32)], axis=0).T


def _route(x1, wr, br):
    T, D = x1.shape
    tt = ROUTE_ROWS
    assert T // tt >= 2
    return pl.pallas_call(
        _route_kernel,
        grid=(T // tt,),
        in_specs=[pl.BlockSpec(memory_space=pl.ANY),
                  pl.BlockSpec(wr.shape, lambda i: (0, 0)),
                  pl.BlockSpec(br.shape, lambda i: (0, 0))],
        out_specs=[pl.BlockSpec((tt, LANES), lambda i: (i, 0)),
                   pl.BlockSpec((1, SUBLANES, tt), lambda i: (i, 0, 0)),
                   pl.BlockSpec((N_EXPERTS, LANES), lambda i: (0, 0))],
        out_shape=[jax.ShapeDtypeStruct((T, LANES), F32), jax.ShapeDtypeStruct((T // tt, SUBLANES, tt), F32),
                   jax.ShapeDtypeStruct((N_EXPERTS, LANES), F32)],
        scratch_shapes=[pltpu.VMEM((N_EXPERTS, 1), F32), pltpu.VMEM((RANK_CHUNK, RANK_CHUNK), BF16),
                        pltpu.VMEM((3, tt, D), F32), pltpu.SemaphoreType.DMA((3,))],
        compiler_params=pltpu.CompilerParams(dimension_semantics=("arbitrary",), vmem_limit_bytes=VMEM_LIMIT),
        name="route",
    )(x1, wr, br)


def _to_token_tiles(dst_ref, val):
    dst_ref[...] = val.astype(BF16).reshape(dst_ref.shape)


def _from_token_tiles(src_ref, rows):
    return src_ref[...].reshape(rows, TOKEN_ROWS * LANES)


def _to_token_tiles_f32(dst_ref, val):
    rows = val.shape[0]
    for c in range(TOKEN_ROWS):
        dst_ref[pl.ds(c, rows, stride=TOKEN_ROWS), :] = val[:, c * LANES:(c + 1) * LANES]


def _from_token_tiles_f32(src_ref, rows):
    return jnp.concatenate([src_ref[pl.ds(c, rows, stride=TOKEN_ROWS), :] for c in range(TOKEN_ROWS)], axis=1)


def _tile_copy(src_ref, src_row, dst_ref, dst_row, sem):
    return pltpu.make_async_copy(src_ref.at[pl.ds(pl.multiple_of(src_row, TOKEN_ROWS), TOKEN_ROWS)],
                                 dst_ref.at[pl.ds(pl.multiple_of(dst_row, TOKEN_ROWS), TOKEN_ROWS)], sem)


def _dispatch_kernel(seg_ref, dest_ref, x0_ref, xn_ref, buf_ref, stage_scr, zero_scr, sem, zero_sem, *, n_steps):
    i = pl.program_id(0)
    rows = xn_ref.shape[0]
    cur = i % 3
    nxt = (i + 1) % 3

    @pl.when(i == 0)
    def _():
        zero_scr[...] = jnp.zeros(zero_scr.shape, BF16)

        block = EXPERT_ROWS * TOKEN_ROWS
        n_blocks = buf_ref.shape[0] // block

        def clear_rows(first):
            return pltpu.make_async_copy(zero_scr, buf_ref.at[pl.ds(pl.multiple_of(first, SUBLANES), block)], zero_sem)

        def clear(e):
            return clear_rows((seg_ref[0, e] - EXPERT_ROWS) * TOKEN_ROWS)

        def start_tail(b, c):
            clear_rows(b * block).start()
            return c

        def wait_tail(b, c):
            clear_rows(b * block).wait()
            return c

        for e in range(N_EXPERTS):
            pl.when(seg_ref[1, e] > 0)(lambda e=e: clear(e).start())
        lax.fori_loop(seg_ref[2, 0], n_blocks, start_tail, 0)
        for e in range(N_EXPERTS):
            pl.when(seg_ref[1, e] > 0)(lambda e=e: clear(e).wait())
        lax.fori_loop(seg_ref[2, 0], n_blocks, wait_tail, 0)

        _to_token_tiles(stage_scr.at[0], x0_ref[...])

    def drain(s):
        for _ in range(TOP_K):
            pltpu.make_async_copy(stage_scr.at[s], stage_scr.at[s], sem.at[s]).wait()

    @pl.when(i >= 2)
    def _():
        drain(nxt)

    _to_token_tiles(stage_scr.at[nxt], xn_ref[...])
    for r in range(rows):
        for kk in range(TOP_K):
            _tile_copy(stage_scr.at[cur], r * TOKEN_ROWS, buf_ref, dest_ref[0, 0, kk * rows + r],
                       sem.at[cur]).start(priority=kk)

    @pl.when(i == n_steps - 1)
    def _():
        drain(cur)
        if n_steps >= 2:
            drain((i + 2) % 3)


def _dispatch(seg, dest3, x1, n_rows):
    T, D = x1.shape
    td = MOVE_ROWS
    n_steps = T // td
    grid_spec = pltpu.PrefetchScalarGridSpec(
        num_scalar_prefetch=1,
        grid=(n_steps,),
        in_specs=[pl.BlockSpec((1, 1, TOP_K * td), lambda i, seg: (i, 0, 0), memory_space=pltpu.SMEM),
                  pl.BlockSpec((td, D), lambda i, seg: (0, 0)),
                  pl.BlockSpec((td, D), lambda i, seg: (jnp.minimum(i + 1, n_steps - 1), 0))],
        out_specs=pl.BlockSpec(memory_space=pl.ANY),
        scratch_shapes=[pltpu.VMEM((3, td * TOKEN_ROWS, LANES), BF16),
                        pltpu.VMEM((EXPERT_ROWS * TOKEN_ROWS, LANES), BF16),
                        pltpu.SemaphoreType.DMA((3,)), pltpu.SemaphoreType.DMA(())],
    )
    return pl.pallas_call(
        functools.partial(_dispatch_kernel, n_steps=n_steps),
        grid_spec=grid_spec,
        out_shape=jax.ShapeDtypeStruct((n_rows * TOKEN_ROWS, LANES), BF16),
        compiler_params=pltpu.CompilerParams(dimension_semantics=("arbitrary",), vmem_limit_bytes=VMEM_LIMIT),
        name="dispatch",
    )(seg, dest3, x1, x1)


def _expert_kernel(be_ref, ne_ref, nu_ref, buf0_ref, bufa_ref, bufb_ref, wg_hbm, wu_hbm, wd_hbm, y_hbm,
                   sg_scr, su_scr, sd_scr, wg_scr, wu_scr, wd_scr, xa_scr, xb_scr, y_scr, cur_ref, sem, ysem):
    step = pl.program_id(0)
    last_step = pl.num_programs(0) - 1
    bm = EXPERT_ROWS
    half = bm * TOKEN_ROWS
    slot = step % 2
    y_ref = y_scr.at[slot]

    def write_back(s, at_step):
        first = pl.multiple_of(at_step * (2 * half), 2 * half)
        return pltpu.make_async_copy(y_scr.at[s], y_hbm.at[pl.ds(first, 2 * half)], ysem.at[s])

    @pl.when(step >= 2)
    def _():
        write_back(slot, step - 2).wait()

    def fetch(expert, s):
        return (pltpu.make_async_copy(wg_hbm.at[expert], sg_scr.at[s], sem.at[s, 0]),
                pltpu.make_async_copy(wu_hbm.at[expert], su_scr.at[s], sem.at[s, 1]),
                pltpu.make_async_copy(wd_hbm.at[expert], sd_scr.at[s], sem.at[s, 2]))

    @pl.when(step == 0)
    def _():
        cur_ref[0] = 0
        for c in fetch(be_ref[0], 0):
            c.start()
        xa_scr[...] = _from_token_tiles(buf0_ref, bm)

    def load_weights(blk):
        e = be_ref[blk]

        @pl.when((blk == 0) | (be_ref[jnp.maximum(blk - 1, 0)] != e))
        def _():
            s = cur_ref[0]
            for c in fetch(e, s):
                c.wait()
            wg_scr[...] = sg_scr[s].astype(BF16)
            wu_scr[...] = su_scr[s].astype(BF16)
            wd_scr[...] = sd_scr[s].astype(BF16)
            nxt = ne_ref[blk]

            @pl.when(nxt >= 0)
            def _():
                for c in fetch(nxt, 1 - s):
                    c.start()

            cur_ref[0] = 1 - s

    def compute(x_scr, nxt_ref, nxt_scr, out_rows):
        nxt_scr[...] = _from_token_tiles(nxt_ref, bm)
        xb = x_scr[...]
        hidden = jax.nn.silu(_dot(xb, wg_scr[...])) * _dot(xb, wu_scr[...])
        _to_token_tiles_f32(y_ref.at[out_rows], _dot(hidden.astype(BF16), wd_scr[...]))

    def run(blk, x_scr, nxt_ref, nxt_scr, out_rows):
        @pl.when(blk < nu_ref[0])
        def _():
            compute(x_scr, nxt_ref, nxt_scr, out_rows)

        @pl.when(blk >= nu_ref[0])
        def _():
            y_ref[out_rows, :] = jnp.zeros((half, LANES), F32)

    blk_a, blk_b = 2 * step, 2 * step + 1
    rows_a, rows_b = pl.ds(0, half), pl.ds(half, half)
    load_weights(blk_a)
    same = (be_ref[blk_a] == be_ref[blk_b]) & (blk_b < nu_ref[0])

    @pl.when(same)
    def _():
        compute(xa_scr, bufa_ref, xb_scr, rows_a)
        compute(xb_scr, bufb_ref, xa_scr, rows_b)

    @pl.when(jnp.logical_not(same))
    def _():
        run(blk_a, xa_scr, bufa_ref, xb_scr, rows_a)
        load_weights(blk_b)
        run(blk_b, xb_scr, bufb_ref, xa_scr, rows_b)

    write_back(slot, step).start(priority=1)

    @pl.when(step == last_step)
    def _():
        write_back(slot, step).wait()

        @pl.when(step >= 1)
        def _():
            write_back(1 - slot, step - 1).wait()


def _experts(block_expert, next_expert, n_used, buf, w_gate, w_up, w_down):
    bm = EXPERT_ROWS
    D, ff = w_gate.shape[1:]
    n_blocks = buf.shape[0] // (bm * TOKEN_ROWS)
    assert n_blocks % 2 == 0
    last = n_blocks - 1
    grid_spec = pltpu.PrefetchScalarGridSpec(
        num_scalar_prefetch=3,
        grid=(n_blocks // 2,),
        in_specs=[pl.BlockSpec((bm * TOKEN_ROWS, LANES), lambda s, *_: (0, 0)),
                  pl.BlockSpec((bm * TOKEN_ROWS, LANES), lambda s, *_: (2 * s + 1, 0)),
                  pl.BlockSpec((bm * TOKEN_ROWS, LANES), lambda s, *_: (jnp.minimum(2 * s + 2, last), 0)),
                  pl.BlockSpec(memory_space=pl.ANY),
                  pl.BlockSpec(memory_space=pl.ANY),
                  pl.BlockSpec(memory_space=pl.ANY)],
        out_specs=pl.BlockSpec(memory_space=pl.ANY),
        scratch_shapes=[pltpu.VMEM((2, D, ff), F32), pltpu.VMEM((2, D, ff), F32), pltpu.VMEM((2, ff, D), F32),
                        pltpu.VMEM((D, ff), BF16), pltpu.VMEM((D, ff), BF16), pltpu.VMEM((ff, D), BF16),
                        pltpu.VMEM((bm, D), BF16), pltpu.VMEM((bm, D), BF16),
                        pltpu.VMEM((2, 2 * bm * TOKEN_ROWS, LANES), F32),
                        pltpu.SMEM((1,), jnp.int32), pltpu.SemaphoreType.DMA((2, 3)), pltpu.SemaphoreType.DMA((2,))],
    )
    return pl.pallas_call(
        _expert_kernel,
        grid_spec=grid_spec,
        out_shape=jax.ShapeDtypeStruct(buf.shape, F32),
        compiler_params=pltpu.CompilerParams(dimension_semantics=("arbitrary",), vmem_limit_bytes=VMEM_LIMIT),
        name="experts",
    )(block_expert, next_expert, n_used, buf, buf, buf, w_gate, w_up, w_down)


def _final_kernel(d0_ref, d1_ref, d2_ref, x_ref, info_ref, y_ref, p_ref, wpg_ref, bpg_ref, wpp_ref,
                  l2g_ref, l2b_ref, l3g_ref, l3b_ref, o_ref, rows_scr, sem):
    i = pl.program_id(0)
    last = pl.num_programs(0) - 1
    rows = x_ref.shape[0]
    slot = i % 3
    ahead = (i + 2) % 3

    def row_copy(dref, s, r, kk):
        return _tile_copy(y_ref, dref[0, 0, kk * rows + r], rows_scr.at[s, kk], r * TOKEN_ROWS, sem.at[s])

    def landed(s):
        pltpu.make_async_copy(rows_scr.at[s], rows_scr.at[s], sem.at[s]).wait()

    @pl.when(i == 0)
    def _():
        def start(c, carry):
            for u in range(MOVE_UNROLL):
                for kk in range(TOP_K):
                    row_copy(d0_ref, 0, c * MOVE_UNROLL + u, kk).start(priority=kk)
                    row_copy(d1_ref, 1, c * MOVE_UNROLL + u, kk).start(priority=kk)
            return carry

        lax.fori_loop(0, rows // MOVE_UNROLL, start, 0)

    landed(slot)
    info = info_ref[...]
    gate0 = info[:, I_G0:I_G0 + 1]
    gate1 = info[:, I_G1:I_G1 + 1]
    moe = (_from_token_tiles_f32(rows_scr.at[slot, 0], rows) * gate0
           + _from_token_tiles_f32(rows_scr.at[slot, 1], rows) * gate1)

    for r in range(rows):
        for kk in range(TOP_K):
            row_copy(d2_ref, ahead, r, kk).start(priority=1)

    pp = _dot(p_ref[...].astype(BF16), wpp_ref[...])
    x2 = _ln(ALPHA * x_ref[...] + moe, l2g_ref[...], l2b_ref[...])
    gate = jax.nn.sigmoid(_dot(x2.astype(BF16), wpg_ref[...]) + bpg_ref[...])
    o_ref[...] = _ln(ALPHA * x2 + gate * pp, l3g_ref[...], l3b_ref[...])

    @pl.when(i == last)
    def _():
        landed((i + 1) % 3)
        landed(ahead)


def _final(dest3, x1, info, y, p2, w):
    T, D = x1.shape
    tc = MOVE_ROWS
    pd = p2.shape[1]
    full = lambda a: pl.BlockSpec(a.shape, lambda i: (0,) * a.ndim)
    consts = [w["wpg"], w["bpg"], w["wpp"], w["l2g"], w["l2b"], w["l3g"], w["l3b"]]
    last = T // tc - 1
    assert last >= 2
    return pl.pallas_call(
        _final_kernel,
        grid=(T // tc,),
        in_specs=[pl.BlockSpec((1, 1, TOP_K * tc), lambda i: (i, 0, 0), memory_space=pltpu.SMEM),
                  pl.BlockSpec((1, 1, TOP_K * tc), lambda i: (jnp.minimum(i + 1, last), 0, 0), memory_space=pltpu.SMEM),
                  pl.BlockSpec((1, 1, TOP_K * tc), lambda i: (jnp.minimum(i + 2, last), 0, 0), memory_space=pltpu.SMEM),
                  pl.BlockSpec((tc, D), lambda i: (i, 0)),
                  pl.BlockSpec((tc, LANES), lambda i: (i, 0)),
                  pl.BlockSpec(memory_space=pl.ANY),
                  pl.BlockSpec((tc, pd), lambda i: (i, 0))] + [full(a) for a in consts],
        out_specs=pl.BlockSpec((tc, D), lambda i: (i, 0)),
        out_shape=jax.ShapeDtypeStruct((T, D), F32),
        scratch_shapes=[pltpu.VMEM((3, TOP_K, tc * TOKEN_ROWS, LANES), F32), pltpu.SemaphoreType.DMA((3,))],
        compiler_params=pltpu.CompilerParams(dimension_semantics=("arbitrary",), vmem_limit_bytes=VMEM_LIMIT),
        name="final",
    )(dest3, dest3, dest3, x1, info, y, p2, *consts)


def _pad_heads(a, width):
    lead = a.shape[:-1]
    a = a.reshape(lead + (MLA_HEADS, width))
    a = jnp.pad(a, [(0, 0)] * len(lead) + [(0, 0), (0, LANES - width)])
    return a.reshape(lead + (HP,))


def _layer_weights(w_in, q_norm_g, w_q_up, kv_norm_g, w_kv_up, gm_ln_g, gm_ln_b, gm_w_s, gm_b_s,
                   mla_out_g, gm_out_g, w_o, ln1_g, ln1_b):
    D = w_in.shape[0]
    half = QK_ROPE // 2
    c1, c2, c3 = Q_RANK, Q_RANK + KV_RANK, Q_RANK + KV_RANK + QK_ROPE
    zeros = lambda *s: jnp.zeros(s, F32)
    kr = jnp.concatenate([zeros(D, QK_NOPE), w_in[:, c2:c3], zeros(D, LANES - QK_NOPE - QK_ROPE)], axis=1)
    win = jnp.concatenate([w_in[:, :c2], kr, w_in[:, c3:]], axis=1).astype(BF16)
    wq = _pad_heads(w_q_up, QK_NOPE + QK_ROPE).astype(BF16)

    wkv3 = w_kv_up.reshape(KV_RANK, MLA_HEADS, QK_NOPE + V_HEAD)
    wk = _pad_heads(wkv3[..., :QK_NOPE].reshape(KV_RANK, -1), QK_NOPE).astype(BF16)
    wv = wkv3[..., QK_NOPE:].reshape(KV_RANK, -1).astype(BF16)

    inv = (ROPE_THETA ** (-jnp.arange(0, QK_ROPE, 2, dtype=F32) / QK_ROPE))[:, None]
    eye = jnp.eye(half, dtype=F32)
    first = jnp.pad(eye, ((0, 0), (QK_NOPE, LANES - QK_NOPE - half)))
    second = jnp.pad(eye, ((0, 0), (QK_NOPE + half, LANES - QK_NOPE - QK_ROPE)))
    zero = jnp.zeros_like(first)
    cos_rows = jnp.concatenate([first + second, zero, zero], axis=1)
    sin_rows = jnp.concatenate([zero, -first, second], axis=1)
    rope = jnp.concatenate([cos_rows, cos_rows, sin_rows, sin_rows], axis=0).astype(BF16)
    lane = jnp.arange(LANES)
    one = jnp.where((lane >= QK_NOPE) & (lane < QK_NOPE + QK_ROPE), 0.0, 1.0)[None, :]

    grp = jnp.arange(GM_OUT) // GM_CH
    gavg = jnp.where(grp[:, None] == grp[None, :], 1.0 / GM_CH, 0.0).astype(BF16)
    bias = jnp.repeat(gm_b_s.T, GM_CH, axis=1)

    woa = w_o[:MLA_OUT].astype(BF16)
    wog = w_o[MLA_OUT:].astype(BF16)
    return dict(win=win, qg=q_norm_g[None, :], wq=wq, kvg=kv_norm_g[None, :], wk=wk, wv=wv, inv=inv, rope=rope, one=one,
                lng=gm_ln_g[None, :], lnb=gm_ln_b[None, :], gavg=gavg, ws=gm_w_s, bias=bias, gog=gm_out_g[None, :],
                woa=woa, wog=wog, mog=mla_out_g[:, None], l1g=ln1_g[None, :], l1b=ln1_b[None, :])


def _moe(x1, w_rg, b_rg, w_re, b_re, w_gate, w_up, w_down):
    T, D = x1.shape
    pad = ROUTE_OUT - N_EXPERTS - N_GROUPS
    wr = jnp.concatenate([w_re.T, w_rg.T, jnp.zeros((pad, D), F32)], axis=0)
    br = jnp.concatenate([b_re, b_rg, jnp.zeros((pad,), F32)])[:, None]
    info, info_t, cnt = _route(x1, wr, br)

    bm = EXPERT_ROWS
    n_blocks = (T * TOP_K) // bm + N_EXPERTS
    counts = cnt[:, 0].astype(jnp.int32)
    padded = (counts + bm - 1) // bm * bm
    pad_ends = jnp.cumsum(padded)
    pad_starts = pad_ends - padded
    def dest_rows(e_lane, r_lane):
        e = info_t[:, e_lane, :].astype(jnp.int32)
        ids = jnp.arange(N_EXPERTS)[:, None, None]
        seg_start = jnp.sum(jnp.where(e[None] == ids, pad_starts[:, None, None], 0), axis=0)
        return ((seg_start + info_t[:, r_lane, :].astype(jnp.int32)) * TOKEN_ROWS).reshape(T // MOVE_ROWS, MOVE_ROWS)

    dest = jnp.concatenate([dest_rows(I_E0, I_R0), dest_rows(I_E1, I_R1)], axis=1)[:, None, :]
    block_start = jnp.arange(n_blocks, dtype=jnp.int32) * bm
    block_expert = jnp.minimum(jnp.sum(pad_ends[None, :] <= block_start[:, None], axis=1),
                               N_EXPERTS - 1).astype(jnp.int32)

    blk = jnp.arange(n_blocks)
    later = (blk[None, :] > blk[:, None]) & (block_expert[None, :] != block_expert[:, None])
    next_expert = jnp.min(jnp.where(later, block_expert[None, :], N_EXPERTS), axis=1)
    next_expert = jnp.where(next_expert == N_EXPERTS, -1, next_expert).astype(jnp.int32)
    n_used = (pad_ends[-1:] // bm).astype(jnp.int32)

    seg = jnp.stack([pad_ends, padded, jnp.broadcast_to(n_used, (N_EXPERTS,))]).astype(jnp.int32)
    buf = _dispatch(seg, dest, x1, n_blocks * bm)
    y = _experts(block_expert, next_expert, n_used, buf, w_gate, w_up, w_down)
    return info, dest, y


def kernel(x, p, positions, w_in, q_norm_g, w_q_up, kv_norm_g, w_kv_up, gm_ln_g, gm_ln_b, gm_w_s, gm_b_s, mla_out_g, gm_out_g, w_o, ln1_g, ln1_b, w_rg, b_rg, w_re, b_re, w_gate, w_up, w_down, ln2_g, ln2_b, w_pg, b_pg, w_pp, ln3_g, ln3_b):
    B, S, D = x.shape
    T = B * S
    assert S % PREP_ROWS == 0 and PREP_ROWS % ATTN_ROWS == 0 and PREP_ROWS % CHUNK == 0
    assert S % (ATTN_TILES * ATTN_ROWS) == 0 and ATTN_TILES % 2 == 0
    assert T % ROUTE_ROWS == 0 and ROUTE_ROWS % RANK_CHUNK == 0
    assert T % MOVE_ROWS == 0 and (T * TOP_K) % EXPERT_ROWS == 0
    assert D == TOKEN_ROWS * LANES and MOVE_ROWS % MOVE_UNROLL == 0
    pos4 = positions.reshape(B, S // PREP_ROWS, 1, PREP_ROWS)
    for i in range(DEPTH):
        w = _layer_weights(w_in[i], q_norm_g[i], w_q_up[i], kv_norm_g[i], w_kv_up[i], gm_ln_g[i], gm_ln_b[i],
                           gm_w_s[i], gm_b_s[i], mla_out_g[i], gm_out_g[i], w_o[i], ln1_g[i], ln1_b[i])
        q, k, vt, g = _prep(x, pos4, w)
        x1 = _attn(q, k, vt, g, x, w).reshape(T, D)
        info, dest, y = _moe(x1, w_rg[i], b_rg[i], w_re[i], b_re[i], w_gate[i], w_up[i], w_down[i])
        wf = dict(wpg=w_pg[i].astype(BF16), bpg=b_pg[i][None, :], wpp=w_pp[i].astype(BF16),
                  l2g=ln2_g[i][None, :], l2b=ln2_b[i][None, :], l3g=ln3_g[i][None, :], l3b=ln3_b[i][None, :])
        x = _final(dest, x1, info, y, p[i].reshape(T, -1), wf).reshape(B, S, D)
    return x
```

```python
import functools

import jax
import jax.numpy as jnp
from jax import lax
from jax.experimental import pallas as pl
from jax.experimental.pallas import tpu as pltpu

F32 = jnp.float32
BF16 = jnp.bfloat16

MLA_HEADS = 8
QK_NOPE = 64
QK_ROPE = 32
V_HEAD = 64
Q_RANK = 256
KV_RANK = 128
ROPE_THETA = 10000.0
MLA_OUT = MLA_HEADS * V_HEAD
GM_GROUPS = 8
GM_CH = 64
GM_OUT = GM_GROUPS * GM_CH
CHUNK = 128
N_GROUPS = 4
EXP_PER_GROUP = 8
N_EXPERTS = N_GROUPS * EXP_PER_GROUP
GROUP_SHIFT = EXP_PER_GROUP.bit_length() - 1
assert EXP_PER_GROUP == 1 << GROUP_SHIFT
TOP_K = 2
EPS = 1e-6
DEPTH = 1
ALPHA = (2.0 * DEPTH) ** 0.25
SM_SCALE = (QK_NOPE + QK_ROPE) ** -0.5
LOG2E = 1.4426950408889634
MASK_VALUE = -1e30

LANES = 128
SUBLANES = 8
TOKEN_ROWS = 8
ONES_ROWS = 16
VMEM_LIMIT = 56 * 1024 * 1024

PREP_ROWS = 512
ATTN_ROWS = 256
ATTN_TILES = 4
ROUTE_ROWS = 2048
RANK_CHUNK = 256
MOVE_ROWS = 256
MOVE_UNROLL = 8
EXPERT_ROWS = 256

C_Q = 0
C_KV = C_Q + Q_RANK
C_KR = C_KV + KV_RANK
C_U = C_KR + LANES
C_V = C_U + GM_OUT
C_END = C_V + GM_OUT
HP = MLA_HEADS * LANES

I_E0, I_E1, I_R0, I_R1, I_G0, I_G1 = range(6)
ROUTE_OUT = 48


def _rms(v, g):
    return v * lax.rsqrt(jnp.mean(v * v, axis=-1, keepdims=True) + EPS) * g


def _ln(v, g, b):
    mu = jnp.mean(v, axis=-1, keepdims=True)
    d = v - mu
    var = jnp.mean(d * d, axis=-1, keepdims=True)
    return d * lax.rsqrt(var + EPS) * g + b


def _dot(a, b):
    return jnp.dot(a, b, preferred_element_type=F32)


def _prep_kernel(x_ref, pos_ref, win_ref, qg_ref, wq_ref, kvg_ref, wk_ref, wv_ref, inv_ref, rope_ref, one_ref,
                 lng_ref, lnb_ref, gavg_ref, ws_ref, bias_ref, gog_ref,
                 q_ref, k_ref, vt_ref, g_ref):
    rows = x_ref.shape[1]
    h = _dot(x_ref[0].astype(BF16), win_ref[...])

    ang = inv_ref[...] * pos_ref[0, 0].astype(F32)
    parts = []
    for t in (jnp.cos(ang), jnp.sin(ang)):
        hi = t.astype(BF16).astype(F32)
        parts += [hi, t - hi]
    tabs = _dot(jnp.concatenate(parts, axis=0).T.astype(BF16), rope_ref[...])
    cos_t = tabs[:, :LANES] + one_ref[...]
    sin_a = tabs[:, LANES:2 * LANES]
    sin_b = tabs[:, 2 * LANES:]
    half = QK_ROPE // 2

    def rotate(v):
        return v * cos_t + pltpu.roll(v, LANES - half, 1) * sin_a + pltpu.roll(v, half, 1) * sin_b

    cq = _rms(h[:, C_Q:C_Q + Q_RANK], qg_ref[...]).astype(BF16)
    q2 = _dot(cq, wq_ref[...])
    for hd in range(MLA_HEADS):
        lo = hd * LANES
        q_ref[0, :, lo:lo + LANES] = (rotate(q2[:, lo:lo + LANES]) * (SM_SCALE * LOG2E)).astype(BF16)

    ckv = _rms(h[:, C_KV:C_KV + KV_RANK], kvg_ref[...]).astype(BF16)
    kp = _dot(ckv, wk_ref[...])
    kr = rotate(h[:, C_KR:C_KR + LANES])
    for hd in range(MLA_HEADS):
        lo = hd * LANES
        k_ref[0, :, lo:lo + LANES] = (kp[:, lo:lo + LANES] + kr).astype(BF16)
    vp = _dot(ckv, wv_ref[...])
    for kb in range(rows // ATTN_ROWS):
        vt_ref[0, kb] = vp[kb * ATTN_ROWS:(kb + 1) * ATTN_ROWS].T.astype(BF16)

    u = jax.nn.gelu(h[:, C_U:C_U + GM_OUT])
    vv = jax.nn.gelu(h[:, C_V:C_V + GM_OUT])
    mu = _dot(vv.astype(BF16), gavg_ref[...])
    d = vv - mu
    var = _dot((d * d).astype(BF16), gavg_ref[...])
    vn = (d * lax.rsqrt(var + EPS) * lng_ref[...] + lnb_ref[...]).astype(BF16)

    tri = lax.broadcasted_iota(jnp.int32, (CHUNK, CHUNK), 0) >= lax.broadcasted_iota(jnp.int32, (CHUNK, CHUNK), 1)
    wm = [jnp.where(tri, ws_ref[g], 0.0).astype(BF16) for g in range(GM_GROUPS)]
    low_half = lax.broadcasted_iota(jnp.int32, (CHUNK, LANES), 1) < GM_CH
    for c in range(rows // CHUNK):
        r0 = c * CHUNK
        parts = []
        for pr in range(GM_GROUPS // 2):
            tile = vn[r0:r0 + CHUNK, pr * LANES:(pr + 1) * LANES]
            parts.append(jnp.where(low_half, _dot(wm[2 * pr], tile), _dot(wm[2 * pr + 1], tile)))
        sg = jnp.concatenate(parts, axis=1) + bias_ref[...]
        gm = u[r0:r0 + CHUNK] * sg
        g_ref[0, r0:r0 + CHUNK, :] = _rms(gm, gog_ref[...]).astype(BF16)


def _prep(x, pos4, w):
    B, S, D = x.shape
    ts = PREP_ROWS
    full = lambda a: pl.BlockSpec(a.shape, lambda b, i: (0,) * a.ndim)
    consts = [w["win"], w["qg"], w["wq"], w["kvg"], w["wk"], w["wv"], w["inv"], w["rope"], w["one"],
              w["lng"], w["lnb"], w["gavg"], w["ws"], w["bias"], w["gog"]]
    return pl.pallas_call(
        _prep_kernel,
        grid=(B, S // ts),
        in_specs=[pl.BlockSpec((1, ts, D), lambda b, i: (b, i, 0)),
                  pl.BlockSpec((1, 1, 1, ts), lambda b, i: (b, i, 0, 0))] + [full(a) for a in consts],
        out_specs=[pl.BlockSpec((1, ts, HP), lambda b, i: (b, i, 0)),
                   pl.BlockSpec((1, ts, HP), lambda b, i: (b, i, 0)),
                   pl.BlockSpec((1, ts // ATTN_ROWS, MLA_OUT, ATTN_ROWS), lambda b, i: (b, i, 0, 0)),
                   pl.BlockSpec((1, ts, GM_OUT), lambda b, i: (b, i, 0))],
        out_shape=[jax.ShapeDtypeStruct((B, S, HP), BF16)] * 2
        + [jax.ShapeDtypeStruct((B, S // ATTN_ROWS, MLA_OUT, ATTN_ROWS), BF16),
           jax.ShapeDtypeStruct((B, S, GM_OUT), BF16)],
        compiler_params=pltpu.CompilerParams(dimension_semantics=("parallel", "parallel"),
                                             vmem_limit_bytes=VMEM_LIMIT),
        name="prep",
    )(x, pos4, *consts)


def _attn_kernel(q_ref, k_ref, vt_ref, g_ref, x_ref, woa_ref, wog_ref, mog_ref, l1g_ref, l1b_ref,
                 o_ref, m_scr, acc_scr, sa_scr, sb_scr):
    pid = pl.program_id(1)
    tq = ATTN_ROWS
    tk = tq
    key = lax.broadcasted_iota(jnp.int32, (tk, tq), 0)
    qry = lax.broadcasted_iota(jnp.int32, (tk, tq), 1)
    diag_mask = key <= qry
    ones = jnp.ones((ONES_ROWS, tk), BF16)

    def tile(t):
        r0 = t * tq
        i = ATTN_TILES * pid + t
        odd = t % 2 == 1
        m_scr[...] = jnp.full(m_scr.shape, MASK_VALUE, F32)
        acc_scr[...] = jnp.zeros(acc_scr.shape, F32)

        def scores(j, s_scr):
            k0 = pl.multiple_of(j * tk, tk)
            for hd in range(MLA_HEADS):
                lo = hd * LANES
                qh = q_ref[0, r0:r0 + tq, lo:lo + LANES]
                kj = k_ref[0, pl.ds(k0, tk), lo:lo + LANES]
                s_scr[hd] = lax.dot_general(kj, qh, (((1,), (1,)), ((), ())), preferred_element_type=F32)

        def update(j, s_scr, masked):
            for hd in range(MLA_HEADS):
                s = s_scr[hd]
                vt = vt_ref[0, j, hd * V_HEAD:(hd + 1) * V_HEAD, :]
                if masked:
                    s = jnp.where(diag_mask, s, MASK_VALUE)
                m_prev = m_scr[hd]
                m_new = jnp.maximum(m_prev, jnp.max(s, axis=0, keepdims=True))
                p = jnp.exp2(s - m_new).astype(BF16)
                scale = jnp.exp2(m_prev - m_new)
                acc_scr[hd] = scale * acc_scr[hd] + _dot(jnp.concatenate([vt, ones], axis=0), p)
                m_scr[hd] = m_new

        def pair(jj, c):
            j = 2 * jj
            scores(j + 1, sb_scr)
            update(j, sa_scr, False)
            scores(j + 2, sa_scr)
            update(j + 1, sb_scr, False)
            return c

        scores(0, sa_scr)
        lax.fori_loop(0, (ATTN_TILES // 2) * pid + t // 2, pair, 0)
        if odd:
            scores(i, sb_scr)
            update(i - 1, sa_scr, False)
            update(i, sb_scr, True)
        else:
            update(i, sa_scr, True)

        at = jnp.concatenate([acc_scr[hd, :V_HEAD] / acc_scr[hd, V_HEAD:V_HEAD + 1] for hd in range(MLA_HEADS)],
                             axis=0)
        at = at * lax.rsqrt(jnp.mean(at * at, axis=0, keepdims=True) + EPS) * mog_ref[...]
        mix = _dot(at.T.astype(BF16), woa_ref[...]) + _dot(g_ref[0, r0:r0 + tq, :], wog_ref[...])
        o_ref[0, r0:r0 + tq, :] = _ln(ALPHA * x_ref[0, r0:r0 + tq, :] + mix, l1g_ref[...], l1b_ref[...])

    for t in range(ATTN_TILES):
        tile(t)


def _attn(q, k, vt, g, x, w):
    B, S, D = x.shape
    tq = ATTN_ROWS
    rows = ATTN_TILES * tq
    full = lambda a: pl.BlockSpec(a.shape, lambda b, i: (0,) * a.ndim)
    consts = [w["woa"], w["wog"], w["mog"], w["l1g"], w["l1b"]]
    return pl.pallas_call(
        _attn_kernel,
        grid=(B, S // rows),
        in_specs=[pl.BlockSpec((1, rows, HP), lambda b, i: (b, i, 0)),
                  pl.BlockSpec((1, S, HP), lambda b, i: (b, 0, 0)),
                  pl.BlockSpec((1,) + vt.shape[1:], lambda b, i: (b, 0, 0, 0)),
                  pl.BlockSpec((1, rows, GM_OUT), lambda b, i: (b, i, 0)),
                  pl.BlockSpec((1, rows, D), lambda b, i: (b, i, 0))] + [full(a) for a in consts],
        out_specs=pl.BlockSpec((1, rows, D), lambda b, i: (b, i, 0)),
        out_shape=jax.ShapeDtypeStruct((B, S, D), F32),
        scratch_shapes=[pltpu.VMEM((MLA_HEADS, 1, tq), F32),
                        pltpu.VMEM((MLA_HEADS, V_HEAD + ONES_ROWS, tq), F32),
                        pltpu.VMEM((MLA_HEADS, tq, tq), F32), pltpu.VMEM((MLA_HEADS, tq, tq), F32)],
        compiler_params=pltpu.CompilerParams(dimension_semantics=("parallel", "parallel"),
                                             vmem_limit_bytes=VMEM_LIMIT),
        name="attn",
    )(q, k, vt, g, x, *consts)


def _route_kernel(x_ref, wr_ref, br_ref, info_ref, infot_ref, cnt_ref, carry_scr, tri_scr):
    step = pl.program_id(0)
    tt = x_ref.shape[0]

    @pl.when(step == 0)
    def _():
        carry_scr[...] = jnp.zeros_like(carry_scr)
        s = lax.broadcasted_iota(jnp.int32, tri_scr.shape, 0)
        t = lax.broadcasted_iota(jnp.int32, tri_scr.shape, 1)
        tri_scr[...] = jnp.where(s < t, 1.0, 0.0).astype(BF16)

    x = x_ref[...]
    xh = x.astype(BF16)
    xl = (x - xh.astype(F32)).astype(BF16)
    wr = wr_ref[...]
    wh = wr.astype(BF16)
    wl = (wr - wh.astype(F32)).astype(BF16)
    nt = (((1,), (1,)), ((), ()))
    by_xh = lax.dot_general(jnp.concatenate([wh, wl], axis=0), xh, nt, preferred_element_type=F32)
    logits = (by_xh[:ROUTE_OUT] + lax.dot_general(wh, xl, nt, preferred_element_type=F32)
              + by_xh[ROUTE_OUT:]) + br_ref[...]
    neg = jnp.float32(-jnp.inf)

    lg = logits[N_EXPERTS:N_EXPERTS + SUBLANES]
    grow = lax.broadcasted_iota(jnp.int32, lg.shape, 0)
    lg = jnp.where(grow < N_GROUPS, lg, neg)
    gmax = jnp.max(lg, axis=0, keepdims=True)
    g_idx = jnp.min(jnp.where(lg == gmax, grow, SUBLANES), axis=0, keepdims=True)
    g_p = 1.0 / jnp.sum(jnp.exp(lg - gmax), axis=0, keepdims=True)

    le = logits[:N_EXPERTS]
    row = lax.broadcasted_iota(jnp.int32, le.shape, 0)
    le = jnp.where((row >> GROUP_SHIFT) == g_idx, le, neg)
    m1 = jnp.max(le, axis=0, keepdims=True)
    i1 = jnp.min(jnp.where(le == m1, row, N_EXPERTS), axis=0, keepdims=True)
    le2 = jnp.where(row == i1, neg, le)
    m2 = jnp.max(le2, axis=0, keepdims=True)
    i2 = jnp.min(jnp.where(le2 == m2, row, N_EXPERTS), axis=0, keepdims=True)
    e2 = jnp.exp(m2 - m1)
    gate0 = g_p / (1.0 + e2)
    gate1 = g_p * e2 / (1.0 + e2)

    hit1 = row == i1
    hit2 = row == i2
    onehot = jnp.where(hit1 | hit2, 1.0, 0.0)
    rc = RANK_CHUNK
    chunks = [onehot[:, c * rc:(c + 1) * rc] for c in range(tt // rc)]
    inside = _dot(jnp.concatenate(chunks, axis=0).astype(BF16), tri_scr[...])
    seen = carry_scr[...]
    parts = []
    for c, chunk in enumerate(chunks):
        parts.append(inside[c * N_EXPERTS:(c + 1) * N_EXPERTS] + seen)
        seen = seen + jnp.sum(chunk, axis=1, keepdims=True)
    before = jnp.concatenate(parts, axis=1)
    rank0 = jnp.sum(jnp.where(hit1, before, 0.0), axis=0, keepdims=True)
    rank1 = jnp.sum(jnp.where(hit2, before, 0.0), axis=0, keepdims=True)
    carry_scr[...] = seen
    cnt_ref[...] = jnp.broadcast_to(seen, cnt_ref.shape)

    fields = jnp.concatenate([i1.astype(F32), i2.astype(F32), rank0, rank1, gate0, gate1,
                              jnp.zeros((SUBLANES - 6, tt), F32)], axis=0)
    infot_ref[0] = fields
    info_ref[...] = jnp.concatenate([fields, jnp.zeros((LANES - SUBLANES, tt), F32)], axis=0).T


def _route(x1, wr, br):
    T, D = x1.shape
    tt = ROUTE_ROWS
    return pl.pallas_call(
        _route_kernel,
        grid=(T // tt,),
        in_specs=[pl.BlockSpec((tt, D), lambda i: (i, 0)),
                  pl.BlockSpec(wr.shape, lambda i: (0, 0)),
                  pl.BlockSpec(br.shape, lambda i: (0, 0))],
        out_specs=[pl.BlockSpec((tt, LANES), lambda i: (i, 0)),
                   pl.BlockSpec((1, SUBLANES, tt), lambda i: (i, 0, 0)),
                   pl.BlockSpec((N_EXPERTS, LANES), lambda i: (0, 0))],
        out_shape=[jax.ShapeDtypeStruct((T, LANES), F32), jax.ShapeDtypeStruct((T // tt, SUBLANES, tt), F32),
                   jax.ShapeDtypeStruct((N_EXPERTS, LANES), F32)],
        scratch_shapes=[pltpu.VMEM((N_EXPERTS, 1), F32), pltpu.VMEM((RANK_CHUNK, RANK_CHUNK), BF16)],
        compiler_params=pltpu.CompilerParams(dimension_semantics=("arbitrary",), vmem_limit_bytes=VMEM_LIMIT),
        name="route",
    )(x1, wr, br)


def _to_token_tiles(dst_ref, val):
    dst_ref[...] = val.astype(BF16).reshape(dst_ref.shape)


def _from_token_tiles(src_ref, rows):
    return src_ref[...].reshape(rows, TOKEN_ROWS * LANES)


def _to_token_tiles_f32(dst_ref, val):
    rows = val.shape[0]
    for c in range(TOKEN_ROWS):
        dst_ref[pl.ds(c, rows, stride=TOKEN_ROWS), :] = val[:, c * LANES:(c + 1) * LANES]


def _from_token_tiles_f32(src_ref, rows):
    return jnp.concatenate([src_ref[pl.ds(c, rows, stride=TOKEN_ROWS), :] for c in range(TOKEN_ROWS)], axis=1)


def _tile_copy(src_ref, src_row, dst_ref, dst_row, sem):
    return pltpu.make_async_copy(src_ref.at[pl.ds(pl.multiple_of(src_row, TOKEN_ROWS), TOKEN_ROWS)],
                                 dst_ref.at[pl.ds(pl.multiple_of(dst_row, TOKEN_ROWS), TOKEN_ROWS)], sem)


def _dispatch_kernel(seg_ref, dest_ref, x0_ref, xn_ref, buf_ref, stage_scr, zero_scr, sem, zero_sem, *, n_steps):
    i = pl.program_id(0)
    rows = xn_ref.shape[0]
    cur = i % 3
    nxt = (i + 1) % 3

    @pl.when(i == 0)
    def _():
        zero_scr[...] = jnp.zeros(zero_scr.shape, BF16)

        block = EXPERT_ROWS * TOKEN_ROWS
        n_blocks = buf_ref.shape[0] // block

        def clear_rows(first):
            return pltpu.make_async_copy(zero_scr, buf_ref.at[pl.ds(pl.multiple_of(first, SUBLANES), block)], zero_sem)

        def clear(e):
            return clear_rows((seg_ref[0, e] - EXPERT_ROWS) * TOKEN_ROWS)

        def start_tail(b, c):
            clear_rows(b * block).start()
            return c

        def wait_tail(b, c):
            clear_rows(b * block).wait()
            return c

        for e in range(N_EXPERTS):
            pl.when(seg_ref[1, e] > 0)(lambda e=e: clear(e).start())
        lax.fori_loop(seg_ref[2, 0], n_blocks, start_tail, 0)
        for e in range(N_EXPERTS):
            pl.when(seg_ref[1, e] > 0)(lambda e=e: clear(e).wait())
        lax.fori_loop(seg_ref[2, 0], n_blocks, wait_tail, 0)

        _to_token_tiles(stage_scr.at[0], x0_ref[...])

    def drain(s):
        for _ in range(TOP_K):
            pltpu.make_async_copy(stage_scr.at[s], stage_scr.at[s], sem.at[s]).wait()

    @pl.when(i >= 2)
    def _():
        drain(nxt)

    _to_token_tiles(stage_scr.at[nxt], xn_ref[...])
    for r in range(rows):
        for kk in range(TOP_K):
            _tile_copy(stage_scr.at[cur], r * TOKEN_ROWS, buf_ref, dest_ref[0, 0, kk * rows + r],
                       sem.at[cur]).start(priority=kk)

    @pl.when(i == n_steps - 1)
    def _():
        drain(cur)
        if n_steps >= 2:
            drain((i + 2) % 3)


def _dispatch(seg, dest3, x1, n_rows):
    T, D = x1.shape
    td = MOVE_ROWS
    n_steps = T // td
    grid_spec = pltpu.PrefetchScalarGridSpec(
        num_scalar_prefetch=1,
        grid=(n_steps,),
        in_specs=[pl.BlockSpec((1, 1, TOP_K * td), lambda i, seg: (i, 0, 0), memory_space=pltpu.SMEM),
                  pl.BlockSpec((td, D), lambda i, seg: (0, 0)),
                  pl.BlockSpec((td, D), lambda i, seg: (jnp.minimum(i + 1, n_steps - 1), 0))],
        out_specs=pl.BlockSpec(memory_space=pl.ANY),
        scratch_shapes=[pltpu.VMEM((3, td * TOKEN_ROWS, LANES), BF16),
                        pltpu.VMEM((EXPERT_ROWS * TOKEN_ROWS, LANES), BF16),
                        pltpu.SemaphoreType.DMA((3,)), pltpu.SemaphoreType.DMA(())],
    )
    return pl.pallas_call(
        functools.partial(_dispatch_kernel, n_steps=n_steps),
        grid_spec=grid_spec,
        out_shape=jax.ShapeDtypeStruct((n_rows * TOKEN_ROWS, LANES), BF16),
        compiler_params=pltpu.CompilerParams(dimension_semantics=("arbitrary",), vmem_limit_bytes=VMEM_LIMIT),
        name="dispatch",
    )(seg, dest3, x1, x1)


def _expert_kernel(be_ref, ne_ref, nu_ref, buf0_ref, bufa_ref, bufb_ref, wg_hbm, wu_hbm, wd_hbm, y_hbm,
                   sg_scr, su_scr, sd_scr, wg_scr, wu_scr, wd_scr, xa_scr, xb_scr, y_scr, cur_ref, sem, ysem):
    step = pl.program_id(0)
    last_step = pl.num_programs(0) - 1
    bm = EXPERT_ROWS
    half = bm * TOKEN_ROWS
    slot = step % 2
    y_ref = y_scr.at[slot]

    def write_back(s, at_step):
        first = pl.multiple_of(at_step * (2 * half), 2 * half)
        return pltpu.make_async_copy(y_scr.at[s], y_hbm.at[pl.ds(first, 2 * half)], ysem.at[s])

    @pl.when(step >= 2)
    def _():
        write_back(slot, step - 2).wait()

    def fetch(expert, s):
        return (pltpu.make_async_copy(wg_hbm.at[expert], sg_scr.at[s], sem.at[s, 0]),
                pltpu.make_async_copy(wu_hbm.at[expert], su_scr.at[s], sem.at[s, 1]),
                pltpu.make_async_copy(wd_hbm.at[expert], sd_scr.at[s], sem.at[s, 2]))

    @pl.when(step == 0)
    def _():
        cur_ref[0] = 0
        for c in fetch(be_ref[0], 0):
            c.start()
        xa_scr[...] = _from_token_tiles(buf0_ref, bm)

    def load_weights(blk):
        e = be_ref[blk]

        @pl.when((blk == 0) | (be_ref[jnp.maximum(blk - 1, 0)] != e))
        def _():
            s = cur_ref[0]
            for c in fetch(e, s):
                c.wait()
            wg_scr[...] = sg_scr[s].astype(BF16)
            wu_scr[...] = su_scr[s].astype(BF16)
            wd_scr[...] = sd_scr[s].astype(BF16)
            nxt = ne_ref[blk]

            @pl.when(nxt >= 0)
            def _():
                for c in fetch(nxt, 1 - s):
                    c.start()

            cur_ref[0] = 1 - s

    def compute(x_scr, nxt_ref, nxt_scr, out_rows):
        nxt_scr[...] = _from_token_tiles(nxt_ref, bm)
        xb = x_scr[...]
        hidden = jax.nn.silu(_dot(xb, wg_scr[...])) * _dot(xb, wu_scr[...])
        _to_token_tiles_f32(y_ref.at[out_rows], _dot(hidden.astype(BF16), wd_scr[...]))

    def run(blk, x_scr, nxt_ref, nxt_scr, out_rows):
        @pl.when(blk < nu_ref[0])
        def _():
            compute(x_scr, nxt_ref, nxt_scr, out_rows)

        @pl.when(blk >= nu_ref[0])
        def _():
            y_ref[out_rows, :] = jnp.zeros((half, LANES), F32)

    blk_a, blk_b = 2 * step, 2 * step + 1
    rows_a, rows_b = pl.ds(0, half), pl.ds(half, half)
    load_weights(blk_a)
    same = (be_ref[blk_a] == be_ref[blk_b]) & (blk_b < nu_ref[0])

    @pl.when(same)
    def _():
        compute(xa_scr, bufa_ref, xb_scr, rows_a)
        compute(xb_scr, bufb_ref, xa_scr, rows_b)

    @pl.when(jnp.logical_not(same))
    def _():
        run(blk_a, xa_scr, bufa_ref, xb_scr, rows_a)
        load_weights(blk_b)
        run(blk_b, xb_scr, bufb_ref, xa_scr, rows_b)

    write_back(slot, step).start(priority=1)

    @pl.when(step == last_step)
    def _():
        write_back(slot, step).wait()

        @pl.when(step >= 1)
        def _():
            write_back(1 - slot, step - 1).wait()


def _experts(block_expert, next_expert, n_used, buf, w_gate, w_up, w_down):
    bm = EXPERT_ROWS
    D, ff = w_gate.shape[1:]
    n_blocks = buf.shape[0] // (bm * TOKEN_ROWS)
    assert n_blocks % 2 == 0
    last = n_blocks - 1
    grid_spec = pltpu.PrefetchScalarGridSpec(
        num_scalar_prefetch=3,
        grid=(n_blocks // 2,),
        in_specs=[pl.BlockSpec((bm * TOKEN_ROWS, LANES), lambda s, *_: (0, 0)),
                  pl.BlockSpec((bm * TOKEN_ROWS, LANES), lambda s, *_: (2 * s + 1, 0)),
                  pl.BlockSpec((bm * TOKEN_ROWS, LANES), lambda s, *_: (jnp.minimum(2 * s + 2, last), 0)),
                  pl.BlockSpec(memory_space=pl.ANY),
                  pl.BlockSpec(memory_space=pl.ANY),
                  pl.BlockSpec(memory_space=pl.ANY)],
        out_specs=pl.BlockSpec(memory_space=pl.ANY),
        scratch_shapes=[pltpu.VMEM((2, D, ff), F32), pltpu.VMEM((2, D, ff), F32), pltpu.VMEM((2, ff, D), F32),
                        pltpu.VMEM((D, ff), BF16), pltpu.VMEM((D, ff), BF16), pltpu.VMEM((ff, D), BF16),
                        pltpu.VMEM((bm, D), BF16), pltpu.VMEM((bm, D), BF16),
                        pltpu.VMEM((2, 2 * bm * TOKEN_ROWS, LANES), F32),
                        pltpu.SMEM((1,), jnp.int32), pltpu.SemaphoreType.DMA((2, 3)), pltpu.SemaphoreType.DMA((2,))],
    )
    return pl.pallas_call(
        _expert_kernel,
        grid_spec=grid_spec,
        out_shape=jax.ShapeDtypeStruct(buf.shape, F32),
        compiler_params=pltpu.CompilerParams(dimension_semantics=("arbitrary",), vmem_limit_bytes=VMEM_LIMIT),
        name="experts",
    )(block_expert, next_expert, n_used, buf, buf, buf, w_gate, w_up, w_down)


def _final_kernel(d0_ref, d1_ref, d2_ref, x_ref, info_ref, y_ref, p_ref, wpg_ref, bpg_ref, wpp_ref,
                  l2g_ref, l2b_ref, l3g_ref, l3b_ref, o_ref, rows_scr, sem):
    i = pl.program_id(0)
    last = pl.num_programs(0) - 1
    rows = x_ref.shape[0]
    slot = i % 3
    ahead = (i + 2) % 3

    def row_copy(dref, s, r, kk):
        return _tile_copy(y_ref, dref[0, 0, kk * rows + r], rows_scr.at[s, kk], r * TOKEN_ROWS, sem.at[s])

    def landed(s):
        pltpu.make_async_copy(rows_scr.at[s], rows_scr.at[s], sem.at[s]).wait()

    @pl.when(i == 0)
    def _():
        def start(c, carry):
            for u in range(MOVE_UNROLL):
                for kk in range(TOP_K):
                    row_copy(d0_ref, 0, c * MOVE_UNROLL + u, kk).start(priority=kk)
                    row_copy(d1_ref, 1, c * MOVE_UNROLL + u, kk).start(priority=kk)
            return carry

        lax.fori_loop(0, rows // MOVE_UNROLL, start, 0)

    landed(slot)
    info = info_ref[...]
    gate0 = info[:, I_G0:I_G0 + 1]
    gate1 = info[:, I_G1:I_G1 + 1]
    moe = (_from_token_tiles_f32(rows_scr.at[slot, 0], rows) * gate0
           + _from_token_tiles_f32(rows_scr.at[slot, 1], rows) * gate1)

    for r in range(rows):
        for kk in range(TOP_K):
            row_copy(d2_ref, ahead, r, kk).start(priority=1 if r % 4 == 0 else kk)

    pp = _dot(p_ref[...].astype(BF16), wpp_ref[...])
    x2 = _ln(ALPHA * x_ref[...] + moe, l2g_ref[...], l2b_ref[...])
    gate = jax.nn.sigmoid(_dot(x2.astype(BF16), wpg_ref[...]) + bpg_ref[...])
    o_ref[...] = _ln(ALPHA * x2 + gate * pp, l3g_ref[...], l3b_ref[...])

    @pl.when(i == last)
    def _():
        landed((i + 1) % 3)
        landed(ahead)


def _final(dest3, x1, info, y, p2, w):
    T, D = x1.shape
    tc = MOVE_ROWS
    pd = p2.shape[1]
    full = lambda a: pl.BlockSpec(a.shape, lambda i: (0,) * a.ndim)
    consts = [w["wpg"], w["bpg"], w["wpp"], w["l2g"], w["l2b"], w["l3g"], w["l3b"]]
    last = T // tc - 1
    assert last >= 2
    return pl.pallas_call(
        _final_kernel,
        grid=(T // tc,),
        in_specs=[pl.BlockSpec((1, 1, TOP_K * tc), lambda i: (i, 0, 0), memory_space=pltpu.SMEM),
                  pl.BlockSpec((1, 1, TOP_K * tc), lambda i: (jnp.minimum(i + 1, last), 0, 0), memory_space=pltpu.SMEM),
                  pl.BlockSpec((1, 1, TOP_K * tc), lambda i: (jnp.minimum(i + 2, last), 0, 0), memory_space=pltpu.SMEM),
                  pl.BlockSpec((tc, D), lambda i: (i, 0)),
                  pl.BlockSpec((tc, LANES), lambda i: (i, 0)),
                  pl.BlockSpec(memory_space=pl.ANY),
                  pl.BlockSpec((tc, pd), lambda i: (i, 0))] + [full(a) for a in consts],
        out_specs=pl.BlockSpec((tc, D), lambda i: (i, 0)),
        out_shape=jax.ShapeDtypeStruct((T, D), F32),
        scratch_shapes=[pltpu.VMEM((3, TOP_K, tc * TOKEN_ROWS, LANES), F32), pltpu.SemaphoreType.DMA((3,))],
        compiler_params=pltpu.CompilerParams(dimension_semantics=("arbitrary",), vmem_limit_bytes=VMEM_LIMIT),
        name="final",
    )(dest3, dest3, dest3, x1, info, y, p2, *consts)


def _pad_heads(a, width):
    lead = a.shape[:-1]
    a = a.reshape(lead + (MLA_HEADS, width))
    a = jnp.pad(a, [(0, 0)] * len(lead) + [(0, 0), (0, LANES - width)])
    return a.reshape(lead + (HP,))


def _layer_weights(w_in, q_norm_g, w_q_up, kv_norm_g, w_kv_up, gm_ln_g, gm_ln_b, gm_w_s, gm_b_s,
                   mla_out_g, gm_out_g, w_o, ln1_g, ln1_b):
    D = w_in.shape[0]
    half = QK_ROPE // 2
    c1, c2, c3 = Q_RANK, Q_RANK + KV_RANK, Q_RANK + KV_RANK + QK_ROPE
    zeros = lambda *s: jnp.zeros(s, F32)
    kr = jnp.concatenate([zeros(D, QK_NOPE), w_in[:, c2:c3], zeros(D, LANES - QK_NOPE - QK_ROPE)], axis=1)
    win = jnp.concatenate([w_in[:, :c2], kr, w_in[:, c3:]], axis=1).astype(BF16)
    wq = _pad_heads(w_q_up, QK_NOPE + QK_ROPE).astype(BF16)

    wkv3 = w_kv_up.reshape(KV_RANK, MLA_HEADS, QK_NOPE + V_HEAD)
    wk = _pad_heads(wkv3[..., :QK_NOPE].reshape(KV_RANK, -1), QK_NOPE).astype(BF16)
    wv = wkv3[..., QK_NOPE:].reshape(KV_RANK, -1).astype(BF16)

    inv = (ROPE_THETA ** (-jnp.arange(0, QK_ROPE, 2, dtype=F32) / QK_ROPE))[:, None]
    eye = jnp.eye(half, dtype=F32)
    first = jnp.pad(eye, ((0, 0), (QK_NOPE, LANES - QK_NOPE - half)))
    second = jnp.pad(eye, ((0, 0), (QK_NOPE + half, LANES - QK_NOPE - QK_ROPE)))
    zero = jnp.zeros_like(first)
    cos_rows = jnp.concatenate([first + second, zero, zero], axis=1)
    sin_rows = jnp.concatenate([zero, -first, second], axis=1)
    rope = jnp.concatenate([cos_rows, cos_rows, sin_rows, sin_rows], axis=0).astype(BF16)
    lane = jnp.arange(LANES)
    one = jnp.where((lane >= QK_NOPE) & (lane < QK_NOPE + QK_ROPE), 0.0, 1.0)[None, :]

    grp = jnp.arange(GM_OUT) // GM_CH
    gavg = jnp.where(grp[:, None] == grp[None, :], 1.0 / GM_CH, 0.0).astype(BF16)
    bias = jnp.repeat(gm_b_s.T, GM_CH, axis=1)

    woa = w_o[:MLA_OUT].astype(BF16)
    wog = w_o[MLA_OUT:].astype(BF16)
    return dict(win=win, qg=q_norm_g[None, :], wq=wq, kvg=kv_norm_g[None, :], wk=wk, wv=wv, inv=inv, rope=rope, one=one,
                lng=gm_ln_g[None, :], lnb=gm_ln_b[None, :], gavg=gavg, ws=gm_w_s, bias=bias, gog=gm_out_g[None, :],
                woa=woa, wog=wog, mog=mla_out_g[:, None], l1g=ln1_g[None, :], l1b=ln1_b[None, :])


def _moe(x1, w_rg, b_rg, w_re, b_re, w_gate, w_up, w_down):
    T, D = x1.shape
    pad = ROUTE_OUT - N_EXPERTS - N_GROUPS
    wr = jnp.concatenate([w_re.T, w_rg.T, jnp.zeros((pad, D), F32)], axis=0)
    br = jnp.concatenate([b_re, b_rg, jnp.zeros((pad,), F32)])[:, None]
    info, info_t, cnt = _route(x1, wr, br)

    bm = EXPERT_ROWS
    n_blocks = (T * TOP_K) // bm + N_EXPERTS
    counts = cnt[:, 0].astype(jnp.int32)
    padded = (counts + bm - 1) // bm * bm
    pad_ends = jnp.cumsum(padded)
    pad_starts = pad_ends - padded
    def dest_rows(e_lane, r_lane):
        e = info_t[:, e_lane, :].astype(jnp.int32)
        ids = jnp.arange(N_EXPERTS)[:, None, None]
        seg_start = jnp.sum(jnp.where(e[None] == ids, pad_starts[:, None, None], 0), axis=0)
        return ((seg_start + info_t[:, r_lane, :].astype(jnp.int32)) * TOKEN_ROWS).reshape(T // MOVE_ROWS, MOVE_ROWS)

    dest = jnp.concatenate([dest_rows(I_E0, I_R0), dest_rows(I_E1, I_R1)], axis=1)[:, None, :]
    block_start = jnp.arange(n_blocks, dtype=jnp.int32) * bm
    block_expert = jnp.minimum(jnp.sum(pad_ends[None, :] <= block_start[:, None], axis=1),
                               N_EXPERTS - 1).astype(jnp.int32)

    blk = jnp.arange(n_blocks)
    later = (blk[None, :] > blk[:, None]) & (block_expert[None, :] != block_expert[:, None])
    next_expert = jnp.min(jnp.where(later, block_expert[None, :], N_EXPERTS), axis=1)
    next_expert = jnp.where(next_expert == N_EXPERTS, -1, next_expert).astype(jnp.int32)
    n_used = (pad_ends[-1:] // bm).astype(jnp.int32)

    seg = jnp.stack([pad_ends, padded, jnp.broadcast_to(n_used, (N_EXPERTS,))]).astype(jnp.int32)
    buf = _dispatch(seg, dest, x1, n_blocks * bm)
    y = _experts(block_expert, next_expert, n_used, buf, w_gate, w_up, w_down)
    return info, dest, y


def kernel(x, p, positions, w_in, q_norm_g, w_q_up, kv_norm_g, w_kv_up, gm_ln_g, gm_ln_b, gm_w_s, gm_b_s, mla_out_g, gm_out_g, w_o, ln1_g, ln1_b, w_rg, b_rg, w_re, b_re, w_gate, w_up, w_down, ln2_g, ln2_b, w_pg, b_pg, w_pp, ln3_g, ln3_b):
    B, S, D = x.shape
    T = B * S
    assert S % PREP_ROWS == 0 and PREP_ROWS % ATTN_ROWS == 0 and PREP_ROWS % CHUNK == 0
    assert S % (ATTN_TILES * ATTN_ROWS) == 0 and ATTN_TILES % 2 == 0
    assert T % ROUTE_ROWS == 0 and ROUTE_ROWS % RANK_CHUNK == 0
    assert T % MOVE_ROWS == 0 and (T * TOP_K) % EXPERT_ROWS == 0
    assert D == TOKEN_ROWS * LANES and MOVE_ROWS % MOVE_UNROLL == 0
    pos4 = positions.reshape(B, S // PREP_ROWS, 1, PREP_ROWS)
    for i in range(DEPTH):
        w = _layer_weights(w_in[i], q_norm_g[i], w_q_up[i], kv_norm_g[i], w_kv_up[i], gm_ln_g[i], gm_ln_b[i],
                           gm_w_s[i], gm_b_s[i], mla_out_g[i], gm_out_g[i], w_o[i], ln1_g[i], ln1_b[i])
        q, k, vt, g = _prep(x, pos4, w)
        x1 = _attn(q, k, vt, g, x, w).reshape(T, D)
        info, dest, y = _moe(x1, w_rg[i], b_rg[i], w_re[i], b_re[i], w_gate[i], w_up[i], w_down[i])
        wf = dict(wpg=w_pg[i].astype(BF16), bpg=b_pg[i][None, :], wpp=w_pp[i].astype(BF16),
                  l2g=ln2_g[i][None, :], l2b=ln2_b[i][None, :], l3g=ln3_g[i][None, :], l3b=ln3_b[i][None, :])
        x = _final(dest, x1, info, y, p[i].reshape(T, -1), wf).reshape(B, S, D)
    return x
```

```python
import functools

import jax
import jax.numpy as jnp
from jax import lax
from jax.experimental import pallas as pl
from jax.experimental.pallas import tpu as pltpu

F32 = jnp.float32
BF16 = jnp.bfloat16

MLA_HEADS = 8
QK_NOPE = 64
QK_ROPE = 32
V_HEAD = 64
Q_RANK = 256
KV_RANK = 128
ROPE_THETA = 10000.0
MLA_OUT = MLA_HEADS * V_HEAD
GM_GROUPS = 8
GM_CH = 64
GM_OUT = GM_GROUPS * GM_CH
CHUNK = 128
N_GROUPS = 4
EXP_PER_GROUP = 8
N_EXPERTS = N_GROUPS * EXP_PER_GROUP
GROUP_SHIFT = EXP_PER_GROUP.bit_length() - 1
assert EXP_PER_GROUP == 1 << GROUP_SHIFT
TOP_K = 2
EPS = 1e-6
DEPTH = 1
ALPHA = (2.0 * DEPTH) ** 0.25
SM_SCALE = (QK_NOPE + QK_ROPE) ** -0.5
LOG2E = 1.4426950408889634
MASK_VALUE = -1e30

LANES = 128
SUBLANES = 8
TOKEN_ROWS = 8
ONES_ROWS = 16
VMEM_LIMIT = 56 * 1024 * 1024

PREP_ROWS = 1024
ATTN_ROWS = 256
ATTN_TILES = 4
ROUTE_ROWS = 2048
RANK_CHUNK = 256
MOVE_ROWS = 256
MOVE_UNROLL = 8
EXPERT_ROWS = 256

C_Q = 0
C_KV = C_Q + Q_RANK
C_KR = C_KV + KV_RANK
C_U = C_KR + LANES
C_V = C_U + GM_OUT
C_END = C_V + GM_OUT
HP = MLA_HEADS * LANES

I_E0, I_E1, I_R0, I_R1, I_G0, I_G1 = range(6)
ROUTE_OUT = 48


def _rms(v, g):
    return v * lax.rsqrt(jnp.mean(v * v, axis=-1, keepdims=True) + EPS) * g


def _ln(v, g, b):
    mu = jnp.mean(v, axis=-1, keepdims=True)
    d = v - mu
    var = jnp.mean(d * d, axis=-1, keepdims=True)
    return d * lax.rsqrt(var + EPS) * g + b


def _dot(a, b):
    return jnp.dot(a, b, preferred_element_type=F32)


def _prep_kernel(x_ref, pos_ref, win_ref, qg_ref, wq_ref, kvg_ref, wk_ref, wv_ref, inv_ref, rope_ref, one_ref,
                 lng_ref, lnb_ref, gavg_ref, ws_ref, bias_ref, gog_ref,
                 q_ref, k_ref, vt_ref, g_ref):
    rows = x_ref.shape[1]
    h = _dot(x_ref[0].astype(BF16), win_ref[...])

    ang = inv_ref[...] * pos_ref[0, 0].astype(F32)
    parts = []
    for t in (jnp.cos(ang), jnp.sin(ang)):
        hi = t.astype(BF16).astype(F32)
        parts += [hi, t - hi]
    tabs = _dot(jnp.concatenate(parts, axis=0).T.astype(BF16), rope_ref[...])
    cos_t = tabs[:, :LANES] + one_ref[...]
    sin_a = tabs[:, LANES:2 * LANES]
    sin_b = tabs[:, 2 * LANES:]
    half = QK_ROPE // 2

    def rotate(v):
        return v * cos_t + pltpu.roll(v, LANES - half, 1) * sin_a + pltpu.roll(v, half, 1) * sin_b

    cq = _rms(h[:, C_Q:C_Q + Q_RANK], qg_ref[...]).astype(BF16)
    q2 = _dot(cq, wq_ref[...])
    for hd in range(MLA_HEADS):
        lo = hd * LANES
        q_ref[0, :, lo:lo + LANES] = (rotate(q2[:, lo:lo + LANES]) * (SM_SCALE * LOG2E)).astype(BF16)

    ckv = _rms(h[:, C_KV:C_KV + KV_RANK], kvg_ref[...]).astype(BF16)
    kp = _dot(ckv, wk_ref[...])
    kr = rotate(h[:, C_KR:C_KR + LANES])
    for hd in range(MLA_HEADS):
        lo = hd * LANES
        k_ref[0, :, lo:lo + LANES] = (kp[:, lo:lo + LANES] + kr).astype(BF16)
    vp = _dot(ckv, wv_ref[...])
    for kb in range(rows // ATTN_ROWS):
        vt_ref[0, kb] = vp[kb * ATTN_ROWS:(kb + 1) * ATTN_ROWS].T.astype(BF16)

    u = jax.nn.gelu(h[:, C_U:C_U + GM_OUT])
    vv = jax.nn.gelu(h[:, C_V:C_V + GM_OUT])
    mu = _dot(vv.astype(BF16), gavg_ref[...])
    d = vv - mu
    var = _dot((d * d).astype(BF16), gavg_ref[...])
    vn = (d * lax.rsqrt(var + EPS) * lng_ref[...] + lnb_ref[...]).astype(BF16)

    tri = lax.broadcasted_iota(jnp.int32, (CHUNK, CHUNK), 0) >= lax.broadcasted_iota(jnp.int32, (CHUNK, CHUNK), 1)
    wm = [jnp.where(tri, ws_ref[g], 0.0).astype(BF16) for g in range(GM_GROUPS)]
    low_half = lax.broadcasted_iota(jnp.int32, (CHUNK, LANES), 1) < GM_CH
    for c in range(rows // CHUNK):
        r0 = c * CHUNK
        parts = []
        for pr in range(GM_GROUPS // 2):
            tile = vn[r0:r0 + CHUNK, pr * LANES:(pr + 1) * LANES]
            parts.append(jnp.where(low_half, _dot(wm[2 * pr], tile), _dot(wm[2 * pr + 1], tile)))
        sg = jnp.concatenate(parts, axis=1) + bias_ref[...]
        gm = u[r0:r0 + CHUNK] * sg
        g_ref[0, r0:r0 + CHUNK, :] = _rms(gm, gog_ref[...]).astype(BF16)


def _prep(x, pos4, w):
    B, S, D = x.shape
    ts = PREP_ROWS
    full = lambda a: pl.BlockSpec(a.shape, lambda b, i: (0,) * a.ndim)
    consts = [w["win"], w["qg"], w["wq"], w["kvg"], w["wk"], w["wv"], w["inv"], w["rope"], w["one"],
              w["lng"], w["lnb"], w["gavg"], w["ws"], w["bias"], w["gog"]]
    return pl.pallas_call(
        _prep_kernel,
        grid=(B, S // ts),
        in_specs=[pl.BlockSpec((1, ts, D), lambda b, i: (b, i, 0)),
                  pl.BlockSpec((1, 1, 1, ts), lambda b, i: (b, i, 0, 0))] + [full(a) for a in consts],
        out_specs=[pl.BlockSpec((1, ts, HP), lambda b, i: (b, i, 0)),
                   pl.BlockSpec((1, ts, HP), lambda b, i: (b, i, 0)),
                   pl.BlockSpec((1, ts // ATTN_ROWS, MLA_OUT, ATTN_ROWS), lambda b, i: (b, i, 0, 0)),
                   pl.BlockSpec((1, ts, GM_OUT), lambda b, i: (b, i, 0))],
        out_shape=[jax.ShapeDtypeStruct((B, S, HP), BF16)] * 2
        + [jax.ShapeDtypeStruct((B, S // ATTN_ROWS, MLA_OUT, ATTN_ROWS), BF16),
           jax.ShapeDtypeStruct((B, S, GM_OUT), BF16)],
        compiler_params=pltpu.CompilerParams(dimension_semantics=("parallel", "parallel"),
                                             vmem_limit_bytes=VMEM_LIMIT),
        name="prep",
    )(x, pos4, *consts)


def _attn_kernel(q_ref, k_ref, vt_ref, g_ref, x_ref, woa_ref, wog_ref, mog_ref, l1g_ref, l1b_ref,
                 o_ref, m_scr, acc_scr, sa_scr, sb_scr):
    pid = pl.program_id(1)
    tq = ATTN_ROWS
    tk = tq
    key = lax.broadcasted_iota(jnp.int32, (tk, tq), 0)
    qry = lax.broadcasted_iota(jnp.int32, (tk, tq), 1)
    diag_mask = key <= qry
    ones = jnp.ones((ONES_ROWS, tk), BF16)

    def tile(t):
        r0 = t * tq
        i = ATTN_TILES * pid + t
        odd = t % 2 == 1
        m_scr[...] = jnp.full(m_scr.shape, MASK_VALUE, F32)
        acc_scr[...] = jnp.zeros(acc_scr.shape, F32)

        def scores(j, s_scr):
            k0 = pl.multiple_of(j * tk, tk)
            for hd in range(MLA_HEADS):
                lo = hd * LANES
                qh = q_ref[0, r0:r0 + tq, lo:lo + LANES]
                kj = k_ref[0, pl.ds(k0, tk), lo:lo + LANES]
                s_scr[hd] = lax.dot_general(kj, qh, (((1,), (1,)), ((), ())), preferred_element_type=F32)

        def update(j, s_scr, masked):
            for hd in range(MLA_HEADS):
                s = s_scr[hd]
                vt = vt_ref[0, j, hd * V_HEAD:(hd + 1) * V_HEAD, :]
                if masked:
                    s = jnp.where(diag_mask, s, MASK_VALUE)
                m_prev = m_scr[hd]
                m_new = jnp.maximum(m_prev, jnp.max(s, axis=0, keepdims=True))
                p = jnp.exp2(s - m_new).astype(BF16)
                scale = jnp.exp2(m_prev - m_new)
                acc_scr[hd] = scale * acc_scr[hd] + _dot(jnp.concatenate([vt, ones], axis=0), p)
                m_scr[hd] = m_new

        def pair(jj, c):
            j = 2 * jj
            scores(j + 1, sb_scr)
            update(j, sa_scr, False)
            scores(j + 2, sa_scr)
            update(j + 1, sb_scr, False)
            return c

        scores(0, sa_scr)
        lax.fori_loop(0, (ATTN_TILES // 2) * pid + t // 2, pair, 0)
        if odd:
            scores(i, sb_scr)
            update(i - 1, sa_scr, False)
            update(i, sb_scr, True)
        else:
            update(i, sa_scr, True)

        at = jnp.concatenate([acc_scr[hd, :V_HEAD] / acc_scr[hd, V_HEAD:V_HEAD + 1] for hd in range(MLA_HEADS)],
                             axis=0)
        at = at * lax.rsqrt(jnp.mean(at * at, axis=0, keepdims=True) + EPS) * mog_ref[...]
        mix = _dot(at.T.astype(BF16), woa_ref[...]) + _dot(g_ref[0, r0:r0 + tq, :], wog_ref[...])
        o_ref[0, r0:r0 + tq, :] = _ln(ALPHA * x_ref[0, r0:r0 + tq, :] + mix, l1g_ref[...], l1b_ref[...])

    for t in range(ATTN_TILES):
        tile(t)


def _attn(q, k, vt, g, x, w):
    B, S, D = x.shape
    tq = ATTN_ROWS
    rows = ATTN_TILES * tq
    full = lambda a: pl.BlockSpec(a.shape, lambda b, i: (0,) * a.ndim)
    consts = [w["woa"], w["wog"], w["mog"], w["l1g"], w["l1b"]]
    return pl.pallas_call(
        _attn_kernel,
        grid=(B, S // rows),
        in_specs=[pl.BlockSpec((1, rows, HP), lambda b, i: (b, i, 0)),
                  pl.BlockSpec((1, S, HP), lambda b, i: (b, 0, 0)),
                  pl.BlockSpec((1,) + vt.shape[1:], lambda b, i: (b, 0, 0, 0)),
                  pl.BlockSpec((1, rows, GM_OUT), lambda b, i: (b, i, 0)),
                  pl.BlockSpec((1, rows, D), lambda b, i: (b, i, 0))] + [full(a) for a in consts],
        out_specs=pl.BlockSpec((1, rows, D), lambda b, i: (b, i, 0)),
        out_shape=jax.ShapeDtypeStruct((B, S, D), F32),
        scratch_shapes=[pltpu.VMEM((MLA_HEADS, 1, tq), F32),
                        pltpu.VMEM((MLA_HEADS, V_HEAD + ONES_ROWS, tq), F32),
                        pltpu.VMEM((MLA_HEADS, tq, tq), F32), pltpu.VMEM((MLA_HEADS, tq, tq), F32)],
        compiler_params=pltpu.CompilerParams(dimension_semantics=("parallel", "parallel"),
                                             vmem_limit_bytes=VMEM_LIMIT),
        name="attn",
    )(q, k, vt, g, x, *consts)


def _route_kernel(x_ref, wr_ref, br_ref, info_ref, infot_ref, cnt_ref, carry_scr, tri_scr):
    step = pl.program_id(0)
    tt = x_ref.shape[0]

    @pl.when(step == 0)
    def _():
        carry_scr[...] = jnp.zeros_like(carry_scr)
        s = lax.broadcasted_iota(jnp.int32, tri_scr.shape, 0)
        t = lax.broadcasted_iota(jnp.int32, tri_scr.shape, 1)
        tri_scr[...] = jnp.where(s < t, 1.0, 0.0).astype(BF16)

    x = x_ref[...]
    xh = x.astype(BF16)
    xl = (x - xh.astype(F32)).astype(BF16)
    wr = wr_ref[...]
    wh = wr.astype(BF16)
    wl = (wr - wh.astype(F32)).astype(BF16)
    nt = (((1,), (1,)), ((), ()))
    by_xh = lax.dot_general(jnp.concatenate([wh, wl], axis=0), xh, nt, preferred_element_type=F32)
    logits = (by_xh[:ROUTE_OUT] + lax.dot_general(wh, xl, nt, preferred_element_type=F32)
              + by_xh[ROUTE_OUT:]) + br_ref[...]
    neg = jnp.float32(-jnp.inf)

    lg = logits[N_EXPERTS:N_EXPERTS + SUBLANES]
    grow = lax.broadcasted_iota(jnp.int32, lg.shape, 0)
    lg = jnp.where(grow < N_GROUPS, lg, neg)
    gmax = jnp.max(lg, axis=0, keepdims=True)
    g_idx = jnp.min(jnp.where(lg == gmax, grow, SUBLANES), axis=0, keepdims=True)
    g_p = 1.0 / jnp.sum(jnp.exp(lg - gmax), axis=0, keepdims=True)

    le = logits[:N_EXPERTS]
    row = lax.broadcasted_iota(jnp.int32, le.shape, 0)
    le = jnp.where((row >> GROUP_SHIFT) == g_idx, le, neg)
    m1 = jnp.max(le, axis=0, keepdims=True)
    i1 = jnp.min(jnp.where(le == m1, row, N_EXPERTS), axis=0, keepdims=True)
    le2 = jnp.where(row == i1, neg, le)
    m2 = jnp.max(le2, axis=0, keepdims=True)
    i2 = jnp.min(jnp.where(le2 == m2, row, N_EXPERTS), axis=0, keepdims=True)
    e2 = jnp.exp(m2 - m1)
    gate0 = g_p / (1.0 + e2)
    gate1 = g_p * e2 / (1.0 + e2)

    hit1 = row == i1
    hit2 = row == i2
    onehot = jnp.where(hit1 | hit2, 1.0, 0.0)
    rc = RANK_CHUNK
    chunks = [onehot[:, c * rc:(c + 1) * rc] for c in range(tt // rc)]
    inside = _dot(jnp.concatenate(chunks, axis=0).astype(BF16), tri_scr[...])
    seen = carry_scr[...]
    parts = []
    for c, chunk in enumerate(chunks):
        parts.append(inside[c * N_EXPERTS:(c + 1) * N_EXPERTS] + seen)
        seen = seen + jnp.sum(chunk, axis=1, keepdims=True)
    before = jnp.concatenate(parts, axis=1)
    rank0 = jnp.sum(jnp.where(hit1, before, 0.0), axis=0, keepdims=True)
    rank1 = jnp.sum(jnp.where(hit2, before, 0.0), axis=0, keepdims=True)
    carry_scr[...] = seen
    cnt_ref[...] = jnp.broadcast_to(seen, cnt_ref.shape)

    fields = jnp.concatenate([i1.astype(F32), i2.astype(F32), rank0, rank1, gate0, gate1,
                              jnp.zeros((SUBLANES - 6, tt), F32)], axis=0)
    infot_ref[0] = fields
    info_ref[...] = jnp.concatenate([fields, jnp.zeros((LANES - SUBLANES, tt), F32)], axis=0).T


def _route(x1, wr, br):
    T, D = x1.shape
    tt = ROUTE_ROWS
    return pl.pallas_call(
        _route_kernel,
        grid=(T // tt,),
        in_specs=[pl.BlockSpec((tt, D), lambda i: (i, 0)),
                  pl.BlockSpec(wr.shape, lambda i: (0, 0)),
                  pl.BlockSpec(br.shape, lambda i: (0, 0))],
        out_specs=[pl.BlockSpec((tt, LANES), lambda i: (i, 0)),
                   pl.BlockSpec((1, SUBLANES, tt), lambda i: (i, 0, 0)),
                   pl.BlockSpec((N_EXPERTS, LANES), lambda i: (0, 0))],
        out_shape=[jax.ShapeDtypeStruct((T, LANES), F32), jax.ShapeDtypeStruct((T // tt, SUBLANES, tt), F32),
                   jax.ShapeDtypeStruct((N_EXPERTS, LANES), F32)],
        scratch_shapes=[pltpu.VMEM((N_EXPERTS, 1), F32), pltpu.VMEM((RANK_CHUNK, RANK_CHUNK), BF16)],
        compiler_params=pltpu.CompilerParams(dimension_semantics=("arbitrary",), vmem_limit_bytes=VMEM_LIMIT),
        name="route",
    )(x1, wr, br)


def _to_token_tiles(dst_ref, val):
    dst_ref[...] = val.astype(BF16).reshape(dst_ref.shape)


def _from_token_tiles(src_ref, rows):
    return src_ref[...].reshape(rows, TOKEN_ROWS * LANES)


def _to_token_tiles_f32(dst_ref, val):
    rows = val.shape[0]
    for c in range(TOKEN_ROWS):
        dst_ref[pl.ds(c, rows, stride=TOKEN_ROWS), :] = val[:, c * LANES:(c + 1) * LANES]


def _from_token_tiles_f32(src_ref, rows):
    return jnp.concatenate([src_ref[pl.ds(c, rows, stride=TOKEN_ROWS), :] for c in range(TOKEN_ROWS)], axis=1)


def _tile_copy(src_ref, src_row, dst_ref, dst_row, sem):
    return pltpu.make_async_copy(src_ref.at[pl.ds(pl.multiple_of(src_row, TOKEN_ROWS), TOKEN_ROWS)],
                                 dst_ref.at[pl.ds(pl.multiple_of(dst_row, TOKEN_ROWS), TOKEN_ROWS)], sem)


def _dispatch_kernel(seg_ref, dest_ref, x0_ref, xn_ref, buf_ref, stage_scr, zero_scr, sem, zero_sem, *, n_steps):
    i = pl.program_id(0)
    rows = xn_ref.shape[0]
    cur = i % 3
    nxt = (i + 1) % 3

    @pl.when(i == 0)
    def _():
        zero_scr[...] = jnp.zeros(zero_scr.shape, BF16)

        block = EXPERT_ROWS * TOKEN_ROWS
        n_blocks = buf_ref.shape[0] // block

        def clear_rows(first):
            return pltpu.make_async_copy(zero_scr, buf_ref.at[pl.ds(pl.multiple_of(first, SUBLANES), block)], zero_sem)

        def clear(e):
            return clear_rows((seg_ref[0, e] - EXPERT_ROWS) * TOKEN_ROWS)

        def start_tail(b, c):
            clear_rows(b * block).start()
            return c

        def wait_tail(b, c):
            clear_rows(b * block).wait()
            return c

        for e in range(N_EXPERTS):
            pl.when(seg_ref[1, e] > 0)(lambda e=e: clear(e).start())
        lax.fori_loop(seg_ref[2, 0], n_blocks, start_tail, 0)
        for e in range(N_EXPERTS):
            pl.when(seg_ref[1, e] > 0)(lambda e=e: clear(e).wait())
        lax.fori_loop(seg_ref[2, 0], n_blocks, wait_tail, 0)

        _to_token_tiles(stage_scr.at[0], x0_ref[...])

    def drain(s):
        for _ in range(TOP_K):
            pltpu.make_async_copy(stage_scr.at[s], stage_scr.at[s], sem.at[s]).wait()

    @pl.when(i >= 2)
    def _():
        drain(nxt)

    _to_token_tiles(stage_scr.at[nxt], xn_ref[...])
    for r in range(rows):
        for kk in range(TOP_K):
            _tile_copy(stage_scr.at[cur], r * TOKEN_ROWS, buf_ref, dest_ref[0, 0, kk * rows + r],
                       sem.at[cur]).start(priority=kk)

    @pl.when(i == n_steps - 1)
    def _():
        drain(cur)
        if n_steps >= 2:
            drain((i + 2) % 3)


def _dispatch(seg, dest3, x1, n_rows):
    T, D = x1.shape
    td = MOVE_ROWS
    n_steps = T // td
    grid_spec = pltpu.PrefetchScalarGridSpec(
        num_scalar_prefetch=1,
        grid=(n_steps,),
        in_specs=[pl.BlockSpec((1, 1, TOP_K * td), lambda i, seg: (i, 0, 0), memory_space=pltpu.SMEM),
                  pl.BlockSpec((td, D), lambda i, seg: (0, 0)),
                  pl.BlockSpec((td, D), lambda i, seg: (jnp.minimum(i + 1, n_steps - 1), 0))],
        out_specs=pl.BlockSpec(memory_space=pl.ANY),
        scratch_shapes=[pltpu.VMEM((3, td * TOKEN_ROWS, LANES), BF16),
                        pltpu.VMEM((EXPERT_ROWS * TOKEN_ROWS, LANES), BF16),
                        pltpu.SemaphoreType.DMA((3,)), pltpu.SemaphoreType.DMA(())],
    )
    return pl.pallas_call(
        functools.partial(_dispatch_kernel, n_steps=n_steps),
        grid_spec=grid_spec,
        out_shape=jax.ShapeDtypeStruct((n_rows * TOKEN_ROWS, LANES), BF16),
        compiler_params=pltpu.CompilerParams(dimension_semantics=("arbitrary",), vmem_limit_bytes=VMEM_LIMIT),
        name="dispatch",
    )(seg, dest3, x1, x1)


def _expert_kernel(be_ref, ne_ref, nu_ref, buf0_ref, bufa_ref, bufb_ref, wg_hbm, wu_hbm, wd_hbm, y_hbm,
                   sg_scr, su_scr, sd_scr, wg_scr, wu_scr, wd_scr, xa_scr, xb_scr, y_scr, cur_ref, sem, ysem):
    step = pl.program_id(0)
    last_step = pl.num_programs(0) - 1
    bm = EXPERT_ROWS
    half = bm * TOKEN_ROWS
    slot = step % 2
    y_ref = y_scr.at[slot]

    def write_back(s, at_step):
        first = pl.multiple_of(at_step * (2 * half), 2 * half)
        return pltpu.make_async_copy(y_scr.at[s], y_hbm.at[pl.ds(first, 2 * half)], ysem.at[s])

    @pl.when(step >= 2)
    def _():
        write_back(slot, step - 2).wait()

    def fetch(expert, s):
        return (pltpu.make_async_copy(wg_hbm.at[expert], sg_scr.at[s], sem.at[s, 0]),
                pltpu.make_async_copy(wu_hbm.at[expert], su_scr.at[s], sem.at[s, 1]),
                pltpu.make_async_copy(wd_hbm.at[expert], sd_scr.at[s], sem.at[s, 2]))

    @pl.when(step == 0)
    def _():
        cur_ref[0] = 0
        for c in fetch(be_ref[0], 0):
            c.start()
        xa_scr[...] = _from_token_tiles(buf0_ref, bm)

    def load_weights(blk):
        e = be_ref[blk]

        @pl.when((blk == 0) | (be_ref[jnp.maximum(blk - 1, 0)] != e))
        def _():
            s = cur_ref[0]
            for c in fetch(e, s):
                c.wait()
            wg_scr[...] = sg_scr[s].astype(BF16)
            wu_scr[...] = su_scr[s].astype(BF16)
            wd_scr[...] = sd_scr[s].astype(BF16)
            nxt = ne_ref[blk]

            @pl.when(nxt >= 0)
            def _():
                for c in fetch(nxt, 1 - s):
                    c.start()

            cur_ref[0] = 1 - s

    def compute(x_scr, nxt_ref, nxt_scr, out_rows):
        nxt_scr[...] = _from_token_tiles(nxt_ref, bm)
        xb = x_scr[...]
        hidden = jax.nn.silu(_dot(xb, wg_scr[...])) * _dot(xb, wu_scr[...])
        _to_token_tiles_f32(y_ref.at[out_rows], _dot(hidden.astype(BF16), wd_scr[...]))

    def run(blk, x_scr, nxt_ref, nxt_scr, out_rows):
        @pl.when(blk < nu_ref[0])
        def _():
            compute(x_scr, nxt_ref, nxt_scr, out_rows)

        @pl.when(blk >= nu_ref[0])
        def _():
            y_ref[out_rows, :] = jnp.zeros((half, LANES), F32)

    blk_a, blk_b = 2 * step, 2 * step + 1
    rows_a, rows_b = pl.ds(0, half), pl.ds(half, half)
    load_weights(blk_a)
    same = (be_ref[blk_a] == be_ref[blk_b]) & (blk_b < nu_ref[0])

    @pl.when(same)
    def _():
        compute(xa_scr, bufa_ref, xb_scr, rows_a)
        compute(xb_scr, bufb_ref, xa_scr, rows_b)

    @pl.when(jnp.logical_not(same))
    def _():
        run(blk_a, xa_scr, bufa_ref, xb_scr, rows_a)
        load_weights(blk_b)
        run(blk_b, xb_scr, bufb_ref, xa_scr, rows_b)

    write_back(slot, step).start(priority=1)

    @pl.when(step == last_step)
    def _():
        write_back(slot, step).wait()

        @pl.when(step >= 1)
        def _():
            write_back(1 - slot, step - 1).wait()


def _experts(block_expert, next_expert, n_used, buf, w_gate, w_up, w_down):
    bm = EXPERT_ROWS
    D, ff = w_gate.shape[1:]
    n_blocks = buf.shape[0] // (bm * TOKEN_ROWS)
    assert n_blocks % 2 == 0
    last = n_blocks - 1
    grid_spec = pltpu.PrefetchScalarGridSpec(
        num_scalar_prefetch=3,
        grid=(n_blocks // 2,),
        in_specs=[pl.BlockSpec((bm * TOKEN_ROWS, LANES), lambda s, *_: (0, 0)),
                  pl.BlockSpec((bm * TOKEN_ROWS, LANES), lambda s, *_: (2 * s + 1, 0)),
                  pl.BlockSpec((bm * TOKEN_ROWS, LANES), lambda s, *_: (jnp.minimum(2 * s + 2, last), 0)),
                  pl.BlockSpec(memory_space=pl.ANY),
                  pl.BlockSpec(memory_space=pl.ANY),
                  pl.BlockSpec(memory_space=pl.ANY)],
        out_specs=pl.BlockSpec(memory_space=pl.ANY),
        scratch_shapes=[pltpu.VMEM((2, D, ff), F32), pltpu.VMEM((2, D, ff), F32), pltpu.VMEM((2, ff, D), F32),
                        pltpu.VMEM((D, ff), BF16), pltpu.VMEM((D, ff), BF16), pltpu.VMEM((ff, D), BF16),
                        pltpu.VMEM((bm, D), BF16), pltpu.VMEM((bm, D), BF16),
                        pltpu.VMEM((2, 2 * bm * TOKEN_ROWS, LANES), F32),
                        pltpu.SMEM((1,), jnp.int32), pltpu.SemaphoreType.DMA((2, 3)), pltpu.SemaphoreType.DMA((2,))],
    )
    return pl.pallas_call(
        _expert_kernel,
        grid_spec=grid_spec,
        out_shape=jax.ShapeDtypeStruct(buf.shape, F32),
        compiler_params=pltpu.CompilerParams(dimension_semantics=("arbitrary",), vmem_limit_bytes=VMEM_LIMIT),
        name="experts",
    )(block_expert, next_expert, n_used, buf, buf, buf, w_gate, w_up, w_down)


def _final_kernel(d0_ref, d1_ref, d2_ref, x_ref, info_ref, y_ref, p_ref, wpg_ref, bpg_ref, wpp_ref,
                  l2g_ref, l2b_ref, l3g_ref, l3b_ref, o_ref, rows_scr, sem):
    i = pl.program_id(0)
    last = pl.num_programs(0) - 1
    rows = x_ref.shape[0]
    slot = i % 3
    ahead = (i + 2) % 3

    def row_copy(dref, s, r, kk):
        return _tile_copy(y_ref, dref[0, 0, kk * rows + r], rows_scr.at[s, kk], r * TOKEN_ROWS, sem.at[s])

    def landed(s):
        pltpu.make_async_copy(rows_scr.at[s], rows_scr.at[s], sem.at[s]).wait()

    @pl.when(i == 0)
    def _():
        def start(c, carry):
            for u in range(MOVE_UNROLL):
                for kk in range(TOP_K):
                    row_copy(d0_ref, 0, c * MOVE_UNROLL + u, kk).start(priority=kk)
                    row_copy(d1_ref, 1, c * MOVE_UNROLL + u, kk).start(priority=kk)
            return carry

        lax.fori_loop(0, rows // MOVE_UNROLL, start, 0)

    landed(slot)
    info = info_ref[...]
    gate0 = info[:, I_G0:I_G0 + 1]
    gate1 = info[:, I_G1:I_G1 + 1]
    moe = (_from_token_tiles_f32(rows_scr.at[slot, 0], rows) * gate0
           + _from_token_tiles_f32(rows_scr.at[slot, 1], rows) * gate1)

    for r in range(rows):
        for kk in range(TOP_K):
            row_copy(d2_ref, ahead, r, kk).start(priority=kk)

    pp = _dot(p_ref[...].astype(BF16), wpp_ref[...])
    x2 = _ln(ALPHA * x_ref[...] + moe, l2g_ref[...], l2b_ref[...])
    gate = jax.nn.sigmoid(_dot(x2.astype(BF16), wpg_ref[...]) + bpg_ref[...])
    o_ref[...] = _ln(ALPHA * x2 + gate * pp, l3g_ref[...], l3b_ref[...])

    @pl.when(i == last)
    def _():
        landed((i + 1) % 3)
        landed(ahead)


def _final(dest3, x1, info, y, p2, w):
    T, D = x1.shape
    tc = MOVE_ROWS
    pd = p2.shape[1]
    full = lambda a: pl.BlockSpec(a.shape, lambda i: (0,) * a.ndim)
    consts = [w["wpg"], w["bpg"], w["wpp"], w["l2g"], w["l2b"], w["l3g"], w["l3b"]]
    last = T // tc - 1
    assert last >= 2
    return pl.pallas_call(
        _final_kernel,
        grid=(T // tc,),
        in_specs=[pl.BlockSpec((1, 1, TOP_K * tc), lambda i: (i, 0, 0), memory_space=pltpu.SMEM),
                  pl.BlockSpec((1, 1, TOP_K * tc), lambda i: (jnp.minimum(i + 1, last), 0, 0), memory_space=pltpu.SMEM),
                  pl.BlockSpec((1, 1, TOP_K * tc), lambda i: (jnp.minimum(i + 2, last), 0, 0), memory_space=pltpu.SMEM),
                  pl.BlockSpec((tc, D), lambda i: (i, 0)),
                  pl.BlockSpec((tc, LANES), lambda i: (i, 0)),
                  pl.BlockSpec(memory_space=pl.ANY),
                  pl.BlockSpec((tc, pd), lambda i: (i, 0))] + [full(a) for a in consts],
        out_specs=pl.BlockSpec((tc, D), lambda i: (i, 0)),
        out_shape=jax.ShapeDtypeStruct((T, D), F32),
        scratch_shapes=[pltpu.VMEM((3, TOP_K, tc * TOKEN_ROWS, LANES), F32), pltpu.SemaphoreType.DMA((3,))],
        compiler_params=pltpu.CompilerParams(dimension_semantics=("arbitrary",), vmem_limit_bytes=VMEM_LIMIT),
        name="final",
    )(dest3, dest3, dest3, x1, info, y, p2, *consts)


def _pad_heads(a, width):
    lead = a.shape[:-1]
    a = a.reshape(lead + (MLA_HEADS, width))
    a = jnp.pad(a, [(0, 0)] * len(lead) + [(0, 0), (0, LANES - width)])
    return a.reshape(lead + (HP,))


def _layer_weights(w_in, q_norm_g, w_q_up, kv_norm_g, w_kv_up, gm_ln_g, gm_ln_b, gm_w_s, gm_b_s,
                   mla_out_g, gm_out_g, w_o, ln1_g, ln1_b):
    D = w_in.shape[0]
    half = QK_ROPE // 2
    c1, c2, c3 = Q_RANK, Q_RANK + KV_RANK, Q_RANK + KV_RANK + QK_ROPE
    zeros = lambda *s: jnp.zeros(s, F32)
    kr = jnp.concatenate([zeros(D, QK_NOPE), w_in[:, c2:c3], zeros(D, LANES - QK_NOPE - QK_ROPE)], axis=1)
    win = jnp.concatenate([w_in[:, :c2], kr, w_in[:, c3:]], axis=1).astype(BF16)
    wq = _pad_heads(w_q_up, QK_NOPE + QK_ROPE).astype(BF16)

    wkv3 = w_kv_up.reshape(KV_RANK, MLA_HEADS, QK_NOPE + V_HEAD)
    wk = _pad_heads(wkv3[..., :QK_NOPE].reshape(KV_RANK, -1), QK_NOPE).astype(BF16)
    wv = wkv3[..., QK_NOPE:].reshape(KV_RANK, -1).astype(BF16)

    inv = (ROPE_THETA ** (-jnp.arange(0, QK_ROPE, 2, dtype=F32) / QK_ROPE))[:, None]
    eye = jnp.eye(half, dtype=F32)
    first = jnp.pad(eye, ((0, 0), (QK_NOPE, LANES - QK_NOPE - half)))
    second = jnp.pad(eye, ((0, 0), (QK_NOPE + half, LANES - QK_NOPE - QK_ROPE)))
    zero = jnp.zeros_like(first)
    cos_rows = jnp.concatenate([first + second, zero, zero], axis=1)
    sin_rows = jnp.concatenate([zero, -first, second], axis=1)
    rope = jnp.concatenate([cos_rows, cos_rows, sin_rows, sin_rows], axis=0).astype(BF16)
    lane = jnp.arange(LANES)
    one = jnp.where((lane >= QK_NOPE) & (lane < QK_NOPE + QK_ROPE), 0.0, 1.0)[None, :]

    grp = jnp.arange(GM_OUT) // GM_CH
    gavg = jnp.where(grp[:, None] == grp[None, :], 1.0 / GM_CH, 0.0).astype(BF16)
    bias = jnp.repeat(gm_b_s.T, GM_CH, axis=1)

    woa = w_o[:MLA_OUT].astype(BF16)
    wog = w_o[MLA_OUT:].astype(BF16)
    return dict(win=win, qg=q_norm_g[None, :], wq=wq, kvg=kv_norm_g[None, :], wk=wk, wv=wv, inv=inv, rope=rope, one=one,
                lng=gm_ln_g[None, :], lnb=gm_ln_b[None, :], gavg=gavg, ws=gm_w_s, bias=bias, gog=gm_out_g[None, :],
                woa=woa, wog=wog, mog=mla_out_g[:, None], l1g=ln1_g[None, :], l1b=ln1_b[None, :])


def _moe(x1, w_rg, b_rg, w_re, b_re, w_gate, w_up, w_down):
    T, D = x1.shape
    pad = ROUTE_OUT - N_EXPERTS - N_GROUPS
    wr = jnp.concatenate([w_re.T, w_rg.T, jnp.zeros((pad, D), F32)], axis=0)
    br = jnp.concatenate([b_re, b_rg, jnp.zeros((pad,), F32)])[:, None]
    info, info_t, cnt = _route(x1, wr, br)

    bm = EXPERT_ROWS
    n_blocks = (T * TOP_K) // bm + N_EXPERTS
    counts = cnt[:, 0].astype(jnp.int32)
    padded = (counts + bm - 1) // bm * bm
    pad_ends = jnp.cumsum(padded)
    pad_starts = pad_ends - padded
    def dest_rows(e_lane, r_lane):
        e = info_t[:, e_lane, :].astype(jnp.int32)
        ids = jnp.arange(N_EXPERTS)[:, None, None]
        seg_start = jnp.sum(jnp.where(e[None] == ids, pad_starts[:, None, None], 0), axis=0)
        return ((seg_start + info_t[:, r_lane, :].astype(jnp.int32)) * TOKEN_ROWS).reshape(T // MOVE_ROWS, MOVE_ROWS)

    dest = jnp.concatenate([dest_rows(I_E0, I_R0), dest_rows(I_E1, I_R1)], axis=1)[:, None, :]
    block_start = jnp.arange(n_blocks, dtype=jnp.int32) * bm
    block_expert = jnp.minimum(jnp.sum(pad_ends[None, :] <= block_start[:, None], axis=1),
                               N_EXPERTS - 1).astype(jnp.int32)

    blk = jnp.arange(n_blocks)
    later = (blk[None, :] > blk[:, None]) & (block_expert[None, :] != block_expert[:, None])
    next_expert = jnp.min(jnp.where(later, block_expert[None, :], N_EXPERTS), axis=1)
    next_expert = jnp.where(next_expert == N_EXPERTS, -1, next_expert).astype(jnp.int32)
    n_used = (pad_ends[-1:] // bm).astype(jnp.int32)

    seg = jnp.stack([pad_ends, padded, jnp.broadcast_to(n_used, (N_EXPERTS,))]).astype(jnp.int32)
    buf = _dispatch(seg, dest, x1, n_blocks * bm)
    y = _experts(block_expert, next_expert, n_used, buf, w_gate, w_up, w_down)
    return info, dest, y


def kernel(x, p, positions, w_in, q_norm_g, w_q_up, kv_norm_g, w_kv_up, gm_ln_g, gm_ln_b, gm_w_s, gm_b_s, mla_out_g, gm_out_g, w_o, ln1_g, ln1_b, w_rg, b_rg, w_re, b_re, w_gate, w_up, w_down, ln2_g, ln2_b, w_pg, b_pg, w_pp, ln3_g, ln3_b):
    B, S, D = x.shape
    T = B * S
    assert S % PREP_ROWS == 0 and PREP_ROWS % ATTN_ROWS == 0 and PREP_ROWS % CHUNK == 0
    assert S % (ATTN_TILES * ATTN_ROWS) == 0 and ATTN_TILES % 2 == 0
    assert T % ROUTE_ROWS == 0 and ROUTE_ROWS % RANK_CHUNK == 0
    assert T % MOVE_ROWS == 0 and (T * TOP_K) % EXPERT_ROWS == 0
    assert D == TOKEN_ROWS * LANES and MOVE_ROWS % MOVE_UNROLL == 0
    pos4 = positions.reshape(B, S // PREP_ROWS, 1, PREP_ROWS)
    for i in range(DEPTH):
        w = _layer_weights(w_in[i], q_norm_g[i], w_q_up[i], kv_norm_g[i], w_kv_up[i], gm_ln_g[i], gm_ln_b[i],
                           gm_w_s[i], gm_b_s[i], mla_out_g[i], gm_out_g[i], w_o[i], ln1_g[i], ln1_b[i])
        q, k, vt, g = _prep(x, pos4, w)
        x1 = _attn(q, k, vt, g, x, w).reshape(T, D)
        info, dest, y = _moe(x1, w_rg[i], b_rg[i], w_re[i], b_re[i], w_gate[i], w_up[i], w_down[i])
        wf = dict(wpg=w_pg[i].astype(BF16), bpg=b_pg[i][None, :], wpp=w_pp[i].astype(BF16),
                  l2g=ln2_g[i][None, :], l2b=ln2_b[i][None, :], l3g=ln3_g[i][None, :], l3b=ln3_b[i][None, :])
        x = _final(dest, x1, info, y, p[i].reshape(T, -1), wf).reshape(B, S, D)
    return x
```
